```python
import jax, jax.numpy as jnp
from jax import lax
import numpy as np

D_MODEL = 1024
BATCH = 4
SEQ = 4096
DEPTH = 2

CHUNK = 64
QBLOCK = 128
HEAD_DIM = 64
EPS = 1e-6
H_FOX = 8
H_CHK = 8
N_LEFT_CHUNKS = 8
BAND = (N_LEFT_CHUNKS + 1) * CHUNK
REL_CLIP = 256
N_REL = CHUNK - 1 + REL_CLIP + 1
FORGET_BIAS = 3.0
H_SB = 8
H_MLA = 8
Q_LORA = 384
KV_LORA = 256
NOPE_DIM = 64
ROPE_DIM = 32
V_DIM = 64
ROPE_THETA = 10000.0
D_FF = 4 * D_MODEL
N_EVEN = (DEPTH + 1) // 2
N_ODD = DEPTH // 2

W_FOX = H_FOX * HEAD_DIM
W_CHK = H_CHK * HEAD_DIM
W_SB = H_SB * HEAD_DIM
W_MLA = H_MLA * V_DIM
SPLIT_AB = [W_FOX, W_FOX, W_FOX, H_FOX, W_CHK, W_CHK, W_CHK]
SPLIT_CD = [W_SB, W_SB, W_SB, Q_LORA, KV_LORA, ROPE_DIM]
IN_AB = sum(SPLIT_AB)
IN_CD = sum(SPLIT_CD)
MIX_AB = W_FOX + W_CHK
MIX_CD = W_SB + W_MLA

kernel_name = "chunk_causal_hybrid_fox_chunkrel_stickbreak_mla"


def _split(h, sizes):
    return jnp.split(h, np.cumsum(sizes)[:-1].tolist(), axis=-1)


def rmsnorm(x, g):
    xf = x.astype(jnp.float32)
    y = xf * lax.rsqrt(jnp.mean(xf * xf, axis=-1, keepdims=True) + EPS)
    return (y * g.astype(jnp.float32)).astype(x.dtype)


def rope(x, positions):
    half = ROPE_DIM // 2
    inv_freq = ROPE_THETA ** (-jnp.arange(half, dtype=jnp.float32) / half)
    ang = positions.astype(jnp.float32)[..., None] * inv_freq
    cos = jnp.cos(ang)[:, :, None, :]
    sin = jnp.sin(ang)[:, :, None, :]
    x1 = x[..., :half].astype(jnp.float32)
    x2 = x[..., half:].astype(jnp.float32)
    out = jnp.concatenate([x1 * cos - x2 * sin, x2 * cos + x1 * sin], axis=-1)
    return out.astype(x.dtype)


def _sweep_query_blocks(block_fn, seq):
    out = lax.map(block_fn, jnp.arange(seq // QBLOCK))
    nb, b, qb, h, dv = out.shape
    return jnp.moveaxis(out, 0, 1).reshape(b, nb * qb, h, dv)


def fox_attention(q, k, v, log_f):
    seq = q.shape[1]
    scale = HEAD_DIM ** -0.5
    cum = jnp.transpose(jnp.cumsum(log_f, axis=1), (0, 2, 1))
    k_pos = jnp.arange(seq)

    def block(i):
        start = i * QBLOCK
        qb = lax.dynamic_slice_in_dim(q, start, QBLOCK, axis=1)
        cq = lax.dynamic_slice_in_dim(cum, start, QBLOCK, axis=2)
        q_pos = start + jnp.arange(QBLOCK)
        s = jnp.einsum('bqhd,bkhd->bhqk', qb, k).astype(jnp.float32) * scale
        s = s + cq[..., :, None] - cum[..., None, :]
        mask = k_pos[None, :] <= q_pos[:, None]
        p = jax.nn.softmax(jnp.where(mask, s, -jnp.inf), axis=-1)
        return jnp.einsum('bhqk,bkhd->bqhd', p.astype(v.dtype), v)

    return _sweep_query_blocks(block, seq)


def chunked_relpos_attention(q, k, v, rel_bias):
    b, seq, h, d = q.shape
    nc = seq // CHUNK
    left = N_LEFT_CHUNKS * CHUNK
    scale = HEAD_DIM ** -0.5

    def band(t):
        tp = jnp.pad(t, ((0, 0), (left, 0), (0, 0), (0, 0)))
        tp = tp.reshape(b, nc + N_LEFT_CHUNKS, CHUNK, h, d)
        return jnp.concatenate([tp[:, i:i + nc] for i in range(N_LEFT_CHUNKS + 1)], axis=2)

    kb, vb = band(k), band(v)
    qc = q.reshape(b, nc, CHUNK, h, d)
    rel = np.arange(CHUNK)[:, None] + left - np.arange(BAND)[None, :]
    rel_idx = np.clip(rel, -(CHUNK - 1), REL_CLIP) + (CHUNK - 1)
    bias = rel_bias[:, rel_idx].astype(jnp.float32)
    s = jnp.einsum('bcqhd,bckhd->bhcqk', qc, kb).astype(jnp.float32) * scale
    s = s + bias[None, :, None]
    key_abs = jnp.arange(nc)[:, None] * CHUNK - left + jnp.arange(BAND)[None, :]
    valid = (key_abs >= 0)[None, None, :, None, :]
    p = jax.nn.softmax(jnp.where(valid, s, -jnp.inf), axis=-1)
    out = jnp.einsum('bhcqk,bckhd->bcqhd', p.astype(v.dtype), vb)
    return out.reshape(b, seq, h, d)


def stick_breaking_attention(q, k, v):
    seq = q.shape[1]
    scale = HEAD_DIM ** -0.5
    k_pos = jnp.arange(seq)

    def block(i):
        start = i * QBLOCK
        qb = lax.dynamic_slice_in_dim(q, start, QBLOCK, axis=1)
        q_pos = start + jnp.arange(QBLOCK)
        z = jnp.einsum('bqhd,bkhd->bhqk', qb, k).astype(jnp.float32) * scale
        mask = k_pos[None, :] < q_pos[:, None]
        log_beta = jax.nn.log_sigmoid(z)
        log_keep = jnp.where(mask, jax.nn.log_sigmoid(-z), 0.0)
        suffix = lax.cumsum(log_keep, axis=3, reverse=True) - log_keep
        a = jnp.where(mask, jnp.exp(log_beta + suffix), 0.0)
        return jnp.einsum('bhqk,bkhd->bqhd', a.astype(v.dtype), v)

    return _sweep_query_blocks(block, seq)


def mla_attention(q_nope, q_rope, k_nope, k_rope, v):
    seq = q_nope.shape[1]
    scale = (NOPE_DIM + ROPE_DIM) ** -0.5
    k_chunk = jnp.arange(seq) // CHUNK

    def block(i):
        start = i * QBLOCK
        qn = lax.dynamic_slice_in_dim(q_nope, start, QBLOCK, axis=1)
        qr = lax.dynamic_slice_in_dim(q_rope, start, QBLOCK, axis=1)
        q_chunk = (start + jnp.arange(QBLOCK)) // CHUNK
        s = (jnp.einsum('bqhd,bkhd->bhqk', qn, k_nope)
             + jnp.einsum('bqhr,bkr->bhqk', qr, k_rope)).astype(jnp.float32) * scale
        mask = k_chunk[None, :] <= q_chunk[:, None]
        p = jax.nn.softmax(jnp.where(mask, s, -jnp.inf), axis=-1)
        return jnp.einsum('bhqk,bkhd->bqhd', p.astype(v.dtype), v)

    return _sweep_query_blocks(block, seq)


def even_mixer(h, w_in, b_forget, rel_bias, w_out):
    b, s, _ = h.shape
    qa, ka, va, fa, qb, kb, vb = _split(h @ w_in, SPLIT_AB)
    heads = lambda t, n: t.reshape(b, s, n, HEAD_DIM)
    log_f = jax.nn.log_sigmoid((fa + b_forget).astype(jnp.float32))
    o_a = fox_attention(heads(qa, H_FOX), heads(ka, H_FOX), heads(va, H_FOX), log_f)
    o_b = chunked_relpos_attention(heads(qb, H_CHK), heads(kb, H_CHK), heads(vb, H_CHK), rel_bias)
    o = jnp.concatenate([o_a.reshape(b, s, W_FOX), o_b.reshape(b, s, W_CHK)], axis=-1)
    return o @ w_out


def odd_mixer(h, positions, w_in, q_norm, kv_norm, w_uq, w_ukv, w_out):
    b, s, _ = h.shape
    qc, kc, vc, c_q, c_kv, k_r = _split(h @ w_in, SPLIT_CD)
    heads = lambda t, n, d: t.reshape(b, s, n, d)
    o_c = stick_breaking_attention(heads(qc, H_SB, HEAD_DIM), heads(kc, H_SB, HEAD_DIM), heads(vc, H_SB, HEAD_DIM))
    q_full = heads(rmsnorm(c_q, q_norm) @ w_uq, H_MLA, NOPE_DIM + ROPE_DIM)
    q_nope, q_rope = q_full[..., :NOPE_DIM], rope(q_full[..., NOPE_DIM:], positions)
    kv_full = heads(rmsnorm(c_kv, kv_norm) @ w_ukv, H_MLA, NOPE_DIM + V_DIM)
    k_nope, v_d = kv_full[..., :NOPE_DIM], kv_full[..., NOPE_DIM:]
    k_rope = rope(k_r[:, :, None, :], positions)[:, :, 0, :]
    o_d = mla_attention(q_nope, q_rope, k_nope, k_rope, v_d)
    o = jnp.concatenate([o_c.reshape(b, s, W_SB), o_d.reshape(b, s, W_MLA)], axis=-1)
    return o @ w_out


def squared_relu_mlp(h, w_up, w_down):
    return jnp.square(jax.nn.relu(h @ w_up)) @ w_down


def setup_inputs(seed: int = 0) -> dict:
    key = jax.random.key(seed)
    ks = jax.random.split(key, 20)
    nrm = lambda k, shape, fan_in: jax.random.normal(k, shape, jnp.float32) * fan_in ** -0.5
    gain = lambda k, shape: 1.0 + 0.05 * jax.random.normal(k, shape, jnp.float32)
    x = jax.random.normal(ks[0], (BATCH, SEQ, D_MODEL), jnp.float32)
    offset = jax.random.randint(ks[1], (BATCH, 1), 0, 100000, dtype=jnp.int32)
    positions = offset + jnp.arange(SEQ, dtype=jnp.int32)[None, :]
    return {
        "x": x,
        "positions": positions,
        "norm_mix": gain(ks[2], (DEPTH, D_MODEL)),
        "norm_mlp": gain(ks[3], (DEPTH, D_MODEL)),
        "norm_final": gain(ks[4], (D_MODEL,)),
        "w_in_ab": nrm(ks[5], (N_EVEN, D_MODEL, IN_AB), D_MODEL),
        "b_forget": FORGET_BIAS + 0.5 * jax.random.normal(ks[6], (N_EVEN, H_FOX), jnp.float32),
        "rel_bias": 0.2 * jax.random.normal(ks[7], (N_EVEN, H_CHK, N_REL), jnp.float32),
        "w_out_ab": nrm(ks[8], (N_EVEN, MIX_AB, D_MODEL), MIX_AB),
        "w_in_cd": nrm(ks[9], (N_ODD, D_MODEL, IN_CD), D_MODEL),
        "q_norm": gain(ks[10], (N_ODD, Q_LORA)),
        "kv_norm": gain(ks[11], (N_ODD, KV_LORA)),
        "w_uq": nrm(ks[12], (N_ODD, Q_LORA, H_MLA * (NOPE_DIM + ROPE_DIM)), Q_LORA),
        "w_ukv": nrm(ks[13], (N_ODD, KV_LORA, H_MLA * (NOPE_DIM + V_DIM)), KV_LORA),
        "w_out_cd": nrm(ks[14], (N_ODD, MIX_CD, D_MODEL), MIX_CD),
        "w_up": nrm(ks[15], (DEPTH, D_MODEL, D_FF), D_MODEL),
        "w_down": nrm(ks[16], (DEPTH, D_FF, D_MODEL), D_FF),
    }


def reference(x, positions, norm_mix, norm_mlp, norm_final, w_in_ab, b_forget, rel_bias, w_out_ab,
              w_in_cd, q_norm, kv_norm, w_uq, w_ukv, w_out_cd, w_up, w_down):
    for layer in range(DEPTH):
        h = rmsnorm(x, norm_mix[layer])
        if layer % 2 == 0:
            e = layer // 2
            x = x + even_mixer(h, w_in_ab[e], b_forget[e], rel_bias[e], w_out_ab[e])
        else:
            o = layer // 2
            x = x + odd_mixer(h, positions, w_in_cd[o], q_norm[o], kv_norm[o], w_uq[o], w_ukv[o], w_out_cd[o])
        x = x + squared_relu_mlp(rmsnorm(x, norm_mlp[layer]), w_up[layer], w_down[layer])
    return rmsnorm(x, norm_final)
```

```python
import functools
import math

import numpy as np
import jax
import jax.numpy as jnp
from jax import lax
from jax.experimental import pallas as pl
from jax.experimental.pallas import tpu as pltpu

F32 = jnp.float32
BF16 = jnp.bfloat16

EPS = 1e-6
HEAD_DIM = 64
CHUNK = 64
N_LEFT_CHUNKS = 8
REL_CLIP = 256
ROPE_DIM = 32
NOPE_DIM = 64
ROPE_THETA = 10000.0
Q_LORA = 384
KV_LORA = 256

LANES = 128
VT_BLK = LANES
NEG = -1e30
SB_ZERO_LOG = -104.0
VMEM_LIMIT = 56 * 1024 * 1024

_NT = (((1,), (1,)), ((), ()))


def _cparams(sem):
    return pltpu.CompilerParams(dimension_semantics=sem, vmem_limit_bytes=VMEM_LIMIT)


def _rms_bf16(x, g):
    ms = jnp.mean(x * x, axis=-1, keepdims=True)
    return (x * lax.rsqrt(ms + EPS) * g).astype(BF16)


def _store_vt(ovt_ref, vt, row0):
    rows, tm = vt.shape
    for c in range(tm // VT_BLK):
        ovt_ref[c, row0:row0 + rows, :] = vt[:, c * VT_BLK:(c + 1) * VT_BLK]


def _inproj_kernel(x_ref, g_ref, wm_ref, wvt_ref, wa_ref, om_ref, ovt_ref, oa_ref):
    h = _rms_bf16(x_ref[...], g_ref[...])
    nm = om_ref.shape[-1]
    for c in range(0, nm, 512):
        om_ref[:, c:c + 512] = jnp.dot(
            h, wm_ref[:, c:c + 512], preferred_element_type=F32).astype(BF16)
    nv = wvt_ref.shape[0]
    for r in range(0, nv, 256):
        vt = lax.dot_general(wvt_ref[r:r + 256, :], h, _NT,
                             preferred_element_type=F32).astype(BF16)
        _store_vt(ovt_ref, vt, r)
    oa_ref[...] = jnp.dot(h, wa_ref[...], preferred_element_type=F32)


def _inproj(x, g, wm, wvt, wa, tm=512):
    B, S, D = x.shape
    nm, nv, na = wm.shape[1], wvt.shape[0], wa.shape[1]
    return pl.pallas_call(
        _inproj_kernel,
        grid=(B, S // tm),
        in_specs=[
            pl.BlockSpec((None, tm, D), lambda b, i: (b, i, 0)),
            pl.BlockSpec((1, D), lambda b, i: (0, 0)),
            pl.BlockSpec((D, nm), lambda b, i: (0, 0)),
            pl.BlockSpec((nv, D), lambda b, i: (0, 0)),
            pl.BlockSpec((D, na), lambda b, i: (0, 0)),
        ],
        out_specs=[
            pl.BlockSpec((None, tm, nm), lambda b, i: (b, i, 0)),
            pl.BlockSpec((None, tm // VT_BLK, nv, VT_BLK), lambda b, i: (b, i, 0, 0)),
            pl.BlockSpec((None, tm, na), lambda b, i: (b, i, 0)),
        ],
        out_shape=[
            jax.ShapeDtypeStruct((B, S, nm), BF16),
            jax.ShapeDtypeStruct((B, S // VT_BLK, nv, VT_BLK), BF16),
            jax.ShapeDtypeStruct((B, S, na), F32),
        ],
        compiler_params=_cparams(("arbitrary", "arbitrary")),
        name="inproj",
    )(x, g.reshape(1, D), wm, wvt, wa)


def _split3(x):
    hi = x.astype(BF16)
    r = x - hi.astype(F32)
    mid = r.astype(BF16)
    lo = (r - mid.astype(F32)).astype(BF16)
    return hi, mid, lo


def _logcum_kernel(fa_ref, b_ref, o_ref, carry_ref):
    @pl.when(pl.program_id(1) == 0)
    def _():
        carry_ref[...] = jnp.zeros_like(carry_ref)

    z = fa_ref[...] + b_ref[...]
    lf = jnp.minimum(z, 0.0) - jnp.log(1.0 + jnp.exp(-jnp.abs(z)))
    tc = lf.shape[0]
    r = lax.broadcasted_iota(jnp.int32, (tc, tc), 0)
    c = lax.broadcasted_iota(jnp.int32, (tc, tc), 1)
    tri = jnp.where(r >= c, 1.0, 0.0).astype(BF16)
    cs = carry_ref[...]
    for part in _split3(lf):
        cs = cs + jnp.dot(tri, part, preferred_element_type=F32)
    o_ref[...] = cs
    carry_ref[...] = cs[tc - 1:tc, :]


def _logcum(fa, bias, tc=512):
    B, S, W = fa.shape
    return pl.pallas_call(
        _logcum_kernel,
        grid=(B, S // tc),
        in_specs=[pl.BlockSpec((None, tc, W), lambda b, i: (b, i, 0)),
                  pl.BlockSpec((1, W), lambda b, i: (0, 0))],
        out_specs=pl.BlockSpec((None, tc, W), lambda b, i: (b, i, 0)),
        out_shape=jax.ShapeDtypeStruct((B, S, W), F32),
        scratch_shapes=[pltpu.VMEM((1, W), F32)],
        compiler_params=_cparams(("arbitrary", "arbitrary")),
        name="logcum",
    )(fa, bias)


def _pair_mask_q(q2, j):
    lane = lax.broadcasted_iota(jnp.int32, q2.shape, 1)
    keep = (lane >= HEAD_DIM * j) & (lane < HEAD_DIM * (j + 1))
    return jnp.where(keep, q2, jnp.zeros_like(q2))


def _load_vt(vt_ref, kb, j, bk):
    n = bk // VT_BLK
    rows = slice(HEAD_DIM * j, HEAD_DIM * (j + 1))
    parts = [vt_ref[kb * n + c, rows, :] for c in range(n)]
    return parts[0] if n == 1 else jnp.concatenate(parts, axis=1)


def _softmax_step(sT, vt, carry):
    m, l, acc = carry
    m_new = jnp.maximum(m, jnp.max(sT, axis=0, keepdims=True))
    alpha = jnp.exp(m - m_new)
    p = jnp.exp(sT - m_new)
    l = alpha * l + jnp.sum(p, axis=0, keepdims=True)
    acc = alpha * acc + jnp.dot(vt, p.astype(BF16), preferred_element_type=F32)
    return m_new, l, acc


def _softmax_init(bq):
    return (jnp.full((1, bq), NEG, F32), jnp.zeros((1, bq), F32),
            jnp.zeros((HEAD_DIM, bq), F32))


def _store_pair(o_ref, outs):
    oT = jnp.concatenate(outs, axis=0)
    o_ref[...] = oT.T.astype(o_ref.dtype)


def _flash_kernel(*refs, mode, bq, bk, scale):
    if mode == "fox":
        q_ref, k_ref, vt_ref, crow_ref, ccol_ref, o_ref = refs
    else:
        q_ref, k_ref, vt_ref, o_ref = refs
    qs = pl.program_id(2) * bq
    row = lax.broadcasted_iota(jnp.int32, (bk, bq), 0)
    col = lax.broadcasted_iota(jnp.int32, (bk, bq), 1)
    n_full = qs // bk
    outs = []
    for j in range(2):
        if mode == "fox":
            qm = _pair_mask_q(q_ref[...], j)
            kcols = slice(None)
        else:
            kcols = slice(LANES * j, LANES * (j + 1))
            qm = q_ref[:, kcols]

        def step(kb, carry, masked, qm=qm, kcols=kcols, j=j):
            ks = pl.multiple_of(kb * bk, bk)
            k = k_ref[pl.ds(ks, bk), kcols]
            sT = lax.dot_general(k, qm, _NT, preferred_element_type=F32)
            if mode == "fox":
                sT = sT + (crow_ref[j:j + 1, :] - ccol_ref[pl.ds(ks, bk), j:j + 1])
                if masked:
                    sT = jnp.where(ks + row <= qs + col, sT, NEG)
            else:
                sT = sT * scale
                if masked:
                    sT = jnp.where(((ks + row) >> 6) <= ((qs + col) >> 6), sT, NEG)
            return _softmax_step(sT, _load_vt(vt_ref, kb, j, bk), carry)

        carry = lax.fori_loop(0, n_full, lambda kb, c: step(kb, c, False),
                              _softmax_init(bq))
        for d in range(bq // bk):
            carry = step(n_full + d, carry, True)
        m, l, acc = carry
        outs.append(acc / l)
    _store_pair(o_ref, outs)


def _flash(mode, q_arr, q_blk0, k_arr, k_blk0, vt_arr, vt_blk0, extra, n_pairs,
           scale, bq=256, bk=128):
    B, S, _ = q_arr.shape
    qw = LANES if mode == "fox" else 2 * LANES
    in_specs = [
        pl.BlockSpec((None, bq, qw), lambda b, p, i: (b, i, q_blk0 + p)),
        pl.BlockSpec((None, S, qw), lambda b, p, i: (b, 0, k_blk0 + p)),
        pl.BlockSpec((None, S // VT_BLK, LANES, VT_BLK),
                     lambda b, p, i: (b, 0, vt_blk0 + p, 0)),
    ]
    args = [q_arr, k_arr, vt_arr]
    if mode == "fox":
        crow, ccol = extra
        in_specs += [
            pl.BlockSpec((None, None, 2, bq), lambda b, p, i: (b, p, 0, i)),
            pl.BlockSpec((None, None, S, 2), lambda b, p, i: (b, p, 0, 0)),
        ]
        args += [crow, ccol]
    return pl.pallas_call(
        functools.partial(_flash_kernel, mode=mode, bq=bq, bk=bk, scale=scale),
        grid=(B, n_pairs, S // bq),
        in_specs=in_specs,
        out_specs=pl.BlockSpec((None, bq, LANES), lambda b, p, i: (b, i, p)),
        out_shape=jax.ShapeDtypeStruct((B, S, n_pairs * LANES), BF16),
        compiler_params=_cparams(("arbitrary", "arbitrary", "arbitrary")),
        name="flash_" + mode,
    )(*args)


CK_B = 2 * CHUNK
CK_NW = N_LEFT_CHUNKS * CHUNK // CK_B + 1
CK_EXT = (CK_NW + 1) * CK_B


def _chunk_kernel(q_ref, k_ref, vt_ref, ext_ref, o_ref, tab_ref):
    i = pl.program_id(2)

    @pl.when(i == 0)
    def _():
        jj = lax.broadcasted_iota(jnp.int32, (CK_B, CK_B), 0)
        rr = lax.broadcasted_iota(jnp.int32, (CK_B, CK_B), 1)
        for j in range(2):
            for w in range(CK_NW):
                a = (CK_NW - 1 - w) * CK_B
                g = jnp.broadcast_to(ext_ref[j:j + 1, a:a + 2 * CK_B], (CK_B, 2 * CK_B))
                t = pltpu.roll(g, CK_B, 1, stride=1, stride_axis=0)[:, :CK_B]
                if w == 0:
                    t = jnp.where((rr >= CHUNK) & (jj < CHUNK), NEG, t)
                if w == CK_NW - 1:
                    t = jnp.where((rr < CHUNK) & (jj >= CHUNK), NEG, t)
                tab_ref[j, w] = t

    outs = []
    q2 = q_ref[...]
    for j in range(2):
        qm = _pair_mask_q(q2, j)
        carry = _softmax_init(CK_B)
        for w in range(CK_NW - 1, -1, -1):
            kb = i - (CK_NW - 1) + w
            kbc = jnp.maximum(kb, 0)
            ks = pl.multiple_of(kbc * CK_B, CK_B)
            sT = lax.dot_general(k_ref[pl.ds(ks, CK_B), :], qm, _NT,
                                 preferred_element_type=F32) + tab_ref[j, w]
            if w < CK_NW - 1:
                sT = jnp.where(kb >= 0, sT, NEG)
            carry = _softmax_step(sT, _load_vt(vt_ref, kbc, j, CK_B), carry)
        m, l, acc = carry
        outs.append(acc / l)
    _store_pair(o_ref, outs)


def _chunk_attn(main, q_blk0, k_blk0, vt_arr, vt_blk0, ext, n_pairs):
    B, S, _ = main.shape
    return pl.pallas_call(
        _chunk_kernel,
        grid=(B, n_pairs, S // CK_B),
        in_specs=[
            pl.BlockSpec((None, CK_B, LANES), lambda b, p, i: (b, i, q_blk0 + p)),
            pl.BlockSpec((None, S, LANES), lambda b, p, i: (b, 0, k_blk0 + p)),
            pl.BlockSpec((None, S // VT_BLK, LANES, VT_BLK),
                         lambda b, p, i: (b, 0, vt_blk0 + p, 0)),
            pl.BlockSpec((None, 2, CK_EXT), lambda b, p, i: (p, 0, 0)),
        ],
        out_specs=pl.BlockSpec((None, CK_B, LANES), lambda b, p, i: (b, i, p)),
        out_shape=jax.ShapeDtypeStruct((B, S, n_pairs * LANES), BF16),
        scratch_shapes=[pltpu.VMEM((2, CK_NW, CK_B, CK_B), F32)],
        compiler_params=_cparams(("arbitrary", "arbitrary", "arbitrary")),
        name="chunk_attn",
    )(main, main, vt_arr, ext)


def _sb_kernel(q_ref, k_ref, vt_ref, o_ref, *, bq, bk):
    qs = pl.program_id(2) * bq
    row = lax.broadcasted_iota(jnp.int32, (bk, bq), 0)
    col = lax.broadcasted_iota(jnp.int32, (bk, bq), 1)
    ur = lax.broadcasted_iota(jnp.int32, (bk, 2 * bk), 0)
    uc = lax.broadcasted_iota(jnp.int32, (bk, 2 * bk), 1) & (bk - 1)
    upper2 = jnp.where(uc > ur, 1.0, 0.0).astype(BF16)
    n_diag = bq // bk
    n_full = qs // bk
    q2 = q_ref[...]
    outs = []
    for j in range(2):
        qm = _pair_mask_q(q2, j)

        def step(kb, carry, masked, qm=qm, j=j):
            tail, acc = carry
            ks = pl.multiple_of(kb * bk, bk)
            z = lax.dot_general(k_ref[pl.ds(ks, bk), :], qm, _NT,
                                preferred_element_type=F32)
            l1 = jnp.log(1.0 + jnp.exp(-jnp.abs(z)))
            log_beta = jnp.minimum(z, 0.0) - l1
            log_keep = log_beta - z
            if masked:
                valid = ks + row < qs + col
                log_keep = jnp.where(valid, log_keep, 0.0)
            hi = log_keep.astype(BF16)
            lo = (log_keep - hi.astype(F32)).astype(BF16)
            sfx = jnp.dot(upper2, jnp.concatenate([hi, lo], axis=0),
                          preferred_element_type=F32)
            a = jnp.exp(log_beta + sfx + tail)
            if masked:
                a = jnp.where(valid, a, 0.0)
            acc = acc + jnp.dot(_load_vt(vt_ref, kb, j, bk), a.astype(BF16),
                                preferred_element_type=F32)
            tail = tail + sfx[0:1, :] + log_keep[0:1, :]
            return tail, acc

        carry = (jnp.zeros((1, bq), F32), jnp.zeros((HEAD_DIM, bq), F32))
        for d in range(n_diag - 1, -1, -1):
            carry = step(n_full + d, carry, True)

        def cond(state):
            kb, (tail, _) = state
            return (kb >= 0) & (jnp.max(tail) > SB_ZERO_LOG)

        def body(state):
            kb, c = state
            return kb - 1, step(kb, c, False)

        _, (_, acc) = lax.while_loop(cond, body, (n_full - 1, carry))
        outs.append(acc)
    _store_pair(o_ref, outs)


def _sb_attn(main, q_blk0, k_blk0, vt_arr, vt_blk0, n_pairs, bq=256, bk=128):
    B, S, _ = main.shape
    return pl.pallas_call(
        functools.partial(_sb_kernel, bq=bq, bk=bk),
        grid=(B, n_pairs, S // bq),
        in_specs=[
            pl.BlockSpec((None, bq, LANES), lambda b, p, i: (b, i, q_blk0 + p)),
            pl.BlockSpec((None, S, LANES), lambda b, p, i: (b, 0, k_blk0 + p)),
            pl.BlockSpec((None, S // VT_BLK, LANES, VT_BLK),
                         lambda b, p, i: (b, 0, vt_blk0 + p, 0)),
        ],
        out_specs=pl.BlockSpec((None, bq, LANES), lambda b, p, i: (b, i, p)),
        out_shape=jax.ShapeDtypeStruct((B, S, n_pairs * LANES), BF16),
        compiler_params=_cparams(("arbitrary", "arbitrary", "arbitrary")),
        name="sb_attn",
    )(main, main, vt_arr)


def _mla_prep_kernel(aux_ref, pos_ref, invf_ref, qn_ref, kvn_ref, wuq_ref, wuqr_ref,
                     wk_ref, wvt_ref, oq_ref, ok_ref, ovt_ref):
    ang = pos_ref[...] * invf_ref[...]
    cos, sin = jnp.cos(ang), jnp.sin(ang)
    cq = _rms_bf16(aux_ref[:, 0:Q_LORA], qn_ref[...])
    ckv = _rms_bf16(aux_ref[:, Q_LORA:Q_LORA + KV_LORA], kvn_ref[...])
    o = Q_LORA + KV_LORA
    k_rope = aux_ref[:, o:o + LANES] * cos + aux_ref[:, o + LANES:o + 2 * LANES] * sin
    n_heads = oq_ref.shape[-1] // LANES
    for h in range(0, n_heads, 2):
        cols = slice(h * LANES, (h + 2) * LANES)
        qa = jnp.dot(cq, wuq_ref[:, cols], preferred_element_type=F32)
        qb = jnp.dot(cq, wuqr_ref[:, cols], preferred_element_type=F32)
        kn = jnp.dot(ckv, wk_ref[:, cols], preferred_element_type=F32)
        for d in range(2):
            c1 = slice(d * LANES, (d + 1) * LANES)
            c2 = slice((h + d) * LANES, (h + d + 1) * LANES)
            oq_ref[:, c2] = (qa[:, c1] * cos + qb[:, c1] * sin).astype(BF16)
            ok_ref[:, c2] = (kn[:, c1] + k_rope).astype(BF16)
    nv = wvt_ref.shape[0]
    for r in range(0, nv, 256):
        vt = lax.dot_general(wvt_ref[r:r + 256, :], ckv, _NT,
                             preferred_element_type=F32).astype(BF16)
        _store_vt(ovt_ref, vt, r)


def _mla_prep(aux, pos, invf, qn, kvn, wuq, wuqr, wk, wvt, tm=512):
    B, S, na = aux.shape
    nq, nv = wuq.shape[1], wvt.shape[0]
    full = lambda a: pl.BlockSpec(a.shape, lambda b, i: (0,) * a.ndim)
    return pl.pallas_call(
        _mla_prep_kernel,
        grid=(B, S // tm),
        in_specs=[
            pl.BlockSpec((None, tm, na), lambda b, i: (b, i, 0)),
            pl.BlockSpec((None, tm, 1), lambda b, i: (b, i, 0)),
            full(invf), full(qn), full(kvn), full(wuq), full(wuqr), full(wk), full(wvt),
        ],
        out_specs=[
            pl.BlockSpec((None, tm, nq), lambda b, i: (b, i, 0)),
            pl.BlockSpec((None, tm, nq), lambda b, i: (b, i, 0)),
            pl.BlockSpec((None, tm // VT_BLK, nv, VT_BLK), lambda b, i: (b, i, 0, 0)),
        ],
        out_shape=[
            jax.ShapeDtypeStruct((B, S, nq), BF16),
            jax.ShapeDtypeStruct((B, S, nq), BF16),
            jax.ShapeDtypeStruct((B, S // VT_BLK, nv, VT_BLK), BF16),
        ],
        compiler_params=_cparams(("arbitrary", "arbitrary")),
        name="mla_prep",
    )(aux, pos, invf, qn, kvn, wuq, wuqr, wk, wvt)


def _mlp_kernel(x_ref, oa_ref, ob_ref, wo_ref, g_ref, wu_ref, wd_ref, gf_ref, out_ref,
                x1_ref, h_ref, acc_ref, *, final_norm):
    f = pl.program_id(1)

    @pl.when(f == 0)
    def _():
        na = oa_ref.shape[-1]
        x1 = (x_ref[...]
              + jnp.dot(oa_ref[...], wo_ref[0:na, :], preferred_element_type=F32)
              + jnp.dot(ob_ref[...], wo_ref[na:, :], preferred_element_type=F32))
        x1_ref[...] = x1
        h_ref[...] = _rms_bf16(x1, g_ref[...])
        acc_ref[...] = jnp.zeros_like(acc_ref)

    u = jnp.dot(h_ref[...], wu_ref[...], preferred_element_type=F32)
    a = jnp.square(jnp.maximum(u, 0.0)).astype(BF16)
    acc_ref[...] += jnp.dot(a, wd_ref[...], preferred_element_type=F32)

    @pl.when(f == pl.num_programs(1) - 1)
    def _():
        y = x1_ref[...] + acc_ref[...]
        if final_norm:
            ms = jnp.mean(y * y, axis=-1, keepdims=True)
            y = y * lax.rsqrt(ms + EPS) * gf_ref[...]
        out_ref[...] = y


def _mlp(x2, oa, ob, wo, g, wu, wd, gf, final_norm, tm=512, tf=1024):
    T, D = x2.shape
    F = wu.shape[1]
    na, nb = oa.shape[1], ob.shape[1]
    return pl.pallas_call(
        functools.partial(_mlp_kernel, final_norm=final_norm),
        grid=(T // tm, F // tf),
        in_specs=[
            pl.BlockSpec((tm, D), lambda i, f: (i, 0)),
            pl.BlockSpec((tm, na), lambda i, f: (i, 0)),
            pl.BlockSpec((tm, nb), lambda i, f: (i, 0)),
            pl.BlockSpec((na + nb, D), lambda i, f: (0, 0)),
            pl.BlockSpec((1, D), lambda i, f: (0, 0)),
            pl.BlockSpec((D, tf), lambda i, f: (0, f)),
            pl.BlockSpec((tf, D), lambda i, f: (f, 0)),
            pl.BlockSpec((1, D), lambda i, f: (0, 0)),
        ],
        out_specs=pl.BlockSpec((tm, D), lambda i, f: (i, 0)),
        out_shape=jax.ShapeDtypeStruct((T, D), F32),
        scratch_shapes=[pltpu.VMEM((tm, D), F32), pltpu.VMEM((tm, D), BF16),
                        pltpu.VMEM((tm, D), F32)],
        compiler_params=_cparams(("arbitrary", "arbitrary")),
        name="mlp",
    )(x2, oa, ob, wo, g.reshape(1, D), wu, wd, gf.reshape(1, D))


def _pad_cols(w, n):
    return jnp.pad(w, ((0, 0), (0, n - w.shape[1])))


def _rot_cols(w):
    half = ROPE_DIM // 2
    return jnp.concatenate([-w[:, half:], w[:, :half]], axis=1)


def _rope_slab(w):
    z = jnp.zeros((w.shape[0], NOPE_DIM), w.dtype)
    return jnp.concatenate([z, w, jnp.zeros((w.shape[0], LANES - NOPE_DIM - ROPE_DIM), w.dtype)], axis=1)


def _even_layer(x, g_mix, w_in, b_forget, rel_bias, w_out, g_mlp, w_up, w_down, g_final,
                final_norm):
    B, S, D = x.shape
    hf, hc = b_forget.shape[0], rel_bias.shape[0]
    wf, wc = hf * HEAD_DIM, hc * HEAD_DIM
    o = np.cumsum([0, wf, wf, wf, hf, wc, wc, wc])
    qa, ka, va, fa, qb, kb, vb = [w_in[:, o[n]:o[n + 1]] for n in range(7)]
    inv_sqrt_d = HEAD_DIM ** -0.5
    wm = jnp.concatenate([qa * inv_sqrt_d, ka, qb * inv_sqrt_d, kb], axis=1).astype(BF16)
    wvt = jnp.concatenate([va, vb], axis=1).T.astype(BF16)
    wa = _pad_cols(fa, LANES).astype(BF16)
    main, vt, aux = _inproj(x, g_mix, wm, wvt, wa)

    cum = _logcum(aux, _pad_cols(b_forget.reshape(1, hf), LANES))[:, :, :hf]
    cum = cum.reshape(B, S, hf // 2, 2)
    crow = jnp.transpose(cum, (0, 2, 3, 1))
    ccol = jnp.transpose(cum, (0, 2, 1, 3))
    o_a = _flash("fox", main, 0, main, wf // LANES, vt, 0, (crow, ccol), hf // 2, None)

    left = CK_EXT - rel_bias.shape[1] - (CHUNK + 1)
    ext = jnp.pad(rel_bias, ((0, 0), (CHUNK + 1, left)), mode="edge").reshape(hc // 2, 2, CK_EXT)
    o_b = _chunk_attn(main, 2 * wf // LANES, (2 * wf + wc) // LANES, vt, wf // LANES, ext, hc // 2)

    y = _mlp(x.reshape(B * S, D), o_a.reshape(B * S, wf), o_b.reshape(B * S, wc),
             w_out.astype(BF16), g_mlp, w_up.astype(BF16), w_down.astype(BF16), g_final,
             final_norm)
    return y.reshape(B, S, D)


def _odd_layer(x, positions, g_mix, w_in, q_norm, kv_norm, w_uq, w_ukv, w_out, g_mlp, w_up,
               w_down, g_final, final_norm):
    B, S, D = x.shape
    hm = w_ukv.shape[1] // (NOPE_DIM + HEAD_DIM)
    ws = w_in.shape[1] - Q_LORA - KV_LORA - ROPE_DIM
    hs = (ws // 3) // HEAD_DIM
    wsb = hs * HEAD_DIM
    o = np.cumsum([0, wsb, wsb, wsb, Q_LORA, KV_LORA, ROPE_DIM])
    qc, kc, vc, w_cq, w_ckv, w_kr = [w_in[:, o[n]:o[n + 1]] for n in range(6)]
    wm = jnp.concatenate([qc * HEAD_DIM ** -0.5, kc], axis=1).astype(BF16)
    wa = jnp.concatenate([w_cq, w_ckv, _rope_slab(w_kr), _rope_slab(_rot_cols(w_kr))],
                         axis=1).astype(BF16)
    main, vt, aux = _inproj(x, g_mix, wm, vc.T.astype(BF16), wa)
    o_c = _sb_attn(main, 0, wsb // LANES, vt, 0, hs // 2)

    dq = NOPE_DIM + ROPE_DIM
    wuq3 = w_uq.reshape(Q_LORA, hm, dq)
    nope, ropew = wuq3[:, :, :NOPE_DIM], wuq3[:, :, NOPE_DIM:]
    zq = jnp.zeros((Q_LORA, hm, LANES - dq), w_uq.dtype)
    wuq = jnp.concatenate([nope, ropew, zq], axis=2).reshape(Q_LORA, hm * LANES).astype(BF16)
    half = ROPE_DIM // 2
    ropr = jnp.concatenate([-ropew[:, :, half:], ropew[:, :, :half]], axis=2)
    wuqr = jnp.concatenate([jnp.zeros_like(nope), ropr, zq], axis=2)
    wuqr = wuqr.reshape(Q_LORA, hm * LANES).astype(BF16)
    wkv3 = w_ukv.reshape(KV_LORA, hm, NOPE_DIM + HEAD_DIM)
    wk = jnp.concatenate([wkv3[:, :, :NOPE_DIM],
                          jnp.zeros((KV_LORA, hm, LANES - NOPE_DIM), w_ukv.dtype)], axis=2)
    wk = wk.reshape(KV_LORA, hm * LANES).astype(BF16)
    wv_t = wkv3[:, :, NOPE_DIM:].reshape(KV_LORA, hm * HEAD_DIM).T.astype(BF16)
    freqs = (ROPE_THETA ** (-jnp.arange(half, dtype=F32) / half))
    invf = jnp.zeros((1, LANES), F32)
    invf = invf.at[0, NOPE_DIM:NOPE_DIM + half].set(freqs)
    invf = invf.at[0, NOPE_DIM + half:NOPE_DIM + ROPE_DIM].set(freqs)
    pos = positions.astype(F32).reshape(B, S, 1)
    qm, km, vtm = _mla_prep(aux, pos, invf, q_norm.reshape(1, Q_LORA),
                            kv_norm.reshape(1, KV_LORA), wuq, wuqr, wk, wv_t)
    o_d = _flash("mla", qm, 0, km, 0, vtm, 0, None, hm // 2, dq ** -0.5)

    y = _mlp(x.reshape(B * S, D), o_c.reshape(B * S, wsb), o_d.reshape(B * S, hm * HEAD_DIM),
             w_out.astype(BF16), g_mlp, w_up.astype(BF16), w_down.astype(BF16), g_final,
             final_norm)
    return y.reshape(B, S, D)


def kernel(x, positions, norm_mix, norm_mlp, norm_final, w_in_ab, b_forget, rel_bias, w_out_ab,
           w_in_cd, q_norm, kv_norm, w_uq, w_ukv, w_out_cd, w_up, w_down):
    depth = norm_mix.shape[0]
    for layer in range(depth):
        last = layer == depth - 1
        if layer % 2 == 0:
            e = layer // 2
            x = _even_layer(x, norm_mix[layer], w_in_ab[e], b_forget[e], rel_bias[e], w_out_ab[e],
                            norm_mlp[layer], w_up[layer], w_down[layer], norm_final, last)
        else:
            o = layer // 2
            x = _odd_layer(x, positions, norm_mix[layer], w_in_cd[o], q_norm[o], kv_norm[o],
                           w_uq[o], w_ukv[o], w_out_cd[o], norm_mlp[layer], w_up[layer],
                           w_down[layer], norm_final, last)
    return x
```

```python
import functools
import math

import numpy as np
import jax
import jax.numpy as jnp
from jax import lax
from jax.experimental import pallas as pl
from jax.experimental.pallas import tpu as pltpu

F32 = jnp.float32
BF16 = jnp.bfloat16

EPS = 1e-6
HEAD_DIM = 64
CHUNK = 64
N_LEFT_CHUNKS = 8
REL_CLIP = 256
ROPE_DIM = 32
NOPE_DIM = 64
ROPE_THETA = 10000.0
Q_LORA = 384
KV_LORA = 256

LANES = 128
VT_BLK = LANES
SUB = LANES
FLASH_HP = 4
NEG = -1e30
SB_ZERO_LOG = -104.0
VMEM_LIMIT = 56 * 1024 * 1024

_NT = (((1,), (1,)), ((), ()))


def _cparams(sem):
    return pltpu.CompilerParams(dimension_semantics=sem, vmem_limit_bytes=VMEM_LIMIT)


def _rms_bf16(x, g):
    ms = jnp.mean(x * x, axis=-1, keepdims=True)
    return (x * lax.rsqrt(ms + EPS) * g).astype(BF16)


def _store_vt(ovt_ref, vt, row0):
    rows, tm = vt.shape
    for c in range(tm // VT_BLK):
        ovt_ref[c, row0:row0 + rows, :] = vt[:, c * VT_BLK:(c + 1) * VT_BLK]


def _inproj_kernel(x_ref, g_ref, wm_ref, wvt_ref, wa_ref, om_ref, ovt_ref, oa_ref):
    h = _rms_bf16(x_ref[...], g_ref[...])
    nm = om_ref.shape[-1]
    for c in range(0, nm, 512):
        om_ref[:, c:c + 512] = jnp.dot(
            h, wm_ref[:, c:c + 512], preferred_element_type=F32).astype(BF16)
    nv = wvt_ref.shape[0]
    for r in range(0, nv, 256):
        vt = lax.dot_general(wvt_ref[r:r + 256, :], h, _NT,
                             preferred_element_type=F32).astype(BF16)
        _store_vt(ovt_ref, vt, r)
    oa_ref[...] = jnp.dot(h, wa_ref[...], preferred_element_type=F32)


def _inproj(x, g, wm, wvt, wa, tm=512):
    B, S, D = x.shape
    nm, nv, na = wm.shape[1], wvt.shape[0], wa.shape[1]
    return pl.pallas_call(
        _inproj_kernel,
        grid=(B, S // tm),
        in_specs=[
            pl.BlockSpec((None, tm, D), lambda b, i: (b, i, 0)),
            pl.BlockSpec((1, D), lambda b, i: (0, 0)),
            pl.BlockSpec((D, nm), lambda b, i: (0, 0)),
            pl.BlockSpec((nv, D), lambda b, i: (0, 0)),
            pl.BlockSpec((D, na), lambda b, i: (0, 0)),
        ],
        out_specs=[
            pl.BlockSpec((None, tm, nm), lambda b, i: (b, i, 0)),
            pl.BlockSpec((None, tm // VT_BLK, nv, VT_BLK), lambda b, i: (b, i, 0, 0)),
            pl.BlockSpec((None, tm, na), lambda b, i: (b, i, 0)),
        ],
        out_shape=[
            jax.ShapeDtypeStruct((B, S, nm), BF16),
            jax.ShapeDtypeStruct((B, S // VT_BLK, nv, VT_BLK), BF16),
            jax.ShapeDtypeStruct((B, S, na), F32),
        ],
        compiler_params=_cparams(("arbitrary", "arbitrary")),
        name="inproj",
    )(x, g.reshape(1, D), wm, wvt, wa)


def _split3(x):
    hi = x.astype(BF16)
    r = x - hi.astype(F32)
    mid = r.astype(BF16)
    lo = (r - mid.astype(F32)).astype(BF16)
    return hi, mid, lo


def _logcum_kernel(fa_ref, b_ref, o_ref, carry_ref):
    @pl.when(pl.program_id(1) == 0)
    def _():
        carry_ref[...] = jnp.zeros_like(carry_ref)

    z = fa_ref[...] + b_ref[...]
    lf = jnp.minimum(z, 0.0) - jnp.log(1.0 + jnp.exp(-jnp.abs(z)))
    tc = lf.shape[0]
    r = lax.broadcasted_iota(jnp.int32, (tc, tc), 0)
    c = lax.broadcasted_iota(jnp.int32, (tc, tc), 1)
    tri = jnp.where(r >= c, 1.0, 0.0).astype(BF16)
    cs = carry_ref[...]
    for part in _split3(lf):
        cs = cs + jnp.dot(tri, part, preferred_element_type=F32)
    o_ref[...] = cs
    carry_ref[...] = cs[tc - 1:tc, :]


def _logcum(fa, bias, tc=512):
    B, S, W = fa.shape
    return pl.pallas_call(
        _logcum_kernel,
        grid=(B, S // tc),
        in_specs=[pl.BlockSpec((None, tc, W), lambda b, i: (b, i, 0)),
                  pl.BlockSpec((1, W), lambda b, i: (0, 0))],
        out_specs=pl.BlockSpec((None, tc, W), lambda b, i: (b, i, 0)),
        out_shape=jax.ShapeDtypeStruct((B, S, W), F32),
        scratch_shapes=[pltpu.VMEM((1, W), F32)],
        compiler_params=_cparams(("arbitrary", "arbitrary")),
        name="logcum",
    )(fa, bias)


def _pair_mask_q(q2, j):
    lane = lax.broadcasted_iota(jnp.int32, q2.shape, 1)
    keep = (lane >= HEAD_DIM * j) & (lane < HEAD_DIM * (j + 1))
    return jnp.where(keep, q2, jnp.zeros_like(q2))


def _load_vt(vt_ref, kb, j, bk):
    n = bk // VT_BLK
    rows = slice(HEAD_DIM * j, HEAD_DIM * (j + 1))
    parts = [vt_ref[kb * n + c, rows, :] for c in range(n)]
    return parts[0] if n == 1 else jnp.concatenate(parts, axis=1)


def _softmax_step(tiles, vts, carry):
    m, l, acc = carry
    m_new = m
    for tile in tiles:
        m_new = jnp.maximum(m_new, jnp.max(tile(), axis=0, keepdims=True))
    alpha = jnp.exp(m - m_new)
    l = alpha * l
    pv = None
    for tile, vt in zip(tiles, vts):
        p = jnp.exp(tile() - m_new)
        l = l + jnp.sum(p, axis=0, keepdims=True)
        d = jnp.dot(vt, p.astype(BF16), preferred_element_type=F32)
        pv = d if pv is None else pv + d
    return m_new, l, alpha * acc + pv


def _softmax_init(bq):
    return (jnp.full((1, bq), NEG, F32), jnp.zeros((1, bq), F32),
            jnp.zeros((HEAD_DIM, bq), F32))


def _store_heads(o_ref, outs):
    oT = jnp.concatenate(outs, axis=0)
    o_ref[...] = oT.T.astype(o_ref.dtype)


def _flash_kernel(*refs, mode, hp, bq, scale):
    if mode == "fox":
        q_ref, k_ref, vt_ref, crow_ref, ccol_ref, o_ref, sa_ref, sb_ref = refs
    else:
        q_ref, k_ref, vt_ref, o_ref, sa_ref, sb_ref = refs
    qs = pl.program_id(2) * bq
    row = lax.broadcasted_iota(jnp.int32, (SUB, bq), 0)
    col = lax.broadcasted_iota(jnp.int32, (SUB, bq), 1)
    if mode == "fox":
        kcols = [slice(LANES * (h // 2), LANES * (h // 2 + 1)) for h in range(hp)]
        qms = [_pair_mask_q(q_ref[:, kcols[h]], h % 2) for h in range(hp)]
    else:
        kcols = [slice(LANES * h, LANES * (h + 1)) for h in range(hp)]
        qms = [q_ref[:, kcols[h]] for h in range(hp)]

    def scores(sb, h, masked):
        ks = pl.multiple_of(sb * SUB, SUB)
        sT = lax.dot_general(k_ref[pl.ds(ks, SUB), kcols[h]], qms[h], _NT,
                             preferred_element_type=F32)
        if mode == "fox":
            sT = sT + (crow_ref[h:h + 1, :] - ccol_ref[pl.ds(ks, SUB), h:h + 1])
            if masked:
                sT = jnp.where(ks + row <= qs + col, sT, NEG)
        else:
            sT = sT * scale
            if masked:
                sT = jnp.where(((ks + row) >> 6) <= ((qs + col) >> 6), sT, NEG)
        return sT

    nsub = bq // SUB

    def produce(buf, sb0, masked):
        for h in range(hp):
            for c in range(nsub):
                buf[h, c] = scores(sb0 + c, h, masked)

    def consume(buf, sb0, carries):
        out = []
        for h in range(hp):
            tiles = [lambda h=h, c=c: buf[h, c] for c in range(nsub)]
            vts = [vt_ref[sb0 + c, HEAD_DIM * h:HEAD_DIM * (h + 1), :] for c in range(nsub)]
            out.append(_softmax_step(tiles, vts, carries[h]))
        return tuple(out)

    n = pl.program_id(2)
    diag_sb = qs // SUB
    produce(sa_ref, diag_sb, True)

    def pair(j, carries):
        produce(sb_ref, 2 * j * nsub, False)
        carries = consume(sa_ref, jnp.where(j == 0, diag_sb, (2 * j - 1) * nsub), carries)
        produce(sa_ref, jnp.minimum(2 * j + 1, n - 1) * nsub, False)
        return consume(sb_ref, 2 * j * nsub, carries)

    carries = tuple(_softmax_init(bq) for _ in range(hp))
    carries = lax.fori_loop(0, (n + 1) // 2, pair, carries)
    carries = lax.cond(
        n % 2 == 0,
        lambda c: consume(sa_ref, jnp.where(n == 0, diag_sb, (n - 1) * nsub), c),
        lambda c: c, carries)
    _store_heads(o_ref, [acc / l for (_, l, acc) in carries])


def _flash(mode, q_arr, q_col0, k_arr, k_col0, vt_arr, vt_row0, extra, n_heads,
           scale, hp=FLASH_HP, bq=256):
    B, S, _ = q_arr.shape
    qw = (HEAD_DIM if mode == "fox" else LANES) * hp
    vw = HEAD_DIM * hp
    in_specs = [
        pl.BlockSpec((None, bq, qw), lambda b, g, i: (b, i, q_col0 // qw + g)),
        pl.BlockSpec((None, S, qw), lambda b, g, i: (b, 0, k_col0 // qw + g)),
        pl.BlockSpec((None, S // VT_BLK, vw, VT_BLK),
                     lambda b, g, i: (b, 0, vt_row0 // vw + g, 0)),
    ]
    args = [q_arr, k_arr, vt_arr]
    if mode == "fox":
        crow, ccol = extra
        in_specs += [
            pl.BlockSpec((None, None, hp, bq), lambda b, g, i: (b, g, 0, i)),
            pl.BlockSpec((None, None, S, hp), lambda b, g, i: (b, g, 0, 0)),
        ]
        args += [crow, ccol]
    return pl.pallas_call(
        functools.partial(_flash_kernel, mode=mode, hp=hp, bq=bq, scale=scale),
        grid=(B, n_heads // hp, S // bq),
        in_specs=in_specs,
        out_specs=pl.BlockSpec((None, bq, vw), lambda b, g, i: (b, i, g)),
        out_shape=jax.ShapeDtypeStruct((B, S, n_heads * HEAD_DIM), BF16),
        scratch_shapes=[pltpu.VMEM((hp, bq // SUB, SUB, bq), F32)] * 2,
        compiler_params=_cparams(("arbitrary", "arbitrary", "arbitrary")),
        name="flash_" + mode,
    )(*args)


CK_B = 2 * CHUNK
CK_NW = N_LEFT_CHUNKS * CHUNK // CK_B + 1
CK_EXT = (CK_NW + 1) * CK_B


def _chunk_kernel(q_ref, k_ref, vt_ref, ext_ref, o_ref, tab_ref, s_ref, *, hp):
    i = pl.program_id(1)

    @pl.when(i == 0)
    def _():
        jj = lax.broadcasted_iota(jnp.int32, (CK_B, CK_B), 0)
        rr = lax.broadcasted_iota(jnp.int32, (CK_B, CK_B), 1)
        for h in range(hp):
            for w in range(CK_NW):
                a = (CK_NW - 1 - w) * CK_B
                g = jnp.broadcast_to(ext_ref[h:h + 1, a:a + 2 * CK_B], (CK_B, 2 * CK_B))
                t = pltpu.roll(g, CK_B, 1, stride=1, stride_axis=0)[:, :CK_B]
                if w == 0:
                    t = jnp.where((rr >= CHUNK) & (jj < CHUNK), NEG, t)
                if w == CK_NW - 1:
                    t = jnp.where((rr < CHUNK) & (jj >= CHUNK), NEG, t)
                tab_ref[h, w] = t

    kbs = [i - (CK_NW - 1) + w for w in range(CK_NW)]
    kbc = [jnp.maximum(kb, 0) for kb in kbs]
    outs = []
    for h in range(hp):
        kcols = slice(LANES * (h // 2), LANES * (h // 2 + 1))
        qm = _pair_mask_q(q_ref[:, kcols], h % 2)
        for w in range(CK_NW):
            ks = pl.multiple_of(kbc[w] * CK_B, CK_B)
            sT = lax.dot_general(k_ref[pl.ds(ks, CK_B), kcols], qm, _NT,
                                 preferred_element_type=F32) + tab_ref[h, w]
            if w < CK_NW - 1:
                sT = jnp.where(kbs[w] >= 0, sT, NEG)
            s_ref[h, w] = sT
        tiles = [lambda h=h, w=w: s_ref[h, w] for w in range(CK_NW)]
        vts = [vt_ref[kbc[w], HEAD_DIM * h:HEAD_DIM * (h + 1), :] for w in range(CK_NW)]
        _, l, acc = _softmax_step(tiles, vts, _softmax_init(CK_B))
        outs.append(acc / l)
    _store_heads(o_ref, outs)


def _chunk_attn(main, q_col0, k_col0, vt_arr, vt_row0, ext, n_heads):
    B, S, _ = main.shape
    hp = n_heads
    qw, vw = HEAD_DIM * hp, HEAD_DIM * hp
    return pl.pallas_call(
        functools.partial(_chunk_kernel, hp=hp),
        grid=(B, S // CK_B),
        in_specs=[
            pl.BlockSpec((None, CK_B, qw), lambda b, i: (b, i, q_col0 // qw)),
            pl.BlockSpec((None, S, qw), lambda b, i: (b, 0, k_col0 // qw)),
            pl.BlockSpec((None, S // VT_BLK, vw, VT_BLK), lambda b, i: (b, 0, vt_row0 // vw, 0)),
            pl.BlockSpec((hp, CK_EXT), lambda b, i: (0, 0)),
        ],
        out_specs=pl.BlockSpec((None, CK_B, vw), lambda b, i: (b, i, 0)),
        out_shape=jax.ShapeDtypeStruct((B, S, n_heads * HEAD_DIM), BF16),
        scratch_shapes=[pltpu.VMEM((hp, CK_NW, CK_B, CK_B), F32),
                        pltpu.VMEM((hp, CK_NW, CK_B, CK_B), F32)],
        compiler_params=_cparams(("arbitrary", "arbitrary")),
        name="chunk_attn",
    )(main, main, vt_arr, ext)


def _sb_kernel(q_ref, k_ref, vt_ref, o_ref, *, hp, bq):
    qs = pl.program_id(1) * bq
    row = lax.broadcasted_iota(jnp.int32, (SUB, bq), 0)
    col = lax.broadcasted_iota(jnp.int32, (SUB, bq), 1)
    ur = lax.broadcasted_iota(jnp.int32, (SUB, 2 * SUB), 0)
    uc = lax.broadcasted_iota(jnp.int32, (SUB, 2 * SUB), 1) & (SUB - 1)
    upper2 = jnp.where(uc > ur, 1.0, 0.0).astype(BF16)
    kcols = [slice(LANES * (h // 2), LANES * (h // 2 + 1)) for h in range(hp)]
    qms = [_pair_mask_q(q_ref[:, kcols[h]], h % 2) for h in range(hp)]

    def step(sb, carries, masked):
        ks = pl.multiple_of(sb * SUB, SUB)
        out = []
        for h in range(hp):
            tail, acc = carries[h]
            z = lax.dot_general(k_ref[pl.ds(ks, SUB), kcols[h]], qms[h], _NT,
                                preferred_element_type=F32)
            l1 = jnp.log(1.0 + jnp.exp(-jnp.abs(z)))
            log_beta = jnp.minimum(z, 0.0) - l1
            log_keep = log_beta - z
            if masked:
                valid = ks + row < qs + col
                log_keep = jnp.where(valid, log_keep, 0.0)
            hi = log_keep.astype(BF16)
            lo = (log_keep - hi.astype(F32)).astype(BF16)
            sfx = jnp.dot(upper2, jnp.concatenate([hi, lo], axis=0),
                          preferred_element_type=F32)
            a = jnp.exp(log_beta + sfx + tail)
            if masked:
                a = jnp.where(valid, a, 0.0)
            acc = acc + jnp.dot(vt_ref[sb, HEAD_DIM * h:HEAD_DIM * (h + 1), :],
                                a.astype(BF16), preferred_element_type=F32)
            out.append((tail + sfx[0:1, :] + log_keep[0:1, :], acc))
        return tuple(out)

    n_full = qs // SUB
    carries = tuple((jnp.zeros((1, bq), F32), jnp.zeros((HEAD_DIM, bq), F32))
                    for _ in range(hp))
    for d in range(bq // SUB - 1, -1, -1):
        carries = step(n_full + d, carries, True)

    def cond(state):
        sb, carries = state
        tail_max = carries[0][0]
        for h in range(1, hp):
            tail_max = jnp.maximum(tail_max, carries[h][0])
        return (sb >= 0) & (jnp.max(tail_max) > SB_ZERO_LOG)

    def body(state):
        sb, carries = state
        return sb - 1, step(sb, carries, False)

    _, carries = lax.while_loop(cond, body, (n_full - 1, carries))
    _store_heads(o_ref, [acc for (_, acc) in carries])


def _sb_attn(main, q_col0, k_col0, vt_arr, vt_row0, n_heads, bq=256):
    B, S, _ = main.shape
    hp = n_heads
    qw = HEAD_DIM * hp
    return pl.pallas_call(
        functools.partial(_sb_kernel, hp=hp, bq=bq),
        grid=(B, S // bq),
        in_specs=[
            pl.BlockSpec((None, bq, qw), lambda b, i: (b, i, q_col0 // qw)),
            pl.BlockSpec((None, S, qw), lambda b, i: (b, 0, k_col0 // qw)),
            pl.BlockSpec((None, S // VT_BLK, qw, VT_BLK), lambda b, i: (b, 0, vt_row0 // qw, 0)),
        ],
        out_specs=pl.BlockSpec((None, bq, qw), lambda b, i: (b, i, 0)),
        out_shape=jax.ShapeDtypeStruct((B, S, n_heads * HEAD_DIM), BF16),
        compiler_params=_cparams(("arbitrary", "arbitrary")),
        name="sb_attn",
    )(main, main, vt_arr)


def _mla_prep_kernel(aux_ref, pos_ref, invf_ref, qn_ref, kvn_ref, wuq_ref, wuqr_ref,
                     wk_ref, wvt_ref, oq_ref, ok_ref, ovt_ref):
    ang = pos_ref[...] * invf_ref[...]
    cos, sin = jnp.cos(ang), jnp.sin(ang)
    cq = _rms_bf16(aux_ref[:, 0:Q_LORA], qn_ref[...])
    ckv = _rms_bf16(aux_ref[:, Q_LORA:Q_LORA + KV_LORA], kvn_ref[...])
    o = Q_LORA + KV_LORA
    k_rope = aux_ref[:, o:o + LANES] * cos + aux_ref[:, o + LANES:o + 2 * LANES] * sin
    n_heads = oq_ref.shape[-1] // LANES
    for h in range(0, n_heads, 2):
        cols = slice(h * LANES, (h + 2) * LANES)
        qa = jnp.dot(cq, wuq_ref[:, cols], preferred_element_type=F32)
        qb = jnp.dot(cq, wuqr_ref[:, cols], preferred_element_type=F32)
        kn = jnp.dot(ckv, wk_ref[:, cols], preferred_element_type=F32)
        for d in range(2):
            c1 = slice(d * LANES, (d + 1) * LANES)
            c2 = slice((h + d) * LANES, (h + d + 1) * LANES)
            oq_ref[:, c2] = (qa[:, c1] * cos + qb[:, c1] * sin).astype(BF16)
            ok_ref[:, c2] = (kn[:, c1] + k_rope).astype(BF16)
    nv = wvt_ref.shape[0]
    for r in range(0, nv, 256):
        vt = lax.dot_general(wvt_ref[r:r + 256, :], ckv, _NT,
                             preferred_element_type=F32).astype(BF16)
        _store_vt(ovt_ref, vt, r)


def _mla_prep(aux, pos, invf, qn, kvn, wuq, wuqr, wk, wvt, tm=512):
    B, S, na = aux.shape
    nq, nv = wuq.shape[1], wvt.shape[0]
    full = lambda a: pl.BlockSpec(a.shape, lambda b, i: (0,) * a.ndim)
    return pl.pallas_call(
        _mla_prep_kernel,
        grid=(B, S // tm),
        in_specs=[
            pl.BlockSpec((None, tm, na), lambda b, i: (b, i, 0)),
            pl.BlockSpec((None, tm, 1), lambda b, i: (b, i, 0)),
            full(invf), full(qn), full(kvn), full(wuq), full(wuqr), full(wk), full(wvt),
        ],
        out_specs=[
            pl.BlockSpec((None, tm, nq), lambda b, i: (b, i, 0)),
            pl.BlockSpec((None, tm, nq), lambda b, i: (b, i, 0)),
            pl.BlockSpec((None, tm // VT_BLK, nv, VT_BLK), lambda b, i: (b, i, 0, 0)),
        ],
        out_shape=[
            jax.ShapeDtypeStruct((B, S, nq), BF16),
            jax.ShapeDtypeStruct((B, S, nq), BF16),
            jax.ShapeDtypeStruct((B, S // VT_BLK, nv, VT_BLK), BF16),
        ],
        compiler_params=_cparams(("arbitrary", "arbitrary")),
        name="mla_prep",
    )(aux, pos, invf, qn, kvn, wuq, wuqr, wk, wvt)


def _mlp_kernel(x_ref, oa_ref, ob_ref, wo_ref, g_ref, wu_ref, wd_ref, gf_ref, out_ref,
                x1_ref, h_ref, acc_ref, *, final_norm):
    f = pl.program_id(1)

    @pl.when(f == 0)
    def _():
        na = oa_ref.shape[-1]
        x1 = (x_ref[...]
              + jnp.dot(oa_ref[...], wo_ref[0:na, :], preferred_element_type=F32)
              + jnp.dot(ob_ref[...], wo_ref[na:, :], preferred_element_type=F32))
        x1_ref[...] = x1
        h_ref[...] = _rms_bf16(x1, g_ref[...])
        acc_ref[...] = jnp.zeros_like(acc_ref)

    u = jnp.dot(h_ref[...], wu_ref[...], preferred_element_type=F32)
    a = jnp.square(jnp.maximum(u, 0.0)).astype(BF16)
    acc_ref[...] += jnp.dot(a, wd_ref[...], preferred_element_type=F32)

    @pl.when(f == pl.num_programs(1) - 1)
    def _():
        y = x1_ref[...] + acc_ref[...]
        if final_norm:
            ms = jnp.mean(y * y, axis=-1, keepdims=True)
            y = y * lax.rsqrt(ms + EPS) * gf_ref[...]
        out_ref[...] = y


def _mlp(x2, oa, ob, wo, g, wu, wd, gf, final_norm, tm=512, tf=1024):
    T, D = x2.shape
    F = wu.shape[1]
    na, nb = oa.shape[1], ob.shape[1]
    return pl.pallas_call(
        functools.partial(_mlp_kernel, final_norm=final_norm),
        grid=(T // tm, F // tf),
        in_specs=[
            pl.BlockSpec((tm, D), lambda i, f: (i, 0)),
            pl.BlockSpec((tm, na), lambda i, f: (i, 0)),
            pl.BlockSpec((tm, nb), lambda i, f: (i, 0)),
            pl.BlockSpec((na + nb, D), lambda i, f: (0, 0)),
            pl.BlockSpec((1, D), lambda i, f: (0, 0)),
            pl.BlockSpec((D, tf), lambda i, f: (0, f)),
            pl.BlockSpec((tf, D), lambda i, f: (f, 0)),
            pl.BlockSpec((1, D), lambda i, f: (0, 0)),
        ],
        out_specs=pl.BlockSpec((tm, D), lambda i, f: (i, 0)),
        out_shape=jax.ShapeDtypeStruct((T, D), F32),
        scratch_shapes=[pltpu.VMEM((tm, D), F32), pltpu.VMEM((tm, D), BF16),
                        pltpu.VMEM((tm, D), F32)],
        compiler_params=_cparams(("arbitrary", "arbitrary")),
        name="mlp",
    )(x2, oa, ob, wo, g.reshape(1, D), wu, wd, gf.reshape(1, D))


def _pad_cols(w, n):
    return jnp.pad(w, ((0, 0), (0, n - w.shape[1])))


def _rot_cols(w):
    half = ROPE_DIM // 2
    return jnp.concatenate([-w[:, half:], w[:, :half]], axis=1)


def _rope_slab(w):
    z = jnp.zeros((w.shape[0], NOPE_DIM), w.dtype)
    return jnp.concatenate([z, w, jnp.zeros((w.shape[0], LANES - NOPE_DIM - ROPE_DIM), w.dtype)], axis=1)


def _even_layer(x, g_mix, w_in, b_forget, rel_bias, w_out, g_mlp, w_up, w_down, g_final,
                final_norm):
    B, S, D = x.shape
    hf, hc = b_forget.shape[0], rel_bias.shape[0]
    wf, wc = hf * HEAD_DIM, hc * HEAD_DIM
    o = np.cumsum([0, wf, wf, wf, hf, wc, wc, wc])
    qa, ka, va, fa, qb, kb, vb = [w_in[:, o[n]:o[n + 1]] for n in range(7)]
    inv_sqrt_d = HEAD_DIM ** -0.5
    wm = jnp.concatenate([qa * inv_sqrt_d, ka, qb * inv_sqrt_d, kb], axis=1).astype(BF16)
    wvt = jnp.concatenate([va, vb], axis=1).T.astype(BF16)
    wa = _pad_cols(fa, LANES).astype(BF16)
    main, vt, aux = _inproj(x, g_mix, wm, wvt, wa)

    cum = _logcum(aux, _pad_cols(b_forget.reshape(1, hf), LANES))[:, :, :hf]
    cum = cum.reshape(B, S, hf // FLASH_HP, FLASH_HP)
    crow = jnp.transpose(cum, (0, 2, 3, 1))
    ccol = jnp.transpose(cum, (0, 2, 1, 3))
    o_a = _flash("fox", main, 0, main, wf, vt, 0, (crow, ccol), hf, None)

    right = CK_EXT - rel_bias.shape[1] - (CHUNK + 1)
    ext = jnp.pad(rel_bias, ((0, 0), (CHUNK + 1, right)), mode="edge")
    o_b = _chunk_attn(main, 2 * wf, 2 * wf + wc, vt, wf, ext, hc)

    y = _mlp(x.reshape(B * S, D), o_a.reshape(B * S, wf), o_b.reshape(B * S, wc),
             w_out.astype(BF16), g_mlp, w_up.astype(BF16), w_down.astype(BF16), g_final,
             final_norm)
    return y.reshape(B, S, D)


def _odd_layer(x, positions, g_mix, w_in, q_norm, kv_norm, w_uq, w_ukv, w_out, g_mlp, w_up,
               w_down, g_final, final_norm):
    B, S, D = x.shape
    hm = w_ukv.shape[1] // (NOPE_DIM + HEAD_DIM)
    ws = w_in.shape[1] - Q_LORA - KV_LORA - ROPE_DIM
    hs = (ws // 3) // HEAD_DIM
    wsb = hs * HEAD_DIM
    o = np.cumsum([0, wsb, wsb, wsb, Q_LORA, KV_LORA, ROPE_DIM])
    qc, kc, vc, w_cq, w_ckv, w_kr = [w_in[:, o[n]:o[n + 1]] for n in range(6)]
    wm = jnp.concatenate([qc * HEAD_DIM ** -0.5, kc], axis=1).astype(BF16)
    wa = jnp.concatenate([w_cq, w_ckv, _rope_slab(w_kr), _rope_slab(_rot_cols(w_kr))],
                         axis=1).astype(BF16)
    main, vt, aux = _inproj(x, g_mix, wm, vc.T.astype(BF16), wa)
    o_c = _sb_attn(main, 0, wsb, vt, 0, hs)

    dq = NOPE_DIM + ROPE_DIM
    wuq3 = w_uq.reshape(Q_LORA, hm, dq)
    nope, ropew = wuq3[:, :, :NOPE_DIM], wuq3[:, :, NOPE_DIM:]
    zq = jnp.zeros((Q_LORA, hm, LANES - dq), w_uq.dtype)
    wuq = jnp.concatenate([nope, ropew, zq], axis=2).reshape(Q_LORA, hm * LANES).astype(BF16)
    half = ROPE_DIM // 2
    ropr = jnp.concatenate([-ropew[:, :, half:], ropew[:, :, :half]], axis=2)
    wuqr = jnp.concatenate([jnp.zeros_like(nope), ropr, zq], axis=2)
    wuqr = wuqr.reshape(Q_LORA, hm * LANES).astype(BF16)
    wkv3 = w_ukv.reshape(KV_LORA, hm, NOPE_DIM + HEAD_DIM)
    wk = jnp.concatenate([wkv3[:, :, :NOPE_DIM],
                          jnp.zeros((KV_LORA, hm, LANES - NOPE_DIM), w_ukv.dtype)], axis=2)
    wk = wk.reshape(KV_LORA, hm * LANES).astype(BF16)
    wv_t = wkv3[:, :, NOPE_DIM:].reshape(KV_LORA, hm * HEAD_DIM).T.astype(BF16)
    freqs = (ROPE_THETA ** (-jnp.arange(half, dtype=F32) / half))
    invf = jnp.zeros((1, LANES), F32)
    invf = invf.at[0, NOPE_DIM:NOPE_DIM + half].set(freqs)
    invf = invf.at[0, NOPE_DIM + half:NOPE_DIM + ROPE_DIM].set(freqs)
    pos = positions.astype(F32).reshape(B, S, 1)
    qm, km, vtm = _mla_prep(aux, pos, invf, q_norm.reshape(1, Q_LORA),
                            kv_norm.reshape(1, KV_LORA), wuq, wuqr, wk, wv_t)
    o_d = _flash("mla", qm, 0, km, 0, vtm, 0, None, hm, dq ** -0.5)

    y = _mlp(x.reshape(B * S, D), o_c.reshape(B * S, wsb), o_d.reshape(B * S, hm * HEAD_DIM),
             w_out.astype(BF16), g_mlp, w_up.astype(BF16), w_down.astype(BF16), g_final,
             final_norm)
    return y.reshape(B, S, D)


def kernel(x, positions, norm_mix, norm_mlp, norm_final, w_in_ab, b_forget, rel_bias, w_out_ab,
           w_in_cd, q_norm, kv_norm, w_uq, w_ukv, w_out_cd, w_up, w_down):
    depth = norm_mix.shape[0]
    for layer in range(depth):
        last = layer == depth - 1
        if layer % 2 == 0:
            e = layer // 2
            x = _even_layer(x, norm_mix[layer], w_in_ab[e], b_forget[e], rel_bias[e], w_out_ab[e],
                            norm_mlp[layer], w_up[layer], w_down[layer], norm_final, last)
        else:
            o = layer // 2
            x = _odd_layer(x, positions, norm_mix[layer], w_in_cd[o], q_norm[o], kv_norm[o],
                           w_uq[o], w_ukv[o], w_out_cd[o], norm_mlp[layer], w_up[layer],
                           w_down[layer], norm_final, last)
    return x
```

```python
import functools
import math

import numpy as np
import jax
import jax.numpy as jnp
from jax import lax
from jax.experimental import pallas as pl
from jax.experimental.pallas import tpu as pltpu

F32 = jnp.float32
BF16 = jnp.bfloat16

EPS = 1e-6
HEAD_DIM = 64
CHUNK = 64
N_LEFT_CHUNKS = 8
REL_CLIP = 256
ROPE_DIM = 32
NOPE_DIM = 64
ROPE_THETA = 10000.0
Q_LORA = 384
KV_LORA = 256

LANES = 128
VT_BLK = LANES
SUB = LANES
FLASH_HP = 4
NEG = -1e30
SB_ZERO_LOG = -104.0
VMEM_LIMIT = 56 * 1024 * 1024

_NT = (((1,), (1,)), ((), ()))


def _cparams(sem, flags=None):
    return pltpu.CompilerParams(dimension_semantics=sem, vmem_limit_bytes=VMEM_LIMIT, flags=flags)


def _rms_bf16(x, g):
    ms = jnp.mean(x * x, axis=-1, keepdims=True)
    return (x * lax.rsqrt(ms + EPS) * g).astype(BF16)


def _store_vt(ovt_ref, vt, row0):
    rows, tm = vt.shape
    for c in range(tm // VT_BLK):
        ovt_ref[c, row0:row0 + rows, :] = vt[:, c * VT_BLK:(c + 1) * VT_BLK]


def _inproj_kernel(x_ref, g_ref, wm_ref, wvt_ref, wa_ref, om_ref, ovt_ref, oa_ref):
    h = _rms_bf16(x_ref[...], g_ref[...])
    nm = om_ref.shape[-1]
    for c in range(0, nm, 512):
        om_ref[:, c:c + 512] = jnp.dot(
            h, wm_ref[:, c:c + 512], preferred_element_type=F32).astype(BF16)
    nv = wvt_ref.shape[0]
    for r in range(0, nv, 256):
        vt = lax.dot_general(wvt_ref[r:r + 256, :], h, _NT,
                             preferred_element_type=F32).astype(BF16)
        _store_vt(ovt_ref, vt, r)
    oa_ref[...] = jnp.dot(h, wa_ref[...], preferred_element_type=F32)


def _inproj(x, g, wm, wvt, wa, tm=512):
    B, S, D = x.shape
    nm, nv, na = wm.shape[1], wvt.shape[0], wa.shape[1]
    return pl.pallas_call(
        _inproj_kernel,
        grid=(B, S // tm),
        in_specs=[
            pl.BlockSpec((None, tm, D), lambda b, i: (b, i, 0)),
            pl.BlockSpec((1, D), lambda b, i: (0, 0)),
            pl.BlockSpec((D, nm), lambda b, i: (0, 0)),
            pl.BlockSpec((nv, D), lambda b, i: (0, 0)),
            pl.BlockSpec((D, na), lambda b, i: (0, 0)),
        ],
        out_specs=[
            pl.BlockSpec((None, tm, nm), lambda b, i: (b, i, 0)),
            pl.BlockSpec((None, tm // VT_BLK, nv, VT_BLK), lambda b, i: (b, i, 0, 0)),
            pl.BlockSpec((None, tm, na), lambda b, i: (b, i, 0)),
        ],
        out_shape=[
            jax.ShapeDtypeStruct((B, S, nm), BF16),
            jax.ShapeDtypeStruct((B, S // VT_BLK, nv, VT_BLK), BF16),
            jax.ShapeDtypeStruct((B, S, na), F32),
        ],
        compiler_params=_cparams(("arbitrary", "arbitrary")),
        name="inproj",
    )(x, g.reshape(1, D), wm, wvt, wa)


def _split3(x):
    hi = x.astype(BF16)
    r = x - hi.astype(F32)
    mid = r.astype(BF16)
    lo = (r - mid.astype(F32)).astype(BF16)
    return hi, mid, lo


AUG_W = 8


def _logcum_kernel(fa_ref, b_ref, pq_ref, pk_ref, oneq_ref, onek_ref, oq_ref, ok_ref, carry_ref):
    @pl.when(pl.program_id(1) == 0)
    def _():
        carry_ref[...] = jnp.zeros_like(carry_ref)

    z = fa_ref[...] + b_ref[...]
    lf = jnp.minimum(z, 0.0) - jnp.log(1.0 + jnp.exp(-jnp.abs(z)))
    tc = lf.shape[0]
    r = lax.broadcasted_iota(jnp.int32, (tc, tc), 0)
    c = lax.broadcasted_iota(jnp.int32, (tc, tc), 1)
    tri = jnp.where(r >= c, 1.0, 0.0).astype(BF16)
    cs = carry_ref[...]
    for part in _split3(lf):
        cs = cs + jnp.dot(tri, part, preferred_element_type=F32)
    carry_ref[...] = cs[tc - 1:tc, :]
    qa, ka = oneq_ref[...], onek_ref[...]
    for n, part in enumerate(_split3(cs)):
        qa = qa + jnp.dot(part, pq_ref[n], preferred_element_type=F32)
        ka = ka + jnp.dot(part, pk_ref[n], preferred_element_type=F32)
    oq_ref[...] = qa.astype(BF16)
    ok_ref[...] = ka.astype(BF16)


def _logcum(fa, bias, n_heads, tc=512):
    B, S, W = fa.shape
    na = (n_heads // 2) * LANES
    pq = np.zeros((3, W, na), np.float32)
    pk = np.zeros((3, W, na), np.float32)
    oneq = np.zeros((1, na), np.float32)
    onek = np.zeros((1, na), np.float32)
    for h in range(n_heads):
        base = (h // 2) * LANES + (h % 2) * AUG_W
        for n in range(3):
            pq[n, h, base + n] = 1.0
            pk[n, h, base + 3 + n] = -1.0
        oneq[0, base + 3:base + 6] = 1.0
        onek[0, base:base + 3] = 1.0
    const = lambda a: pl.BlockSpec(a.shape, lambda b, i: (0,) * a.ndim)
    args = [jnp.asarray(pq, BF16), jnp.asarray(pk, BF16), jnp.asarray(oneq), jnp.asarray(onek)]
    return pl.pallas_call(
        _logcum_kernel,
        grid=(B, S // tc),
        in_specs=[pl.BlockSpec((None, tc, W), lambda b, i: (b, i, 0)),
                  pl.BlockSpec((1, W), lambda b, i: (0, 0))] + [const(a) for a in args],
        out_specs=[pl.BlockSpec((None, tc, na), lambda b, i: (b, i, 0))] * 2,
        out_shape=[jax.ShapeDtypeStruct((B, S, na), BF16)] * 2,
        scratch_shapes=[pltpu.VMEM((1, W), F32)],
        compiler_params=_cparams(("arbitrary", "arbitrary")),
        name="logcum",
    )(fa, bias, *args)


def _pair_mask_q(q2, j):
    lane = lax.broadcasted_iota(jnp.int32, q2.shape, 1)
    keep = (lane >= HEAD_DIM * j) & (lane < HEAD_DIM * (j + 1))
    return jnp.where(keep, q2, jnp.zeros_like(q2))


ONES_ROWS = 16
LOG2E = math.log2(math.e)


def _softmax_step(tiles, vts, carry, c):
    m, acc = carry
    m_new = m
    for tile in tiles:
        m_new = jnp.maximum(m_new, jnp.max(tile(), axis=0, keepdims=True))
    alpha = jnp.exp2((m - m_new) * c)
    mc = m_new * c
    pv = None
    for tile, vt in zip(tiles, vts):
        p = jnp.exp2(tile() * c - mc).astype(BF16)
        vt1 = jnp.concatenate([vt, jnp.ones((ONES_ROWS, vt.shape[1]), BF16)], axis=0)
        d = jnp.dot(vt1, p, preferred_element_type=F32)
        pv = d if pv is None else pv + d
    return m_new, alpha * acc + pv


def _softmax_init(bq):
    return (jnp.full((1, bq), NEG, F32), jnp.zeros((HEAD_DIM + ONES_ROWS, bq), F32))


def _softmax_out(carry):
    _, acc = carry
    return acc[0:HEAD_DIM] / acc[HEAD_DIM:HEAD_DIM + 1]


def _store_heads(o_ref, outs):
    oT = jnp.concatenate(outs, axis=0)
    o_ref[...] = oT.T.astype(o_ref.dtype)


def _flash_kernel(*refs, mode, hp, bq, scale):
    if mode == "fox":
        q_ref, k_ref, vt_ref, qaug_ref, kaug_ref, o_ref, sa_ref, sb_ref = refs
    else:
        q_ref, k_ref, vt_ref, o_ref, sa_ref, sb_ref = refs
    qs = pl.program_id(2) * bq
    sub = bq
    row = lax.broadcasted_iota(jnp.int32, (sub, bq), 0)
    col = lax.broadcasted_iota(jnp.int32, (sub, bq), 1)
    if mode == "fox":
        kcols = [slice(LANES * (h // 2), LANES * (h // 2 + 1)) for h in range(hp)]
        lane = lax.broadcasted_iota(jnp.int32, (bq, LANES), 1)
        qms = []
        for h in range(hp):
            qa = qaug_ref[:, kcols[h]]
            own = (lane >= AUG_W * (h % 2)) & (lane < AUG_W * (h % 2 + 1))
            qms.append(jnp.concatenate(
                [_pair_mask_q(q_ref[:, kcols[h]], h % 2),
                 jnp.where(own, qa, jnp.zeros_like(qa))], axis=1))
    else:
        kcols = [slice(LANES * h, LANES * (h + 1)) for h in range(hp)]
        qms = [q_ref[:, kcols[h]] for h in range(hp)]

    def scores(sb, h, masked):
        ks = pl.multiple_of(sb * sub, sub)
        k = k_ref[pl.ds(ks, sub), kcols[h]]
        if mode == "fox":
            k = jnp.concatenate([k, kaug_ref[pl.ds(ks, sub), kcols[h]]], axis=1)
        sT = lax.dot_general(k, qms[h], _NT, preferred_element_type=F32)
        if masked and mode == "fox":
            sT = jnp.where(ks + row <= qs + col, sT, NEG)
        elif masked:
            sT = jnp.where(((ks + row) >> 6) <= ((qs + col) >> 6), sT, NEG)
        return sT

    nsub = bq // sub
    nvt = sub // VT_BLK

    def produce(buf, sb0, masked, h):
        for c in range(nsub):
            buf[h, c] = scores(sb0 + c, h, masked)

    def consume(buf, sb0, carry, h):
        tiles = [lambda c=c: buf[h, c] for c in range(nsub)]
        vts = [jnp.concatenate([vt_ref[(sb0 + c) * nvt + v, HEAD_DIM * h:HEAD_DIM * (h + 1), :]
                                for v in range(nvt)], axis=1) for c in range(nsub)]
        return _softmax_step(tiles, vts, carry, scale * LOG2E)

    def stage(cur, cur_sb, nxt, nxt_sb, carries):
        if nxt is not None:
            for h in range(hp):
                produce(nxt, nxt_sb, False, h)
        return tuple(consume(cur, cur_sb, carries[h], h) for h in range(hp))

    n = pl.program_id(2)
    diag_sb = qs // sub
    for h in range(hp):
        produce(sa_ref, diag_sb, True, h)

    def pair(j, carries):
        carries = stage(sa_ref, jnp.where(j == 0, diag_sb, (2 * j - 1) * nsub),
                        sb_ref, 2 * j * nsub, carries)
        return stage(sb_ref, 2 * j * nsub,
                     sa_ref, jnp.minimum(2 * j + 1, n - 1) * nsub, carries)

    carries = tuple(_softmax_init(bq) for _ in range(hp))
    carries = lax.fori_loop(0, (n + 1) // 2, pair, carries)
    carries = lax.cond(
        n % 2 == 0,
        lambda c: stage(sa_ref, jnp.where(n == 0, diag_sb, (n - 1) * nsub), None, None, c),
        lambda c: c, carries)
    _store_heads(o_ref, [_softmax_out(c) for c in carries])


def _flash(mode, q_arr, q_col0, k_arr, k_col0, vt_arr, vt_row0, extra, n_heads,
           scale, hp=FLASH_HP, bq=256):
    B, S, _ = q_arr.shape
    qw = (HEAD_DIM if mode == "fox" else LANES) * hp
    vw = HEAD_DIM * hp
    in_specs = [
        pl.BlockSpec((None, bq, qw), lambda b, g, i: (b, i, q_col0 // qw + g)),
        pl.BlockSpec((None, S, qw), lambda b, g, i: (b, 0, k_col0 // qw + g)),
        pl.BlockSpec((None, S // VT_BLK, vw, VT_BLK),
                     lambda b, g, i: (b, 0, vt_row0 // vw + g, 0)),
    ]
    args = [q_arr, k_arr, vt_arr]
    if mode == "fox":
        q_aug, k_aug = extra
        in_specs += [
            pl.BlockSpec((None, bq, qw), lambda b, g, i: (b, i, g)),
            pl.BlockSpec((None, S, qw), lambda b, g, i: (b, 0, g)),
        ]
        args += [q_aug, k_aug]
    return pl.pallas_call(
        functools.partial(_flash_kernel, mode=mode, hp=hp, bq=bq, scale=scale),
        grid=(B, n_heads // hp, S // bq),
        in_specs=in_specs,
        out_specs=pl.BlockSpec((None, bq, vw), lambda b, g, i: (b, i, g)),
        out_shape=jax.ShapeDtypeStruct((B, S, n_heads * HEAD_DIM), BF16),
        scratch_shapes=[pltpu.VMEM((hp, 1, bq, bq), F32)] * 2,
        compiler_params=_cparams(("arbitrary", "arbitrary", "arbitrary")),
        name="flash_" + mode,
    )(*args)


CK_B = 2 * CHUNK
CK_NW = N_LEFT_CHUNKS * CHUNK // CK_B + 1
CK_EXT = (CK_NW + 1) * CK_B


def _chunk_kernel(q_ref, k_ref, vt_ref, ext_ref, o_ref, tab_ref, s_ref, *, hp):
    i = pl.program_id(1)

    @pl.when(i == 0)
    def _():
        jj = lax.broadcasted_iota(jnp.int32, (CK_B, CK_B), 0)
        rr = lax.broadcasted_iota(jnp.int32, (CK_B, CK_B), 1)
        for h in range(hp):
            for w in range(CK_NW):
                a = (CK_NW - 1 - w) * CK_B
                g = jnp.broadcast_to(ext_ref[h:h + 1, a:a + 2 * CK_B], (CK_B, 2 * CK_B))
                t = pltpu.roll(g, CK_B, 1, stride=1, stride_axis=0)[:, :CK_B]
                if w == 0:
                    t = jnp.where((rr >= CHUNK) & (jj < CHUNK), NEG, t)
                if w == CK_NW - 1:
                    t = jnp.where((rr < CHUNK) & (jj >= CHUNK), NEG, t)
                tab_ref[h, w * CK_B:(w + 1) * CK_B, :] = t

    kcols = [slice(LANES * (h // 2), LANES * (h // 2 + 1)) for h in range(hp)]
    first = i - (CK_NW - 1)

    def finish():
        kbc = [jnp.maximum(first + w, 0) for w in range(CK_NW)]
        outs = []
        for h in range(hp):
            vt = jnp.concatenate([vt_ref[kbc[w], HEAD_DIM * h:HEAD_DIM * (h + 1), :]
                                  for w in range(CK_NW)], axis=1)
            outs.append(_softmax_out(_softmax_step([lambda h=h: s_ref[h]], [vt],
                                                   _softmax_init(CK_B), LOG2E)))
        _store_heads(o_ref, outs)

    @pl.when(first >= 0)
    def _():
        ks = pl.multiple_of(first * CK_B, CK_B)
        for h in range(hp):
            s_ref[h] = lax.dot_general(k_ref[pl.ds(ks, CK_NW * CK_B), kcols[h]],
                                       _pair_mask_q(q_ref[:, kcols[h]], h % 2), _NT,
                                       preferred_element_type=F32) + tab_ref[h]
        finish()

    @pl.when(first < 0)
    def _():
        for h in range(hp):
            qm = _pair_mask_q(q_ref[:, kcols[h]], h % 2)
            for w in range(CK_NW):
                rows = slice(w * CK_B, (w + 1) * CK_B)
                ks = pl.multiple_of(jnp.maximum(first + w, 0) * CK_B, CK_B)
                sT = lax.dot_general(k_ref[pl.ds(ks, CK_B), kcols[h]], qm, _NT,
                                     preferred_element_type=F32) + tab_ref[h, rows, :]
                s_ref[h, rows, :] = jnp.where(first + w >= 0, sT, NEG)
        finish()


def _chunk_attn(main, q_col0, k_col0, vt_arr, vt_row0, ext, n_heads):
    B, S, _ = main.shape
    hp = n_heads
    qw, vw = HEAD_DIM * hp, HEAD_DIM * hp
    return pl.pallas_call(
        functools.partial(_chunk_kernel, hp=hp),
        grid=(B, S // CK_B),
        in_specs=[
            pl.BlockSpec((None, CK_B, qw), lambda b, i: (b, i, q_col0 // qw)),
            pl.BlockSpec((None, S, qw), lambda b, i: (b, 0, k_col0 // qw)),
            pl.BlockSpec((None, S // VT_BLK, vw, VT_BLK), lambda b, i: (b, 0, vt_row0 // vw, 0)),
            pl.BlockSpec((hp, CK_EXT), lambda b, i: (0, 0)),
        ],
        out_specs=pl.BlockSpec((None, CK_B, vw), lambda b, i: (b, i, 0)),
        out_shape=jax.ShapeDtypeStruct((B, S, n_heads * HEAD_DIM), BF16),
        scratch_shapes=[pltpu.VMEM((hp, CK_NW * CK_B, CK_B), F32)] * 2,
        compiler_params=_cparams(("arbitrary", "arbitrary")),
        name="chunk_attn",
    )(main, main, vt_arr, ext)


def _sb_kernel(q_ref, k_ref, vt_ref, o_ref, z_ref, lb_ref, sfx_ref, *, hp, bq):
    qs = pl.program_id(1) * bq
    nsub = bq // SUB
    row = lax.broadcasted_iota(jnp.int32, (SUB, bq), 0)
    col = lax.broadcasted_iota(jnp.int32, (SUB, bq), 1)
    ur = lax.broadcasted_iota(jnp.int32, (SUB, 2 * SUB), 0)
    uc = lax.broadcasted_iota(jnp.int32, (SUB, 2 * SUB), 1) & (SUB - 1)
    upper2 = jnp.where(uc > ur, 1.0, 0.0).astype(BF16)
    kcols = [slice(LANES * (h // 2), LANES * (h // 2 + 1)) for h in range(hp)]
    qms = [_pair_mask_q(q_ref[:, kcols[h]], h % 2) for h in range(hp)]

    def step(kb, carries, masked):
        ks = pl.multiple_of(kb * bq, bq)
        for h in range(hp):
            z_ref[h] = lax.dot_general(k_ref[pl.ds(ks, bq), kcols[h]], qms[h], _NT,
                                       preferred_element_type=F32)
        totals = []
        for h in range(hp):
            tot = []
            for c in range(nsub):
                rows = slice(c * SUB, (c + 1) * SUB)
                z = z_ref[h, rows, :]
                l1 = jnp.log(1.0 + jnp.exp(-jnp.abs(z)))
                log_beta = jnp.minimum(z, 0.0) - l1
                log_keep = log_beta - z
                if masked:
                    log_keep = jnp.where(ks + c * SUB + row < qs + col, log_keep, 0.0)
                lb_ref[h, rows, :] = log_beta
                hi = log_keep.astype(BF16)
                lo = (log_keep - hi.astype(F32)).astype(BF16)
                sfx = jnp.dot(upper2, jnp.concatenate([hi, lo], axis=0),
                              preferred_element_type=F32)
                sfx_ref[h, rows, :] = sfx
                tot.append(sfx[0:1, :] + log_keep[0:1, :])
            totals.append(tot)
        out = []
        for h in range(hp):
            tail, acc = carries[h]
            parts = [None] * nsub
            for c in range(nsub - 1, -1, -1):
                rows = slice(c * SUB, (c + 1) * SUB)
                a = jnp.exp(lb_ref[h, rows, :] + sfx_ref[h, rows, :] + tail)
                if masked:
                    a = jnp.where(ks + c * SUB + row < qs + col, a, 0.0)
                parts[c] = a.astype(BF16)
                tail = tail + totals[h][c]
            vt = jnp.concatenate([vt_ref[kb * nsub + c, HEAD_DIM * h:HEAD_DIM * (h + 1), :]
                                  for c in range(nsub)], axis=1)
            acc = acc + jnp.dot(vt, jnp.concatenate(parts, axis=0), preferred_element_type=F32)
            out.append((tail, acc))
        return tuple(out)

    n_full = qs // bq
    carries = tuple((jnp.zeros((1, bq), F32), jnp.zeros((HEAD_DIM, bq), F32))
                    for _ in range(hp))
    carries = step(n_full, carries, True)

    def cond(state):
        kb, carries = state
        tail_max = carries[0][0]
        for h in range(1, hp):
            tail_max = jnp.maximum(tail_max, carries[h][0])
        return (kb >= 0) & (jnp.max(tail_max) > SB_ZERO_LOG)

    def body(state):
        kb, carries = state
        return kb - 1, step(kb, carries, False)

    _, carries = lax.while_loop(cond, body, (n_full - 1, carries))
    _store_heads(o_ref, [acc for (_, acc) in carries])


def _sb_attn(main, q_col0, k_col0, vt_arr, vt_row0, n_heads, bq=256):
    B, S, _ = main.shape
    hp = n_heads
    qw = HEAD_DIM * hp
    return pl.pallas_call(
        functools.partial(_sb_kernel, hp=hp, bq=bq),
        grid=(B, S // bq),
        in_specs=[
            pl.BlockSpec((None, bq, qw), lambda b, i: (b, i, q_col0 // qw)),
            pl.BlockSpec((None, S, qw), lambda b, i: (b, 0, k_col0 // qw)),
            pl.BlockSpec((None, S // VT_BLK, qw, VT_BLK), lambda b, i: (b, 0, vt_row0 // qw, 0)),
        ],
        out_specs=pl.BlockSpec((None, bq, qw), lambda b, i: (b, i, 0)),
        out_shape=jax.ShapeDtypeStruct((B, S, n_heads * HEAD_DIM), BF16),
        scratch_shapes=[pltpu.VMEM((hp, bq, bq), F32)] * 3,
        compiler_params=_cparams(("arbitrary", "arbitrary")),
        name="sb_attn",
    )(main, main, vt_arr)


def _mla_prep_kernel(aux_ref, pos_ref, invf_ref, qn_ref, kvn_ref, wuq_ref, wuqr_ref,
                     wk_ref, wvt_ref, oq_ref, ok_ref, ovt_ref):
    ang = pos_ref[...] * invf_ref[...]
    cos, sin = jnp.cos(ang), jnp.sin(ang)
    cq = _rms_bf16(aux_ref[:, 0:Q_LORA], qn_ref[...])
    ckv = _rms_bf16(aux_ref[:, Q_LORA:Q_LORA + KV_LORA], kvn_ref[...])
    o = Q_LORA + KV_LORA
    k_rope = aux_ref[:, o:o + LANES] * cos + aux_ref[:, o + LANES:o + 2 * LANES] * sin
    n_heads = oq_ref.shape[-1] // LANES
    for h in range(0, n_heads, 2):
        cols = slice(h * LANES, (h + 2) * LANES)
        qa = jnp.dot(cq, wuq_ref[:, cols], preferred_element_type=F32)
        qb = jnp.dot(cq, wuqr_ref[:, cols], preferred_element_type=F32)
        kn = jnp.dot(ckv, wk_ref[:, cols], preferred_element_type=F32)
        for d in range(2):
            c1 = slice(d * LANES, (d + 1) * LANES)
            c2 = slice((h + d) * LANES, (h + d + 1) * LANES)
            oq_ref[:, c2] = (qa[:, c1] * cos + qb[:, c1] * sin).astype(BF16)
            ok_ref[:, c2] = (kn[:, c1] + k_rope).astype(BF16)
    nv = wvt_ref.shape[0]
    for r in range(0, nv, 256):
        vt = lax.dot_general(wvt_ref[r:r + 256, :], ckv, _NT,
                             preferred_element_type=F32).astype(BF16)
        _store_vt(ovt_ref, vt, r)


def _mla_prep(aux, pos, invf, qn, kvn, wuq, wuqr, wk, wvt, tm=512):
    B, S, na = aux.shape
    nq, nv = wuq.shape[1], wvt.shape[0]
    full = lambda a: pl.BlockSpec(a.shape, lambda b, i: (0,) * a.ndim)
    return pl.pallas_call(
        _mla_prep_kernel,
        grid=(B, S // tm),
        in_specs=[
            pl.BlockSpec((None, tm, na), lambda b, i: (b, i, 0)),
            pl.BlockSpec((None, tm, 1), lambda b, i: (b, i, 0)),
            full(invf), full(qn), full(kvn), full(wuq), full(wuqr), full(wk), full(wvt),
        ],
        out_specs=[
            pl.BlockSpec((None, tm, nq), lambda b, i: (b, i, 0)),
            pl.BlockSpec((None, tm, nq), lambda b, i: (b, i, 0)),
            pl.BlockSpec((None, tm // VT_BLK, nv, VT_BLK), lambda b, i: (b, i, 0, 0)),
        ],
        out_shape=[
            jax.ShapeDtypeStruct((B, S, nq), BF16),
            jax.ShapeDtypeStruct((B, S, nq), BF16),
            jax.ShapeDtypeStruct((B, S // VT_BLK, nv, VT_BLK), BF16),
        ],
        compiler_params=_cparams(("arbitrary", "arbitrary")),
        name="mla_prep",
    )(aux, pos, invf, qn, kvn, wuq, wuqr, wk, wvt)


def _mlp_kernel(x_ref, oa_ref, ob_ref, wo_ref, g_ref, wu_ref, wd_ref, gf_ref, out_ref,
                x1_ref, h_ref, acc_ref, *, final_norm):
    f = pl.program_id(1)

    @pl.when(f == 0)
    def _():
        na = oa_ref.shape[-1]
        x1 = (x_ref[...]
              + jnp.dot(oa_ref[...], wo_ref[0:na, :], preferred_element_type=F32)
              + jnp.dot(ob_ref[...], wo_ref[na:, :], preferred_element_type=F32))
        x1_ref[...] = x1
        h_ref[...] = _rms_bf16(x1, g_ref[...])
        acc_ref[...] = jnp.zeros_like(acc_ref)

    u = jnp.dot(h_ref[...], wu_ref[...], preferred_element_type=F32)
    a = jnp.square(jnp.maximum(u, 0.0)).astype(BF16)
    acc_ref[...] += jnp.dot(a, wd_ref[...], preferred_element_type=F32)

    @pl.when(f == pl.num_programs(1) - 1)
    def _():
        y = x1_ref[...] + acc_ref[...]
        if final_norm:
            ms = jnp.mean(y * y, axis=-1, keepdims=True)
            y = y * lax.rsqrt(ms + EPS) * gf_ref[...]
        out_ref[...] = y


def _mlp(x2, oa, ob, wo, g, wu, wd, gf, final_norm, tm=512, tf=1024):
    T, D = x2.shape
    F = wu.shape[1]
    na, nb = oa.shape[1], ob.shape[1]
    return pl.pallas_call(
        functools.partial(_mlp_kernel, final_norm=final_norm),
        grid=(T // tm, F // tf),
        in_specs=[
            pl.BlockSpec((tm, D), lambda i, f: (i, 0)),
            pl.BlockSpec((tm, na), lambda i, f: (i, 0)),
            pl.BlockSpec((tm, nb), lambda i, f: (i, 0)),
            pl.BlockSpec((na + nb, D), lambda i, f: (0, 0)),
            pl.BlockSpec((1, D), lambda i, f: (0, 0)),
            pl.BlockSpec((D, tf), lambda i, f: (0, f)),
            pl.BlockSpec((tf, D), lambda i, f: (f, 0)),
            pl.BlockSpec((1, D), lambda i, f: (0, 0)),
        ],
        out_specs=pl.BlockSpec((tm, D), lambda i, f: (i, 0)),
        out_shape=jax.ShapeDtypeStruct((T, D), F32),
        scratch_shapes=[pltpu.VMEM((tm, D), F32), pltpu.VMEM((tm, D), BF16),
                        pltpu.VMEM((tm, D), F32)],
        compiler_params=_cparams(("arbitrary", "arbitrary")),
        name="mlp",
    )(x2, oa, ob, wo, g.reshape(1, D), wu, wd, gf.reshape(1, D))


def _pad_cols(w, n):
    return jnp.pad(w, ((0, 0), (0, n - w.shape[1])))


def _rot_cols(w):
    half = ROPE_DIM // 2
    return jnp.concatenate([-w[:, half:], w[:, :half]], axis=1)


def _rope_slab(w):
    z = jnp.zeros((w.shape[0], NOPE_DIM), w.dtype)
    return jnp.concatenate([z, w, jnp.zeros((w.shape[0], LANES - NOPE_DIM - ROPE_DIM), w.dtype)], axis=1)


def _even_layer(x, g_mix, w_in, b_forget, rel_bias, w_out, g_mlp, w_up, w_down, g_final,
                final_norm):
    B, S, D = x.shape
    hf, hc = b_forget.shape[0], rel_bias.shape[0]
    wf, wc = hf * HEAD_DIM, hc * HEAD_DIM
    o = np.cumsum([0, wf, wf, wf, hf, wc, wc, wc])
    qa, ka, va, fa, qb, kb, vb = [w_in[:, o[n]:o[n + 1]] for n in range(7)]
    inv_sqrt_d = HEAD_DIM ** -0.5
    wm = jnp.concatenate([qa * inv_sqrt_d, ka, qb * inv_sqrt_d, kb], axis=1).astype(BF16)
    wvt = jnp.concatenate([va, vb], axis=1).T.astype(BF16)
    wa = _pad_cols(fa, LANES).astype(BF16)
    main, vt, aux = _inproj(x, g_mix, wm, wvt, wa)

    q_aug, k_aug = _logcum(aux, _pad_cols(b_forget.reshape(1, hf), LANES), hf)
    o_a = _flash("fox", main, 0, main, wf, vt, 0, (q_aug, k_aug), hf, 1.0)

    right = CK_EXT - rel_bias.shape[1] - (CHUNK + 1)
    ext = jnp.pad(rel_bias, ((0, 0), (CHUNK + 1, right)), mode="edge")
    o_b = _chunk_attn(main, 2 * wf, 2 * wf + wc, vt, wf, ext, hc)

    y = _mlp(x.reshape(B * S, D), o_a.reshape(B * S, wf), o_b.reshape(B * S, wc),
             w_out.astype(BF16), g_mlp, w_up.astype(BF16), w_down.astype(BF16), g_final,
             final_norm)
    return y.reshape(B, S, D)


def _odd_layer(x, positions, g_mix, w_in, q_norm, kv_norm, w_uq, w_ukv, w_out, g_mlp, w_up,
               w_down, g_final, final_norm):
    B, S, D = x.shape
    hm = w_ukv.shape[1] // (NOPE_DIM + HEAD_DIM)
    ws = w_in.shape[1] - Q_LORA - KV_LORA - ROPE_DIM
    hs = (ws // 3) // HEAD_DIM
    wsb = hs * HEAD_DIM
    o = np.cumsum([0, wsb, wsb, wsb, Q_LORA, KV_LORA, ROPE_DIM])
    qc, kc, vc, w_cq, w_ckv, w_kr = [w_in[:, o[n]:o[n + 1]] for n in range(6)]
    wm = jnp.concatenate([qc * HEAD_DIM ** -0.5, kc], axis=1).astype(BF16)
    wa = jnp.concatenate([w_cq, w_ckv, _rope_slab(w_kr), _rope_slab(_rot_cols(w_kr))],
                         axis=1).astype(BF16)
    main, vt, aux = _inproj(x, g_mix, wm, vc.T.astype(BF16), wa)
    o_c = _sb_attn(main, 0, wsb, vt, 0, hs)

    dq = NOPE_DIM + ROPE_DIM
    wuq3 = w_uq.reshape(Q_LORA, hm, dq)
    nope, ropew = wuq3[:, :, :NOPE_DIM], wuq3[:, :, NOPE_DIM:]
    zq = jnp.zeros((Q_LORA, hm, LANES - dq), w_uq.dtype)
    wuq = jnp.concatenate([nope, ropew, zq], axis=2).reshape(Q_LORA, hm * LANES).astype(BF16)
    half = ROPE_DIM // 2
    ropr = jnp.concatenate([-ropew[:, :, half:], ropew[:, :, :half]], axis=2)
    wuqr = jnp.concatenate([jnp.zeros_like(nope), ropr, zq], axis=2)
    wuqr = wuqr.reshape(Q_LORA, hm * LANES).astype(BF16)
    wkv3 = w_ukv.reshape(KV_LORA, hm, NOPE_DIM + HEAD_DIM)
    wk = jnp.concatenate([wkv3[:, :, :NOPE_DIM],
                          jnp.zeros((KV_LORA, hm, LANES - NOPE_DIM), w_ukv.dtype)], axis=2)
    wk = wk.reshape(KV_LORA, hm * LANES).astype(BF16)
    wv_t = wkv3[:, :, NOPE_DIM:].reshape(KV_LORA, hm * HEAD_DIM).T.astype(BF16)
    freqs = (ROPE_THETA ** (-jnp.arange(half, dtype=F32) / half))
    invf = jnp.zeros((1, LANES), F32)
    invf = invf.at[0, NOPE_DIM:NOPE_DIM + half].set(freqs)
    invf = invf.at[0, NOPE_DIM + half:NOPE_DIM + ROPE_DIM].set(freqs)
    pos = positions.astype(F32).reshape(B, S, 1)
    qm, km, vtm = _mla_prep(aux, pos, invf, q_norm.reshape(1, Q_LORA),
                            kv_norm.reshape(1, KV_LORA), wuq, wuqr, wk, wv_t)
    o_d = _flash("mla", qm, 0, km, 0, vtm, 0, None, hm, dq ** -0.5)

    y = _mlp(x.reshape(B * S, D), o_c.reshape(B * S, wsb), o_d.reshape(B * S, hm * HEAD_DIM),
             w_out.astype(BF16), g_mlp, w_up.astype(BF16), w_down.astype(BF16), g_final,
             final_norm)
    return y.reshape(B, S, D)


def kernel(x, positions, norm_mix, norm_mlp, norm_final, w_in_ab, b_forget, rel_bias, w_out_ab,
           w_in_cd, q_norm, kv_norm, w_uq, w_ukv, w_out_cd, w_up, w_down):
    depth = norm_mix.shape[0]
    for layer in range(depth):
        last = layer == depth - 1
        if layer % 2 == 0:
            e = layer // 2
            x = _even_layer(x, norm_mix[layer], w_in_ab[e], b_forget[e], rel_bias[e], w_out_ab[e],
                            norm_mlp[layer], w_up[layer], w_down[layer], norm_final, last)
        else:
            o = layer // 2
            x = _odd_layer(x, positions, norm_mix[layer], w_in_cd[o], q_norm[o], kv_norm[o],
                           w_uq[o], w_ukv[o], w_out_cd[o], norm_mlp[layer], w_up[layer],
                           w_down[layer], norm_final, last)
    return x
```

```python
import functools
import math

import numpy as np
import jax
import jax.numpy as jnp
from jax import lax
from jax.experimental import pallas as pl
from jax.experimental.pallas import tpu as pltpu

F32 = jnp.float32
BF16 = jnp.bfloat16

EPS = 1e-6
HEAD_DIM = 64
CHUNK = 64
N_LEFT_CHUNKS = 8
REL_CLIP = 256
ROPE_DIM = 32
NOPE_DIM = 64
ROPE_THETA = 10000.0
Q_LORA = 384
KV_LORA = 256

LANES = 128
VT_BLK = LANES
SUB = LANES
FLASH_HP = 4
NEG = -1e30
SB_ZERO_LOG = -104.0
VMEM_LIMIT = 56 * 1024 * 1024

_NT = (((1,), (1,)), ((), ()))


def _cparams(sem, flags=None):
    return pltpu.CompilerParams(dimension_semantics=sem, vmem_limit_bytes=VMEM_LIMIT, flags=flags)


def _rms_bf16(x, g):
    ms = jnp.mean(x * x, axis=-1, keepdims=True)
    return (x * lax.rsqrt(ms + EPS) * g).astype(BF16)


def _store_vt(ovt_ref, vt, row0):
    rows, tm = vt.shape
    for c in range(tm // VT_BLK):
        ovt_ref[c, row0:row0 + rows, :] = vt[:, c * VT_BLK:(c + 1) * VT_BLK]


def _inproj_kernel(x_ref, g_ref, wm_ref, wvt_ref, wa_ref, om_ref, ovt_ref, oa_ref):
    h = _rms_bf16(x_ref[...], g_ref[...])
    nm = om_ref.shape[-1]
    for c in range(0, nm, 512):
        om_ref[:, c:c + 512] = jnp.dot(
            h, wm_ref[:, c:c + 512], preferred_element_type=F32).astype(BF16)
    nv = wvt_ref.shape[0]
    for r in range(0, nv, 256):
        vt = lax.dot_general(wvt_ref[r:r + 256, :], h, _NT,
                             preferred_element_type=F32).astype(BF16)
        _store_vt(ovt_ref, vt, r)
    oa_ref[...] = jnp.dot(h, wa_ref[...], preferred_element_type=F32)


def _inproj(x, g, wm, wvt, wa, tm=512):
    B, S, D = x.shape
    nm, nv, na = wm.shape[1], wvt.shape[0], wa.shape[1]
    return pl.pallas_call(
        _inproj_kernel,
        grid=(B, S // tm),
        in_specs=[
            pl.BlockSpec((None, tm, D), lambda b, i: (b, i, 0)),
            pl.BlockSpec((1, D), lambda b, i: (0, 0)),
            pl.BlockSpec((D, nm), lambda b, i: (0, 0)),
            pl.BlockSpec((nv, D), lambda b, i: (0, 0)),
            pl.BlockSpec((D, na), lambda b, i: (0, 0)),
        ],
        out_specs=[
            pl.BlockSpec((None, tm, nm), lambda b, i: (b, i, 0)),
            pl.BlockSpec((None, tm // VT_BLK, nv, VT_BLK), lambda b, i: (b, i, 0, 0)),
            pl.BlockSpec((None, tm, na), lambda b, i: (b, i, 0)),
        ],
        out_shape=[
            jax.ShapeDtypeStruct((B, S, nm), BF16),
            jax.ShapeDtypeStruct((B, S // VT_BLK, nv, VT_BLK), BF16),
            jax.ShapeDtypeStruct((B, S, na), F32),
        ],
        compiler_params=_cparams(("arbitrary", "arbitrary")),
        name="inproj",
    )(x, g.reshape(1, D), wm, wvt, wa)


def _split3(x):
    hi = x.astype(BF16)
    r = x - hi.astype(F32)
    mid = r.astype(BF16)
    lo = (r - mid.astype(F32)).astype(BF16)
    return hi, mid, lo


AUG_W = 8


def _logcum_kernel(fa_ref, b_ref, pq_ref, pk_ref, oneq_ref, onek_ref, oq_ref, ok_ref, carry_ref):
    @pl.when(pl.program_id(1) == 0)
    def _():
        carry_ref[...] = jnp.zeros_like(carry_ref)

    z = fa_ref[...] + b_ref[...]
    lf = jnp.minimum(z, 0.0) - jnp.log(1.0 + jnp.exp(-jnp.abs(z)))
    tc = lf.shape[0]
    r = lax.broadcasted_iota(jnp.int32, (tc, tc), 0)
    c = lax.broadcasted_iota(jnp.int32, (tc, tc), 1)
    tri = jnp.where(r >= c, 1.0, 0.0).astype(BF16)
    cs = carry_ref[...]
    for part in _split3(lf):
        cs = cs + jnp.dot(tri, part, preferred_element_type=F32)
    carry_ref[...] = cs[tc - 1:tc, :]
    qa, ka = oneq_ref[...], onek_ref[...]
    for n, part in enumerate(_split3(cs)):
        qa = qa + jnp.dot(part, pq_ref[n], preferred_element_type=F32)
        ka = ka + jnp.dot(part, pk_ref[n], preferred_element_type=F32)
    oq_ref[...] = qa.astype(BF16)
    ok_ref[...] = ka.astype(BF16)


def _logcum(fa, bias, n_heads, tc=512):
    B, S, W = fa.shape
    na = LANES
    pq = np.zeros((3, W, na), np.float32)
    pk = np.zeros((3, W, na), np.float32)
    oneq = np.zeros((1, na), np.float32)
    onek = np.zeros((1, na), np.float32)
    for h in range(n_heads):
        base = h * AUG_W
        for n in range(3):
            pq[n, h, base + n] = 1.0
            pk[n, h, base + 3 + n] = -1.0
        oneq[0, base + 3:base + 6] = 1.0
        onek[0, base:base + 3] = 1.0
    const = lambda a: pl.BlockSpec(a.shape, lambda b, i: (0,) * a.ndim)
    args = [jnp.asarray(pq, BF16), jnp.asarray(pk, BF16), jnp.asarray(oneq), jnp.asarray(onek)]
    return pl.pallas_call(
        _logcum_kernel,
        grid=(B, S // tc),
        in_specs=[pl.BlockSpec((None, tc, W), lambda b, i: (b, i, 0)),
                  pl.BlockSpec((1, W), lambda b, i: (0, 0))] + [const(a) for a in args],
        out_specs=[pl.BlockSpec((None, tc, na), lambda b, i: (b, i, 0))] * 2,
        out_shape=[jax.ShapeDtypeStruct((B, S, na), BF16)] * 2,
        scratch_shapes=[pltpu.VMEM((1, W), F32)],
        compiler_params=_cparams(("arbitrary", "arbitrary")),
        name="logcum",
    )(fa, bias, *args)


def _pair_mask_q(q2, j):
    lane = lax.broadcasted_iota(jnp.int32, q2.shape, 1)
    keep = (lane >= HEAD_DIM * j) & (lane < HEAD_DIM * (j + 1))
    return jnp.where(keep, q2, jnp.zeros_like(q2))


ONES_ROWS = 16
LOG2E = math.log2(math.e)


def _softmax_step(tiles, vts, carry, c):
    m, acc = carry
    m_new = m
    for tile in tiles:
        m_new = jnp.maximum(m_new, jnp.max(tile(), axis=0, keepdims=True))
    alpha = jnp.exp2((m - m_new) * c)
    mc = m_new * c
    pv = None
    for tile, vt in zip(tiles, vts):
        p = jnp.exp2(tile() * c - mc).astype(BF16)
        vt1 = jnp.concatenate([vt, jnp.ones((ONES_ROWS, vt.shape[1]), BF16)], axis=0)
        d = jnp.dot(vt1, p, preferred_element_type=F32)
        pv = d if pv is None else pv + d
    return m_new, alpha * acc + pv


def _softmax_init(bq):
    return (jnp.full((1, bq), NEG, F32), jnp.zeros((HEAD_DIM + ONES_ROWS, bq), F32))


def _softmax_out(carry):
    _, acc = carry
    return acc[0:HEAD_DIM] / acc[HEAD_DIM:HEAD_DIM + 1]


def _store_heads(o_ref, outs):
    oT = jnp.concatenate(outs, axis=0)
    o_ref[...] = oT.T.astype(o_ref.dtype)


def _flash_kernel(*refs, mode, hp, bq, scale):
    if mode == "fox":
        q_ref, k_ref, vt_ref, qaug_ref, kaug_ref, o_ref, sa_ref, sb_ref = refs
    else:
        q_ref, k_ref, vt_ref, o_ref, sa_ref, sb_ref = refs
    qs = pl.program_id(2) * bq
    sub = bq
    row = lax.broadcasted_iota(jnp.int32, (sub, bq), 0)
    col = lax.broadcasted_iota(jnp.int32, (sub, bq), 1)
    if mode == "fox":
        kcols = [slice(LANES * (h // 2), LANES * (h // 2 + 1)) for h in range(hp)]
        lane = lax.broadcasted_iota(jnp.int32, (bq, LANES), 1)
        qa = qaug_ref[...]
        qms = []
        for h in range(hp):
            first = AUG_W * (pl.program_id(1) * hp + h)
            own = (lane >= first) & (lane < first + AUG_W)
            qms.append(jnp.concatenate(
                [_pair_mask_q(q_ref[:, kcols[h]], h % 2),
                 jnp.where(own, qa, jnp.zeros_like(qa))], axis=1))
    else:
        kcols = [slice(LANES * h, LANES * (h + 1)) for h in range(hp)]
        qms = [q_ref[:, kcols[h]] for h in range(hp)]

    def scores(sb, h, masked):
        ks = pl.multiple_of(sb * sub, sub)
        k = k_ref[pl.ds(ks, sub), kcols[h]]
        if mode == "fox":
            k = jnp.concatenate([k, kaug_ref[pl.ds(ks, sub), :]], axis=1)
        sT = lax.dot_general(k, qms[h], _NT, preferred_element_type=F32)
        if masked and mode == "fox":
            sT = jnp.where(ks + row <= qs + col, sT, NEG)
        elif masked:
            sT = jnp.where(((ks + row) >> 6) <= ((qs + col) >> 6), sT, NEG)
        return sT

    nsub = bq // sub
    nvt = sub // VT_BLK

    def produce(buf, sb0, masked, h):
        for c in range(nsub):
            buf[h, c] = scores(sb0 + c, h, masked)

    def consume(buf, sb0, carry, h):
        tiles = [lambda c=c: buf[h, c] for c in range(nsub)]
        vts = [jnp.concatenate([vt_ref[(sb0 + c) * nvt + v, HEAD_DIM * h:HEAD_DIM * (h + 1), :]
                                for v in range(nvt)], axis=1) for c in range(nsub)]
        return _softmax_step(tiles, vts, carry, scale * LOG2E)

    def stage(cur, cur_sb, nxt, nxt_sb, carries):
        if nxt is not None:
            for h in range(hp):
                produce(nxt, nxt_sb, False, h)
        return tuple(consume(cur, cur_sb, carries[h], h) for h in range(hp))

    n = pl.program_id(2)
    diag_sb = qs // sub
    for h in range(hp):
        produce(sa_ref, diag_sb, True, h)

    def pair(j, carries):
        carries = stage(sa_ref, jnp.where(j == 0, diag_sb, (2 * j - 1) * nsub),
                        sb_ref, 2 * j * nsub, carries)
        return stage(sb_ref, 2 * j * nsub,
                     sa_ref, jnp.minimum(2 * j + 1, n - 1) * nsub, carries)

    carries = tuple(_softmax_init(bq) for _ in range(hp))
    carries = lax.fori_loop(0, (n + 1) // 2, pair, carries)
    carries = lax.cond(
        n % 2 == 0,
        lambda c: stage(sa_ref, jnp.where(n == 0, diag_sb, (n - 1) * nsub), None, None, c),
        lambda c: c, carries)
    _store_heads(o_ref, [_softmax_out(c) for c in carries])


def _flash(mode, q_arr, q_col0, k_arr, k_col0, vt_arr, vt_row0, extra, n_heads,
           scale, hp=FLASH_HP, bq=256):
    B, S, _ = q_arr.shape
    qw = (HEAD_DIM if mode == "fox" else LANES) * hp
    vw = HEAD_DIM * hp
    in_specs = [
        pl.BlockSpec((None, bq, qw), lambda b, g, i: (b, i, q_col0 // qw + g)),
        pl.BlockSpec((None, S, qw), lambda b, g, i: (b, 0, k_col0 // qw + g)),
        pl.BlockSpec((None, S // VT_BLK, vw, VT_BLK),
                     lambda b, g, i: (b, 0, vt_row0 // vw + g, 0)),
    ]
    args = [q_arr, k_arr, vt_arr]
    if mode == "fox":
        q_aug, k_aug = extra
        in_specs += [
            pl.BlockSpec((None, bq, LANES), lambda b, g, i: (b, i, 0)),
            pl.BlockSpec((None, S, LANES), lambda b, g, i: (b, 0, 0)),
        ]
        args += [q_aug, k_aug]
    return pl.pallas_call(
        functools.partial(_flash_kernel, mode=mode, hp=hp, bq=bq, scale=scale),
        grid=(B, n_heads // hp, S // bq),
        in_specs=in_specs,
        out_specs=pl.BlockSpec((None, bq, vw), lambda b, g, i: (b, i, g)),
        out_shape=jax.ShapeDtypeStruct((B, S, n_heads * HEAD_DIM), BF16),
        scratch_shapes=[pltpu.VMEM((hp, 1, bq, bq), F32)] * 2,
        compiler_params=_cparams(("arbitrary", "arbitrary", "arbitrary")),
        name="flash_" + mode,
    )(*args)


CK_B = 2 * CHUNK
CK_NW = N_LEFT_CHUNKS * CHUNK // CK_B + 1
CK_EXT = (CK_NW + 1) * CK_B


def _chunk_kernel(q_ref, k_ref, vt_ref, ext_ref, o_ref, tab_ref, s_ref, *, hp):
    i = pl.program_id(1)

    @pl.when(i == 0)
    def _():
        jj = lax.broadcasted_iota(jnp.int32, (CK_B, CK_B), 0)
        rr = lax.broadcasted_iota(jnp.int32, (CK_B, CK_B), 1)
        for h in range(hp):
            for w in range(CK_NW):
                a = (CK_NW - 1 - w) * CK_B
                g = jnp.broadcast_to(ext_ref[h:h + 1, a:a + 2 * CK_B], (CK_B, 2 * CK_B))
                t = pltpu.roll(g, CK_B, 1, stride=1, stride_axis=0)[:, :CK_B]
                if w == 0:
                    t = jnp.where((rr >= CHUNK) & (jj < CHUNK), NEG, t)
                if w == CK_NW - 1:
                    t = jnp.where((rr < CHUNK) & (jj >= CHUNK), NEG, t)
                tab_ref[h, w * CK_B:(w + 1) * CK_B, :] = t

    kcols = [slice(LANES * (h // 2), LANES * (h // 2 + 1)) for h in range(hp)]
    first = i - (CK_NW - 1)

    def finish():
        kbc = [jnp.maximum(first + w, 0) for w in range(CK_NW)]
        outs = []
        for h in range(hp):
            vt = jnp.concatenate([vt_ref[kbc[w], HEAD_DIM * h:HEAD_DIM * (h + 1), :]
                                  for w in range(CK_NW)], axis=1)
            outs.append(_softmax_out(_softmax_step([lambda h=h: s_ref[h]], [vt],
                                                   _softmax_init(CK_B), LOG2E)))
        _store_heads(o_ref, outs)

    @pl.when(first >= 0)
    def _():
        ks = pl.multiple_of(first * CK_B, CK_B)
        for h in range(hp):
            s_ref[h] = lax.dot_general(k_ref[pl.ds(ks, CK_NW * CK_B), kcols[h]],
                                       _pair_mask_q(q_ref[:, kcols[h]], h % 2), _NT,
                                       preferred_element_type=F32) + tab_ref[h]
        finish()

    @pl.when(first < 0)
    def _():
        for h in range(hp):
            qm = _pair_mask_q(q_ref[:, kcols[h]], h % 2)
            for w in range(CK_NW):
                rows = slice(w * CK_B, (w + 1) * CK_B)
                ks = pl.multiple_of(jnp.maximum(first + w, 0) * CK_B, CK_B)
                sT = lax.dot_general(k_ref[pl.ds(ks, CK_B), kcols[h]], qm, _NT,
                                     preferred_element_type=F32) + tab_ref[h, rows, :]
                s_ref[h, rows, :] = jnp.where(first + w >= 0, sT, NEG)
        finish()


def _chunk_attn(main, q_col0, k_col0, vt_arr, vt_row0, ext, n_heads):
    B, S, _ = main.shape
    hp = n_heads
    qw, vw = HEAD_DIM * hp, HEAD_DIM * hp
    return pl.pallas_call(
        functools.partial(_chunk_kernel, hp=hp),
        grid=(B, S // CK_B),
        in_specs=[
            pl.BlockSpec((None, CK_B, qw), lambda b, i: (b, i, q_col0 // qw)),
            pl.BlockSpec((None, S, qw), lambda b, i: (b, 0, k_col0 // qw)),
            pl.BlockSpec((None, S // VT_BLK, vw, VT_BLK), lambda b, i: (b, 0, vt_row0 // vw, 0)),
            pl.BlockSpec((hp, CK_EXT), lambda b, i: (0, 0)),
        ],
        out_specs=pl.BlockSpec((None, CK_B, vw), lambda b, i: (b, i, 0)),
        out_shape=jax.ShapeDtypeStruct((B, S, n_heads * HEAD_DIM), BF16),
        scratch_shapes=[pltpu.VMEM((hp, CK_NW * CK_B, CK_B), F32)] * 2,
        compiler_params=_cparams(("arbitrary", "arbitrary")),
        name="chunk_attn",
    )(main, main, vt_arr, ext)


def _sb_kernel(q_ref, k_ref, vt_ref, o_ref, z_ref, lb_ref, sfx_ref, *, hp, bq):
    qs = pl.program_id(1) * bq
    nsub = bq // SUB
    row = lax.broadcasted_iota(jnp.int32, (SUB, bq), 0)
    col = lax.broadcasted_iota(jnp.int32, (SUB, bq), 1)
    ur = lax.broadcasted_iota(jnp.int32, (SUB, 2 * SUB), 0)
    uc = lax.broadcasted_iota(jnp.int32, (SUB, 2 * SUB), 1) & (SUB - 1)
    upper2 = jnp.where(uc > ur, 1.0, 0.0).astype(BF16)
    kcols = [slice(LANES * (h // 2), LANES * (h // 2 + 1)) for h in range(hp)]
    qms = [_pair_mask_q(q_ref[:, kcols[h]], h % 2) for h in range(hp)]

    def step(kb, carries, masked):
        ks = pl.multiple_of(kb * bq, bq)
        for h in range(hp):
            z_ref[h] = lax.dot_general(k_ref[pl.ds(ks, bq), kcols[h]], qms[h], _NT,
                                       preferred_element_type=F32)
        totals = []
        for h in range(hp):
            tot = []
            for c in range(nsub):
                rows = slice(c * SUB, (c + 1) * SUB)
                z = z_ref[h, rows, :]
                l1 = jnp.log(1.0 + jnp.exp(-jnp.abs(z)))
                log_beta = jnp.minimum(z, 0.0) - l1
                log_keep = log_beta - z
                if masked:
                    log_keep = jnp.where(ks + c * SUB + row < qs + col, log_keep, 0.0)
                lb_ref[h, rows, :] = log_beta
                hi = log_keep.astype(BF16)
                lo = (log_keep - hi.astype(F32)).astype(BF16)
                sfx = jnp.dot(upper2, jnp.concatenate([hi, lo], axis=0),
                              preferred_element_type=F32)
                sfx_ref[h, rows, :] = sfx
                tot.append(sfx[0:1, :] + log_keep[0:1, :])
            totals.append(tot)
        out = []
        for h in range(hp):
            tail, acc = carries[h]
            parts = [None] * nsub
            for c in range(nsub - 1, -1, -1):
                rows = slice(c * SUB, (c + 1) * SUB)
                a = jnp.exp(lb_ref[h, rows, :] + sfx_ref[h, rows, :] + tail)
                if masked:
                    a = jnp.where(ks + c * SUB + row < qs + col, a, 0.0)
                parts[c] = a.astype(BF16)
                tail = tail + totals[h][c]
            vt = jnp.concatenate([vt_ref[kb * nsub + c, HEAD_DIM * h:HEAD_DIM * (h + 1), :]
                                  for c in range(nsub)], axis=1)
            acc = acc + jnp.dot(vt, jnp.concatenate(parts, axis=0), preferred_element_type=F32)
            out.append((tail, acc))
        return tuple(out)

    n_full = qs // bq
    carries = tuple((jnp.zeros((1, bq), F32), jnp.zeros((HEAD_DIM, bq), F32))
                    for _ in range(hp))
    carries = step(n_full, carries, True)

    def cond(state):
        kb, carries = state
        tail_max = carries[0][0]
        for h in range(1, hp):
            tail_max = jnp.maximum(tail_max, carries[h][0])
        return (kb >= 0) & (jnp.max(tail_max) > SB_ZERO_LOG)

    def body(state):
        kb, carries = state
        return kb - 1, step(kb, carries, False)

    _, carries = lax.while_loop(cond, body, (n_full - 1, carries))
    _store_heads(o_ref, [acc for (_, acc) in carries])


def _sb_attn(main, q_col0, k_col0, vt_arr, vt_row0, n_heads, bq=256):
    B, S, _ = main.shape
    hp = n_heads
    qw = HEAD_DIM * hp
    return pl.pallas_call(
        functools.partial(_sb_kernel, hp=hp, bq=bq),
        grid=(B, S // bq),
        in_specs=[
            pl.BlockSpec((None, bq, qw), lambda b, i: (b, i, q_col0 // qw)),
            pl.BlockSpec((None, S, qw), lambda b, i: (b, 0, k_col0 // qw)),
            pl.BlockSpec((None, S // VT_BLK, qw, VT_BLK), lambda b, i: (b, 0, vt_row0 // qw, 0)),
        ],
        out_specs=pl.BlockSpec((None, bq, qw), lambda b, i: (b, i, 0)),
        out_shape=jax.ShapeDtypeStruct((B, S, n_heads * HEAD_DIM), BF16),
        scratch_shapes=[pltpu.VMEM((hp, bq, bq), F32)] * 3,
        compiler_params=_cparams(("arbitrary", "arbitrary")),
        name="sb_attn",
    )(main, main, vt_arr)


def _mla_prep_kernel(aux_ref, pos_ref, invf_ref, qn_ref, kvn_ref, wuq_ref, wuqr_ref,
                     wk_ref, wvt_ref, oq_ref, ok_ref, ovt_ref):
    ang = pos_ref[...] * invf_ref[...]
    cos4, sin4 = jnp.cos(ang), jnp.sin(ang)
    lane = lax.broadcasted_iota(jnp.int32, ang.shape, 1)
    rotary = (lane >= NOPE_DIM) & (lane < NOPE_DIM + ROPE_DIM)
    cos_rows, sin_rows = [], []
    for m in range(LANES // ROPE_DIM):
        shift = (NOPE_DIM - ROPE_DIM * m) % LANES
        cm = cos4 if shift == 0 else pltpu.roll(cos4, shift, 1)
        sm = sin4 if shift == 0 else pltpu.roll(sin4, shift, 1)
        cos_rows.append(jnp.where(rotary, cm, 1.0))
        sin_rows.append(jnp.where(rotary, sm, 0.0))
    cos = jnp.concatenate(cos_rows, axis=0)
    sin = jnp.concatenate(sin_rows, axis=0)
    cq = _rms_bf16(aux_ref[:, 0:Q_LORA], qn_ref[...])
    ckv = _rms_bf16(aux_ref[:, Q_LORA:Q_LORA + KV_LORA], kvn_ref[...])
    o = Q_LORA + KV_LORA
    k_rope = aux_ref[:, o:o + LANES] * cos + aux_ref[:, o + LANES:o + 2 * LANES] * sin
    n_heads = oq_ref.shape[-1] // LANES
    for h in range(0, n_heads, 2):
        cols = slice(h * LANES, (h + 2) * LANES)
        qa = jnp.dot(cq, wuq_ref[:, cols], preferred_element_type=F32)
        qb = jnp.dot(cq, wuqr_ref[:, cols], preferred_element_type=F32)
        kn = jnp.dot(ckv, wk_ref[:, cols], preferred_element_type=F32)
        for d in range(2):
            c1 = slice(d * LANES, (d + 1) * LANES)
            c2 = slice((h + d) * LANES, (h + d + 1) * LANES)
            oq_ref[:, c2] = (qa[:, c1] * cos + qb[:, c1] * sin).astype(BF16)
            ok_ref[:, c2] = (kn[:, c1] + k_rope).astype(BF16)
    nv = wvt_ref.shape[0]
    for r in range(0, nv, 256):
        vt = lax.dot_general(wvt_ref[r:r + 256, :], ckv, _NT,
                             preferred_element_type=F32).astype(BF16)
        _store_vt(ovt_ref, vt, r)


MLA_TM = 512


def _mla_prep(aux, pos, invf, qn, kvn, wuq, wuqr, wk, wvt, tm=MLA_TM):
    B, S, na = aux.shape
    nq, nv = wuq.shape[1], wvt.shape[0]
    full = lambda a: pl.BlockSpec(a.shape, lambda b, i: (0,) * a.ndim)
    return pl.pallas_call(
        _mla_prep_kernel,
        grid=(B, S // tm),
        in_specs=[
            pl.BlockSpec((None, tm, na), lambda b, i: (b, i, 0)),
            pl.BlockSpec((None, None) + pos.shape[2:], lambda b, i: (b, i, 0, 0)),
            full(invf), full(qn), full(kvn), full(wuq), full(wuqr), full(wk), full(wvt),
        ],
        out_specs=[
            pl.BlockSpec((None, tm, nq), lambda b, i: (b, i, 0)),
            pl.BlockSpec((None, tm, nq), lambda b, i: (b, i, 0)),
            pl.BlockSpec((None, tm // VT_BLK, nv, VT_BLK), lambda b, i: (b, i, 0, 0)),
        ],
        out_shape=[
            jax.ShapeDtypeStruct((B, S, nq), BF16),
            jax.ShapeDtypeStruct((B, S, nq), BF16),
            jax.ShapeDtypeStruct((B, S // VT_BLK, nv, VT_BLK), BF16),
        ],
        compiler_params=_cparams(("arbitrary", "arbitrary")),
        name="mla_prep",
    )(aux, pos, invf, qn, kvn, wuq, wuqr, wk, wvt)


def _mlp_kernel(x_ref, oa_ref, ob_ref, wo_ref, g_ref, wu_ref, wd_ref, gf_ref, out_ref,
                x1_ref, h_ref, a_ref, *, final_norm, tf):
    na = oa_ref.shape[-1]
    x1 = (x_ref[...]
          + jnp.dot(oa_ref[...], wo_ref[0:na, :], preferred_element_type=F32)
          + jnp.dot(ob_ref[...], wo_ref[na:, :], preferred_element_type=F32))
    x1_ref[...] = x1
    h_ref[...] = _rms_bf16(x1, g_ref[...])
    for f in range(0, wu_ref.shape[1], tf):
        u = jnp.dot(h_ref[...], wu_ref[:, f:f + tf], preferred_element_type=F32)
        a_ref[:, f:f + tf] = jnp.square(jnp.maximum(u, 0.0)).astype(BF16)
    y = x1_ref[...] + jnp.dot(a_ref[...], wd_ref[...], preferred_element_type=F32)
    if final_norm:
        ms = jnp.mean(y * y, axis=-1, keepdims=True)
        y = y * lax.rsqrt(ms + EPS) * gf_ref[...]
    out_ref[...] = y


def _mlp(x2, oa, ob, wo, g, wu, wd, gf, final_norm, tm=512, tf=1024):
    T, D = x2.shape
    F = wu.shape[1]
    na, nb = oa.shape[1], ob.shape[1]
    const = lambda shape: pl.BlockSpec(shape, lambda i: (0, 0), pipeline_mode=pl.Buffered(1))
    return pl.pallas_call(
        functools.partial(_mlp_kernel, final_norm=final_norm, tf=tf),
        grid=(T // tm,),
        in_specs=[
            pl.BlockSpec((tm, D), lambda i: (i, 0)),
            pl.BlockSpec((tm, na), lambda i: (i, 0)),
            pl.BlockSpec((tm, nb), lambda i: (i, 0)),
            const((na + nb, D)),
            const((1, D)),
            const((D, F)),
            const((F, D)),
            const((1, D)),
        ],
        out_specs=pl.BlockSpec((tm, D), lambda i: (i, 0)),
        out_shape=jax.ShapeDtypeStruct((T, D), F32),
        scratch_shapes=[pltpu.VMEM((tm, D), F32), pltpu.VMEM((tm, D), BF16),
                        pltpu.VMEM((tm, F), BF16)],
        compiler_params=_cparams(("arbitrary",)),
        name="mlp",
    )(x2, oa, ob, wo, g.reshape(1, D), wu, wd, gf.reshape(1, D))


def _pad_cols(w, n):
    return jnp.pad(w, ((0, 0), (0, n - w.shape[1])))


def _rot_cols(w):
    half = ROPE_DIM // 2
    return jnp.concatenate([-w[:, half:], w[:, :half]], axis=1)


def _rope_slab(w):
    z = jnp.zeros((w.shape[0], NOPE_DIM), w.dtype)
    return jnp.concatenate([z, w, jnp.zeros((w.shape[0], LANES - NOPE_DIM - ROPE_DIM), w.dtype)], axis=1)


def _even_layer(x, g_mix, w_in, b_forget, rel_bias, w_out, g_mlp, w_up, w_down, g_final,
                final_norm):
    B, S, D = x.shape
    hf, hc = b_forget.shape[0], rel_bias.shape[0]
    wf, wc = hf * HEAD_DIM, hc * HEAD_DIM
    o = np.cumsum([0, wf, wf, wf, hf, wc, wc, wc])
    qa, ka, va, fa, qb, kb, vb = [w_in[:, o[n]:o[n + 1]] for n in range(7)]
    inv_sqrt_d = HEAD_DIM ** -0.5
    wm = jnp.concatenate([qa * inv_sqrt_d, ka, qb * inv_sqrt_d, kb], axis=1).astype(BF16)
    wvt = jnp.concatenate([va, vb], axis=1).T.astype(BF16)
    wa = _pad_cols(fa, LANES).astype(BF16)
    main, vt, aux = _inproj(x, g_mix, wm, wvt, wa)

    q_aug, k_aug = _logcum(aux, _pad_cols(b_forget.reshape(1, hf), LANES), hf)
    o_a = _flash("fox", main, 0, main, wf, vt, 0, (q_aug, k_aug), hf, 1.0)

    right = CK_EXT - rel_bias.shape[1] - (CHUNK + 1)
    ext = jnp.pad(rel_bias, ((0, 0), (CHUNK + 1, right)), mode="edge")
    o_b = _chunk_attn(main, 2 * wf, 2 * wf + wc, vt, wf, ext, hc)

    y = _mlp(x.reshape(B * S, D), o_a.reshape(B * S, wf), o_b.reshape(B * S, wc),
             w_out.astype(BF16), g_mlp, w_up.astype(BF16), w_down.astype(BF16), g_final,
             final_norm)
    return y.reshape(B, S, D)


def _odd_layer(x, positions, g_mix, w_in, q_norm, kv_norm, w_uq, w_ukv, w_out, g_mlp, w_up,
               w_down, g_final, final_norm):
    B, S, D = x.shape
    hm = w_ukv.shape[1] // (NOPE_DIM + HEAD_DIM)
    ws = w_in.shape[1] - Q_LORA - KV_LORA - ROPE_DIM
    hs = (ws // 3) // HEAD_DIM
    wsb = hs * HEAD_DIM
    o = np.cumsum([0, wsb, wsb, wsb, Q_LORA, KV_LORA, ROPE_DIM])
    qc, kc, vc, w_cq, w_ckv, w_kr = [w_in[:, o[n]:o[n + 1]] for n in range(6)]
    wm = jnp.concatenate([qc * HEAD_DIM ** -0.5, kc], axis=1).astype(BF16)
    wa = jnp.concatenate([w_cq, w_ckv, _rope_slab(w_kr), _rope_slab(_rot_cols(w_kr))],
                         axis=1).astype(BF16)
    main, vt, aux = _inproj(x, g_mix, wm, vc.T.astype(BF16), wa)
    o_c = _sb_attn(main, 0, wsb, vt, 0, hs)

    dq = NOPE_DIM + ROPE_DIM
    wuq3 = w_uq.reshape(Q_LORA, hm, dq)
    nope, ropew = wuq3[:, :, :NOPE_DIM], wuq3[:, :, NOPE_DIM:]
    zq = jnp.zeros((Q_LORA, hm, LANES - dq), w_uq.dtype)
    wuq = jnp.concatenate([nope, ropew, zq], axis=2).reshape(Q_LORA, hm * LANES).astype(BF16)
    half = ROPE_DIM // 2
    ropr = jnp.concatenate([-ropew[:, :, half:], ropew[:, :, :half]], axis=2)
    wuqr = jnp.concatenate([jnp.zeros_like(nope), ropr, zq], axis=2)
    wuqr = wuqr.reshape(Q_LORA, hm * LANES).astype(BF16)
    wkv3 = w_ukv.reshape(KV_LORA, hm, NOPE_DIM + HEAD_DIM)
    wk = jnp.concatenate([wkv3[:, :, :NOPE_DIM],
                          jnp.zeros((KV_LORA, hm, LANES - NOPE_DIM), w_ukv.dtype)], axis=2)
    wk = wk.reshape(KV_LORA, hm * LANES).astype(BF16)
    wv_t = wkv3[:, :, NOPE_DIM:].reshape(KV_LORA, hm * HEAD_DIM).T.astype(BF16)
    freqs = (ROPE_THETA ** (-jnp.arange(half, dtype=F32) / half))
    invf = jnp.tile(freqs, 2 * LANES // ROPE_DIM).reshape(1, LANES)
    groups = LANES // ROPE_DIM
    pos = positions.astype(F32).reshape(B, S // MLA_TM, groups, MLA_TM // groups)
    pos = jnp.repeat(jnp.swapaxes(pos, 2, 3), ROPE_DIM, axis=-1)
    qm, km, vtm = _mla_prep(aux, pos, invf, q_norm.reshape(1, Q_LORA),
                            kv_norm.reshape(1, KV_LORA), wuq, wuqr, wk, wv_t)
    o_d = _flash("mla", qm, 0, km, 0, vtm, 0, None, hm, dq ** -0.5)

    y = _mlp(x.reshape(B * S, D), o_c.reshape(B * S, wsb), o_d.reshape(B * S, hm * HEAD_DIM),
             w_out.astype(BF16), g_mlp, w_up.astype(BF16), w_down.astype(BF16), g_final,
             final_norm)
    return y.reshape(B, S, D)


def kernel(x, positions, norm_mix, norm_mlp, norm_final, w_in_ab, b_forget, rel_bias, w_out_ab,
           w_in_cd, q_norm, kv_norm, w_uq, w_ukv, w_out_cd, w_up, w_down):
    depth = norm_mix.shape[0]
    for layer in range(depth):
        last = layer == depth - 1
        if layer % 2 == 0:
            e = layer // 2
            x = _even_layer(x, norm_mix[layer], w_in_ab[e], b_forget[e], rel_bias[e], w_out_ab[e],
                            norm_mlp[layer], w_up[layer], w_down[layer], norm_final, last)
        else:
            o = layer // 2
            x = _odd_layer(x, positions, norm_mix[layer], w_in_cd[o], q_norm[o], kv_norm[o],
                           w_uq[o], w_ukv[o], w_out_cd[o], norm_mlp[layer], w_up[layer],
                           w_down[layer], norm_final, last)
    return x
```

```python
import functools
import math

import numpy as np
import jax
import jax.numpy as jnp
from jax import lax
from jax.experimental import pallas as pl
from jax.experimental.pallas import tpu as pltpu

F32 = jnp.float32
BF16 = jnp.bfloat16

EPS = 1e-6
HEAD_DIM = 64
CHUNK = 64
N_LEFT_CHUNKS = 8
REL_CLIP = 256
ROPE_DIM = 32
NOPE_DIM = 64
ROPE_THETA = 10000.0
Q_LORA = 384
KV_LORA = 256

LANES = 128
VT_BLK = LANES
SUB = LANES
FLASH_HP = 4
NEG = -1e30
LOG2E = math.log2(math.e)
SB_ZERO_LOG = -104.0
VMEM_LIMIT = 56 * 1024 * 1024

_NT = (((1,), (1,)), ((), ()))


def _cparams(sem, flags=None):
    return pltpu.CompilerParams(dimension_semantics=sem, vmem_limit_bytes=VMEM_LIMIT, flags=flags)


def _rms_bf16(x, g):
    ms = jnp.mean(x * x, axis=-1, keepdims=True)
    return (x * lax.rsqrt(ms + EPS) * g).astype(BF16)


def _store_vt(ovt_ref, vt, row0):
    rows, tm = vt.shape
    for c in range(tm // VT_BLK):
        ovt_ref[c, row0:row0 + rows, :] = vt[:, c * VT_BLK:(c + 1) * VT_BLK]


def _inproj_kernel(x_ref, g_ref, wm_ref, wvt_ref, wa_ref, om_ref, ovt_ref, oa_ref):
    h = _rms_bf16(x_ref[...], g_ref[...])
    nm = om_ref.shape[-1]
    for c in range(0, nm, 512):
        om_ref[:, c:c + 512] = jnp.dot(
            h, wm_ref[:, c:c + 512], preferred_element_type=F32).astype(BF16)
    nv = wvt_ref.shape[0]
    for r in range(0, nv, 256):
        vt = lax.dot_general(wvt_ref[r:r + 256, :], h, _NT,
                             preferred_element_type=F32).astype(BF16)
        _store_vt(ovt_ref, vt, r)
    oa_ref[...] = jnp.dot(h, wa_ref[...], preferred_element_type=F32)


def _inproj(x, g, wm, wvt, wa, tm=512):
    B, S, D = x.shape
    nm, nv, na = wm.shape[1], wvt.shape[0], wa.shape[1]
    return pl.pallas_call(
        _inproj_kernel,
        grid=(B, S // tm),
        in_specs=[
            pl.BlockSpec((None, tm, D), lambda b, i: (b, i, 0)),
            pl.BlockSpec((1, D), lambda b, i: (0, 0)),
            pl.BlockSpec((D, nm), lambda b, i: (0, 0)),
            pl.BlockSpec((nv, D), lambda b, i: (0, 0)),
            pl.BlockSpec((D, na), lambda b, i: (0, 0)),
        ],
        out_specs=[
            pl.BlockSpec((None, tm, nm), lambda b, i: (b, i, 0)),
            pl.BlockSpec((None, tm // VT_BLK, nv, VT_BLK), lambda b, i: (b, i, 0, 0)),
            pl.BlockSpec((None, tm, na), lambda b, i: (b, i, 0)),
        ],
        out_shape=[
            jax.ShapeDtypeStruct((B, S, nm), BF16),
            jax.ShapeDtypeStruct((B, S // VT_BLK, nv, VT_BLK), BF16),
            jax.ShapeDtypeStruct((B, S, na), F32),
        ],
        compiler_params=_cparams(("arbitrary", "arbitrary")),
        name="inproj",
    )(x, g.reshape(1, D), wm, wvt, wa)


def _split3(x):
    hi = x.astype(BF16)
    r = x - hi.astype(F32)
    mid = r.astype(BF16)
    lo = (r - mid.astype(F32)).astype(BF16)
    return hi, mid, lo


AUG_W = 8


def _logcum_kernel(fa_ref, b_ref, pq_ref, pk_ref, oneq_ref, onek_ref, oq_ref, ok_ref, carry_ref):
    @pl.when(pl.program_id(1) == 0)
    def _():
        carry_ref[...] = jnp.zeros_like(carry_ref)

    z = fa_ref[...] + b_ref[...]
    lf = jnp.minimum(z, 0.0) - jnp.log(1.0 + jnp.exp(-jnp.abs(z)))
    tc = lf.shape[0]
    r = lax.broadcasted_iota(jnp.int32, (tc, tc), 0)
    c = lax.broadcasted_iota(jnp.int32, (tc, tc), 1)
    tri = jnp.where(r >= c, 1.0, 0.0).astype(BF16)
    cs = carry_ref[...]
    for part in _split3(lf):
        cs = cs + jnp.dot(tri, part, preferred_element_type=F32)
    carry_ref[...] = cs[tc - 1:tc, :]
    qa, ka = oneq_ref[...], onek_ref[...]
    for n, part in enumerate(_split3(cs * LOG2E)):
        qa = qa + jnp.dot(part, pq_ref[n], preferred_element_type=F32)
        ka = ka + jnp.dot(part, pk_ref[n], preferred_element_type=F32)
    oq_ref[...] = qa.astype(BF16)
    ok_ref[...] = ka.astype(BF16)


def _logcum(fa, bias, n_heads, tc=512):
    B, S, W = fa.shape
    na = LANES
    pq = np.zeros((3, W, na), np.float32)
    pk = np.zeros((3, W, na), np.float32)
    oneq = np.zeros((1, na), np.float32)
    onek = np.zeros((1, na), np.float32)
    for h in range(n_heads):
        base = h * AUG_W
        for n in range(3):
            pq[n, h, base + n] = 1.0
            pk[n, h, base + 3 + n] = -1.0
        oneq[0, base + 3:base + 6] = 1.0
        onek[0, base:base + 3] = 1.0
    const = lambda a: pl.BlockSpec(a.shape, lambda b, i: (0,) * a.ndim)
    args = [jnp.asarray(pq, BF16), jnp.asarray(pk, BF16), jnp.asarray(oneq), jnp.asarray(onek)]
    return pl.pallas_call(
        _logcum_kernel,
        grid=(B, S // tc),
        in_specs=[pl.BlockSpec((None, tc, W), lambda b, i: (b, i, 0)),
                  pl.BlockSpec((1, W), lambda b, i: (0, 0))] + [const(a) for a in args],
        out_specs=[pl.BlockSpec((None, tc, na), lambda b, i: (b, i, 0))] * 2,
        out_shape=[jax.ShapeDtypeStruct((B, S, na), BF16)] * 2,
        scratch_shapes=[pltpu.VMEM((1, W), F32)],
        compiler_params=_cparams(("arbitrary", "arbitrary")),
        name="logcum",
    )(fa, bias, *args)


def _pair_mask_q(q2, j):
    lane = lax.broadcasted_iota(jnp.int32, q2.shape, 1)
    keep = (lane >= HEAD_DIM * j) & (lane < HEAD_DIM * (j + 1))
    return jnp.where(keep, q2, jnp.zeros_like(q2))


ONES_ROWS = 16


def _softmax_step(tiles, vts, carry, tile_max=None):
    m, acc = carry
    if tile_max is not None:
        m_new = jnp.maximum(m, tile_max)
    else:
        m_new = m
        for tile in tiles:
            m_new = jnp.maximum(m_new, jnp.max(tile(), axis=0, keepdims=True))
    alpha = jnp.exp2(m - m_new)
    pv = None
    for tile, vt in zip(tiles, vts):
        p = jnp.exp2(tile() - m_new).astype(BF16)
        vt1 = jnp.concatenate([vt, jnp.ones((ONES_ROWS, vt.shape[1]), BF16)], axis=0)
        d = jnp.dot(vt1, p, preferred_element_type=F32)
        pv = d if pv is None else pv + d
    return m_new, alpha * acc + pv


def _softmax_init(bq):
    return (jnp.full((1, bq), NEG, F32), jnp.zeros((HEAD_DIM + ONES_ROWS, bq), F32))


def _softmax_out(carry):
    _, acc = carry
    return acc[0:HEAD_DIM] / acc[HEAD_DIM:HEAD_DIM + 1]


def _store_heads(o_ref, outs):
    oT = jnp.concatenate(outs, axis=0)
    o_ref[...] = oT.T.astype(o_ref.dtype)


def _flash_kernel(*refs, mode, hp, bq):
    if mode == "fox":
        q_ref, k_ref, vt_ref, qaug_ref, kaug_ref, o_ref = refs[:6]
    else:
        q_ref, k_ref, vt_ref, o_ref = refs[:4]
    sa_ref, sb_ref, ma_ref, mb_ref, qt_ref = refs[-5:]
    qs = pl.program_id(2) * bq
    sub = bq
    row = lax.broadcasted_iota(jnp.int32, (sub, bq), 0)
    col = lax.broadcasted_iota(jnp.int32, (sub, bq), 1)
    if mode == "fox":
        kcols = [slice(LANES * (h // 2), LANES * (h // 2 + 1)) for h in range(hp)]
        lane = lax.broadcasted_iota(jnp.int32, (bq, LANES), 1)
        qa = qaug_ref[...]
        qms = []
        for h in range(hp):
            first = AUG_W * (pl.program_id(1) * hp + h)
            own = (lane >= first) & (lane < first + AUG_W)
            qms.append(jnp.concatenate(
                [_pair_mask_q(q_ref[:, kcols[h]], h % 2),
                 jnp.where(own, qa, jnp.zeros_like(qa))], axis=1))
    else:
        kcols = [slice(LANES * h, LANES * (h + 1)) for h in range(hp)]
        qms = [q_ref[:, kcols[h]] for h in range(hp)]
    for h in range(hp):
        qt_ref[h] = qms[h].T

    def scores(sb, h, masked):
        ks = pl.multiple_of(sb * sub, sub)
        k = k_ref[pl.ds(ks, sub), kcols[h]]
        if mode == "fox":
            k = jnp.concatenate([k, kaug_ref[pl.ds(ks, sub), :]], axis=1)
        sT = jnp.dot(k, qt_ref[h], preferred_element_type=F32)
        if masked and mode == "fox":
            sT = jnp.where(ks + row <= qs + col, sT, NEG)
        elif masked:
            sT = jnp.where(((ks + row) >> 6) <= ((qs + col) >> 6), sT, NEG)
        return sT

    nsub = bq // sub
    nvt = sub // VT_BLK

    def produce(buf, sb0, masked, h):
        s_buf, m_buf = buf
        tile_max = None
        for c in range(nsub):
            sT = scores(sb0 + c, h, masked)
            s_buf[h, c] = sT
            cm = jnp.max(sT, axis=0, keepdims=True)
            tile_max = cm if tile_max is None else jnp.maximum(tile_max, cm)
        m_buf[h] = tile_max

    def consume(buf, sb0, carry, h):
        s_buf, m_buf = buf
        tiles = [lambda c=c: s_buf[h, c] for c in range(nsub)]
        vts = [jnp.concatenate([vt_ref[(sb0 + c) * nvt + v, HEAD_DIM * h:HEAD_DIM * (h + 1), :]
                                for v in range(nvt)], axis=1) for c in range(nsub)]
        return _softmax_step(tiles, vts, carry, tile_max=m_buf[h])

    def stage(cur, cur_sb, nxt, nxt_sb, carries):
        if nxt is not None:
            for h in range(hp):
                produce(nxt, nxt_sb, False, h)
        return tuple(consume(cur, cur_sb, carries[h], h) for h in range(hp))

    n = pl.program_id(2)
    diag_sb = qs // sub
    buf_a, buf_b = (sa_ref, ma_ref), (sb_ref, mb_ref)
    for h in range(hp):
        produce(buf_a, diag_sb, True, h)

    def pair(j, carries):
        carries = stage(buf_a, jnp.where(j == 0, diag_sb, (2 * j - 1) * nsub),
                        buf_b, 2 * j * nsub, carries)
        return stage(buf_b, 2 * j * nsub,
                     buf_a, jnp.minimum(2 * j + 1, n - 1) * nsub, carries)

    carries = tuple(_softmax_init(bq) for _ in range(hp))
    carries = lax.fori_loop(0, (n + 1) // 2, pair, carries)
    carries = lax.cond(
        n % 2 == 0,
        lambda c: stage(buf_a, jnp.where(n == 0, diag_sb, (n - 1) * nsub), None, None, c),
        lambda c: c, carries)
    _store_heads(o_ref, [_softmax_out(c) for c in carries])


def _flash(mode, q_arr, q_col0, k_arr, k_col0, vt_arr, vt_row0, extra, n_heads,
           hp=FLASH_HP, bq=256):
    B, S, _ = q_arr.shape
    qw = (HEAD_DIM if mode == "fox" else LANES) * hp
    vw = HEAD_DIM * hp
    in_specs = [
        pl.BlockSpec((None, bq, qw), lambda b, g, i: (b, i, q_col0 // qw + g)),
        pl.BlockSpec((None, S, qw), lambda b, g, i: (b, 0, k_col0 // qw + g)),
        pl.BlockSpec((None, S // VT_BLK, vw, VT_BLK),
                     lambda b, g, i: (b, 0, vt_row0 // vw + g, 0)),
    ]
    args = [q_arr, k_arr, vt_arr]
    if mode == "fox":
        q_aug, k_aug = extra
        in_specs += [
            pl.BlockSpec((None, bq, LANES), lambda b, g, i: (b, i, 0)),
            pl.BlockSpec((None, S, LANES), lambda b, g, i: (b, 0, 0)),
        ]
        args += [q_aug, k_aug]
    return pl.pallas_call(
        functools.partial(_flash_kernel, mode=mode, hp=hp, bq=bq),
        grid=(B, n_heads // hp, S // bq),
        in_specs=in_specs,
        out_specs=pl.BlockSpec((None, bq, vw), lambda b, g, i: (b, i, g)),
        out_shape=jax.ShapeDtypeStruct((B, S, n_heads * HEAD_DIM), BF16),
        scratch_shapes=([pltpu.VMEM((hp, 1, bq, bq), F32)] * 2
                        + [pltpu.VMEM((hp, 1, bq), F32)] * 2
                        + [pltpu.VMEM((hp, 2 * LANES if mode == "fox" else LANES, bq), BF16)]),
        compiler_params=_cparams(("arbitrary", "arbitrary", "arbitrary")),
        name="flash_" + mode,
    )(*args)


CK_B = 2 * CHUNK
CK_NW = N_LEFT_CHUNKS * CHUNK // CK_B + 1
CK_EXT = (CK_NW + 1) * CK_B


def _chunk_kernel(q_ref, k_ref, vt_ref, ext_ref, o_ref, tab_ref, s_ref, *, hp, nq):
    i = pl.program_id(1)

    @pl.when(i == 0)
    def _():
        jj = lax.broadcasted_iota(jnp.int32, (CK_B, CK_B), 0)
        rr = lax.broadcasted_iota(jnp.int32, (CK_B, CK_B), 1)
        for h in range(hp):
            for w in range(CK_NW):
                a = (CK_NW - 1 - w) * CK_B
                g = jnp.broadcast_to(ext_ref[h:h + 1, a:a + 2 * CK_B], (CK_B, 2 * CK_B))
                t = pltpu.roll(g, CK_B, 1, stride=1, stride_axis=0)[:, :CK_B]
                if w == 0:
                    t = jnp.where((rr >= CHUNK) & (jj < CHUNK), NEG, t)
                if w == CK_NW - 1:
                    t = jnp.where((rr < CHUNK) & (jj >= CHUNK), NEG, t)
                tab_ref[h, w * CK_B:(w + 1) * CK_B, :] = t

    kcols = [slice(LANES * (h // 2), LANES * (h // 2 + 1)) for h in range(hp)]
    firsts = [i * nq + u - (CK_NW - 1) for u in range(nq)]

    def finish():
        for u in range(nq):
            kbc = [jnp.maximum(firsts[u] + w, 0) for w in range(CK_NW)]
            outs = []
            for h in range(hp):
                vt = jnp.concatenate([vt_ref[kbc[w], HEAD_DIM * h:HEAD_DIM * (h + 1), :]
                                      for w in range(CK_NW)], axis=1)
                outs.append(_softmax_out(_softmax_step([lambda u=u, h=h: s_ref[u, h]], [vt],
                                                       _softmax_init(CK_B))))
            oT = jnp.concatenate(outs, axis=0)
            o_ref[u * CK_B:(u + 1) * CK_B, :] = oT.T.astype(o_ref.dtype)

    def pair_scores(u, p, ks, nrows):
        q2 = q_ref[u * CK_B:(u + 1) * CK_B, kcols[2 * p]]
        qq = jnp.concatenate([_pair_mask_q(q2, 0), _pair_mask_q(q2, 1)], axis=0)
        return lax.dot_general(k_ref[pl.ds(ks, nrows), kcols[2 * p]], qq, _NT,
                               preferred_element_type=F32)

    @pl.when(firsts[0] >= 0)
    def _():
        for u in range(nq):
            ks = pl.multiple_of(firsts[u] * CK_B, CK_B)
            for p in range(hp // 2):
                sT = pair_scores(u, p, ks, CK_NW * CK_B)
                for j in range(2):
                    s_ref[u, 2 * p + j] = sT[:, j * CK_B:(j + 1) * CK_B] + tab_ref[2 * p + j]
        finish()

    @pl.when(firsts[0] < 0)
    def _():
        for u in range(nq):
            for p in range(hp // 2):
                for w in range(CK_NW):
                    rows = slice(w * CK_B, (w + 1) * CK_B)
                    ks = pl.multiple_of(jnp.maximum(firsts[u] + w, 0) * CK_B, CK_B)
                    sT = pair_scores(u, p, ks, CK_B)
                    for j in range(2):
                        s_ref[u, 2 * p + j, rows, :] = jnp.where(
                            firsts[u] + w >= 0,
                            sT[:, j * CK_B:(j + 1) * CK_B] + tab_ref[2 * p + j, rows, :], NEG)
        finish()


def _chunk_attn(main, q_col0, k_col0, vt_arr, vt_row0, ext, n_heads, nq=4):
    B, S, _ = main.shape
    hp = n_heads
    qw, vw = HEAD_DIM * hp, HEAD_DIM * hp
    return pl.pallas_call(
        functools.partial(_chunk_kernel, hp=hp, nq=nq),
        grid=(B, S // (nq * CK_B)),
        in_specs=[
            pl.BlockSpec((None, nq * CK_B, qw), lambda b, i: (b, i, q_col0 // qw)),
            pl.BlockSpec((None, S, qw), lambda b, i: (b, 0, k_col0 // qw)),
            pl.BlockSpec((None, S // VT_BLK, vw, VT_BLK), lambda b, i: (b, 0, vt_row0 // vw, 0)),
            pl.BlockSpec((hp, CK_EXT), lambda b, i: (0, 0)),
        ],
        out_specs=pl.BlockSpec((None, nq * CK_B, vw), lambda b, i: (b, i, 0)),
        out_shape=jax.ShapeDtypeStruct((B, S, n_heads * HEAD_DIM), BF16),
        scratch_shapes=[pltpu.VMEM((hp, CK_NW * CK_B, CK_B), F32),
                        pltpu.VMEM((nq, hp, CK_NW * CK_B, CK_B), F32)],
        compiler_params=_cparams(("arbitrary", "arbitrary")),
        name="chunk_attn",
    )(main, main, vt_arr, ext)


def _sb_kernel(q_ref, k_ref, vt_ref, o_ref, z_ref, lb_ref, sfx_ref, *, hp, bq):
    qs = pl.program_id(1) * bq
    nsub = bq // SUB
    row = lax.broadcasted_iota(jnp.int32, (SUB, bq), 0)
    col = lax.broadcasted_iota(jnp.int32, (SUB, bq), 1)
    ur = lax.broadcasted_iota(jnp.int32, (SUB, 2 * SUB), 0)
    uc = lax.broadcasted_iota(jnp.int32, (SUB, 2 * SUB), 1) & (SUB - 1)
    upper2 = jnp.where(uc > ur, 1.0, 0.0).astype(BF16)
    kcols = [slice(LANES * (h // 2), LANES * (h // 2 + 1)) for h in range(hp)]
    qms = [_pair_mask_q(q_ref[:, kcols[h]], h % 2) for h in range(hp)]

    def step(kb, carries, masked):
        ks = pl.multiple_of(kb * bq, bq)
        for h in range(hp):
            z_ref[h] = lax.dot_general(k_ref[pl.ds(ks, bq), kcols[h]], qms[h], _NT,
                                       preferred_element_type=F32)
        totals = []
        for h in range(hp):
            tot = []
            for c in range(nsub):
                rows = slice(c * SUB, (c + 1) * SUB)
                z = z_ref[h, rows, :]
                l1 = jnp.log(1.0 + jnp.exp(-jnp.abs(z)))
                log_beta = jnp.minimum(z, 0.0) - l1
                log_keep = log_beta - z
                if masked:
                    log_keep = jnp.where(ks + c * SUB + row < qs + col, log_keep, 0.0)
                lb_ref[h, rows, :] = log_beta
                hi = log_keep.astype(BF16)
                lo = (log_keep - hi.astype(F32)).astype(BF16)
                sfx = jnp.dot(upper2, jnp.concatenate([hi, lo], axis=0),
                              preferred_element_type=F32)
                sfx_ref[h, rows, :] = sfx
                tot.append(sfx[0:1, :] + log_keep[0:1, :])
            totals.append(tot)
        out = []
        for h in range(hp):
            tail, acc = carries[h]
            parts = [None] * nsub
            for c in range(nsub - 1, -1, -1):
                rows = slice(c * SUB, (c + 1) * SUB)
                a = jnp.exp(lb_ref[h, rows, :] + sfx_ref[h, rows, :] + tail)
                if masked:
                    a = jnp.where(ks + c * SUB + row < qs + col, a, 0.0)
                parts[c] = a.astype(BF16)
                tail = tail + totals[h][c]
            vt = jnp.concatenate([vt_ref[kb * nsub + c, HEAD_DIM * h:HEAD_DIM * (h + 1), :]
                                  for c in range(nsub)], axis=1)
            acc = acc + jnp.dot(vt, jnp.concatenate(parts, axis=0), preferred_element_type=F32)
            out.append((tail, acc))
        return tuple(out)

    n_full = qs // bq
    carries = tuple((jnp.zeros((1, bq), F32), jnp.zeros((HEAD_DIM, bq), F32))
                    for _ in range(hp))
    carries = step(n_full, carries, True)

    def cond(state):
        kb, carries = state
        tail_max = carries[0][0]
        for h in range(1, hp):
            tail_max = jnp.maximum(tail_max, carries[h][0])
        return (kb >= 0) & (jnp.max(tail_max) > SB_ZERO_LOG)

    def body(state):
        kb, carries = state
        return kb - 1, step(kb, carries, False)

    _, carries = lax.while_loop(cond, body, (n_full - 1, carries))
    _store_heads(o_ref, [acc for (_, acc) in carries])


def _sb_attn(main, q_col0, k_col0, vt_arr, vt_row0, n_heads, bq=256):
    B, S, _ = main.shape
    hp = n_heads
    qw = HEAD_DIM * hp
    return pl.pallas_call(
        functools.partial(_sb_kernel, hp=hp, bq=bq),
        grid=(B, S // bq),
        in_specs=[
            pl.BlockSpec((None, bq, qw), lambda b, i: (b, i, q_col0 // qw)),
            pl.BlockSpec((None, S, qw), lambda b, i: (b, 0, k_col0 // qw)),
            pl.BlockSpec((None, S // VT_BLK, qw, VT_BLK), lambda b, i: (b, 0, vt_row0 // qw, 0)),
        ],
        out_specs=pl.BlockSpec((None, bq, qw), lambda b, i: (b, i, 0)),
        out_shape=jax.ShapeDtypeStruct((B, S, n_heads * HEAD_DIM), BF16),
        scratch_shapes=[pltpu.VMEM((hp, bq, bq), F32)] * 3,
        compiler_params=_cparams(("arbitrary", "arbitrary")),
        name="sb_attn",
    )(main, main, vt_arr)


def _mla_prep_kernel(aux_ref, pos_ref, invf_ref, qn_ref, kvn_ref, wuq_ref, wuqr_ref,
                     wk_ref, wvt_ref, oq_ref, ok_ref, ovt_ref, *, q_scale):
    ang = pos_ref[...] * invf_ref[...]
    cos4, sin4 = jnp.cos(ang), jnp.sin(ang)
    lane = lax.broadcasted_iota(jnp.int32, ang.shape, 1)
    rotary = (lane >= NOPE_DIM) & (lane < NOPE_DIM + ROPE_DIM)
    cos_rows, sin_rows = [], []
    for m in range(LANES // ROPE_DIM):
        shift = (NOPE_DIM - ROPE_DIM * m) % LANES
        cm = cos4 if shift == 0 else pltpu.roll(cos4, shift, 1)
        sm = sin4 if shift == 0 else pltpu.roll(sin4, shift, 1)
        cos_rows.append(jnp.where(rotary, cm, 1.0))
        sin_rows.append(jnp.where(rotary, sm, 0.0))
    cos = jnp.concatenate(cos_rows, axis=0)
    sin = jnp.concatenate(sin_rows, axis=0)
    cq = _rms_bf16(aux_ref[:, 0:Q_LORA], qn_ref[...])
    ckv = _rms_bf16(aux_ref[:, Q_LORA:Q_LORA + KV_LORA], kvn_ref[...])
    o = Q_LORA + KV_LORA
    k_rope = aux_ref[:, o:o + LANES] * cos + aux_ref[:, o + LANES:o + 2 * LANES] * sin
    cos_q, sin_q = cos * q_scale, sin * q_scale
    n_heads = oq_ref.shape[-1] // LANES
    for h in range(0, n_heads, 2):
        cols = slice(h * LANES, (h + 2) * LANES)
        qa = jnp.dot(cq, wuq_ref[:, cols], preferred_element_type=F32)
        qb = jnp.dot(cq, wuqr_ref[:, cols], preferred_element_type=F32)
        kn = jnp.dot(ckv, wk_ref[:, cols], preferred_element_type=F32)
        for d in range(2):
            c1 = slice(d * LANES, (d + 1) * LANES)
            c2 = slice((h + d) * LANES, (h + d + 1) * LANES)
            oq_ref[:, c2] = (qa[:, c1] * cos_q + qb[:, c1] * sin_q).astype(BF16)
            ok_ref[:, c2] = (kn[:, c1] + k_rope).astype(BF16)
    nv = wvt_ref.shape[0]
    for r in range(0, nv, 256):
        vt = lax.dot_general(wvt_ref[r:r + 256, :], ckv, _NT,
                             preferred_element_type=F32).astype(BF16)
        _store_vt(ovt_ref, vt, r)


MLA_TM = 512


def _mla_prep(aux, pos, invf, qn, kvn, wuq, wuqr, wk, wvt, q_scale, tm=MLA_TM):
    B, S, na = aux.shape
    nq, nv = wuq.shape[1], wvt.shape[0]
    full = lambda a: pl.BlockSpec(a.shape, lambda b, i: (0,) * a.ndim)
    return pl.pallas_call(
        functools.partial(_mla_prep_kernel, q_scale=q_scale),
        grid=(B, S // tm),
        in_specs=[
            pl.BlockSpec((None, tm, na), lambda b, i: (b, i, 0)),
            pl.BlockSpec((None, None) + pos.shape[2:], lambda b, i: (b, i, 0, 0)),
            full(invf), full(qn), full(kvn), full(wuq), full(wuqr), full(wk), full(wvt),
        ],
        out_specs=[
            pl.BlockSpec((None, tm, nq), lambda b, i: (b, i, 0)),
            pl.BlockSpec((None, tm, nq), lambda b, i: (b, i, 0)),
            pl.BlockSpec((None, tm // VT_BLK, nv, VT_BLK), lambda b, i: (b, i, 0, 0)),
        ],
        out_shape=[
            jax.ShapeDtypeStruct((B, S, nq), BF16),
            jax.ShapeDtypeStruct((B, S, nq), BF16),
            jax.ShapeDtypeStruct((B, S // VT_BLK, nv, VT_BLK), BF16),
        ],
        compiler_params=_cparams(("arbitrary", "arbitrary")),
        name="mla_prep",
    )(aux, pos, invf, qn, kvn, wuq, wuqr, wk, wvt)


def _mlp_kernel(x_ref, oa_ref, ob_ref, wo_ref, g_ref, wu_ref, wd_ref, gf_ref, out_ref,
                x1_ref, h_ref, a_ref, *, final_norm, tf):
    na = oa_ref.shape[-1]
    x1 = (x_ref[...]
          + jnp.dot(oa_ref[...], wo_ref[0:na, :], preferred_element_type=F32)
          + jnp.dot(ob_ref[...], wo_ref[na:, :], preferred_element_type=F32))
    x1_ref[...] = x1
    h_ref[...] = _rms_bf16(x1, g_ref[...])
    for f in range(0, wu_ref.shape[1], tf):
        u = jnp.dot(h_ref[...], wu_ref[:, f:f + tf], preferred_element_type=F32)
        a_ref[:, f:f + tf] = jnp.square(jnp.maximum(u, 0.0)).astype(BF16)
    y = x1_ref[...] + jnp.dot(a_ref[...], wd_ref[...], preferred_element_type=F32)
    if final_norm:
        ms = jnp.mean(y * y, axis=-1, keepdims=True)
        y = y * lax.rsqrt(ms + EPS) * gf_ref[...]
    out_ref[...] = y


def _mlp(x2, oa, ob, wo, g, wu, wd, gf, final_norm, tm=512, tf=1024):
    T, D = x2.shape
    F = wu.shape[1]
    na, nb = oa.shape[1], ob.shape[1]
    const = lambda shape: pl.BlockSpec(shape, lambda i: (0, 0), pipeline_mode=pl.Buffered(1))
    return pl.pallas_call(
        functools.partial(_mlp_kernel, final_norm=final_norm, tf=tf),
        grid=(T // tm,),
        in_specs=[
            pl.BlockSpec((tm, D), lambda i: (i, 0)),
            pl.BlockSpec((tm, na), lambda i: (i, 0)),
            pl.BlockSpec((tm, nb), lambda i: (i, 0)),
            const((na + nb, D)),
            const((1, D)),
            const((D, F)),
            const((F, D)),
            const((1, D)),
        ],
        out_specs=pl.BlockSpec((tm, D), lambda i: (i, 0)),
        out_shape=jax.ShapeDtypeStruct((T, D), F32),
        scratch_shapes=[pltpu.VMEM((tm, D), F32), pltpu.VMEM((tm, D), BF16),
                        pltpu.VMEM((tm, F), BF16)],
        compiler_params=_cparams(("arbitrary",)),
        name="mlp",
    )(x2, oa, ob, wo, g.reshape(1, D), wu, wd, gf.reshape(1, D))


def _pad_cols(w, n):
    return jnp.pad(w, ((0, 0), (0, n - w.shape[1])))


def _rot_cols(w):
    half = ROPE_DIM // 2
    return jnp.concatenate([-w[:, half:], w[:, :half]], axis=1)


def _rope_slab(w):
    z = jnp.zeros((w.shape[0], NOPE_DIM), w.dtype)
    return jnp.concatenate([z, w, jnp.zeros((w.shape[0], LANES - NOPE_DIM - ROPE_DIM), w.dtype)], axis=1)


def _even_layer(x, g_mix, w_in, b_forget, rel_bias, w_out, g_mlp, w_up, w_down, g_final,
                final_norm):
    B, S, D = x.shape
    hf, hc = b_forget.shape[0], rel_bias.shape[0]
    wf, wc = hf * HEAD_DIM, hc * HEAD_DIM
    o = np.cumsum([0, wf, wf, wf, hf, wc, wc, wc])
    qa, ka, va, fa, qb, kb, vb = [w_in[:, o[n]:o[n + 1]] for n in range(7)]
    q_scale = HEAD_DIM ** -0.5 * LOG2E
    wm = jnp.concatenate([qa * q_scale, ka, qb * q_scale, kb], axis=1).astype(BF16)
    wvt = jnp.concatenate([va, vb], axis=1).T.astype(BF16)
    wa = _pad_cols(fa, LANES).astype(BF16)
    main, vt, aux = _inproj(x, g_mix, wm, wvt, wa)

    q_aug, k_aug = _logcum(aux, _pad_cols(b_forget.reshape(1, hf), LANES), hf)
    o_a = _flash("fox", main, 0, main, wf, vt, 0, (q_aug, k_aug), hf)

    right = CK_EXT - rel_bias.shape[1] - (CHUNK + 1)
    ext = jnp.pad(rel_bias * LOG2E, ((0, 0), (CHUNK + 1, right)), mode="edge")
    o_b = _chunk_attn(main, 2 * wf, 2 * wf + wc, vt, wf, ext, hc)

    y = _mlp(x.reshape(B * S, D), o_a.reshape(B * S, wf), o_b.reshape(B * S, wc),
             w_out.astype(BF16), g_mlp, w_up.astype(BF16), w_down.astype(BF16), g_final,
             final_norm)
    return y.reshape(B, S, D)


def _odd_layer(x, positions, g_mix, w_in, q_norm, kv_norm, w_uq, w_ukv, w_out, g_mlp, w_up,
               w_down, g_final, final_norm):
    B, S, D = x.shape
    hm = w_ukv.shape[1] // (NOPE_DIM + HEAD_DIM)
    ws = w_in.shape[1] - Q_LORA - KV_LORA - ROPE_DIM
    hs = (ws // 3) // HEAD_DIM
    wsb = hs * HEAD_DIM
    o = np.cumsum([0, wsb, wsb, wsb, Q_LORA, KV_LORA, ROPE_DIM])
    qc, kc, vc, w_cq, w_ckv, w_kr = [w_in[:, o[n]:o[n + 1]] for n in range(6)]
    wm = jnp.concatenate([qc * HEAD_DIM ** -0.5, kc], axis=1).astype(BF16)
    wa = jnp.concatenate([w_cq, w_ckv, _rope_slab(w_kr), _rope_slab(_rot_cols(w_kr))],
                         axis=1).astype(BF16)
    main, vt, aux = _inproj(x, g_mix, wm, vc.T.astype(BF16), wa)
    o_c = _sb_attn(main, 0, wsb, vt, 0, hs)

    dq = NOPE_DIM + ROPE_DIM
    wuq3 = w_uq.reshape(Q_LORA, hm, dq)
    nope, ropew = wuq3[:, :, :NOPE_DIM], wuq3[:, :, NOPE_DIM:]
    zq = jnp.zeros((Q_LORA, hm, LANES - dq), w_uq.dtype)
    wuq = jnp.concatenate([nope, ropew, zq], axis=2).reshape(Q_LORA, hm * LANES).astype(BF16)
    half = ROPE_DIM // 2
    ropr = jnp.concatenate([-ropew[:, :, half:], ropew[:, :, :half]], axis=2)
    wuqr = jnp.concatenate([jnp.zeros_like(nope), ropr, zq], axis=2)
    wuqr = wuqr.reshape(Q_LORA, hm * LANES).astype(BF16)
    wkv3 = w_ukv.reshape(KV_LORA, hm, NOPE_DIM + HEAD_DIM)
    wk = jnp.concatenate([wkv3[:, :, :NOPE_DIM],
                          jnp.zeros((KV_LORA, hm, LANES - NOPE_DIM), w_ukv.dtype)], axis=2)
    wk = wk.reshape(KV_LORA, hm * LANES).astype(BF16)
    wv_t = wkv3[:, :, NOPE_DIM:].reshape(KV_LORA, hm * HEAD_DIM).T.astype(BF16)
    freqs = (ROPE_THETA ** (-jnp.arange(half, dtype=F32) / half))
    invf = jnp.tile(freqs, 2 * LANES // ROPE_DIM).reshape(1, LANES)
    groups = LANES // ROPE_DIM
    pos = positions.astype(F32).reshape(B, S // MLA_TM, groups, MLA_TM // groups)
    pos = jnp.repeat(jnp.swapaxes(pos, 2, 3), ROPE_DIM, axis=-1)
    qm, km, vtm = _mla_prep(aux, pos, invf, q_norm.reshape(1, Q_LORA),
                            kv_norm.reshape(1, KV_LORA), wuq, wuqr, wk, wv_t,
                            dq ** -0.5 * LOG2E)
    o_d = _flash("mla", qm, 0, km, 0, vtm, 0, None, hm)

    y = _mlp(x.reshape(B * S, D), o_c.reshape(B * S, wsb), o_d.reshape(B * S, hm * HEAD_DIM),
             w_out.astype(BF16), g_mlp, w_up.astype(BF16), w_down.astype(BF16), g_final,
             final_norm)
    return y.reshape(B, S, D)


def kernel(x, positions, norm_mix, norm_mlp, norm_final, w_in_ab, b_forget, rel_bias, w_out_ab,
           w_in_cd, q_norm, kv_norm, w_uq, w_ukv, w_out_cd, w_up, w_down):
    depth = norm_mix.shape[0]
    for layer in range(depth):
        last = layer == depth - 1
        if layer % 2 == 0:
            e = layer // 2
            x = _even_layer(x, norm_mix[layer], w_in_ab[e], b_forget[e], rel_bias[e], w_out_ab[e],
                            norm_mlp[layer], w_up[layer], w_down[layer], norm_final, last)
        else:
            o = layer // 2
            x = _odd_layer(x, positions, norm_mix[layer], w_in_cd[o], q_norm[o], kv_norm[o],
                           w_uq[o], w_ukv[o], w_out_cd[o], norm_mlp[layer], w_up[layer],
                           w_down[layer], norm_final, last)
    return x
```

```python
import functools
import math

import numpy as np
import jax
import jax.numpy as jnp
from jax import lax
from jax.experimental import pallas as pl
from jax.experimental.pallas import tpu as pltpu

F32 = jnp.float32
BF16 = jnp.bfloat16

EPS = 1e-6
HEAD_DIM = 64
CHUNK = 64
N_LEFT_CHUNKS = 8
REL_CLIP = 256
ROPE_DIM = 32
NOPE_DIM = 64
ROPE_THETA = 10000.0
Q_LORA = 384
KV_LORA = 256

LANES = 128
VT_BLK = LANES
SUB = LANES
FLASH_HP = 4
FIXED_WIDTHS = (4, 2, 1)
SAFE_GAP = 80.0
NEG = -1e30
LOG2E = math.log2(math.e)
SB_ZERO_LOG = -104.0
VMEM_LIMIT = 56 * 1024 * 1024

_NT = (((1,), (1,)), ((), ()))


def _cparams(sem, flags=None):
    return pltpu.CompilerParams(dimension_semantics=sem, vmem_limit_bytes=VMEM_LIMIT, flags=flags)


def _rms_bf16(x, g):
    ms = jnp.mean(x * x, axis=-1, keepdims=True)
    return (x * lax.rsqrt(ms + EPS) * g).astype(BF16)


def _store_vt(ovt_ref, vt, row0):
    rows, tm = vt.shape
    for c in range(tm // VT_BLK):
        ovt_ref[c, row0:row0 + rows, :] = vt[:, c * VT_BLK:(c + 1) * VT_BLK]


def _inproj_kernel(x_ref, g_ref, wm_ref, wvt_ref, wa_ref, om_ref, ovt_ref, oa_ref):
    h = _rms_bf16(x_ref[...], g_ref[...])
    nm = om_ref.shape[-1]
    for c in range(0, nm, 512):
        om_ref[:, c:c + 512] = jnp.dot(
            h, wm_ref[:, c:c + 512], preferred_element_type=F32).astype(BF16)
    nv = wvt_ref.shape[0]
    for r in range(0, nv, 256):
        vt = lax.dot_general(wvt_ref[r:r + 256, :], h, _NT,
                             preferred_element_type=F32).astype(BF16)
        _store_vt(ovt_ref, vt, r)
    oa_ref[...] = jnp.dot(h, wa_ref[...], preferred_element_type=F32)


def _inproj(x, g, wm, wvt, wa, tm=512):
    B, S, D = x.shape
    nm, nv, na = wm.shape[1], wvt.shape[0], wa.shape[1]
    return pl.pallas_call(
        _inproj_kernel,
        grid=(B, S // tm),
        in_specs=[
            pl.BlockSpec((None, tm, D), lambda b, i: (b, i, 0)),
            pl.BlockSpec((1, D), lambda b, i: (0, 0)),
            pl.BlockSpec((D, nm), lambda b, i: (0, 0)),
            pl.BlockSpec((nv, D), lambda b, i: (0, 0)),
            pl.BlockSpec((D, na), lambda b, i: (0, 0)),
        ],
        out_specs=[
            pl.BlockSpec((None, tm, nm), lambda b, i: (b, i, 0)),
            pl.BlockSpec((None, tm // VT_BLK, nv, VT_BLK), lambda b, i: (b, i, 0, 0)),
            pl.BlockSpec((None, tm, na), lambda b, i: (b, i, 0)),
        ],
        out_shape=[
            jax.ShapeDtypeStruct((B, S, nm), BF16),
            jax.ShapeDtypeStruct((B, S // VT_BLK, nv, VT_BLK), BF16),
            jax.ShapeDtypeStruct((B, S, na), F32),
        ],
        compiler_params=_cparams(("arbitrary", "arbitrary")),
        name="inproj",
    )(x, g.reshape(1, D), wm, wvt, wa)


def _split3(x):
    hi = x.astype(BF16)
    r = x - hi.astype(F32)
    mid = r.astype(BF16)
    lo = (r - mid.astype(F32)).astype(BF16)
    return hi, mid, lo


AUG_W = 8


def _logcum_kernel(fa_ref, b_ref, pq_ref, pk_ref, oneq_ref, onek_ref, oq_ref, ok_ref, carry_ref):
    @pl.when(pl.program_id(1) == 0)
    def _():
        carry_ref[...] = jnp.zeros_like(carry_ref)

    z = fa_ref[...] + b_ref[...]
    lf = jnp.minimum(z, 0.0) - jnp.log(1.0 + jnp.exp(-jnp.abs(z)))
    tc = lf.shape[0]
    r = lax.broadcasted_iota(jnp.int32, (tc, tc), 0)
    c = lax.broadcasted_iota(jnp.int32, (tc, tc), 1)
    tri = jnp.where(r >= c, 1.0, 0.0).astype(BF16)
    cs = carry_ref[...]
    for part in _split3(lf):
        cs = cs + jnp.dot(tri, part, preferred_element_type=F32)
    carry_ref[...] = cs[tc - 1:tc, :]
    qa, ka = oneq_ref[...], onek_ref[...]
    for n, part in enumerate(_split3(cs * LOG2E)):
        qa = qa + jnp.dot(part, pq_ref[n], preferred_element_type=F32)
        ka = ka + jnp.dot(part, pk_ref[n], preferred_element_type=F32)
    oq_ref[...] = qa.astype(BF16)
    ok_ref[...] = ka.astype(BF16)


def _logcum(fa, bias, n_heads, tc=512):
    B, S, W = fa.shape
    na = LANES
    pq = np.zeros((3, W, na), np.float32)
    pk = np.zeros((3, W, na), np.float32)
    oneq = np.zeros((1, na), np.float32)
    onek = np.zeros((1, na), np.float32)
    for h in range(n_heads):
        base = h * AUG_W
        for n in range(3):
            pq[n, h, base + n] = 1.0
            pk[n, h, base + 3 + n] = -1.0
        oneq[0, base + 3:base + 6] = 1.0
        onek[0, base:base + 3] = 1.0
    const = lambda a: pl.BlockSpec(a.shape, lambda b, i: (0,) * a.ndim)
    args = [jnp.asarray(pq, BF16), jnp.asarray(pk, BF16), jnp.asarray(oneq), jnp.asarray(onek)]
    return pl.pallas_call(
        _logcum_kernel,
        grid=(B, S // tc),
        in_specs=[pl.BlockSpec((None, tc, W), lambda b, i: (b, i, 0)),
                  pl.BlockSpec((1, W), lambda b, i: (0, 0))] + [const(a) for a in args],
        out_specs=[pl.BlockSpec((None, tc, na), lambda b, i: (b, i, 0))] * 2,
        out_shape=[jax.ShapeDtypeStruct((B, S, na), BF16)] * 2,
        scratch_shapes=[pltpu.VMEM((1, W), F32)],
        compiler_params=_cparams(("arbitrary", "arbitrary")),
        name="logcum",
    )(fa, bias, *args)


def _pair_mask_q(q2, j):
    lane = lax.broadcasted_iota(jnp.int32, q2.shape, 1)
    keep = (lane >= HEAD_DIM * j) & (lane < HEAD_DIM * (j + 1))
    return jnp.where(keep, q2, jnp.zeros_like(q2))


ONES_ROWS = 16


def _softmax_step(tiles, vts, carry, tile_max=None):
    m, acc = carry
    if tile_max is not None:
        m_new = jnp.maximum(m, tile_max)
    else:
        m_new = m
        for tile in tiles:
            m_new = jnp.maximum(m_new, jnp.max(tile(), axis=0, keepdims=True))
    alpha = jnp.exp2(m - m_new)
    pv = None
    for tile, vt in zip(tiles, vts):
        p = jnp.exp2(tile() - m_new).astype(BF16)
        vt1 = jnp.concatenate([vt, jnp.ones((ONES_ROWS, vt.shape[1]), BF16)], axis=0)
        d = jnp.dot(vt1, p, preferred_element_type=F32)
        pv = d if pv is None else pv + d
    return m_new, alpha * acc + pv


def _softmax_init(bq):
    return (jnp.full((1, bq), NEG, F32), jnp.zeros((HEAD_DIM + ONES_ROWS, bq), F32))


def _softmax_out(carry):
    _, acc = carry
    return acc[0:HEAD_DIM] / acc[HEAD_DIM:HEAD_DIM + 1]


def _store_heads(o_ref, outs):
    oT = jnp.concatenate(outs, axis=0)
    o_ref[...] = oT.T.astype(o_ref.dtype)


def _flash_kernel(*refs, mode, hp, bq):
    if mode == "fox":
        q_ref, k_ref, vt_ref, qaug_ref, kaug_ref, o_ref = refs[:6]
    else:
        q_ref, k_ref, vt_ref, o_ref = refs[:4]
    sa_ref, sb_ref, ma_ref, mb_ref, qt_ref, kn_ref = refs[-6:]
    qs = pl.program_id(2) * bq
    sub = bq
    row = lax.broadcasted_iota(jnp.int32, (sub, bq), 0)
    col = lax.broadcasted_iota(jnp.int32, (sub, bq), 1)
    if mode == "fox":
        kcols = [slice(LANES * (h // 2), LANES * (h // 2 + 1)) for h in range(hp)]
        lane = lax.broadcasted_iota(jnp.int32, (bq, LANES), 1)
        qa = qaug_ref[...]
        qms = []
        for h in range(hp):
            first = AUG_W * (pl.program_id(1) * hp + h)
            own = (lane >= first) & (lane < first + AUG_W)
            qms.append(jnp.concatenate(
                [_pair_mask_q(q_ref[:, kcols[h]], h % 2),
                 jnp.where(own, qa, jnp.zeros_like(qa))], axis=1))
    else:
        kcols = [slice(LANES * h, LANES * (h + 1)) for h in range(hp)]
        qms = [q_ref[:, kcols[h]] for h in range(hp)]
    for h in range(hp):
        qt_ref[h] = qms[h].T

    @pl.when(pl.program_id(2) == 0)
    def _():
        klane = lax.broadcasted_iota(jnp.int32, (k_ref.shape[0], LANES), 1)
        for h in range(hp):
            kf = k_ref[:, kcols[h]].astype(F32)
            sq = kf * kf
            if mode == "fox":
                sq = jnp.where((klane >= HEAD_DIM * (h % 2)) & (klane < HEAD_DIM * (h % 2 + 1)),
                               sq, 0.0)
            kmax = jnp.sqrt(jnp.max(jnp.sum(sq, axis=1, keepdims=True), axis=0, keepdims=True))
            kn_ref[h] = jnp.broadcast_to(kmax, (1, bq))

    def scores(sb, h, masked):
        ks = pl.multiple_of(sb * sub, sub)
        k = k_ref[pl.ds(ks, sub), kcols[h]]
        if mode == "fox":
            k = jnp.concatenate([k, kaug_ref[pl.ds(ks, sub), :]], axis=1)
        sT = jnp.dot(k, qt_ref[h], preferred_element_type=F32)
        if masked and mode == "fox":
            sT = jnp.where(ks + row <= qs + col, sT, NEG)
        elif masked:
            sT = jnp.where(((ks + row) >> 6) <= ((qs + col) >> 6), sT, NEG)
        return sT

    nsub = bq // sub
    nvt = sub // VT_BLK

    def produce(buf, sb0, masked, h):
        s_buf, m_buf = buf
        tile_max = None
        for c in range(nsub):
            sT = scores(sb0 + c, h, masked)
            s_buf[h, c] = sT
            cm = jnp.max(sT, axis=0, keepdims=True)
            tile_max = cm if tile_max is None else jnp.maximum(tile_max, cm)
        m_buf[h] = tile_max

    def consume(buf, sb0, carry, h):
        s_buf, m_buf = buf
        tiles = [lambda c=c: s_buf[h, c] for c in range(nsub)]
        vts = [jnp.concatenate([vt_ref[(sb0 + c) * nvt + v, HEAD_DIM * h:HEAD_DIM * (h + 1), :]
                                for v in range(nvt)], axis=1) for c in range(nsub)]
        return _softmax_step(tiles, vts, carry, tile_max=m_buf[h])

    def stage(cur, cur_sb, nxt, nxt_sb, carries):
        if nxt is not None:
            for h in range(hp):
                produce(nxt, nxt_sb, False, h)
        return tuple(consume(cur, cur_sb, carries[h], h) for h in range(hp))

    n = pl.program_id(2)
    diag_sb = qs // sub
    buf_a, buf_b = (sa_ref, ma_ref), (sb_ref, mb_ref)
    for h in range(hp):
        produce(buf_a, diag_sb, True, h)

    def pair(j, carries):
        carries = stage(buf_a, jnp.where(j == 0, diag_sb, (2 * j - 1) * nsub),
                        buf_b, 2 * j * nsub, carries)
        return stage(buf_b, 2 * j * nsub,
                     buf_a, jnp.minimum(2 * j + 1, n - 1) * nsub, carries)

    def online(_):
        carries = tuple(_softmax_init(bq) for _ in range(hp))
        carries = lax.fori_loop(0, (n + 1) // 2, pair, carries)
        carries = lax.cond(
            n % 2 == 0,
            lambda c: stage(buf_a, jnp.where(n == 0, diag_sb, (n - 1) * nsub), None, None, c),
            lambda c: c, carries)
        return jnp.concatenate([_softmax_out(c) for c in carries], axis=0)

    refs_ = []
    gap = None
    for h in range(hp):
        qf = qt_ref[h, 0:LANES, :].astype(F32)
        bound = jnp.sqrt(jnp.sum(qf * qf, axis=0, keepdims=True)) * kn_ref[h] * 1.01 + 1e-3
        refs_.append(bound)
        g = jnp.max(bound - ma_ref[h])
        gap = g if gap is None else jnp.maximum(gap, g)

    def weigh_add(acc, h, s, sb0, nblk):
        p = jnp.exp2(s - refs_[h]).astype(BF16)
        vt = jnp.concatenate([vt_ref[sb0 * nvt + v, HEAD_DIM * h:HEAD_DIM * (h + 1), :]
                              for v in range(nblk * nvt)], axis=1)
        vt1 = jnp.concatenate([vt, jnp.ones((ONES_ROWS, nblk * sub), BF16)], axis=0)
        return acc + jnp.dot(vt1, p, preferred_element_type=F32)

    def fixed_reference(_):
        def run(kb, accs, nblk):
            ks = pl.multiple_of(kb * sub, sub)
            ss = []
            for h in range(hp):
                k = k_ref[pl.ds(ks, nblk * sub), kcols[h]]
                if mode == "fox":
                    k = jnp.concatenate([k, kaug_ref[pl.ds(ks, nblk * sub), :]], axis=1)
                ss.append(jnp.dot(k, qt_ref[h], preferred_element_type=F32))
            return tuple(weigh_add(accs[h], h, ss[h], kb, nblk) for h in range(hp))

        accs = tuple(jnp.zeros((HEAD_DIM + ONES_ROWS, bq), F32) for _ in range(hp))
        done = 0
        for width in FIXED_WIDTHS:
            trips = (n - done) // width
            accs = lax.fori_loop(0, trips,
                                 lambda j, a, done=done, width=width: run(done + j * width, a, width),
                                 accs)
            done = done + trips * width
        accs = [weigh_add(accs[h], h, sa_ref[h, 0], diag_sb, 1) for h in range(hp)]
        return jnp.concatenate([a[0:HEAD_DIM] / a[HEAD_DIM:HEAD_DIM + 1] for a in accs], axis=0)

    oT = lax.cond(gap <= SAFE_GAP, fixed_reference, online, None)
    o_ref[...] = oT.T.astype(o_ref.dtype)


def _flash(mode, q_arr, q_col0, k_arr, k_col0, vt_arr, vt_row0, extra, n_heads,
           hp=FLASH_HP, bq=256):
    B, S, _ = q_arr.shape
    qw = (HEAD_DIM if mode == "fox" else LANES) * hp
    vw = HEAD_DIM * hp
    in_specs = [
        pl.BlockSpec((None, bq, qw), lambda b, g, i: (b, i, q_col0 // qw + g)),
        pl.BlockSpec((None, S, qw), lambda b, g, i: (b, 0, k_col0 // qw + g)),
        pl.BlockSpec((None, S // VT_BLK, vw, VT_BLK),
                     lambda b, g, i: (b, 0, vt_row0 // vw + g, 0)),
    ]
    args = [q_arr, k_arr, vt_arr]
    if mode == "fox":
        q_aug, k_aug = extra
        in_specs += [
            pl.BlockSpec((None, bq, LANES), lambda b, g, i: (b, i, 0)),
            pl.BlockSpec((None, S, LANES), lambda b, g, i: (b, 0, 0)),
        ]
        args += [q_aug, k_aug]
    return pl.pallas_call(
        functools.partial(_flash_kernel, mode=mode, hp=hp, bq=bq),
        grid=(B, n_heads // hp, S // bq),
        in_specs=in_specs,
        out_specs=pl.BlockSpec((None, bq, vw), lambda b, g, i: (b, i, g)),
        out_shape=jax.ShapeDtypeStruct((B, S, n_heads * HEAD_DIM), BF16),
        scratch_shapes=([pltpu.VMEM((hp, 1, bq, bq), F32)] * 2
                        + [pltpu.VMEM((hp, 1, bq), F32)] * 2
                        + [pltpu.VMEM((hp, 2 * LANES if mode == "fox" else LANES, bq), BF16),
                           pltpu.VMEM((hp, 1, bq), F32)]),
        compiler_params=_cparams(("arbitrary", "arbitrary", "arbitrary")),
        name="flash_" + mode,
    )(*args)


CK_B = 2 * CHUNK
CK_NW = N_LEFT_CHUNKS * CHUNK // CK_B + 1
CK_EXT = (CK_NW + 1) * CK_B


def _chunk_kernel(q_ref, k_ref, vt_ref, ext_ref, o_ref, tab_ref, s_ref, *, hp, nq):
    i = pl.program_id(1)

    @pl.when(i == 0)
    def _():
        jj = lax.broadcasted_iota(jnp.int32, (CK_B, CK_B), 0)
        rr = lax.broadcasted_iota(jnp.int32, (CK_B, CK_B), 1)
        for h in range(hp):
            for w in range(CK_NW):
                a = (CK_NW - 1 - w) * CK_B
                g = jnp.broadcast_to(ext_ref[h:h + 1, a:a + 2 * CK_B], (CK_B, 2 * CK_B))
                t = pltpu.roll(g, CK_B, 1, stride=1, stride_axis=0)[:, :CK_B]
                if w == 0:
                    t = jnp.where((rr >= CHUNK) & (jj < CHUNK), NEG, t)
                if w == CK_NW - 1:
                    t = jnp.where((rr < CHUNK) & (jj >= CHUNK), NEG, t)
                tab_ref[h, w * CK_B:(w + 1) * CK_B, :] = t

    kcols = [slice(LANES * (h // 2), LANES * (h // 2 + 1)) for h in range(hp)]
    firsts = [i * nq + u - (CK_NW - 1) for u in range(nq)]

    def finish():
        for u in range(nq):
            kbc = [jnp.maximum(firsts[u] + w, 0) for w in range(CK_NW)]
            outs = []
            for h in range(hp):
                vt = jnp.concatenate([vt_ref[kbc[w], HEAD_DIM * h:HEAD_DIM * (h + 1), :]
                                      for w in range(CK_NW)], axis=1)
                outs.append(_softmax_out(_softmax_step([lambda u=u, h=h: s_ref[u, h]], [vt],
                                                       _softmax_init(CK_B))))
            oT = jnp.concatenate(outs, axis=0)
            o_ref[u * CK_B:(u + 1) * CK_B, :] = oT.T.astype(o_ref.dtype)

    def pair_scores(u, p, ks, nrows):
        q2 = q_ref[u * CK_B:(u + 1) * CK_B, kcols[2 * p]]
        qq = jnp.concatenate([_pair_mask_q(q2, 0), _pair_mask_q(q2, 1)], axis=0)
        return lax.dot_general(k_ref[pl.ds(ks, nrows), kcols[2 * p]], qq, _NT,
                               preferred_element_type=F32)

    @pl.when(firsts[0] >= 0)
    def _():
        for u in range(nq):
            ks = pl.multiple_of(firsts[u] * CK_B, CK_B)
            for p in range(hp // 2):
                sT = pair_scores(u, p, ks, CK_NW * CK_B)
                for j in range(2):
                    s_ref[u, 2 * p + j] = sT[:, j * CK_B:(j + 1) * CK_B] + tab_ref[2 * p + j]
        finish()

    @pl.when(firsts[0] < 0)
    def _():
        for u in range(nq):
            for p in range(hp // 2):
                for w in range(CK_NW):
                    rows = slice(w * CK_B, (w + 1) * CK_B)
                    ks = pl.multiple_of(jnp.maximum(firsts[u] + w, 0) * CK_B, CK_B)
                    sT = pair_scores(u, p, ks, CK_B)
                    for j in range(2):
                        s_ref[u, 2 * p + j, rows, :] = jnp.where(
                            firsts[u] + w >= 0,
                            sT[:, j * CK_B:(j + 1) * CK_B] + tab_ref[2 * p + j, rows, :], NEG)
        finish()


def _chunk_attn(main, q_col0, k_col0, vt_arr, vt_row0, ext, n_heads, nq=4):
    B, S, _ = main.shape
    hp = n_heads
    qw, vw = HEAD_DIM * hp, HEAD_DIM * hp
    return pl.pallas_call(
        functools.partial(_chunk_kernel, hp=hp, nq=nq),
        grid=(B, S // (nq * CK_B)),
        in_specs=[
            pl.BlockSpec((None, nq * CK_B, qw), lambda b, i: (b, i, q_col0 // qw)),
            pl.BlockSpec((None, S, qw), lambda b, i: (b, 0, k_col0 // qw)),
            pl.BlockSpec((None, S // VT_BLK, vw, VT_BLK), lambda b, i: (b, 0, vt_row0 // vw, 0)),
            pl.BlockSpec((hp, CK_EXT), lambda b, i: (0, 0)),
        ],
        out_specs=pl.BlockSpec((None, nq * CK_B, vw), lambda b, i: (b, i, 0)),
        out_shape=jax.ShapeDtypeStruct((B, S, n_heads * HEAD_DIM), BF16),
        scratch_shapes=[pltpu.VMEM((hp, CK_NW * CK_B, CK_B), F32),
                        pltpu.VMEM((nq, hp, CK_NW * CK_B, CK_B), F32)],
        compiler_params=_cparams(("arbitrary", "arbitrary")),
        name="chunk_attn",
    )(main, main, vt_arr, ext)


def _sb_kernel(q_ref, k_ref, vt_ref, o_ref, z_ref, lb_ref, sfx_ref, *, hp, bq):
    qs = pl.program_id(1) * bq
    nsub = bq // SUB
    row = lax.broadcasted_iota(jnp.int32, (SUB, bq), 0)
    col = lax.broadcasted_iota(jnp.int32, (SUB, bq), 1)
    ur = lax.broadcasted_iota(jnp.int32, (SUB, 2 * SUB), 0)
    uc = lax.broadcasted_iota(jnp.int32, (SUB, 2 * SUB), 1) & (SUB - 1)
    upper2 = jnp.where(uc > ur, 1.0, 0.0).astype(BF16)
    kcols = [slice(LANES * (h // 2), LANES * (h // 2 + 1)) for h in range(hp)]
    qms = [_pair_mask_q(q_ref[:, kcols[h]], h % 2) for h in range(hp)]

    def step(kb, carries, masked):
        ks = pl.multiple_of(kb * bq, bq)
        for h in range(hp):
            z_ref[h] = lax.dot_general(k_ref[pl.ds(ks, bq), kcols[h]], qms[h], _NT,
                                       preferred_element_type=F32)
        totals = []
        for h in range(hp):
            tot = []
            for c in range(nsub):
                rows = slice(c * SUB, (c + 1) * SUB)
                z = z_ref[h, rows, :]
                l1 = jnp.log(1.0 + jnp.exp(-jnp.abs(z)))
                log_beta = jnp.minimum(z, 0.0) - l1
                log_keep = log_beta - z
                if masked:
                    log_keep = jnp.where(ks + c * SUB + row < qs + col, log_keep, 0.0)
                lb_ref[h, rows, :] = log_beta
                hi = log_keep.astype(BF16)
                lo = (log_keep - hi.astype(F32)).astype(BF16)
                sfx = jnp.dot(upper2, jnp.concatenate([hi, lo], axis=0),
                              preferred_element_type=F32)
                sfx_ref[h, rows, :] = sfx
                tot.append(sfx[0:1, :] + log_keep[0:1, :])
            totals.append(tot)
        out = []
        for h in range(hp):
            tail, acc = carries[h]
            parts = [None] * nsub
            for c in range(nsub - 1, -1, -1):
                rows = slice(c * SUB, (c + 1) * SUB)
                a = jnp.exp(lb_ref[h, rows, :] + sfx_ref[h, rows, :] + tail)
                if masked:
                    a = jnp.where(ks + c * SUB + row < qs + col, a, 0.0)
                parts[c] = a.astype(BF16)
                tail = tail + totals[h][c]
            vt = jnp.concatenate([vt_ref[kb * nsub + c, HEAD_DIM * h:HEAD_DIM * (h + 1), :]
                                  for c in range(nsub)], axis=1)
            acc = acc + jnp.dot(vt, jnp.concatenate(parts, axis=0), preferred_element_type=F32)
            out.append((tail, acc))
        return tuple(out)

    n_full = qs // bq
    carries = tuple((jnp.zeros((1, bq), F32), jnp.zeros((HEAD_DIM, bq), F32))
                    for _ in range(hp))
    carries = step(n_full, carries, True)

    def cond(state):
        kb, carries = state
        tail_max = carries[0][0]
        for h in range(1, hp):
            tail_max = jnp.maximum(tail_max, carries[h][0])
        return (kb >= 0) & (jnp.max(tail_max) > SB_ZERO_LOG)

    def body(state):
        kb, carries = state
        return kb - 1, step(kb, carries, False)

    _, carries = lax.while_loop(cond, body, (n_full - 1, carries))
    _store_heads(o_ref, [acc for (_, acc) in carries])


def _sb_attn(main, q_col0, k_col0, vt_arr, vt_row0, n_heads, bq=256):
    B, S, _ = main.shape
    hp = n_heads
    qw = HEAD_DIM * hp
    return pl.pallas_call(
        functools.partial(_sb_kernel, hp=hp, bq=bq),
        grid=(B, S // bq),
        in_specs=[
            pl.BlockSpec((None, bq, qw), lambda b, i: (b, i, q_col0 // qw)),
            pl.BlockSpec((None, S, qw), lambda b, i: (b, 0, k_col0 // qw)),
            pl.BlockSpec((None, S // VT_BLK, qw, VT_BLK), lambda b, i: (b, 0, vt_row0 // qw, 0)),
        ],
        out_specs=pl.BlockSpec((None, bq, qw), lambda b, i: (b, i, 0)),
        out_shape=jax.ShapeDtypeStruct((B, S, n_heads * HEAD_DIM), BF16),
        scratch_shapes=[pltpu.VMEM((hp, bq, bq), F32)] * 3,
        compiler_params=_cparams(("arbitrary", "arbitrary")),
        name="sb_attn",
    )(main, main, vt_arr)


def _mla_prep_kernel(aux_ref, pos_ref, invf_ref, qn_ref, kvn_ref, wuq_ref, wuqr_ref,
                     wk_ref, wvt_ref, oq_ref, ok_ref, ovt_ref, *, q_scale):
    ang = pos_ref[...] * invf_ref[...]
    cos4, sin4 = jnp.cos(ang), jnp.sin(ang)
    lane = lax.broadcasted_iota(jnp.int32, ang.shape, 1)
    rotary = (lane >= NOPE_DIM) & (lane < NOPE_DIM + ROPE_DIM)
    cos_rows, sin_rows = [], []
    for m in range(LANES // ROPE_DIM):
        shift = (NOPE_DIM - ROPE_DIM * m) % LANES
        cm = cos4 if shift == 0 else pltpu.roll(cos4, shift, 1)
        sm = sin4 if shift == 0 else pltpu.roll(sin4, shift, 1)
        cos_rows.append(jnp.where(rotary, cm, 1.0))
        sin_rows.append(jnp.where(rotary, sm, 0.0))
    cos = jnp.concatenate(cos_rows, axis=0)
    sin = jnp.concatenate(sin_rows, axis=0)
    cq = _rms_bf16(aux_ref[:, 0:Q_LORA], qn_ref[...])
    ckv = _rms_bf16(aux_ref[:, Q_LORA:Q_LORA + KV_LORA], kvn_ref[...])
    o = Q_LORA + KV_LORA
    k_rope = aux_ref[:, o:o + LANES] * cos + aux_ref[:, o + LANES:o + 2 * LANES] * sin
    cos_q, sin_q = cos * q_scale, sin * q_scale
    n_heads = oq_ref.shape[-1] // LANES
    for h in range(0, n_heads, 2):
        cols = slice(h * LANES, (h + 2) * LANES)
        qa = jnp.dot(cq, wuq_ref[:, cols], preferred_element_type=F32)
        qb = jnp.dot(cq, wuqr_ref[:, cols], preferred_element_type=F32)
        kn = jnp.dot(ckv, wk_ref[:, cols], preferred_element_type=F32)
        for d in range(2):
            c1 = slice(d * LANES, (d + 1) * LANES)
            c2 = slice((h + d) * LANES, (h + d + 1) * LANES)
            oq_ref[:, c2] = (qa[:, c1] * cos_q + qb[:, c1] * sin_q).astype(BF16)
            ok_ref[:, c2] = (kn[:, c1] + k_rope).astype(BF16)
    nv = wvt_ref.shape[0]
    for r in range(0, nv, 256):
        vt = lax.dot_general(wvt_ref[r:r + 256, :], ckv, _NT,
                             preferred_element_type=F32).astype(BF16)
        _store_vt(ovt_ref, vt, r)


MLA_TM = 512


def _mla_prep(aux, pos, invf, qn, kvn, wuq, wuqr, wk, wvt, q_scale, tm=MLA_TM):
    B, S, na = aux.shape
    nq, nv = wuq.shape[1], wvt.shape[0]
    full = lambda a: pl.BlockSpec(a.shape, lambda b, i: (0,) * a.ndim)
    return pl.pallas_call(
        functools.partial(_mla_prep_kernel, q_scale=q_scale),
        grid=(B, S // tm),
        in_specs=[
            pl.BlockSpec((None, tm, na), lambda b, i: (b, i, 0)),
            pl.BlockSpec((None, None) + pos.shape[2:], lambda b, i: (b, i, 0, 0)),
            full(invf), full(qn), full(kvn), full(wuq), full(wuqr), full(wk), full(wvt),
        ],
        out_specs=[
            pl.BlockSpec((None, tm, nq), lambda b, i: (b, i, 0)),
            pl.BlockSpec((None, tm, nq), lambda b, i: (b, i, 0)),
            pl.BlockSpec((None, tm // VT_BLK, nv, VT_BLK), lambda b, i: (b, i, 0, 0)),
        ],
        out_shape=[
            jax.ShapeDtypeStruct((B, S, nq), BF16),
            jax.ShapeDtypeStruct((B, S, nq), BF16),
            jax.ShapeDtypeStruct((B, S // VT_BLK, nv, VT_BLK), BF16),
        ],
        compiler_params=_cparams(("arbitrary", "arbitrary")),
        name="mla_prep",
    )(aux, pos, invf, qn, kvn, wuq, wuqr, wk, wvt)


def _mlp_kernel(x_ref, oa_ref, ob_ref, wo_ref, g_ref, wu_ref, wd_ref, gf_ref, out_ref,
                x1_ref, h_ref, a_ref, *, final_norm, tf):
    na = oa_ref.shape[-1]
    x1 = (x_ref[...]
          + jnp.dot(oa_ref[...], wo_ref[0:na, :], preferred_element_type=F32)
          + jnp.dot(ob_ref[...], wo_ref[na:, :], preferred_element_type=F32))
    x1_ref[...] = x1
    h_ref[...] = _rms_bf16(x1, g_ref[...])
    for f in range(0, wu_ref.shape[1], tf):
        u = jnp.dot(h_ref[...], wu_ref[:, f:f + tf], preferred_element_type=F32)
        a_ref[:, f:f + tf] = jnp.square(jnp.maximum(u, 0.0)).astype(BF16)
    y = x1_ref[...] + jnp.dot(a_ref[...], wd_ref[...], preferred_element_type=F32)
    if final_norm:
        ms = jnp.mean(y * y, axis=-1, keepdims=True)
        y = y * lax.rsqrt(ms + EPS) * gf_ref[...]
    out_ref[...] = y


def _mlp(x2, oa, ob, wo, g, wu, wd, gf, final_norm, tm=512, tf=1024):
    T, D = x2.shape
    F = wu.shape[1]
    na, nb = oa.shape[1], ob.shape[1]
    const = lambda shape: pl.BlockSpec(shape, lambda i: (0, 0), pipeline_mode=pl.Buffered(1))
    return pl.pallas_call(
        functools.partial(_mlp_kernel, final_norm=final_norm, tf=tf),
        grid=(T // tm,),
        in_specs=[
            pl.BlockSpec((tm, D), lambda i: (i, 0)),
            pl.BlockSpec((tm, na), lambda i: (i, 0)),
            pl.BlockSpec((tm, nb), lambda i: (i, 0)),
            const((na + nb, D)),
            const((1, D)),
            const((D, F)),
            const((F, D)),
            const((1, D)),
        ],
        out_specs=pl.BlockSpec((tm, D), lambda i: (i, 0)),
        out_shape=jax.ShapeDtypeStruct((T, D), F32),
        scratch_shapes=[pltpu.VMEM((tm, D), F32), pltpu.VMEM((tm, D), BF16),
                        pltpu.VMEM((tm, F), BF16)],
        compiler_params=_cparams(("arbitrary",)),
        name="mlp",
    )(x2, oa, ob, wo, g.reshape(1, D), wu, wd, gf.reshape(1, D))


def _pad_cols(w, n):
    return jnp.pad(w, ((0, 0), (0, n - w.shape[1])))


def _rot_cols(w):
    half = ROPE_DIM // 2
    return jnp.concatenate([-w[:, half:], w[:, :half]], axis=1)


def _rope_slab(w):
    z = jnp.zeros((w.shape[0], NOPE_DIM), w.dtype)
    return jnp.concatenate([z, w, jnp.zeros((w.shape[0], LANES - NOPE_DIM - ROPE_DIM), w.dtype)], axis=1)


def _even_layer(x, g_mix, w_in, b_forget, rel_bias, w_out, g_mlp, w_up, w_down, g_final,
                final_norm):
    B, S, D = x.shape
    hf, hc = b_forget.shape[0], rel_bias.shape[0]
    wf, wc = hf * HEAD_DIM, hc * HEAD_DIM
    o = np.cumsum([0, wf, wf, wf, hf, wc, wc, wc])
    qa, ka, va, fa, qb, kb, vb = [w_in[:, o[n]:o[n + 1]] for n in range(7)]
    q_scale = HEAD_DIM ** -0.5 * LOG2E
    wm = jnp.concatenate([qa * q_scale, ka, qb * q_scale, kb], axis=1).astype(BF16)
    wvt = jnp.concatenate([va, vb], axis=1).T.astype(BF16)
    wa = _pad_cols(fa, LANES).astype(BF16)
    main, vt, aux = _inproj(x, g_mix, wm, wvt, wa)

    q_aug, k_aug = _logcum(aux, _pad_cols(b_forget.reshape(1, hf), LANES), hf)
    o_a = _flash("fox", main, 0, main, wf, vt, 0, (q_aug, k_aug), hf)

    right = CK_EXT - rel_bias.shape[1] - (CHUNK + 1)
    ext = jnp.pad(rel_bias * LOG2E, ((0, 0), (CHUNK + 1, right)), mode="edge")
    o_b = _chunk_attn(main, 2 * wf, 2 * wf + wc, vt, wf, ext, hc)

    y = _mlp(x.reshape(B * S, D), o_a.reshape(B * S, wf), o_b.reshape(B * S, wc),
             w_out.astype(BF16), g_mlp, w_up.astype(BF16), w_down.astype(BF16), g_final,
             final_norm)
    return y.reshape(B, S, D)


def _odd_layer(x, positions, g_mix, w_in, q_norm, kv_norm, w_uq, w_ukv, w_out, g_mlp, w_up,
               w_down, g_final, final_norm):
    B, S, D = x.shape
    hm = w_ukv.shape[1] // (NOPE_DIM + HEAD_DIM)
    ws = w_in.shape[1] - Q_LORA - KV_LORA - ROPE_DIM
    hs = (ws // 3) // HEAD_DIM
    wsb = hs * HEAD_DIM
    o = np.cumsum([0, wsb, wsb, wsb, Q_LORA, KV_LORA, ROPE_DIM])
    qc, kc, vc, w_cq, w_ckv, w_kr = [w_in[:, o[n]:o[n + 1]] for n in range(6)]
    wm = jnp.concatenate([qc * HEAD_DIM ** -0.5, kc], axis=1).astype(BF16)
    wa = jnp.concatenate([w_cq, w_ckv, _rope_slab(w_kr), _rope_slab(_rot_cols(w_kr))],
                         axis=1).astype(BF16)
    main, vt, aux = _inproj(x, g_mix, wm, vc.T.astype(BF16), wa)
    o_c = _sb_attn(main, 0, wsb, vt, 0, hs)

    dq = NOPE_DIM + ROPE_DIM
    wuq3 = w_uq.reshape(Q_LORA, hm, dq)
    nope, ropew = wuq3[:, :, :NOPE_DIM], wuq3[:, :, NOPE_DIM:]
    zq = jnp.zeros((Q_LORA, hm, LANES - dq), w_uq.dtype)
    wuq = jnp.concatenate([nope, ropew, zq], axis=2).reshape(Q_LORA, hm * LANES).astype(BF16)
    half = ROPE_DIM // 2
    ropr = jnp.concatenate([-ropew[:, :, half:], ropew[:, :, :half]], axis=2)
    wuqr = jnp.concatenate([jnp.zeros_like(nope), ropr, zq], axis=2)
    wuqr = wuqr.reshape(Q_LORA, hm * LANES).astype(BF16)
    wkv3 = w_ukv.reshape(KV_LORA, hm, NOPE_DIM + HEAD_DIM)
    wk = jnp.concatenate([wkv3[:, :, :NOPE_DIM],
                          jnp.zeros((KV_LORA, hm, LANES - NOPE_DIM), w_ukv.dtype)], axis=2)
    wk = wk.reshape(KV_LORA, hm * LANES).astype(BF16)
    wv_t = wkv3[:, :, NOPE_DIM:].reshape(KV_LORA, hm * HEAD_DIM).T.astype(BF16)
    freqs = (ROPE_THETA ** (-jnp.arange(half, dtype=F32) / half))
    invf = jnp.tile(freqs, 2 * LANES // ROPE_DIM).reshape(1, LANES)
    groups = LANES // ROPE_DIM
    pos = positions.astype(F32).reshape(B, S // MLA_TM, groups, MLA_TM // groups)
    pos = jnp.repeat(jnp.swapaxes(pos, 2, 3), ROPE_DIM, axis=-1)
    qm, km, vtm = _mla_prep(aux, pos, invf, q_norm.reshape(1, Q_LORA),
                            kv_norm.reshape(1, KV_LORA), wuq, wuqr, wk, wv_t,
                            dq ** -0.5 * LOG2E)
    o_d = _flash("mla", qm, 0, km, 0, vtm, 0, None, hm)

    y = _mlp(x.reshape(B * S, D), o_c.reshape(B * S, wsb), o_d.reshape(B * S, hm * HEAD_DIM),
             w_out.astype(BF16), g_mlp, w_up.astype(BF16), w_down.astype(BF16), g_final,
             final_norm)
    return y.reshape(B, S, D)


def kernel(x, positions, norm_mix, norm_mlp, norm_final, w_in_ab, b_forget, rel_bias, w_out_ab,
           w_in_cd, q_norm, kv_norm, w_uq, w_ukv, w_out_cd, w_up, w_down):
    depth = norm_mix.shape[0]
    for layer in range(depth):
        last = layer == depth - 1
        if layer % 2 == 0:
            e = layer // 2
            x = _even_layer(x, norm_mix[layer], w_in_ab[e], b_forget[e], rel_bias[e], w_out_ab[e],
                            norm_mlp[layer], w_up[layer], w_down[layer], norm_final, last)
        else:
            o = layer // 2
            x = _odd_layer(x, positions, norm_mix[layer], w_in_cd[o], q_norm[o], kv_norm[o],
                           w_uq[o], w_ukv[o], w_out_cd[o], norm_mlp[layer], w_up[layer],
                           w_down[layer], norm_final, last)
    return x
```

```python
import functools
import math

import numpy as np
import jax
import jax.numpy as jnp
from jax import lax
from jax.experimental import pallas as pl
from jax.experimental.pallas import tpu as pltpu

F32 = jnp.float32
BF16 = jnp.bfloat16

EPS = 1e-6
HEAD_DIM = 64
CHUNK = 64
N_LEFT_CHUNKS = 8
REL_CLIP = 256
ROPE_DIM = 32
NOPE_DIM = 64
ROPE_THETA = 10000.0
Q_LORA = 384
KV_LORA = 256

LANES = 128
VT_BLK = LANES
SUB = LANES
FLASH_HP = 4
FIXED_WIDTHS = (4, 2, 1)
SAFE_GAP = 80.0
NEG = -1e30
LOG2E = math.log2(math.e)
SB_ZERO_LOG = -104.0
VMEM_LIMIT = 56 * 1024 * 1024

_NT = (((1,), (1,)), ((), ()))


def _cparams(sem, flags=None):
    return pltpu.CompilerParams(dimension_semantics=sem, vmem_limit_bytes=VMEM_LIMIT, flags=flags)


def _rms_bf16(x, g):
    ms = jnp.mean(x * x, axis=-1, keepdims=True)
    return (x * lax.rsqrt(ms + EPS) * g).astype(BF16)


def _store_vt(ovt_ref, vt, row0):
    rows, tm = vt.shape
    for c in range(tm // VT_BLK):
        ovt_ref[c, row0:row0 + rows, :] = vt[:, c * VT_BLK:(c + 1) * VT_BLK]


def _inproj_kernel(x_ref, g_ref, wm_ref, wvt_ref, wa_ref, om_ref, ovt_ref, oa_ref):
    h = _rms_bf16(x_ref[...], g_ref[...])
    nm = om_ref.shape[-1]
    for c in range(0, nm, 512):
        om_ref[:, c:c + 512] = jnp.dot(
            h, wm_ref[:, c:c + 512], preferred_element_type=F32).astype(BF16)
    nv = wvt_ref.shape[0]
    for r in range(0, nv, 256):
        vt = lax.dot_general(wvt_ref[r:r + 256, :], h, _NT,
                             preferred_element_type=F32).astype(BF16)
        _store_vt(ovt_ref, vt, r)
    oa_ref[...] = jnp.dot(h, wa_ref[...], preferred_element_type=F32)


def _inproj(x, g, wm, wvt, wa, tm=512):
    B, S, D = x.shape
    nm, nv, na = wm.shape[1], wvt.shape[0], wa.shape[1]
    return pl.pallas_call(
        _inproj_kernel,
        grid=(B, S // tm),
        in_specs=[
            pl.BlockSpec((None, tm, D), lambda b, i: (b, i, 0)),
            pl.BlockSpec((1, D), lambda b, i: (0, 0)),
            pl.BlockSpec((D, nm), lambda b, i: (0, 0)),
            pl.BlockSpec((nv, D), lambda b, i: (0, 0)),
            pl.BlockSpec((D, na), lambda b, i: (0, 0)),
        ],
        out_specs=[
            pl.BlockSpec((None, tm, nm), lambda b, i: (b, i, 0)),
            pl.BlockSpec((None, tm // VT_BLK, nv, VT_BLK), lambda b, i: (b, i, 0, 0)),
            pl.BlockSpec((None, tm, na), lambda b, i: (b, i, 0)),
        ],
        out_shape=[
            jax.ShapeDtypeStruct((B, S, nm), BF16),
            jax.ShapeDtypeStruct((B, S // VT_BLK, nv, VT_BLK), BF16),
            jax.ShapeDtypeStruct((B, S, na), F32),
        ],
        compiler_params=_cparams(("arbitrary", "arbitrary")),
        name="inproj",
    )(x, g.reshape(1, D), wm, wvt, wa)


def _split3(x):
    hi = x.astype(BF16)
    r = x - hi.astype(F32)
    mid = r.astype(BF16)
    lo = (r - mid.astype(F32)).astype(BF16)
    return hi, mid, lo


AUG_W = 8


def _logcum_kernel(fa_ref, b_ref, pq_ref, pk_ref, oneq_ref, onek_ref, oq_ref, ok_ref, carry_ref):
    @pl.when(pl.program_id(1) == 0)
    def _():
        carry_ref[...] = jnp.zeros_like(carry_ref)

    z = fa_ref[...] + b_ref[...]
    lf = jnp.minimum(z, 0.0) - jnp.log(1.0 + jnp.exp(-jnp.abs(z)))
    tc = lf.shape[0]
    r = lax.broadcasted_iota(jnp.int32, (tc, tc), 0)
    c = lax.broadcasted_iota(jnp.int32, (tc, tc), 1)
    tri = jnp.where(r >= c, 1.0, 0.0).astype(BF16)
    cs = carry_ref[...]
    for part in _split3(lf):
        cs = cs + jnp.dot(tri, part, preferred_element_type=F32)
    carry_ref[...] = cs[tc - 1:tc, :]
    qa, ka = oneq_ref[...], onek_ref[...]
    for n, part in enumerate(_split3(cs * LOG2E)):
        qa = qa + jnp.dot(part, pq_ref[n], preferred_element_type=F32)
        ka = ka + jnp.dot(part, pk_ref[n], preferred_element_type=F32)
    oq_ref[...] = qa.astype(BF16)
    ok_ref[...] = ka.astype(BF16)


def _logcum(fa, bias, n_heads, tc=512):
    B, S, W = fa.shape
    na = LANES
    pq = np.zeros((3, W, na), np.float32)
    pk = np.zeros((3, W, na), np.float32)
    oneq = np.zeros((1, na), np.float32)
    onek = np.zeros((1, na), np.float32)
    for h in range(n_heads):
        base = h * AUG_W
        for n in range(3):
            pq[n, h, base + n] = 1.0
            pk[n, h, base + 3 + n] = -1.0
        oneq[0, base + 3:base + 6] = 1.0
        onek[0, base:base + 3] = 1.0
    const = lambda a: pl.BlockSpec(a.shape, lambda b, i: (0,) * a.ndim)
    args = [jnp.asarray(pq, BF16), jnp.asarray(pk, BF16), jnp.asarray(oneq), jnp.asarray(onek)]
    return pl.pallas_call(
        _logcum_kernel,
        grid=(B, S // tc),
        in_specs=[pl.BlockSpec((None, tc, W), lambda b, i: (b, i, 0)),
                  pl.BlockSpec((1, W), lambda b, i: (0, 0))] + [const(a) for a in args],
        out_specs=[pl.BlockSpec((None, tc, na), lambda b, i: (b, i, 0))] * 2,
        out_shape=[jax.ShapeDtypeStruct((B, S, na), BF16)] * 2,
        scratch_shapes=[pltpu.VMEM((1, W), F32)],
        compiler_params=_cparams(("arbitrary", "arbitrary")),
        name="logcum",
    )(fa, bias, *args)


def _pair_mask_q(q2, j):
    lane = lax.broadcasted_iota(jnp.int32, q2.shape, 1)
    keep = (lane >= HEAD_DIM * j) & (lane < HEAD_DIM * (j + 1))
    return jnp.where(keep, q2, jnp.zeros_like(q2))


ONES_ROWS = 16


def _softmax_step(tiles, vts, carry, tile_max=None):
    m, acc = carry
    if tile_max is not None:
        m_new = jnp.maximum(m, tile_max)
    else:
        m_new = m
        for tile in tiles:
            m_new = jnp.maximum(m_new, jnp.max(tile(), axis=0, keepdims=True))
    alpha = jnp.exp2(m - m_new)
    pv = None
    for tile, vt in zip(tiles, vts):
        p = jnp.exp2(tile() - m_new).astype(BF16)
        vt1 = jnp.concatenate([vt, jnp.ones((ONES_ROWS, vt.shape[1]), BF16)], axis=0)
        d = jnp.dot(vt1, p, preferred_element_type=F32)
        pv = d if pv is None else pv + d
    return m_new, alpha * acc + pv


def _softmax_init(bq):
    return (jnp.full((1, bq), NEG, F32), jnp.zeros((HEAD_DIM + ONES_ROWS, bq), F32))


def _softmax_out(carry):
    _, acc = carry
    return acc[0:HEAD_DIM] / acc[HEAD_DIM:HEAD_DIM + 1]


def _store_heads(o_ref, outs):
    oT = jnp.concatenate(outs, axis=0)
    o_ref[...] = oT.T.astype(o_ref.dtype)


def _flash_kernel(*refs, mode, hp, bq):
    if mode == "fox":
        q_ref, k_ref, vt_ref, qaug_ref, kaug_ref, o_ref = refs[:6]
    else:
        q_ref, k_ref, vt_ref, o_ref = refs[:4]
    sa_ref, sb_ref, ma_ref, mb_ref, qt_ref, kn_ref = refs[-6:]
    qs = pl.program_id(2) * bq
    sub = bq
    row = lax.broadcasted_iota(jnp.int32, (sub, bq), 0)
    col = lax.broadcasted_iota(jnp.int32, (sub, bq), 1)
    if mode == "fox":
        kcols = [slice(LANES * (h // 2), LANES * (h // 2 + 1)) for h in range(hp)]
        lane = lax.broadcasted_iota(jnp.int32, (bq, LANES), 1)
        qa = qaug_ref[...]
        qms = []
        for h in range(hp):
            first = AUG_W * (pl.program_id(1) * hp + h)
            own = (lane >= first) & (lane < first + AUG_W)
            qms.append(jnp.concatenate(
                [_pair_mask_q(q_ref[:, kcols[h]], h % 2),
                 jnp.where(own, qa, jnp.zeros_like(qa))], axis=1))
    else:
        kcols = [slice(LANES * h, LANES * (h + 1)) for h in range(hp)]
        qms = [q_ref[:, kcols[h]] for h in range(hp)]
    for h in range(hp):
        qt_ref[h] = qms[h].T

    @pl.when(pl.program_id(2) == 0)
    def _():
        klane = lax.broadcasted_iota(jnp.int32, (1, LANES), 1)
        for h in range(hp):
            kabs = jnp.max(jnp.abs(k_ref[:, kcols[h]].astype(F32)), axis=0, keepdims=True)
            sq = kabs * kabs
            if mode == "fox":
                sq = jnp.where((klane >= HEAD_DIM * (h % 2)) & (klane < HEAD_DIM * (h % 2 + 1)),
                               sq, 0.0)
            kn_ref[h] = jnp.broadcast_to(jnp.sqrt(jnp.sum(sq, axis=1, keepdims=True)), (1, bq))

    def scores(sb, h, masked):
        ks = pl.multiple_of(sb * sub, sub)
        k = k_ref[pl.ds(ks, sub), kcols[h]]
        if mode == "fox":
            k = jnp.concatenate([k, kaug_ref[pl.ds(ks, sub), :]], axis=1)
        sT = jnp.dot(k, qt_ref[h], preferred_element_type=F32)
        if masked and mode == "fox":
            sT = jnp.where(ks + row <= qs + col, sT, NEG)
        elif masked:
            sT = jnp.where(((ks + row) >> 6) <= ((qs + col) >> 6), sT, NEG)
        return sT

    nsub = bq // sub
    nvt = sub // VT_BLK

    def produce(buf, sb0, masked, h):
        s_buf, m_buf = buf
        tile_max = None
        for c in range(nsub):
            sT = scores(sb0 + c, h, masked)
            s_buf[h, c] = sT
            cm = jnp.max(sT, axis=0, keepdims=True)
            tile_max = cm if tile_max is None else jnp.maximum(tile_max, cm)
        m_buf[h] = tile_max

    def consume(buf, sb0, carry, h):
        s_buf, m_buf = buf
        tiles = [lambda c=c: s_buf[h, c] for c in range(nsub)]
        vts = [jnp.concatenate([vt_ref[(sb0 + c) * nvt + v, HEAD_DIM * h:HEAD_DIM * (h + 1), :]
                                for v in range(nvt)], axis=1) for c in range(nsub)]
        return _softmax_step(tiles, vts, carry, tile_max=m_buf[h])

    def stage(cur, cur_sb, nxt, nxt_sb, carries):
        if nxt is not None:
            for h in range(hp):
                produce(nxt, nxt_sb, False, h)
        return tuple(consume(cur, cur_sb, carries[h], h) for h in range(hp))

    n = pl.program_id(2)
    diag_sb = qs // sub
    buf_a, buf_b = (sa_ref, ma_ref), (sb_ref, mb_ref)
    for h in range(hp):
        produce(buf_a, diag_sb, True, h)

    def pair(j, carries):
        carries = stage(buf_a, jnp.where(j == 0, diag_sb, (2 * j - 1) * nsub),
                        buf_b, 2 * j * nsub, carries)
        return stage(buf_b, 2 * j * nsub,
                     buf_a, jnp.minimum(2 * j + 1, n - 1) * nsub, carries)

    def online(_):
        carries = tuple(_softmax_init(bq) for _ in range(hp))
        carries = lax.fori_loop(0, (n + 1) // 2, pair, carries)
        carries = lax.cond(
            n % 2 == 0,
            lambda c: stage(buf_a, jnp.where(n == 0, diag_sb, (n - 1) * nsub), None, None, c),
            lambda c: c, carries)
        return jnp.concatenate([_softmax_out(c) for c in carries], axis=0)

    refs_ = []
    gap = None
    for h in range(hp):
        qf = qt_ref[h, 0:LANES, :].astype(F32)
        bound = jnp.sqrt(jnp.sum(qf * qf, axis=0, keepdims=True)) * kn_ref[h] * 1.01 + 1e-3
        refs_.append(bound)
        g = jnp.max(bound - ma_ref[h])
        gap = g if gap is None else jnp.maximum(gap, g)

    def weigh_add(acc, h, s, sb0, nblk):
        p = jnp.exp2(s - refs_[h]).astype(BF16)
        vt = jnp.concatenate([vt_ref[sb0 * nvt + v, HEAD_DIM * h:HEAD_DIM * (h + 1), :]
                              for v in range(nblk * nvt)], axis=1)
        vt1 = jnp.concatenate([vt, jnp.ones((ONES_ROWS, nblk * sub), BF16)], axis=0)
        return acc + jnp.dot(vt1, p, preferred_element_type=F32)

    def fixed_reference(_):
        def run(kb, accs, nblk):
            ks = pl.multiple_of(kb * sub, sub)
            ss = []
            for h in range(hp):
                k = k_ref[pl.ds(ks, nblk * sub), kcols[h]]
                if mode == "fox":
                    k = jnp.concatenate([k, kaug_ref[pl.ds(ks, nblk * sub), :]], axis=1)
                ss.append(jnp.dot(k, qt_ref[h], preferred_element_type=F32))
            return tuple(weigh_add(accs[h], h, ss[h], kb, nblk) for h in range(hp))

        accs = tuple(jnp.zeros((HEAD_DIM + ONES_ROWS, bq), F32) for _ in range(hp))
        done = 0
        for width in FIXED_WIDTHS:
            trips = (n - done) // width
            accs = lax.fori_loop(0, trips,
                                 lambda j, a, done=done, width=width: run(done + j * width, a, width),
                                 accs)
            done = done + trips * width
        accs = [weigh_add(accs[h], h, sa_ref[h, 0], diag_sb, 1) for h in range(hp)]
        return jnp.concatenate([a[0:HEAD_DIM] / a[HEAD_DIM:HEAD_DIM + 1] for a in accs], axis=0)

    oT = lax.cond(gap <= SAFE_GAP, fixed_reference, online, None)
    o_ref[...] = oT.T.astype(o_ref.dtype)


def _flash(mode, q_arr, q_col0, k_arr, k_col0, vt_arr, vt_row0, extra, n_heads,
           hp=FLASH_HP, bq=256):
    B, S, _ = q_arr.shape
    qw = (HEAD_DIM if mode == "fox" else LANES) * hp
    vw = HEAD_DIM * hp
    in_specs = [
        pl.BlockSpec((None, bq, qw), lambda b, g, i: (b, i, q_col0 // qw + g)),
        pl.BlockSpec((None, S, qw), lambda b, g, i: (b, 0, k_col0 // qw + g)),
        pl.BlockSpec((None, S // VT_BLK, vw, VT_BLK),
                     lambda b, g, i: (b, 0, vt_row0 // vw + g, 0)),
    ]
    args = [q_arr, k_arr, vt_arr]
    if mode == "fox":
        q_aug, k_aug = extra
        in_specs += [
            pl.BlockSpec((None, bq, LANES), lambda b, g, i: (b, i, 0)),
            pl.BlockSpec((None, S, LANES), lambda b, g, i: (b, 0, 0)),
        ]
        args += [q_aug, k_aug]
    return pl.pallas_call(
        functools.partial(_flash_kernel, mode=mode, hp=hp, bq=bq),
        grid=(B, n_heads // hp, S // bq),
        in_specs=in_specs,
        out_specs=pl.BlockSpec((None, bq, vw), lambda b, g, i: (b, i, g)),
        out_shape=jax.ShapeDtypeStruct((B, S, n_heads * HEAD_DIM), BF16),
        scratch_shapes=([pltpu.VMEM((hp, 1, bq, bq), F32)] * 2
                        + [pltpu.VMEM((hp, 1, bq), F32)] * 2
                        + [pltpu.VMEM((hp, 2 * LANES if mode == "fox" else LANES, bq), BF16),
                           pltpu.VMEM((hp, 1, bq), F32)]),
        compiler_params=_cparams(("arbitrary", "arbitrary", "arbitrary")),
        name="flash_" + mode,
    )(*args)


CK_B = 2 * CHUNK
CK_NW = N_LEFT_CHUNKS * CHUNK // CK_B + 1
CK_EXT = (CK_NW + 1) * CK_B


def _chunk_kernel(q_ref, k_ref, vt_ref, ext_ref, o_ref, tab_ref, s_ref, *, hp, nq):
    i = pl.program_id(1)

    @pl.when(i == 0)
    def _():
        jj = lax.broadcasted_iota(jnp.int32, (CK_B, CK_B), 0)
        rr = lax.broadcasted_iota(jnp.int32, (CK_B, CK_B), 1)
        for h in range(hp):
            for w in range(CK_NW):
                a = (CK_NW - 1 - w) * CK_B
                g = jnp.broadcast_to(ext_ref[h:h + 1, a:a + 2 * CK_B], (CK_B, 2 * CK_B))
                t = pltpu.roll(g, CK_B, 1, stride=1, stride_axis=0)[:, :CK_B]
                if w == 0:
                    t = jnp.where((rr >= CHUNK) & (jj < CHUNK), NEG, t)
                if w == CK_NW - 1:
                    t = jnp.where((rr < CHUNK) & (jj >= CHUNK), NEG, t)
                tab_ref[h, w * CK_B:(w + 1) * CK_B, :] = t

    kcols = [slice(LANES * (h // 2), LANES * (h // 2 + 1)) for h in range(hp)]
    firsts = [i * nq + u - (CK_NW - 1) for u in range(nq)]

    def finish():
        for u in range(nq):
            kbc = [jnp.maximum(firsts[u] + w, 0) for w in range(CK_NW)]
            outs = []
            for h in range(hp):
                vt = jnp.concatenate([vt_ref[kbc[w], HEAD_DIM * h:HEAD_DIM * (h + 1), :]
                                      for w in range(CK_NW)], axis=1)
                outs.append(_softmax_out(_softmax_step([lambda u=u, h=h: s_ref[u, h]], [vt],
                                                       _softmax_init(CK_B))))
            oT = jnp.concatenate(outs, axis=0)
            o_ref[u * CK_B:(u + 1) * CK_B, :] = oT.T.astype(o_ref.dtype)

    def pair_scores(u, p, ks, nrows):
        q2 = q_ref[u * CK_B:(u + 1) * CK_B, kcols[2 * p]]
        qq = jnp.concatenate([_pair_mask_q(q2, 0), _pair_mask_q(q2, 1)], axis=0)
        return lax.dot_general(k_ref[pl.ds(ks, nrows), kcols[2 * p]], qq, _NT,
                               preferred_element_type=F32)

    @pl.when(firsts[0] >= 0)
    def _():
        for u in range(nq):
            ks = pl.multiple_of(firsts[u] * CK_B, CK_B)
            for p in range(hp // 2):
                sT = pair_scores(u, p, ks, CK_NW * CK_B)
                for j in range(2):
                    s_ref[u, 2 * p + j] = sT[:, j * CK_B:(j + 1) * CK_B] + tab_ref[2 * p + j]
        finish()

    @pl.when(firsts[0] < 0)
    def _():
        for u in range(nq):
            for p in range(hp // 2):
                for w in range(CK_NW):
                    rows = slice(w * CK_B, (w + 1) * CK_B)
                    ks = pl.multiple_of(jnp.maximum(firsts[u] + w, 0) * CK_B, CK_B)
                    sT = pair_scores(u, p, ks, CK_B)
                    for j in range(2):
                        s_ref[u, 2 * p + j, rows, :] = jnp.where(
                            firsts[u] + w >= 0,
                            sT[:, j * CK_B:(j + 1) * CK_B] + tab_ref[2 * p + j, rows, :], NEG)
        finish()


def _chunk_attn(main, q_col0, k_col0, vt_arr, vt_row0, ext, n_heads, nq=4):
    B, S, _ = main.shape
    hp = n_heads
    qw, vw = HEAD_DIM * hp, HEAD_DIM * hp
    return pl.pallas_call(
        functools.partial(_chunk_kernel, hp=hp, nq=nq),
        grid=(B, S // (nq * CK_B)),
        in_specs=[
            pl.BlockSpec((None, nq * CK_B, qw), lambda b, i: (b, i, q_col0 // qw)),
            pl.BlockSpec((None, S, qw), lambda b, i: (b, 0, k_col0 // qw)),
            pl.BlockSpec((None, S // VT_BLK, vw, VT_BLK), lambda b, i: (b, 0, vt_row0 // vw, 0)),
            pl.BlockSpec((hp, CK_EXT), lambda b, i: (0, 0)),
        ],
        out_specs=pl.BlockSpec((None, nq * CK_B, vw), lambda b, i: (b, i, 0)),
        out_shape=jax.ShapeDtypeStruct((B, S, n_heads * HEAD_DIM), BF16),
        scratch_shapes=[pltpu.VMEM((hp, CK_NW * CK_B, CK_B), F32),
                        pltpu.VMEM((nq, hp, CK_NW * CK_B, CK_B), F32)],
        compiler_params=_cparams(("arbitrary", "arbitrary")),
        name="chunk_attn",
    )(main, main, vt_arr, ext)


def _sb_kernel(q_ref, k_ref, vt_ref, o_ref, z_ref, lb_ref, sfx_ref, *, hp, bq):
    qs = pl.program_id(1) * bq
    nsub = bq // SUB
    row = lax.broadcasted_iota(jnp.int32, (SUB, bq), 0)
    col = lax.broadcasted_iota(jnp.int32, (SUB, bq), 1)
    ur = lax.broadcasted_iota(jnp.int32, (SUB, 2 * SUB), 0)
    uc = lax.broadcasted_iota(jnp.int32, (SUB, 2 * SUB), 1) & (SUB - 1)
    upper2 = jnp.where(uc > ur, 1.0, 0.0).astype(BF16)
    kcols = [slice(LANES * (h // 2), LANES * (h // 2 + 1)) for h in range(hp)]
    qms = [_pair_mask_q(q_ref[:, kcols[h]], h % 2) for h in range(hp)]

    def step(kb, carries, masked):
        ks = pl.multiple_of(kb * bq, bq)
        for h in range(hp):
            z_ref[h] = lax.dot_general(k_ref[pl.ds(ks, bq), kcols[h]], qms[h], _NT,
                                       preferred_element_type=F32)
        first_col = [c * SUB if masked else 0 for c in range(nsub)]

        def widen(x, c):
            if first_col[c] == 0:
                return x
            return jnp.concatenate([jnp.zeros((x.shape[0], first_col[c]), x.dtype), x], axis=1)

        totals = []
        for h in range(hp):
            tot = []
            for c in range(nsub):
                rows, cols = slice(c * SUB, (c + 1) * SUB), slice(first_col[c], bq)
                z = z_ref[h, rows, cols]
                l1 = jnp.log(1.0 + jnp.exp(-jnp.abs(z)))
                log_beta = jnp.minimum(z, 0.0) - l1
                log_keep = log_beta - z
                if masked:
                    valid = (ks + c * SUB + row < qs + col)[:, cols]
                    log_keep = jnp.where(valid, log_keep, 0.0)
                lb_ref[h, rows, cols] = log_beta
                hi = log_keep.astype(BF16)
                lo = (log_keep - hi.astype(F32)).astype(BF16)
                sfx = jnp.dot(upper2, jnp.concatenate([hi, lo], axis=0),
                              preferred_element_type=F32)
                sfx_ref[h, rows, cols] = sfx
                tot.append(widen(sfx[0:1, :] + log_keep[0:1, :], c))
            totals.append(tot)
        out = []
        for h in range(hp):
            tail, acc = carries[h]
            parts = [None] * nsub
            for c in range(nsub - 1, -1, -1):
                rows, cols = slice(c * SUB, (c + 1) * SUB), slice(first_col[c], bq)
                a = jnp.exp(lb_ref[h, rows, cols] + sfx_ref[h, rows, cols] + tail[:, cols])
                if masked:
                    a = jnp.where((ks + c * SUB + row < qs + col)[:, cols], a, 0.0)
                parts[c] = a.astype(BF16)
                tail = tail + totals[h][c]
            vts = [vt_ref[kb * nsub + c, HEAD_DIM * h:HEAD_DIM * (h + 1), :] for c in range(nsub)]
            if masked:
                for c in range(nsub):
                    acc = acc + widen(jnp.dot(vts[c], parts[c], preferred_element_type=F32), c)
            else:
                acc = acc + jnp.dot(jnp.concatenate(vts, axis=1), jnp.concatenate(parts, axis=0),
                                    preferred_element_type=F32)
            out.append((tail, acc))
        return tuple(out)

    n_full = qs // bq
    carries = tuple((jnp.zeros((1, bq), F32), jnp.zeros((HEAD_DIM, bq), F32))
                    for _ in range(hp))
    carries = step(n_full, carries, True)

    def cond(state):
        kb, carries = state
        tail_max = carries[0][0]
        for h in range(1, hp):
            tail_max = jnp.maximum(tail_max, carries[h][0])
        return (kb >= 0) & (jnp.max(tail_max) > SB_ZERO_LOG)

    def body(state):
        kb, carries = state
        return kb - 1, step(kb, carries, False)

    _, carries = lax.while_loop(cond, body, (n_full - 1, carries))
    _store_heads(o_ref, [acc for (_, acc) in carries])


def _sb_attn(main, q_col0, k_col0, vt_arr, vt_row0, n_heads, bq=256):
    B, S, _ = main.shape
    hp = n_heads
    qw = HEAD_DIM * hp
    return pl.pallas_call(
        functools.partial(_sb_kernel, hp=hp, bq=bq),
        grid=(B, S // bq),
        in_specs=[
            pl.BlockSpec((None, bq, qw), lambda b, i: (b, i, q_col0 // qw)),
            pl.BlockSpec((None, S, qw), lambda b, i: (b, 0, k_col0 // qw)),
            pl.BlockSpec((None, S // VT_BLK, qw, VT_BLK), lambda b, i: (b, 0, vt_row0 // qw, 0)),
        ],
        out_specs=pl.BlockSpec((None, bq, qw), lambda b, i: (b, i, 0)),
        out_shape=jax.ShapeDtypeStruct((B, S, n_heads * HEAD_DIM), BF16),
        scratch_shapes=[pltpu.VMEM((hp, bq, bq), F32)] * 3,
        compiler_params=_cparams(("arbitrary", "arbitrary")),
        name="sb_attn",
    )(main, main, vt_arr)


def _mla_prep_kernel(aux_ref, pos_ref, invf_ref, qn_ref, kvn_ref, wuq_ref, wuqr_ref,
                     wk_ref, wvt_ref, oq_ref, ok_ref, ovt_ref, *, q_scale):
    ang = pos_ref[...] * invf_ref[...]
    cos4, sin4 = jnp.cos(ang), jnp.sin(ang)
    lane = lax.broadcasted_iota(jnp.int32, ang.shape, 1)
    rotary = (lane >= NOPE_DIM) & (lane < NOPE_DIM + ROPE_DIM)
    cos_rows, sin_rows = [], []
    for m in range(LANES // ROPE_DIM):
        shift = (NOPE_DIM - ROPE_DIM * m) % LANES
        cm = cos4 if shift == 0 else pltpu.roll(cos4, shift, 1)
        sm = sin4 if shift == 0 else pltpu.roll(sin4, shift, 1)
        cos_rows.append(jnp.where(rotary, cm, 1.0))
        sin_rows.append(jnp.where(rotary, sm, 0.0))
    cos = jnp.concatenate(cos_rows, axis=0)
    sin = jnp.concatenate(sin_rows, axis=0)
    cq = _rms_bf16(aux_ref[:, 0:Q_LORA], qn_ref[...])
    ckv = _rms_bf16(aux_ref[:, Q_LORA:Q_LORA + KV_LORA], kvn_ref[...])
    o = Q_LORA + KV_LORA
    k_rope = aux_ref[:, o:o + LANES] * cos + aux_ref[:, o + LANES:o + 2 * LANES] * sin
    cos_q, sin_q = cos * q_scale, sin * q_scale
    n_heads = oq_ref.shape[-1] // LANES
    for h in range(0, n_heads, 2):
        cols = slice(h * LANES, (h + 2) * LANES)
        qa = jnp.dot(cq, wuq_ref[:, cols], preferred_element_type=F32)
        qb = jnp.dot(cq, wuqr_ref[:, cols], preferred_element_type=F32)
        kn = jnp.dot(ckv, wk_ref[:, cols], preferred_element_type=F32)
        for d in range(2):
            c1 = slice(d * LANES, (d + 1) * LANES)
            c2 = slice((h + d) * LANES, (h + d + 1) * LANES)
            oq_ref[:, c2] = (qa[:, c1] * cos_q + qb[:, c1] * sin_q).astype(BF16)
            ok_ref[:, c2] = (kn[:, c1] + k_rope).astype(BF16)
    nv = wvt_ref.shape[0]
    for r in range(0, nv, 256):
        vt = lax.dot_general(wvt_ref[r:r + 256, :], ckv, _NT,
                             preferred_element_type=F32).astype(BF16)
        _store_vt(ovt_ref, vt, r)


MLA_TM = 512


def _mla_prep(aux, pos, invf, qn, kvn, wuq, wuqr, wk, wvt, q_scale, tm=MLA_TM):
    B, S, na = aux.shape
    nq, nv = wuq.shape[1], wvt.shape[0]
    full = lambda a: pl.BlockSpec(a.shape, lambda b, i: (0,) * a.ndim)
    return pl.pallas_call(
        functools.partial(_mla_prep_kernel, q_scale=q_scale),
        grid=(B, S // tm),
        in_specs=[
            pl.BlockSpec((None, tm, na), lambda b, i: (b, i, 0)),
            pl.BlockSpec((None, None) + pos.shape[2:], lambda b, i: (b, i, 0, 0)),
            full(invf), full(qn), full(kvn), full(wuq), full(wuqr), full(wk), full(wvt),
        ],
        out_specs=[
            pl.BlockSpec((None, tm, nq), lambda b, i: (b, i, 0)),
            pl.BlockSpec((None, tm, nq), lambda b, i: (b, i, 0)),
            pl.BlockSpec((None, tm // VT_BLK, nv, VT_BLK), lambda b, i: (b, i, 0, 0)),
        ],
        out_shape=[
            jax.ShapeDtypeStruct((B, S, nq), BF16),
            jax.ShapeDtypeStruct((B, S, nq), BF16),
            jax.ShapeDtypeStruct((B, S // VT_BLK, nv, VT_BLK), BF16),
        ],
        compiler_params=_cparams(("arbitrary", "arbitrary")),
        name="mla_prep",
    )(aux, pos, invf, qn, kvn, wuq, wuqr, wk, wvt)


def _mlp_kernel(x_ref, oa_ref, ob_ref, wo_ref, g_ref, wu_ref, wd_ref, gf_ref, out_ref,
                x1_ref, h_ref, a_ref, *, final_norm, tf):
    na = oa_ref.shape[-1]
    x1 = (x_ref[...]
          + jnp.dot(oa_ref[...], wo_ref[0:na, :], preferred_element_type=F32)
          + jnp.dot(ob_ref[...], wo_ref[na:, :], preferred_element_type=F32))
    x1_ref[...] = x1
    h_ref[...] = _rms_bf16(x1, g_ref[...])
    for f in range(0, wu_ref.shape[1], tf):
        u = jnp.dot(h_ref[...], wu_ref[:, f:f + tf], preferred_element_type=F32)
        a_ref[:, f:f + tf] = jnp.square(jnp.maximum(u, 0.0)).astype(BF16)
    y = x1_ref[...] + jnp.dot(a_ref[...], wd_ref[...], preferred_element_type=F32)
    if final_norm:
        ms = jnp.mean(y * y, axis=-1, keepdims=True)
        y = y * lax.rsqrt(ms + EPS) * gf_ref[...]
    out_ref[...] = y


def _mlp(x2, oa, ob, wo, g, wu, wd, gf, final_norm, tm=512, tf=1024):
    T, D = x2.shape
    F = wu.shape[1]
    na, nb = oa.shape[1], ob.shape[1]
    const = lambda shape: pl.BlockSpec(shape, lambda i: (0, 0), pipeline_mode=pl.Buffered(1))
    return pl.pallas_call(
        functools.partial(_mlp_kernel, final_norm=final_norm, tf=tf),
        grid=(T // tm,),
        in_specs=[
            pl.BlockSpec((tm, D), lambda i: (i, 0)),
            pl.BlockSpec((tm, na), lambda i: (i, 0)),
            pl.BlockSpec((tm, nb), lambda i: (i, 0)),
            const((na + nb, D)),
            const((1, D)),
            const((D, F)),
            const((F, D)),
            const((1, D)),
        ],
        out_specs=pl.BlockSpec((tm, D), lambda i: (i, 0)),
        out_shape=jax.ShapeDtypeStruct((T, D), F32),
        scratch_shapes=[pltpu.VMEM((tm, D), F32), pltpu.VMEM((tm, D), BF16),
                        pltpu.VMEM((tm, F), BF16)],
        compiler_params=_cparams(("arbitrary",)),
        name="mlp",
    )(x2, oa, ob, wo, g.reshape(1, D), wu, wd, gf.reshape(1, D))


def _pad_cols(w, n):
    return jnp.pad(w, ((0, 0), (0, n - w.shape[1])))


def _rot_cols(w):
    half = ROPE_DIM // 2
    return jnp.concatenate([-w[:, half:], w[:, :half]], axis=1)


def _rope_slab(w):
    z = jnp.zeros((w.shape[0], NOPE_DIM), w.dtype)
    return jnp.concatenate([z, w, jnp.zeros((w.shape[0], LANES - NOPE_DIM - ROPE_DIM), w.dtype)], axis=1)


def _even_layer(x, g_mix, w_in, b_forget, rel_bias, w_out, g_mlp, w_up, w_down, g_final,
                final_norm):
    B, S, D = x.shape
    hf, hc = b_forget.shape[0], rel_bias.shape[0]
    wf, wc = hf * HEAD_DIM, hc * HEAD_DIM
    o = np.cumsum([0, wf, wf, wf, hf, wc, wc, wc])
    qa, ka, va, fa, qb, kb, vb = [w_in[:, o[n]:o[n + 1]] for n in range(7)]
    q_scale = HEAD_DIM ** -0.5 * LOG2E
    wm = jnp.concatenate([qa * q_scale, ka, qb * q_scale, kb], axis=1).astype(BF16)
    wvt = jnp.concatenate([va, vb], axis=1).T.astype(BF16)
    wa = _pad_cols(fa, LANES).astype(BF16)
    main, vt, aux = _inproj(x, g_mix, wm, wvt, wa)

    q_aug, k_aug = _logcum(aux, _pad_cols(b_forget.reshape(1, hf), LANES), hf)
    o_a = _flash("fox", main, 0, main, wf, vt, 0, (q_aug, k_aug), hf)

    right = CK_EXT - rel_bias.shape[1] - (CHUNK + 1)
    ext = jnp.pad(rel_bias * LOG2E, ((0, 0), (CHUNK + 1, right)), mode="edge")
    o_b = _chunk_attn(main, 2 * wf, 2 * wf + wc, vt, wf, ext, hc)

    y = _mlp(x.reshape(B * S, D), o_a.reshape(B * S, wf), o_b.reshape(B * S, wc),
             w_out.astype(BF16), g_mlp, w_up.astype(BF16), w_down.astype(BF16), g_final,
             final_norm)
    return y.reshape(B, S, D)


def _odd_layer(x, positions, g_mix, w_in, q_norm, kv_norm, w_uq, w_ukv, w_out, g_mlp, w_up,
               w_down, g_final, final_norm):
    B, S, D = x.shape
    hm = w_ukv.shape[1] // (NOPE_DIM + HEAD_DIM)
    ws = w_in.shape[1] - Q_LORA - KV_LORA - ROPE_DIM
    hs = (ws // 3) // HEAD_DIM
    wsb = hs * HEAD_DIM
    o = np.cumsum([0, wsb, wsb, wsb, Q_LORA, KV_LORA, ROPE_DIM])
    qc, kc, vc, w_cq, w_ckv, w_kr = [w_in[:, o[n]:o[n + 1]] for n in range(6)]
    wm = jnp.concatenate([qc * HEAD_DIM ** -0.5, kc], axis=1).astype(BF16)
    wa = jnp.concatenate([w_cq, w_ckv, _rope_slab(w_kr), _rope_slab(_rot_cols(w_kr))],
                         axis=1).astype(BF16)
    main, vt, aux = _inproj(x, g_mix, wm, vc.T.astype(BF16), wa)
    o_c = _sb_attn(main, 0, wsb, vt, 0, hs)

    dq = NOPE_DIM + ROPE_DIM
    wuq3 = w_uq.reshape(Q_LORA, hm, dq)
    nope, ropew = wuq3[:, :, :NOPE_DIM], wuq3[:, :, NOPE_DIM:]
    zq = jnp.zeros((Q_LORA, hm, LANES - dq), w_uq.dtype)
    wuq = jnp.concatenate([nope, ropew, zq], axis=2).reshape(Q_LORA, hm * LANES).astype(BF16)
    half = ROPE_DIM // 2
    ropr = jnp.concatenate([-ropew[:, :, half:], ropew[:, :, :half]], axis=2)
    wuqr = jnp.concatenate([jnp.zeros_like(nope), ropr, zq], axis=2)
    wuqr = wuqr.reshape(Q_LORA, hm * LANES).astype(BF16)
    wkv3 = w_ukv.reshape(KV_LORA, hm, NOPE_DIM + HEAD_DIM)
    wk = jnp.concatenate([wkv3[:, :, :NOPE_DIM],
                          jnp.zeros((KV_LORA, hm, LANES - NOPE_DIM), w_ukv.dtype)], axis=2)
    wk = wk.reshape(KV_LORA, hm * LANES).astype(BF16)
    wv_t = wkv3[:, :, NOPE_DIM:].reshape(KV_LORA, hm * HEAD_DIM).T.astype(BF16)
    freqs = (ROPE_THETA ** (-jnp.arange(half, dtype=F32) / half))
    invf = jnp.tile(freqs, 2 * LANES // ROPE_DIM).reshape(1, LANES)
    groups = LANES // ROPE_DIM
    pos = positions.astype(F32).reshape(B, S // MLA_TM, groups, MLA_TM // groups)
    pos = jnp.repeat(jnp.swapaxes(pos, 2, 3), ROPE_DIM, axis=-1)
    qm, km, vtm = _mla_prep(aux, pos, invf, q_norm.reshape(1, Q_LORA),
                            kv_norm.reshape(1, KV_LORA), wuq, wuqr, wk, wv_t,
                            dq ** -0.5 * LOG2E)
    o_d = _flash("mla", qm, 0, km, 0, vtm, 0, None, hm)

    y = _mlp(x.reshape(B * S, D), o_c.reshape(B * S, wsb), o_d.reshape(B * S, hm * HEAD_DIM),
             w_out.astype(BF16), g_mlp, w_up.astype(BF16), w_down.astype(BF16), g_final,
             final_norm)
    return y.reshape(B, S, D)


def kernel(x, positions, norm_mix, norm_mlp, norm_final, w_in_ab, b_forget, rel_bias, w_out_ab,
           w_in_cd, q_norm, kv_norm, w_uq, w_ukv, w_out_cd, w_up, w_down):
    depth = norm_mix.shape[0]
    for layer in range(depth):
        last = layer == depth - 1
        if layer % 2 == 0:
            e = layer // 2
            x = _even_layer(x, norm_mix[layer], w_in_ab[e], b_forget[e], rel_bias[e], w_out_ab[e],
                            norm_mlp[layer], w_up[layer], w_down[layer], norm_final, last)
        else:
            o = layer // 2
            x = _odd_layer(x, positions, norm_mix[layer], w_in_cd[o], q_norm[o], kv_norm[o],
                           w_uq[o], w_ukv[o], w_out_cd[o], norm_mlp[layer], w_up[layer],
                           w_down[layer], norm_final, last)
    return x
```

```python
import functools
import math

import numpy as np
import jax
import jax.numpy as jnp
from jax import lax
from jax.experimental import pallas as pl
from jax.experimental.pallas import tpu as pltpu

F32 = jnp.float32
BF16 = jnp.bfloat16

EPS = 1e-6
HEAD_DIM = 64
CHUNK = 64
N_LEFT_CHUNKS = 8
REL_CLIP = 256
ROPE_DIM = 32
NOPE_DIM = 64
ROPE_THETA = 10000.0
Q_LORA = 384
KV_LORA = 256

LANES = 128
VT_BLK = LANES
SUB = LANES
FLASH_HP = 4
FIXED_WIDTHS = (2, 1)
SAFE_GAP = 80.0
NEG = -1e30
LOG2E = math.log2(math.e)
SB_ZERO_LOG = -104.0
VMEM_LIMIT = 56 * 1024 * 1024

_NT = (((1,), (1,)), ((), ()))


def _cparams(sem, flags=None):
    return pltpu.CompilerParams(dimension_semantics=sem, vmem_limit_bytes=VMEM_LIMIT, flags=flags)


def _rms_bf16(x, g):
    ms = jnp.mean(x * x, axis=-1, keepdims=True)
    return (x * lax.rsqrt(ms + EPS) * g).astype(BF16)


def _store_vt(ovt_ref, vt, row0):
    rows, tm = vt.shape
    for c in range(tm // VT_BLK):
        ovt_ref[c, row0:row0 + rows, :] = vt[:, c * VT_BLK:(c + 1) * VT_BLK]


def _inproj_kernel(x_ref, g_ref, wm_ref, wvt_ref, wa_ref, om_ref, ovt_ref, oa_ref):
    h = _rms_bf16(x_ref[...], g_ref[...])
    nm = om_ref.shape[-1]
    for c in range(0, nm, 512):
        om_ref[:, c:c + 512] = jnp.dot(
            h, wm_ref[:, c:c + 512], preferred_element_type=F32).astype(BF16)
    nv = wvt_ref.shape[0]
    for r in range(0, nv, 256):
        vt = lax.dot_general(wvt_ref[r:r + 256, :], h, _NT,
                             preferred_element_type=F32).astype(BF16)
        _store_vt(ovt_ref, vt, r)
    oa_ref[...] = jnp.dot(h, wa_ref[...], preferred_element_type=F32)


def _inproj(x, g, wm, wvt, wa, tm=512):
    B, S, D = x.shape
    nm, nv, na = wm.shape[1], wvt.shape[0], wa.shape[1]
    return pl.pallas_call(
        _inproj_kernel,
        grid=(B, S // tm),
        in_specs=[
            pl.BlockSpec((None, tm, D), lambda b, i: (b, i, 0)),
            pl.BlockSpec((1, D), lambda b, i: (0, 0)),
            pl.BlockSpec((D, nm), lambda b, i: (0, 0)),
            pl.BlockSpec((nv, D), lambda b, i: (0, 0)),
            pl.BlockSpec((D, na), lambda b, i: (0, 0)),
        ],
        out_specs=[
            pl.BlockSpec((None, tm, nm), lambda b, i: (b, i, 0)),
            pl.BlockSpec((None, tm // VT_BLK, nv, VT_BLK), lambda b, i: (b, i, 0, 0)),
            pl.BlockSpec((None, tm, na), lambda b, i: (b, i, 0)),
        ],
        out_shape=[
            jax.ShapeDtypeStruct((B, S, nm), BF16),
            jax.ShapeDtypeStruct((B, S // VT_BLK, nv, VT_BLK), BF16),
            jax.ShapeDtypeStruct((B, S, na), F32),
        ],
        compiler_params=_cparams(("arbitrary", "arbitrary")),
        name="inproj",
    )(x, g.reshape(1, D), wm, wvt, wa)


def _split3(x):
    hi = x.astype(BF16)
    r = x - hi.astype(F32)
    mid = r.astype(BF16)
    lo = (r - mid.astype(F32)).astype(BF16)
    return hi, mid, lo


AUG_W = 8


def _logcum_kernel(fa_ref, b_ref, pq_ref, pk_ref, oneq_ref, onek_ref, oq_ref, ok_ref, carry_ref):
    @pl.when(pl.program_id(1) == 0)
    def _():
        carry_ref[...] = jnp.zeros_like(carry_ref)

    z = fa_ref[...] + b_ref[...]
    lf = jnp.minimum(z, 0.0) - jnp.log(1.0 + jnp.exp(-jnp.abs(z)))
    tc = lf.shape[0]
    r = lax.broadcasted_iota(jnp.int32, (tc, tc), 0)
    c = lax.broadcasted_iota(jnp.int32, (tc, tc), 1)
    tri = jnp.where(r >= c, 1.0, 0.0).astype(BF16)
    cs = carry_ref[...]
    for part in _split3(lf):
        cs = cs + jnp.dot(tri, part, preferred_element_type=F32)
    carry_ref[...] = cs[tc - 1:tc, :]
    qa, ka = oneq_ref[...], onek_ref[...]
    for n, part in enumerate(_split3(cs * LOG2E)):
        qa = qa + jnp.dot(part, pq_ref[n], preferred_element_type=F32)
        ka = ka + jnp.dot(part, pk_ref[n], preferred_element_type=F32)
    oq_ref[...] = qa.astype(BF16)
    ok_ref[...] = ka.astype(BF16)


def _logcum(fa, bias, n_heads, tc=512):
    B, S, W = fa.shape
    na = LANES
    pq = np.zeros((3, W, na), np.float32)
    pk = np.zeros((3, W, na), np.float32)
    oneq = np.zeros((1, na), np.float32)
    onek = np.zeros((1, na), np.float32)
    for h in range(n_heads):
        base = h * AUG_W
        for n in range(3):
            pq[n, h, base + n] = 1.0
            pk[n, h, base + 3 + n] = -1.0
        oneq[0, base + 3:base + 6] = 1.0
        onek[0, base:base + 3] = 1.0
    const = lambda a: pl.BlockSpec(a.shape, lambda b, i: (0,) * a.ndim)
    args = [jnp.asarray(pq, BF16), jnp.asarray(pk, BF16), jnp.asarray(oneq), jnp.asarray(onek)]
    return pl.pallas_call(
        _logcum_kernel,
        grid=(B, S // tc),
        in_specs=[pl.BlockSpec((None, tc, W), lambda b, i: (b, i, 0)),
                  pl.BlockSpec((1, W), lambda b, i: (0, 0))] + [const(a) for a in args],
        out_specs=[pl.BlockSpec((None, tc, na), lambda b, i: (b, i, 0))] * 2,
        out_shape=[jax.ShapeDtypeStruct((B, S, na), BF16)] * 2,
        scratch_shapes=[pltpu.VMEM((1, W), F32)],
        compiler_params=_cparams(("arbitrary", "arbitrary")),
        name="logcum",
    )(fa, bias, *args)


def _pair_mask_q(q2, j):
    lane = lax.broadcasted_iota(jnp.int32, q2.shape, 1)
    keep = (lane >= HEAD_DIM * j) & (lane < HEAD_DIM * (j + 1))
    return jnp.where(keep, q2, jnp.zeros_like(q2))


ONES_ROWS = 16


def _softmax_step(tiles, vts, carry, tile_max=None):
    m, acc = carry
    if tile_max is not None:
        m_new = jnp.maximum(m, tile_max)
    else:
        m_new = m
        for tile in tiles:
            m_new = jnp.maximum(m_new, jnp.max(tile(), axis=0, keepdims=True))
    alpha = jnp.exp2(m - m_new)
    pv = None
    for tile, vt in zip(tiles, vts):
        p = jnp.exp2(tile() - m_new).astype(BF16)
        vt1 = jnp.concatenate([vt, jnp.ones((ONES_ROWS, vt.shape[1]), BF16)], axis=0)
        d = jnp.dot(vt1, p, preferred_element_type=F32)
        pv = d if pv is None else pv + d
    return m_new, alpha * acc + pv


def _softmax_init(bq):
    return (jnp.full((1, bq), NEG, F32), jnp.zeros((HEAD_DIM + ONES_ROWS, bq), F32))


def _softmax_out(carry):
    _, acc = carry
    return acc[0:HEAD_DIM] / acc[HEAD_DIM:HEAD_DIM + 1]


def _store_heads(o_ref, outs):
    oT = jnp.concatenate(outs, axis=0)
    o_ref[...] = oT.T.astype(o_ref.dtype)


def _flash_kernel(*refs, mode, hp, bq):
    if mode == "fox":
        q_ref, k_ref, vt_ref, qaug_ref, kaug_ref, o_ref = refs[:6]
    else:
        q_ref, k_ref, vt_ref, o_ref = refs[:4]
    sa_ref, sb_ref, ma_ref, mb_ref, qt_ref, kn_ref = refs[-6:]
    qs = pl.program_id(2) * bq
    sub = bq
    row = lax.broadcasted_iota(jnp.int32, (sub, bq), 0)
    col = lax.broadcasted_iota(jnp.int32, (sub, bq), 1)
    if mode == "fox":
        kcols = [slice(LANES * (h // 2), LANES * (h // 2 + 1)) for h in range(hp)]
        lane = lax.broadcasted_iota(jnp.int32, (bq, LANES), 1)
        qa = qaug_ref[...]
        qms = []
        for h in range(hp):
            first = AUG_W * (pl.program_id(1) * hp + h)
            own = (lane >= first) & (lane < first + AUG_W)
            qms.append(jnp.concatenate(
                [_pair_mask_q(q_ref[:, kcols[h]], h % 2),
                 jnp.where(own, qa, jnp.zeros_like(qa))], axis=1))
    else:
        kcols = [slice(LANES * h, LANES * (h + 1)) for h in range(hp)]
        qms = [q_ref[:, kcols[h]] for h in range(hp)]
    for h in range(hp):
        qt_ref[h] = qms[h].T

    @pl.when(pl.program_id(2) == 0)
    def _():
        klane = lax.broadcasted_iota(jnp.int32, (1, LANES), 1)
        for h in range(hp):
            kabs = jnp.max(jnp.abs(k_ref[:, kcols[h]].astype(F32)), axis=0, keepdims=True)
            sq = kabs * kabs
            if mode == "fox":
                sq = jnp.where((klane >= HEAD_DIM * (h % 2)) & (klane < HEAD_DIM * (h % 2 + 1)),
                               sq, 0.0)
            kn_ref[h] = jnp.broadcast_to(jnp.sqrt(jnp.sum(sq, axis=1, keepdims=True)), (1, bq))

    def scores(sb, h, masked):
        ks = pl.multiple_of(sb * sub, sub)
        k = k_ref[pl.ds(ks, sub), kcols[h]]
        if mode == "fox":
            k = jnp.concatenate([k, kaug_ref[pl.ds(ks, sub), :]], axis=1)
        sT = jnp.dot(k, qt_ref[h], preferred_element_type=F32)
        if masked and mode == "fox":
            sT = jnp.where(ks + row <= qs + col, sT, NEG)
        elif masked:
            sT = jnp.where(((ks + row) >> 6) <= ((qs + col) >> 6), sT, NEG)
        return sT

    nsub = bq // sub
    nvt = sub // VT_BLK

    def produce(buf, sb0, masked, h):
        s_buf, m_buf = buf
        tile_max = None
        for c in range(nsub):
            sT = scores(sb0 + c, h, masked)
            s_buf[h, c] = sT
            cm = jnp.max(sT, axis=0, keepdims=True)
            tile_max = cm if tile_max is None else jnp.maximum(tile_max, cm)
        m_buf[h] = tile_max

    def consume(buf, sb0, carry, h):
        s_buf, m_buf = buf
        tiles = [lambda c=c: s_buf[h, c] for c in range(nsub)]
        vts = [jnp.concatenate([vt_ref[(sb0 + c) * nvt + v, HEAD_DIM * h:HEAD_DIM * (h + 1), :]
                                for v in range(nvt)], axis=1) for c in range(nsub)]
        return _softmax_step(tiles, vts, carry, tile_max=m_buf[h])

    def stage(cur, cur_sb, nxt, nxt_sb, carries):
        if nxt is not None:
            for h in range(hp):
                produce(nxt, nxt_sb, False, h)
        return tuple(consume(cur, cur_sb, carries[h], h) for h in range(hp))

    n = pl.program_id(2)
    diag_sb = qs // sub
    buf_a, buf_b = (sa_ref, ma_ref), (sb_ref, mb_ref)
    for h in range(hp):
        produce(buf_a, diag_sb, True, h)

    def pair(j, carries):
        carries = stage(buf_a, jnp.where(j == 0, diag_sb, (2 * j - 1) * nsub),
                        buf_b, 2 * j * nsub, carries)
        return stage(buf_b, 2 * j * nsub,
                     buf_a, jnp.minimum(2 * j + 1, n - 1) * nsub, carries)

    def online(_):
        carries = tuple(_softmax_init(bq) for _ in range(hp))
        carries = lax.fori_loop(0, (n + 1) // 2, pair, carries)
        carries = lax.cond(
            n % 2 == 0,
            lambda c: stage(buf_a, jnp.where(n == 0, diag_sb, (n - 1) * nsub), None, None, c),
            lambda c: c, carries)
        return jnp.concatenate([_softmax_out(c) for c in carries], axis=0)

    refs_ = []
    gap = None
    for h in range(hp):
        qf = qt_ref[h, 0:LANES, :].astype(F32)
        bound = jnp.sqrt(jnp.sum(qf * qf, axis=0, keepdims=True)) * kn_ref[h] * 1.01 + 1e-3
        refs_.append(bound)
        g = jnp.max(bound - ma_ref[h])
        gap = g if gap is None else jnp.maximum(gap, g)

    def weigh_add(acc, h, s, sb0, nblk):
        p = jnp.exp2(s - refs_[h]).astype(BF16)
        vt = jnp.concatenate([vt_ref[sb0 * nvt + v, HEAD_DIM * h:HEAD_DIM * (h + 1), :]
                              for v in range(nblk * nvt)], axis=1)
        vt1 = jnp.concatenate([vt, jnp.ones((ONES_ROWS, nblk * sub), BF16)], axis=0)
        return acc + jnp.dot(vt1, p, preferred_element_type=F32)

    def fixed_reference(_):
        def run(kb, accs, nblk):
            ks = pl.multiple_of(kb * sub, sub)
            ss = []
            for h in range(hp):
                k = k_ref[pl.ds(ks, nblk * sub), kcols[h]]
                if mode == "fox":
                    k = jnp.concatenate([k, kaug_ref[pl.ds(ks, nblk * sub), :]], axis=1)
                ss.append(jnp.dot(k, qt_ref[h], preferred_element_type=F32))
            return tuple(weigh_add(accs[h], h, ss[h], kb, nblk) for h in range(hp))

        accs = tuple(jnp.zeros((HEAD_DIM + ONES_ROWS, bq), F32) for _ in range(hp))
        done = 0
        for width in FIXED_WIDTHS:
            trips = (n - done) // width
            accs = lax.fori_loop(0, trips,
                                 lambda j, a, done=done, width=width: run(done + j * width, a, width),
                                 accs)
            done = done + trips * width
        accs = [weigh_add(accs[h], h, sa_ref[h, 0], diag_sb, 1) for h in range(hp)]
        return jnp.concatenate([a[0:HEAD_DIM] / a[HEAD_DIM:HEAD_DIM + 1] for a in accs], axis=0)

    oT = lax.cond(gap <= SAFE_GAP, fixed_reference, online, None)
    o_ref[...] = oT.T.astype(o_ref.dtype)


def _flash(mode, q_arr, q_col0, k_arr, k_col0, vt_arr, vt_row0, extra, n_heads,
           hp=FLASH_HP, bq=256):
    B, S, _ = q_arr.shape
    qw = (HEAD_DIM if mode == "fox" else LANES) * hp
    vw = HEAD_DIM * hp
    in_specs = [
        pl.BlockSpec((None, bq, qw), lambda b, g, i: (b, i, q_col0 // qw + g)),
        pl.BlockSpec((None, S, qw), lambda b, g, i: (b, 0, k_col0 // qw + g)),
        pl.BlockSpec((None, S // VT_BLK, vw, VT_BLK),
                     lambda b, g, i: (b, 0, vt_row0 // vw + g, 0)),
    ]
    args = [q_arr, k_arr, vt_arr]
    if mode == "fox":
        q_aug, k_aug = extra
        in_specs += [
            pl.BlockSpec((None, bq, LANES), lambda b, g, i: (b, i, 0)),
            pl.BlockSpec((None, S, LANES), lambda b, g, i: (b, 0, 0)),
        ]
        args += [q_aug, k_aug]
    return pl.pallas_call(
        functools.partial(_flash_kernel, mode=mode, hp=hp, bq=bq),
        grid=(B, n_heads // hp, S // bq),
        in_specs=in_specs,
        out_specs=pl.BlockSpec((None, bq, vw), lambda b, g, i: (b, i, g)),
        out_shape=jax.ShapeDtypeStruct((B, S, n_heads * HEAD_DIM), BF16),
        scratch_shapes=([pltpu.VMEM((hp, 1, bq, bq), F32)] * 2
                        + [pltpu.VMEM((hp, 1, bq), F32)] * 2
                        + [pltpu.VMEM((hp, 2 * LANES if mode == "fox" else LANES, bq), BF16),
                           pltpu.VMEM((hp, 1, bq), F32)]),
        compiler_params=_cparams(("arbitrary", "arbitrary", "arbitrary")),
        name="flash_" + mode,
    )(*args)


CK_B = 2 * CHUNK
CK_NW = N_LEFT_CHUNKS * CHUNK // CK_B + 1
CK_EXT = (CK_NW + 1) * CK_B


def _chunk_kernel(q_ref, k_ref, vt_ref, ext_ref, o_ref, tab_ref, s_ref, *, hp, nq):
    i = pl.program_id(1)

    @pl.when(i == 0)
    def _():
        jj = lax.broadcasted_iota(jnp.int32, (CK_B, CK_B), 0)
        rr = lax.broadcasted_iota(jnp.int32, (CK_B, CK_B), 1)
        for h in range(hp):
            for w in range(CK_NW):
                a = (CK_NW - 1 - w) * CK_B
                g = jnp.broadcast_to(ext_ref[h:h + 1, a:a + 2 * CK_B], (CK_B, 2 * CK_B))
                t = pltpu.roll(g, CK_B, 1, stride=1, stride_axis=0)[:, :CK_B]
                if w == 0:
                    t = jnp.where((rr >= CHUNK) & (jj < CHUNK), NEG, t)
                if w == CK_NW - 1:
                    t = jnp.where((rr < CHUNK) & (jj >= CHUNK), NEG, t)
                tab_ref[h, w * CK_B:(w + 1) * CK_B, :] = t

    kcols = [slice(LANES * (h // 2), LANES * (h // 2 + 1)) for h in range(hp)]
    firsts = [i * nq + u - (CK_NW - 1) for u in range(nq)]

    def finish():
        for u in range(nq):
            kbc = [jnp.maximum(firsts[u] + w, 0) for w in range(CK_NW)]
            outs = []
            for h in range(hp):
                vt = jnp.concatenate([vt_ref[kbc[w], HEAD_DIM * h:HEAD_DIM * (h + 1), :]
                                      for w in range(CK_NW)], axis=1)
                outs.append(_softmax_out(_softmax_step([lambda u=u, h=h: s_ref[u, h]], [vt],
                                                       _softmax_init(CK_B))))
            oT = jnp.concatenate(outs, axis=0)
            o_ref[u * CK_B:(u + 1) * CK_B, :] = oT.T.astype(o_ref.dtype)

    def pair_scores(u, p, ks, nrows):
        q2 = q_ref[u * CK_B:(u + 1) * CK_B, kcols[2 * p]]
        qq = jnp.concatenate([_pair_mask_q(q2, 0), _pair_mask_q(q2, 1)], axis=0)
        return lax.dot_general(k_ref[pl.ds(ks, nrows), kcols[2 * p]], qq, _NT,
                               preferred_element_type=F32)

    @pl.when(firsts[0] >= 0)
    def _():
        for u in range(nq):
            ks = pl.multiple_of(firsts[u] * CK_B, CK_B)
            for p in range(hp // 2):
                sT = pair_scores(u, p, ks, CK_NW * CK_B)
                for j in range(2):
                    s_ref[u, 2 * p + j] = sT[:, j * CK_B:(j + 1) * CK_B] + tab_ref[2 * p + j]
        finish()

    @pl.when(firsts[0] < 0)
    def _():
        for u in range(nq):
            for p in range(hp // 2):
                for w in range(CK_NW):
                    rows = slice(w * CK_B, (w + 1) * CK_B)
                    ks = pl.multiple_of(jnp.maximum(firsts[u] + w, 0) * CK_B, CK_B)
                    sT = pair_scores(u, p, ks, CK_B)
                    for j in range(2):
                        s_ref[u, 2 * p + j, rows, :] = jnp.where(
                            firsts[u] + w >= 0,
                            sT[:, j * CK_B:(j + 1) * CK_B] + tab_ref[2 * p + j, rows, :], NEG)
        finish()


def _chunk_attn(main, q_col0, k_col0, vt_arr, vt_row0, ext, n_heads, nq=4):
    B, S, _ = main.shape
    hp = n_heads
    qw, vw = HEAD_DIM * hp, HEAD_DIM * hp
    return pl.pallas_call(
        functools.partial(_chunk_kernel, hp=hp, nq=nq),
        grid=(B, S // (nq * CK_B)),
        in_specs=[
            pl.BlockSpec((None, nq * CK_B, qw), lambda b, i: (b, i, q_col0 // qw)),
            pl.BlockSpec((None, S, qw), lambda b, i: (b, 0, k_col0 // qw)),
            pl.BlockSpec((None, S // VT_BLK, vw, VT_BLK), lambda b, i: (b, 0, vt_row0 // vw, 0)),
            pl.BlockSpec((hp, CK_EXT), lambda b, i: (0, 0)),
        ],
        out_specs=pl.BlockSpec((None, nq * CK_B, vw), lambda b, i: (b, i, 0)),
        out_shape=jax.ShapeDtypeStruct((B, S, n_heads * HEAD_DIM), BF16),
        scratch_shapes=[pltpu.VMEM((hp, CK_NW * CK_B, CK_B), F32),
                        pltpu.VMEM((nq, hp, CK_NW * CK_B, CK_B), F32)],
        compiler_params=_cparams(("arbitrary", "arbitrary")),
        name="chunk_attn",
    )(main, main, vt_arr, ext)


def _sb_kernel(q_ref, k_ref, vt_ref, o_ref, z_ref, lb_ref, sfx_ref, *, hp, bq):
    qs = pl.program_id(1) * bq
    nsub = bq // SUB
    row = lax.broadcasted_iota(jnp.int32, (SUB, bq), 0)
    col = lax.broadcasted_iota(jnp.int32, (SUB, bq), 1)
    ur = lax.broadcasted_iota(jnp.int32, (SUB, 2 * SUB), 0)
    uc = lax.broadcasted_iota(jnp.int32, (SUB, 2 * SUB), 1) & (SUB - 1)
    upper2 = jnp.where(uc > ur, 1.0, 0.0).astype(BF16)
    kcols = [slice(LANES * (h // 2), LANES * (h // 2 + 1)) for h in range(hp)]
    qms = [_pair_mask_q(q_ref[:, kcols[h]], h % 2) for h in range(hp)]

    def step(kb, carries, masked):
        ks = pl.multiple_of(kb * bq, bq)
        for h in range(hp):
            z_ref[h] = lax.dot_general(k_ref[pl.ds(ks, bq), kcols[h]], qms[h], _NT,
                                       preferred_element_type=F32)
        first_col = [c * SUB if masked else 0 for c in range(nsub)]

        def widen(x, c):
            if first_col[c] == 0:
                return x
            return jnp.concatenate([jnp.zeros((x.shape[0], first_col[c]), x.dtype), x], axis=1)

        totals = []
        for h in range(hp):
            tot = []
            for c in range(nsub):
                rows, cols = slice(c * SUB, (c + 1) * SUB), slice(first_col[c], bq)
                z = z_ref[h, rows, cols]
                l1 = jnp.log(1.0 + jnp.exp(-jnp.abs(z)))
                log_beta = jnp.minimum(z, 0.0) - l1
                log_keep = log_beta - z
                if masked:
                    valid = (ks + c * SUB + row < qs + col)[:, cols]
                    log_keep = jnp.where(valid, log_keep, 0.0)
                lb_ref[h, rows, cols] = log_beta
                hi = log_keep.astype(BF16)
                lo = (log_keep - hi.astype(F32)).astype(BF16)
                sfx = jnp.dot(upper2, jnp.concatenate([hi, lo], axis=0),
                              preferred_element_type=F32)
                sfx_ref[h, rows, cols] = sfx
                tot.append(widen(sfx[0:1, :] + log_keep[0:1, :], c))
            totals.append(tot)
        out = []
        for h in range(hp):
            tail, acc = carries[h]
            parts = [None] * nsub
            for c in range(nsub - 1, -1, -1):
                rows, cols = slice(c * SUB, (c + 1) * SUB), slice(first_col[c], bq)
                a = jnp.exp(lb_ref[h, rows, cols] + sfx_ref[h, rows, cols] + tail[:, cols])
                if masked:
                    a = jnp.where((ks + c * SUB + row < qs + col)[:, cols], a, 0.0)
                parts[c] = a.astype(BF16)
                tail = tail + totals[h][c]
            vts = [vt_ref[kb * nsub + c, HEAD_DIM * h:HEAD_DIM * (h + 1), :] for c in range(nsub)]
            if masked:
                for c in range(nsub):
                    acc = acc + widen(jnp.dot(vts[c], parts[c], preferred_element_type=F32), c)
            else:
                acc = acc + jnp.dot(jnp.concatenate(vts, axis=1), jnp.concatenate(parts, axis=0),
                                    preferred_element_type=F32)
            out.append((tail, acc))
        return tuple(out)

    n_full = qs // bq
    carries = tuple((jnp.zeros((1, bq), F32), jnp.zeros((HEAD_DIM, bq), F32))
                    for _ in range(hp))
    carries = step(n_full, carries, True)

    def cond(state):
        kb, carries = state
        tail_max = carries[0][0]
        for h in range(1, hp):
            tail_max = jnp.maximum(tail_max, carries[h][0])
        return (kb >= 0) & (jnp.max(tail_max) > SB_ZERO_LOG)

    def body(state):
        kb, carries = state
        return kb - 1, step(kb, carries, False)

    _, carries = lax.while_loop(cond, body, (n_full - 1, carries))
    _store_heads(o_ref, [acc for (_, acc) in carries])


def _sb_attn(main, q_col0, k_col0, vt_arr, vt_row0, n_heads, bq=256):
    B, S, _ = main.shape
    hp = n_heads
    qw = HEAD_DIM * hp
    return pl.pallas_call(
        functools.partial(_sb_kernel, hp=hp, bq=bq),
        grid=(B, S // bq),
        in_specs=[
            pl.BlockSpec((None, bq, qw), lambda b, i: (b, i, q_col0 // qw)),
            pl.BlockSpec((None, S, qw), lambda b, i: (b, 0, k_col0 // qw)),
            pl.BlockSpec((None, S // VT_BLK, qw, VT_BLK), lambda b, i: (b, 0, vt_row0 // qw, 0)),
        ],
        out_specs=pl.BlockSpec((None, bq, qw), lambda b, i: (b, i, 0)),
        out_shape=jax.ShapeDtypeStruct((B, S, n_heads * HEAD_DIM), BF16),
        scratch_shapes=[pltpu.VMEM((hp, bq, bq), F32)] * 3,
        compiler_params=_cparams(("arbitrary", "arbitrary")),
        name="sb_attn",
    )(main, main, vt_arr)


def _mla_prep_kernel(aux_ref, pos_ref, invf_ref, qn_ref, kvn_ref, wuq_ref, wuqr_ref,
                     wk_ref, wvt_ref, oq_ref, ok_ref, ovt_ref, *, q_scale):
    ang = pos_ref[...] * invf_ref[...]
    cos4, sin4 = jnp.cos(ang), jnp.sin(ang)
    lane = lax.broadcasted_iota(jnp.int32, ang.shape, 1)
    rotary = (lane >= NOPE_DIM) & (lane < NOPE_DIM + ROPE_DIM)
    cos_rows, sin_rows = [], []
    for m in range(LANES // ROPE_DIM):
        shift = (NOPE_DIM - ROPE_DIM * m) % LANES
        cm = cos4 if shift == 0 else pltpu.roll(cos4, shift, 1)
        sm = sin4 if shift == 0 else pltpu.roll(sin4, shift, 1)
        cos_rows.append(jnp.where(rotary, cm, 1.0))
        sin_rows.append(jnp.where(rotary, sm, 0.0))
    cos = jnp.concatenate(cos_rows, axis=0)
    sin = jnp.concatenate(sin_rows, axis=0)
    cq = _rms_bf16(aux_ref[:, 0:Q_LORA], qn_ref[...])
    ckv = _rms_bf16(aux_ref[:, Q_LORA:Q_LORA + KV_LORA], kvn_ref[...])
    o = Q_LORA + KV_LORA
    k_rope = aux_ref[:, o:o + LANES] * cos + aux_ref[:, o + LANES:o + 2 * LANES] * sin
    cos_q, sin_q = cos * q_scale, sin * q_scale
    n_heads = oq_ref.shape[-1] // LANES
    for h in range(0, n_heads, 2):
        cols = slice(h * LANES, (h + 2) * LANES)
        qa = jnp.dot(cq, wuq_ref[:, cols], preferred_element_type=F32)
        qb = jnp.dot(cq, wuqr_ref[:, cols], preferred_element_type=F32)
        kn = jnp.dot(ckv, wk_ref[:, cols], preferred_element_type=F32)
        for d in range(2):
            c1 = slice(d * LANES, (d + 1) * LANES)
            c2 = slice((h + d) * LANES, (h + d + 1) * LANES)
            oq_ref[:, c2] = (qa[:, c1] * cos_q + qb[:, c1] * sin_q).astype(BF16)
            ok_ref[:, c2] = (kn[:, c1] + k_rope).astype(BF16)
    nv = wvt_ref.shape[0]
    for r in range(0, nv, 256):
        vt = lax.dot_general(wvt_ref[r:r + 256, :], ckv, _NT,
                             preferred_element_type=F32).astype(BF16)
        _store_vt(ovt_ref, vt, r)


MLA_TM = 512


def _mla_prep(aux, pos, invf, qn, kvn, wuq, wuqr, wk, wvt, q_scale, tm=MLA_TM):
    B, S, na = aux.shape
    nq, nv = wuq.shape[1], wvt.shape[0]
    full = lambda a: pl.BlockSpec(a.shape, lambda b, i: (0,) * a.ndim)
    return pl.pallas_call(
        functools.partial(_mla_prep_kernel, q_scale=q_scale),
        grid=(B, S // tm),
        in_specs=[
            pl.BlockSpec((None, tm, na), lambda b, i: (b, i, 0)),
            pl.BlockSpec((None, None) + pos.shape[2:], lambda b, i: (b, i, 0, 0)),
            full(invf), full(qn), full(kvn), full(wuq), full(wuqr), full(wk), full(wvt),
        ],
        out_specs=[
            pl.BlockSpec((None, tm, nq), lambda b, i: (b, i, 0)),
            pl.BlockSpec((None, tm, nq), lambda b, i: (b, i, 0)),
            pl.BlockSpec((None, tm // VT_BLK, nv, VT_BLK), lambda b, i: (b, i, 0, 0)),
        ],
        out_shape=[
            jax.ShapeDtypeStruct((B, S, nq), BF16),
            jax.ShapeDtypeStruct((B, S, nq), BF16),
            jax.ShapeDtypeStruct((B, S // VT_BLK, nv, VT_BLK), BF16),
        ],
        compiler_params=_cparams(("arbitrary", "arbitrary")),
        name="mla_prep",
    )(aux, pos, invf, qn, kvn, wuq, wuqr, wk, wvt)


def _mlp_kernel(x_ref, oa_ref, ob_ref, wo_ref, g_ref, wu_ref, wd_ref, gf_ref, out_ref,
                x1_ref, h_ref, a_ref, *, final_norm, tf):
    na = oa_ref.shape[-1]
    x1 = (x_ref[...]
          + jnp.dot(oa_ref[...], wo_ref[0:na, :], preferred_element_type=F32)
          + jnp.dot(ob_ref[...], wo_ref[na:, :], preferred_element_type=F32))
    x1_ref[...] = x1
    h_ref[...] = _rms_bf16(x1, g_ref[...])
    for f in range(0, wu_ref.shape[1], tf):
        u = jnp.dot(h_ref[...], wu_ref[:, f:f + tf], preferred_element_type=F32)
        a_ref[:, f:f + tf] = jnp.square(jnp.maximum(u, 0.0)).astype(BF16)
    y = x1_ref[...] + jnp.dot(a_ref[...], wd_ref[...], preferred_element_type=F32)
    if final_norm:
        ms = jnp.mean(y * y, axis=-1, keepdims=True)
        y = y * lax.rsqrt(ms + EPS) * gf_ref[...]
    out_ref[...] = y


def _mlp(x2, oa, ob, wo, g, wu, wd, gf, final_norm, tm=512, tf=1024):
    T, D = x2.shape
    F = wu.shape[1]
    na, nb = oa.shape[1], ob.shape[1]
    const = lambda shape: pl.BlockSpec(shape, lambda i: (0, 0), pipeline_mode=pl.Buffered(1))
    return pl.pallas_call(
        functools.partial(_mlp_kernel, final_norm=final_norm, tf=tf),
        grid=(T // tm,),
        in_specs=[
            pl.BlockSpec((tm, D), lambda i: (i, 0)),
            pl.BlockSpec((tm, na), lambda i: (i, 0)),
            pl.BlockSpec((tm, nb), lambda i: (i, 0)),
            const((na + nb, D)),
            const((1, D)),
            const((D, F)),
            const((F, D)),
            const((1, D)),
        ],
        out_specs=pl.BlockSpec((tm, D), lambda i: (i, 0)),
        out_shape=jax.ShapeDtypeStruct((T, D), F32),
        scratch_shapes=[pltpu.VMEM((tm, D), F32), pltpu.VMEM((tm, D), BF16),
                        pltpu.VMEM((tm, F), BF16)],
        compiler_params=_cparams(("arbitrary",)),
        name="mlp",
    )(x2, oa, ob, wo, g.reshape(1, D), wu, wd, gf.reshape(1, D))


def _pad_cols(w, n):
    return jnp.pad(w, ((0, 0), (0, n - w.shape[1])))


def _rot_cols(w):
    half = ROPE_DIM // 2
    return jnp.concatenate([-w[:, half:], w[:, :half]], axis=1)


def _rope_slab(w):
    z = jnp.zeros((w.shape[0], NOPE_DIM), w.dtype)
    return jnp.concatenate([z, w, jnp.zeros((w.shape[0], LANES - NOPE_DIM - ROPE_DIM), w.dtype)], axis=1)


def _even_layer(x, g_mix, w_in, b_forget, rel_bias, w_out, g_mlp, w_up, w_down, g_final,
                final_norm):
    B, S, D = x.shape
    hf, hc = b_forget.shape[0], rel_bias.shape[0]
    wf, wc = hf * HEAD_DIM, hc * HEAD_DIM
    o = np.cumsum([0, wf, wf, wf, hf, wc, wc, wc])
    qa, ka, va, fa, qb, kb, vb = [w_in[:, o[n]:o[n + 1]] for n in range(7)]
    q_scale = HEAD_DIM ** -0.5 * LOG2E
    wm = jnp.concatenate([qa * q_scale, ka, qb * q_scale, kb], axis=1).astype(BF16)
    wvt = jnp.concatenate([va, vb], axis=1).T.astype(BF16)
    wa = _pad_cols(fa, LANES).astype(BF16)
    main, vt, aux = _inproj(x, g_mix, wm, wvt, wa)

    q_aug, k_aug = _logcum(aux, _pad_cols(b_forget.reshape(1, hf), LANES), hf)
    o_a = _flash("fox", main, 0, main, wf, vt, 0, (q_aug, k_aug), hf)

    right = CK_EXT - rel_bias.shape[1] - (CHUNK + 1)
    ext = jnp.pad(rel_bias * LOG2E, ((0, 0), (CHUNK + 1, right)), mode="edge")
    o_b = _chunk_attn(main, 2 * wf, 2 * wf + wc, vt, wf, ext, hc)

    y = _mlp(x.reshape(B * S, D), o_a.reshape(B * S, wf), o_b.reshape(B * S, wc),
             w_out.astype(BF16), g_mlp, w_up.astype(BF16), w_down.astype(BF16), g_final,
             final_norm)
    return y.reshape(B, S, D)


def _odd_layer(x, positions, g_mix, w_in, q_norm, kv_norm, w_uq, w_ukv, w_out, g_mlp, w_up,
               w_down, g_final, final_norm):
    B, S, D = x.shape
    hm = w_ukv.shape[1] // (NOPE_DIM + HEAD_DIM)
    ws = w_in.shape[1] - Q_LORA - KV_LORA - ROPE_DIM
    hs = (ws // 3) // HEAD_DIM
    wsb = hs * HEAD_DIM
    o = np.cumsum([0, wsb, wsb, wsb, Q_LORA, KV_LORA, ROPE_DIM])
    qc, kc, vc, w_cq, w_ckv, w_kr = [w_in[:, o[n]:o[n + 1]] for n in range(6)]
    wm = jnp.concatenate([qc * HEAD_DIM ** -0.5, kc], axis=1).astype(BF16)
    wa = jnp.concatenate([w_cq, w_ckv, _rope_slab(w_kr), _rope_slab(_rot_cols(w_kr))],
                         axis=1).astype(BF16)
    main, vt, aux = _inproj(x, g_mix, wm, vc.T.astype(BF16), wa)
    o_c = _sb_attn(main, 0, wsb, vt, 0, hs)

    dq = NOPE_DIM + ROPE_DIM
    wuq3 = w_uq.reshape(Q_LORA, hm, dq)
    nope, ropew = wuq3[:, :, :NOPE_DIM], wuq3[:, :, NOPE_DIM:]
    zq = jnp.zeros((Q_LORA, hm, LANES - dq), w_uq.dtype)
    wuq = jnp.concatenate([nope, ropew, zq], axis=2).reshape(Q_LORA, hm * LANES).astype(BF16)
    half = ROPE_DIM // 2
    ropr = jnp.concatenate([-ropew[:, :, half:], ropew[:, :, :half]], axis=2)
    wuqr = jnp.concatenate([jnp.zeros_like(nope), ropr, zq], axis=2)
    wuqr = wuqr.reshape(Q_LORA, hm * LANES).astype(BF16)
    wkv3 = w_ukv.reshape(KV_LORA, hm, NOPE_DIM + HEAD_DIM)
    wk = jnp.concatenate([wkv3[:, :, :NOPE_DIM],
                          jnp.zeros((KV_LORA, hm, LANES - NOPE_DIM), w_ukv.dtype)], axis=2)
    wk = wk.reshape(KV_LORA, hm * LANES).astype(BF16)
    wv_t = wkv3[:, :, NOPE_DIM:].reshape(KV_LORA, hm * HEAD_DIM).T.astype(BF16)
    freqs = (ROPE_THETA ** (-jnp.arange(half, dtype=F32) / half))
    invf = jnp.tile(freqs, 2 * LANES // ROPE_DIM).reshape(1, LANES)
    groups = LANES // ROPE_DIM
    pos = positions.astype(F32).reshape(B, S // MLA_TM, groups, MLA_TM // groups)
    pos = jnp.repeat(jnp.swapaxes(pos, 2, 3), ROPE_DIM, axis=-1)
    qm, km, vtm = _mla_prep(aux, pos, invf, q_norm.reshape(1, Q_LORA),
                            kv_norm.reshape(1, KV_LORA), wuq, wuqr, wk, wv_t,
                            dq ** -0.5 * LOG2E)
    o_d = _flash("mla", qm, 0, km, 0, vtm, 0, None, hm)

    y = _mlp(x.reshape(B * S, D), o_c.reshape(B * S, wsb), o_d.reshape(B * S, hm * HEAD_DIM),
             w_out.astype(BF16), g_mlp, w_up.astype(BF16), w_down.astype(BF16), g_final,
             final_norm)
    return y.reshape(B, S, D)


def kernel(x, positions, norm_mix, norm_mlp, norm_final, w_in_ab, b_forget, rel_bias, w_out_ab,
           w_in_cd, q_norm, kv_norm, w_uq, w_ukv, w_out_cd, w_up, w_down):
    depth = norm_mix.shape[0]
    for layer in range(depth):
        last = layer == depth - 1
        if layer % 2 == 0:
            e = layer // 2
            x = _even_layer(x, norm_mix[layer], w_in_ab[e], b_forget[e], rel_bias[e], w_out_ab[e],
                            norm_mlp[layer], w_up[layer], w_down[layer], norm_final, last)
        else:
            o = layer // 2
            x = _odd_layer(x, positions, norm_mix[layer], w_in_cd[o], q_norm[o], kv_norm[o],
                           w_uq[o], w_ukv[o], w_out_cd[o], norm_mlp[layer], w_up[layer],
                           w_down[layer], norm_final, last)
    return x
```

```python
import functools
import math

import numpy as np
import jax
import jax.numpy as jnp
from jax import lax
from jax.experimental import pallas as pl
from jax.experimental.pallas import tpu as pltpu

F32 = jnp.float32
BF16 = jnp.bfloat16

EPS = 1e-6
HEAD_DIM = 64
CHUNK = 64
N_LEFT_CHUNKS = 8
REL_CLIP = 256
ROPE_DIM = 32
NOPE_DIM = 64
ROPE_THETA = 10000.0
Q_LORA = 384
KV_LORA = 256

LANES = 128
VT_BLK = LANES
SUB = LANES
FLASH_HP = 4
FIXED_WIDTHS = (8, 4, 2, 1)
SAFE_GAP = 80.0
NEG = -1e30
LOG2E = math.log2(math.e)
SB_ZERO_LOG = -104.0
VMEM_LIMIT = 56 * 1024 * 1024

_NT = (((1,), (1,)), ((), ()))


def _cparams(sem, flags=None):
    return pltpu.CompilerParams(dimension_semantics=sem, vmem_limit_bytes=VMEM_LIMIT, flags=flags)


def _rms_bf16(x, g):
    ms = jnp.mean(x * x, axis=-1, keepdims=True)
    return (x * lax.rsqrt(ms + EPS) * g).astype(BF16)


def _store_vt(ovt_ref, vt, row0):
    rows, tm = vt.shape
    for c in range(tm // VT_BLK):
        ovt_ref[c, row0:row0 + rows, :] = vt[:, c * VT_BLK:(c + 1) * VT_BLK]


def _inproj_kernel(x_ref, g_ref, wm_ref, wvt_ref, wa_ref, om_ref, ovt_ref, oa_ref):
    h = _rms_bf16(x_ref[...], g_ref[...])
    nm = om_ref.shape[-1]
    for c in range(0, nm, 512):
        om_ref[:, c:c + 512] = jnp.dot(
            h, wm_ref[:, c:c + 512], preferred_element_type=F32).astype(BF16)
    nv = wvt_ref.shape[0]
    for r in range(0, nv, 256):
        vt = lax.dot_general(wvt_ref[r:r + 256, :], h, _NT,
                             preferred_element_type=F32).astype(BF16)
        _store_vt(ovt_ref, vt, r)
    oa_ref[...] = jnp.dot(h, wa_ref[...], preferred_element_type=F32)


def _inproj(x, g, wm, wvt, wa, tm=512):
    B, S, D = x.shape
    nm, nv, na = wm.shape[1], wvt.shape[0], wa.shape[1]
    return pl.pallas_call(
        _inproj_kernel,
        grid=(B, S // tm),
        in_specs=[
            pl.BlockSpec((None, tm, D), lambda b, i: (b, i, 0)),
            pl.BlockSpec((1, D), lambda b, i: (0, 0)),
            pl.BlockSpec((D, nm), lambda b, i: (0, 0)),
            pl.BlockSpec((nv, D), lambda b, i: (0, 0)),
            pl.BlockSpec((D, na), lambda b, i: (0, 0)),
        ],
        out_specs=[
            pl.BlockSpec((None, tm, nm), lambda b, i: (b, i, 0)),
            pl.BlockSpec((None, tm // VT_BLK, nv, VT_BLK), lambda b, i: (b, i, 0, 0)),
            pl.BlockSpec((None, tm, na), lambda b, i: (b, i, 0)),
        ],
        out_shape=[
            jax.ShapeDtypeStruct((B, S, nm), BF16),
            jax.ShapeDtypeStruct((B, S // VT_BLK, nv, VT_BLK), BF16),
            jax.ShapeDtypeStruct((B, S, na), F32),
        ],
        compiler_params=_cparams(("arbitrary", "arbitrary")),
        name="inproj",
    )(x, g.reshape(1, D), wm, wvt, wa)


def _split3(x):
    hi = x.astype(BF16)
    r = x - hi.astype(F32)
    mid = r.astype(BF16)
    lo = (r - mid.astype(F32)).astype(BF16)
    return hi, mid, lo


AUG_W = 8


def _logcum_kernel(fa_ref, b_ref, pq_ref, pk_ref, oneq_ref, onek_ref, oq_ref, ok_ref, carry_ref):
    @pl.when(pl.program_id(1) == 0)
    def _():
        carry_ref[...] = jnp.zeros_like(carry_ref)

    z = fa_ref[...] + b_ref[...]
    lf = jnp.minimum(z, 0.0) - jnp.log(1.0 + jnp.exp(-jnp.abs(z)))
    tc = lf.shape[0]
    r = lax.broadcasted_iota(jnp.int32, (tc, tc), 0)
    c = lax.broadcasted_iota(jnp.int32, (tc, tc), 1)
    tri = jnp.where(r >= c, 1.0, 0.0).astype(BF16)
    cs = carry_ref[...]
    for part in _split3(lf):
        cs = cs + jnp.dot(tri, part, preferred_element_type=F32)
    carry_ref[...] = cs[tc - 1:tc, :]
    qa, ka = oneq_ref[...], onek_ref[...]
    for n, part in enumerate(_split3(cs * LOG2E)):
        qa = qa + jnp.dot(part, pq_ref[n], preferred_element_type=F32)
        ka = ka + jnp.dot(part, pk_ref[n], preferred_element_type=F32)
    oq_ref[...] = qa.astype(BF16)
    ok_ref[...] = ka.astype(BF16)


def _logcum(fa, bias, n_heads, tc=512):
    B, S, W = fa.shape
    na = LANES
    pq = np.zeros((3, W, na), np.float32)
    pk = np.zeros((3, W, na), np.float32)
    oneq = np.zeros((1, na), np.float32)
    onek = np.zeros((1, na), np.float32)
    for h in range(n_heads):
        base = h * AUG_W
        for n in range(3):
            pq[n, h, base + n] = 1.0
            pk[n, h, base + 3 + n] = -1.0
        oneq[0, base + 3:base + 6] = 1.0
        onek[0, base:base + 3] = 1.0
    const = lambda a: pl.BlockSpec(a.shape, lambda b, i: (0,) * a.ndim)
    args = [jnp.asarray(pq, BF16), jnp.asarray(pk, BF16), jnp.asarray(oneq), jnp.asarray(onek)]
    return pl.pallas_call(
        _logcum_kernel,
        grid=(B, S // tc),
        in_specs=[pl.BlockSpec((None, tc, W), lambda b, i: (b, i, 0)),
                  pl.BlockSpec((1, W), lambda b, i: (0, 0))] + [const(a) for a in args],
        out_specs=[pl.BlockSpec((None, tc, na), lambda b, i: (b, i, 0))] * 2,
        out_shape=[jax.ShapeDtypeStruct((B, S, na), BF16)] * 2,
        scratch_shapes=[pltpu.VMEM((1, W), F32)],
        compiler_params=_cparams(("arbitrary", "arbitrary")),
        name="logcum",
    )(fa, bias, *args)


def _pair_mask_q(q2, j):
    lane = lax.broadcasted_iota(jnp.int32, q2.shape, 1)
    keep = (lane >= HEAD_DIM * j) & (lane < HEAD_DIM * (j + 1))
    return jnp.where(keep, q2, jnp.zeros_like(q2))


ONES_ROWS = 16


def _softmax_step(tiles, vts, carry, tile_max=None):
    m, acc = carry
    if tile_max is not None:
        m_new = jnp.maximum(m, tile_max)
    else:
        m_new = m
        for tile in tiles:
            m_new = jnp.maximum(m_new, jnp.max(tile(), axis=0, keepdims=True))
    alpha = jnp.exp2(m - m_new)
    pv = None
    for tile, vt in zip(tiles, vts):
        p = jnp.exp2(tile() - m_new).astype(BF16)
        vt1 = jnp.concatenate([vt, jnp.ones((ONES_ROWS, vt.shape[1]), BF16)], axis=0)
        d = jnp.dot(vt1, p, preferred_element_type=F32)
        pv = d if pv is None else pv + d
    return m_new, alpha * acc + pv


def _softmax_init(bq):
    return (jnp.full((1, bq), NEG, F32), jnp.zeros((HEAD_DIM + ONES_ROWS, bq), F32))


def _softmax_out(carry):
    _, acc = carry
    return acc[0:HEAD_DIM] / acc[HEAD_DIM:HEAD_DIM + 1]


def _store_heads(o_ref, outs):
    oT = jnp.concatenate(outs, axis=0)
    o_ref[...] = oT.T.astype(o_ref.dtype)


def _flash_kernel(*refs, mode, hp, bq):
    if mode == "fox":
        q_ref, k_ref, vt_ref, qaug_ref, kaug_ref, o_ref = refs[:6]
    else:
        q_ref, k_ref, vt_ref, o_ref = refs[:4]
    sa_ref, sb_ref, ma_ref, mb_ref, qt_ref, kn_ref = refs[-6:]
    qs = pl.program_id(2) * bq
    sub = bq
    row = lax.broadcasted_iota(jnp.int32, (sub, bq), 0)
    col = lax.broadcasted_iota(jnp.int32, (sub, bq), 1)
    if mode == "fox":
        kcols = [slice(LANES * (h // 2), LANES * (h // 2 + 1)) for h in range(hp)]
        lane = lax.broadcasted_iota(jnp.int32, (bq, LANES), 1)
        qa = qaug_ref[...]
        qms = []
        for h in range(hp):
            first = AUG_W * (pl.program_id(1) * hp + h)
            own = (lane >= first) & (lane < first + AUG_W)
            qms.append(jnp.concatenate(
                [_pair_mask_q(q_ref[:, kcols[h]], h % 2),
                 jnp.where(own, qa, jnp.zeros_like(qa))], axis=1))
    else:
        kcols = [slice(LANES * h, LANES * (h + 1)) for h in range(hp)]
        qms = [q_ref[:, kcols[h]] for h in range(hp)]
    for h in range(hp):
        qt_ref[h] = qms[h].T

    @pl.when(pl.program_id(2) == 0)
    def _():
        klane = lax.broadcasted_iota(jnp.int32, (1, LANES), 1)
        for h in range(hp):
            kabs = jnp.max(jnp.abs(k_ref[:, kcols[h]].astype(F32)), axis=0, keepdims=True)
            sq = kabs * kabs
            if mode == "fox":
                sq = jnp.where((klane >= HEAD_DIM * (h % 2)) & (klane < HEAD_DIM * (h % 2 + 1)),
                               sq, 0.0)
            kn_ref[h] = jnp.broadcast_to(jnp.sqrt(jnp.sum(sq, axis=1, keepdims=True)), (1, bq))

    def scores(sb, h, masked):
        ks = pl.multiple_of(sb * sub, sub)
        k = k_ref[pl.ds(ks, sub), kcols[h]]
        if mode == "fox":
            k = jnp.concatenate([k, kaug_ref[pl.ds(ks, sub), :]], axis=1)
        sT = jnp.dot(k, qt_ref[h], preferred_element_type=F32)
        if masked and mode == "fox":
            sT = jnp.where(ks + row <= qs + col, sT, NEG)
        elif masked:
            sT = jnp.where(((ks + row) >> 6) <= ((qs + col) >> 6), sT, NEG)
        return sT

    nsub = bq // sub
    nvt = sub // VT_BLK

    def produce(buf, sb0, masked, h):
        s_buf, m_buf = buf
        tile_max = None
        for c in range(nsub):
            sT = scores(sb0 + c, h, masked)
            s_buf[h, c] = sT
            cm = jnp.max(sT, axis=0, keepdims=True)
            tile_max = cm if tile_max is None else jnp.maximum(tile_max, cm)
        m_buf[h] = tile_max

    def consume(buf, sb0, carry, h):
        s_buf, m_buf = buf
        tiles = [lambda c=c: s_buf[h, c] for c in range(nsub)]
        vts = [jnp.concatenate([vt_ref[(sb0 + c) * nvt + v, HEAD_DIM * h:HEAD_DIM * (h + 1), :]
                                for v in range(nvt)], axis=1) for c in range(nsub)]
        return _softmax_step(tiles, vts, carry, tile_max=m_buf[h])

    def stage(cur, cur_sb, nxt, nxt_sb, carries):
        if nxt is not None:
            for h in range(hp):
                produce(nxt, nxt_sb, False, h)
        return tuple(consume(cur, cur_sb, carries[h], h) for h in range(hp))

    n = pl.program_id(2)
    diag_sb = qs // sub
    buf_a, buf_b = (sa_ref, ma_ref), (sb_ref, mb_ref)
    for h in range(hp):
        produce(buf_a, diag_sb, True, h)

    def pair(j, carries):
        carries = stage(buf_a, jnp.where(j == 0, diag_sb, (2 * j - 1) * nsub),
                        buf_b, 2 * j * nsub, carries)
        return stage(buf_b, 2 * j * nsub,
                     buf_a, jnp.minimum(2 * j + 1, n - 1) * nsub, carries)

    def online(_):
        carries = tuple(_softmax_init(bq) for _ in range(hp))
        carries = lax.fori_loop(0, (n + 1) // 2, pair, carries)
        carries = lax.cond(
            n % 2 == 0,
            lambda c: stage(buf_a, jnp.where(n == 0, diag_sb, (n - 1) * nsub), None, None, c),
            lambda c: c, carries)
        return jnp.concatenate([_softmax_out(c) for c in carries], axis=0)

    refs_ = []
    gap = None
    for h in range(hp):
        qf = qt_ref[h, 0:LANES, :].astype(F32)
        bound = jnp.sqrt(jnp.sum(qf * qf, axis=0, keepdims=True)) * kn_ref[h] * 1.01 + 1e-3
        refs_.append(bound)
        g = jnp.max(bound - ma_ref[h])
        gap = g if gap is None else jnp.maximum(gap, g)

    def weigh_add(acc, h, s, sb0, nblk):
        p = jnp.exp2(s - refs_[h]).astype(BF16)
        vt = jnp.concatenate([vt_ref[sb0 * nvt + v, HEAD_DIM * h:HEAD_DIM * (h + 1), :]
                              for v in range(nblk * nvt)], axis=1)
        vt1 = jnp.concatenate([vt, jnp.ones((ONES_ROWS, nblk * sub), BF16)], axis=0)
        return acc + jnp.dot(vt1, p, preferred_element_type=F32)

    def fixed_reference(_):
        def run(kb, accs, nblk):
            ks = pl.multiple_of(kb * sub, sub)
            ss = []
            for h in range(hp):
                k = k_ref[pl.ds(ks, nblk * sub), kcols[h]]
                if mode == "fox":
                    k = jnp.concatenate([k, kaug_ref[pl.ds(ks, nblk * sub), :]], axis=1)
                ss.append(jnp.dot(k, qt_ref[h], preferred_element_type=F32))
            return tuple(weigh_add(accs[h], h, ss[h], kb, nblk) for h in range(hp))

        accs = tuple(jnp.zeros((HEAD_DIM + ONES_ROWS, bq), F32) for _ in range(hp))
        done = 0
        for width in FIXED_WIDTHS:
            trips = (n - done) // width
            accs = lax.fori_loop(0, trips,
                                 lambda j, a, done=done, width=width: run(done + j * width, a, width),
                                 accs)
            done = done + trips * width
        accs = [weigh_add(accs[h], h, sa_ref[h, 0], diag_sb, 1) for h in range(hp)]
        return jnp.concatenate([a[0:HEAD_DIM] / a[HEAD_DIM:HEAD_DIM + 1] for a in accs], axis=0)

    oT = lax.cond(gap <= SAFE_GAP, fixed_reference, online, None)
    o_ref[...] = oT.T.astype(o_ref.dtype)


def _flash(mode, q_arr, q_col0, k_arr, k_col0, vt_arr, vt_row0, extra, n_heads,
           hp=FLASH_HP, bq=256):
    B, S, _ = q_arr.shape
    qw = (HEAD_DIM if mode == "fox" else LANES) * hp
    vw = HEAD_DIM * hp
    in_specs = [
        pl.BlockSpec((None, bq, qw), lambda b, g, i: (b, i, q_col0 // qw + g)),
        pl.BlockSpec((None, S, qw), lambda b, g, i: (b, 0, k_col0 // qw + g)),
        pl.BlockSpec((None, S // VT_BLK, vw, VT_BLK),
                     lambda b, g, i: (b, 0, vt_row0 // vw + g, 0)),
    ]
    args = [q_arr, k_arr, vt_arr]
    if mode == "fox":
        q_aug, k_aug = extra
        in_specs += [
            pl.BlockSpec((None, bq, LANES), lambda b, g, i: (b, i, 0)),
            pl.BlockSpec((None, S, LANES), lambda b, g, i: (b, 0, 0)),
        ]
        args += [q_aug, k_aug]
    return pl.pallas_call(
        functools.partial(_flash_kernel, mode=mode, hp=hp, bq=bq),
        grid=(B, n_heads // hp, S // bq),
        in_specs=in_specs,
        out_specs=pl.BlockSpec((None, bq, vw), lambda b, g, i: (b, i, g)),
        out_shape=jax.ShapeDtypeStruct((B, S, n_heads * HEAD_DIM), BF16),
        scratch_shapes=([pltpu.VMEM((hp, 1, bq, bq), F32)] * 2
                        + [pltpu.VMEM((hp, 1, bq), F32)] * 2
                        + [pltpu.VMEM((hp, 2 * LANES if mode == "fox" else LANES, bq), BF16),
                           pltpu.VMEM((hp, 1, bq), F32)]),
        compiler_params=_cparams(("arbitrary", "arbitrary", "arbitrary")),
        name="flash_" + mode,
    )(*args)


CK_B = 2 * CHUNK
CK_NW = N_LEFT_CHUNKS * CHUNK // CK_B + 1
CK_EXT = (CK_NW + 1) * CK_B


def _chunk_kernel(q_ref, k_ref, vt_ref, ext_ref, o_ref, tab_ref, s_ref, *, hp, nq):
    i = pl.program_id(1)

    @pl.when(i == 0)
    def _():
        jj = lax.broadcasted_iota(jnp.int32, (CK_B, CK_B), 0)
        rr = lax.broadcasted_iota(jnp.int32, (CK_B, CK_B), 1)
        for h in range(hp):
            for w in range(CK_NW):
                a = (CK_NW - 1 - w) * CK_B
                g = jnp.broadcast_to(ext_ref[h:h + 1, a:a + 2 * CK_B], (CK_B, 2 * CK_B))
                t = pltpu.roll(g, CK_B, 1, stride=1, stride_axis=0)[:, :CK_B]
                if w == 0:
                    t = jnp.where((rr >= CHUNK) & (jj < CHUNK), NEG, t)
                if w == CK_NW - 1:
                    t = jnp.where((rr < CHUNK) & (jj >= CHUNK), NEG, t)
                tab_ref[h, w * CK_B:(w + 1) * CK_B, :] = t

    kcols = [slice(LANES * (h // 2), LANES * (h // 2 + 1)) for h in range(hp)]
    firsts = [i * nq + u - (CK_NW - 1) for u in range(nq)]

    def finish():
        for u in range(nq):
            kbc = [jnp.maximum(firsts[u] + w, 0) for w in range(CK_NW)]
            outs = []
            for h in range(hp):
                vt = jnp.concatenate([vt_ref[kbc[w], HEAD_DIM * h:HEAD_DIM * (h + 1), :]
                                      for w in range(CK_NW)], axis=1)
                outs.append(_softmax_out(_softmax_step([lambda u=u, h=h: s_ref[u, h]], [vt],
                                                       _softmax_init(CK_B))))
            oT = jnp.concatenate(outs, axis=0)
            o_ref[u * CK_B:(u + 1) * CK_B, :] = oT.T.astype(o_ref.dtype)

    def pair_scores(u, p, ks, nrows):
        q2 = q_ref[u * CK_B:(u + 1) * CK_B, kcols[2 * p]]
        qq = jnp.concatenate([_pair_mask_q(q2, 0), _pair_mask_q(q2, 1)], axis=0)
        return lax.dot_general(k_ref[pl.ds(ks, nrows), kcols[2 * p]], qq, _NT,
                               preferred_element_type=F32)

    @pl.when(firsts[0] >= 0)
    def _():
        for u in range(nq):
            ks = pl.multiple_of(firsts[u] * CK_B, CK_B)
            for p in range(hp // 2):
                sT = pair_scores(u, p, ks, CK_NW * CK_B)
                for j in range(2):
                    s_ref[u, 2 * p + j] = sT[:, j * CK_B:(j + 1) * CK_B] + tab_ref[2 * p + j]
        finish()

    @pl.when(firsts[0] < 0)
    def _():
        for u in range(nq):
            for p in range(hp // 2):
                for w in range(CK_NW):
                    rows = slice(w * CK_B, (w + 1) * CK_B)
                    ks = pl.multiple_of(jnp.maximum(firsts[u] + w, 0) * CK_B, CK_B)
                    sT = pair_scores(u, p, ks, CK_B)
                    for j in range(2):
                        s_ref[u, 2 * p + j, rows, :] = jnp.where(
                            firsts[u] + w >= 0,
                            sT[:, j * CK_B:(j + 1) * CK_B] + tab_ref[2 * p + j, rows, :], NEG)
        finish()


def _chunk_attn(main, q_col0, k_col0, vt_arr, vt_row0, ext, n_heads, nq=4):
    B, S, _ = main.shape
    hp = n_heads
    qw, vw = HEAD_DIM * hp, HEAD_DIM * hp
    return pl.pallas_call(
        functools.partial(_chunk_kernel, hp=hp, nq=nq),
        grid=(B, S // (nq * CK_B)),
        in_specs=[
            pl.BlockSpec((None, nq * CK_B, qw), lambda b, i: (b, i, q_col0 // qw)),
            pl.BlockSpec((None, S, qw), lambda b, i: (b, 0, k_col0 // qw)),
            pl.BlockSpec((None, S // VT_BLK, vw, VT_BLK), lambda b, i: (b, 0, vt_row0 // vw, 0)),
            pl.BlockSpec((hp, CK_EXT), lambda b, i: (0, 0)),
        ],
        out_specs=pl.BlockSpec((None, nq * CK_B, vw), lambda b, i: (b, i, 0)),
        out_shape=jax.ShapeDtypeStruct((B, S, n_heads * HEAD_DIM), BF16),
        scratch_shapes=[pltpu.VMEM((hp, CK_NW * CK_B, CK_B), F32),
                        pltpu.VMEM((nq, hp, CK_NW * CK_B, CK_B), F32)],
        compiler_params=_cparams(("arbitrary", "arbitrary")),
        name="chunk_attn",
    )(main, main, vt_arr, ext)


def _sb_kernel(q_ref, k_ref, vt_ref, o_ref, z_ref, lb_ref, sfx_ref, *, hp, bq):
    qs = pl.program_id(1) * bq
    nsub = bq // SUB
    row = lax.broadcasted_iota(jnp.int32, (SUB, bq), 0)
    col = lax.broadcasted_iota(jnp.int32, (SUB, bq), 1)
    ur = lax.broadcasted_iota(jnp.int32, (SUB, 2 * SUB), 0)
    uc = lax.broadcasted_iota(jnp.int32, (SUB, 2 * SUB), 1) & (SUB - 1)
    upper2 = jnp.where(uc > ur, 1.0, 0.0).astype(BF16)
    kcols = [slice(LANES * (h // 2), LANES * (h // 2 + 1)) for h in range(hp)]
    qms = [_pair_mask_q(q_ref[:, kcols[h]], h % 2) for h in range(hp)]

    def step(kb, carries, masked):
        ks = pl.multiple_of(kb * bq, bq)
        for h in range(hp):
            z_ref[h] = lax.dot_general(k_ref[pl.ds(ks, bq), kcols[h]], qms[h], _NT,
                                       preferred_element_type=F32)
        first_col = [c * SUB if masked else 0 for c in range(nsub)]

        def widen(x, c):
            if first_col[c] == 0:
                return x
            return jnp.concatenate([jnp.zeros((x.shape[0], first_col[c]), x.dtype), x], axis=1)

        totals = []
        for h in range(hp):
            tot = []
            for c in range(nsub):
                rows, cols = slice(c * SUB, (c + 1) * SUB), slice(first_col[c], bq)
                z = z_ref[h, rows, cols]
                l1 = jnp.log(1.0 + jnp.exp(-jnp.abs(z)))
                log_beta = jnp.minimum(z, 0.0) - l1
                log_keep = log_beta - z
                if masked:
                    valid = (ks + c * SUB + row < qs + col)[:, cols]
                    log_keep = jnp.where(valid, log_keep, 0.0)
                lb_ref[h, rows, cols] = log_beta
                hi = log_keep.astype(BF16)
                lo = (log_keep - hi.astype(F32)).astype(BF16)
                sfx = jnp.dot(upper2, jnp.concatenate([hi, lo], axis=0),
                              preferred_element_type=F32)
                sfx_ref[h, rows, cols] = sfx
                tot.append(widen(sfx[0:1, :] + log_keep[0:1, :], c))
            totals.append(tot)
        out = []
        for h in range(hp):
            tail, acc = carries[h]
            parts = [None] * nsub
            for c in range(nsub - 1, -1, -1):
                rows, cols = slice(c * SUB, (c + 1) * SUB), slice(first_col[c], bq)
                a = jnp.exp(lb_ref[h, rows, cols] + sfx_ref[h, rows, cols] + tail[:, cols])
                if masked:
                    a = jnp.where((ks + c * SUB + row < qs + col)[:, cols], a, 0.0)
                parts[c] = a.astype(BF16)
                tail = tail + totals[h][c]
            vts = [vt_ref[kb * nsub + c, HEAD_DIM * h:HEAD_DIM * (h + 1), :] for c in range(nsub)]
            if masked:
                for c in range(nsub):
                    acc = acc + widen(jnp.dot(vts[c], parts[c], preferred_element_type=F32), c)
            else:
                acc = acc + jnp.dot(jnp.concatenate(vts, axis=1), jnp.concatenate(parts, axis=0),
                                    preferred_element_type=F32)
            out.append((tail, acc))
        return tuple(out)

    n_full = qs // bq
    carries = tuple((jnp.zeros((1, bq), F32), jnp.zeros((HEAD_DIM, bq), F32))
                    for _ in range(hp))
    carries = step(n_full, carries, True)

    def cond(state):
        kb, carries = state
        tail_max = carries[0][0]
        for h in range(1, hp):
            tail_max = jnp.maximum(tail_max, carries[h][0])
        return (kb >= 0) & (jnp.max(tail_max) > SB_ZERO_LOG)

    def body(state):
        kb, carries = state
        return kb - 1, step(kb, carries, False)

    _, carries = lax.while_loop(cond, body, (n_full - 1, carries))
    _store_heads(o_ref, [acc for (_, acc) in carries])


def _sb_attn(main, q_col0, k_col0, vt_arr, vt_row0, n_heads, bq=256):
    B, S, _ = main.shape
    hp = n_heads
    qw = HEAD_DIM * hp
    return pl.pallas_call(
        functools.partial(_sb_kernel, hp=hp, bq=bq),
        grid=(B, S // bq),
        in_specs=[
            pl.BlockSpec((None, bq, qw), lambda b, i: (b, i, q_col0 // qw)),
            pl.BlockSpec((None, S, qw), lambda b, i: (b, 0, k_col0 // qw)),
            pl.BlockSpec((None, S // VT_BLK, qw, VT_BLK), lambda b, i: (b, 0, vt_row0 // qw, 0)),
        ],
        out_specs=pl.BlockSpec((None, bq, qw), lambda b, i: (b, i, 0)),
        out_shape=jax.ShapeDtypeStruct((B, S, n_heads * HEAD_DIM), BF16),
        scratch_shapes=[pltpu.VMEM((hp, bq, bq), F32)] * 3,
        compiler_params=_cparams(("arbitrary", "arbitrary")),
        name="sb_attn",
    )(main, main, vt_arr)


def _mla_prep_kernel(aux_ref, pos_ref, invf_ref, qn_ref, kvn_ref, wuq_ref, wuqr_ref,
                     wk_ref, wvt_ref, oq_ref, ok_ref, ovt_ref, *, q_scale):
    ang = pos_ref[...] * invf_ref[...]
    cos4, sin4 = jnp.cos(ang), jnp.sin(ang)
    lane = lax.broadcasted_iota(jnp.int32, ang.shape, 1)
    rotary = (lane >= NOPE_DIM) & (lane < NOPE_DIM + ROPE_DIM)
    cos_rows, sin_rows = [], []
    for m in range(LANES // ROPE_DIM):
        shift = (NOPE_DIM - ROPE_DIM * m) % LANES
        cm = cos4 if shift == 0 else pltpu.roll(cos4, shift, 1)
        sm = sin4 if shift == 0 else pltpu.roll(sin4, shift, 1)
        cos_rows.append(jnp.where(rotary, cm, 1.0))
        sin_rows.append(jnp.where(rotary, sm, 0.0))
    cos = jnp.concatenate(cos_rows, axis=0)
    sin = jnp.concatenate(sin_rows, axis=0)
    cq = _rms_bf16(aux_ref[:, 0:Q_LORA], qn_ref[...])
    ckv = _rms_bf16(aux_ref[:, Q_LORA:Q_LORA + KV_LORA], kvn_ref[...])
    o = Q_LORA + KV_LORA
    k_rope = aux_ref[:, o:o + LANES] * cos + aux_ref[:, o + LANES:o + 2 * LANES] * sin
    cos_q, sin_q = cos * q_scale, sin * q_scale
    n_heads = oq_ref.shape[-1] // LANES
    for h in range(0, n_heads, 2):
        cols = slice(h * LANES, (h + 2) * LANES)
        qa = jnp.dot(cq, wuq_ref[:, cols], preferred_element_type=F32)
        qb = jnp.dot(cq, wuqr_ref[:, cols], preferred_element_type=F32)
        kn = jnp.dot(ckv, wk_ref[:, cols], preferred_element_type=F32)
        for d in range(2):
            c1 = slice(d * LANES, (d + 1) * LANES)
            c2 = slice((h + d) * LANES, (h + d + 1) * LANES)
            oq_ref[:, c2] = (qa[:, c1] * cos_q + qb[:, c1] * sin_q).astype(BF16)
            ok_ref[:, c2] = (kn[:, c1] + k_rope).astype(BF16)
    nv = wvt_ref.shape[0]
    for r in range(0, nv, 256):
        vt = lax.dot_general(wvt_ref[r:r + 256, :], ckv, _NT,
                             preferred_element_type=F32).astype(BF16)
        _store_vt(ovt_ref, vt, r)


MLA_TM = 512


def _mla_prep(aux, pos, invf, qn, kvn, wuq, wuqr, wk, wvt, q_scale, tm=MLA_TM):
    B, S, na = aux.shape
    nq, nv = wuq.shape[1], wvt.shape[0]
    full = lambda a: pl.BlockSpec(a.shape, lambda b, i: (0,) * a.ndim)
    return pl.pallas_call(
        functools.partial(_mla_prep_kernel, q_scale=q_scale),
        grid=(B, S // tm),
        in_specs=[
            pl.BlockSpec((None, tm, na), lambda b, i: (b, i, 0)),
            pl.BlockSpec((None, None) + pos.shape[2:], lambda b, i: (b, i, 0, 0)),
            full(invf), full(qn), full(kvn), full(wuq), full(wuqr), full(wk), full(wvt),
        ],
        out_specs=[
            pl.BlockSpec((None, tm, nq), lambda b, i: (b, i, 0)),
            pl.BlockSpec((None, tm, nq), lambda b, i: (b, i, 0)),
            pl.BlockSpec((None, tm // VT_BLK, nv, VT_BLK), lambda b, i: (b, i, 0, 0)),
        ],
        out_shape=[
            jax.ShapeDtypeStruct((B, S, nq), BF16),
            jax.ShapeDtypeStruct((B, S, nq), BF16),
            jax.ShapeDtypeStruct((B, S // VT_BLK, nv, VT_BLK), BF16),
        ],
        compiler_params=_cparams(("arbitrary", "arbitrary")),
        name="mla_prep",
    )(aux, pos, invf, qn, kvn, wuq, wuqr, wk, wvt)


def _mlp_kernel(x_ref, oa_ref, ob_ref, wo_ref, g_ref, wu_ref, wd_ref, gf_ref, out_ref,
                x1_ref, h_ref, a_ref, *, final_norm, tf):
    na = oa_ref.shape[-1]
    x1 = (x_ref[...]
          + jnp.dot(oa_ref[...], wo_ref[0:na, :], preferred_element_type=F32)
          + jnp.dot(ob_ref[...], wo_ref[na:, :], preferred_element_type=F32))
    x1_ref[...] = x1
    h_ref[...] = _rms_bf16(x1, g_ref[...])
    for f in range(0, wu_ref.shape[1], tf):
        u = jnp.dot(h_ref[...], wu_ref[:, f:f + tf], preferred_element_type=F32)
        a_ref[:, f:f + tf] = jnp.square(jnp.maximum(u, 0.0)).astype(BF16)
    y = x1_ref[...] + jnp.dot(a_ref[...], wd_ref[...], preferred_element_type=F32)
    if final_norm:
        ms = jnp.mean(y * y, axis=-1, keepdims=True)
        y = y * lax.rsqrt(ms + EPS) * gf_ref[...]
    out_ref[...] = y


def _mlp(x2, oa, ob, wo, g, wu, wd, gf, final_norm, tm=512, tf=1024):
    T, D = x2.shape
    F = wu.shape[1]
    na, nb = oa.shape[1], ob.shape[1]
    const = lambda shape: pl.BlockSpec(shape, lambda i: (0, 0), pipeline_mode=pl.Buffered(1))
    return pl.pallas_call(
        functools.partial(_mlp_kernel, final_norm=final_norm, tf=tf),
        grid=(T // tm,),
        in_specs=[
            pl.BlockSpec((tm, D), lambda i: (i, 0)),
            pl.BlockSpec((tm, na), lambda i: (i, 0)),
            pl.BlockSpec((tm, nb), lambda i: (i, 0)),
            const((na + nb, D)),
            const((1, D)),
            const((D, F)),
            const((F, D)),
            const((1, D)),
        ],
        out_specs=pl.BlockSpec((tm, D), lambda i: (i, 0)),
        out_shape=jax.ShapeDtypeStruct((T, D), F32),
        scratch_shapes=[pltpu.VMEM((tm, D), F32), pltpu.VMEM((tm, D), BF16),
                        pltpu.VMEM((tm, F), BF16)],
        compiler_params=_cparams(("arbitrary",)),
        name="mlp",
    )(x2, oa, ob, wo, g.reshape(1, D), wu, wd, gf.reshape(1, D))


def _pad_cols(w, n):
    return jnp.pad(w, ((0, 0), (0, n - w.shape[1])))


def _rot_cols(w):
    half = ROPE_DIM // 2
    return jnp.concatenate([-w[:, half:], w[:, :half]], axis=1)


def _rope_slab(w):
    z = jnp.zeros((w.shape[0], NOPE_DIM), w.dtype)
    return jnp.concatenate([z, w, jnp.zeros((w.shape[0], LANES - NOPE_DIM - ROPE_DIM), w.dtype)], axis=1)


def _even_layer(x, g_mix, w_in, b_forget, rel_bias, w_out, g_mlp, w_up, w_down, g_final,
                final_norm):
    B, S, D = x.shape
    hf, hc = b_forget.shape[0], rel_bias.shape[0]
    wf, wc = hf * HEAD_DIM, hc * HEAD_DIM
    o = np.cumsum([0, wf, wf, wf, hf, wc, wc, wc])
    qa, ka, va, fa, qb, kb, vb = [w_in[:, o[n]:o[n + 1]] for n in range(7)]
    q_scale = HEAD_DIM ** -0.5 * LOG2E
    wm = jnp.concatenate([qa * q_scale, ka, qb * q_scale, kb], axis=1).astype(BF16)
    wvt = jnp.concatenate([va, vb], axis=1).T.astype(BF16)
    wa = _pad_cols(fa, LANES).astype(BF16)
    main, vt, aux = _inproj(x, g_mix, wm, wvt, wa)

    q_aug, k_aug = _logcum(aux, _pad_cols(b_forget.reshape(1, hf), LANES), hf)
    o_a = _flash("fox", main, 0, main, wf, vt, 0, (q_aug, k_aug), hf)

    right = CK_EXT - rel_bias.shape[1] - (CHUNK + 1)
    ext = jnp.pad(rel_bias * LOG2E, ((0, 0), (CHUNK + 1, right)), mode="edge")
    o_b = _chunk_attn(main, 2 * wf, 2 * wf + wc, vt, wf, ext, hc)

    y = _mlp(x.reshape(B * S, D), o_a.reshape(B * S, wf), o_b.reshape(B * S, wc),
             w_out.astype(BF16), g_mlp, w_up.astype(BF16), w_down.astype(BF16), g_final,
             final_norm)
    return y.reshape(B, S, D)


def _odd_layer(x, positions, g_mix, w_in, q_norm, kv_norm, w_uq, w_ukv, w_out, g_mlp, w_up,
               w_down, g_final, final_norm):
    B, S, D = x.shape
    hm = w_ukv.shape[1] // (NOPE_DIM + HEAD_DIM)
    ws = w_in.shape[1] - Q_LORA - KV_LORA - ROPE_DIM
    hs = (ws // 3) // HEAD_DIM
    wsb = hs * HEAD_DIM
    o = np.cumsum([0, wsb, wsb, wsb, Q_LORA, KV_LORA, ROPE_DIM])
    qc, kc, vc, w_cq, w_ckv, w_kr = [w_in[:, o[n]:o[n + 1]] for n in range(6)]
    wm = jnp.concatenate([qc * HEAD_DIM ** -0.5, kc], axis=1).astype(BF16)
    wa = jnp.concatenate([w_cq, w_ckv, _rope_slab(w_kr), _rope_slab(_rot_cols(w_kr))],
                         axis=1).astype(BF16)
    main, vt, aux = _inproj(x, g_mix, wm, vc.T.astype(BF16), wa)
    o_c = _sb_attn(main, 0, wsb, vt, 0, hs)

    dq = NOPE_DIM + ROPE_DIM
    wuq3 = w_uq.reshape(Q_LORA, hm, dq)
    nope, ropew = wuq3[:, :, :NOPE_DIM], wuq3[:, :, NOPE_DIM:]
    zq = jnp.zeros((Q_LORA, hm, LANES - dq), w_uq.dtype)
    wuq = jnp.concatenate([nope, ropew, zq], axis=2).reshape(Q_LORA, hm * LANES).astype(BF16)
    half = ROPE_DIM // 2
    ropr = jnp.concatenate([-ropew[:, :, half:], ropew[:, :, :half]], axis=2)
    wuqr = jnp.concatenate([jnp.zeros_like(nope), ropr, zq], axis=2)
    wuqr = wuqr.reshape(Q_LORA, hm * LANES).astype(BF16)
    wkv3 = w_ukv.reshape(KV_LORA, hm, NOPE_DIM + HEAD_DIM)
    wk = jnp.concatenate([wkv3[:, :, :NOPE_DIM],
                          jnp.zeros((KV_LORA, hm, LANES - NOPE_DIM), w_ukv.dtype)], axis=2)
    wk = wk.reshape(KV_LORA, hm * LANES).astype(BF16)
    wv_t = wkv3[:, :, NOPE_DIM:].reshape(KV_LORA, hm * HEAD_DIM).T.astype(BF16)
    freqs = (ROPE_THETA ** (-jnp.arange(half, dtype=F32) / half))
    invf = jnp.tile(freqs, 2 * LANES // ROPE_DIM).reshape(1, LANES)
    groups = LANES // ROPE_DIM
    pos = positions.astype(F32).reshape(B, S // MLA_TM, groups, MLA_TM // groups)
    pos = jnp.repeat(jnp.swapaxes(pos, 2, 3), ROPE_DIM, axis=-1)
    qm, km, vtm = _mla_prep(aux, pos, invf, q_norm.reshape(1, Q_LORA),
                            kv_norm.reshape(1, KV_LORA), wuq, wuqr, wk, wv_t,
                            dq ** -0.5 * LOG2E)
    o_d = _flash("mla", qm, 0, km, 0, vtm, 0, None, hm)

    y = _mlp(x.reshape(B * S, D), o_c.reshape(B * S, wsb), o_d.reshape(B * S, hm * HEAD_DIM),
             w_out.astype(BF16), g_mlp, w_up.astype(BF16), w_down.astype(BF16), g_final,
             final_norm)
    return y.reshape(B, S, D)


def kernel(x, positions, norm_mix, norm_mlp, norm_final, w_in_ab, b_forget, rel_bias, w_out_ab,
           w_in_cd, q_norm, kv_norm, w_uq, w_ukv, w_out_cd, w_up, w_down):
    depth = norm_mix.shape[0]
    for layer in range(depth):
        last = layer == depth - 1
        if layer % 2 == 0:
            e = layer // 2
            x = _even_layer(x, norm_mix[layer], w_in_ab[e], b_forget[e], rel_bias[e], w_out_ab[e],
                            norm_mlp[layer], w_up[layer], w_down[layer], norm_final, last)
        else:
            o = layer // 2
            x = _odd_layer(x, positions, norm_mix[layer], w_in_cd[o], q_norm[o], kv_norm[o],
                           w_uq[o], w_ukv[o], w_out_cd[o], norm_mlp[layer], w_up[layer],
                           w_down[layer], norm_final, last)
    return x
```

```python
import functools
import math

import numpy as np
import jax
import jax.numpy as jnp
from jax import lax
from jax.experimental import pallas as pl
from jax.experimental.pallas import tpu as pltpu

F32 = jnp.float32
BF16 = jnp.bfloat16

EPS = 1e-6
HEAD_DIM = 64
CHUNK = 64
N_LEFT_CHUNKS = 8
REL_CLIP = 256
ROPE_DIM = 32
NOPE_DIM = 64
ROPE_THETA = 10000.0
Q_LORA = 384
KV_LORA = 256

LANES = 128
VT_BLK = LANES
SUB = LANES
FLASH_HP = 8
FIXED_WIDTHS = (8, 4, 2, 1)
SAFE_GAP = 80.0
NEG = -1e30
LOG2E = math.log2(math.e)
SB_ZERO_LOG = -104.0
VMEM_LIMIT = 56 * 1024 * 1024

_NT = (((1,), (1,)), ((), ()))


def _cparams(sem, flags=None):
    return pltpu.CompilerParams(dimension_semantics=sem, vmem_limit_bytes=VMEM_LIMIT, flags=flags)


def _rms_bf16(x, g):
    ms = jnp.mean(x * x, axis=-1, keepdims=True)
    return (x * lax.rsqrt(ms + EPS) * g).astype(BF16)


def _store_vt(ovt_ref, vt, row0):
    rows, tm = vt.shape
    for c in range(tm // VT_BLK):
        ovt_ref[c, row0:row0 + rows, :] = vt[:, c * VT_BLK:(c + 1) * VT_BLK]


def _inproj_kernel(x_ref, g_ref, wm_ref, wvt_ref, wa_ref, om_ref, ovt_ref, oa_ref):
    h = _rms_bf16(x_ref[...], g_ref[...])
    nm = om_ref.shape[-1]
    for c in range(0, nm, 512):
        om_ref[:, c:c + 512] = jnp.dot(
            h, wm_ref[:, c:c + 512], preferred_element_type=F32).astype(BF16)
    nv = wvt_ref.shape[0]
    for r in range(0, nv, 256):
        vt = lax.dot_general(wvt_ref[r:r + 256, :], h, _NT,
                             preferred_element_type=F32).astype(BF16)
        _store_vt(ovt_ref, vt, r)
    oa_ref[...] = jnp.dot(h, wa_ref[...], preferred_element_type=F32)


def _inproj(x, g, wm, wvt, wa, tm=512):
    B, S, D = x.shape
    nm, nv, na = wm.shape[1], wvt.shape[0], wa.shape[1]
    return pl.pallas_call(
        _inproj_kernel,
        grid=(B, S // tm),
        in_specs=[
            pl.BlockSpec((None, tm, D), lambda b, i: (b, i, 0)),
            pl.BlockSpec((1, D), lambda b, i: (0, 0)),
            pl.BlockSpec((D, nm), lambda b, i: (0, 0)),
            pl.BlockSpec((nv, D), lambda b, i: (0, 0)),
            pl.BlockSpec((D, na), lambda b, i: (0, 0)),
        ],
        out_specs=[
            pl.BlockSpec((None, tm, nm), lambda b, i: (b, i, 0)),
            pl.BlockSpec((None, tm // VT_BLK, nv, VT_BLK), lambda b, i: (b, i, 0, 0)),
            pl.BlockSpec((None, tm, na), lambda b, i: (b, i, 0)),
        ],
        out_shape=[
            jax.ShapeDtypeStruct((B, S, nm), BF16),
            jax.ShapeDtypeStruct((B, S // VT_BLK, nv, VT_BLK), BF16),
            jax.ShapeDtypeStruct((B, S, na), F32),
        ],
        compiler_params=_cparams(("arbitrary", "arbitrary")),
        name="inproj",
    )(x, g.reshape(1, D), wm, wvt, wa)


def _split3(x):
    hi = x.astype(BF16)
    r = x - hi.astype(F32)
    mid = r.astype(BF16)
    lo = (r - mid.astype(F32)).astype(BF16)
    return hi, mid, lo


AUG_W = 8


def _logcum_kernel(fa_ref, b_ref, pq_ref, pk_ref, oneq_ref, onek_ref, oq_ref, ok_ref, carry_ref):
    @pl.when(pl.program_id(1) == 0)
    def _():
        carry_ref[...] = jnp.zeros_like(carry_ref)

    z = fa_ref[...] + b_ref[...]
    lf = jnp.minimum(z, 0.0) - jnp.log(1.0 + jnp.exp(-jnp.abs(z)))
    tc = lf.shape[0]
    r = lax.broadcasted_iota(jnp.int32, (tc, tc), 0)
    c = lax.broadcasted_iota(jnp.int32, (tc, tc), 1)
    tri = jnp.where(r >= c, 1.0, 0.0).astype(BF16)
    cs = carry_ref[...]
    for part in _split3(lf):
        cs = cs + jnp.dot(tri, part, preferred_element_type=F32)
    carry_ref[...] = cs[tc - 1:tc, :]
    qa, ka = oneq_ref[...], onek_ref[...]
    for n, part in enumerate(_split3(cs * LOG2E)):
        qa = qa + jnp.dot(part, pq_ref[n], preferred_element_type=F32)
        ka = ka + jnp.dot(part, pk_ref[n], preferred_element_type=F32)
    oq_ref[...] = qa.astype(BF16)
    ok_ref[...] = ka.astype(BF16)


def _logcum(fa, bias, n_heads, tc=512):
    B, S, W = fa.shape
    na = LANES
    pq = np.zeros((3, W, na), np.float32)
    pk = np.zeros((3, W, na), np.float32)
    oneq = np.zeros((1, na), np.float32)
    onek = np.zeros((1, na), np.float32)
    for h in range(n_heads):
        base = h * AUG_W
        for n in range(3):
            pq[n, h, base + n] = 1.0
            pk[n, h, base + 3 + n] = -1.0
        oneq[0, base + 3:base + 6] = 1.0
        onek[0, base:base + 3] = 1.0
    const = lambda a: pl.BlockSpec(a.shape, lambda b, i: (0,) * a.ndim)
    args = [jnp.asarray(pq, BF16), jnp.asarray(pk, BF16), jnp.asarray(oneq), jnp.asarray(onek)]
    return pl.pallas_call(
        _logcum_kernel,
        grid=(B, S // tc),
        in_specs=[pl.BlockSpec((None, tc, W), lambda b, i: (b, i, 0)),
                  pl.BlockSpec((1, W), lambda b, i: (0, 0))] + [const(a) for a in args],
        out_specs=[pl.BlockSpec((None, tc, na), lambda b, i: (b, i, 0))] * 2,
        out_shape=[jax.ShapeDtypeStruct((B, S, na), BF16)] * 2,
        scratch_shapes=[pltpu.VMEM((1, W), F32)],
        compiler_params=_cparams(("arbitrary", "arbitrary")),
        name="logcum",
    )(fa, bias, *args)


def _pair_mask_q(q2, j):
    lane = lax.broadcasted_iota(jnp.int32, q2.shape, 1)
    keep = (lane >= HEAD_DIM * j) & (lane < HEAD_DIM * (j + 1))
    return jnp.where(keep, q2, jnp.zeros_like(q2))


ONES_ROWS = 16


def _softmax_step(tiles, vts, carry, tile_max=None):
    m, acc = carry
    if tile_max is not None:
        m_new = jnp.maximum(m, tile_max)
    else:
        m_new = m
        for tile in tiles:
            m_new = jnp.maximum(m_new, jnp.max(tile(), axis=0, keepdims=True))
    alpha = jnp.exp2(m - m_new)
    pv = None
    for tile, vt in zip(tiles, vts):
        p = jnp.exp2(tile() - m_new).astype(BF16)
        vt1 = jnp.concatenate([vt, jnp.ones((ONES_ROWS, vt.shape[1]), BF16)], axis=0)
        d = jnp.dot(vt1, p, preferred_element_type=F32)
        pv = d if pv is None else pv + d
    return m_new, alpha * acc + pv


def _softmax_init(bq):
    return (jnp.full((1, bq), NEG, F32), jnp.zeros((HEAD_DIM + ONES_ROWS, bq), F32))


def _softmax_out(carry):
    _, acc = carry
    return acc[0:HEAD_DIM] / acc[HEAD_DIM:HEAD_DIM + 1]


def _store_heads(o_ref, outs):
    oT = jnp.concatenate(outs, axis=0)
    o_ref[...] = oT.T.astype(o_ref.dtype)


def _flash_kernel(*refs, mode, hp, bq):
    if mode == "fox":
        q_ref, k_ref, vt_ref, qaug_ref, kaug_ref, o_ref = refs[:6]
    else:
        q_ref, k_ref, vt_ref, o_ref = refs[:4]
    sa_ref, sb_ref, ma_ref, mb_ref, qt_ref, kn_ref = refs[-6:]
    qs = pl.program_id(2) * bq
    sub = bq
    row = lax.broadcasted_iota(jnp.int32, (sub, bq), 0)
    col = lax.broadcasted_iota(jnp.int32, (sub, bq), 1)
    if mode == "fox":
        kcols = [slice(LANES * (h // 2), LANES * (h // 2 + 1)) for h in range(hp)]
        lane = lax.broadcasted_iota(jnp.int32, (bq, LANES), 1)
        qa = qaug_ref[...]
        qms = []
        for h in range(hp):
            first = AUG_W * (pl.program_id(1) * hp + h)
            own = (lane >= first) & (lane < first + AUG_W)
            qms.append(jnp.concatenate(
                [_pair_mask_q(q_ref[:, kcols[h]], h % 2),
                 jnp.where(own, qa, jnp.zeros_like(qa))], axis=1))
    else:
        kcols = [slice(LANES * h, LANES * (h + 1)) for h in range(hp)]
        qms = [q_ref[:, kcols[h]] for h in range(hp)]
    for h in range(hp):
        qt_ref[h] = qms[h].T

    @pl.when(pl.program_id(2) == 0)
    def _():
        klane = lax.broadcasted_iota(jnp.int32, (1, LANES), 1)
        for h in range(hp):
            kabs = jnp.max(jnp.abs(k_ref[:, kcols[h]].astype(F32)), axis=0, keepdims=True)
            sq = kabs * kabs
            if mode == "fox":
                sq = jnp.where((klane >= HEAD_DIM * (h % 2)) & (klane < HEAD_DIM * (h % 2 + 1)),
                               sq, 0.0)
            kn_ref[h] = jnp.broadcast_to(jnp.sqrt(jnp.sum(sq, axis=1, keepdims=True)), (1, bq))

    def scores(sb, h, masked):
        ks = pl.multiple_of(sb * sub, sub)
        k = k_ref[pl.ds(ks, sub), kcols[h]]
        if mode == "fox":
            k = jnp.concatenate([k, kaug_ref[pl.ds(ks, sub), :]], axis=1)
        sT = jnp.dot(k, qt_ref[h], preferred_element_type=F32)
        if masked and mode == "fox":
            sT = jnp.where(ks + row <= qs + col, sT, NEG)
        elif masked:
            sT = jnp.where(((ks + row) >> 6) <= ((qs + col) >> 6), sT, NEG)
        return sT

    nsub = bq // sub
    nvt = sub // VT_BLK

    def produce(buf, sb0, masked, h):
        s_buf, m_buf = buf
        tile_max = None
        for c in range(nsub):
            sT = scores(sb0 + c, h, masked)
            s_buf[h, c] = sT
            cm = jnp.max(sT, axis=0, keepdims=True)
            tile_max = cm if tile_max is None else jnp.maximum(tile_max, cm)
        m_buf[h] = tile_max

    def consume(buf, sb0, carry, h):
        s_buf, m_buf = buf
        tiles = [lambda c=c: s_buf[h, c] for c in range(nsub)]
        vts = [jnp.concatenate([vt_ref[(sb0 + c) * nvt + v, HEAD_DIM * h:HEAD_DIM * (h + 1), :]
                                for v in range(nvt)], axis=1) for c in range(nsub)]
        return _softmax_step(tiles, vts, carry, tile_max=m_buf[h])

    def stage(cur, cur_sb, nxt, nxt_sb, carries):
        if nxt is not None:
            for h in range(hp):
                produce(nxt, nxt_sb, False, h)
        return tuple(consume(cur, cur_sb, carries[h], h) for h in range(hp))

    n = pl.program_id(2)
    diag_sb = qs // sub
    buf_a, buf_b = (sa_ref, ma_ref), (sb_ref, mb_ref)
    for h in range(hp):
        produce(buf_a, diag_sb, True, h)

    def pair(j, carries):
        carries = stage(buf_a, jnp.where(j == 0, diag_sb, (2 * j - 1) * nsub),
                        buf_b, 2 * j * nsub, carries)
        return stage(buf_b, 2 * j * nsub,
                     buf_a, jnp.minimum(2 * j + 1, n - 1) * nsub, carries)

    def online(_):
        carries = tuple(_softmax_init(bq) for _ in range(hp))
        carries = lax.fori_loop(0, (n + 1) // 2, pair, carries)
        carries = lax.cond(
            n % 2 == 0,
            lambda c: stage(buf_a, jnp.where(n == 0, diag_sb, (n - 1) * nsub), None, None, c),
            lambda c: c, carries)
        return jnp.concatenate([_softmax_out(c) for c in carries], axis=0)

    refs_ = []
    gap = None
    for h in range(hp):
        qf = qt_ref[h, 0:LANES, :].astype(F32)
        bound = jnp.sqrt(jnp.sum(qf * qf, axis=0, keepdims=True)) * kn_ref[h] * 1.01 + 1e-3
        refs_.append(bound)
        g = jnp.max(bound - ma_ref[h])
        gap = g if gap is None else jnp.maximum(gap, g)

    def weigh_add(acc, h, s, sb0, nblk):
        p = jnp.exp2(s - refs_[h]).astype(BF16)
        vt = jnp.concatenate([vt_ref[sb0 * nvt + v, HEAD_DIM * h:HEAD_DIM * (h + 1), :]
                              for v in range(nblk * nvt)], axis=1)
        vt1 = jnp.concatenate([vt, jnp.ones((ONES_ROWS, nblk * sub), BF16)], axis=0)
        return acc + jnp.dot(vt1, p, preferred_element_type=F32)

    def fixed_reference(_):
        def run(kb, accs, nblk):
            ks = pl.multiple_of(kb * sub, sub)
            ss = []
            for h in range(hp):
                k = k_ref[pl.ds(ks, nblk * sub), kcols[h]]
                if mode == "fox":
                    k = jnp.concatenate([k, kaug_ref[pl.ds(ks, nblk * sub), :]], axis=1)
                ss.append(jnp.dot(k, qt_ref[h], preferred_element_type=F32))
            return tuple(weigh_add(accs[h], h, ss[h], kb, nblk) for h in range(hp))

        accs = tuple(jnp.zeros((HEAD_DIM + ONES_ROWS, bq), F32) for _ in range(hp))
        done = 0
        for width in FIXED_WIDTHS:
            trips = (n - done) // width
            accs = lax.fori_loop(0, trips,
                                 lambda j, a, done=done, width=width: run(done + j * width, a, width),
                                 accs)
            done = done + trips * width
        accs = [weigh_add(accs[h], h, sa_ref[h, 0], diag_sb, 1) for h in range(hp)]
        return jnp.concatenate([a[0:HEAD_DIM] / a[HEAD_DIM:HEAD_DIM + 1] for a in accs], axis=0)

    oT = lax.cond(gap <= SAFE_GAP, fixed_reference, online, None)
    o_ref[...] = oT.T.astype(o_ref.dtype)


def _flash(mode, q_arr, q_col0, k_arr, k_col0, vt_arr, vt_row0, extra, n_heads,
           hp=FLASH_HP, bq=256):
    B, S, _ = q_arr.shape
    qw = (HEAD_DIM if mode == "fox" else LANES) * hp
    vw = HEAD_DIM * hp
    in_specs = [
        pl.BlockSpec((None, bq, qw), lambda b, g, i: (b, i, q_col0 // qw + g)),
        pl.BlockSpec((None, S, qw), lambda b, g, i: (b, 0, k_col0 // qw + g)),
        pl.BlockSpec((None, S // VT_BLK, vw, VT_BLK),
                     lambda b, g, i: (b, 0, vt_row0 // vw + g, 0)),
    ]
    args = [q_arr, k_arr, vt_arr]
    if mode == "fox":
        q_aug, k_aug = extra
        in_specs += [
            pl.BlockSpec((None, bq, LANES), lambda b, g, i: (b, i, 0)),
            pl.BlockSpec((None, S, LANES), lambda b, g, i: (b, 0, 0)),
        ]
        args += [q_aug, k_aug]
    return pl.pallas_call(
        functools.partial(_flash_kernel, mode=mode, hp=hp, bq=bq),
        grid=(B, n_heads // hp, S // bq),
        in_specs=in_specs,
        out_specs=pl.BlockSpec((None, bq, vw), lambda b, g, i: (b, i, g)),
        out_shape=jax.ShapeDtypeStruct((B, S, n_heads * HEAD_DIM), BF16),
        scratch_shapes=([pltpu.VMEM((hp, 1, bq, bq), F32)] * 2
                        + [pltpu.VMEM((hp, 1, bq), F32)] * 2
                        + [pltpu.VMEM((hp, 2 * LANES if mode == "fox" else LANES, bq), BF16),
                           pltpu.VMEM((hp, 1, bq), F32)]),
        compiler_params=_cparams(("arbitrary", "arbitrary", "arbitrary")),
        name="flash_" + mode,
    )(*args)


CK_B = 2 * CHUNK
CK_NW = N_LEFT_CHUNKS * CHUNK // CK_B + 1
CK_EXT = (CK_NW + 1) * CK_B


def _chunk_kernel(q_ref, k_ref, vt_ref, ext_ref, o_ref, tab_ref, s_ref, *, hp, nq):
    i = pl.program_id(1)

    @pl.when(i == 0)
    def _():
        jj = lax.broadcasted_iota(jnp.int32, (CK_B, CK_B), 0)
        rr = lax.broadcasted_iota(jnp.int32, (CK_B, CK_B), 1)
        for h in range(hp):
            for w in range(CK_NW):
                a = (CK_NW - 1 - w) * CK_B
                g = jnp.broadcast_to(ext_ref[h:h + 1, a:a + 2 * CK_B], (CK_B, 2 * CK_B))
                t = pltpu.roll(g, CK_B, 1, stride=1, stride_axis=0)[:, :CK_B]
                if w == 0:
                    t = jnp.where((rr >= CHUNK) & (jj < CHUNK), NEG, t)
                if w == CK_NW - 1:
                    t = jnp.where((rr < CHUNK) & (jj >= CHUNK), NEG, t)
                tab_ref[h, w * CK_B:(w + 1) * CK_B, :] = t

    kcols = [slice(LANES * (h // 2), LANES * (h // 2 + 1)) for h in range(hp)]
    firsts = [i * nq + u - (CK_NW - 1) for u in range(nq)]

    def finish():
        for u in range(nq):
            kbc = [jnp.maximum(firsts[u] + w, 0) for w in range(CK_NW)]
            outs = []
            for h in range(hp):
                vt = jnp.concatenate([vt_ref[kbc[w], HEAD_DIM * h:HEAD_DIM * (h + 1), :]
                                      for w in range(CK_NW)], axis=1)
                outs.append(_softmax_out(_softmax_step([lambda u=u, h=h: s_ref[u, h]], [vt],
                                                       _softmax_init(CK_B))))
            oT = jnp.concatenate(outs, axis=0)
            o_ref[u * CK_B:(u + 1) * CK_B, :] = oT.T.astype(o_ref.dtype)

    def pair_scores(u, p, ks, nrows):
        q2 = q_ref[u * CK_B:(u + 1) * CK_B, kcols[2 * p]]
        qq = jnp.concatenate([_pair_mask_q(q2, 0), _pair_mask_q(q2, 1)], axis=0)
        return lax.dot_general(k_ref[pl.ds(ks, nrows), kcols[2 * p]], qq, _NT,
                               preferred_element_type=F32)

    @pl.when(firsts[0] >= 0)
    def _():
        for u in range(nq):
            ks = pl.multiple_of(firsts[u] * CK_B, CK_B)
            for p in range(hp // 2):
                sT = pair_scores(u, p, ks, CK_NW * CK_B)
                for j in range(2):
                    s_ref[u, 2 * p + j] = sT[:, j * CK_B:(j + 1) * CK_B] + tab_ref[2 * p + j]
        finish()

    @pl.when(firsts[0] < 0)
    def _():
        for u in range(nq):
            for p in range(hp // 2):
                for w in range(CK_NW):
                    rows = slice(w * CK_B, (w + 1) * CK_B)
                    ks = pl.multiple_of(jnp.maximum(firsts[u] + w, 0) * CK_B, CK_B)
                    sT = pair_scores(u, p, ks, CK_B)
                    for j in range(2):
                        s_ref[u, 2 * p + j, rows, :] = jnp.where(
                            firsts[u] + w >= 0,
                            sT[:, j * CK_B:(j + 1) * CK_B] + tab_ref[2 * p + j, rows, :], NEG)
        finish()


def _chunk_attn(main, q_col0, k_col0, vt_arr, vt_row0, ext, n_heads, nq=4):
    B, S, _ = main.shape
    hp = n_heads
    qw, vw = HEAD_DIM * hp, HEAD_DIM * hp
    return pl.pallas_call(
        functools.partial(_chunk_kernel, hp=hp, nq=nq),
        grid=(B, S // (nq * CK_B)),
        in_specs=[
            pl.BlockSpec((None, nq * CK_B, qw), lambda b, i: (b, i, q_col0 // qw)),
            pl.BlockSpec((None, S, qw), lambda b, i: (b, 0, k_col0 // qw)),
            pl.BlockSpec((None, S // VT_BLK, vw, VT_BLK), lambda b, i: (b, 0, vt_row0 // vw, 0)),
            pl.BlockSpec((hp, CK_EXT), lambda b, i: (0, 0)),
        ],
        out_specs=pl.BlockSpec((None, nq * CK_B, vw), lambda b, i: (b, i, 0)),
        out_shape=jax.ShapeDtypeStruct((B, S, n_heads * HEAD_DIM), BF16),
        scratch_shapes=[pltpu.VMEM((hp, CK_NW * CK_B, CK_B), F32),
                        pltpu.VMEM((nq, hp, CK_NW * CK_B, CK_B), F32)],
        compiler_params=_cparams(("arbitrary", "arbitrary")),
        name="chunk_attn",
    )(main, main, vt_arr, ext)


def _sb_kernel(q_ref, k_ref, vt_ref, o_ref, z_ref, lb_ref, sfx_ref, *, hp, bq):
    qs = pl.program_id(1) * bq
    nsub = bq // SUB
    row = lax.broadcasted_iota(jnp.int32, (SUB, bq), 0)
    col = lax.broadcasted_iota(jnp.int32, (SUB, bq), 1)
    ur = lax.broadcasted_iota(jnp.int32, (SUB, 2 * SUB), 0)
    uc = lax.broadcasted_iota(jnp.int32, (SUB, 2 * SUB), 1) & (SUB - 1)
    upper2 = jnp.where(uc > ur, 1.0, 0.0).astype(BF16)
    kcols = [slice(LANES * (h // 2), LANES * (h // 2 + 1)) for h in range(hp)]
    qms = [_pair_mask_q(q_ref[:, kcols[h]], h % 2) for h in range(hp)]

    def step(kb, carries, masked):
        ks = pl.multiple_of(kb * bq, bq)
        for h in range(hp):
            z_ref[h] = lax.dot_general(k_ref[pl.ds(ks, bq), kcols[h]], qms[h], _NT,
                                       preferred_element_type=F32)
        first_col = [c * SUB if masked else 0 for c in range(nsub)]

        def widen(x, c):
            if first_col[c] == 0:
                return x
            return jnp.concatenate([jnp.zeros((x.shape[0], first_col[c]), x.dtype), x], axis=1)

        totals = []
        for h in range(hp):
            tot = []
            for c in range(nsub):
                rows, cols = slice(c * SUB, (c + 1) * SUB), slice(first_col[c], bq)
                z = z_ref[h, rows, cols]
                l1 = jnp.log(1.0 + jnp.exp(-jnp.abs(z)))
                log_beta = jnp.minimum(z, 0.0) - l1
                log_keep = log_beta - z
                if masked:
                    valid = (ks + c * SUB + row < qs + col)[:, cols]
                    log_keep = jnp.where(valid, log_keep, 0.0)
                lb_ref[h, rows, cols] = log_beta
                hi = log_keep.astype(BF16)
                lo = (log_keep - hi.astype(F32)).astype(BF16)
                sfx = jnp.dot(upper2, jnp.concatenate([hi, lo], axis=0),
                              preferred_element_type=F32)
                sfx_ref[h, rows, cols] = sfx
                tot.append(widen(sfx[0:1, :] + log_keep[0:1, :], c))
            totals.append(tot)
        out = []
        for h in range(hp):
            tail, acc = carries[h]
            parts = [None] * nsub
            for c in range(nsub - 1, -1, -1):
                rows, cols = slice(c * SUB, (c + 1) * SUB), slice(first_col[c], bq)
                a = jnp.exp(lb_ref[h, rows, cols] + sfx_ref[h, rows, cols] + tail[:, cols])
                if masked:
                    a = jnp.where((ks + c * SUB + row < qs + col)[:, cols], a, 0.0)
                parts[c] = a.astype(BF16)
                tail = tail + totals[h][c]
            vts = [vt_ref[kb * nsub + c, HEAD_DIM * h:HEAD_DIM * (h + 1), :] for c in range(nsub)]
            if masked:
                for c in range(nsub):
                    acc = acc + widen(jnp.dot(vts[c], parts[c], preferred_element_type=F32), c)
            else:
                acc = acc + jnp.dot(jnp.concatenate(vts, axis=1), jnp.concatenate(parts, axis=0),
                                    preferred_element_type=F32)
            out.append((tail, acc))
        return tuple(out)

    n_full = qs // bq
    carries = tuple((jnp.zeros((1, bq), F32), jnp.zeros((HEAD_DIM, bq), F32))
                    for _ in range(hp))
    carries = step(n_full, carries, True)

    def cond(state):
        kb, carries = state
        tail_max = carries[0][0]
        for h in range(1, hp):
            tail_max = jnp.maximum(tail_max, carries[h][0])
        return (kb >= 0) & (jnp.max(tail_max) > SB_ZERO_LOG)

    def body(state):
        kb, carries = state
        return kb - 1, step(kb, carries, False)

    _, carries = lax.while_loop(cond, body, (n_full - 1, carries))
    _store_heads(o_ref, [acc for (_, acc) in carries])


def _sb_attn(main, q_col0, k_col0, vt_arr, vt_row0, n_heads, bq=256):
    B, S, _ = main.shape
    hp = n_heads
    qw = HEAD_DIM * hp
    return pl.pallas_call(
        functools.partial(_sb_kernel, hp=hp, bq=bq),
        grid=(B, S // bq),
        in_specs=[
            pl.BlockSpec((None, bq, qw), lambda b, i: (b, i, q_col0 // qw)),
            pl.BlockSpec((None, S, qw), lambda b, i: (b, 0, k_col0 // qw)),
            pl.BlockSpec((None, S // VT_BLK, qw, VT_BLK), lambda b, i: (b, 0, vt_row0 // qw, 0)),
        ],
        out_specs=pl.BlockSpec((None, bq, qw), lambda b, i: (b, i, 0)),
        out_shape=jax.ShapeDtypeStruct((B, S, n_heads * HEAD_DIM), BF16),
        scratch_shapes=[pltpu.VMEM((hp, bq, bq), F32)] * 3,
        compiler_params=_cparams(("arbitrary", "arbitrary")),
        name="sb_attn",
    )(main, main, vt_arr)


def _mla_prep_kernel(aux_ref, pos_ref, invf_ref, qn_ref, kvn_ref, wuq_ref, wuqr_ref,
                     wk_ref, wvt_ref, oq_ref, ok_ref, ovt_ref, *, q_scale):
    ang = pos_ref[...] * invf_ref[...]
    cos4, sin4 = jnp.cos(ang), jnp.sin(ang)
    lane = lax.broadcasted_iota(jnp.int32, ang.shape, 1)
    rotary = (lane >= NOPE_DIM) & (lane < NOPE_DIM + ROPE_DIM)
    cos_rows, sin_rows = [], []
    for m in range(LANES // ROPE_DIM):
        shift = (NOPE_DIM - ROPE_DIM * m) % LANES
        cm = cos4 if shift == 0 else pltpu.roll(cos4, shift, 1)
        sm = sin4 if shift == 0 else pltpu.roll(sin4, shift, 1)
        cos_rows.append(jnp.where(rotary, cm, 1.0))
        sin_rows.append(jnp.where(rotary, sm, 0.0))
    cos = jnp.concatenate(cos_rows, axis=0)
    sin = jnp.concatenate(sin_rows, axis=0)
    cq = _rms_bf16(aux_ref[:, 0:Q_LORA], qn_ref[...])
    ckv = _rms_bf16(aux_ref[:, Q_LORA:Q_LORA + KV_LORA], kvn_ref[...])
    o = Q_LORA + KV_LORA
    k_rope = aux_ref[:, o:o + LANES] * cos + aux_ref[:, o + LANES:o + 2 * LANES] * sin
    cos_q, sin_q = cos * q_scale, sin * q_scale
    n_heads = oq_ref.shape[-1] // LANES
    for h in range(0, n_heads, 2):
        cols = slice(h * LANES, (h + 2) * LANES)
        qa = jnp.dot(cq, wuq_ref[:, cols], preferred_element_type=F32)
        qb = jnp.dot(cq, wuqr_ref[:, cols], preferred_element_type=F32)
        kn = jnp.dot(ckv, wk_ref[:, cols], preferred_element_type=F32)
        for d in range(2):
            c1 = slice(d * LANES, (d + 1) * LANES)
            c2 = slice((h + d) * LANES, (h + d + 1) * LANES)
            oq_ref[:, c2] = (qa[:, c1] * cos_q + qb[:, c1] * sin_q).astype(BF16)
            ok_ref[:, c2] = (kn[:, c1] + k_rope).astype(BF16)
    nv = wvt_ref.shape[0]
    for r in range(0, nv, 256):
        vt = lax.dot_general(wvt_ref[r:r + 256, :], ckv, _NT,
                             preferred_element_type=F32).astype(BF16)
        _store_vt(ovt_ref, vt, r)


MLA_TM = 512


def _mla_prep(aux, pos, invf, qn, kvn, wuq, wuqr, wk, wvt, q_scale, tm=MLA_TM):
    B, S, na = aux.shape
    nq, nv = wuq.shape[1], wvt.shape[0]
    full = lambda a: pl.BlockSpec(a.shape, lambda b, i: (0,) * a.ndim)
    return pl.pallas_call(
        functools.partial(_mla_prep_kernel, q_scale=q_scale),
        grid=(B, S // tm),
        in_specs=[
            pl.BlockSpec((None, tm, na), lambda b, i: (b, i, 0)),
            pl.BlockSpec((None, None) + pos.shape[2:], lambda b, i: (b, i, 0, 0)),
            full(invf), full(qn), full(kvn), full(wuq), full(wuqr), full(wk), full(wvt),
        ],
        out_specs=[
            pl.BlockSpec((None, tm, nq), lambda b, i: (b, i, 0)),
            pl.BlockSpec((None, tm, nq), lambda b, i: (b, i, 0)),
            pl.BlockSpec((None, tm // VT_BLK, nv, VT_BLK), lambda b, i: (b, i, 0, 0)),
        ],
        out_shape=[
            jax.ShapeDtypeStruct((B, S, nq), BF16),
            jax.ShapeDtypeStruct((B, S, nq), BF16),
            jax.ShapeDtypeStruct((B, S // VT_BLK, nv, VT_BLK), BF16),
        ],
        compiler_params=_cparams(("arbitrary", "arbitrary")),
        name="mla_prep",
    )(aux, pos, invf, qn, kvn, wuq, wuqr, wk, wvt)


def _mlp_kernel(x_ref, oa_ref, ob_ref, wo_ref, g_ref, wu_ref, wd_ref, gf_ref, out_ref,
                x1_ref, h_ref, a_ref, *, final_norm, tf):
    na = oa_ref.shape[-1]
    x1 = (x_ref[...]
          + jnp.dot(oa_ref[...], wo_ref[0:na, :], preferred_element_type=F32)
          + jnp.dot(ob_ref[...], wo_ref[na:, :], preferred_element_type=F32))
    x1_ref[...] = x1
    h_ref[...] = _rms_bf16(x1, g_ref[...])
    for f in range(0, wu_ref.shape[1], tf):
        u = jnp.dot(h_ref[...], wu_ref[:, f:f + tf], preferred_element_type=F32)
        a_ref[:, f:f + tf] = jnp.square(jnp.maximum(u, 0.0)).astype(BF16)
    y = x1_ref[...] + jnp.dot(a_ref[...], wd_ref[...], preferred_element_type=F32)
    if final_norm:
        ms = jnp.mean(y * y, axis=-1, keepdims=True)
        y = y * lax.rsqrt(ms + EPS) * gf_ref[...]
    out_ref[...] = y


def _mlp(x2, oa, ob, wo, g, wu, wd, gf, final_norm, tm=512, tf=1024):
    T, D = x2.shape
    F = wu.shape[1]
    na, nb = oa.shape[1], ob.shape[1]
    const = lambda shape: pl.BlockSpec(shape, lambda i: (0, 0), pipeline_mode=pl.Buffered(1))
    return pl.pallas_call(
        functools.partial(_mlp_kernel, final_norm=final_norm, tf=tf),
        grid=(T // tm,),
        in_specs=[
            pl.BlockSpec((tm, D), lambda i: (i, 0)),
            pl.BlockSpec((tm, na), lambda i: (i, 0)),
            pl.BlockSpec((tm, nb), lambda i: (i, 0)),
            const((na + nb, D)),
            const((1, D)),
            const((D, F)),
            const((F, D)),
            const((1, D)),
        ],
        out_specs=pl.BlockSpec((tm, D), lambda i: (i, 0)),
        out_shape=jax.ShapeDtypeStruct((T, D), F32),
        scratch_shapes=[pltpu.VMEM((tm, D), F32), pltpu.VMEM((tm, D), BF16),
                        pltpu.VMEM((tm, F), BF16)],
        compiler_params=_cparams(("arbitrary",)),
        name="mlp",
    )(x2, oa, ob, wo, g.reshape(1, D), wu, wd, gf.reshape(1, D))


def _pad_cols(w, n):
    return jnp.pad(w, ((0, 0), (0, n - w.shape[1])))


def _rot_cols(w):
    half = ROPE_DIM // 2
    return jnp.concatenate([-w[:, half:], w[:, :half]], axis=1)


def _rope_slab(w):
    z = jnp.zeros((w.shape[0], NOPE_DIM), w.dtype)
    return jnp.concatenate([z, w, jnp.zeros((w.shape[0], LANES - NOPE_DIM - ROPE_DIM), w.dtype)], axis=1)


def _even_layer(x, g_mix, w_in, b_forget, rel_bias, w_out, g_mlp, w_up, w_down, g_final,
                final_norm):
    B, S, D = x.shape
    hf, hc = b_forget.shape[0], rel_bias.shape[0]
    wf, wc = hf * HEAD_DIM, hc * HEAD_DIM
    o = np.cumsum([0, wf, wf, wf, hf, wc, wc, wc])
    qa, ka, va, fa, qb, kb, vb = [w_in[:, o[n]:o[n + 1]] for n in range(7)]
    q_scale = HEAD_DIM ** -0.5 * LOG2E
    wm = jnp.concatenate([qa * q_scale, ka, qb * q_scale, kb], axis=1).astype(BF16)
    wvt = jnp.concatenate([va, vb], axis=1).T.astype(BF16)
    wa = _pad_cols(fa, LANES).astype(BF16)
    main, vt, aux = _inproj(x, g_mix, wm, wvt, wa)

    q_aug, k_aug = _logcum(aux, _pad_cols(b_forget.reshape(1, hf), LANES), hf)
    o_a = _flash("fox", main, 0, main, wf, vt, 0, (q_aug, k_aug), hf)

    right = CK_EXT - rel_bias.shape[1] - (CHUNK + 1)
    ext = jnp.pad(rel_bias * LOG2E, ((0, 0), (CHUNK + 1, right)), mode="edge")
    o_b = _chunk_attn(main, 2 * wf, 2 * wf + wc, vt, wf, ext, hc)

    y = _mlp(x.reshape(B * S, D), o_a.reshape(B * S, wf), o_b.reshape(B * S, wc),
             w_out.astype(BF16), g_mlp, w_up.astype(BF16), w_down.astype(BF16), g_final,
             final_norm)
    return y.reshape(B, S, D)


def _odd_layer(x, positions, g_mix, w_in, q_norm, kv_norm, w_uq, w_ukv, w_out, g_mlp, w_up,
               w_down, g_final, final_norm):
    B, S, D = x.shape
    hm = w_ukv.shape[1] // (NOPE_DIM + HEAD_DIM)
    ws = w_in.shape[1] - Q_LORA - KV_LORA - ROPE_DIM
    hs = (ws // 3) // HEAD_DIM
    wsb = hs * HEAD_DIM
    o = np.cumsum([0, wsb, wsb, wsb, Q_LORA, KV_LORA, ROPE_DIM])
    qc, kc, vc, w_cq, w_ckv, w_kr = [w_in[:, o[n]:o[n + 1]] for n in range(6)]
    wm = jnp.concatenate([qc * HEAD_DIM ** -0.5, kc], axis=1).astype(BF16)
    wa = jnp.concatenate([w_cq, w_ckv, _rope_slab(w_kr), _rope_slab(_rot_cols(w_kr))],
                         axis=1).astype(BF16)
    main, vt, aux = _inproj(x, g_mix, wm, vc.T.astype(BF16), wa)
    o_c = _sb_attn(main, 0, wsb, vt, 0, hs)

    dq = NOPE_DIM + ROPE_DIM
    wuq3 = w_uq.reshape(Q_LORA, hm, dq)
    nope, ropew = wuq3[:, :, :NOPE_DIM], wuq3[:, :, NOPE_DIM:]
    zq = jnp.zeros((Q_LORA, hm, LANES - dq), w_uq.dtype)
    wuq = jnp.concatenate([nope, ropew, zq], axis=2).reshape(Q_LORA, hm * LANES).astype(BF16)
    half = ROPE_DIM // 2
    ropr = jnp.concatenate([-ropew[:, :, half:], ropew[:, :, :half]], axis=2)
    wuqr = jnp.concatenate([jnp.zeros_like(nope), ropr, zq], axis=2)
    wuqr = wuqr.reshape(Q_LORA, hm * LANES).astype(BF16)
    wkv3 = w_ukv.reshape(KV_LORA, hm, NOPE_DIM + HEAD_DIM)
    wk = jnp.concatenate([wkv3[:, :, :NOPE_DIM],
                          jnp.zeros((KV_LORA, hm, LANES - NOPE_DIM), w_ukv.dtype)], axis=2)
    wk = wk.reshape(KV_LORA, hm * LANES).astype(BF16)
    wv_t = wkv3[:, :, NOPE_DIM:].reshape(KV_LORA, hm * HEAD_DIM).T.astype(BF16)
    freqs = (ROPE_THETA ** (-jnp.arange(half, dtype=F32) / half))
    invf = jnp.tile(freqs, 2 * LANES // ROPE_DIM).reshape(1, LANES)
    groups = LANES // ROPE_DIM
    pos = positions.astype(F32).reshape(B, S // MLA_TM, groups, MLA_TM // groups)
    pos = jnp.repeat(jnp.swapaxes(pos, 2, 3), ROPE_DIM, axis=-1)
    qm, km, vtm = _mla_prep(aux, pos, invf, q_norm.reshape(1, Q_LORA),
                            kv_norm.reshape(1, KV_LORA), wuq, wuqr, wk, wv_t,
                            dq ** -0.5 * LOG2E)
    o_d = _flash("mla", qm, 0, km, 0, vtm, 0, None, hm)

    y = _mlp(x.reshape(B * S, D), o_c.reshape(B * S, wsb), o_d.reshape(B * S, hm * HEAD_DIM),
             w_out.astype(BF16), g_mlp, w_up.astype(BF16), w_down.astype(BF16), g_final,
             final_norm)
    return y.reshape(B, S, D)


def kernel(x, positions, norm_mix, norm_mlp, norm_final, w_in_ab, b_forget, rel_bias, w_out_ab,
           w_in_cd, q_norm, kv_norm, w_uq, w_ukv, w_out_cd, w_up, w_down):
    depth = norm_mix.shape[0]
    for layer in range(depth):
        last = layer == depth - 1
        if layer % 2 == 0:
            e = layer // 2
            x = _even_layer(x, norm_mix[layer], w_in_ab[e], b_forget[e], rel_bias[e], w_out_ab[e],
                            norm_mlp[layer], w_up[layer], w_down[layer], norm_final, last)
        else:
            o = layer // 2
            x = _odd_layer(x, positions, norm_mix[layer], w_in_cd[o], q_norm[o], kv_norm[o],
                           w_uq[o], w_ukv[o], w_out_cd[o], norm_mlp[layer], w_up[layer],
                           w_down[layer], norm_final, last)
    return x
```

```python
import functools
import math

import numpy as np
import jax
import jax.numpy as jnp
from jax import lax
from jax.experimental import pallas as pl
from jax.experimental.pallas import tpu as pltpu

F32 = jnp.float32
BF16 = jnp.bfloat16

EPS = 1e-6
HEAD_DIM = 64
CHUNK = 64
N_LEFT_CHUNKS = 8
REL_CLIP = 256
ROPE_DIM = 32
NOPE_DIM = 64
ROPE_THETA = 10000.0
Q_LORA = 384
KV_LORA = 256

LANES = 128
VT_BLK = LANES
SUB = LANES
FLASH_HP = 8
FLASH_BQ = 256
FLASH_SUB = 256
FIXED_WIDTHS = (8, 4, 2, 1)
SAFE_GAP = 80.0
NEG = -1e30
LOG2E = math.log2(math.e)
SB_ZERO_LOG = -104.0
VMEM_LIMIT = 56 * 1024 * 1024

_NT = (((1,), (1,)), ((), ()))


def _cparams(sem):
    return pltpu.CompilerParams(dimension_semantics=sem, vmem_limit_bytes=VMEM_LIMIT)


def _rms_bf16(x, g):
    ms = jnp.mean(x * x, axis=-1, keepdims=True)
    return (x * lax.rsqrt(ms + EPS) * g).astype(BF16)


def _store_vt(ovt_ref, vt, row0):
    rows, tm = vt.shape
    for c in range(tm // VT_BLK):
        ovt_ref[c, row0:row0 + rows, :] = vt[:, c * VT_BLK:(c + 1) * VT_BLK]


def _inproj_kernel(x_ref, g_ref, wm_ref, wvt_ref, wa_ref, om_ref, ovt_ref, oa_ref):
    h = _rms_bf16(x_ref[...], g_ref[...])
    nm = om_ref.shape[-1]
    for c in range(0, nm, 512):
        om_ref[:, c:c + 512] = jnp.dot(
            h, wm_ref[:, c:c + 512], preferred_element_type=F32).astype(BF16)
    nv = wvt_ref.shape[0]
    for r in range(0, nv, 256):
        vt = lax.dot_general(wvt_ref[r:r + 256, :], h, _NT,
                             preferred_element_type=F32).astype(BF16)
        _store_vt(ovt_ref, vt, r)
    oa_ref[...] = jnp.dot(h, wa_ref[...], preferred_element_type=F32)


def _inproj(x, g, wm, wvt, wa, tm=512):
    B, S, D = x.shape
    nm, nv, na = wm.shape[1], wvt.shape[0], wa.shape[1]
    return pl.pallas_call(
        _inproj_kernel,
        grid=(B, S // tm),
        in_specs=[
            pl.BlockSpec((None, tm, D), lambda b, i: (b, i, 0)),
            pl.BlockSpec((1, D), lambda b, i: (0, 0)),
            pl.BlockSpec((D, nm), lambda b, i: (0, 0)),
            pl.BlockSpec((nv, D), lambda b, i: (0, 0)),
            pl.BlockSpec((D, na), lambda b, i: (0, 0)),
        ],
        out_specs=[
            pl.BlockSpec((None, tm, nm), lambda b, i: (b, i, 0)),
            pl.BlockSpec((None, tm // VT_BLK, nv, VT_BLK), lambda b, i: (b, i, 0, 0)),
            pl.BlockSpec((None, tm, na), lambda b, i: (b, i, 0)),
        ],
        out_shape=[
            jax.ShapeDtypeStruct((B, S, nm), BF16),
            jax.ShapeDtypeStruct((B, S // VT_BLK, nv, VT_BLK), BF16),
            jax.ShapeDtypeStruct((B, S, na), F32),
        ],
        compiler_params=_cparams(("arbitrary", "arbitrary")),
        name="inproj",
    )(x, g.reshape(1, D), wm, wvt, wa)


def _split3(x):
    hi = x.astype(BF16)
    r = x - hi.astype(F32)
    mid = r.astype(BF16)
    lo = (r - mid.astype(F32)).astype(BF16)
    return hi, mid, lo


AUG_W = 8


def _logcum_kernel(fa_ref, b_ref, pq_ref, pk_ref, oneq_ref, onek_ref, oq_ref, ok_ref, carry_ref):
    @pl.when(pl.program_id(1) == 0)
    def _():
        carry_ref[...] = jnp.zeros_like(carry_ref)

    z = fa_ref[...] + b_ref[...]
    lf = jnp.minimum(z, 0.0) - jnp.log(1.0 + jnp.exp(-jnp.abs(z)))
    tc = lf.shape[0]
    r = lax.broadcasted_iota(jnp.int32, (tc, tc), 0)
    c = lax.broadcasted_iota(jnp.int32, (tc, tc), 1)
    tri = jnp.where(r >= c, 1.0, 0.0).astype(BF16)
    cs = carry_ref[...]
    for part in _split3(lf):
        cs = cs + jnp.dot(tri, part, preferred_element_type=F32)
    carry_ref[...] = cs[tc - 1:tc, :]
    qa, ka = oneq_ref[...], onek_ref[...]
    for n, part in enumerate(_split3(cs * LOG2E)):
        qa = qa + jnp.dot(part, pq_ref[n], preferred_element_type=F32)
        ka = ka + jnp.dot(part, pk_ref[n], preferred_element_type=F32)
    oq_ref[...] = qa.astype(BF16)
    ok_ref[...] = ka.astype(BF16)


def _logcum(fa, bias, n_heads, tc=512):
    B, S, W = fa.shape
    na = LANES
    pq = np.zeros((3, W, na), np.float32)
    pk = np.zeros((3, W, na), np.float32)
    oneq = np.zeros((1, na), np.float32)
    onek = np.zeros((1, na), np.float32)
    for h in range(n_heads):
        base = h * AUG_W
        for n in range(3):
            pq[n, h, base + n] = 1.0
            pk[n, h, base + 3 + n] = -1.0
        oneq[0, base + 3:base + 6] = 1.0
        onek[0, base:base + 3] = 1.0
    const = lambda a: pl.BlockSpec(a.shape, lambda b, i: (0,) * a.ndim)
    args = [jnp.asarray(pq, BF16), jnp.asarray(pk, BF16), jnp.asarray(oneq), jnp.asarray(onek)]
    return pl.pallas_call(
        _logcum_kernel,
        grid=(B, S // tc),
        in_specs=[pl.BlockSpec((None, tc, W), lambda b, i: (b, i, 0)),
                  pl.BlockSpec((1, W), lambda b, i: (0, 0))] + [const(a) for a in args],
        out_specs=[pl.BlockSpec((None, tc, na), lambda b, i: (b, i, 0))] * 2,
        out_shape=[jax.ShapeDtypeStruct((B, S, na), BF16)] * 2,
        scratch_shapes=[pltpu.VMEM((1, W), F32)],
        compiler_params=_cparams(("arbitrary", "arbitrary")),
        name="logcum",
    )(fa, bias, *args)


def _pair_mask_q(q2, j):
    lane = lax.broadcasted_iota(jnp.int32, q2.shape, 1)
    keep = (lane >= HEAD_DIM * j) & (lane < HEAD_DIM * (j + 1))
    return jnp.where(keep, q2, jnp.zeros_like(q2))


ONES_ROWS = 16


def _softmax_step(tiles, vts, carry, tile_max=None):
    m, acc = carry
    if tile_max is not None:
        m_new = jnp.maximum(m, tile_max)
    else:
        m_new = m
        for tile in tiles:
            m_new = jnp.maximum(m_new, jnp.max(tile(), axis=0, keepdims=True))
    alpha = jnp.exp2(m - m_new)
    pv = None
    for tile, vt in zip(tiles, vts):
        p = jnp.exp2(tile() - m_new).astype(BF16)
        vt1 = jnp.concatenate([vt, jnp.ones((ONES_ROWS, vt.shape[1]), BF16)], axis=0)
        d = jnp.dot(vt1, p, preferred_element_type=F32)
        pv = d if pv is None else pv + d
    return m_new, alpha * acc + pv


def _softmax_init(bq):
    return (jnp.full((1, bq), NEG, F32), jnp.zeros((HEAD_DIM + ONES_ROWS, bq), F32))


def _softmax_out(carry):
    _, acc = carry
    return acc[0:HEAD_DIM] / acc[HEAD_DIM:HEAD_DIM + 1]


def _store_heads(o_ref, outs):
    oT = jnp.concatenate(outs, axis=0)
    o_ref[...] = oT.T.astype(o_ref.dtype)


def _flash_kernel(*refs, mode, hp, bq):
    if mode == "fox":
        q_ref, k_ref, vt_ref, qaug_ref, kaug_ref, o_ref = refs[:6]
    else:
        q_ref, k_ref, vt_ref, o_ref = refs[:4]
    sa_ref, sb_ref, ma_ref, mb_ref, qt_ref, kn_ref = refs[-6:]
    qs = pl.program_id(2) * bq
    sub = FLASH_SUB
    row = lax.broadcasted_iota(jnp.int32, (sub, bq), 0)
    col = lax.broadcasted_iota(jnp.int32, (sub, bq), 1)
    if mode == "fox":
        kcols = [slice(LANES * (h // 2), LANES * (h // 2 + 1)) for h in range(hp)]
        lane = lax.broadcasted_iota(jnp.int32, (bq, LANES), 1)
        qa = qaug_ref[...]
        qms = []
        for h in range(hp):
            first = AUG_W * (pl.program_id(1) * hp + h)
            own = (lane >= first) & (lane < first + AUG_W)
            qms.append(jnp.concatenate(
                [_pair_mask_q(q_ref[:, kcols[h]], h % 2),
                 jnp.where(own, qa, jnp.zeros_like(qa))], axis=1))
    else:
        kcols = [slice(LANES * h, LANES * (h + 1)) for h in range(hp)]
        qms = [q_ref[:, kcols[h]] for h in range(hp)]
    for h in range(hp):
        qt_ref[h] = qms[h].T

    @pl.when(pl.program_id(2) == 0)
    def _():
        klane = lax.broadcasted_iota(jnp.int32, (1, LANES), 1)
        for h in range(hp):
            kabs = jnp.max(jnp.abs(k_ref[:, kcols[h]].astype(F32)), axis=0, keepdims=True)
            sq = kabs * kabs
            if mode == "fox":
                sq = jnp.where((klane >= HEAD_DIM * (h % 2)) & (klane < HEAD_DIM * (h % 2 + 1)),
                               sq, 0.0)
            kn_ref[h] = jnp.broadcast_to(jnp.sqrt(jnp.sum(sq, axis=1, keepdims=True)), (1, bq))

    def scores(sb, h, masked):
        ks = pl.multiple_of(sb * sub, sub)
        k = k_ref[pl.ds(ks, sub), kcols[h]]
        if mode == "fox":
            k = jnp.concatenate([k, kaug_ref[pl.ds(ks, sub), :]], axis=1)
        sT = jnp.dot(k, qt_ref[h], preferred_element_type=F32)
        if masked and mode == "fox":
            sT = jnp.where(ks + row <= qs + col, sT, NEG)
        elif masked:
            sT = jnp.where(((ks + row) >> 6) <= ((qs + col) >> 6), sT, NEG)
        return sT

    nsub = bq // sub
    nvt = sub // VT_BLK

    def produce(buf, sb0, masked, h):
        s_buf, m_buf = buf
        tile_max = None
        for c in range(nsub):
            sT = scores(sb0 + c, h, masked)
            s_buf[h, c] = sT
            cm = jnp.max(sT, axis=0, keepdims=True)
            tile_max = cm if tile_max is None else jnp.maximum(tile_max, cm)
        m_buf[h] = tile_max

    def consume(buf, sb0, carry, h):
        s_buf, m_buf = buf
        tiles = [lambda c=c: s_buf[h, c] for c in range(nsub)]
        vts = [jnp.concatenate([vt_ref[(sb0 + c) * nvt + v, HEAD_DIM * h:HEAD_DIM * (h + 1), :]
                                for v in range(nvt)], axis=1) for c in range(nsub)]
        return _softmax_step(tiles, vts, carry, tile_max=m_buf[h])

    def stage(cur, cur_sb, nxt, nxt_sb, carries):
        if nxt is not None:
            for h in range(hp):
                produce(nxt, nxt_sb, False, h)
        return tuple(consume(cur, cur_sb, carries[h], h) for h in range(hp))

    n = pl.program_id(2)
    diag_sb = qs // sub
    buf_a, buf_b = (sa_ref, ma_ref), (sb_ref, mb_ref)
    for h in range(hp):
        produce(buf_a, diag_sb, True, h)

    def pair(j, carries):
        carries = stage(buf_a, jnp.where(j == 0, diag_sb, (2 * j - 1) * nsub),
                        buf_b, 2 * j * nsub, carries)
        return stage(buf_b, 2 * j * nsub,
                     buf_a, jnp.minimum(2 * j + 1, n - 1) * nsub, carries)

    def online(_):
        carries = tuple(_softmax_init(bq) for _ in range(hp))
        carries = lax.fori_loop(0, (n + 1) // 2, pair, carries)
        carries = lax.cond(
            n % 2 == 0,
            lambda c: stage(buf_a, jnp.where(n == 0, diag_sb, (n - 1) * nsub), None, None, c),
            lambda c: c, carries)
        return jnp.concatenate([_softmax_out(c) for c in carries], axis=0)

    refs_ = []
    gap = None
    for h in range(hp):
        qf = qt_ref[h, 0:LANES, :].astype(F32)
        bound = jnp.sqrt(jnp.sum(qf * qf, axis=0, keepdims=True)) * kn_ref[h] * 1.01 + 1e-3
        refs_.append(bound)
        g = jnp.max(bound - ma_ref[h])
        gap = g if gap is None else jnp.maximum(gap, g)

    def weigh_add(acc, h, s, sb0, nblk):
        p = jnp.exp2(s - refs_[h]).astype(BF16)
        vt = jnp.concatenate([vt_ref[sb0 * nvt + v, HEAD_DIM * h:HEAD_DIM * (h + 1), :]
                              for v in range(nblk * nvt)], axis=1)
        vt1 = jnp.concatenate([vt, jnp.ones((ONES_ROWS, nblk * sub), BF16)], axis=0)
        return acc + jnp.dot(vt1, p, preferred_element_type=F32)

    def fixed_reference(_):
        def run(kb, accs, nblk):
            ks = pl.multiple_of(kb * sub, sub)
            ss = []
            for h in range(hp):
                k = k_ref[pl.ds(ks, nblk * sub), kcols[h]]
                if mode == "fox":
                    k = jnp.concatenate([k, kaug_ref[pl.ds(ks, nblk * sub), :]], axis=1)
                ss.append(jnp.dot(k, qt_ref[h], preferred_element_type=F32))
            return tuple(weigh_add(accs[h], h, ss[h], kb, nblk) for h in range(hp))

        accs = tuple(jnp.zeros((HEAD_DIM + ONES_ROWS, bq), F32) for _ in range(hp))
        done = 0
        for width in FIXED_WIDTHS:
            trips = (diag_sb - done) // width
            accs = lax.fori_loop(0, trips,
                                 lambda j, a, done=done, width=width: run(done + j * width, a, width),
                                 accs)
            done = done + trips * width
        for c in range(nsub):
            accs = [weigh_add(accs[h], h, sa_ref[h, c], diag_sb + c, 1) for h in range(hp)]
        return jnp.concatenate([a[0:HEAD_DIM] / a[HEAD_DIM:HEAD_DIM + 1] for a in accs], axis=0)

    oT = lax.cond(gap <= SAFE_GAP, fixed_reference, online, None)
    o_ref[...] = oT.T.astype(o_ref.dtype)


def _flash(mode, q_arr, q_col0, k_arr, k_col0, vt_arr, vt_row0, extra, n_heads,
           hp=FLASH_HP, bq=FLASH_BQ):
    B, S, _ = q_arr.shape
    qw = (HEAD_DIM if mode == "fox" else LANES) * hp
    vw = HEAD_DIM * hp
    in_specs = [
        pl.BlockSpec((None, bq, qw), lambda b, g, i: (b, i, q_col0 // qw + g)),
        pl.BlockSpec((None, S, qw), lambda b, g, i: (b, 0, k_col0 // qw + g)),
        pl.BlockSpec((None, S // VT_BLK, vw, VT_BLK),
                     lambda b, g, i: (b, 0, vt_row0 // vw + g, 0)),
    ]
    args = [q_arr, k_arr, vt_arr]
    if mode == "fox":
        q_aug, k_aug = extra
        in_specs += [
            pl.BlockSpec((None, bq, LANES), lambda b, g, i: (b, i, 0)),
            pl.BlockSpec((None, S, LANES), lambda b, g, i: (b, 0, 0)),
        ]
        args += [q_aug, k_aug]
    return pl.pallas_call(
        functools.partial(_flash_kernel, mode=mode, hp=hp, bq=bq),
        grid=(B, n_heads // hp, S // bq),
        in_specs=in_specs,
        out_specs=pl.BlockSpec((None, bq, vw), lambda b, g, i: (b, i, g)),
        out_shape=jax.ShapeDtypeStruct((B, S, n_heads * HEAD_DIM), BF16),
        scratch_shapes=([pltpu.VMEM((hp, bq // FLASH_SUB, FLASH_SUB, bq), F32)] * 2
                        + [pltpu.VMEM((hp, 1, bq), F32)] * 2
                        + [pltpu.VMEM((hp, 2 * LANES if mode == "fox" else LANES, bq), BF16),
                           pltpu.VMEM((hp, 1, bq), F32)]),
        compiler_params=_cparams(("arbitrary", "arbitrary", "arbitrary")),
        name="flash_" + mode,
    )(*args)


CK_B = 2 * CHUNK
CK_NW = N_LEFT_CHUNKS * CHUNK // CK_B + 1
CK_EXT = (CK_NW + 1) * CK_B


def _chunk_kernel(q_ref, k_ref, vt_ref, ext_ref, o_ref, tab_ref, s_ref, *, hp, nq):
    i = pl.program_id(1)

    @pl.when(i == 0)
    def _():
        jj = lax.broadcasted_iota(jnp.int32, (CK_B, CK_B), 0)
        rr = lax.broadcasted_iota(jnp.int32, (CK_B, CK_B), 1)
        for h in range(hp):
            for w in range(CK_NW):
                a = (CK_NW - 1 - w) * CK_B
                g = jnp.broadcast_to(ext_ref[h:h + 1, a:a + 2 * CK_B], (CK_B, 2 * CK_B))
                t = pltpu.roll(g, CK_B, 1, stride=1, stride_axis=0)[:, :CK_B]
                if w == 0:
                    t = jnp.where((rr >= CHUNK) & (jj < CHUNK), NEG, t)
                if w == CK_NW - 1:
                    t = jnp.where((rr < CHUNK) & (jj >= CHUNK), NEG, t)
                tab_ref[h, w * CK_B:(w + 1) * CK_B, :] = t

    kcols = [slice(LANES * (h // 2), LANES * (h // 2 + 1)) for h in range(hp)]
    firsts = [i * nq + u - (CK_NW - 1) for u in range(nq)]

    def finish():
        for u in range(nq):
            kbc = [jnp.maximum(firsts[u] + w, 0) for w in range(CK_NW)]
            outs = []
            for h in range(hp):
                vt = jnp.concatenate([vt_ref[kbc[w], HEAD_DIM * h:HEAD_DIM * (h + 1), :]
                                      for w in range(CK_NW)], axis=1)
                outs.append(_softmax_out(_softmax_step([lambda u=u, h=h: s_ref[u, h]], [vt],
                                                       _softmax_init(CK_B))))
            oT = jnp.concatenate(outs, axis=0)
            o_ref[u * CK_B:(u + 1) * CK_B, :] = oT.T.astype(o_ref.dtype)

    def pair_scores(u, p, ks, nrows):
        q2 = q_ref[u * CK_B:(u + 1) * CK_B, kcols[2 * p]]
        qq = jnp.concatenate([_pair_mask_q(q2, 0), _pair_mask_q(q2, 1)], axis=0)
        return lax.dot_general(k_ref[pl.ds(ks, nrows), kcols[2 * p]], qq, _NT,
                               preferred_element_type=F32)

    @pl.when(firsts[0] >= 0)
    def _():
        for u in range(nq):
            ks = pl.multiple_of(firsts[u] * CK_B, CK_B)
            for p in range(hp // 2):
                sT = pair_scores(u, p, ks, CK_NW * CK_B)
                for j in range(2):
                    s_ref[u, 2 * p + j] = sT[:, j * CK_B:(j + 1) * CK_B] + tab_ref[2 * p + j]
        finish()

    @pl.when(firsts[0] < 0)
    def _():
        for u in range(nq):
            for p in range(hp // 2):
                for w in range(CK_NW):
                    rows = slice(w * CK_B, (w + 1) * CK_B)
                    ks = pl.multiple_of(jnp.maximum(firsts[u] + w, 0) * CK_B, CK_B)
                    sT = pair_scores(u, p, ks, CK_B)
                    for j in range(2):
                        s_ref[u, 2 * p + j, rows, :] = jnp.where(
                            firsts[u] + w >= 0,
                            sT[:, j * CK_B:(j + 1) * CK_B] + tab_ref[2 * p + j, rows, :], NEG)
        finish()


def _chunk_attn(main, q_col0, k_col0, vt_arr, vt_row0, ext, n_heads, nq=4):
    B, S, _ = main.shape
    hp = n_heads
    qw, vw = HEAD_DIM * hp, HEAD_DIM * hp
    return pl.pallas_call(
        functools.partial(_chunk_kernel, hp=hp, nq=nq),
        grid=(B, S // (nq * CK_B)),
        in_specs=[
            pl.BlockSpec((None, nq * CK_B, qw), lambda b, i: (b, i, q_col0 // qw)),
            pl.BlockSpec((None, S, qw), lambda b, i: (b, 0, k_col0 // qw)),
            pl.BlockSpec((None, S // VT_BLK, vw, VT_BLK), lambda b, i: (b, 0, vt_row0 // vw, 0)),
            pl.BlockSpec((hp, CK_EXT), lambda b, i: (0, 0)),
        ],
        out_specs=pl.BlockSpec((None, nq * CK_B, vw), lambda b, i: (b, i, 0)),
        out_shape=jax.ShapeDtypeStruct((B, S, n_heads * HEAD_DIM), BF16),
        scratch_shapes=[pltpu.VMEM((hp, CK_NW * CK_B, CK_B), F32),
                        pltpu.VMEM((nq, hp, CK_NW * CK_B, CK_B), F32)],
        compiler_params=_cparams(("arbitrary", "arbitrary")),
        name="chunk_attn",
    )(main, main, vt_arr, ext)


def _sb_kernel(q_ref, k_ref, vt_ref, o_ref, z_ref, lb_ref, sfx_ref, *, hp, bq):
    qs = pl.program_id(1) * bq
    nsub = bq // SUB
    row = lax.broadcasted_iota(jnp.int32, (SUB, bq), 0)
    col = lax.broadcasted_iota(jnp.int32, (SUB, bq), 1)
    ur = lax.broadcasted_iota(jnp.int32, (SUB, 2 * SUB), 0)
    uc = lax.broadcasted_iota(jnp.int32, (SUB, 2 * SUB), 1) & (SUB - 1)
    upper2 = jnp.where(uc > ur, 1.0, 0.0).astype(BF16)
    kcols = [slice(LANES * (h // 2), LANES * (h // 2 + 1)) for h in range(hp)]
    qms = [_pair_mask_q(q_ref[:, kcols[h]], h % 2) for h in range(hp)]

    def step(kb, carries, masked):
        ks = pl.multiple_of(kb * bq, bq)
        for h in range(hp):
            z_ref[h] = lax.dot_general(k_ref[pl.ds(ks, bq), kcols[h]], qms[h], _NT,
                                       preferred_element_type=F32)
        first_col = [c * SUB if masked else 0 for c in range(nsub)]

        def widen(x, c):
            if first_col[c] == 0:
                return x
            return jnp.concatenate([jnp.zeros((x.shape[0], first_col[c]), x.dtype), x], axis=1)

        totals = []
        for h in range(hp):
            tot = []
            for c in range(nsub):
                rows, cols = slice(c * SUB, (c + 1) * SUB), slice(first_col[c], bq)
                z = z_ref[h, rows, cols]
                l1 = jnp.log(1.0 + jnp.exp(-jnp.abs(z)))
                log_beta = jnp.minimum(z, 0.0) - l1
                log_keep = log_beta - z
                if masked:
                    valid = (ks + c * SUB + row < qs + col)[:, cols]
                    log_keep = jnp.where(valid, log_keep, 0.0)
                lb_ref[h, rows, cols] = log_beta
                hi = log_keep.astype(BF16)
                lo = (log_keep - hi.astype(F32)).astype(BF16)
                sfx = jnp.dot(upper2, jnp.concatenate([hi, lo], axis=0),
                              preferred_element_type=F32)
                sfx_ref[h, rows, cols] = sfx
                tot.append(widen(sfx[0:1, :] + log_keep[0:1, :], c))
            totals.append(tot)
        out = []
        for h in range(hp):
            tail, acc = carries[h]
            parts = [None] * nsub
            for c in range(nsub - 1, -1, -1):
                rows, cols = slice(c * SUB, (c + 1) * SUB), slice(first_col[c], bq)
                a = jnp.exp(lb_ref[h, rows, cols] + sfx_ref[h, rows, cols] + tail[:, cols])
                if masked:
                    a = jnp.where((ks + c * SUB + row < qs + col)[:, cols], a, 0.0)
                parts[c] = a.astype(BF16)
                tail = tail + totals[h][c]
            vts = [vt_ref[kb * nsub + c, HEAD_DIM * h:HEAD_DIM * (h + 1), :] for c in range(nsub)]
            if masked:
                for c in range(nsub):
                    acc = acc + widen(jnp.dot(vts[c], parts[c], preferred_element_type=F32), c)
            else:
                acc = acc + jnp.dot(jnp.concatenate(vts, axis=1), jnp.concatenate(parts, axis=0),
                                    preferred_element_type=F32)
            out.append((tail, acc))
        return tuple(out)

    n_full = qs // bq
    carries = tuple((jnp.zeros((1, bq), F32), jnp.zeros((HEAD_DIM, bq), F32))
                    for _ in range(hp))
    carries = step(n_full, carries, True)

    def cond(state):
        kb, carries = state
        tail_max = carries[0][0]
        for h in range(1, hp):
            tail_max = jnp.maximum(tail_max, carries[h][0])
        return (kb >= 0) & (jnp.max(tail_max) > SB_ZERO_LOG)

    def body(state):
        kb, carries = state
        return kb - 1, step(kb, carries, False)

    _, carries = lax.while_loop(cond, body, (n_full - 1, carries))
    _store_heads(o_ref, [acc for (_, acc) in carries])


def _sb_attn(main, q_col0, k_col0, vt_arr, vt_row0, n_heads, bq=256):
    B, S, _ = main.shape
    hp = n_heads
    qw = HEAD_DIM * hp
    return pl.pallas_call(
        functools.partial(_sb_kernel, hp=hp, bq=bq),
        grid=(B, S // bq),
        in_specs=[
            pl.BlockSpec((None, bq, qw), lambda b, i: (b, i, q_col0 // qw)),
            pl.BlockSpec((None, S, qw), lambda b, i: (b, 0, k_col0 // qw)),
            pl.BlockSpec((None, S // VT_BLK, qw, VT_BLK), lambda b, i: (b, 0, vt_row0 // qw, 0)),
        ],
        out_specs=pl.BlockSpec((None, bq, qw), lambda b, i: (b, i, 0)),
        out_shape=jax.ShapeDtypeStruct((B, S, n_heads * HEAD_DIM), BF16),
        scratch_shapes=[pltpu.VMEM((hp, bq, bq), F32)] * 3,
        compiler_params=_cparams(("arbitrary", "arbitrary")),
        name="sb_attn",
    )(main, main, vt_arr)


def _mla_prep_kernel(aux_ref, pos_ref, invf_ref, qn_ref, kvn_ref, wuq_ref, wuqr_ref,
                     wk_ref, wvt_ref, oq_ref, ok_ref, ovt_ref, *, q_scale):
    ang = pos_ref[...] * invf_ref[...]
    cos4, sin4 = jnp.cos(ang), jnp.sin(ang)
    lane = lax.broadcasted_iota(jnp.int32, ang.shape, 1)
    rotary = (lane >= NOPE_DIM) & (lane < NOPE_DIM + ROPE_DIM)
    cos_rows, sin_rows = [], []
    for m in range(LANES // ROPE_DIM):
        shift = (NOPE_DIM - ROPE_DIM * m) % LANES
        cm = cos4 if shift == 0 else pltpu.roll(cos4, shift, 1)
        sm = sin4 if shift == 0 else pltpu.roll(sin4, shift, 1)
        cos_rows.append(jnp.where(rotary, cm, 1.0))
        sin_rows.append(jnp.where(rotary, sm, 0.0))
    cos = jnp.concatenate(cos_rows, axis=0)
    sin = jnp.concatenate(sin_rows, axis=0)
    cq = _rms_bf16(aux_ref[:, 0:Q_LORA], qn_ref[...])
    ckv = _rms_bf16(aux_ref[:, Q_LORA:Q_LORA + KV_LORA], kvn_ref[...])
    o = Q_LORA + KV_LORA
    k_rope = aux_ref[:, o:o + LANES] * cos + aux_ref[:, o + LANES:o + 2 * LANES] * sin
    cos_q, sin_q = cos * q_scale, sin * q_scale
    n_heads = oq_ref.shape[-1] // LANES
    for h in range(0, n_heads, 2):
        cols = slice(h * LANES, (h + 2) * LANES)
        qa = jnp.dot(cq, wuq_ref[:, cols], preferred_element_type=F32)
        qb = jnp.dot(cq, wuqr_ref[:, cols], preferred_element_type=F32)
        kn = jnp.dot(ckv, wk_ref[:, cols], preferred_element_type=F32)
        for d in range(2):
            c1 = slice(d * LANES, (d + 1) * LANES)
            c2 = slice((h + d) * LANES, (h + d + 1) * LANES)
            oq_ref[:, c2] = (qa[:, c1] * cos_q + qb[:, c1] * sin_q).astype(BF16)
            ok_ref[:, c2] = (kn[:, c1] + k_rope).astype(BF16)
    nv = wvt_ref.shape[0]
    for r in range(0, nv, 256):
        vt = lax.dot_general(wvt_ref[r:r + 256, :], ckv, _NT,
                             preferred_element_type=F32).astype(BF16)
        _store_vt(ovt_ref, vt, r)


MLA_TM = 512


def _mla_prep(aux, pos, invf, qn, kvn, wuq, wuqr, wk, wvt, q_scale, tm=MLA_TM):
    B, S, na = aux.shape
    nq, nv = wuq.shape[1], wvt.shape[0]
    full = lambda a: pl.BlockSpec(a.shape, lambda b, i: (0,) * a.ndim)
    return pl.pallas_call(
        functools.partial(_mla_prep_kernel, q_scale=q_scale),
        grid=(B, S // tm),
        in_specs=[
            pl.BlockSpec((None, tm, na), lambda b, i: (b, i, 0)),
            pl.BlockSpec((None, None) + pos.shape[2:], lambda b, i: (b, i, 0, 0)),
            full(invf), full(qn), full(kvn), full(wuq), full(wuqr), full(wk), full(wvt),
        ],
        out_specs=[
            pl.BlockSpec((None, tm, nq), lambda b, i: (b, i, 0)),
            pl.BlockSpec((None, tm, nq), lambda b, i: (b, i, 0)),
            pl.BlockSpec((None, tm // VT_BLK, nv, VT_BLK), lambda b, i: (b, i, 0, 0)),
        ],
        out_shape=[
            jax.ShapeDtypeStruct((B, S, nq), BF16),
            jax.ShapeDtypeStruct((B, S, nq), BF16),
            jax.ShapeDtypeStruct((B, S // VT_BLK, nv, VT_BLK), BF16),
        ],
        compiler_params=_cparams(("arbitrary", "arbitrary")),
        name="mla_prep",
    )(aux, pos, invf, qn, kvn, wuq, wuqr, wk, wvt)


def _mlp_kernel(x_ref, oa_ref, ob_ref, wo_ref, g_ref, wu_ref, wd_ref, gf_ref, out_ref,
                x1_ref, h_ref, a_ref, *, final_norm, tf):
    na = oa_ref.shape[-1]
    x1 = (x_ref[...]
          + jnp.dot(oa_ref[...], wo_ref[0:na, :], preferred_element_type=F32)
          + jnp.dot(ob_ref[...], wo_ref[na:, :], preferred_element_type=F32))
    x1_ref[...] = x1
    h_ref[...] = _rms_bf16(x1, g_ref[...])
    for f in range(0, wu_ref.shape[1], tf):
        u = jnp.dot(h_ref[...], wu_ref[:, f:f + tf], preferred_element_type=F32)
        a_ref[:, f:f + tf] = jnp.square(jnp.maximum(u, 0.0)).astype(BF16)
    y = x1_ref[...] + jnp.dot(a_ref[...], wd_ref[...], preferred_element_type=F32)
    if final_norm:
        ms = jnp.mean(y * y, axis=-1, keepdims=True)
        y = y * lax.rsqrt(ms + EPS) * gf_ref[...]
    out_ref[...] = y


def _mlp(x2, oa, ob, wo, g, wu, wd, gf, final_norm, tm=512, tf=1024):
    T, D = x2.shape
    F = wu.shape[1]
    na, nb = oa.shape[1], ob.shape[1]
    const = lambda shape: pl.BlockSpec(shape, lambda i: (0, 0), pipeline_mode=pl.Buffered(1))
    return pl.pallas_call(
        functools.partial(_mlp_kernel, final_norm=final_norm, tf=tf),
        grid=(T // tm,),
        in_specs=[
            pl.BlockSpec((tm, D), lambda i: (i, 0)),
            pl.BlockSpec((tm, na), lambda i: (i, 0)),
            pl.BlockSpec((tm, nb), lambda i: (i, 0)),
            const((na + nb, D)),
            const((1, D)),
            const((D, F)),
            const((F, D)),
            const((1, D)),
        ],
        out_specs=pl.BlockSpec((tm, D), lambda i: (i, 0)),
        out_shape=jax.ShapeDtypeStruct((T, D), F32),
        scratch_shapes=[pltpu.VMEM((tm, D), F32), pltpu.VMEM((tm, D), BF16),
                        pltpu.VMEM((tm, F), BF16)],
        compiler_params=_cparams(("arbitrary",)),
        name="mlp",
    )(x2, oa, ob, wo, g.reshape(1, D), wu, wd, gf.reshape(1, D))


def _pad_cols(w, n):
    return jnp.pad(w, ((0, 0), (0, n - w.shape[1])))


def _rot_cols(w):
    half = ROPE_DIM // 2
    return jnp.concatenate([-w[:, half:], w[:, :half]], axis=1)


def _rope_slab(w):
    z = jnp.zeros((w.shape[0], NOPE_DIM), w.dtype)
    return jnp.concatenate([z, w, jnp.zeros((w.shape[0], LANES - NOPE_DIM - ROPE_DIM), w.dtype)], axis=1)


def _even_layer(x, g_mix, w_in, b_forget, rel_bias, w_out, g_mlp, w_up, w_down, g_final,
                final_norm):
    B, S, D = x.shape
    hf, hc = b_forget.shape[0], rel_bias.shape[0]
    wf, wc = hf * HEAD_DIM, hc * HEAD_DIM
    o = np.cumsum([0, wf, wf, wf, hf, wc, wc, wc])
    qa, ka, va, fa, qb, kb, vb = [w_in[:, o[n]:o[n + 1]] for n in range(7)]
    q_scale = HEAD_DIM ** -0.5 * LOG2E
    wm = jnp.concatenate([qa * q_scale, ka, qb * q_scale, kb], axis=1).astype(BF16)
    wvt = jnp.concatenate([va, vb], axis=1).T.astype(BF16)
    wa = _pad_cols(fa, LANES).astype(BF16)
    main, vt, aux = _inproj(x, g_mix, wm, wvt, wa)

    q_aug, k_aug = _logcum(aux, _pad_cols(b_forget.reshape(1, hf), LANES), hf)
    o_a = _flash("fox", main, 0, main, wf, vt, 0, (q_aug, k_aug), hf)

    assert rel_bias.shape[1] == CHUNK + REL_CLIP
    right = CK_EXT - rel_bias.shape[1] - (CHUNK + 1)
    ext = jnp.pad(rel_bias * LOG2E, ((0, 0), (CHUNK + 1, right)), mode="edge")
    o_b = _chunk_attn(main, 2 * wf, 2 * wf + wc, vt, wf, ext, hc)

    y = _mlp(x.reshape(B * S, D), o_a.reshape(B * S, wf), o_b.reshape(B * S, wc),
             w_out.astype(BF16), g_mlp, w_up.astype(BF16), w_down.astype(BF16), g_final,
             final_norm)
    return y.reshape(B, S, D)


def _odd_layer(x, positions, g_mix, w_in, q_norm, kv_norm, w_uq, w_ukv, w_out, g_mlp, w_up,
               w_down, g_final, final_norm):
    B, S, D = x.shape
    hm = w_ukv.shape[1] // (NOPE_DIM + HEAD_DIM)
    ws = w_in.shape[1] - Q_LORA - KV_LORA - ROPE_DIM
    hs = (ws // 3) // HEAD_DIM
    wsb = hs * HEAD_DIM
    o = np.cumsum([0, wsb, wsb, wsb, Q_LORA, KV_LORA, ROPE_DIM])
    qc, kc, vc, w_cq, w_ckv, w_kr = [w_in[:, o[n]:o[n + 1]] for n in range(6)]
    wm = jnp.concatenate([qc * HEAD_DIM ** -0.5, kc], axis=1).astype(BF16)
    wa = jnp.concatenate([w_cq, w_ckv, _rope_slab(w_kr), _rope_slab(_rot_cols(w_kr))],
                         axis=1).astype(BF16)
    main, vt, aux = _inproj(x, g_mix, wm, vc.T.astype(BF16), wa)
    o_c = _sb_attn(main, 0, wsb, vt, 0, hs)

    dq = NOPE_DIM + ROPE_DIM
    wuq3 = w_uq.reshape(Q_LORA, hm, dq)
    nope, ropew = wuq3[:, :, :NOPE_DIM], wuq3[:, :, NOPE_DIM:]
    zq = jnp.zeros((Q_LORA, hm, LANES - dq), w_uq.dtype)
    wuq = jnp.concatenate([nope, ropew, zq], axis=2).reshape(Q_LORA, hm * LANES).astype(BF16)
    half = ROPE_DIM // 2
    ropr = jnp.concatenate([-ropew[:, :, half:], ropew[:, :, :half]], axis=2)
    wuqr = jnp.concatenate([jnp.zeros_like(nope), ropr, zq], axis=2)
    wuqr = wuqr.reshape(Q_LORA, hm * LANES).astype(BF16)
    wkv3 = w_ukv.reshape(KV_LORA, hm, NOPE_DIM + HEAD_DIM)
    wk = jnp.concatenate([wkv3[:, :, :NOPE_DIM],
                          jnp.zeros((KV_LORA, hm, LANES - NOPE_DIM), w_ukv.dtype)], axis=2)
    wk = wk.reshape(KV_LORA, hm * LANES).astype(BF16)
    wv_t = wkv3[:, :, NOPE_DIM:].reshape(KV_LORA, hm * HEAD_DIM).T.astype(BF16)
    freqs = (ROPE_THETA ** (-jnp.arange(half, dtype=F32) / half))
    invf = jnp.tile(freqs, 2 * LANES // ROPE_DIM).reshape(1, LANES)
    groups = LANES // ROPE_DIM
    pos = positions.astype(F32).reshape(B, S // MLA_TM, groups, MLA_TM // groups)
    pos = jnp.repeat(jnp.swapaxes(pos, 2, 3), ROPE_DIM, axis=-1)
    qm, km, vtm = _mla_prep(aux, pos, invf, q_norm.reshape(1, Q_LORA),
                            kv_norm.reshape(1, KV_LORA), wuq, wuqr, wk, wv_t,
                            dq ** -0.5 * LOG2E)
    o_d = _flash("mla", qm, 0, km, 0, vtm, 0, None, hm)

    y = _mlp(x.reshape(B * S, D), o_c.reshape(B * S, wsb), o_d.reshape(B * S, hm * HEAD_DIM),
             w_out.astype(BF16), g_mlp, w_up.astype(BF16), w_down.astype(BF16), g_final,
             final_norm)
    return y.reshape(B, S, D)


def kernel(x, positions, norm_mix, norm_mlp, norm_final, w_in_ab, b_forget, rel_bias, w_out_ab,
           w_in_cd, q_norm, kv_norm, w_uq, w_ukv, w_out_cd, w_up, w_down):
    depth = norm_mix.shape[0]
    for layer in range(depth):
        last = layer == depth - 1
        if layer % 2 == 0:
            e = layer // 2
            x = _even_layer(x, norm_mix[layer], w_in_ab[e], b_forget[e], rel_bias[e], w_out_ab[e],
                            norm_mlp[layer], w_up[layer], w_down[layer], norm_final, last)
        else:
            o = layer // 2
            x = _odd_layer(x, positions, norm_mix[layer], w_in_cd[o], q_norm[o], kv_norm[o],
                           w_uq[o], w_ukv[o], w_out_cd[o], norm_mlp[layer], w_up[layer],
                           w_down[layer], norm_final, last)
    return x
```

```python
import functools
import math

import numpy as np
import jax
import jax.numpy as jnp
from jax import lax
from jax.experimental import pallas as pl
from jax.experimental.pallas import tpu as pltpu

F32 = jnp.float32
BF16 = jnp.bfloat16

EPS = 1e-6
HEAD_DIM = 64
CHUNK = 64
N_LEFT_CHUNKS = 8
REL_CLIP = 256
ROPE_DIM = 32
NOPE_DIM = 64
ROPE_THETA = 10000.0
Q_LORA = 384
KV_LORA = 256

LANES = 128
VT_BLK = LANES
SUB = LANES
FLASH_HP = 8
FLASH_BQ = 512
FLASH_SUB = 256
FIXED_WIDTHS = (4, 2, 1)
SAFE_GAP = 80.0
NEG = -1e30
LOG2E = math.log2(math.e)
SB_ZERO_LOG = -104.0
VMEM_LIMIT = 56 * 1024 * 1024

_NT = (((1,), (1,)), ((), ()))


def _cparams(sem):
    return pltpu.CompilerParams(dimension_semantics=sem, vmem_limit_bytes=VMEM_LIMIT)


def _rms_bf16(x, g):
    ms = jnp.mean(x * x, axis=-1, keepdims=True)
    return (x * lax.rsqrt(ms + EPS) * g).astype(BF16)


def _store_vt(ovt_ref, vt, row0):
    rows, tm = vt.shape
    for c in range(tm // VT_BLK):
        ovt_ref[c, row0:row0 + rows, :] = vt[:, c * VT_BLK:(c + 1) * VT_BLK]


def _inproj_kernel(x_ref, g_ref, wm_ref, wvt_ref, wa_ref, om_ref, ovt_ref, oa_ref):
    h = _rms_bf16(x_ref[...], g_ref[...])
    nm = om_ref.shape[-1]
    for c in range(0, nm, 512):
        om_ref[:, c:c + 512] = jnp.dot(
            h, wm_ref[:, c:c + 512], preferred_element_type=F32).astype(BF16)
    nv = wvt_ref.shape[0]
    for r in range(0, nv, 256):
        vt = lax.dot_general(wvt_ref[r:r + 256, :], h, _NT,
                             preferred_element_type=F32).astype(BF16)
        _store_vt(ovt_ref, vt, r)
    oa_ref[...] = jnp.dot(h, wa_ref[...], preferred_element_type=F32)


def _inproj(x, g, wm, wvt, wa, tm=512):
    B, S, D = x.shape
    nm, nv, na = wm.shape[1], wvt.shape[0], wa.shape[1]
    return pl.pallas_call(
        _inproj_kernel,
        grid=(B, S // tm),
        in_specs=[
            pl.BlockSpec((None, tm, D), lambda b, i: (b, i, 0)),
            pl.BlockSpec((1, D), lambda b, i: (0, 0)),
            pl.BlockSpec((D, nm), lambda b, i: (0, 0)),
            pl.BlockSpec((nv, D), lambda b, i: (0, 0)),
            pl.BlockSpec((D, na), lambda b, i: (0, 0)),
        ],
        out_specs=[
            pl.BlockSpec((None, tm, nm), lambda b, i: (b, i, 0)),
            pl.BlockSpec((None, tm // VT_BLK, nv, VT_BLK), lambda b, i: (b, i, 0, 0)),
            pl.BlockSpec((None, tm, na), lambda b, i: (b, i, 0)),
        ],
        out_shape=[
            jax.ShapeDtypeStruct((B, S, nm), BF16),
            jax.ShapeDtypeStruct((B, S // VT_BLK, nv, VT_BLK), BF16),
            jax.ShapeDtypeStruct((B, S, na), F32),
        ],
        compiler_params=_cparams(("arbitrary", "arbitrary")),
        name="inproj",
    )(x, g.reshape(1, D), wm, wvt, wa)


def _split3(x):
    hi = x.astype(BF16)
    r = x - hi.astype(F32)
    mid = r.astype(BF16)
    lo = (r - mid.astype(F32)).astype(BF16)
    return hi, mid, lo


AUG_W = 8


def _logcum_kernel(fa_ref, b_ref, pq_ref, pk_ref, oneq_ref, onek_ref, oq_ref, ok_ref, carry_ref):
    @pl.when(pl.program_id(1) == 0)
    def _():
        carry_ref[...] = jnp.zeros_like(carry_ref)

    z = fa_ref[...] + b_ref[...]
    lf = jnp.minimum(z, 0.0) - jnp.log(1.0 + jnp.exp(-jnp.abs(z)))
    tc = lf.shape[0]
    r = lax.broadcasted_iota(jnp.int32, (tc, tc), 0)
    c = lax.broadcasted_iota(jnp.int32, (tc, tc), 1)
    tri = jnp.where(r >= c, 1.0, 0.0).astype(BF16)
    cs = carry_ref[...]
    for part in _split3(lf):
        cs = cs + jnp.dot(tri, part, preferred_element_type=F32)
    carry_ref[...] = cs[tc - 1:tc, :]
    qa, ka = oneq_ref[...], onek_ref[...]
    for n, part in enumerate(_split3(cs * LOG2E)):
        qa = qa + jnp.dot(part, pq_ref[n], preferred_element_type=F32)
        ka = ka + jnp.dot(part, pk_ref[n], preferred_element_type=F32)
    oq_ref[...] = qa.astype(BF16)
    ok_ref[...] = ka.astype(BF16)


def _logcum(fa, bias, n_heads, tc=512):
    B, S, W = fa.shape
    na = LANES
    pq = np.zeros((3, W, na), np.float32)
    pk = np.zeros((3, W, na), np.float32)
    oneq = np.zeros((1, na), np.float32)
    onek = np.zeros((1, na), np.float32)
    for h in range(n_heads):
        base = h * AUG_W
        for n in range(3):
            pq[n, h, base + n] = 1.0
            pk[n, h, base + 3 + n] = -1.0
        oneq[0, base + 3:base + 6] = 1.0
        onek[0, base:base + 3] = 1.0
    const = lambda a: pl.BlockSpec(a.shape, lambda b, i: (0,) * a.ndim)
    args = [jnp.asarray(pq, BF16), jnp.asarray(pk, BF16), jnp.asarray(oneq), jnp.asarray(onek)]
    return pl.pallas_call(
        _logcum_kernel,
        grid=(B, S // tc),
        in_specs=[pl.BlockSpec((None, tc, W), lambda b, i: (b, i, 0)),
                  pl.BlockSpec((1, W), lambda b, i: (0, 0))] + [const(a) for a in args],
        out_specs=[pl.BlockSpec((None, tc, na), lambda b, i: (b, i, 0))] * 2,
        out_shape=[jax.ShapeDtypeStruct((B, S, na), BF16)] * 2,
        scratch_shapes=[pltpu.VMEM((1, W), F32)],
        compiler_params=_cparams(("arbitrary", "arbitrary")),
        name="logcum",
    )(fa, bias, *args)


def _pair_mask_q(q2, j):
    lane = lax.broadcasted_iota(jnp.int32, q2.shape, 1)
    keep = (lane >= HEAD_DIM * j) & (lane < HEAD_DIM * (j + 1))
    return jnp.where(keep, q2, jnp.zeros_like(q2))


ONES_ROWS = 16


def _softmax_step(tiles, vts, carry, tile_max=None):
    m, acc = carry
    if tile_max is not None:
        m_new = jnp.maximum(m, tile_max)
    else:
        m_new = m
        for tile in tiles:
            m_new = jnp.maximum(m_new, jnp.max(tile(), axis=0, keepdims=True))
    alpha = jnp.exp2(m - m_new)
    pv = None
    for tile, vt in zip(tiles, vts):
        p = jnp.exp2(tile() - m_new).astype(BF16)
        vt1 = jnp.concatenate([vt, jnp.ones((ONES_ROWS, vt.shape[1]), BF16)], axis=0)
        d = jnp.dot(vt1, p, preferred_element_type=F32)
        pv = d if pv is None else pv + d
    return m_new, alpha * acc + pv


def _softmax_init(bq):
    return (jnp.full((1, bq), NEG, F32), jnp.zeros((HEAD_DIM + ONES_ROWS, bq), F32))


def _softmax_out(carry):
    _, acc = carry
    return acc[0:HEAD_DIM] / acc[HEAD_DIM:HEAD_DIM + 1]


def _store_heads(o_ref, outs):
    oT = jnp.concatenate(outs, axis=0)
    o_ref[...] = oT.T.astype(o_ref.dtype)


def _flash_kernel(*refs, mode, hp, bq):
    if mode == "fox":
        q_ref, k_ref, vt_ref, qaug_ref, kaug_ref, o_ref = refs[:6]
    else:
        q_ref, k_ref, vt_ref, o_ref = refs[:4]
    sa_ref, sb_ref, ma_ref, mb_ref, qt_ref, kn_ref = refs[-6:]
    qs = pl.program_id(2) * bq
    sub = FLASH_SUB
    row = lax.broadcasted_iota(jnp.int32, (sub, bq), 0)
    col = lax.broadcasted_iota(jnp.int32, (sub, bq), 1)
    if mode == "fox":
        kcols = [slice(LANES * (h // 2), LANES * (h // 2 + 1)) for h in range(hp)]
        lane = lax.broadcasted_iota(jnp.int32, (bq, LANES), 1)
        qa = qaug_ref[...]
        qms = []
        for h in range(hp):
            first = AUG_W * (pl.program_id(1) * hp + h)
            own = (lane >= first) & (lane < first + AUG_W)
            qms.append(jnp.concatenate(
                [_pair_mask_q(q_ref[:, kcols[h]], h % 2),
                 jnp.where(own, qa, jnp.zeros_like(qa))], axis=1))
    else:
        kcols = [slice(LANES * h, LANES * (h + 1)) for h in range(hp)]
        qms = [q_ref[:, kcols[h]] for h in range(hp)]
    for h in range(hp):
        qt_ref[h] = qms[h].T

    @pl.when(pl.program_id(2) == 0)
    def _():
        klane = lax.broadcasted_iota(jnp.int32, (1, LANES), 1)
        for h in range(hp):
            kabs = jnp.max(jnp.abs(k_ref[:, kcols[h]].astype(F32)), axis=0, keepdims=True)
            sq = kabs * kabs
            if mode == "fox":
                sq = jnp.where((klane >= HEAD_DIM * (h % 2)) & (klane < HEAD_DIM * (h % 2 + 1)),
                               sq, 0.0)
            kn_ref[h] = jnp.broadcast_to(jnp.sqrt(jnp.sum(sq, axis=1, keepdims=True)), (1, bq))

    def scores(sb, h, masked):
        ks = pl.multiple_of(sb * sub, sub)
        k = k_ref[pl.ds(ks, sub), kcols[h]]
        if mode == "fox":
            k = jnp.concatenate([k, kaug_ref[pl.ds(ks, sub), :]], axis=1)
        sT = jnp.dot(k, qt_ref[h], preferred_element_type=F32)
        if masked and mode == "fox":
            sT = jnp.where(ks + row <= qs + col, sT, NEG)
        elif masked:
            sT = jnp.where(((ks + row) >> 6) <= ((qs + col) >> 6), sT, NEG)
        return sT

    nsub = bq // sub
    nvt = sub // VT_BLK

    def produce(buf, sb0, masked, h):
        s_buf, m_buf = buf
        tile_max = None
        for c in range(nsub):
            sT = scores(sb0 + c, h, masked)
            s_buf[h, c] = sT
            cm = jnp.max(sT, axis=0, keepdims=True)
            tile_max = cm if tile_max is None else jnp.maximum(tile_max, cm)
        m_buf[h] = tile_max

    def consume(buf, sb0, carry, h):
        s_buf, m_buf = buf
        tiles = [lambda c=c: s_buf[h, c] for c in range(nsub)]
        vts = [jnp.concatenate([vt_ref[(sb0 + c) * nvt + v, HEAD_DIM * h:HEAD_DIM * (h + 1), :]
                                for v in range(nvt)], axis=1) for c in range(nsub)]
        return _softmax_step(tiles, vts, carry, tile_max=m_buf[h])

    def stage(cur, cur_sb, nxt, nxt_sb, carries):
        if nxt is not None:
            for h in range(hp):
                produce(nxt, nxt_sb, False, h)
        return tuple(consume(cur, cur_sb, carries[h], h) for h in range(hp))

    n = pl.program_id(2)
    diag_sb = qs // sub
    buf_a, buf_b = (sa_ref, ma_ref), (sb_ref, mb_ref)
    for h in range(hp):
        produce(buf_a, diag_sb, True, h)

    def pair(j, carries):
        carries = stage(buf_a, jnp.where(j == 0, diag_sb, (2 * j - 1) * nsub),
                        buf_b, 2 * j * nsub, carries)
        return stage(buf_b, 2 * j * nsub,
                     buf_a, jnp.minimum(2 * j + 1, n - 1) * nsub, carries)

    def online(_):
        carries = tuple(_softmax_init(bq) for _ in range(hp))
        carries = lax.fori_loop(0, (n + 1) // 2, pair, carries)
        carries = lax.cond(
            n % 2 == 0,
            lambda c: stage(buf_a, jnp.where(n == 0, diag_sb, (n - 1) * nsub), None, None, c),
            lambda c: c, carries)
        return jnp.concatenate([_softmax_out(c) for c in carries], axis=0)

    refs_ = []
    gap = None
    for h in range(hp):
        qf = qt_ref[h, 0:LANES, :].astype(F32)
        bound = jnp.sqrt(jnp.sum(qf * qf, axis=0, keepdims=True)) * kn_ref[h] * 1.01 + 1e-3
        refs_.append(bound)
        g = jnp.max(bound - ma_ref[h])
        gap = g if gap is None else jnp.maximum(gap, g)

    def weigh_add(acc, h, s, sb0, nblk):
        p = jnp.exp2(s - refs_[h]).astype(BF16)
        vt = jnp.concatenate([vt_ref[sb0 * nvt + v, HEAD_DIM * h:HEAD_DIM * (h + 1), :]
                              for v in range(nblk * nvt)], axis=1)
        vt1 = jnp.concatenate([vt, jnp.ones((ONES_ROWS, nblk * sub), BF16)], axis=0)
        return acc + jnp.dot(vt1, p, preferred_element_type=F32)

    def fixed_reference(_):
        def run(kb, accs, nblk):
            ks = pl.multiple_of(kb * sub, sub)
            ss = []
            for h in range(hp):
                k = k_ref[pl.ds(ks, nblk * sub), kcols[h]]
                if mode == "fox":
                    k = jnp.concatenate([k, kaug_ref[pl.ds(ks, nblk * sub), :]], axis=1)
                ss.append(jnp.dot(k, qt_ref[h], preferred_element_type=F32))
            return tuple(weigh_add(accs[h], h, ss[h], kb, nblk) for h in range(hp))

        accs = tuple(jnp.zeros((HEAD_DIM + ONES_ROWS, bq), F32) for _ in range(hp))
        done = 0
        for width in FIXED_WIDTHS:
            trips = (diag_sb - done) // width
            accs = lax.fori_loop(0, trips,
                                 lambda j, a, done=done, width=width: run(done + j * width, a, width),
                                 accs)
            done = done + trips * width
        for c in range(nsub):
            accs = [weigh_add(accs[h], h, sa_ref[h, c], diag_sb + c, 1) for h in range(hp)]
        return jnp.concatenate([a[0:HEAD_DIM] / a[HEAD_DIM:HEAD_DIM + 1] for a in accs], axis=0)

    oT = lax.cond(gap <= SAFE_GAP, fixed_reference, online, None)
    o_ref[...] = oT.T.astype(o_ref.dtype)


def _flash(mode, q_arr, q_col0, k_arr, k_col0, vt_arr, vt_row0, extra, n_heads,
           hp=FLASH_HP, bq=FLASH_BQ):
    B, S, _ = q_arr.shape
    qw = (HEAD_DIM if mode == "fox" else LANES) * hp
    vw = HEAD_DIM * hp
    in_specs = [
        pl.BlockSpec((None, bq, qw), lambda b, g, i: (b, i, q_col0 // qw + g)),
        pl.BlockSpec((None, S, qw), lambda b, g, i: (b, 0, k_col0 // qw + g)),
        pl.BlockSpec((None, S // VT_BLK, vw, VT_BLK),
                     lambda b, g, i: (b, 0, vt_row0 // vw + g, 0)),
    ]
    args = [q_arr, k_arr, vt_arr]
    if mode == "fox":
        q_aug, k_aug = extra
        in_specs += [
            pl.BlockSpec((None, bq, LANES), lambda b, g, i: (b, i, 0)),
            pl.BlockSpec((None, S, LANES), lambda b, g, i: (b, 0, 0)),
        ]
        args += [q_aug, k_aug]
    return pl.pallas_call(
        functools.partial(_flash_kernel, mode=mode, hp=hp, bq=bq),
        grid=(B, n_heads // hp, S // bq),
        in_specs=in_specs,
        out_specs=pl.BlockSpec((None, bq, vw), lambda b, g, i: (b, i, g)),
        out_shape=jax.ShapeDtypeStruct((B, S, n_heads * HEAD_DIM), BF16),
        scratch_shapes=([pltpu.VMEM((hp, bq // FLASH_SUB, FLASH_SUB, bq), F32)] * 2
                        + [pltpu.VMEM((hp, 1, bq), F32)] * 2
                        + [pltpu.VMEM((hp, 2 * LANES if mode == "fox" else LANES, bq), BF16),
                           pltpu.VMEM((hp, 1, bq), F32)]),
        compiler_params=_cparams(("arbitrary", "arbitrary", "arbitrary")),
        name="flash_" + mode,
    )(*args)


CK_B = 2 * CHUNK
CK_NW = N_LEFT_CHUNKS * CHUNK // CK_B + 1
CK_EXT = (CK_NW + 1) * CK_B


def _chunk_kernel(q_ref, k_ref, vt_ref, ext_ref, o_ref, tab_ref, s_ref, *, hp, nq):
    i = pl.program_id(1)

    @pl.when(i == 0)
    def _():
        jj = lax.broadcasted_iota(jnp.int32, (CK_B, CK_B), 0)
        rr = lax.broadcasted_iota(jnp.int32, (CK_B, CK_B), 1)
        for h in range(hp):
            for w in range(CK_NW):
                a = (CK_NW - 1 - w) * CK_B
                g = jnp.broadcast_to(ext_ref[h:h + 1, a:a + 2 * CK_B], (CK_B, 2 * CK_B))
                t = pltpu.roll(g, CK_B, 1, stride=1, stride_axis=0)[:, :CK_B]
                if w == 0:
                    t = jnp.where((rr >= CHUNK) & (jj < CHUNK), NEG, t)
                if w == CK_NW - 1:
                    t = jnp.where((rr < CHUNK) & (jj >= CHUNK), NEG, t)
                tab_ref[h, w * CK_B:(w + 1) * CK_B, :] = t

    kcols = [slice(LANES * (h // 2), LANES * (h // 2 + 1)) for h in range(hp)]
    firsts = [i * nq + u - (CK_NW - 1) for u in range(nq)]

    def finish():
        for u in range(nq):
            kbc = [jnp.maximum(firsts[u] + w, 0) for w in range(CK_NW)]
            outs = []
            for h in range(hp):
                vt = jnp.concatenate([vt_ref[kbc[w], HEAD_DIM * h:HEAD_DIM * (h + 1), :]
                                      for w in range(CK_NW)], axis=1)
                outs.append(_softmax_out(_softmax_step([lambda u=u, h=h: s_ref[u, h]], [vt],
                                                       _softmax_init(CK_B))))
            oT = jnp.concatenate(outs, axis=0)
            o_ref[u * CK_B:(u + 1) * CK_B, :] = oT.T.astype(o_ref.dtype)

    def pair_scores(u, p, ks, nrows):
        q2 = q_ref[u * CK_B:(u + 1) * CK_B, kcols[2 * p]]
        qq = jnp.concatenate([_pair_mask_q(q2, 0), _pair_mask_q(q2, 1)], axis=0)
        return lax.dot_general(k_ref[pl.ds(ks, nrows), kcols[2 * p]], qq, _NT,
                               preferred_element_type=F32)

    @pl.when(firsts[0] >= 0)
    def _():
        for u in range(nq):
            ks = pl.multiple_of(firsts[u] * CK_B, CK_B)
            for p in range(hp // 2):
                sT = pair_scores(u, p, ks, CK_NW * CK_B)
                for j in range(2):
                    s_ref[u, 2 * p + j] = sT[:, j * CK_B:(j + 1) * CK_B] + tab_ref[2 * p + j]
        finish()

    @pl.when(firsts[0] < 0)
    def _():
        for u in range(nq):
            for p in range(hp // 2):
                for w in range(CK_NW):
                    rows = slice(w * CK_B, (w + 1) * CK_B)
                    ks = pl.multiple_of(jnp.maximum(firsts[u] + w, 0) * CK_B, CK_B)
                    sT = pair_scores(u, p, ks, CK_B)
                    for j in range(2):
                        s_ref[u, 2 * p + j, rows, :] = jnp.where(
                            firsts[u] + w >= 0,
                            sT[:, j * CK_B:(j + 1) * CK_B] + tab_ref[2 * p + j, rows, :], NEG)
        finish()


def _chunk_attn(main, q_col0, k_col0, vt_arr, vt_row0, ext, n_heads, nq=4):
    B, S, _ = main.shape
    hp = n_heads
    qw, vw = HEAD_DIM * hp, HEAD_DIM * hp
    return pl.pallas_call(
        functools.partial(_chunk_kernel, hp=hp, nq=nq),
        grid=(B, S // (nq * CK_B)),
        in_specs=[
            pl.BlockSpec((None, nq * CK_B, qw), lambda b, i: (b, i, q_col0 // qw)),
            pl.BlockSpec((None, S, qw), lambda b, i: (b, 0, k_col0 // qw)),
            pl.BlockSpec((None, S // VT_BLK, vw, VT_BLK), lambda b, i: (b, 0, vt_row0 // vw, 0)),
            pl.BlockSpec((hp, CK_EXT), lambda b, i: (0, 0)),
        ],
        out_specs=pl.BlockSpec((None, nq * CK_B, vw), lambda b, i: (b, i, 0)),
        out_shape=jax.ShapeDtypeStruct((B, S, n_heads * HEAD_DIM), BF16),
        scratch_shapes=[pltpu.VMEM((hp, CK_NW * CK_B, CK_B), F32),
                        pltpu.VMEM((nq, hp, CK_NW * CK_B, CK_B), F32)],
        compiler_params=_cparams(("arbitrary", "arbitrary")),
        name="chunk_attn",
    )(main, main, vt_arr, ext)


def _sb_kernel(q_ref, k_ref, vt_ref, o_ref, z_ref, lb_ref, sfx_ref, *, hp, bq):
    qs = pl.program_id(1) * bq
    nsub = bq // SUB
    row = lax.broadcasted_iota(jnp.int32, (SUB, bq), 0)
    col = lax.broadcasted_iota(jnp.int32, (SUB, bq), 1)
    ur = lax.broadcasted_iota(jnp.int32, (SUB, 2 * SUB), 0)
    uc = lax.broadcasted_iota(jnp.int32, (SUB, 2 * SUB), 1) & (SUB - 1)
    upper2 = jnp.where(uc > ur, 1.0, 0.0).astype(BF16)
    kcols = [slice(LANES * (h // 2), LANES * (h // 2 + 1)) for h in range(hp)]
    qms = [_pair_mask_q(q_ref[:, kcols[h]], h % 2) for h in range(hp)]

    def step(kb, carries, masked):
        ks = pl.multiple_of(kb * bq, bq)
        for h in range(hp):
            z_ref[h] = lax.dot_general(k_ref[pl.ds(ks, bq), kcols[h]], qms[h], _NT,
                                       preferred_element_type=F32)
        first_col = [c * SUB if masked else 0 for c in range(nsub)]

        def widen(x, c):
            if first_col[c] == 0:
                return x
            return jnp.concatenate([jnp.zeros((x.shape[0], first_col[c]), x.dtype), x], axis=1)

        totals = []
        for h in range(hp):
            tot = []
            for c in range(nsub):
                rows, cols = slice(c * SUB, (c + 1) * SUB), slice(first_col[c], bq)
                z = z_ref[h, rows, cols]
                l1 = jnp.log(1.0 + jnp.exp(-jnp.abs(z)))
                log_beta = jnp.minimum(z, 0.0) - l1
                log_keep = log_beta - z
                if masked:
                    valid = (ks + c * SUB + row < qs + col)[:, cols]
                    log_keep = jnp.where(valid, log_keep, 0.0)
                lb_ref[h, rows, cols] = log_beta
                hi = log_keep.astype(BF16)
                lo = (log_keep - hi.astype(F32)).astype(BF16)
                sfx = jnp.dot(upper2, jnp.concatenate([hi, lo], axis=0),
                              preferred_element_type=F32)
                sfx_ref[h, rows, cols] = sfx
                tot.append(widen(sfx[0:1, :] + log_keep[0:1, :], c))
            totals.append(tot)
        out = []
        for h in range(hp):
            tail, acc = carries[h]
            parts = [None] * nsub
            for c in range(nsub - 1, -1, -1):
                rows, cols = slice(c * SUB, (c + 1) * SUB), slice(first_col[c], bq)
                a = jnp.exp(lb_ref[h, rows, cols] + sfx_ref[h, rows, cols] + tail[:, cols])
                if masked:
                    a = jnp.where((ks + c * SUB + row < qs + col)[:, cols], a, 0.0)
                parts[c] = a.astype(BF16)
                tail = tail + totals[h][c]
            vts = [vt_ref[kb * nsub + c, HEAD_DIM * h:HEAD_DIM * (h + 1), :] for c in range(nsub)]
            if masked:
                for c in range(nsub):
                    acc = acc + widen(jnp.dot(vts[c], parts[c], preferred_element_type=F32), c)
            else:
                acc = acc + jnp.dot(jnp.concatenate(vts, axis=1), jnp.concatenate(parts, axis=0),
                                    preferred_element_type=F32)
            out.append((tail, acc))
        return tuple(out)

    n_full = qs // bq
    carries = tuple((jnp.zeros((1, bq), F32), jnp.zeros((HEAD_DIM, bq), F32))
                    for _ in range(hp))
    carries = step(n_full, carries, True)

    def cond(state):
        kb, carries = state
        tail_max = carries[0][0]
        for h in range(1, hp):
            tail_max = jnp.maximum(tail_max, carries[h][0])
        return (kb >= 0) & (jnp.max(tail_max) > SB_ZERO_LOG)

    def body(state):
        kb, carries = state
        return kb - 1, step(kb, carries, False)

    _, carries = lax.while_loop(cond, body, (n_full - 1, carries))
    _store_heads(o_ref, [acc for (_, acc) in carries])


def _sb_attn(main, q_col0, k_col0, vt_arr, vt_row0, n_heads, bq=256):
    B, S, _ = main.shape
    hp = n_heads
    qw = HEAD_DIM * hp
    return pl.pallas_call(
        functools.partial(_sb_kernel, hp=hp, bq=bq),
        grid=(B, S // bq),
        in_specs=[
            pl.BlockSpec((None, bq, qw), lambda b, i: (b, i, q_col0 // qw)),
            pl.BlockSpec((None, S, qw), lambda b, i: (b, 0, k_col0 // qw)),
            pl.BlockSpec((None, S // VT_BLK, qw, VT_BLK), lambda b, i: (b, 0, vt_row0 // qw, 0)),
        ],
        out_specs=pl.BlockSpec((None, bq, qw), lambda b, i: (b, i, 0)),
        out_shape=jax.ShapeDtypeStruct((B, S, n_heads * HEAD_DIM), BF16),
        scratch_shapes=[pltpu.VMEM((hp, bq, bq), F32)] * 3,
        compiler_params=_cparams(("arbitrary", "arbitrary")),
        name="sb_attn",
    )(main, main, vt_arr)


def _mla_prep_kernel(aux_ref, pos_ref, invf_ref, qn_ref, kvn_ref, wuq_ref, wuqr_ref,
                     wk_ref, wvt_ref, oq_ref, ok_ref, ovt_ref, *, q_scale):
    ang = pos_ref[...] * invf_ref[...]
    cos4, sin4 = jnp.cos(ang), jnp.sin(ang)
    lane = lax.broadcasted_iota(jnp.int32, ang.shape, 1)
    rotary = (lane >= NOPE_DIM) & (lane < NOPE_DIM + ROPE_DIM)
    cos_rows, sin_rows = [], []
    for m in range(LANES // ROPE_DIM):
        shift = (NOPE_DIM - ROPE_DIM * m) % LANES
        cm = cos4 if shift == 0 else pltpu.roll(cos4, shift, 1)
        sm = sin4 if shift == 0 else pltpu.roll(sin4, shift, 1)
        cos_rows.append(jnp.where(rotary, cm, 1.0))
        sin_rows.append(jnp.where(rotary, sm, 0.0))
    cos = jnp.concatenate(cos_rows, axis=0)
    sin = jnp.concatenate(sin_rows, axis=0)
    cq = _rms_bf16(aux_ref[:, 0:Q_LORA], qn_ref[...])
    ckv = _rms_bf16(aux_ref[:, Q_LORA:Q_LORA + KV_LORA], kvn_ref[...])
    o = Q_LORA + KV_LORA
    k_rope = aux_ref[:, o:o + LANES] * cos + aux_ref[:, o + LANES:o + 2 * LANES] * sin
    cos_q, sin_q = cos * q_scale, sin * q_scale
    n_heads = oq_ref.shape[-1] // LANES
    for h in range(0, n_heads, 2):
        cols = slice(h * LANES, (h + 2) * LANES)
        qa = jnp.dot(cq, wuq_ref[:, cols], preferred_element_type=F32)
        qb = jnp.dot(cq, wuqr_ref[:, cols], preferred_element_type=F32)
        kn = jnp.dot(ckv, wk_ref[:, cols], preferred_element_type=F32)
        for d in range(2):
            c1 = slice(d * LANES, (d + 1) * LANES)
            c2 = slice((h + d) * LANES, (h + d + 1) * LANES)
            oq_ref[:, c2] = (qa[:, c1] * cos_q + qb[:, c1] * sin_q).astype(BF16)
            ok_ref[:, c2] = (kn[:, c1] + k_rope).astype(BF16)
    nv = wvt_ref.shape[0]
    for r in range(0, nv, 256):
        vt = lax.dot_general(wvt_ref[r:r + 256, :], ckv, _NT,
                             preferred_element_type=F32).astype(BF16)
        _store_vt(ovt_ref, vt, r)


MLA_TM = 512


def _mla_prep(aux, pos, invf, qn, kvn, wuq, wuqr, wk, wvt, q_scale, tm=MLA_TM):
    B, S, na = aux.shape
    nq, nv = wuq.shape[1], wvt.shape[0]
    full = lambda a: pl.BlockSpec(a.shape, lambda b, i: (0,) * a.ndim)
    return pl.pallas_call(
        functools.partial(_mla_prep_kernel, q_scale=q_scale),
        grid=(B, S // tm),
        in_specs=[
            pl.BlockSpec((None, tm, na), lambda b, i: (b, i, 0)),
            pl.BlockSpec((None, None) + pos.shape[2:], lambda b, i: (b, i, 0, 0)),
            full(invf), full(qn), full(kvn), full(wuq), full(wuqr), full(wk), full(wvt),
        ],
        out_specs=[
            pl.BlockSpec((None, tm, nq), lambda b, i: (b, i, 0)),
            pl.BlockSpec((None, tm, nq), lambda b, i: (b, i, 0)),
            pl.BlockSpec((None, tm // VT_BLK, nv, VT_BLK), lambda b, i: (b, i, 0, 0)),
        ],
        out_shape=[
            jax.ShapeDtypeStruct((B, S, nq), BF16),
            jax.ShapeDtypeStruct((B, S, nq), BF16),
            jax.ShapeDtypeStruct((B, S // VT_BLK, nv, VT_BLK), BF16),
        ],
        compiler_params=_cparams(("arbitrary", "arbitrary")),
        name="mla_prep",
    )(aux, pos, invf, qn, kvn, wuq, wuqr, wk, wvt)


def _mlp_kernel(x_ref, oa_ref, ob_ref, wo_ref, g_ref, wu_ref, wd_ref, gf_ref, out_ref,
                x1_ref, h_ref, a_ref, *, final_norm, tf):
    na = oa_ref.shape[-1]
    x1 = (x_ref[...]
          + jnp.dot(oa_ref[...], wo_ref[0:na, :], preferred_element_type=F32)
          + jnp.dot(ob_ref[...], wo_ref[na:, :], preferred_element_type=F32))
    x1_ref[...] = x1
    h_ref[...] = _rms_bf16(x1, g_ref[...])
    for f in range(0, wu_ref.shape[1], tf):
        u = jnp.dot(h_ref[...], wu_ref[:, f:f + tf], preferred_element_type=F32)
        a_ref[:, f:f + tf] = jnp.square(jnp.maximum(u, 0.0)).astype(BF16)
    y = x1_ref[...] + jnp.dot(a_ref[...], wd_ref[...], preferred_element_type=F32)
    if final_norm:
        ms = jnp.mean(y * y, axis=-1, keepdims=True)
        y = y * lax.rsqrt(ms + EPS) * gf_ref[...]
    out_ref[...] = y


def _mlp(x2, oa, ob, wo, g, wu, wd, gf, final_norm, tm=512, tf=1024):
    T, D = x2.shape
    F = wu.shape[1]
    na, nb = oa.shape[1], ob.shape[1]
    const = lambda shape: pl.BlockSpec(shape, lambda i: (0, 0), pipeline_mode=pl.Buffered(1))
    return pl.pallas_call(
        functools.partial(_mlp_kernel, final_norm=final_norm, tf=tf),
        grid=(T // tm,),
        in_specs=[
            pl.BlockSpec((tm, D), lambda i: (i, 0)),
            pl.BlockSpec((tm, na), lambda i: (i, 0)),
            pl.BlockSpec((tm, nb), lambda i: (i, 0)),
            const((na + nb, D)),
            const((1, D)),
            const((D, F)),
            const((F, D)),
            const((1, D)),
        ],
        out_specs=pl.BlockSpec((tm, D), lambda i: (i, 0)),
        out_shape=jax.ShapeDtypeStruct((T, D), F32),
        scratch_shapes=[pltpu.VMEM((tm, D), F32), pltpu.VMEM((tm, D), BF16),
                        pltpu.VMEM((tm, F), BF16)],
        compiler_params=_cparams(("arbitrary",)),
        name="mlp",
    )(x2, oa, ob, wo, g.reshape(1, D), wu, wd, gf.reshape(1, D))


def _pad_cols(w, n):
    return jnp.pad(w, ((0, 0), (0, n - w.shape[1])))


def _rot_cols(w):
    half = ROPE_DIM // 2
    return jnp.concatenate([-w[:, half:], w[:, :half]], axis=1)


def _rope_slab(w):
    z = jnp.zeros((w.shape[0], NOPE_DIM), w.dtype)
    return jnp.concatenate([z, w, jnp.zeros((w.shape[0], LANES - NOPE_DIM - ROPE_DIM), w.dtype)], axis=1)


def _even_layer(x, g_mix, w_in, b_forget, rel_bias, w_out, g_mlp, w_up, w_down, g_final,
                final_norm):
    B, S, D = x.shape
    hf, hc = b_forget.shape[0], rel_bias.shape[0]
    wf, wc = hf * HEAD_DIM, hc * HEAD_DIM
    o = np.cumsum([0, wf, wf, wf, hf, wc, wc, wc])
    qa, ka, va, fa, qb, kb, vb = [w_in[:, o[n]:o[n + 1]] for n in range(7)]
    q_scale = HEAD_DIM ** -0.5 * LOG2E
    wm = jnp.concatenate([qa * q_scale, ka, qb * q_scale, kb], axis=1).astype(BF16)
    wvt = jnp.concatenate([va, vb], axis=1).T.astype(BF16)
    wa = _pad_cols(fa, LANES).astype(BF16)
    main, vt, aux = _inproj(x, g_mix, wm, wvt, wa)

    q_aug, k_aug = _logcum(aux, _pad_cols(b_forget.reshape(1, hf), LANES), hf)
    o_a = _flash("fox", main, 0, main, wf, vt, 0, (q_aug, k_aug), hf)

    assert rel_bias.shape[1] == CHUNK + REL_CLIP
    right = CK_EXT - rel_bias.shape[1] - (CHUNK + 1)
    ext = jnp.pad(rel_bias * LOG2E, ((0, 0), (CHUNK + 1, right)), mode="edge")
    o_b = _chunk_attn(main, 2 * wf, 2 * wf + wc, vt, wf, ext, hc)

    y = _mlp(x.reshape(B * S, D), o_a.reshape(B * S, wf), o_b.reshape(B * S, wc),
             w_out.astype(BF16), g_mlp, w_up.astype(BF16), w_down.astype(BF16), g_final,
             final_norm)
    return y.reshape(B, S, D)


def _odd_layer(x, positions, g_mix, w_in, q_norm, kv_norm, w_uq, w_ukv, w_out, g_mlp, w_up,
               w_down, g_final, final_norm):
    B, S, D = x.shape
    hm = w_ukv.shape[1] // (NOPE_DIM + HEAD_DIM)
    ws = w_in.shape[1] - Q_LORA - KV_LORA - ROPE_DIM
    hs = (ws // 3) // HEAD_DIM
    wsb = hs * HEAD_DIM
    o = np.cumsum([0, wsb, wsb, wsb, Q_LORA, KV_LORA, ROPE_DIM])
    qc, kc, vc, w_cq, w_ckv, w_kr = [w_in[:, o[n]:o[n + 1]] for n in range(6)]
    wm = jnp.concatenate([qc * HEAD_DIM ** -0.5, kc], axis=1).astype(BF16)
    wa = jnp.concatenate([w_cq, w_ckv, _rope_slab(w_kr), _rope_slab(_rot_cols(w_kr))],
                         axis=1).astype(BF16)
    main, vt, aux = _inproj(x, g_mix, wm, vc.T.astype(BF16), wa)
    o_c = _sb_attn(main, 0, wsb, vt, 0, hs)

    dq = NOPE_DIM + ROPE_DIM
    wuq3 = w_uq.reshape(Q_LORA, hm, dq)
    nope, ropew = wuq3[:, :, :NOPE_DIM], wuq3[:, :, NOPE_DIM:]
    zq = jnp.zeros((Q_LORA, hm, LANES - dq), w_uq.dtype)
    wuq = jnp.concatenate([nope, ropew, zq], axis=2).reshape(Q_LORA, hm * LANES).astype(BF16)
    half = ROPE_DIM // 2
    ropr = jnp.concatenate([-ropew[:, :, half:], ropew[:, :, :half]], axis=2)
    wuqr = jnp.concatenate([jnp.zeros_like(nope), ropr, zq], axis=2)
    wuqr = wuqr.reshape(Q_LORA, hm * LANES).astype(BF16)
    wkv3 = w_ukv.reshape(KV_LORA, hm, NOPE_DIM + HEAD_DIM)
    wk = jnp.concatenate([wkv3[:, :, :NOPE_DIM],
                          jnp.zeros((KV_LORA, hm, LANES - NOPE_DIM), w_ukv.dtype)], axis=2)
    wk = wk.reshape(KV_LORA, hm * LANES).astype(BF16)
    wv_t = wkv3[:, :, NOPE_DIM:].reshape(KV_LORA, hm * HEAD_DIM).T.astype(BF16)
    freqs = (ROPE_THETA ** (-jnp.arange(half, dtype=F32) / half))
    invf = jnp.tile(freqs, 2 * LANES // ROPE_DIM).reshape(1, LANES)
    groups = LANES // ROPE_DIM
    pos = positions.astype(F32).reshape(B, S // MLA_TM, groups, MLA_TM // groups)
    pos = jnp.repeat(jnp.swapaxes(pos, 2, 3), ROPE_DIM, axis=-1)
    qm, km, vtm = _mla_prep(aux, pos, invf, q_norm.reshape(1, Q_LORA),
                            kv_norm.reshape(1, KV_LORA), wuq, wuqr, wk, wv_t,
                            dq ** -0.5 * LOG2E)
    o_d = _flash("mla", qm, 0, km, 0, vtm, 0, None, hm)

    y = _mlp(x.reshape(B * S, D), o_c.reshape(B * S, wsb), o_d.reshape(B * S, hm * HEAD_DIM),
             w_out.astype(BF16), g_mlp, w_up.astype(BF16), w_down.astype(BF16), g_final,
             final_norm)
    return y.reshape(B, S, D)


def kernel(x, positions, norm_mix, norm_mlp, norm_final, w_in_ab, b_forget, rel_bias, w_out_ab,
           w_in_cd, q_norm, kv_norm, w_uq, w_ukv, w_out_cd, w_up, w_down):
    depth = norm_mix.shape[0]
    for layer in range(depth):
        last = layer == depth - 1
        if layer % 2 == 0:
            e = layer // 2
            x = _even_layer(x, norm_mix[layer], w_in_ab[e], b_forget[e], rel_bias[e], w_out_ab[e],
                            norm_mlp[layer], w_up[layer], w_down[layer], norm_final, last)
        else:
            o = layer // 2
            x = _odd_layer(x, positions, norm_mix[layer], w_in_cd[o], q_norm[o], kv_norm[o],
                           w_uq[o], w_ukv[o], w_out_cd[o], norm_mlp[layer], w_up[layer],
                           w_down[layer], norm_final, last)
    return x
```

```python
import functools
import math

import numpy as np
import jax
import jax.numpy as jnp
from jax import lax
from jax.experimental import pallas as pl
from jax.experimental.pallas import tpu as pltpu

F32 = jnp.float32
BF16 = jnp.bfloat16

EPS = 1e-6
HEAD_DIM = 64
CHUNK = 64
N_LEFT_CHUNKS = 8
REL_CLIP = 256
ROPE_DIM = 32
NOPE_DIM = 64
ROPE_THETA = 10000.0
Q_LORA = 384
KV_LORA = 256

LANES = 128
VT_BLK = LANES
SUB = LANES
FLASH_HP = 8
FLASH_BQ = 512
FLASH_SUB = 256
FIXED_WIDTHS = (4, 2, 1)
SAFE_GAP = 80.0
NEG = -1e30
LOG2E = math.log2(math.e)
SB_ZERO_LOG = -104.0
VMEM_LIMIT = 56 * 1024 * 1024

_NT = (((1,), (1,)), ((), ()))


def _cparams(sem):
    return pltpu.CompilerParams(dimension_semantics=sem, vmem_limit_bytes=VMEM_LIMIT)


def _rms_bf16(x, g):
    ms = jnp.mean(x * x, axis=-1, keepdims=True)
    return (x * lax.rsqrt(ms + EPS) * g).astype(BF16)


def _store_vt(ovt_ref, vt, row0):
    rows, tm = vt.shape
    for c in range(tm // VT_BLK):
        ovt_ref[c, row0:row0 + rows, :] = vt[:, c * VT_BLK:(c + 1) * VT_BLK]


def _inproj_kernel(x_ref, g_ref, wm_ref, wvt_ref, wa_ref, om_ref, ovt_ref, oa_ref):
    h = _rms_bf16(x_ref[...], g_ref[...])
    nm = om_ref.shape[-1]
    for c in range(0, nm, 512):
        om_ref[:, c:c + 512] = jnp.dot(
            h, wm_ref[:, c:c + 512], preferred_element_type=F32).astype(BF16)
    nv = wvt_ref.shape[0]
    for r in range(0, nv, 256):
        vt = lax.dot_general(wvt_ref[r:r + 256, :], h, _NT,
                             preferred_element_type=F32).astype(BF16)
        _store_vt(ovt_ref, vt, r)
    oa_ref[...] = jnp.dot(h, wa_ref[...], preferred_element_type=F32)


def _inproj(x, g, wm, wvt, wa, tm=512):
    B, S, D = x.shape
    nm, nv, na = wm.shape[1], wvt.shape[0], wa.shape[1]
    return pl.pallas_call(
        _inproj_kernel,
        grid=(B, S // tm),
        in_specs=[
            pl.BlockSpec((None, tm, D), lambda b, i: (b, i, 0)),
            pl.BlockSpec((1, D), lambda b, i: (0, 0)),
            pl.BlockSpec((D, nm), lambda b, i: (0, 0)),
            pl.BlockSpec((nv, D), lambda b, i: (0, 0)),
            pl.BlockSpec((D, na), lambda b, i: (0, 0)),
        ],
        out_specs=[
            pl.BlockSpec((None, tm, nm), lambda b, i: (b, i, 0)),
            pl.BlockSpec((None, tm // VT_BLK, nv, VT_BLK), lambda b, i: (b, i, 0, 0)),
            pl.BlockSpec((None, tm, na), lambda b, i: (b, i, 0)),
        ],
        out_shape=[
            jax.ShapeDtypeStruct((B, S, nm), BF16),
            jax.ShapeDtypeStruct((B, S // VT_BLK, nv, VT_BLK), BF16),
            jax.ShapeDtypeStruct((B, S, na), F32),
        ],
        compiler_params=_cparams(("arbitrary", "arbitrary")),
        name="inproj",
    )(x, g.reshape(1, D), wm, wvt, wa)


def _split3(x):
    hi = x.astype(BF16)
    r = x - hi.astype(F32)
    mid = r.astype(BF16)
    lo = (r - mid.astype(F32)).astype(BF16)
    return hi, mid, lo


AUG_W = 8


def _logcum_kernel(fa_ref, b_ref, pq_ref, pk_ref, oneq_ref, onek_ref, oq_ref, ok_ref, carry_ref):
    @pl.when(pl.program_id(1) == 0)
    def _():
        carry_ref[...] = jnp.zeros_like(carry_ref)

    z = fa_ref[...] + b_ref[...]
    lf = jnp.minimum(z, 0.0) - jnp.log(1.0 + jnp.exp(-jnp.abs(z)))
    tc = lf.shape[0]
    r = lax.broadcasted_iota(jnp.int32, (tc, tc), 0)
    c = lax.broadcasted_iota(jnp.int32, (tc, tc), 1)
    tri = jnp.where(r >= c, 1.0, 0.0).astype(BF16)
    cs = carry_ref[...]
    for part in _split3(lf):
        cs = cs + jnp.dot(tri, part, preferred_element_type=F32)
    carry_ref[...] = cs[tc - 1:tc, :]
    qa, ka = oneq_ref[...], onek_ref[...]
    for n, part in enumerate(_split3(cs * LOG2E)):
        qa = qa + jnp.dot(part, pq_ref[n], preferred_element_type=F32)
        ka = ka + jnp.dot(part, pk_ref[n], preferred_element_type=F32)
    oq_ref[...] = qa.astype(BF16)
    ok_ref[...] = ka.astype(BF16)


def _logcum(fa, bias, n_heads, tc=512):
    B, S, W = fa.shape
    na = LANES
    pq = np.zeros((3, W, na), np.float32)
    pk = np.zeros((3, W, na), np.float32)
    oneq = np.zeros((1, na), np.float32)
    onek = np.zeros((1, na), np.float32)
    for h in range(n_heads):
        base = h * AUG_W
        for n in range(3):
            pq[n, h, base + n] = 1.0
            pk[n, h, base + 3 + n] = -1.0
        oneq[0, base + 3:base + 6] = 1.0
        onek[0, base:base + 3] = 1.0
    const = lambda a: pl.BlockSpec(a.shape, lambda b, i: (0,) * a.ndim)
    args = [jnp.asarray(pq, BF16), jnp.asarray(pk, BF16), jnp.asarray(oneq), jnp.asarray(onek)]
    return pl.pallas_call(
        _logcum_kernel,
        grid=(B, S // tc),
        in_specs=[pl.BlockSpec((None, tc, W), lambda b, i: (b, i, 0)),
                  pl.BlockSpec((1, W), lambda b, i: (0, 0))] + [const(a) for a in args],
        out_specs=[pl.BlockSpec((None, tc, na), lambda b, i: (b, i, 0))] * 2,
        out_shape=[jax.ShapeDtypeStruct((B, S, na), BF16)] * 2,
        scratch_shapes=[pltpu.VMEM((1, W), F32)],
        compiler_params=_cparams(("arbitrary", "arbitrary")),
        name="logcum",
    )(fa, bias, *args)


def _pair_mask_q(q2, j):
    lane = lax.broadcasted_iota(jnp.int32, q2.shape, 1)
    keep = (lane >= HEAD_DIM * j) & (lane < HEAD_DIM * (j + 1))
    return jnp.where(keep, q2, jnp.zeros_like(q2))


ONES_ROWS = 16


def _softmax_step(tiles, vts, carry, tile_max=None):
    m, acc = carry
    if tile_max is not None:
        m_new = jnp.maximum(m, tile_max)
    else:
        m_new = m
        for tile in tiles:
            m_new = jnp.maximum(m_new, jnp.max(tile(), axis=0, keepdims=True))
    alpha = jnp.exp2(m - m_new)
    pv = None
    for tile, vt in zip(tiles, vts):
        p = jnp.exp2(tile() - m_new).astype(BF16)
        vt1 = jnp.concatenate([vt, jnp.ones((ONES_ROWS, vt.shape[1]), BF16)], axis=0)
        d = jnp.dot(vt1, p, preferred_element_type=F32)
        pv = d if pv is None else pv + d
    return m_new, alpha * acc + pv


def _softmax_init(bq):
    return (jnp.full((1, bq), NEG, F32), jnp.zeros((HEAD_DIM + ONES_ROWS, bq), F32))


def _softmax_out(carry):
    _, acc = carry
    return acc[0:HEAD_DIM] / acc[HEAD_DIM:HEAD_DIM + 1]


def _store_heads(o_ref, outs):
    oT = jnp.concatenate(outs, axis=0)
    o_ref[...] = oT.T.astype(o_ref.dtype)


def _flash_kernel(*refs, mode, hp, bq):
    if mode == "fox":
        q_ref, k_ref, vt_ref, qaug_ref, kaug_ref, o_ref = refs[:6]
    else:
        q_ref, k_ref, vt_ref, o_ref = refs[:4]
    sa_ref, sb_ref, ma_ref, mb_ref, qt_ref, kn_ref = refs[-6:]
    qs = pl.program_id(2) * bq
    sub = FLASH_SUB
    row = lax.broadcasted_iota(jnp.int32, (sub, bq), 0)
    col = lax.broadcasted_iota(jnp.int32, (sub, bq), 1)
    if mode == "fox":
        kcols = [slice(LANES * (h // 2), LANES * (h // 2 + 1)) for h in range(hp)]
        lane = lax.broadcasted_iota(jnp.int32, (bq, LANES), 1)
        qa = qaug_ref[...]
        qms = []
        for h in range(hp):
            first = AUG_W * (pl.program_id(1) * hp + h)
            own = (lane >= first) & (lane < first + AUG_W)
            qms.append(jnp.concatenate(
                [_pair_mask_q(q_ref[:, kcols[h]], h % 2),
                 jnp.where(own, qa, jnp.zeros_like(qa))], axis=1))
    else:
        kcols = [slice(LANES * h, LANES * (h + 1)) for h in range(hp)]
        qms = [q_ref[:, kcols[h]] for h in range(hp)]
    for h in range(hp):
        qt_ref[h] = qms[h].T

    @pl.when(pl.program_id(2) == 0)
    def _():
        klane = lax.broadcasted_iota(jnp.int32, (1, LANES), 1)
        for h in range(hp):
            kabs = jnp.max(jnp.abs(k_ref[:, kcols[h]].astype(F32)), axis=0, keepdims=True)
            sq = kabs * kabs
            if mode == "fox":
                sq = jnp.where((klane >= HEAD_DIM * (h % 2)) & (klane < HEAD_DIM * (h % 2 + 1)),
                               sq, 0.0)
            kn_ref[h] = jnp.broadcast_to(jnp.sqrt(jnp.sum(sq, axis=1, keepdims=True)), (1, bq))

    def scores(sb, h, masked, col0=0):
        ks = pl.multiple_of(sb * sub, sub)
        k = k_ref[pl.ds(ks, sub), kcols[h]]
        if mode == "fox":
            k = jnp.concatenate([k, kaug_ref[pl.ds(ks, sub), :]], axis=1)
        sT = jnp.dot(k, qt_ref[h, :, col0:], preferred_element_type=F32)
        if masked and mode == "fox":
            sT = jnp.where((ks + row <= qs + col)[:, col0:], sT, NEG)
        elif masked:
            sT = jnp.where((((ks + row) >> 6) <= ((qs + col) >> 6))[:, col0:], sT, NEG)
        return sT

    nsub = bq // sub
    nvt = sub // VT_BLK

    def produce(buf, sb0, masked, h):
        s_buf, m_buf = buf
        tile_max = None
        for c in range(nsub):
            col0 = c * sub if masked else 0
            sT = scores(sb0 + c, h, masked, col0)
            cm = jnp.max(sT, axis=0, keepdims=True)
            if col0:
                s_buf[h, c, :, 0:col0] = jnp.full((sub, col0), NEG, F32)
                cm = jnp.concatenate([jnp.full((1, col0), NEG, F32), cm], axis=1)
            s_buf[h, c, :, col0:] = sT
            tile_max = cm if tile_max is None else jnp.maximum(tile_max, cm)
        m_buf[h] = tile_max

    def consume(buf, sb0, carry, h):
        s_buf, m_buf = buf
        tiles = [lambda c=c: s_buf[h, c] for c in range(nsub)]
        vts = [jnp.concatenate([vt_ref[(sb0 + c) * nvt + v, HEAD_DIM * h:HEAD_DIM * (h + 1), :]
                                for v in range(nvt)], axis=1) for c in range(nsub)]
        return _softmax_step(tiles, vts, carry, tile_max=m_buf[h])

    def stage(cur, cur_sb, nxt, nxt_sb, carries):
        if nxt is not None:
            for h in range(hp):
                produce(nxt, nxt_sb, False, h)
        return tuple(consume(cur, cur_sb, carries[h], h) for h in range(hp))

    n = pl.program_id(2)
    diag_sb = qs // sub
    buf_a, buf_b = (sa_ref, ma_ref), (sb_ref, mb_ref)
    for h in range(hp):
        produce(buf_a, diag_sb, True, h)

    def pair(j, carries):
        carries = stage(buf_a, jnp.where(j == 0, diag_sb, (2 * j - 1) * nsub),
                        buf_b, 2 * j * nsub, carries)
        return stage(buf_b, 2 * j * nsub,
                     buf_a, jnp.minimum(2 * j + 1, n - 1) * nsub, carries)

    def online(_):
        carries = tuple(_softmax_init(bq) for _ in range(hp))
        carries = lax.fori_loop(0, (n + 1) // 2, pair, carries)
        carries = lax.cond(
            n % 2 == 0,
            lambda c: stage(buf_a, jnp.where(n == 0, diag_sb, (n - 1) * nsub), None, None, c),
            lambda c: c, carries)
        return jnp.concatenate([_softmax_out(c) for c in carries], axis=0)

    refs_ = []
    gap = None
    for h in range(hp):
        qf = qt_ref[h, 0:LANES, :].astype(F32)
        bound = jnp.sqrt(jnp.sum(qf * qf, axis=0, keepdims=True)) * kn_ref[h] * 1.01 + 1e-3
        refs_.append(bound)
        g = jnp.max(bound - ma_ref[h])
        gap = g if gap is None else jnp.maximum(gap, g)

    def weigh_add(acc, h, s, sb0, nblk, col0=0):
        p = jnp.exp2(s - refs_[h][:, col0:]).astype(BF16)
        vt = jnp.concatenate([vt_ref[sb0 * nvt + v, HEAD_DIM * h:HEAD_DIM * (h + 1), :]
                              for v in range(nblk * nvt)], axis=1)
        vt1 = jnp.concatenate([vt, jnp.ones((ONES_ROWS, nblk * sub), BF16)], axis=0)
        d = jnp.dot(vt1, p, preferred_element_type=F32)
        if col0:
            d = jnp.concatenate([jnp.zeros((d.shape[0], col0), F32), d], axis=1)
        return acc + d

    def fixed_reference(_):
        def run(kb, accs, nblk):
            ks = pl.multiple_of(kb * sub, sub)
            ss = []
            for h in range(hp):
                k = k_ref[pl.ds(ks, nblk * sub), kcols[h]]
                if mode == "fox":
                    k = jnp.concatenate([k, kaug_ref[pl.ds(ks, nblk * sub), :]], axis=1)
                ss.append(jnp.dot(k, qt_ref[h], preferred_element_type=F32))
            return tuple(weigh_add(accs[h], h, ss[h], kb, nblk) for h in range(hp))

        accs = tuple(jnp.zeros((HEAD_DIM + ONES_ROWS, bq), F32) for _ in range(hp))
        done = 0
        for width in FIXED_WIDTHS:
            trips = (diag_sb - done) // width
            accs = lax.fori_loop(0, trips,
                                 lambda j, a, done=done, width=width: run(done + j * width, a, width),
                                 accs)
            done = done + trips * width
        for c in range(nsub):
            accs = [weigh_add(accs[h], h, sa_ref[h, c, :, c * sub:], diag_sb + c, 1, c * sub)
                    for h in range(hp)]
        return jnp.concatenate([a[0:HEAD_DIM] / a[HEAD_DIM:HEAD_DIM + 1] for a in accs], axis=0)

    oT = lax.cond(gap <= SAFE_GAP, fixed_reference, online, None)
    o_ref[...] = oT.T.astype(o_ref.dtype)


def _flash(mode, q_arr, q_col0, k_arr, k_col0, vt_arr, vt_row0, extra, n_heads,
           hp=FLASH_HP, bq=FLASH_BQ):
    B, S, _ = q_arr.shape
    qw = (HEAD_DIM if mode == "fox" else LANES) * hp
    vw = HEAD_DIM * hp
    in_specs = [
        pl.BlockSpec((None, bq, qw), lambda b, g, i: (b, i, q_col0 // qw + g)),
        pl.BlockSpec((None, S, qw), lambda b, g, i: (b, 0, k_col0 // qw + g)),
        pl.BlockSpec((None, S // VT_BLK, vw, VT_BLK),
                     lambda b, g, i: (b, 0, vt_row0 // vw + g, 0)),
    ]
    args = [q_arr, k_arr, vt_arr]
    if mode == "fox":
        q_aug, k_aug = extra
        in_specs += [
            pl.BlockSpec((None, bq, LANES), lambda b, g, i: (b, i, 0)),
            pl.BlockSpec((None, S, LANES), lambda b, g, i: (b, 0, 0)),
        ]
        args += [q_aug, k_aug]
    return pl.pallas_call(
        functools.partial(_flash_kernel, mode=mode, hp=hp, bq=bq),
        grid=(B, n_heads // hp, S // bq),
        in_specs=in_specs,
        out_specs=pl.BlockSpec((None, bq, vw), lambda b, g, i: (b, i, g)),
        out_shape=jax.ShapeDtypeStruct((B, S, n_heads * HEAD_DIM), BF16),
        scratch_shapes=([pltpu.VMEM((hp, bq // FLASH_SUB, FLASH_SUB, bq), F32)] * 2
                        + [pltpu.VMEM((hp, 1, bq), F32)] * 2
                        + [pltpu.VMEM((hp, 2 * LANES if mode == "fox" else LANES, bq), BF16),
                           pltpu.VMEM((hp, 1, bq), F32)]),
        compiler_params=_cparams(("arbitrary", "arbitrary", "arbitrary")),
        name="flash_" + mode,
    )(*args)


CK_B = 2 * CHUNK
CK_NW = N_LEFT_CHUNKS * CHUNK // CK_B + 1
CK_EXT = (CK_NW + 1) * CK_B


def _chunk_kernel(q_ref, k_ref, vt_ref, ext_ref, o_ref, tab_ref, s_ref, *, hp, nq):
    i = pl.program_id(1)

    @pl.when(i == 0)
    def _():
        jj = lax.broadcasted_iota(jnp.int32, (CK_B, CK_B), 0)
        rr = lax.broadcasted_iota(jnp.int32, (CK_B, CK_B), 1)
        for h in range(hp):
            for w in range(CK_NW):
                a = (CK_NW - 1 - w) * CK_B
                g = jnp.broadcast_to(ext_ref[h:h + 1, a:a + 2 * CK_B], (CK_B, 2 * CK_B))
                t = pltpu.roll(g, CK_B, 1, stride=1, stride_axis=0)[:, :CK_B]
                if w == 0:
                    t = jnp.where((rr >= CHUNK) & (jj < CHUNK), NEG, t)
                if w == CK_NW - 1:
                    t = jnp.where((rr < CHUNK) & (jj >= CHUNK), NEG, t)
                tab_ref[h, w * CK_B:(w + 1) * CK_B, :] = t

    kcols = [slice(LANES * (h // 2), LANES * (h // 2 + 1)) for h in range(hp)]
    firsts = [i * nq + u - (CK_NW - 1) for u in range(nq)]

    def finish():
        for u in range(nq):
            kbc = [jnp.maximum(firsts[u] + w, 0) for w in range(CK_NW)]
            outs = []
            for h in range(hp):
                vt = jnp.concatenate([vt_ref[kbc[w], HEAD_DIM * h:HEAD_DIM * (h + 1), :]
                                      for w in range(CK_NW)], axis=1)
                outs.append(_softmax_out(_softmax_step([lambda u=u, h=h: s_ref[u, h]], [vt],
                                                       _softmax_init(CK_B))))
            oT = jnp.concatenate(outs, axis=0)
            o_ref[u * CK_B:(u + 1) * CK_B, :] = oT.T.astype(o_ref.dtype)

    def pair_scores(u, p, ks, nrows):
        q2 = q_ref[u * CK_B:(u + 1) * CK_B, kcols[2 * p]]
        qq = jnp.concatenate([_pair_mask_q(q2, 0), _pair_mask_q(q2, 1)], axis=0)
        return lax.dot_general(k_ref[pl.ds(ks, nrows), kcols[2 * p]], qq, _NT,
                               preferred_element_type=F32)

    @pl.when(firsts[0] >= 0)
    def _():
        for u in range(nq):
            ks = pl.multiple_of(firsts[u] * CK_B, CK_B)
            for p in range(hp // 2):
                sT = pair_scores(u, p, ks, CK_NW * CK_B)
                for j in range(2):
                    s_ref[u, 2 * p + j] = sT[:, j * CK_B:(j + 1) * CK_B] + tab_ref[2 * p + j]
        finish()

    @pl.when(firsts[0] < 0)
    def _():
        for u in range(nq):
            for p in range(hp // 2):
                for w in range(CK_NW):
                    rows = slice(w * CK_B, (w + 1) * CK_B)
                    ks = pl.multiple_of(jnp.maximum(firsts[u] + w, 0) * CK_B, CK_B)
                    sT = pair_scores(u, p, ks, CK_B)
                    for j in range(2):
                        s_ref[u, 2 * p + j, rows, :] = jnp.where(
                            firsts[u] + w >= 0,
                            sT[:, j * CK_B:(j + 1) * CK_B] + tab_ref[2 * p + j, rows, :], NEG)
        finish()


def _chunk_attn(main, q_col0, k_col0, vt_arr, vt_row0, ext, n_heads, nq=4):
    B, S, _ = main.shape
    hp = n_heads
    qw, vw = HEAD_DIM * hp, HEAD_DIM * hp
    return pl.pallas_call(
        functools.partial(_chunk_kernel, hp=hp, nq=nq),
        grid=(B, S // (nq * CK_B)),
        in_specs=[
            pl.BlockSpec((None, nq * CK_B, qw), lambda b, i: (b, i, q_col0 // qw)),
            pl.BlockSpec((None, S, qw), lambda b, i: (b, 0, k_col0 // qw)),
            pl.BlockSpec((None, S // VT_BLK, vw, VT_BLK), lambda b, i: (b, 0, vt_row0 // vw, 0)),
            pl.BlockSpec((hp, CK_EXT), lambda b, i: (0, 0)),
        ],
        out_specs=pl.BlockSpec((None, nq * CK_B, vw), lambda b, i: (b, i, 0)),
        out_shape=jax.ShapeDtypeStruct((B, S, n_heads * HEAD_DIM), BF16),
        scratch_shapes=[pltpu.VMEM((hp, CK_NW * CK_B, CK_B), F32),
                        pltpu.VMEM((nq, hp, CK_NW * CK_B, CK_B), F32)],
        compiler_params=_cparams(("arbitrary", "arbitrary")),
        name="chunk_attn",
    )(main, main, vt_arr, ext)


def _sb_kernel(q_ref, k_ref, vt_ref, o_ref, z_ref, lb_ref, sfx_ref, *, hp, bq):
    qs = pl.program_id(1) * bq
    nsub = bq // SUB
    row = lax.broadcasted_iota(jnp.int32, (SUB, bq), 0)
    col = lax.broadcasted_iota(jnp.int32, (SUB, bq), 1)
    ur = lax.broadcasted_iota(jnp.int32, (SUB, 2 * SUB), 0)
    uc = lax.broadcasted_iota(jnp.int32, (SUB, 2 * SUB), 1) & (SUB - 1)
    upper2 = jnp.where(uc > ur, 1.0, 0.0).astype(BF16)
    kcols = [slice(LANES * (h // 2), LANES * (h // 2 + 1)) for h in range(hp)]
    qms = [_pair_mask_q(q_ref[:, kcols[h]], h % 2) for h in range(hp)]

    def step(kb, carries, masked):
        ks = pl.multiple_of(kb * bq, bq)
        for h in range(hp):
            z_ref[h] = lax.dot_general(k_ref[pl.ds(ks, bq), kcols[h]], qms[h], _NT,
                                       preferred_element_type=F32)
        first_col = [c * SUB if masked else 0 for c in range(nsub)]

        def widen(x, c):
            if first_col[c] == 0:
                return x
            return jnp.concatenate([jnp.zeros((x.shape[0], first_col[c]), x.dtype), x], axis=1)

        totals = []
        for h in range(hp):
            tot = []
            for c in range(nsub):
                rows, cols = slice(c * SUB, (c + 1) * SUB), slice(first_col[c], bq)
                z = z_ref[h, rows, cols]
                l1 = jnp.log(1.0 + jnp.exp(-jnp.abs(z)))
                log_beta = jnp.minimum(z, 0.0) - l1
                log_keep = log_beta - z
                if masked:
                    valid = (ks + c * SUB + row < qs + col)[:, cols]
                    log_keep = jnp.where(valid, log_keep, 0.0)
                lb_ref[h, rows, cols] = log_beta
                hi = log_keep.astype(BF16)
                lo = (log_keep - hi.astype(F32)).astype(BF16)
                sfx = jnp.dot(upper2, jnp.concatenate([hi, lo], axis=0),
                              preferred_element_type=F32)
                sfx_ref[h, rows, cols] = sfx
                tot.append(widen(sfx[0:1, :] + log_keep[0:1, :], c))
            totals.append(tot)
        out = []
        for h in range(hp):
            tail, acc = carries[h]
            parts = [None] * nsub
            for c in range(nsub - 1, -1, -1):
                rows, cols = slice(c * SUB, (c + 1) * SUB), slice(first_col[c], bq)
                a = jnp.exp(lb_ref[h, rows, cols] + sfx_ref[h, rows, cols] + tail[:, cols])
                if masked:
                    a = jnp.where((ks + c * SUB + row < qs + col)[:, cols], a, 0.0)
                parts[c] = a.astype(BF16)
                tail = tail + totals[h][c]
            vts = [vt_ref[kb * nsub + c, HEAD_DIM * h:HEAD_DIM * (h + 1), :] for c in range(nsub)]
            if masked:
                for c in range(nsub):
                    acc = acc + widen(jnp.dot(vts[c], parts[c], preferred_element_type=F32), c)
            else:
                acc = acc + jnp.dot(jnp.concatenate(vts, axis=1), jnp.concatenate(parts, axis=0),
                                    preferred_element_type=F32)
            out.append((tail, acc))
        return tuple(out)

    n_full = qs // bq
    carries = tuple((jnp.zeros((1, bq), F32), jnp.zeros((HEAD_DIM, bq), F32))
                    for _ in range(hp))
    carries = step(n_full, carries, True)

    def cond(state):
        kb, carries = state
        tail_max = carries[0][0]
        for h in range(1, hp):
            tail_max = jnp.maximum(tail_max, carries[h][0])
        return (kb >= 0) & (jnp.max(tail_max) > SB_ZERO_LOG)

    def body(state):
        kb, carries = state
        return kb - 1, step(kb, carries, False)

    _, carries = lax.while_loop(cond, body, (n_full - 1, carries))
    _store_heads(o_ref, [acc for (_, acc) in carries])


def _sb_attn(main, q_col0, k_col0, vt_arr, vt_row0, n_heads, bq=256):
    B, S, _ = main.shape
    hp = n_heads
    qw = HEAD_DIM * hp
    return pl.pallas_call(
        functools.partial(_sb_kernel, hp=hp, bq=bq),
        grid=(B, S // bq),
        in_specs=[
            pl.BlockSpec((None, bq, qw), lambda b, i: (b, i, q_col0 // qw)),
            pl.BlockSpec((None, S, qw), lambda b, i: (b, 0, k_col0 // qw)),
            pl.BlockSpec((None, S // VT_BLK, qw, VT_BLK), lambda b, i: (b, 0, vt_row0 // qw, 0)),
        ],
        out_specs=pl.BlockSpec((None, bq, qw), lambda b, i: (b, i, 0)),
        out_shape=jax.ShapeDtypeStruct((B, S, n_heads * HEAD_DIM), BF16),
        scratch_shapes=[pltpu.VMEM((hp, bq, bq), F32)] * 3,
        compiler_params=_cparams(("arbitrary", "arbitrary")),
        name="sb_attn",
    )(main, main, vt_arr)


def _mla_prep_kernel(aux_ref, pos_ref, invf_ref, qn_ref, kvn_ref, wuq_ref, wuqr_ref,
                     wk_ref, wvt_ref, oq_ref, ok_ref, ovt_ref, *, q_scale):
    ang = pos_ref[...] * invf_ref[...]
    cos4, sin4 = jnp.cos(ang), jnp.sin(ang)
    lane = lax.broadcasted_iota(jnp.int32, ang.shape, 1)
    rotary = (lane >= NOPE_DIM) & (lane < NOPE_DIM + ROPE_DIM)
    cos_rows, sin_rows = [], []
    for m in range(LANES // ROPE_DIM):
        shift = (NOPE_DIM - ROPE_DIM * m) % LANES
        cm = cos4 if shift == 0 else pltpu.roll(cos4, shift, 1)
        sm = sin4 if shift == 0 else pltpu.roll(sin4, shift, 1)
        cos_rows.append(jnp.where(rotary, cm, 1.0))
        sin_rows.append(jnp.where(rotary, sm, 0.0))
    cos = jnp.concatenate(cos_rows, axis=0)
    sin = jnp.concatenate(sin_rows, axis=0)
    cq = _rms_bf16(aux_ref[:, 0:Q_LORA], qn_ref[...])
    ckv = _rms_bf16(aux_ref[:, Q_LORA:Q_LORA + KV_LORA], kvn_ref[...])
    o = Q_LORA + KV_LORA
    k_rope = aux_ref[:, o:o + LANES] * cos + aux_ref[:, o + LANES:o + 2 * LANES] * sin
    cos_q, sin_q = cos * q_scale, sin * q_scale
    n_heads = oq_ref.shape[-1] // LANES
    for h in range(0, n_heads, 2):
        cols = slice(h * LANES, (h + 2) * LANES)
        qa = jnp.dot(cq, wuq_ref[:, cols], preferred_element_type=F32)
        qb = jnp.dot(cq, wuqr_ref[:, cols], preferred_element_type=F32)
        kn = jnp.dot(ckv, wk_ref[:, cols], preferred_element_type=F32)
        for d in range(2):
            c1 = slice(d * LANES, (d + 1) * LANES)
            c2 = slice((h + d) * LANES, (h + d + 1) * LANES)
            oq_ref[:, c2] = (qa[:, c1] * cos_q + qb[:, c1] * sin_q).astype(BF16)
            ok_ref[:, c2] = (kn[:, c1] + k_rope).astype(BF16)
    nv = wvt_ref.shape[0]
    for r in range(0, nv, 256):
        vt = lax.dot_general(wvt_ref[r:r + 256, :], ckv, _NT,
                             preferred_element_type=F32).astype(BF16)
        _store_vt(ovt_ref, vt, r)


MLA_TM = 512


def _mla_prep(aux, pos, invf, qn, kvn, wuq, wuqr, wk, wvt, q_scale, tm=MLA_TM):
    B, S, na = aux.shape
    nq, nv = wuq.shape[1], wvt.shape[0]
    full = lambda a: pl.BlockSpec(a.shape, lambda b, i: (0,) * a.ndim)
    return pl.pallas_call(
        functools.partial(_mla_prep_kernel, q_scale=q_scale),
        grid=(B, S // tm),
        in_specs=[
            pl.BlockSpec((None, tm, na), lambda b, i: (b, i, 0)),
            pl.BlockSpec((None, None) + pos.shape[2:], lambda b, i: (b, i, 0, 0)),
            full(invf), full(qn), full(kvn), full(wuq), full(wuqr), full(wk), full(wvt),
        ],
        out_specs=[
            pl.BlockSpec((None, tm, nq), lambda b, i: (b, i, 0)),
            pl.BlockSpec((None, tm, nq), lambda b, i: (b, i, 0)),
            pl.BlockSpec((None, tm // VT_BLK, nv, VT_BLK), lambda b, i: (b, i, 0, 0)),
        ],
        out_shape=[
            jax.ShapeDtypeStruct((B, S, nq), BF16),
            jax.ShapeDtypeStruct((B, S, nq), BF16),
            jax.ShapeDtypeStruct((B, S // VT_BLK, nv, VT_BLK), BF16),
        ],
        compiler_params=_cparams(("arbitrary", "arbitrary")),
        name="mla_prep",
    )(aux, pos, invf, qn, kvn, wuq, wuqr, wk, wvt)


def _mlp_kernel(x_ref, oa_ref, ob_ref, wo_ref, g_ref, wu_ref, wd_ref, gf_ref, out_ref,
                x1_ref, h_ref, a_ref, *, final_norm, tf):
    na = oa_ref.shape[-1]
    x1 = (x_ref[...]
          + jnp.dot(oa_ref[...], wo_ref[0:na, :], preferred_element_type=F32)
          + jnp.dot(ob_ref[...], wo_ref[na:, :], preferred_element_type=F32))
    x1_ref[...] = x1
    h_ref[...] = _rms_bf16(x1, g_ref[...])
    for f in range(0, wu_ref.shape[1], tf):
        u = jnp.dot(h_ref[...], wu_ref[:, f:f + tf], preferred_element_type=F32)
        a_ref[:, f:f + tf] = jnp.square(jnp.maximum(u, 0.0)).astype(BF16)
    y = x1_ref[...] + jnp.dot(a_ref[...], wd_ref[...], preferred_element_type=F32)
    if final_norm:
        ms = jnp.mean(y * y, axis=-1, keepdims=True)
        y = y * lax.rsqrt(ms + EPS) * gf_ref[...]
    out_ref[...] = y


def _mlp(x2, oa, ob, wo, g, wu, wd, gf, final_norm, tm=512, tf=1024):
    T, D = x2.shape
    F = wu.shape[1]
    na, nb = oa.shape[1], ob.shape[1]
    const = lambda shape: pl.BlockSpec(shape, lambda i: (0, 0), pipeline_mode=pl.Buffered(1))
    return pl.pallas_call(
        functools.partial(_mlp_kernel, final_norm=final_norm, tf=tf),
        grid=(T // tm,),
        in_specs=[
            pl.BlockSpec((tm, D), lambda i: (i, 0)),
            pl.BlockSpec((tm, na), lambda i: (i, 0)),
            pl.BlockSpec((tm, nb), lambda i: (i, 0)),
            const((na + nb, D)),
            const((1, D)),
            const((D, F)),
            const((F, D)),
            const((1, D)),
        ],
        out_specs=pl.BlockSpec((tm, D), lambda i: (i, 0)),
        out_shape=jax.ShapeDtypeStruct((T, D), F32),
        scratch_shapes=[pltpu.VMEM((tm, D), F32), pltpu.VMEM((tm, D), BF16),
                        pltpu.VMEM((tm, F), BF16)],
        compiler_params=_cparams(("arbitrary",)),
        name="mlp",
    )(x2, oa, ob, wo, g.reshape(1, D), wu, wd, gf.reshape(1, D))


def _pad_cols(w, n):
    return jnp.pad(w, ((0, 0), (0, n - w.shape[1])))


def _rot_cols(w):
    half = ROPE_DIM // 2
    return jnp.concatenate([-w[:, half:], w[:, :half]], axis=1)


def _rope_slab(w):
    z = jnp.zeros((w.shape[0], NOPE_DIM), w.dtype)
    return jnp.concatenate([z, w, jnp.zeros((w.shape[0], LANES - NOPE_DIM - ROPE_DIM), w.dtype)], axis=1)


def _even_layer(x, g_mix, w_in, b_forget, rel_bias, w_out, g_mlp, w_up, w_down, g_final,
                final_norm):
    B, S, D = x.shape
    hf, hc = b_forget.shape[0], rel_bias.shape[0]
    wf, wc = hf * HEAD_DIM, hc * HEAD_DIM
    o = np.cumsum([0, wf, wf, wf, hf, wc, wc, wc])
    qa, ka, va, fa, qb, kb, vb = [w_in[:, o[n]:o[n + 1]] for n in range(7)]
    q_scale = HEAD_DIM ** -0.5 * LOG2E
    wm = jnp.concatenate([qa * q_scale, ka, qb * q_scale, kb], axis=1).astype(BF16)
    wvt = jnp.concatenate([va, vb], axis=1).T.astype(BF16)
    wa = _pad_cols(fa, LANES).astype(BF16)
    main, vt, aux = _inproj(x, g_mix, wm, wvt, wa)

    q_aug, k_aug = _logcum(aux, _pad_cols(b_forget.reshape(1, hf), LANES), hf)
    o_a = _flash("fox", main, 0, main, wf, vt, 0, (q_aug, k_aug), hf)

    assert rel_bias.shape[1] == CHUNK + REL_CLIP
    right = CK_EXT - rel_bias.shape[1] - (CHUNK + 1)
    ext = jnp.pad(rel_bias * LOG2E, ((0, 0), (CHUNK + 1, right)), mode="edge")
    o_b = _chunk_attn(main, 2 * wf, 2 * wf + wc, vt, wf, ext, hc)

    y = _mlp(x.reshape(B * S, D), o_a.reshape(B * S, wf), o_b.reshape(B * S, wc),
             w_out.astype(BF16), g_mlp, w_up.astype(BF16), w_down.astype(BF16), g_final,
             final_norm)
    return y.reshape(B, S, D)


def _odd_layer(x, positions, g_mix, w_in, q_norm, kv_norm, w_uq, w_ukv, w_out, g_mlp, w_up,
               w_down, g_final, final_norm):
    B, S, D = x.shape
    hm = w_ukv.shape[1] // (NOPE_DIM + HEAD_DIM)
    ws = w_in.shape[1] - Q_LORA - KV_LORA - ROPE_DIM
    hs = (ws // 3) // HEAD_DIM
    wsb = hs * HEAD_DIM
    o = np.cumsum([0, wsb, wsb, wsb, Q_LORA, KV_LORA, ROPE_DIM])
    qc, kc, vc, w_cq, w_ckv, w_kr = [w_in[:, o[n]:o[n + 1]] for n in range(6)]
    wm = jnp.concatenate([qc * HEAD_DIM ** -0.5, kc], axis=1).astype(BF16)
    wa = jnp.concatenate([w_cq, w_ckv, _rope_slab(w_kr), _rope_slab(_rot_cols(w_kr))],
                         axis=1).astype(BF16)
    main, vt, aux = _inproj(x, g_mix, wm, vc.T.astype(BF16), wa)
    o_c = _sb_attn(main, 0, wsb, vt, 0, hs)

    dq = NOPE_DIM + ROPE_DIM
    wuq3 = w_uq.reshape(Q_LORA, hm, dq)
    nope, ropew = wuq3[:, :, :NOPE_DIM], wuq3[:, :, NOPE_DIM:]
    zq = jnp.zeros((Q_LORA, hm, LANES - dq), w_uq.dtype)
    wuq = jnp.concatenate([nope, ropew, zq], axis=2).reshape(Q_LORA, hm * LANES).astype(BF16)
    half = ROPE_DIM // 2
    ropr = jnp.concatenate([-ropew[:, :, half:], ropew[:, :, :half]], axis=2)
    wuqr = jnp.concatenate([jnp.zeros_like(nope), ropr, zq], axis=2)
    wuqr = wuqr.reshape(Q_LORA, hm * LANES).astype(BF16)
    wkv3 = w_ukv.reshape(KV_LORA, hm, NOPE_DIM + HEAD_DIM)
    wk = jnp.concatenate([wkv3[:, :, :NOPE_DIM],
                          jnp.zeros((KV_LORA, hm, LANES - NOPE_DIM), w_ukv.dtype)], axis=2)
    wk = wk.reshape(KV_LORA, hm * LANES).astype(BF16)
    wv_t = wkv3[:, :, NOPE_DIM:].reshape(KV_LORA, hm * HEAD_DIM).T.astype(BF16)
    freqs = (ROPE_THETA ** (-jnp.arange(half, dtype=F32) / half))
    invf = jnp.tile(freqs, 2 * LANES // ROPE_DIM).reshape(1, LANES)
    groups = LANES // ROPE_DIM
    pos = positions.astype(F32).reshape(B, S // MLA_TM, groups, MLA_TM // groups)
    pos = jnp.repeat(jnp.swapaxes(pos, 2, 3), ROPE_DIM, axis=-1)
    qm, km, vtm = _mla_prep(aux, pos, invf, q_norm.reshape(1, Q_LORA),
                            kv_norm.reshape(1, KV_LORA), wuq, wuqr, wk, wv_t,
                            dq ** -0.5 * LOG2E)
    o_d = _flash("mla", qm, 0, km, 0, vtm, 0, None, hm)

    y = _mlp(x.reshape(B * S, D), o_c.reshape(B * S, wsb), o_d.reshape(B * S, hm * HEAD_DIM),
             w_out.astype(BF16), g_mlp, w_up.astype(BF16), w_down.astype(BF16), g_final,
             final_norm)
    return y.reshape(B, S, D)


def kernel(x, positions, norm_mix, norm_mlp, norm_final, w_in_ab, b_forget, rel_bias, w_out_ab,
           w_in_cd, q_norm, kv_norm, w_uq, w_ukv, w_out_cd, w_up, w_down):
    depth = norm_mix.shape[0]
    for layer in range(depth):
        last = layer == depth - 1
        if layer % 2 == 0:
            e = layer // 2
            x = _even_layer(x, norm_mix[layer], w_in_ab[e], b_forget[e], rel_bias[e], w_out_ab[e],
                            norm_mlp[layer], w_up[layer], w_down[layer], norm_final, last)
        else:
            o = layer // 2
            x = _odd_layer(x, positions, norm_mix[layer], w_in_cd[o], q_norm[o], kv_norm[o],
                           w_uq[o], w_ukv[o], w_out_cd[o], norm_mlp[layer], w_up[layer],
                           w_down[layer], norm_final, last)
    return x
```

```python
import functools
import math

import numpy as np
import jax
import jax.numpy as jnp
from jax import lax
from jax.experimental import pallas as pl
from jax.experimental.pallas import tpu as pltpu

F32 = jnp.float32
BF16 = jnp.bfloat16

EPS = 1e-6
HEAD_DIM = 64
CHUNK = 64
N_LEFT_CHUNKS = 8
REL_CLIP = 256
ROPE_DIM = 32
NOPE_DIM = 64
ROPE_THETA = 10000.0
Q_LORA = 384
KV_LORA = 256

LANES = 128
VT_BLK = LANES
SUB = LANES
FLASH_HP = 8
FLASH_BQ = 512
FLASH_SUB = 256
FIXED_WIDTHS = (4, 2, 1)
SAFE_GAP = 80.0
NEG = -1e30
LOG2E = math.log2(math.e)
SB_ZERO_LOG = -104.0
VMEM_LIMIT = 56 * 1024 * 1024

_NT = (((1,), (1,)), ((), ()))


def _cparams(sem):
    return pltpu.CompilerParams(dimension_semantics=sem, vmem_limit_bytes=VMEM_LIMIT)


def _rms_bf16(x, g):
    ms = jnp.mean(x * x, axis=-1, keepdims=True)
    return (x * lax.rsqrt(ms + EPS) * g).astype(BF16)


def _store_vt(ovt_ref, vt, row0):
    rows, tm = vt.shape
    for c in range(tm // VT_BLK):
        ovt_ref[c, row0:row0 + rows, :] = vt[:, c * VT_BLK:(c + 1) * VT_BLK]


def _inproj_kernel(x_ref, g_ref, wm_ref, wvt_ref, wa_ref, om_ref, ovt_ref, oa_ref):
    h = _rms_bf16(x_ref[...], g_ref[...])
    nm = om_ref.shape[-1]
    for c in range(0, nm, 512):
        om_ref[:, c:c + 512] = jnp.dot(
            h, wm_ref[:, c:c + 512], preferred_element_type=F32).astype(BF16)
    nv = wvt_ref.shape[0]
    for r in range(0, nv, 256):
        vt = lax.dot_general(wvt_ref[r:r + 256, :], h, _NT,
                             preferred_element_type=F32).astype(BF16)
        _store_vt(ovt_ref, vt, r)
    oa_ref[...] = jnp.dot(h, wa_ref[...], preferred_element_type=F32)


def _inproj(x, g, wm, wvt, wa, tm=1024):
    B, S, D = x.shape
    nm, nv, na = wm.shape[1], wvt.shape[0], wa.shape[1]
    return pl.pallas_call(
        _inproj_kernel,
        grid=(B, S // tm),
        in_specs=[
            pl.BlockSpec((None, tm, D), lambda b, i: (b, i, 0)),
            pl.BlockSpec((1, D), lambda b, i: (0, 0)),
            pl.BlockSpec((D, nm), lambda b, i: (0, 0)),
            pl.BlockSpec((nv, D), lambda b, i: (0, 0)),
            pl.BlockSpec((D, na), lambda b, i: (0, 0)),
        ],
        out_specs=[
            pl.BlockSpec((None, tm, nm), lambda b, i: (b, i, 0)),
            pl.BlockSpec((None, tm // VT_BLK, nv, VT_BLK), lambda b, i: (b, i, 0, 0)),
            pl.BlockSpec((None, tm, na), lambda b, i: (b, i, 0)),
        ],
        out_shape=[
            jax.ShapeDtypeStruct((B, S, nm), BF16),
            jax.ShapeDtypeStruct((B, S // VT_BLK, nv, VT_BLK), BF16),
            jax.ShapeDtypeStruct((B, S, na), F32),
        ],
        compiler_params=_cparams(("arbitrary", "arbitrary")),
        name="inproj",
    )(x, g.reshape(1, D), wm, wvt, wa)


def _split3(x):
    hi = x.astype(BF16)
    r = x - hi.astype(F32)
    mid = r.astype(BF16)
    lo = (r - mid.astype(F32)).astype(BF16)
    return hi, mid, lo


AUG_W = 8


def _logcum_kernel(fa_ref, b_ref, pq_ref, pk_ref, oneq_ref, onek_ref, oq_ref, ok_ref, carry_ref):
    @pl.when(pl.program_id(1) == 0)
    def _():
        carry_ref[...] = jnp.zeros_like(carry_ref)

    z = fa_ref[...] + b_ref[...]
    lf = jnp.minimum(z, 0.0) - jnp.log(1.0 + jnp.exp(-jnp.abs(z)))
    tc = lf.shape[0]
    r = lax.broadcasted_iota(jnp.int32, (tc, tc), 0)
    c = lax.broadcasted_iota(jnp.int32, (tc, tc), 1)
    tri = jnp.where(r >= c, 1.0, 0.0).astype(BF16)
    cs = carry_ref[...]
    for part in _split3(lf):
        cs = cs + jnp.dot(tri, part, preferred_element_type=F32)
    carry_ref[...] = cs[tc - 1:tc, :]
    qa, ka = oneq_ref[...], onek_ref[...]
    for n, part in enumerate(_split3(cs * LOG2E)):
        qa = qa + jnp.dot(part, pq_ref[n], preferred_element_type=F32)
        ka = ka + jnp.dot(part, pk_ref[n], preferred_element_type=F32)
    oq_ref[...] = qa.astype(BF16)
    ok_ref[...] = ka.astype(BF16)


def _logcum(fa, bias, n_heads, tc=512):
    B, S, W = fa.shape
    na = LANES
    pq = np.zeros((3, W, na), np.float32)
    pk = np.zeros((3, W, na), np.float32)
    oneq = np.zeros((1, na), np.float32)
    onek = np.zeros((1, na), np.float32)
    for h in range(n_heads):
        base = h * AUG_W
        for n in range(3):
            pq[n, h, base + n] = 1.0
            pk[n, h, base + 3 + n] = -1.0
        oneq[0, base + 3:base + 6] = 1.0
        onek[0, base:base + 3] = 1.0
    const = lambda a: pl.BlockSpec(a.shape, lambda b, i: (0,) * a.ndim)
    args = [jnp.asarray(pq, BF16), jnp.asarray(pk, BF16), jnp.asarray(oneq), jnp.asarray(onek)]
    return pl.pallas_call(
        _logcum_kernel,
        grid=(B, S // tc),
        in_specs=[pl.BlockSpec((None, tc, W), lambda b, i: (b, i, 0)),
                  pl.BlockSpec((1, W), lambda b, i: (0, 0))] + [const(a) for a in args],
        out_specs=[pl.BlockSpec((None, tc, na), lambda b, i: (b, i, 0))] * 2,
        out_shape=[jax.ShapeDtypeStruct((B, S, na), BF16)] * 2,
        scratch_shapes=[pltpu.VMEM((1, W), F32)],
        compiler_params=_cparams(("arbitrary", "arbitrary")),
        name="logcum",
    )(fa, bias, *args)


def _pair_mask_q(q2, j):
    lane = lax.broadcasted_iota(jnp.int32, q2.shape, 1)
    keep = (lane >= HEAD_DIM * j) & (lane < HEAD_DIM * (j + 1))
    return jnp.where(keep, q2, jnp.zeros_like(q2))


ONES_ROWS = 16


def _softmax_step(tiles, vts, carry, tile_max=None):
    m, acc = carry
    if tile_max is not None:
        m_new = jnp.maximum(m, tile_max)
    else:
        m_new = m
        for tile in tiles:
            m_new = jnp.maximum(m_new, jnp.max(tile(), axis=0, keepdims=True))
    alpha = jnp.exp2(m - m_new)
    pv = None
    for tile, vt in zip(tiles, vts):
        p = jnp.exp2(tile() - m_new).astype(BF16)
        vt1 = jnp.concatenate([vt, jnp.ones((ONES_ROWS, vt.shape[1]), BF16)], axis=0)
        d = jnp.dot(vt1, p, preferred_element_type=F32)
        pv = d if pv is None else pv + d
    return m_new, alpha * acc + pv


def _softmax_init(bq):
    return (jnp.full((1, bq), NEG, F32), jnp.zeros((HEAD_DIM + ONES_ROWS, bq), F32))


def _softmax_out(carry):
    _, acc = carry
    return acc[0:HEAD_DIM] / acc[HEAD_DIM:HEAD_DIM + 1]


def _store_heads(o_ref, outs):
    oT = jnp.concatenate(outs, axis=0)
    o_ref[...] = oT.T.astype(o_ref.dtype)


def _flash_kernel(*refs, mode, hp, bq):
    if mode == "fox":
        q_ref, k_ref, vt_ref, qaug_ref, kaug_ref, o_ref = refs[:6]
    else:
        q_ref, k_ref, vt_ref, o_ref = refs[:4]
    sa_ref, sb_ref, ma_ref, mb_ref, qt_ref, kn_ref = refs[-6:]
    qs = pl.program_id(2) * bq
    sub = FLASH_SUB
    row = lax.broadcasted_iota(jnp.int32, (sub, bq), 0)
    col = lax.broadcasted_iota(jnp.int32, (sub, bq), 1)
    if mode == "fox":
        kcols = [slice(LANES * (h // 2), LANES * (h // 2 + 1)) for h in range(hp)]
        lane = lax.broadcasted_iota(jnp.int32, (bq, LANES), 1)
        qa = qaug_ref[...]
        qms = []
        for h in range(hp):
            first = AUG_W * (pl.program_id(1) * hp + h)
            own = (lane >= first) & (lane < first + AUG_W)
            qms.append(jnp.concatenate(
                [_pair_mask_q(q_ref[:, kcols[h]], h % 2),
                 jnp.where(own, qa, jnp.zeros_like(qa))], axis=1))
    else:
        kcols = [slice(LANES * h, LANES * (h + 1)) for h in range(hp)]
        qms = [q_ref[:, kcols[h]] for h in range(hp)]
    for h in range(hp):
        qt_ref[h] = qms[h].T

    @pl.when(pl.program_id(2) == 0)
    def _():
        klane = lax.broadcasted_iota(jnp.int32, (1, LANES), 1)
        for h in range(hp):
            kabs = jnp.max(jnp.abs(k_ref[:, kcols[h]].astype(F32)), axis=0, keepdims=True)
            sq = kabs * kabs
            if mode == "fox":
                sq = jnp.where((klane >= HEAD_DIM * (h % 2)) & (klane < HEAD_DIM * (h % 2 + 1)),
                               sq, 0.0)
            kn_ref[h] = jnp.broadcast_to(jnp.sqrt(jnp.sum(sq, axis=1, keepdims=True)), (1, bq))

    def scores(sb, h, masked, col0=0):
        ks = pl.multiple_of(sb * sub, sub)
        k = k_ref[pl.ds(ks, sub), kcols[h]]
        if mode == "fox":
            k = jnp.concatenate([k, kaug_ref[pl.ds(ks, sub), :]], axis=1)
        sT = jnp.dot(k, qt_ref[h, :, col0:], preferred_element_type=F32)
        if masked and mode == "fox":
            sT = jnp.where((ks + row <= qs + col)[:, col0:], sT, NEG)
        elif masked:
            sT = jnp.where((((ks + row) >> 6) <= ((qs + col) >> 6))[:, col0:], sT, NEG)
        return sT

    nsub = bq // sub
    nvt = sub // VT_BLK

    def produce(buf, sb0, masked, h):
        s_buf, m_buf = buf
        tile_max = None
        for c in range(nsub):
            col0 = c * sub if masked else 0
            sT = scores(sb0 + c, h, masked, col0)
            cm = jnp.max(sT, axis=0, keepdims=True)
            if col0:
                s_buf[h, c, :, 0:col0] = jnp.full((sub, col0), NEG, F32)
                cm = jnp.concatenate([jnp.full((1, col0), NEG, F32), cm], axis=1)
            s_buf[h, c, :, col0:] = sT
            tile_max = cm if tile_max is None else jnp.maximum(tile_max, cm)
        m_buf[h] = tile_max

    def consume(buf, sb0, carry, h):
        s_buf, m_buf = buf
        tiles = [lambda c=c: s_buf[h, c] for c in range(nsub)]
        vts = [jnp.concatenate([vt_ref[(sb0 + c) * nvt + v, HEAD_DIM * h:HEAD_DIM * (h + 1), :]
                                for v in range(nvt)], axis=1) for c in range(nsub)]
        return _softmax_step(tiles, vts, carry, tile_max=m_buf[h])

    def stage(cur, cur_sb, nxt, nxt_sb, carries):
        if nxt is not None:
            for h in range(hp):
                produce(nxt, nxt_sb, False, h)
        return tuple(consume(cur, cur_sb, carries[h], h) for h in range(hp))

    n = pl.program_id(2)
    diag_sb = qs // sub
    buf_a, buf_b = (sa_ref, ma_ref), (sb_ref, mb_ref)
    for h in range(hp):
        produce(buf_a, diag_sb, True, h)

    def pair(j, carries):
        carries = stage(buf_a, jnp.where(j == 0, diag_sb, (2 * j - 1) * nsub),
                        buf_b, 2 * j * nsub, carries)
        return stage(buf_b, 2 * j * nsub,
                     buf_a, jnp.minimum(2 * j + 1, n - 1) * nsub, carries)

    def online(_):
        carries = tuple(_softmax_init(bq) for _ in range(hp))
        carries = lax.fori_loop(0, (n + 1) // 2, pair, carries)
        carries = lax.cond(
            n % 2 == 0,
            lambda c: stage(buf_a, jnp.where(n == 0, diag_sb, (n - 1) * nsub), None, None, c),
            lambda c: c, carries)
        return jnp.concatenate([_softmax_out(c) for c in carries], axis=0)

    refs_ = []
    gap = None
    for h in range(hp):
        qf = qt_ref[h, 0:LANES, :].astype(F32)
        bound = jnp.sqrt(jnp.sum(qf * qf, axis=0, keepdims=True)) * kn_ref[h] * 1.01 + 1e-3
        refs_.append(bound)
        g = jnp.max(bound - ma_ref[h])
        gap = g if gap is None else jnp.maximum(gap, g)

    def weigh_add(acc, h, s, sb0, nblk, col0=0):
        p = jnp.exp2(s - refs_[h][:, col0:]).astype(BF16)
        vt = jnp.concatenate([vt_ref[sb0 * nvt + v, HEAD_DIM * h:HEAD_DIM * (h + 1), :]
                              for v in range(nblk * nvt)], axis=1)
        vt1 = jnp.concatenate([vt, jnp.ones((ONES_ROWS, nblk * sub), BF16)], axis=0)
        d = jnp.dot(vt1, p, preferred_element_type=F32)
        if col0:
            d = jnp.concatenate([jnp.zeros((d.shape[0], col0), F32), d], axis=1)
        return acc + d

    def fixed_reference(_):
        def run(kb, accs, nblk):
            ks = pl.multiple_of(kb * sub, sub)
            ss = []
            for h in range(hp):
                k = k_ref[pl.ds(ks, nblk * sub), kcols[h]]
                if mode == "fox":
                    k = jnp.concatenate([k, kaug_ref[pl.ds(ks, nblk * sub), :]], axis=1)
                ss.append(jnp.dot(k, qt_ref[h], preferred_element_type=F32))
            return tuple(weigh_add(accs[h], h, ss[h], kb, nblk) for h in range(hp))

        accs = tuple(jnp.zeros((HEAD_DIM + ONES_ROWS, bq), F32) for _ in range(hp))
        done = 0
        for width in FIXED_WIDTHS:
            trips = (diag_sb - done) // width
            accs = lax.fori_loop(0, trips,
                                 lambda j, a, done=done, width=width: run(done + j * width, a, width),
                                 accs)
            done = done + trips * width
        for c in range(nsub):
            accs = [weigh_add(accs[h], h, sa_ref[h, c, :, c * sub:], diag_sb + c, 1, c * sub)
                    for h in range(hp)]
        return jnp.concatenate([a[0:HEAD_DIM] / a[HEAD_DIM:HEAD_DIM + 1] for a in accs], axis=0)

    oT = lax.cond(gap <= SAFE_GAP, fixed_reference, online, None)
    o_ref[...] = oT.T.astype(o_ref.dtype)


def _flash(mode, q_arr, q_col0, k_arr, k_col0, vt_arr, vt_row0, extra, n_heads,
           hp=FLASH_HP, bq=FLASH_BQ):
    B, S, _ = q_arr.shape
    qw = (HEAD_DIM if mode == "fox" else LANES) * hp
    vw = HEAD_DIM * hp
    in_specs = [
        pl.BlockSpec((None, bq, qw), lambda b, g, i: (b, i, q_col0 // qw + g)),
        pl.BlockSpec((None, S, qw), lambda b, g, i: (b, 0, k_col0 // qw + g)),
        pl.BlockSpec((None, S // VT_BLK, vw, VT_BLK),
                     lambda b, g, i: (b, 0, vt_row0 // vw + g, 0)),
    ]
    args = [q_arr, k_arr, vt_arr]
    if mode == "fox":
        q_aug, k_aug = extra
        in_specs += [
            pl.BlockSpec((None, bq, LANES), lambda b, g, i: (b, i, 0)),
            pl.BlockSpec((None, S, LANES), lambda b, g, i: (b, 0, 0)),
        ]
        args += [q_aug, k_aug]
    return pl.pallas_call(
        functools.partial(_flash_kernel, mode=mode, hp=hp, bq=bq),
        grid=(B, n_heads // hp, S // bq),
        in_specs=in_specs,
        out_specs=pl.BlockSpec((None, bq, vw), lambda b, g, i: (b, i, g)),
        out_shape=jax.ShapeDtypeStruct((B, S, n_heads * HEAD_DIM), BF16),
        scratch_shapes=([pltpu.VMEM((hp, bq // FLASH_SUB, FLASH_SUB, bq), F32)] * 2
                        + [pltpu.VMEM((hp, 1, bq), F32)] * 2
                        + [pltpu.VMEM((hp, 2 * LANES if mode == "fox" else LANES, bq), BF16),
                           pltpu.VMEM((hp, 1, bq), F32)]),
        compiler_params=_cparams(("arbitrary", "arbitrary", "arbitrary")),
        name="flash_" + mode,
    )(*args)


CK_B = 2 * CHUNK
CK_NW = N_LEFT_CHUNKS * CHUNK // CK_B + 1
CK_EXT = (CK_NW + 1) * CK_B


def _chunk_kernel(q_ref, k_ref, vt_ref, ext_ref, o_ref, tab_ref, s_ref, *, hp, nq):
    i = pl.program_id(1)

    @pl.when(i == 0)
    def _():
        jj = lax.broadcasted_iota(jnp.int32, (CK_B, CK_B), 0)
        rr = lax.broadcasted_iota(jnp.int32, (CK_B, CK_B), 1)
        for h in range(hp):
            for w in range(CK_NW):
                a = (CK_NW - 1 - w) * CK_B
                g = jnp.broadcast_to(ext_ref[h:h + 1, a:a + 2 * CK_B], (CK_B, 2 * CK_B))
                t = pltpu.roll(g, CK_B, 1, stride=1, stride_axis=0)[:, :CK_B]
                if w == 0:
                    t = jnp.where((rr >= CHUNK) & (jj < CHUNK), NEG, t)
                if w == CK_NW - 1:
                    t = jnp.where((rr < CHUNK) & (jj >= CHUNK), NEG, t)
                tab_ref[h, w * CK_B:(w + 1) * CK_B, :] = t

    kcols = [slice(LANES * (h // 2), LANES * (h // 2 + 1)) for h in range(hp)]
    firsts = [i * nq + u - (CK_NW - 1) for u in range(nq)]

    def finish():
        for u in range(nq):
            kbc = [jnp.maximum(firsts[u] + w, 0) for w in range(CK_NW)]
            outs = []
            for h in range(hp):
                vt = jnp.concatenate([vt_ref[kbc[w], HEAD_DIM * h:HEAD_DIM * (h + 1), :]
                                      for w in range(CK_NW)], axis=1)
                outs.append(_softmax_out(_softmax_step([lambda u=u, h=h: s_ref[u, h]], [vt],
                                                       _softmax_init(CK_B))))
            oT = jnp.concatenate(outs, axis=0)
            o_ref[u * CK_B:(u + 1) * CK_B, :] = oT.T.astype(o_ref.dtype)

    def pair_scores(u, p, ks, nrows):
        q2 = q_ref[u * CK_B:(u + 1) * CK_B, kcols[2 * p]]
        qq = jnp.concatenate([_pair_mask_q(q2, 0), _pair_mask_q(q2, 1)], axis=0)
        return lax.dot_general(k_ref[pl.ds(ks, nrows), kcols[2 * p]], qq, _NT,
                               preferred_element_type=F32)

    @pl.when(firsts[0] >= 0)
    def _():
        for u in range(nq):
            ks = pl.multiple_of(firsts[u] * CK_B, CK_B)
            for p in range(hp // 2):
                sT = pair_scores(u, p, ks, CK_NW * CK_B)
                for j in range(2):
                    s_ref[u, 2 * p + j] = sT[:, j * CK_B:(j + 1) * CK_B] + tab_ref[2 * p + j]
        finish()

    @pl.when(firsts[0] < 0)
    def _():
        for u in range(nq):
            for p in range(hp // 2):
                for w in range(CK_NW):
                    rows = slice(w * CK_B, (w + 1) * CK_B)
                    ks = pl.multiple_of(jnp.maximum(firsts[u] + w, 0) * CK_B, CK_B)
                    sT = pair_scores(u, p, ks, CK_B)
                    for j in range(2):
                        s_ref[u, 2 * p + j, rows, :] = jnp.where(
                            firsts[u] + w >= 0,
                            sT[:, j * CK_B:(j + 1) * CK_B] + tab_ref[2 * p + j, rows, :], NEG)
        finish()


def _chunk_attn(main, q_col0, k_col0, vt_arr, vt_row0, ext, n_heads, nq=4):
    B, S, _ = main.shape
    hp = n_heads
    qw, vw = HEAD_DIM * hp, HEAD_DIM * hp
    return pl.pallas_call(
        functools.partial(_chunk_kernel, hp=hp, nq=nq),
        grid=(B, S // (nq * CK_B)),
        in_specs=[
            pl.BlockSpec((None, nq * CK_B, qw), lambda b, i: (b, i, q_col0 // qw)),
            pl.BlockSpec((None, S, qw), lambda b, i: (b, 0, k_col0 // qw)),
            pl.BlockSpec((None, S // VT_BLK, vw, VT_BLK), lambda b, i: (b, 0, vt_row0 // vw, 0)),
            pl.BlockSpec((hp, CK_EXT), lambda b, i: (0, 0)),
        ],
        out_specs=pl.BlockSpec((None, nq * CK_B, vw), lambda b, i: (b, i, 0)),
        out_shape=jax.ShapeDtypeStruct((B, S, n_heads * HEAD_DIM), BF16),
        scratch_shapes=[pltpu.VMEM((hp, CK_NW * CK_B, CK_B), F32),
                        pltpu.VMEM((nq, hp, CK_NW * CK_B, CK_B), F32)],
        compiler_params=_cparams(("arbitrary", "arbitrary")),
        name="chunk_attn",
    )(main, main, vt_arr, ext)


def _sb_kernel(q_ref, k_ref, vt_ref, o_ref, z_ref, lb_ref, sfx_ref, *, hp, bq):
    qs = pl.program_id(1) * bq
    nsub = bq // SUB
    row = lax.broadcasted_iota(jnp.int32, (SUB, bq), 0)
    col = lax.broadcasted_iota(jnp.int32, (SUB, bq), 1)
    ur = lax.broadcasted_iota(jnp.int32, (SUB, 2 * SUB), 0)
    uc = lax.broadcasted_iota(jnp.int32, (SUB, 2 * SUB), 1) & (SUB - 1)
    upper2 = jnp.where(uc > ur, 1.0, 0.0).astype(BF16)
    kcols = [slice(LANES * (h // 2), LANES * (h // 2 + 1)) for h in range(hp)]
    qms = [_pair_mask_q(q_ref[:, kcols[h]], h % 2) for h in range(hp)]

    def step(kb, carries, masked):
        ks = pl.multiple_of(kb * bq, bq)
        for h in range(hp):
            z_ref[h] = lax.dot_general(k_ref[pl.ds(ks, bq), kcols[h]], qms[h], _NT,
                                       preferred_element_type=F32)
        first_col = [c * SUB if masked else 0 for c in range(nsub)]

        def widen(x, c):
            if first_col[c] == 0:
                return x
            return jnp.concatenate([jnp.zeros((x.shape[0], first_col[c]), x.dtype), x], axis=1)

        totals = []
        for h in range(hp):
            tot = []
            for c in range(nsub):
                rows, cols = slice(c * SUB, (c + 1) * SUB), slice(first_col[c], bq)
                z = z_ref[h, rows, cols]
                l1 = jnp.log(1.0 + jnp.exp(-jnp.abs(z)))
                log_beta = jnp.minimum(z, 0.0) - l1
                log_keep = log_beta - z
                if masked:
                    valid = (ks + c * SUB + row < qs + col)[:, cols]
                    log_keep = jnp.where(valid, log_keep, 0.0)
                lb_ref[h, rows, cols] = log_beta
                hi = log_keep.astype(BF16)
                lo = (log_keep - hi.astype(F32)).astype(BF16)
                sfx = jnp.dot(upper2, jnp.concatenate([hi, lo], axis=0),
                              preferred_element_type=F32)
                sfx_ref[h, rows, cols] = sfx
                tot.append(widen(sfx[0:1, :] + log_keep[0:1, :], c))
            totals.append(tot)
        out = []
        for h in range(hp):
            tail, acc = carries[h]
            parts = [None] * nsub
            for c in range(nsub - 1, -1, -1):
                rows, cols = slice(c * SUB, (c + 1) * SUB), slice(first_col[c], bq)
                a = jnp.exp(lb_ref[h, rows, cols] + sfx_ref[h, rows, cols] + tail[:, cols])
                if masked:
                    a = jnp.where((ks + c * SUB + row < qs + col)[:, cols], a, 0.0)
                parts[c] = a.astype(BF16)
                tail = tail + totals[h][c]
            vts = [vt_ref[kb * nsub + c, HEAD_DIM * h:HEAD_DIM * (h + 1), :] for c in range(nsub)]
            if masked:
                for c in range(nsub):
                    acc = acc + widen(jnp.dot(vts[c], parts[c], preferred_element_type=F32), c)
            else:
                acc = acc + jnp.dot(jnp.concatenate(vts, axis=1), jnp.concatenate(parts, axis=0),
                                    preferred_element_type=F32)
            out.append((tail, acc))
        return tuple(out)

    n_full = qs // bq
    carries = tuple((jnp.zeros((1, bq), F32), jnp.zeros((HEAD_DIM, bq), F32))
                    for _ in range(hp))
    carries = step(n_full, carries, True)

    def cond(state):
        kb, carries = state
        tail_max = carries[0][0]
        for h in range(1, hp):
            tail_max = jnp.maximum(tail_max, carries[h][0])
        return (kb >= 0) & (jnp.max(tail_max) > SB_ZERO_LOG)

    def body(state):
        kb, carries = state
        return kb - 1, step(kb, carries, False)

    _, carries = lax.while_loop(cond, body, (n_full - 1, carries))
    _store_heads(o_ref, [acc for (_, acc) in carries])


def _sb_attn(main, q_col0, k_col0, vt_arr, vt_row0, n_heads, bq=256):
    B, S, _ = main.shape
    hp = n_heads
    qw = HEAD_DIM * hp
    return pl.pallas_call(
        functools.partial(_sb_kernel, hp=hp, bq=bq),
        grid=(B, S // bq),
        in_specs=[
            pl.BlockSpec((None, bq, qw), lambda b, i: (b, i, q_col0 // qw)),
            pl.BlockSpec((None, S, qw), lambda b, i: (b, 0, k_col0 // qw)),
            pl.BlockSpec((None, S // VT_BLK, qw, VT_BLK), lambda b, i: (b, 0, vt_row0 // qw, 0)),
        ],
        out_specs=pl.BlockSpec((None, bq, qw), lambda b, i: (b, i, 0)),
        out_shape=jax.ShapeDtypeStruct((B, S, n_heads * HEAD_DIM), BF16),
        scratch_shapes=[pltpu.VMEM((hp, bq, bq), F32)] * 3,
        compiler_params=_cparams(("arbitrary", "arbitrary")),
        name="sb_attn",
    )(main, main, vt_arr)


def _mla_prep_kernel(aux_ref, pos_ref, invf_ref, qn_ref, kvn_ref, wuq_ref, wuqr_ref,
                     wk_ref, wvt_ref, oq_ref, ok_ref, ovt_ref, *, q_scale):
    ang = pos_ref[...] * invf_ref[...]
    cos4, sin4 = jnp.cos(ang), jnp.sin(ang)
    lane = lax.broadcasted_iota(jnp.int32, ang.shape, 1)
    rotary = (lane >= NOPE_DIM) & (lane < NOPE_DIM + ROPE_DIM)
    cos_rows, sin_rows = [], []
    for m in range(LANES // ROPE_DIM):
        shift = (NOPE_DIM - ROPE_DIM * m) % LANES
        cm = cos4 if shift == 0 else pltpu.roll(cos4, shift, 1)
        sm = sin4 if shift == 0 else pltpu.roll(sin4, shift, 1)
        cos_rows.append(jnp.where(rotary, cm, 1.0))
        sin_rows.append(jnp.where(rotary, sm, 0.0))
    cos = jnp.concatenate(cos_rows, axis=0)
    sin = jnp.concatenate(sin_rows, axis=0)
    cq = _rms_bf16(aux_ref[:, 0:Q_LORA], qn_ref[...])
    ckv = _rms_bf16(aux_ref[:, Q_LORA:Q_LORA + KV_LORA], kvn_ref[...])
    o = Q_LORA + KV_LORA
    k_rope = aux_ref[:, o:o + LANES] * cos + aux_ref[:, o + LANES:o + 2 * LANES] * sin
    cos_q, sin_q = cos * q_scale, sin * q_scale
    n_heads = oq_ref.shape[-1] // LANES
    for h in range(0, n_heads, 2):
        cols = slice(h * LANES, (h + 2) * LANES)
        qa = jnp.dot(cq, wuq_ref[:, cols], preferred_element_type=F32)
        qb = jnp.dot(cq, wuqr_ref[:, cols], preferred_element_type=F32)
        kn = jnp.dot(ckv, wk_ref[:, cols], preferred_element_type=F32)
        for d in range(2):
            c1 = slice(d * LANES, (d + 1) * LANES)
            c2 = slice((h + d) * LANES, (h + d + 1) * LANES)
            oq_ref[:, c2] = (qa[:, c1] * cos_q + qb[:, c1] * sin_q).astype(BF16)
            ok_ref[:, c2] = (kn[:, c1] + k_rope).astype(BF16)
    nv = wvt_ref.shape[0]
    for r in range(0, nv, 256):
        vt = lax.dot_general(wvt_ref[r:r + 256, :], ckv, _NT,
                             preferred_element_type=F32).astype(BF16)
        _store_vt(ovt_ref, vt, r)


MLA_TM = 512


def _mla_prep(aux, pos, invf, qn, kvn, wuq, wuqr, wk, wvt, q_scale, tm=MLA_TM):
    B, S, na = aux.shape
    nq, nv = wuq.shape[1], wvt.shape[0]
    full = lambda a: pl.BlockSpec(a.shape, lambda b, i: (0,) * a.ndim)
    return pl.pallas_call(
        functools.partial(_mla_prep_kernel, q_scale=q_scale),
        grid=(B, S // tm),
        in_specs=[
            pl.BlockSpec((None, tm, na), lambda b, i: (b, i, 0)),
            pl.BlockSpec((None, None) + pos.shape[2:], lambda b, i: (b, i, 0, 0)),
            full(invf), full(qn), full(kvn), full(wuq), full(wuqr), full(wk), full(wvt),
        ],
        out_specs=[
            pl.BlockSpec((None, tm, nq), lambda b, i: (b, i, 0)),
            pl.BlockSpec((None, tm, nq), lambda b, i: (b, i, 0)),
            pl.BlockSpec((None, tm // VT_BLK, nv, VT_BLK), lambda b, i: (b, i, 0, 0)),
        ],
        out_shape=[
            jax.ShapeDtypeStruct((B, S, nq), BF16),
            jax.ShapeDtypeStruct((B, S, nq), BF16),
            jax.ShapeDtypeStruct((B, S // VT_BLK, nv, VT_BLK), BF16),
        ],
        compiler_params=_cparams(("arbitrary", "arbitrary")),
        name="mla_prep",
    )(aux, pos, invf, qn, kvn, wuq, wuqr, wk, wvt)


def _mlp_kernel(x_ref, oa_ref, ob_ref, wo_ref, g_ref, wu_ref, wd_ref, gf_ref, out_ref,
                x1_ref, h_ref, a_ref, *, final_norm, tf):
    na = oa_ref.shape[-1]
    x1 = (x_ref[...]
          + jnp.dot(oa_ref[...], wo_ref[0:na, :], preferred_element_type=F32)
          + jnp.dot(ob_ref[...], wo_ref[na:, :], preferred_element_type=F32))
    x1_ref[...] = x1
    h_ref[...] = _rms_bf16(x1, g_ref[...])
    for f in range(0, wu_ref.shape[1], tf):
        u = jnp.dot(h_ref[...], wu_ref[:, f:f + tf], preferred_element_type=F32)
        a_ref[:, f:f + tf] = jnp.square(jnp.maximum(u, 0.0)).astype(BF16)
    y = x1_ref[...] + jnp.dot(a_ref[...], wd_ref[...], preferred_element_type=F32)
    if final_norm:
        ms = jnp.mean(y * y, axis=-1, keepdims=True)
        y = y * lax.rsqrt(ms + EPS) * gf_ref[...]
    out_ref[...] = y


def _mlp(x2, oa, ob, wo, g, wu, wd, gf, final_norm, tm=1024, tf=1024):
    T, D = x2.shape
    F = wu.shape[1]
    na, nb = oa.shape[1], ob.shape[1]
    const = lambda shape: pl.BlockSpec(shape, lambda i: (0, 0), pipeline_mode=pl.Buffered(1))
    return pl.pallas_call(
        functools.partial(_mlp_kernel, final_norm=final_norm, tf=tf),
        grid=(T // tm,),
        in_specs=[
            pl.BlockSpec((tm, D), lambda i: (i, 0)),
            pl.BlockSpec((tm, na), lambda i: (i, 0)),
            pl.BlockSpec((tm, nb), lambda i: (i, 0)),
            const((na + nb, D)),
            const((1, D)),
            const((D, F)),
            const((F, D)),
            const((1, D)),
        ],
        out_specs=pl.BlockSpec((tm, D), lambda i: (i, 0)),
        out_shape=jax.ShapeDtypeStruct((T, D), F32),
        scratch_shapes=[pltpu.VMEM((tm, D), F32), pltpu.VMEM((tm, D), BF16),
                        pltpu.VMEM((tm, F), BF16)],
        compiler_params=_cparams(("arbitrary",)),
        name="mlp",
    )(x2, oa, ob, wo, g.reshape(1, D), wu, wd, gf.reshape(1, D))


def _pad_cols(w, n):
    return jnp.pad(w, ((0, 0), (0, n - w.shape[1])))


def _rot_cols(w):
    half = ROPE_DIM // 2
    return jnp.concatenate([-w[:, half:], w[:, :half]], axis=1)


def _rope_slab(w):
    z = jnp.zeros((w.shape[0], NOPE_DIM), w.dtype)
    return jnp.concatenate([z, w, jnp.zeros((w.shape[0], LANES - NOPE_DIM - ROPE_DIM), w.dtype)], axis=1)


def _even_layer(x, g_mix, w_in, b_forget, rel_bias, w_out, g_mlp, w_up, w_down, g_final,
                final_norm):
    B, S, D = x.shape
    hf, hc = b_forget.shape[0], rel_bias.shape[0]
    wf, wc = hf * HEAD_DIM, hc * HEAD_DIM
    o = np.cumsum([0, wf, wf, wf, hf, wc, wc, wc])
    qa, ka, va, fa, qb, kb, vb = [w_in[:, o[n]:o[n + 1]] for n in range(7)]
    q_scale = HEAD_DIM ** -0.5 * LOG2E
    wm = jnp.concatenate([qa * q_scale, ka, qb * q_scale, kb], axis=1).astype(BF16)
    wvt = jnp.concatenate([va, vb], axis=1).T.astype(BF16)
    wa = _pad_cols(fa, LANES).astype(BF16)
    main, vt, aux = _inproj(x, g_mix, wm, wvt, wa)

    q_aug, k_aug = _logcum(aux, _pad_cols(b_forget.reshape(1, hf), LANES), hf)
    o_a = _flash("fox", main, 0, main, wf, vt, 0, (q_aug, k_aug), hf)

    assert rel_bias.shape[1] == CHUNK + REL_CLIP
    right = CK_EXT - rel_bias.shape[1] - (CHUNK + 1)
    ext = jnp.pad(rel_bias * LOG2E, ((0, 0), (CHUNK + 1, right)), mode="edge")
    o_b = _chunk_attn(main, 2 * wf, 2 * wf + wc, vt, wf, ext, hc)

    y = _mlp(x.reshape(B * S, D), o_a.reshape(B * S, wf), o_b.reshape(B * S, wc),
             w_out.astype(BF16), g_mlp, w_up.astype(BF16), w_down.astype(BF16), g_final,
             final_norm)
    return y.reshape(B, S, D)


def _odd_layer(x, positions, g_mix, w_in, q_norm, kv_norm, w_uq, w_ukv, w_out, g_mlp, w_up,
               w_down, g_final, final_norm):
    B, S, D = x.shape
    hm = w_ukv.shape[1] // (NOPE_DIM + HEAD_DIM)
    ws = w_in.shape[1] - Q_LORA - KV_LORA - ROPE_DIM
    hs = (ws // 3) // HEAD_DIM
    wsb = hs * HEAD_DIM
    o = np.cumsum([0, wsb, wsb, wsb, Q_LORA, KV_LORA, ROPE_DIM])
    qc, kc, vc, w_cq, w_ckv, w_kr = [w_in[:, o[n]:o[n + 1]] for n in range(6)]
    wm = jnp.concatenate([qc * HEAD_DIM ** -0.5, kc], axis=1).astype(BF16)
    wa = jnp.concatenate([w_cq, w_ckv, _rope_slab(w_kr), _rope_slab(_rot_cols(w_kr))],
                         axis=1).astype(BF16)
    main, vt, aux = _inproj(x, g_mix, wm, vc.T.astype(BF16), wa)
    o_c = _sb_attn(main, 0, wsb, vt, 0, hs)

    dq = NOPE_DIM + ROPE_DIM
    wuq3 = w_uq.reshape(Q_LORA, hm, dq)
    nope, ropew = wuq3[:, :, :NOPE_DIM], wuq3[:, :, NOPE_DIM:]
    zq = jnp.zeros((Q_LORA, hm, LANES - dq), w_uq.dtype)
    wuq = jnp.concatenate([nope, ropew, zq], axis=2).reshape(Q_LORA, hm * LANES).astype(BF16)
    half = ROPE_DIM // 2
    ropr = jnp.concatenate([-ropew[:, :, half:], ropew[:, :, :half]], axis=2)
    wuqr = jnp.concatenate([jnp.zeros_like(nope), ropr, zq], axis=2)
    wuqr = wuqr.reshape(Q_LORA, hm * LANES).astype(BF16)
    wkv3 = w_ukv.reshape(KV_LORA, hm, NOPE_DIM + HEAD_DIM)
    wk = jnp.concatenate([wkv3[:, :, :NOPE_DIM],
                          jnp.zeros((KV_LORA, hm, LANES - NOPE_DIM), w_ukv.dtype)], axis=2)
    wk = wk.reshape(KV_LORA, hm * LANES).astype(BF16)
    wv_t = wkv3[:, :, NOPE_DIM:].reshape(KV_LORA, hm * HEAD_DIM).T.astype(BF16)
    freqs = (ROPE_THETA ** (-jnp.arange(half, dtype=F32) / half))
    invf = jnp.tile(freqs, 2 * LANES // ROPE_DIM).reshape(1, LANES)
    groups = LANES // ROPE_DIM
    pos = positions.astype(F32).reshape(B, S // MLA_TM, groups, MLA_TM // groups)
    pos = jnp.repeat(jnp.swapaxes(pos, 2, 3), ROPE_DIM, axis=-1)
    qm, km, vtm = _mla_prep(aux, pos, invf, q_norm.reshape(1, Q_LORA),
                            kv_norm.reshape(1, KV_LORA), wuq, wuqr, wk, wv_t,
                            dq ** -0.5 * LOG2E)
    o_d = _flash("mla", qm, 0, km, 0, vtm, 0, None, hm)

    y = _mlp(x.reshape(B * S, D), o_c.reshape(B * S, wsb), o_d.reshape(B * S, hm * HEAD_DIM),
             w_out.astype(BF16), g_mlp, w_up.astype(BF16), w_down.astype(BF16), g_final,
             final_norm)
    return y.reshape(B, S, D)


def kernel(x, positions, norm_mix, norm_mlp, norm_final, w_in_ab, b_forget, rel_bias, w_out_ab,
           w_in_cd, q_norm, kv_norm, w_uq, w_ukv, w_out_cd, w_up, w_down):
    depth = norm_mix.shape[0]
    for layer in range(depth):
        last = layer == depth - 1
        if layer % 2 == 0:
            e = layer // 2
            x = _even_layer(x, norm_mix[layer], w_in_ab[e], b_forget[e], rel_bias[e], w_out_ab[e],
                            norm_mlp[layer], w_up[layer], w_down[layer], norm_final, last)
        else:
            o = layer // 2
            x = _odd_layer(x, positions, norm_mix[layer], w_in_cd[o], q_norm[o], kv_norm[o],
                           w_uq[o], w_ukv[o], w_out_cd[o], norm_mlp[layer], w_up[layer],
                           w_down[layer], norm_final, last)
    return x
```

```python
import functools
import math

import numpy as np
import jax
import jax.numpy as jnp
from jax import lax
from jax.experimental import pallas as pl
from jax.experimental.pallas import tpu as pltpu

F32 = jnp.float32
BF16 = jnp.bfloat16

EPS = 1e-6
HEAD_DIM = 64
CHUNK = 64
N_LEFT_CHUNKS = 8
REL_CLIP = 256
ROPE_DIM = 32
NOPE_DIM = 64
ROPE_THETA = 10000.0
Q_LORA = 384
KV_LORA = 256

LANES = 128
VT_BLK = LANES
SUB = LANES
FLASH_HP = 8
FLASH_BQ = 512
FLASH_SUB = 256
FIXED_WIDTHS = (4, 2, 1)
SAFE_GAP = 80.0
NEG = -1e30
LOG2E = math.log2(math.e)
SB_ZERO_LOG = -104.0
VMEM_LIMIT = 56 * 1024 * 1024

_NT = (((1,), (1,)), ((), ()))


def _cparams(sem):
    return pltpu.CompilerParams(dimension_semantics=sem, vmem_limit_bytes=VMEM_LIMIT)


def _rms_bf16(x, g):
    ms = jnp.mean(x * x, axis=-1, keepdims=True)
    return (x * lax.rsqrt(ms + EPS) * g).astype(BF16)


def _store_vt(ovt_ref, vt, row0):
    rows, tm = vt.shape
    for c in range(tm // VT_BLK):
        ovt_ref[c, row0:row0 + rows, :] = vt[:, c * VT_BLK:(c + 1) * VT_BLK]


def _inproj_kernel(x_ref, g_ref, wm_ref, wvt_ref, wa_ref, om_ref, ovt_ref, oa_ref):
    h = _rms_bf16(x_ref[...], g_ref[...])
    nm = om_ref.shape[-1]
    for c in range(0, nm, 512):
        om_ref[:, c:c + 512] = jnp.dot(
            h, wm_ref[:, c:c + 512], preferred_element_type=F32).astype(BF16)
    nv = wvt_ref.shape[0]
    for r in range(0, nv, 256):
        vt = lax.dot_general(wvt_ref[r:r + 256, :], h, _NT,
                             preferred_element_type=F32).astype(BF16)
        _store_vt(ovt_ref, vt, r)
    oa_ref[...] = jnp.dot(h, wa_ref[...], preferred_element_type=F32)


def _inproj(x, g, wm, wvt, wa, tm=1024):
    B, S, D = x.shape
    nm, nv, na = wm.shape[1], wvt.shape[0], wa.shape[1]
    return pl.pallas_call(
        _inproj_kernel,
        grid=(B, S // tm),
        in_specs=[
            pl.BlockSpec((None, tm, D), lambda b, i: (b, i, 0)),
            pl.BlockSpec((1, D), lambda b, i: (0, 0)),
            pl.BlockSpec((D, nm), lambda b, i: (0, 0)),
            pl.BlockSpec((nv, D), lambda b, i: (0, 0)),
            pl.BlockSpec((D, na), lambda b, i: (0, 0)),
        ],
        out_specs=[
            pl.BlockSpec((None, tm, nm), lambda b, i: (b, i, 0)),
            pl.BlockSpec((None, tm // VT_BLK, nv, VT_BLK), lambda b, i: (b, i, 0, 0)),
            pl.BlockSpec((None, tm, na), lambda b, i: (b, i, 0)),
        ],
        out_shape=[
            jax.ShapeDtypeStruct((B, S, nm), BF16),
            jax.ShapeDtypeStruct((B, S // VT_BLK, nv, VT_BLK), BF16),
            jax.ShapeDtypeStruct((B, S, na), F32),
        ],
        compiler_params=_cparams(("arbitrary", "arbitrary")),
        name="inproj",
    )(x, g.reshape(1, D), wm, wvt, wa)


def _split3(x):
    hi = x.astype(BF16)
    r = x - hi.astype(F32)
    mid = r.astype(BF16)
    lo = (r - mid.astype(F32)).astype(BF16)
    return hi, mid, lo


AUG_W = 8


def _logcum_kernel(fa_ref, b_ref, pq_ref, pk_ref, oneq_ref, onek_ref, oq_ref, ok_ref, carry_ref):
    @pl.when(pl.program_id(1) == 0)
    def _():
        carry_ref[...] = jnp.zeros_like(carry_ref)

    z = fa_ref[...] + b_ref[...]
    lf = jnp.minimum(z, 0.0) - jnp.log(1.0 + jnp.exp(-jnp.abs(z)))
    tc = lf.shape[0]
    r = lax.broadcasted_iota(jnp.int32, (tc, tc), 0)
    c = lax.broadcasted_iota(jnp.int32, (tc, tc), 1)
    tri = jnp.where(r >= c, 1.0, 0.0).astype(BF16)
    cs = carry_ref[...]
    for part in _split3(lf):
        cs = cs + jnp.dot(tri, part, preferred_element_type=F32)
    carry_ref[...] = cs[tc - 1:tc, :]
    qa, ka = oneq_ref[...], onek_ref[...]
    for n, part in enumerate(_split3(cs * LOG2E)):
        qa = qa + jnp.dot(part, pq_ref[n], preferred_element_type=F32)
        ka = ka + jnp.dot(part, pk_ref[n], preferred_element_type=F32)
    oq_ref[...] = qa.astype(BF16)
    ok_ref[...] = ka.astype(BF16)


def _logcum(fa, bias, n_heads, tc=512):
    B, S, W = fa.shape
    na = LANES
    pq = np.zeros((3, W, na), np.float32)
    pk = np.zeros((3, W, na), np.float32)
    oneq = np.zeros((1, na), np.float32)
    onek = np.zeros((1, na), np.float32)
    for h in range(n_heads):
        base = h * AUG_W
        for n in range(3):
            pq[n, h, base + n] = 1.0
            pk[n, h, base + 3 + n] = -1.0
        oneq[0, base + 3:base + 6] = 1.0
        onek[0, base:base + 3] = 1.0
    const = lambda a: pl.BlockSpec(a.shape, lambda b, i: (0,) * a.ndim)
    args = [jnp.asarray(pq, BF16), jnp.asarray(pk, BF16), jnp.asarray(oneq), jnp.asarray(onek)]
    return pl.pallas_call(
        _logcum_kernel,
        grid=(B, S // tc),
        in_specs=[pl.BlockSpec((None, tc, W), lambda b, i: (b, i, 0)),
                  pl.BlockSpec((1, W), lambda b, i: (0, 0))] + [const(a) for a in args],
        out_specs=[pl.BlockSpec((None, tc, na), lambda b, i: (b, i, 0))] * 2,
        out_shape=[jax.ShapeDtypeStruct((B, S, na), BF16)] * 2,
        scratch_shapes=[pltpu.VMEM((1, W), F32)],
        compiler_params=_cparams(("arbitrary", "arbitrary")),
        name="logcum",
    )(fa, bias, *args)


def _pair_mask_q(q2, j):
    lane = lax.broadcasted_iota(jnp.int32, q2.shape, 1)
    keep = (lane >= HEAD_DIM * j) & (lane < HEAD_DIM * (j + 1))
    return jnp.where(keep, q2, jnp.zeros_like(q2))


ONES_ROWS = 16


def _softmax_step(tiles, vts, carry, tile_max=None):
    m, acc = carry
    if tile_max is not None:
        m_new = jnp.maximum(m, tile_max)
    else:
        m_new = m
        for tile in tiles:
            m_new = jnp.maximum(m_new, jnp.max(tile(), axis=0, keepdims=True))
    alpha = jnp.exp2(m - m_new)
    pv = None
    for tile, vt in zip(tiles, vts):
        p = jnp.exp2(tile() - m_new).astype(BF16)
        vt1 = jnp.concatenate([vt, jnp.ones((ONES_ROWS, vt.shape[1]), BF16)], axis=0)
        d = jnp.dot(vt1, p, preferred_element_type=F32)
        pv = d if pv is None else pv + d
    return m_new, alpha * acc + pv


def _softmax_init(bq):
    return (jnp.full((1, bq), NEG, F32), jnp.zeros((HEAD_DIM + ONES_ROWS, bq), F32))


def _softmax_out(carry):
    _, acc = carry
    return acc[0:HEAD_DIM] / acc[HEAD_DIM:HEAD_DIM + 1]


def _store_heads(o_ref, outs):
    oT = jnp.concatenate(outs, axis=0)
    o_ref[...] = oT.T.astype(o_ref.dtype)


def _flash_kernel(*refs, mode, hp, bq):
    if mode == "fox":
        q_ref, k_ref, vt_ref, qaug_ref, kaug_ref, o_ref = refs[:6]
    else:
        q_ref, k_ref, vt_ref, o_ref = refs[:4]
    sa_ref, sb_ref, ma_ref, mb_ref, qt_ref, kn_ref = refs[-6:]
    qs = pl.program_id(2) * bq
    sub = FLASH_SUB
    row = lax.broadcasted_iota(jnp.int32, (sub, bq), 0)
    col = lax.broadcasted_iota(jnp.int32, (sub, bq), 1)
    if mode == "fox":
        kcols = [slice(LANES * (h // 2), LANES * (h // 2 + 1)) for h in range(hp)]
        lane = lax.broadcasted_iota(jnp.int32, (bq, LANES), 1)
        qa = qaug_ref[...]
        qms = []
        for h in range(hp):
            first = AUG_W * (pl.program_id(1) * hp + h)
            own = (lane >= first) & (lane < first + AUG_W)
            qms.append(jnp.concatenate(
                [_pair_mask_q(q_ref[:, kcols[h]], h % 2),
                 jnp.where(own, qa, jnp.zeros_like(qa))], axis=1))
    else:
        kcols = [slice(LANES * h, LANES * (h + 1)) for h in range(hp)]
        qms = [q_ref[:, kcols[h]] for h in range(hp)]
    for h in range(hp):
        qt_ref[h] = qms[h].T

    @pl.when(pl.program_id(2) == 0)
    def _():
        klane = lax.broadcasted_iota(jnp.int32, (1, LANES), 1)
        for h in range(hp):
            kabs = jnp.max(jnp.abs(k_ref[:, kcols[h]].astype(F32)), axis=0, keepdims=True)
            sq = kabs * kabs
            if mode == "fox":
                sq = jnp.where((klane >= HEAD_DIM * (h % 2)) & (klane < HEAD_DIM * (h % 2 + 1)),
                               sq, 0.0)
            kn_ref[h] = jnp.broadcast_to(jnp.sqrt(jnp.sum(sq, axis=1, keepdims=True)), (1, bq))

    def scores(sb, h, masked, col0=0):
        ks = pl.multiple_of(sb * sub, sub)
        k = k_ref[pl.ds(ks, sub), kcols[h]]
        if mode == "fox":
            k = jnp.concatenate([k, kaug_ref[pl.ds(ks, sub), :]], axis=1)
        sT = jnp.dot(k, qt_ref[h, :, col0:], preferred_element_type=F32)
        if masked and mode == "fox":
            sT = jnp.where((ks + row <= qs + col)[:, col0:], sT, NEG)
        elif masked:
            sT = jnp.where((((ks + row) >> 6) <= ((qs + col) >> 6))[:, col0:], sT, NEG)
        return sT

    nsub = bq // sub
    nvt = sub // VT_BLK

    def produce(buf, sb0, masked, h):
        s_buf, m_buf = buf
        tile_max = None
        for c in range(nsub):
            col0 = c * sub if masked else 0
            sT = scores(sb0 + c, h, masked, col0)
            cm = jnp.max(sT, axis=0, keepdims=True)
            if col0:
                s_buf[h, c, :, 0:col0] = jnp.full((sub, col0), NEG, F32)
                cm = jnp.concatenate([jnp.full((1, col0), NEG, F32), cm], axis=1)
            s_buf[h, c, :, col0:] = sT
            tile_max = cm if tile_max is None else jnp.maximum(tile_max, cm)
        m_buf[h] = tile_max

    def consume(buf, sb0, carry, h):
        s_buf, m_buf = buf
        tiles = [lambda c=c: s_buf[h, c] for c in range(nsub)]
        vts = [jnp.concatenate([vt_ref[(sb0 + c) * nvt + v, HEAD_DIM * h:HEAD_DIM * (h + 1), :]
                                for v in range(nvt)], axis=1) for c in range(nsub)]
        return _softmax_step(tiles, vts, carry, tile_max=m_buf[h])

    def stage(cur, cur_sb, nxt, nxt_sb, carries):
        if nxt is not None:
            for h in range(hp):
                produce(nxt, nxt_sb, False, h)
        return tuple(consume(cur, cur_sb, carries[h], h) for h in range(hp))

    n = pl.program_id(2)
    diag_sb = qs // sub
    buf_a, buf_b = (sa_ref, ma_ref), (sb_ref, mb_ref)
    for h in range(hp):
        produce(buf_a, diag_sb, True, h)

    def pair(j, carries):
        carries = stage(buf_a, jnp.where(j == 0, diag_sb, (2 * j - 1) * nsub),
                        buf_b, 2 * j * nsub, carries)
        return stage(buf_b, 2 * j * nsub,
                     buf_a, jnp.minimum(2 * j + 1, n - 1) * nsub, carries)

    def online(_):
        carries = tuple(_softmax_init(bq) for _ in range(hp))
        carries = lax.fori_loop(0, (n + 1) // 2, pair, carries)
        carries = lax.cond(
            n % 2 == 0,
            lambda c: stage(buf_a, jnp.where(n == 0, diag_sb, (n - 1) * nsub), None, None, c),
            lambda c: c, carries)
        return jnp.concatenate([_softmax_out(c) for c in carries], axis=0)

    refs_ = []
    gap = None
    for h in range(hp):
        qf = qt_ref[h, 0:LANES, :].astype(F32)
        bound = jnp.sqrt(jnp.sum(qf * qf, axis=0, keepdims=True)) * kn_ref[h] * 1.01 + 1e-3
        refs_.append(bound)
        g = jnp.max(bound - ma_ref[h])
        gap = g if gap is None else jnp.maximum(gap, g)

    def weigh_add(acc, h, s, sb0, nblk, col0=0):
        p = jnp.exp2(s - refs_[h][:, col0:]).astype(BF16)
        vt = jnp.concatenate([vt_ref[sb0 * nvt + v, HEAD_DIM * h:HEAD_DIM * (h + 1), :]
                              for v in range(nblk * nvt)], axis=1)
        vt1 = jnp.concatenate([vt, jnp.ones((ONES_ROWS, nblk * sub), BF16)], axis=0)
        d = jnp.dot(vt1, p, preferred_element_type=F32)
        if col0:
            d = jnp.concatenate([jnp.zeros((d.shape[0], col0), F32), d], axis=1)
        return acc + d

    def fixed_reference(_):
        def run(kb, accs, nblk):
            ks = pl.multiple_of(kb * sub, sub)
            ss = []
            for h in range(hp):
                k = k_ref[pl.ds(ks, nblk * sub), kcols[h]]
                if mode == "fox":
                    k = jnp.concatenate([k, kaug_ref[pl.ds(ks, nblk * sub), :]], axis=1)
                ss.append(jnp.dot(k, qt_ref[h], preferred_element_type=F32))
            return tuple(weigh_add(accs[h], h, ss[h], kb, nblk) for h in range(hp))

        accs = tuple(jnp.zeros((HEAD_DIM + ONES_ROWS, bq), F32) for _ in range(hp))
        done = 0
        for width in FIXED_WIDTHS:
            trips = (diag_sb - done) // width
            accs = lax.fori_loop(0, trips,
                                 lambda j, a, done=done, width=width: run(done + j * width, a, width),
                                 accs)
            done = done + trips * width
        for c in range(nsub):
            accs = [weigh_add(accs[h], h, sa_ref[h, c, :, c * sub:], diag_sb + c, 1, c * sub)
                    for h in range(hp)]
        return jnp.concatenate([a[0:HEAD_DIM] / a[HEAD_DIM:HEAD_DIM + 1] for a in accs], axis=0)

    oT = lax.cond(gap <= SAFE_GAP, fixed_reference, online, None)
    o_ref[...] = oT.T.astype(o_ref.dtype)


def _flash(mode, q_arr, q_col0, k_arr, k_col0, vt_arr, vt_row0, extra, n_heads,
           hp=FLASH_HP, bq=FLASH_BQ):
    B, S, _ = q_arr.shape
    qw = (HEAD_DIM if mode == "fox" else LANES) * hp
    vw = HEAD_DIM * hp
    in_specs = [
        pl.BlockSpec((None, bq, qw), lambda b, g, i: (b, i, q_col0 // qw + g)),
        pl.BlockSpec((None, S, qw), lambda b, g, i: (b, 0, k_col0 // qw + g)),
        pl.BlockSpec((None, S // VT_BLK, vw, VT_BLK),
                     lambda b, g, i: (b, 0, vt_row0 // vw + g, 0)),
    ]
    args = [q_arr, k_arr, vt_arr]
    if mode == "fox":
        q_aug, k_aug = extra
        in_specs += [
            pl.BlockSpec((None, bq, LANES), lambda b, g, i: (b, i, 0)),
            pl.BlockSpec((None, S, LANES), lambda b, g, i: (b, 0, 0)),
        ]
        args += [q_aug, k_aug]
    return pl.pallas_call(
        functools.partial(_flash_kernel, mode=mode, hp=hp, bq=bq),
        grid=(B, n_heads // hp, S // bq),
        in_specs=in_specs,
        out_specs=pl.BlockSpec((None, bq, vw), lambda b, g, i: (b, i, g)),
        out_shape=jax.ShapeDtypeStruct((B, S, n_heads * HEAD_DIM), BF16),
        scratch_shapes=([pltpu.VMEM((hp, bq // FLASH_SUB, FLASH_SUB, bq), F32)] * 2
                        + [pltpu.VMEM((hp, 1, bq), F32)] * 2
                        + [pltpu.VMEM((hp, 2 * LANES if mode == "fox" else LANES, bq), BF16),
                           pltpu.VMEM((hp, 1, bq), F32)]),
        compiler_params=_cparams(("arbitrary", "arbitrary", "arbitrary")),
        name="flash_" + mode,
    )(*args)


CK_B = 2 * CHUNK
CK_NW = N_LEFT_CHUNKS * CHUNK // CK_B + 1
CK_EXT = (CK_NW + 1) * CK_B


def _chunk_kernel(q_ref, k_ref, vt_ref, ext_ref, o_ref, tab_ref, s_ref, *, hp, nq):
    i = pl.program_id(1)

    @pl.when(i == 0)
    def _():
        jj = lax.broadcasted_iota(jnp.int32, (CK_B, CK_B), 0)
        rr = lax.broadcasted_iota(jnp.int32, (CK_B, CK_B), 1)
        for h in range(hp):
            for w in range(CK_NW):
                a = (CK_NW - 1 - w) * CK_B
                g = jnp.broadcast_to(ext_ref[h:h + 1, a:a + 2 * CK_B], (CK_B, 2 * CK_B))
                t = pltpu.roll(g, CK_B, 1, stride=1, stride_axis=0)[:, :CK_B]
                if w == 0:
                    t = jnp.where((rr >= CHUNK) & (jj < CHUNK), NEG, t)
                if w == CK_NW - 1:
                    t = jnp.where((rr < CHUNK) & (jj >= CHUNK), NEG, t)
                tab_ref[h, w * CK_B:(w + 1) * CK_B, :] = t

    kcols = [slice(LANES * (h // 2), LANES * (h // 2 + 1)) for h in range(hp)]
    firsts = [i * nq + u - (CK_NW - 1) for u in range(nq)]

    def finish():
        for u in range(nq):
            kbc = [jnp.maximum(firsts[u] + w, 0) for w in range(CK_NW)]
            outs = []
            for h in range(hp):
                vt = jnp.concatenate([vt_ref[kbc[w], HEAD_DIM * h:HEAD_DIM * (h + 1), :]
                                      for w in range(CK_NW)], axis=1)
                outs.append(_softmax_out(_softmax_step([lambda u=u, h=h: s_ref[u, h]], [vt],
                                                       _softmax_init(CK_B))))
            oT = jnp.concatenate(outs, axis=0)
            o_ref[u * CK_B:(u + 1) * CK_B, :] = oT.T.astype(o_ref.dtype)

    def pair_scores(u, p, ks, nrows):
        q2 = q_ref[u * CK_B:(u + 1) * CK_B, kcols[2 * p]]
        qq = jnp.concatenate([_pair_mask_q(q2, 0), _pair_mask_q(q2, 1)], axis=0)
        return lax.dot_general(k_ref[pl.ds(ks, nrows), kcols[2 * p]], qq, _NT,
                               preferred_element_type=F32)

    @pl.when(firsts[0] >= 0)
    def _():
        for u in range(nq):
            ks = pl.multiple_of(firsts[u] * CK_B, CK_B)
            for p in range(hp // 2):
                sT = pair_scores(u, p, ks, CK_NW * CK_B)
                for j in range(2):
                    s_ref[u, 2 * p + j] = sT[:, j * CK_B:(j + 1) * CK_B] + tab_ref[2 * p + j]
        finish()

    @pl.when(firsts[0] < 0)
    def _():
        for u in range(nq):
            for p in range(hp // 2):
                for w in range(CK_NW):
                    rows = slice(w * CK_B, (w + 1) * CK_B)
                    ks = pl.multiple_of(jnp.maximum(firsts[u] + w, 0) * CK_B, CK_B)
                    sT = pair_scores(u, p, ks, CK_B)
                    for j in range(2):
                        s_ref[u, 2 * p + j, rows, :] = jnp.where(
                            firsts[u] + w >= 0,
                            sT[:, j * CK_B:(j + 1) * CK_B] + tab_ref[2 * p + j, rows, :], NEG)
        finish()


def _chunk_attn(main, q_col0, k_col0, vt_arr, vt_row0, ext, n_heads, nq=4):
    B, S, _ = main.shape
    hp = n_heads
    qw, vw = HEAD_DIM * hp, HEAD_DIM * hp
    return pl.pallas_call(
        functools.partial(_chunk_kernel, hp=hp, nq=nq),
        grid=(B, S // (nq * CK_B)),
        in_specs=[
            pl.BlockSpec((None, nq * CK_B, qw), lambda b, i: (b, i, q_col0 // qw)),
            pl.BlockSpec((None, S, qw), lambda b, i: (b, 0, k_col0 // qw)),
            pl.BlockSpec((None, S // VT_BLK, vw, VT_BLK), lambda b, i: (b, 0, vt_row0 // vw, 0)),
            pl.BlockSpec((hp, CK_EXT), lambda b, i: (0, 0)),
        ],
        out_specs=pl.BlockSpec((None, nq * CK_B, vw), lambda b, i: (b, i, 0)),
        out_shape=jax.ShapeDtypeStruct((B, S, n_heads * HEAD_DIM), BF16),
        scratch_shapes=[pltpu.VMEM((hp, CK_NW * CK_B, CK_B), F32),
                        pltpu.VMEM((nq, hp, CK_NW * CK_B, CK_B), F32)],
        compiler_params=_cparams(("arbitrary", "arbitrary")),
        name="chunk_attn",
    )(main, main, vt_arr, ext)


def _sb_kernel(q_ref, k_ref, vt_ref, o_ref, z_ref, lb_ref, sfx_ref, *, hp, bq):
    qs = pl.program_id(1) * bq
    nsub = bq // SUB
    row = lax.broadcasted_iota(jnp.int32, (SUB, bq), 0)
    col = lax.broadcasted_iota(jnp.int32, (SUB, bq), 1)
    ur = lax.broadcasted_iota(jnp.int32, (SUB, 2 * SUB), 0)
    uc = lax.broadcasted_iota(jnp.int32, (SUB, 2 * SUB), 1) & (SUB - 1)
    upper2 = jnp.where(uc > ur, 1.0, 0.0).astype(BF16)
    kcols = [slice(LANES * (h // 2), LANES * (h // 2 + 1)) for h in range(hp)]
    qms = [_pair_mask_q(q_ref[:, kcols[h]], h % 2) for h in range(hp)]

    def step(kb, carries, masked, live=None):
        ks = pl.multiple_of(kb * bq, bq)
        for h in range(hp):
            z_ref[h] = lax.dot_general(k_ref[pl.ds(ks, bq), kcols[h]], qms[h], _NT,
                                       preferred_element_type=F32)
        first_col = [c * SUB if masked else 0 for c in range(nsub)]

        def widen(x, c):
            if first_col[c] == 0:
                return x
            return jnp.concatenate([jnp.zeros((x.shape[0], first_col[c]), x.dtype), x], axis=1)

        totals = []
        for h in range(hp):
            tot = []
            for c in range(nsub):
                rows, cols = slice(c * SUB, (c + 1) * SUB), slice(first_col[c], bq)
                z = z_ref[h, rows, cols]
                l1 = jnp.log(1.0 + jnp.exp(-jnp.abs(z)))
                log_beta = jnp.minimum(z, 0.0) - l1
                log_keep = log_beta - z
                if masked:
                    valid = (ks + c * SUB + row < qs + col)[:, cols]
                    log_keep = jnp.where(valid, log_keep, 0.0)
                lb_ref[h, rows, cols] = log_beta
                hi = log_keep.astype(BF16)
                lo = (log_keep - hi.astype(F32)).astype(BF16)
                sfx = jnp.dot(upper2, jnp.concatenate([hi, lo], axis=0),
                              preferred_element_type=F32)
                sfx_ref[h, rows, cols] = sfx
                tot.append(widen(sfx[0:1, :] + log_keep[0:1, :], c))
            totals.append(tot)
        out = []
        for h in range(hp):
            tail, acc = carries[h]
            if live is not None:
                tail = jnp.where(live, tail, NEG)
            parts = [None] * nsub
            for c in range(nsub - 1, -1, -1):
                rows, cols = slice(c * SUB, (c + 1) * SUB), slice(first_col[c], bq)
                a = jnp.exp(lb_ref[h, rows, cols] + sfx_ref[h, rows, cols] + tail[:, cols])
                if masked:
                    a = jnp.where((ks + c * SUB + row < qs + col)[:, cols], a, 0.0)
                parts[c] = a.astype(BF16)
                tail = tail + totals[h][c]
            vts = [vt_ref[kb * nsub + c, HEAD_DIM * h:HEAD_DIM * (h + 1), :] for c in range(nsub)]
            if masked:
                for c in range(nsub):
                    acc = acc + widen(jnp.dot(vts[c], parts[c], preferred_element_type=F32), c)
            else:
                acc = acc + jnp.dot(jnp.concatenate(vts, axis=1), jnp.concatenate(parts, axis=0),
                                    preferred_element_type=F32)
            out.append((tail, acc))
        return tuple(out)

    n_full = qs // bq
    carries = tuple((jnp.zeros((1, bq), F32), jnp.zeros((HEAD_DIM, bq), F32))
                    for _ in range(hp))
    carries = step(n_full, carries, True)
    carries = step(jnp.maximum(n_full - 1, 0), carries, False, live=n_full >= 1)

    def cond(state):
        kb, carries = state
        tail_max = carries[0][0]
        for h in range(1, hp):
            tail_max = jnp.maximum(tail_max, carries[h][0])
        return (kb >= 0) & (jnp.max(tail_max) > SB_ZERO_LOG)

    def body(state):
        kb, carries = state
        return kb - 1, step(kb, carries, False)

    _, carries = lax.while_loop(cond, body, (n_full - 2, carries))
    _store_heads(o_ref, [acc for (_, acc) in carries])


def _sb_attn(main, q_col0, k_col0, vt_arr, vt_row0, n_heads, bq=256):
    B, S, _ = main.shape
    hp = n_heads
    qw = HEAD_DIM * hp
    return pl.pallas_call(
        functools.partial(_sb_kernel, hp=hp, bq=bq),
        grid=(B, S // bq),
        in_specs=[
            pl.BlockSpec((None, bq, qw), lambda b, i: (b, i, q_col0 // qw)),
            pl.BlockSpec((None, S, qw), lambda b, i: (b, 0, k_col0 // qw)),
            pl.BlockSpec((None, S // VT_BLK, qw, VT_BLK), lambda b, i: (b, 0, vt_row0 // qw, 0)),
        ],
        out_specs=pl.BlockSpec((None, bq, qw), lambda b, i: (b, i, 0)),
        out_shape=jax.ShapeDtypeStruct((B, S, n_heads * HEAD_DIM), BF16),
        scratch_shapes=[pltpu.VMEM((hp, bq, bq), F32)] * 3,
        compiler_params=_cparams(("arbitrary", "arbitrary")),
        name="sb_attn",
    )(main, main, vt_arr)


def _mla_prep_kernel(aux_ref, pos_ref, invf_ref, qn_ref, kvn_ref, wuq_ref, wuqr_ref,
                     wk_ref, wvt_ref, oq_ref, ok_ref, ovt_ref, *, q_scale):
    ang = pos_ref[...] * invf_ref[...]
    cos4, sin4 = jnp.cos(ang), jnp.sin(ang)
    lane = lax.broadcasted_iota(jnp.int32, ang.shape, 1)
    rotary = (lane >= NOPE_DIM) & (lane < NOPE_DIM + ROPE_DIM)
    cos_rows, sin_rows = [], []
    for m in range(LANES // ROPE_DIM):
        shift = (NOPE_DIM - ROPE_DIM * m) % LANES
        cm = cos4 if shift == 0 else pltpu.roll(cos4, shift, 1)
        sm = sin4 if shift == 0 else pltpu.roll(sin4, shift, 1)
        cos_rows.append(jnp.where(rotary, cm, 1.0))
        sin_rows.append(jnp.where(rotary, sm, 0.0))
    cos = jnp.concatenate(cos_rows, axis=0)
    sin = jnp.concatenate(sin_rows, axis=0)
    cq = _rms_bf16(aux_ref[:, 0:Q_LORA], qn_ref[...])
    ckv = _rms_bf16(aux_ref[:, Q_LORA:Q_LORA + KV_LORA], kvn_ref[...])
    o = Q_LORA + KV_LORA
    k_rope = aux_ref[:, o:o + LANES] * cos + aux_ref[:, o + LANES:o + 2 * LANES] * sin
    cos_q, sin_q = cos * q_scale, sin * q_scale
    n_heads = oq_ref.shape[-1] // LANES
    for h in range(0, n_heads, 2):
        cols = slice(h * LANES, (h + 2) * LANES)
        qa = jnp.dot(cq, wuq_ref[:, cols], preferred_element_type=F32)
        qb = jnp.dot(cq, wuqr_ref[:, cols], preferred_element_type=F32)
        kn = jnp.dot(ckv, wk_ref[:, cols], preferred_element_type=F32)
        for d in range(2):
            c1 = slice(d * LANES, (d + 1) * LANES)
            c2 = slice((h + d) * LANES, (h + d + 1) * LANES)
            oq_ref[:, c2] = (qa[:, c1] * cos_q + qb[:, c1] * sin_q).astype(BF16)
            ok_ref[:, c2] = (kn[:, c1] + k_rope).astype(BF16)
    nv = wvt_ref.shape[0]
    for r in range(0, nv, 256):
        vt = lax.dot_general(wvt_ref[r:r + 256, :], ckv, _NT,
                             preferred_element_type=F32).astype(BF16)
        _store_vt(ovt_ref, vt, r)


MLA_TM = 512


def _mla_prep(aux, pos, invf, qn, kvn, wuq, wuqr, wk, wvt, q_scale, tm=MLA_TM):
    B, S, na = aux.shape
    nq, nv = wuq.shape[1], wvt.shape[0]
    full = lambda a: pl.BlockSpec(a.shape, lambda b, i: (0,) * a.ndim)
    return pl.pallas_call(
        functools.partial(_mla_prep_kernel, q_scale=q_scale),
        grid=(B, S // tm),
        in_specs=[
            pl.BlockSpec((None, tm, na), lambda b, i: (b, i, 0)),
            pl.BlockSpec((None, None) + pos.shape[2:], lambda b, i: (b, i, 0, 0)),
            full(invf), full(qn), full(kvn), full(wuq), full(wuqr), full(wk), full(wvt),
        ],
        out_specs=[
            pl.BlockSpec((None, tm, nq), lambda b, i: (b, i, 0)),
            pl.BlockSpec((None, tm, nq), lambda b, i: (b, i, 0)),
            pl.BlockSpec((None, tm // VT_BLK, nv, VT_BLK), lambda b, i: (b, i, 0, 0)),
        ],
        out_shape=[
            jax.ShapeDtypeStruct((B, S, nq), BF16),
            jax.ShapeDtypeStruct((B, S, nq), BF16),
            jax.ShapeDtypeStruct((B, S // VT_BLK, nv, VT_BLK), BF16),
        ],
        compiler_params=_cparams(("arbitrary", "arbitrary")),
        name="mla_prep",
    )(aux, pos, invf, qn, kvn, wuq, wuqr, wk, wvt)


def _mlp_kernel(x_ref, oa_ref, ob_ref, wo_ref, g_ref, wu_ref, wd_ref, gf_ref, out_ref,
                x1_ref, h_ref, a_ref, *, final_norm, tf):
    na = oa_ref.shape[-1]
    x1 = (x_ref[...]
          + jnp.dot(oa_ref[...], wo_ref[0:na, :], preferred_element_type=F32)
          + jnp.dot(ob_ref[...], wo_ref[na:, :], preferred_element_type=F32))
    x1_ref[...] = x1
    h_ref[...] = _rms_bf16(x1, g_ref[...])
    for f in range(0, wu_ref.shape[1], tf):
        u = jnp.dot(h_ref[...], wu_ref[:, f:f + tf], preferred_element_type=F32)
        a_ref[:, f:f + tf] = jnp.square(jnp.maximum(u, 0.0)).astype(BF16)
    y = x1_ref[...] + jnp.dot(a_ref[...], wd_ref[...], preferred_element_type=F32)
    if final_norm:
        ms = jnp.mean(y * y, axis=-1, keepdims=True)
        y = y * lax.rsqrt(ms + EPS) * gf_ref[...]
    out_ref[...] = y


def _mlp(x2, oa, ob, wo, g, wu, wd, gf, final_norm, tm=1024, tf=1024):
    T, D = x2.shape
    F = wu.shape[1]
    na, nb = oa.shape[1], ob.shape[1]
    const = lambda shape: pl.BlockSpec(shape, lambda i: (0, 0), pipeline_mode=pl.Buffered(1))
    return pl.pallas_call(
        functools.partial(_mlp_kernel, final_norm=final_norm, tf=tf),
        grid=(T // tm,),
        in_specs=[
            pl.BlockSpec((tm, D), lambda i: (i, 0)),
            pl.BlockSpec((tm, na), lambda i: (i, 0)),
            pl.BlockSpec((tm, nb), lambda i: (i, 0)),
            const((na + nb, D)),
            const((1, D)),
            const((D, F)),
            const((F, D)),
            const((1, D)),
        ],
        out_specs=pl.BlockSpec((tm, D), lambda i: (i, 0)),
        out_shape=jax.ShapeDtypeStruct((T, D), F32),
        scratch_shapes=[pltpu.VMEM((tm, D), F32), pltpu.VMEM((tm, D), BF16),
                        pltpu.VMEM((tm, F), BF16)],
        compiler_params=_cparams(("arbitrary",)),
        name="mlp",
    )(x2, oa, ob, wo, g.reshape(1, D), wu, wd, gf.reshape(1, D))


def _pad_cols(w, n):
    return jnp.pad(w, ((0, 0), (0, n - w.shape[1])))


def _rot_cols(w):
    half = ROPE_DIM // 2
    return jnp.concatenate([-w[:, half:], w[:, :half]], axis=1)


def _rope_slab(w):
    z = jnp.zeros((w.shape[0], NOPE_DIM), w.dtype)
    return jnp.concatenate([z, w, jnp.zeros((w.shape[0], LANES - NOPE_DIM - ROPE_DIM), w.dtype)], axis=1)


def _even_layer(x, g_mix, w_in, b_forget, rel_bias, w_out, g_mlp, w_up, w_down, g_final,
                final_norm):
    B, S, D = x.shape
    hf, hc = b_forget.shape[0], rel_bias.shape[0]
    wf, wc = hf * HEAD_DIM, hc * HEAD_DIM
    o = np.cumsum([0, wf, wf, wf, hf, wc, wc, wc])
    qa, ka, va, fa, qb, kb, vb = [w_in[:, o[n]:o[n + 1]] for n in range(7)]
    q_scale = HEAD_DIM ** -0.5 * LOG2E
    wm = jnp.concatenate([qa * q_scale, ka, qb * q_scale, kb], axis=1).astype(BF16)
    wvt = jnp.concatenate([va, vb], axis=1).T.astype(BF16)
    wa = _pad_cols(fa, LANES).astype(BF16)
    main, vt, aux = _inproj(x, g_mix, wm, wvt, wa)

    q_aug, k_aug = _logcum(aux, _pad_cols(b_forget.reshape(1, hf), LANES), hf)
    o_a = _flash("fox", main, 0, main, wf, vt, 0, (q_aug, k_aug), hf)

    assert rel_bias.shape[1] == CHUNK + REL_CLIP
    right = CK_EXT - rel_bias.shape[1] - (CHUNK + 1)
    ext = jnp.pad(rel_bias * LOG2E, ((0, 0), (CHUNK + 1, right)), mode="edge")
    o_b = _chunk_attn(main, 2 * wf, 2 * wf + wc, vt, wf, ext, hc)

    y = _mlp(x.reshape(B * S, D), o_a.reshape(B * S, wf), o_b.reshape(B * S, wc),
             w_out.astype(BF16), g_mlp, w_up.astype(BF16), w_down.astype(BF16), g_final,
             final_norm)
    return y.reshape(B, S, D)


def _odd_layer(x, positions, g_mix, w_in, q_norm, kv_norm, w_uq, w_ukv, w_out, g_mlp, w_up,
               w_down, g_final, final_norm):
    B, S, D = x.shape
    hm = w_ukv.shape[1] // (NOPE_DIM + HEAD_DIM)
    ws = w_in.shape[1] - Q_LORA - KV_LORA - ROPE_DIM
    hs = (ws // 3) // HEAD_DIM
    wsb = hs * HEAD_DIM
    o = np.cumsum([0, wsb, wsb, wsb, Q_LORA, KV_LORA, ROPE_DIM])
    qc, kc, vc, w_cq, w_ckv, w_kr = [w_in[:, o[n]:o[n + 1]] for n in range(6)]
    wm = jnp.concatenate([qc * HEAD_DIM ** -0.5, kc], axis=1).astype(BF16)
    wa = jnp.concatenate([w_cq, w_ckv, _rope_slab(w_kr), _rope_slab(_rot_cols(w_kr))],
                         axis=1).astype(BF16)
    main, vt, aux = _inproj(x, g_mix, wm, vc.T.astype(BF16), wa)
    o_c = _sb_attn(main, 0, wsb, vt, 0, hs)

    dq = NOPE_DIM + ROPE_DIM
    wuq3 = w_uq.reshape(Q_LORA, hm, dq)
    nope, ropew = wuq3[:, :, :NOPE_DIM], wuq3[:, :, NOPE_DIM:]
    zq = jnp.zeros((Q_LORA, hm, LANES - dq), w_uq.dtype)
    wuq = jnp.concatenate([nope, ropew, zq], axis=2).reshape(Q_LORA, hm * LANES).astype(BF16)
    half = ROPE_DIM // 2
    ropr = jnp.concatenate([-ropew[:, :, half:], ropew[:, :, :half]], axis=2)
    wuqr = jnp.concatenate([jnp.zeros_like(nope), ropr, zq], axis=2)
    wuqr = wuqr.reshape(Q_LORA, hm * LANES).astype(BF16)
    wkv3 = w_ukv.reshape(KV_LORA, hm, NOPE_DIM + HEAD_DIM)
    wk = jnp.concatenate([wkv3[:, :, :NOPE_DIM],
                          jnp.zeros((KV_LORA, hm, LANES - NOPE_DIM), w_ukv.dtype)], axis=2)
    wk = wk.reshape(KV_LORA, hm * LANES).astype(BF16)
    wv_t = wkv3[:, :, NOPE_DIM:].reshape(KV_LORA, hm * HEAD_DIM).T.astype(BF16)
    freqs = (ROPE_THETA ** (-jnp.arange(half, dtype=F32) / half))
    invf = jnp.tile(freqs, 2 * LANES // ROPE_DIM).reshape(1, LANES)
    groups = LANES // ROPE_DIM
    pos = positions.astype(F32).reshape(B, S // MLA_TM, groups, MLA_TM // groups)
    pos = jnp.repeat(jnp.swapaxes(pos, 2, 3), ROPE_DIM, axis=-1)
    qm, km, vtm = _mla_prep(aux, pos, invf, q_norm.reshape(1, Q_LORA),
                            kv_norm.reshape(1, KV_LORA), wuq, wuqr, wk, wv_t,
                            dq ** -0.5 * LOG2E)
    o_d = _flash("mla", qm, 0, km, 0, vtm, 0, None, hm)

    y = _mlp(x.reshape(B * S, D), o_c.reshape(B * S, wsb), o_d.reshape(B * S, hm * HEAD_DIM),
             w_out.astype(BF16), g_mlp, w_up.astype(BF16), w_down.astype(BF16), g_final,
             final_norm)
    return y.reshape(B, S, D)


def kernel(x, positions, norm_mix, norm_mlp, norm_final, w_in_ab, b_forget, rel_bias, w_out_ab,
           w_in_cd, q_norm, kv_norm, w_uq, w_ukv, w_out_cd, w_up, w_down):
    depth = norm_mix.shape[0]
    for layer in range(depth):
        last = layer == depth - 1
        if layer % 2 == 0:
            e = layer // 2
            x = _even_layer(x, norm_mix[layer], w_in_ab[e], b_forget[e], rel_bias[e], w_out_ab[e],
                            norm_mlp[layer], w_up[layer], w_down[layer], norm_final, last)
        else:
            o = layer // 2
            x = _odd_layer(x, positions, norm_mix[layer], w_in_cd[o], q_norm[o], kv_norm[o],
                           w_uq[o], w_ukv[o], w_out_cd[o], norm_mlp[layer], w_up[layer],
                           w_down[layer], norm_final, last)
    return x
```

```python
import functools
import math

import numpy as np
import jax
import jax.numpy as jnp
from jax import lax
from jax.experimental import pallas as pl
from jax.experimental.pallas import tpu as pltpu

F32 = jnp.float32
BF16 = jnp.bfloat16

EPS = 1e-6
HEAD_DIM = 64
CHUNK = 64
N_LEFT_CHUNKS = 8
REL_CLIP = 256
ROPE_DIM = 32
NOPE_DIM = 64
ROPE_THETA = 10000.0
Q_LORA = 384
KV_LORA = 256

LANES = 128
VT_BLK = LANES
SUB = LANES
FLASH_HP = 8
FLASH_BQ = 512
FLASH_SUB = 256
FIXED_WIDTHS = (4, 2, 1)
SAFE_GAP = 80.0
NEG = -1e30
LOG2E = math.log2(math.e)
SB_ZERO_LOG = -104.0
VMEM_LIMIT = 56 * 1024 * 1024

_NT = (((1,), (1,)), ((), ()))


def _cparams(sem):
    return pltpu.CompilerParams(dimension_semantics=sem, vmem_limit_bytes=VMEM_LIMIT)


def _rms_bf16(x, g):
    ms = jnp.mean(x * x, axis=-1, keepdims=True)
    return (x * lax.rsqrt(ms + EPS) * g).astype(BF16)


def _store_vt(ovt_ref, vt, row0):
    rows, tm = vt.shape
    for c in range(tm // VT_BLK):
        ovt_ref[c, row0:row0 + rows, :] = vt[:, c * VT_BLK:(c + 1) * VT_BLK]


def _inproj_kernel(x_ref, g_ref, wm_ref, wvt_ref, wa_ref, om_ref, ovt_ref, oa_ref):
    h = _rms_bf16(x_ref[...], g_ref[...])
    nm = om_ref.shape[-1]
    for c in range(0, nm, 512):
        om_ref[:, c:c + 512] = jnp.dot(
            h, wm_ref[:, c:c + 512], preferred_element_type=F32).astype(BF16)
    nv = wvt_ref.shape[0]
    for r in range(0, nv, 256):
        vt = lax.dot_general(wvt_ref[r:r + 256, :], h, _NT,
                             preferred_element_type=F32).astype(BF16)
        _store_vt(ovt_ref, vt, r)
    oa_ref[...] = jnp.dot(h, wa_ref[...], preferred_element_type=F32)


def _inproj(x, g, wm, wvt, wa, tm=1024):
    B, S, D = x.shape
    nm, nv, na = wm.shape[1], wvt.shape[0], wa.shape[1]
    return pl.pallas_call(
        _inproj_kernel,
        grid=(B, S // tm),
        in_specs=[
            pl.BlockSpec((None, tm, D), lambda b, i: (b, i, 0)),
            pl.BlockSpec((1, D), lambda b, i: (0, 0)),
            pl.BlockSpec((D, nm), lambda b, i: (0, 0)),
            pl.BlockSpec((nv, D), lambda b, i: (0, 0)),
            pl.BlockSpec((D, na), lambda b, i: (0, 0)),
        ],
        out_specs=[
            pl.BlockSpec((None, tm, nm), lambda b, i: (b, i, 0)),
            pl.BlockSpec((None, tm // VT_BLK, nv, VT_BLK), lambda b, i: (b, i, 0, 0)),
            pl.BlockSpec((None, tm, na), lambda b, i: (b, i, 0)),
        ],
        out_shape=[
            jax.ShapeDtypeStruct((B, S, nm), BF16),
            jax.ShapeDtypeStruct((B, S // VT_BLK, nv, VT_BLK), BF16),
            jax.ShapeDtypeStruct((B, S, na), F32),
        ],
        compiler_params=_cparams(("arbitrary", "arbitrary")),
        name="inproj",
    )(x, g.reshape(1, D), wm, wvt, wa)


def _split3(x):
    hi = x.astype(BF16)
    r = x - hi.astype(F32)
    mid = r.astype(BF16)
    lo = (r - mid.astype(F32)).astype(BF16)
    return hi, mid, lo


AUG_W = 8


def _logcum_kernel(fa_ref, b_ref, pq_ref, pk_ref, oneq_ref, onek_ref, oq_ref, ok_ref, carry_ref):
    @pl.when(pl.program_id(1) == 0)
    def _():
        carry_ref[...] = jnp.zeros_like(carry_ref)

    z = fa_ref[...] + b_ref[...]
    lf = jnp.minimum(z, 0.0) - jnp.log(1.0 + jnp.exp(-jnp.abs(z)))
    tc = lf.shape[0]
    r = lax.broadcasted_iota(jnp.int32, (tc, tc), 0)
    c = lax.broadcasted_iota(jnp.int32, (tc, tc), 1)
    tri = jnp.where(r >= c, 1.0, 0.0).astype(BF16)
    cs = carry_ref[...]
    for part in _split3(lf):
        cs = cs + jnp.dot(tri, part, preferred_element_type=F32)
    carry_ref[...] = cs[tc - 1:tc, :]
    qa, ka = oneq_ref[...], onek_ref[...]
    for n, part in enumerate(_split3(cs * LOG2E)):
        qa = qa + jnp.dot(part, pq_ref[n], preferred_element_type=F32)
        ka = ka + jnp.dot(part, pk_ref[n], preferred_element_type=F32)
    oq_ref[...] = qa.astype(BF16)
    ok_ref[...] = ka.astype(BF16)


def _logcum(fa, bias, n_heads, tc=512):
    B, S, W = fa.shape
    na = LANES
    pq = np.zeros((3, W, na), np.float32)
    pk = np.zeros((3, W, na), np.float32)
    oneq = np.zeros((1, na), np.float32)
    onek = np.zeros((1, na), np.float32)
    for h in range(n_heads):
        base = h * AUG_W
        for n in range(3):
            pq[n, h, base + n] = 1.0
            pk[n, h, base + 3 + n] = -1.0
        oneq[0, base + 3:base + 6] = 1.0
        onek[0, base:base + 3] = 1.0
    const = lambda a: pl.BlockSpec(a.shape, lambda b, i: (0,) * a.ndim)
    args = [jnp.asarray(pq, BF16), jnp.asarray(pk, BF16), jnp.asarray(oneq), jnp.asarray(onek)]
    return pl.pallas_call(
        _logcum_kernel,
        grid=(B, S // tc),
        in_specs=[pl.BlockSpec((None, tc, W), lambda b, i: (b, i, 0)),
                  pl.BlockSpec((1, W), lambda b, i: (0, 0))] + [const(a) for a in args],
        out_specs=[pl.BlockSpec((None, tc, na), lambda b, i: (b, i, 0))] * 2,
        out_shape=[jax.ShapeDtypeStruct((B, S, na), BF16)] * 2,
        scratch_shapes=[pltpu.VMEM((1, W), F32)],
        compiler_params=_cparams(("arbitrary", "arbitrary")),
        name="logcum",
    )(fa, bias, *args)


def _pair_mask_q(q2, j):
    lane = lax.broadcasted_iota(jnp.int32, q2.shape, 1)
    keep = (lane >= HEAD_DIM * j) & (lane < HEAD_DIM * (j + 1))
    return jnp.where(keep, q2, jnp.zeros_like(q2))


ONES_ROWS = 16


def _softmax_step(tiles, vts, carry, tile_max=None):
    m, acc = carry
    if tile_max is not None:
        m_new = jnp.maximum(m, tile_max)
    else:
        m_new = m
        for tile in tiles:
            m_new = jnp.maximum(m_new, jnp.max(tile(), axis=0, keepdims=True))
    alpha = jnp.exp2(m - m_new)
    pv = None
    for tile, vt in zip(tiles, vts):
        p = jnp.exp2(tile() - m_new).astype(BF16)
        vt1 = jnp.concatenate([vt, jnp.ones((ONES_ROWS, vt.shape[1]), BF16)], axis=0)
        d = jnp.dot(vt1, p, preferred_element_type=F32)
        pv = d if pv is None else pv + d
    return m_new, alpha * acc + pv


def _softmax_init(bq):
    return (jnp.full((1, bq), NEG, F32), jnp.zeros((HEAD_DIM + ONES_ROWS, bq), F32))


def _softmax_out(carry):
    _, acc = carry
    return acc[0:HEAD_DIM] / acc[HEAD_DIM:HEAD_DIM + 1]


def _store_heads(o_ref, outs):
    oT = jnp.concatenate(outs, axis=0)
    o_ref[...] = oT.T.astype(o_ref.dtype)


def _flash_kernel(*refs, mode, hp, bq):
    if mode == "fox":
        q_ref, k_ref, vt_ref, qaug_ref, kaug_ref, o_ref = refs[:6]
    else:
        q_ref, k_ref, vt_ref, o_ref = refs[:4]
    sa_ref, sb_ref, ma_ref, mb_ref, qt_ref, kn_ref = refs[-6:]
    qs = pl.program_id(2) * bq
    sub = FLASH_SUB
    row = lax.broadcasted_iota(jnp.int32, (sub, bq), 0)
    col = lax.broadcasted_iota(jnp.int32, (sub, bq), 1)
    if mode == "fox":
        kcols = [slice(LANES * (h // 2), LANES * (h // 2 + 1)) for h in range(hp)]
        lane = lax.broadcasted_iota(jnp.int32, (bq, LANES), 1)
        qa = qaug_ref[...]
        qms = []
        for h in range(hp):
            first = AUG_W * (pl.program_id(1) * hp + h)
            own = (lane >= first) & (lane < first + AUG_W)
            qms.append(jnp.concatenate(
                [_pair_mask_q(q_ref[:, kcols[h]], h % 2),
                 jnp.where(own, qa, jnp.zeros_like(qa))], axis=1))
    else:
        kcols = [slice(LANES * h, LANES * (h + 1)) for h in range(hp)]
        qms = [q_ref[:, kcols[h]] for h in range(hp)]
    for h in range(hp):
        qt_ref[h] = qms[h].T

    @pl.when(pl.program_id(2) == 0)
    def _():
        klane = lax.broadcasted_iota(jnp.int32, (1, LANES), 1)
        for h in range(hp):
            kabs = jnp.max(jnp.abs(k_ref[:, kcols[h]].astype(F32)), axis=0, keepdims=True)
            sq = kabs * kabs
            if mode == "fox":
                sq = jnp.where((klane >= HEAD_DIM * (h % 2)) & (klane < HEAD_DIM * (h % 2 + 1)),
                               sq, 0.0)
            kn_ref[h] = jnp.broadcast_to(jnp.sqrt(jnp.sum(sq, axis=1, keepdims=True)), (1, bq))

    def scores(sb, h, masked, col0=0):
        ks = pl.multiple_of(sb * sub, sub)
        k = k_ref[pl.ds(ks, sub), kcols[h]]
        if mode == "fox":
            k = jnp.concatenate([k, kaug_ref[pl.ds(ks, sub), :]], axis=1)
        sT = jnp.dot(k, qt_ref[h, :, col0:], preferred_element_type=F32)
        if masked and mode == "fox":
            sT = jnp.where((ks + row <= qs + col)[:, col0:], sT, NEG)
        elif masked:
            sT = jnp.where((((ks + row) >> 6) <= ((qs + col) >> 6))[:, col0:], sT, NEG)
        return sT

    nsub = bq // sub
    nvt = sub // VT_BLK

    def produce(buf, sb0, masked, h):
        s_buf, m_buf = buf
        tile_max = None
        for c in range(nsub):
            col0 = c * sub if masked else 0
            sT = scores(sb0 + c, h, masked, col0)
            cm = jnp.max(sT, axis=0, keepdims=True)
            if col0:
                s_buf[h, c, :, 0:col0] = jnp.full((sub, col0), NEG, F32)
                cm = jnp.concatenate([jnp.full((1, col0), NEG, F32), cm], axis=1)
            s_buf[h, c, :, col0:] = sT
            tile_max = cm if tile_max is None else jnp.maximum(tile_max, cm)
        m_buf[h] = tile_max

    def consume(buf, sb0, carry, h):
        s_buf, m_buf = buf
        tiles = [lambda c=c: s_buf[h, c] for c in range(nsub)]
        vts = [jnp.concatenate([vt_ref[(sb0 + c) * nvt + v, HEAD_DIM * h:HEAD_DIM * (h + 1), :]
                                for v in range(nvt)], axis=1) for c in range(nsub)]
        return _softmax_step(tiles, vts, carry, tile_max=m_buf[h])

    def stage(cur, cur_sb, nxt, nxt_sb, carries):
        if nxt is not None:
            for h in range(hp):
                produce(nxt, nxt_sb, False, h)
        return tuple(consume(cur, cur_sb, carries[h], h) for h in range(hp))

    n = pl.program_id(2)
    diag_sb = qs // sub
    buf_a, buf_b = (sa_ref, ma_ref), (sb_ref, mb_ref)
    for h in range(hp):
        produce(buf_a, diag_sb, True, h)

    def pair(j, carries):
        carries = stage(buf_a, jnp.where(j == 0, diag_sb, (2 * j - 1) * nsub),
                        buf_b, 2 * j * nsub, carries)
        return stage(buf_b, 2 * j * nsub,
                     buf_a, jnp.minimum(2 * j + 1, n - 1) * nsub, carries)

    def online(_):
        carries = tuple(_softmax_init(bq) for _ in range(hp))
        carries = lax.fori_loop(0, (n + 1) // 2, pair, carries)
        carries = lax.cond(
            n % 2 == 0,
            lambda c: stage(buf_a, jnp.where(n == 0, diag_sb, (n - 1) * nsub), None, None, c),
            lambda c: c, carries)
        return jnp.concatenate([_softmax_out(c) for c in carries], axis=0)

    refs_ = []
    gap = None
    for h in range(hp):
        qf = qt_ref[h, 0:LANES, :].astype(F32)
        bound = jnp.sqrt(jnp.sum(qf * qf, axis=0, keepdims=True)) * kn_ref[h] * 1.01 + 1e-3
        refs_.append(bound)
        g = jnp.max(bound - ma_ref[h])
        gap = g if gap is None else jnp.maximum(gap, g)

    def weigh_add(acc, h, s, sb0, nblk, col0=0):
        p = jnp.exp2(s - refs_[h][:, col0:]).astype(BF16)
        vt = jnp.concatenate([vt_ref[sb0 * nvt + v, HEAD_DIM * h:HEAD_DIM * (h + 1), :]
                              for v in range(nblk * nvt)], axis=1)
        vt1 = jnp.concatenate([vt, jnp.ones((ONES_ROWS, nblk * sub), BF16)], axis=0)
        d = jnp.dot(vt1, p, preferred_element_type=F32)
        if col0:
            d = jnp.concatenate([jnp.zeros((d.shape[0], col0), F32), d], axis=1)
        return acc + d

    def fixed_reference(_):
        def run(kb, accs, nblk):
            ks = pl.multiple_of(kb * sub, sub)
            ss = []
            for h in range(hp):
                k = k_ref[pl.ds(ks, nblk * sub), kcols[h]]
                if mode == "fox":
                    k = jnp.concatenate([k, kaug_ref[pl.ds(ks, nblk * sub), :]], axis=1)
                ss.append(jnp.dot(k, qt_ref[h], preferred_element_type=F32))
            return tuple(weigh_add(accs[h], h, ss[h], kb, nblk) for h in range(hp))

        accs = tuple(jnp.zeros((HEAD_DIM + ONES_ROWS, bq), F32) for _ in range(hp))
        done = 0
        for width in FIXED_WIDTHS:
            trips = (diag_sb - done) // width
            accs = lax.fori_loop(0, trips,
                                 lambda j, a, done=done, width=width: run(done + j * width, a, width),
                                 accs)
            done = done + trips * width
        for c in range(nsub):
            accs = [weigh_add(accs[h], h, sa_ref[h, c, :, c * sub:], diag_sb + c, 1, c * sub)
                    for h in range(hp)]
        return jnp.concatenate([a[0:HEAD_DIM] / a[HEAD_DIM:HEAD_DIM + 1] for a in accs], axis=0)

    oT = lax.cond(gap <= SAFE_GAP, fixed_reference, online, None)
    o_ref[...] = oT.T.astype(o_ref.dtype)


def _flash(mode, q_arr, q_col0, k_arr, k_col0, vt_arr, vt_row0, extra, n_heads,
           hp=FLASH_HP, bq=FLASH_BQ):
    B, S, _ = q_arr.shape
    qw = (HEAD_DIM if mode == "fox" else LANES) * hp
    vw = HEAD_DIM * hp
    in_specs = [
        pl.BlockSpec((None, bq, qw), lambda b, g, i: (b, i, q_col0 // qw + g)),
        pl.BlockSpec((None, S, qw), lambda b, g, i: (b, 0, k_col0 // qw + g)),
        pl.BlockSpec((None, S // VT_BLK, vw, VT_BLK),
                     lambda b, g, i: (b, 0, vt_row0 // vw + g, 0)),
    ]
    args = [q_arr, k_arr, vt_arr]
    if mode == "fox":
        q_aug, k_aug = extra
        in_specs += [
            pl.BlockSpec((None, bq, LANES), lambda b, g, i: (b, i, 0)),
            pl.BlockSpec((None, S, LANES), lambda b, g, i: (b, 0, 0)),
        ]
        args += [q_aug, k_aug]
    return pl.pallas_call(
        functools.partial(_flash_kernel, mode=mode, hp=hp, bq=bq),
        grid=(B, n_heads // hp, S // bq),
        in_specs=in_specs,
        out_specs=pl.BlockSpec((None, bq, vw), lambda b, g, i: (b, i, g)),
        out_shape=jax.ShapeDtypeStruct((B, S, n_heads * HEAD_DIM), BF16),
        scratch_shapes=([pltpu.VMEM((hp, bq // FLASH_SUB, FLASH_SUB, bq), F32)] * 2
                        + [pltpu.VMEM((hp, 1, bq), F32)] * 2
                        + [pltpu.VMEM((hp, 2 * LANES if mode == "fox" else LANES, bq), BF16),
                           pltpu.VMEM((hp, 1, bq), F32)]),
        compiler_params=_cparams(("arbitrary", "arbitrary", "arbitrary")),
        name="flash_" + mode,
    )(*args)


CK_B = 2 * CHUNK
CK_NW = N_LEFT_CHUNKS * CHUNK // CK_B + 1
CK_EXT = (CK_NW + 1) * CK_B


def _chunk_kernel(q_ref, k_ref, vt_ref, ext_ref, o_ref, tab_ref, s_ref, *, hp, nq):
    i = pl.program_id(1)

    @pl.when(i == 0)
    def _():
        jj = lax.broadcasted_iota(jnp.int32, (CK_B, CK_B), 0)
        rr = lax.broadcasted_iota(jnp.int32, (CK_B, CK_B), 1)
        for h in range(hp):
            for w in range(CK_NW):
                a = (CK_NW - 1 - w) * CK_B
                g = jnp.broadcast_to(ext_ref[h:h + 1, a:a + 2 * CK_B], (CK_B, 2 * CK_B))
                t = pltpu.roll(g, CK_B, 1, stride=1, stride_axis=0)[:, :CK_B]
                if w == 0:
                    t = jnp.where((rr >= CHUNK) & (jj < CHUNK), NEG, t)
                if w == CK_NW - 1:
                    t = jnp.where((rr < CHUNK) & (jj >= CHUNK), NEG, t)
                tab_ref[h, w * CK_B:(w + 1) * CK_B, :] = t

    kcols = [slice(LANES * (h // 2), LANES * (h // 2 + 1)) for h in range(hp)]
    firsts = [i * nq + u - (CK_NW - 1) for u in range(nq)]

    def finish():
        for u in range(nq):
            kbc = [jnp.maximum(firsts[u] + w, 0) for w in range(CK_NW)]
            outs = []
            for h in range(hp):
                vt = jnp.concatenate([vt_ref[kbc[w], HEAD_DIM * h:HEAD_DIM * (h + 1), :]
                                      for w in range(CK_NW)], axis=1)
                outs.append(_softmax_out(_softmax_step([lambda u=u, h=h: s_ref[u, h]], [vt],
                                                       _softmax_init(CK_B))))
            oT = jnp.concatenate(outs, axis=0)
            o_ref[u * CK_B:(u + 1) * CK_B, :] = oT.T.astype(o_ref.dtype)

    def pair_scores(u, p, ks, nrows):
        q2 = q_ref[u * CK_B:(u + 1) * CK_B, kcols[2 * p]]
        qq = jnp.concatenate([_pair_mask_q(q2, 0), _pair_mask_q(q2, 1)], axis=0)
        return lax.dot_general(k_ref[pl.ds(ks, nrows), kcols[2 * p]], qq, _NT,
                               preferred_element_type=F32)

    @pl.when(firsts[0] >= 0)
    def _():
        for u in range(nq):
            ks = pl.multiple_of(firsts[u] * CK_B, CK_B)
            for p in range(hp // 2):
                sT = pair_scores(u, p, ks, CK_NW * CK_B)
                for j in range(2):
                    s_ref[u, 2 * p + j] = sT[:, j * CK_B:(j + 1) * CK_B] + tab_ref[2 * p + j]
        finish()

    @pl.when(firsts[0] < 0)
    def _():
        for u in range(nq):
            for p in range(hp // 2):
                for w in range(CK_NW):
                    rows = slice(w * CK_B, (w + 1) * CK_B)
                    ks = pl.multiple_of(jnp.maximum(firsts[u] + w, 0) * CK_B, CK_B)
                    sT = pair_scores(u, p, ks, CK_B)
                    for j in range(2):
                        s_ref[u, 2 * p + j, rows, :] = jnp.where(
                            firsts[u] + w >= 0,
                            sT[:, j * CK_B:(j + 1) * CK_B] + tab_ref[2 * p + j, rows, :], NEG)
        finish()


def _chunk_attn(main, q_col0, k_col0, vt_arr, vt_row0, ext, n_heads, nq=4):
    B, S, _ = main.shape
    hp = n_heads
    qw, vw = HEAD_DIM * hp, HEAD_DIM * hp
    return pl.pallas_call(
        functools.partial(_chunk_kernel, hp=hp, nq=nq),
        grid=(B, S // (nq * CK_B)),
        in_specs=[
            pl.BlockSpec((None, nq * CK_B, qw), lambda b, i: (b, i, q_col0 // qw)),
            pl.BlockSpec((None, S, qw), lambda b, i: (b, 0, k_col0 // qw)),
            pl.BlockSpec((None, S // VT_BLK, vw, VT_BLK), lambda b, i: (b, 0, vt_row0 // vw, 0)),
            pl.BlockSpec((hp, CK_EXT), lambda b, i: (0, 0)),
        ],
        out_specs=pl.BlockSpec((None, nq * CK_B, vw), lambda b, i: (b, i, 0)),
        out_shape=jax.ShapeDtypeStruct((B, S, n_heads * HEAD_DIM), BF16),
        scratch_shapes=[pltpu.VMEM((hp, CK_NW * CK_B, CK_B), F32),
                        pltpu.VMEM((nq, hp, CK_NW * CK_B, CK_B), F32)],
        compiler_params=_cparams(("arbitrary", "arbitrary")),
        name="chunk_attn",
    )(main, main, vt_arr, ext)


def _sb_kernel(q_ref, k_ref, vt_ref, o_ref, z_ref, lb_ref, sfx_ref, *, hp, bq):
    qs = pl.program_id(1) * bq
    nsub = bq // SUB
    row = lax.broadcasted_iota(jnp.int32, (SUB, bq), 0)
    col = lax.broadcasted_iota(jnp.int32, (SUB, bq), 1)
    ur = lax.broadcasted_iota(jnp.int32, (SUB, 2 * SUB), 0)
    uc = lax.broadcasted_iota(jnp.int32, (SUB, 2 * SUB), 1) & (SUB - 1)
    upper2 = jnp.where(uc > ur, 1.0, 0.0).astype(BF16)
    kcols = [slice(LANES * (h // 2), LANES * (h // 2 + 1)) for h in range(hp)]
    qms = [_pair_mask_q(q_ref[:, kcols[h]], h % 2) for h in range(hp)]

    def step(kb, carries, masked, live=None):
        ks = pl.multiple_of(kb * bq, bq)
        for h in range(hp):
            z_ref[h] = lax.dot_general(k_ref[pl.ds(ks, bq), kcols[h]], qms[h], _NT,
                                       preferred_element_type=F32)
        first_col = [c * SUB if masked else 0 for c in range(nsub)]

        def widen(x, c):
            if first_col[c] == 0:
                return x
            return jnp.concatenate([jnp.zeros((x.shape[0], first_col[c]), x.dtype), x], axis=1)

        totals = []
        for h in range(hp):
            tot = []
            for c in range(nsub):
                rows, cols = slice(c * SUB, (c + 1) * SUB), slice(first_col[c], bq)
                z = z_ref[h, rows, cols]
                l1 = jnp.log(1.0 + jnp.exp2(jnp.abs(z) * (-LOG2E)))
                log_beta = jnp.minimum(z, 0.0) - l1
                log_keep = log_beta - z
                if masked:
                    valid = (ks + c * SUB + row < qs + col)[:, cols]
                    log_keep = jnp.where(valid, log_keep, 0.0)
                lb_ref[h, rows, cols] = log_beta
                hi = log_keep.astype(BF16)
                lo = (log_keep - hi.astype(F32)).astype(BF16)
                sfx = jnp.dot(upper2, jnp.concatenate([hi, lo], axis=0),
                              preferred_element_type=F32)
                sfx_ref[h, rows, cols] = sfx
                tot.append(widen(sfx[0:1, :] + log_keep[0:1, :], c))
            totals.append(tot)
        out = []
        for h in range(hp):
            tail, acc = carries[h]
            if live is not None:
                tail = jnp.where(live, tail, NEG)
            parts = [None] * nsub
            for c in range(nsub - 1, -1, -1):
                rows, cols = slice(c * SUB, (c + 1) * SUB), slice(first_col[c], bq)
                a = jnp.exp(lb_ref[h, rows, cols] + sfx_ref[h, rows, cols] + tail[:, cols])
                if masked:
                    a = jnp.where((ks + c * SUB + row < qs + col)[:, cols], a, 0.0)
                parts[c] = a.astype(BF16)
                tail = tail + totals[h][c]
            vts = [vt_ref[kb * nsub + c, HEAD_DIM * h:HEAD_DIM * (h + 1), :] for c in range(nsub)]
            if masked:
                for c in range(nsub):
                    acc = acc + widen(jnp.dot(vts[c], parts[c], preferred_element_type=F32), c)
            else:
                acc = acc + jnp.dot(jnp.concatenate(vts, axis=1), jnp.concatenate(parts, axis=0),
                                    preferred_element_type=F32)
            out.append((tail, acc))
        return tuple(out)

    n_full = qs // bq
    carries = tuple((jnp.zeros((1, bq), F32), jnp.zeros((HEAD_DIM, bq), F32))
                    for _ in range(hp))
    carries = step(n_full, carries, True)
    carries = step(jnp.maximum(n_full - 1, 0), carries, False, live=n_full >= 1)

    def cond(state):
        kb, carries = state
        tail_max = carries[0][0]
        for h in range(1, hp):
            tail_max = jnp.maximum(tail_max, carries[h][0])
        return (kb >= 0) & (jnp.max(tail_max) > SB_ZERO_LOG)

    def body(state):
        kb, carries = state
        return kb - 1, step(kb, carries, False)

    _, carries = lax.while_loop(cond, body, (n_full - 2, carries))
    _store_heads(o_ref, [acc for (_, acc) in carries])


def _sb_attn(main, q_col0, k_col0, vt_arr, vt_row0, n_heads, bq=256):
    B, S, _ = main.shape
    hp = n_heads
    qw = HEAD_DIM * hp
    return pl.pallas_call(
        functools.partial(_sb_kernel, hp=hp, bq=bq),
        grid=(B, S // bq),
        in_specs=[
            pl.BlockSpec((None, bq, qw), lambda b, i: (b, i, q_col0 // qw)),
            pl.BlockSpec((None, S, qw), lambda b, i: (b, 0, k_col0 // qw)),
            pl.BlockSpec((None, S // VT_BLK, qw, VT_BLK), lambda b, i: (b, 0, vt_row0 // qw, 0)),
        ],
        out_specs=pl.BlockSpec((None, bq, qw), lambda b, i: (b, i, 0)),
        out_shape=jax.ShapeDtypeStruct((B, S, n_heads * HEAD_DIM), BF16),
        scratch_shapes=[pltpu.VMEM((hp, bq, bq), F32)] * 3,
        compiler_params=_cparams(("arbitrary", "arbitrary")),
        name="sb_attn",
    )(main, main, vt_arr)


def _mla_prep_kernel(aux_ref, pos_ref, invf_ref, qn_ref, kvn_ref, wuq_ref, wuqr_ref,
                     wk_ref, wvt_ref, oq_ref, ok_ref, ovt_ref, *, q_scale):
    ang = pos_ref[...] * invf_ref[...]
    cos4, sin4 = jnp.cos(ang), jnp.sin(ang)
    lane = lax.broadcasted_iota(jnp.int32, ang.shape, 1)
    rotary = (lane >= NOPE_DIM) & (lane < NOPE_DIM + ROPE_DIM)
    cos_rows, sin_rows = [], []
    for m in range(LANES // ROPE_DIM):
        shift = (NOPE_DIM - ROPE_DIM * m) % LANES
        cm = cos4 if shift == 0 else pltpu.roll(cos4, shift, 1)
        sm = sin4 if shift == 0 else pltpu.roll(sin4, shift, 1)
        cos_rows.append(jnp.where(rotary, cm, 1.0))
        sin_rows.append(jnp.where(rotary, sm, 0.0))
    cos = jnp.concatenate(cos_rows, axis=0)
    sin = jnp.concatenate(sin_rows, axis=0)
    cq = _rms_bf16(aux_ref[:, 0:Q_LORA], qn_ref[...])
    ckv = _rms_bf16(aux_ref[:, Q_LORA:Q_LORA + KV_LORA], kvn_ref[...])
    o = Q_LORA + KV_LORA
    k_rope = aux_ref[:, o:o + LANES] * cos + aux_ref[:, o + LANES:o + 2 * LANES] * sin
    cos_q, sin_q = cos * q_scale, sin * q_scale
    n_heads = oq_ref.shape[-1] // LANES
    for h in range(0, n_heads, 2):
        cols = slice(h * LANES, (h + 2) * LANES)
        qa = jnp.dot(cq, wuq_ref[:, cols], preferred_element_type=F32)
        qb = jnp.dot(cq, wuqr_ref[:, cols], preferred_element_type=F32)
        kn = jnp.dot(ckv, wk_ref[:, cols], preferred_element_type=F32)
        for d in range(2):
            c1 = slice(d * LANES, (d + 1) * LANES)
            c2 = slice((h + d) * LANES, (h + d + 1) * LANES)
            oq_ref[:, c2] = (qa[:, c1] * cos_q + qb[:, c1] * sin_q).astype(BF16)
            ok_ref[:, c2] = (kn[:, c1] + k_rope).astype(BF16)
    nv = wvt_ref.shape[0]
    for r in range(0, nv, 256):
        vt = lax.dot_general(wvt_ref[r:r + 256, :], ckv, _NT,
                             preferred_element_type=F32).astype(BF16)
        _store_vt(ovt_ref, vt, r)


MLA_TM = 512


def _mla_prep(aux, pos, invf, qn, kvn, wuq, wuqr, wk, wvt, q_scale, tm=MLA_TM):
    B, S, na = aux.shape
    nq, nv = wuq.shape[1], wvt.shape[0]
    full = lambda a: pl.BlockSpec(a.shape, lambda b, i: (0,) * a.ndim)
    return pl.pallas_call(
        functools.partial(_mla_prep_kernel, q_scale=q_scale),
        grid=(B, S // tm),
        in_specs=[
            pl.BlockSpec((None, tm, na), lambda b, i: (b, i, 0)),
            pl.BlockSpec((None, None) + pos.shape[2:], lambda b, i: (b, i, 0, 0)),
            full(invf), full(qn), full(kvn), full(wuq), full(wuqr), full(wk), full(wvt),
        ],
        out_specs=[
            pl.BlockSpec((None, tm, nq), lambda b, i: (b, i, 0)),
            pl.BlockSpec((None, tm, nq), lambda b, i: (b, i, 0)),
            pl.BlockSpec((None, tm // VT_BLK, nv, VT_BLK), lambda b, i: (b, i, 0, 0)),
        ],
        out_shape=[
            jax.ShapeDtypeStruct((B, S, nq), BF16),
            jax.ShapeDtypeStruct((B, S, nq), BF16),
            jax.ShapeDtypeStruct((B, S // VT_BLK, nv, VT_BLK), BF16),
        ],
        compiler_params=_cparams(("arbitrary", "arbitrary")),
        name="mla_prep",
    )(aux, pos, invf, qn, kvn, wuq, wuqr, wk, wvt)


def _mlp_kernel(x_ref, oa_ref, ob_ref, wo_ref, g_ref, wu_ref, wd_ref, gf_ref, out_ref,
                x1_ref, h_ref, a_ref, *, final_norm, tf):
    na = oa_ref.shape[-1]
    x1 = (x_ref[...]
          + jnp.dot(oa_ref[...], wo_ref[0:na, :], preferred_element_type=F32)
          + jnp.dot(ob_ref[...], wo_ref[na:, :], preferred_element_type=F32))
    x1_ref[...] = x1
    h_ref[...] = _rms_bf16(x1, g_ref[...])
    for f in range(0, wu_ref.shape[1], tf):
        u = jnp.dot(h_ref[...], wu_ref[:, f:f + tf], preferred_element_type=F32)
        a_ref[:, f:f + tf] = jnp.square(jnp.maximum(u, 0.0)).astype(BF16)
    y = x1_ref[...] + jnp.dot(a_ref[...], wd_ref[...], preferred_element_type=F32)
    if final_norm:
        ms = jnp.mean(y * y, axis=-1, keepdims=True)
        y = y * lax.rsqrt(ms + EPS) * gf_ref[...]
    out_ref[...] = y


def _mlp(x2, oa, ob, wo, g, wu, wd, gf, final_norm, tm=1024, tf=1024):
    T, D = x2.shape
    F = wu.shape[1]
    na, nb = oa.shape[1], ob.shape[1]
    const = lambda shape: pl.BlockSpec(shape, lambda i: (0, 0), pipeline_mode=pl.Buffered(1))
    return pl.pallas_call(
        functools.partial(_mlp_kernel, final_norm=final_norm, tf=tf),
        grid=(T // tm,),
        in_specs=[
            pl.BlockSpec((tm, D), lambda i: (i, 0)),
            pl.BlockSpec((tm, na), lambda i: (i, 0)),
            pl.BlockSpec((tm, nb), lambda i: (i, 0)),
            const((na + nb, D)),
            const((1, D)),
            const((D, F)),
            const((F, D)),
            const((1, D)),
        ],
        out_specs=pl.BlockSpec((tm, D), lambda i: (i, 0)),
        out_shape=jax.ShapeDtypeStruct((T, D), F32),
        scratch_shapes=[pltpu.VMEM((tm, D), F32), pltpu.VMEM((tm, D), BF16),
                        pltpu.VMEM((tm, F), BF16)],
        compiler_params=_cparams(("arbitrary",)),
        name="mlp",
    )(x2, oa, ob, wo, g.reshape(1, D), wu, wd, gf.reshape(1, D))


def _pad_cols(w, n):
    return jnp.pad(w, ((0, 0), (0, n - w.shape[1])))


def _rot_cols(w):
    half = ROPE_DIM // 2
    return jnp.concatenate([-w[:, half:], w[:, :half]], axis=1)


def _rope_slab(w):
    z = jnp.zeros((w.shape[0], NOPE_DIM), w.dtype)
    return jnp.concatenate([z, w, jnp.zeros((w.shape[0], LANES - NOPE_DIM - ROPE_DIM), w.dtype)], axis=1)


def _even_layer(x, g_mix, w_in, b_forget, rel_bias, w_out, g_mlp, w_up, w_down, g_final,
                final_norm):
    B, S, D = x.shape
    hf, hc = b_forget.shape[0], rel_bias.shape[0]
    wf, wc = hf * HEAD_DIM, hc * HEAD_DIM
    o = np.cumsum([0, wf, wf, wf, hf, wc, wc, wc])
    qa, ka, va, fa, qb, kb, vb = [w_in[:, o[n]:o[n + 1]] for n in range(7)]
    q_scale = HEAD_DIM ** -0.5 * LOG2E
    wm = jnp.concatenate([qa * q_scale, ka, qb * q_scale, kb], axis=1).astype(BF16)
    wvt = jnp.concatenate([va, vb], axis=1).T.astype(BF16)
    wa = _pad_cols(fa, LANES).astype(BF16)
    main, vt, aux = _inproj(x, g_mix, wm, wvt, wa)

    q_aug, k_aug = _logcum(aux, _pad_cols(b_forget.reshape(1, hf), LANES), hf)
    o_a = _flash("fox", main, 0, main, wf, vt, 0, (q_aug, k_aug), hf)

    assert rel_bias.shape[1] == CHUNK + REL_CLIP
    right = CK_EXT - rel_bias.shape[1] - (CHUNK + 1)
    ext = jnp.pad(rel_bias * LOG2E, ((0, 0), (CHUNK + 1, right)), mode="edge")
    o_b = _chunk_attn(main, 2 * wf, 2 * wf + wc, vt, wf, ext, hc)

    y = _mlp(x.reshape(B * S, D), o_a.reshape(B * S, wf), o_b.reshape(B * S, wc),
             w_out.astype(BF16), g_mlp, w_up.astype(BF16), w_down.astype(BF16), g_final,
             final_norm)
    return y.reshape(B, S, D)


def _odd_layer(x, positions, g_mix, w_in, q_norm, kv_norm, w_uq, w_ukv, w_out, g_mlp, w_up,
               w_down, g_final, final_norm):
    B, S, D = x.shape
    hm = w_ukv.shape[1] // (NOPE_DIM + HEAD_DIM)
    ws = w_in.shape[1] - Q_LORA - KV_LORA - ROPE_DIM
    hs = (ws // 3) // HEAD_DIM
    wsb = hs * HEAD_DIM
    o = np.cumsum([0, wsb, wsb, wsb, Q_LORA, KV_LORA, ROPE_DIM])
    qc, kc, vc, w_cq, w_ckv, w_kr = [w_in[:, o[n]:o[n + 1]] for n in range(6)]
    wm = jnp.concatenate([qc * HEAD_DIM ** -0.5, kc], axis=1).astype(BF16)
    wa = jnp.concatenate([w_cq, w_ckv, _rope_slab(w_kr), _rope_slab(_rot_cols(w_kr))],
                         axis=1).astype(BF16)
    main, vt, aux = _inproj(x, g_mix, wm, vc.T.astype(BF16), wa)
    o_c = _sb_attn(main, 0, wsb, vt, 0, hs)

    dq = NOPE_DIM + ROPE_DIM
    wuq3 = w_uq.reshape(Q_LORA, hm, dq)
    nope, ropew = wuq3[:, :, :NOPE_DIM], wuq3[:, :, NOPE_DIM:]
    zq = jnp.zeros((Q_LORA, hm, LANES - dq), w_uq.dtype)
    wuq = jnp.concatenate([nope, ropew, zq], axis=2).reshape(Q_LORA, hm * LANES).astype(BF16)
    half = ROPE_DIM // 2
    ropr = jnp.concatenate([-ropew[:, :, half:], ropew[:, :, :half]], axis=2)
    wuqr = jnp.concatenate([jnp.zeros_like(nope), ropr, zq], axis=2)
    wuqr = wuqr.reshape(Q_LORA, hm * LANES).astype(BF16)
    wkv3 = w_ukv.reshape(KV_LORA, hm, NOPE_DIM + HEAD_DIM)
    wk = jnp.concatenate([wkv3[:, :, :NOPE_DIM],
                          jnp.zeros((KV_LORA, hm, LANES - NOPE_DIM), w_ukv.dtype)], axis=2)
    wk = wk.reshape(KV_LORA, hm * LANES).astype(BF16)
    wv_t = wkv3[:, :, NOPE_DIM:].reshape(KV_LORA, hm * HEAD_DIM).T.astype(BF16)
    freqs = (ROPE_THETA ** (-jnp.arange(half, dtype=F32) / half))
    invf = jnp.tile(freqs, 2 * LANES // ROPE_DIM).reshape(1, LANES)
    groups = LANES // ROPE_DIM
    pos = positions.astype(F32).reshape(B, S // MLA_TM, groups, MLA_TM // groups)
    pos = jnp.repeat(jnp.swapaxes(pos, 2, 3), ROPE_DIM, axis=-1)
    qm, km, vtm = _mla_prep(aux, pos, invf, q_norm.reshape(1, Q_LORA),
                            kv_norm.reshape(1, KV_LORA), wuq, wuqr, wk, wv_t,
                            dq ** -0.5 * LOG2E)
    o_d = _flash("mla", qm, 0, km, 0, vtm, 0, None, hm)

    y = _mlp(x.reshape(B * S, D), o_c.reshape(B * S, wsb), o_d.reshape(B * S, hm * HEAD_DIM),
             w_out.astype(BF16), g_mlp, w_up.astype(BF16), w_down.astype(BF16), g_final,
             final_norm)
    return y.reshape(B, S, D)


def kernel(x, positions, norm_mix, norm_mlp, norm_final, w_in_ab, b_forget, rel_bias, w_out_ab,
           w_in_cd, q_norm, kv_norm, w_uq, w_ukv, w_out_cd, w_up, w_down):
    depth = norm_mix.shape[0]
    for layer in range(depth):
        last = layer == depth - 1
        if layer % 2 == 0:
            e = layer // 2
            x = _even_layer(x, norm_mix[layer], w_in_ab[e], b_forget[e], rel_bias[e], w_out_ab[e],
                            norm_mlp[layer], w_up[layer], w_down[layer], norm_final, last)
        else:
            o = layer // 2
            x = _odd_layer(x, positions, norm_mix[layer], w_in_cd[o], q_norm[o], kv_norm[o],
                           w_uq[o], w_ukv[o], w_out_cd[o], norm_mlp[layer], w_up[layer],
                           w_down[layer], norm_final, last)
    return x
```

```python
import functools
import math

import numpy as np
import jax
import jax.numpy as jnp
from jax import lax
from jax.experimental import pallas as pl
from jax.experimental.pallas import tpu as pltpu

F32 = jnp.float32
BF16 = jnp.bfloat16

EPS = 1e-6
HEAD_DIM = 64
CHUNK = 64
N_LEFT_CHUNKS = 8
REL_CLIP = 256
ROPE_DIM = 32
NOPE_DIM = 64
ROPE_THETA = 10000.0
Q_LORA = 384
KV_LORA = 256

LANES = 128
VT_BLK = LANES
SUB = LANES
FLASH_HP = 8
FLASH_BQ = 512
FLASH_SUB = 256
FIXED_WIDTHS = (4, 2, 1)
SAFE_GAP = 80.0
NEG = -1e30
LOG2E = math.log2(math.e)
SB_ZERO_LOG = -104.0
VMEM_LIMIT = 56 * 1024 * 1024

_NT = (((1,), (1,)), ((), ()))


def _cparams(sem):
    return pltpu.CompilerParams(dimension_semantics=sem, vmem_limit_bytes=VMEM_LIMIT)


def _rms_bf16(x, g):
    ms = jnp.mean(x * x, axis=-1, keepdims=True)
    return (x * lax.rsqrt(ms + EPS) * g).astype(BF16)


def _store_vt(ovt_ref, vt, row0):
    rows, tm = vt.shape
    for c in range(tm // VT_BLK):
        ovt_ref[c, row0:row0 + rows, :] = vt[:, c * VT_BLK:(c + 1) * VT_BLK]


def _inproj_kernel(x_ref, g_ref, wm_ref, wvt_ref, wa_ref, om_ref, ovt_ref, oa_ref):
    h = _rms_bf16(x_ref[...], g_ref[...])
    nm = om_ref.shape[-1]
    for c in range(0, nm, 512):
        om_ref[:, c:c + 512] = jnp.dot(
            h, wm_ref[:, c:c + 512], preferred_element_type=F32).astype(BF16)
    nv = wvt_ref.shape[0]
    for r in range(0, nv, 256):
        vt = lax.dot_general(wvt_ref[r:r + 256, :], h, _NT,
                             preferred_element_type=F32).astype(BF16)
        _store_vt(ovt_ref, vt, r)
    oa_ref[...] = jnp.dot(h, wa_ref[...], preferred_element_type=F32)


def _inproj(x, g, wm, wvt, wa, tm=1024):
    B, S, D = x.shape
    nm, nv, na = wm.shape[1], wvt.shape[0], wa.shape[1]
    return pl.pallas_call(
        _inproj_kernel,
        grid=(B, S // tm),
        in_specs=[
            pl.BlockSpec((None, tm, D), lambda b, i: (b, i, 0)),
            pl.BlockSpec((1, D), lambda b, i: (0, 0)),
            pl.BlockSpec((D, nm), lambda b, i: (0, 0)),
            pl.BlockSpec((nv, D), lambda b, i: (0, 0)),
            pl.BlockSpec((D, na), lambda b, i: (0, 0)),
        ],
        out_specs=[
            pl.BlockSpec((None, tm, nm), lambda b, i: (b, i, 0)),
            pl.BlockSpec((None, tm // VT_BLK, nv, VT_BLK), lambda b, i: (b, i, 0, 0)),
            pl.BlockSpec((None, tm, na), lambda b, i: (b, i, 0)),
        ],
        out_shape=[
            jax.ShapeDtypeStruct((B, S, nm), BF16),
            jax.ShapeDtypeStruct((B, S // VT_BLK, nv, VT_BLK), BF16),
            jax.ShapeDtypeStruct((B, S, na), F32),
        ],
        compiler_params=_cparams(("arbitrary", "arbitrary")),
        name="inproj",
    )(x, g.reshape(1, D), wm, wvt, wa)


def _split3(x):
    hi = x.astype(BF16)
    r = x - hi.astype(F32)
    mid = r.astype(BF16)
    lo = (r - mid.astype(F32)).astype(BF16)
    return hi, mid, lo


AUG_W = 8


def _logcum_kernel(fa_ref, b_ref, pq_ref, pk_ref, oneq_ref, onek_ref, oq_ref, ok_ref, carry_ref):
    @pl.when(pl.program_id(1) == 0)
    def _():
        carry_ref[...] = jnp.zeros_like(carry_ref)

    z = fa_ref[...] + b_ref[...]
    lf = jnp.minimum(z, 0.0) - jnp.log(1.0 + jnp.exp(-jnp.abs(z)))
    tc = lf.shape[0]
    r = lax.broadcasted_iota(jnp.int32, (tc, tc), 0)
    c = lax.broadcasted_iota(jnp.int32, (tc, tc), 1)
    tri = jnp.where(r >= c, 1.0, 0.0).astype(BF16)
    cs = carry_ref[...]
    for part in _split3(lf):
        cs = cs + jnp.dot(tri, part, preferred_element_type=F32)
    carry_ref[...] = cs[tc - 1:tc, :]
    qa, ka = oneq_ref[...], onek_ref[...]
    for n, part in enumerate(_split3(cs * LOG2E)):
        qa = qa + jnp.dot(part, pq_ref[n], preferred_element_type=F32)
        ka = ka + jnp.dot(part, pk_ref[n], preferred_element_type=F32)
    oq_ref[...] = qa.astype(BF16)
    ok_ref[...] = ka.astype(BF16)


def _logcum(fa, bias, n_heads, tc=512):
    B, S, W = fa.shape
    na = LANES
    pq = np.zeros((3, W, na), np.float32)
    pk = np.zeros((3, W, na), np.float32)
    oneq = np.zeros((1, na), np.float32)
    onek = np.zeros((1, na), np.float32)
    for h in range(n_heads):
        base = h * AUG_W
        for n in range(3):
            pq[n, h, base + n] = 1.0
            pk[n, h, base + 3 + n] = -1.0
        oneq[0, base + 3:base + 6] = 1.0
        onek[0, base:base + 3] = 1.0
    const = lambda a: pl.BlockSpec(a.shape, lambda b, i: (0,) * a.ndim)
    args = [jnp.asarray(pq, BF16), jnp.asarray(pk, BF16), jnp.asarray(oneq), jnp.asarray(onek)]
    return pl.pallas_call(
        _logcum_kernel,
        grid=(B, S // tc),
        in_specs=[pl.BlockSpec((None, tc, W), lambda b, i: (b, i, 0)),
                  pl.BlockSpec((1, W), lambda b, i: (0, 0))] + [const(a) for a in args],
        out_specs=[pl.BlockSpec((None, tc, na), lambda b, i: (b, i, 0))] * 2,
        out_shape=[jax.ShapeDtypeStruct((B, S, na), BF16)] * 2,
        scratch_shapes=[pltpu.VMEM((1, W), F32)],
        compiler_params=_cparams(("arbitrary", "arbitrary")),
        name="logcum",
    )(fa, bias, *args)


def _pair_mask_q(q2, j):
    lane = lax.broadcasted_iota(jnp.int32, q2.shape, 1)
    keep = (lane >= HEAD_DIM * j) & (lane < HEAD_DIM * (j + 1))
    return jnp.where(keep, q2, jnp.zeros_like(q2))


ONES_ROWS = 16


def _softmax_step(tiles, vts, carry, tile_max=None):
    m, acc = carry
    if tile_max is not None:
        m_new = jnp.maximum(m, tile_max)
    else:
        m_new = m
        for tile in tiles:
            m_new = jnp.maximum(m_new, jnp.max(tile(), axis=0, keepdims=True))
    alpha = jnp.exp2(m - m_new)
    pv = None
    for tile, vt in zip(tiles, vts):
        p = jnp.exp2(tile() - m_new).astype(BF16)
        vt1 = jnp.concatenate([vt, jnp.ones((ONES_ROWS, vt.shape[1]), BF16)], axis=0)
        d = jnp.dot(vt1, p, preferred_element_type=F32)
        pv = d if pv is None else pv + d
    return m_new, alpha * acc + pv


def _softmax_init(bq):
    return (jnp.full((1, bq), NEG, F32), jnp.zeros((HEAD_DIM + ONES_ROWS, bq), F32))


def _softmax_out(carry):
    _, acc = carry
    return acc[0:HEAD_DIM] / acc[HEAD_DIM:HEAD_DIM + 1]


def _store_heads(o_ref, outs):
    oT = jnp.concatenate(outs, axis=0)
    o_ref[...] = oT.T.astype(o_ref.dtype)


def _flash_kernel(*refs, mode, hp, bq):
    if mode == "fox":
        q_ref, k_ref, vt_ref, qaug_ref, kaug_ref, o_ref = refs[:6]
    else:
        q_ref, k_ref, vt_ref, o_ref = refs[:4]
    sa_ref, sb_ref, ma_ref, mb_ref, qt_ref, kn_ref = refs[-6:]
    qs = pl.program_id(2) * bq
    sub = FLASH_SUB
    row = lax.broadcasted_iota(jnp.int32, (sub, bq), 0)
    col = lax.broadcasted_iota(jnp.int32, (sub, bq), 1)
    if mode == "fox":
        kcols = [slice(LANES * (h // 2), LANES * (h // 2 + 1)) for h in range(hp)]
        lane = lax.broadcasted_iota(jnp.int32, (bq, LANES), 1)
        qa = qaug_ref[...]
        qms = []
        for h in range(hp):
            first = AUG_W * (pl.program_id(1) * hp + h)
            own = (lane >= first) & (lane < first + AUG_W)
            qms.append(jnp.concatenate(
                [_pair_mask_q(q_ref[:, kcols[h]], h % 2),
                 jnp.where(own, qa, jnp.zeros_like(qa))], axis=1))
    else:
        kcols = [slice(LANES * h, LANES * (h + 1)) for h in range(hp)]
        qms = [q_ref[:, kcols[h]] for h in range(hp)]
    for h in range(hp):
        qt_ref[h] = qms[h].T

    @pl.when(pl.program_id(2) == 0)
    def _():
        klane = lax.broadcasted_iota(jnp.int32, (1, LANES), 1)
        for h in range(hp):
            kabs = jnp.max(jnp.abs(k_ref[:, kcols[h]].astype(F32)), axis=0, keepdims=True)
            sq = kabs * kabs
            if mode == "fox":
                sq = jnp.where((klane >= HEAD_DIM * (h % 2)) & (klane < HEAD_DIM * (h % 2 + 1)),
                               sq, 0.0)
            kn_ref[h] = jnp.broadcast_to(jnp.sqrt(jnp.sum(sq, axis=1, keepdims=True)), (1, bq))

    def scores(sb, h, masked, col0=0):
        ks = pl.multiple_of(sb * sub, sub)
        k = k_ref[pl.ds(ks, sub), kcols[h]]
        if mode == "fox":
            k = jnp.concatenate([k, kaug_ref[pl.ds(ks, sub), :]], axis=1)
        sT = jnp.dot(k, qt_ref[h, :, col0:], preferred_element_type=F32)
        if masked and mode == "fox":
            sT = jnp.where((ks + row <= qs + col)[:, col0:], sT, NEG)
        elif masked:
            sT = jnp.where((((ks + row) >> 6) <= ((qs + col) >> 6))[:, col0:], sT, NEG)
        return sT

    nsub = bq // sub
    nvt = sub // VT_BLK

    def produce(buf, sb0, masked, h):
        s_buf, m_buf = buf
        tile_max = None
        for c in range(nsub):
            col0 = c * sub if masked else 0
            sT = scores(sb0 + c, h, masked, col0)
            cm = jnp.max(sT, axis=0, keepdims=True)
            if col0:
                s_buf[h, c, :, 0:col0] = jnp.full((sub, col0), NEG, F32)
                cm = jnp.concatenate([jnp.full((1, col0), NEG, F32), cm], axis=1)
            s_buf[h, c, :, col0:] = sT
            tile_max = cm if tile_max is None else jnp.maximum(tile_max, cm)
        m_buf[h] = tile_max

    def consume(buf, sb0, carry, h):
        s_buf, m_buf = buf
        tiles = [lambda c=c: s_buf[h, c] for c in range(nsub)]
        vts = [jnp.concatenate([vt_ref[(sb0 + c) * nvt + v, HEAD_DIM * h:HEAD_DIM * (h + 1), :]
                                for v in range(nvt)], axis=1) for c in range(nsub)]
        return _softmax_step(tiles, vts, carry, tile_max=m_buf[h])

    def stage(cur, cur_sb, nxt, nxt_sb, carries):
        if nxt is not None:
            for h in range(hp):
                produce(nxt, nxt_sb, False, h)
        return tuple(consume(cur, cur_sb, carries[h], h) for h in range(hp))

    n = pl.program_id(2)
    diag_sb = qs // sub
    buf_a, buf_b = (sa_ref, ma_ref), (sb_ref, mb_ref)
    for h in range(hp):
        produce(buf_a, diag_sb, True, h)

    def pair(j, carries):
        carries = stage(buf_a, jnp.where(j == 0, diag_sb, (2 * j - 1) * nsub),
                        buf_b, 2 * j * nsub, carries)
        return stage(buf_b, 2 * j * nsub,
                     buf_a, jnp.minimum(2 * j + 1, n - 1) * nsub, carries)

    def online(_):
        carries = tuple(_softmax_init(bq) for _ in range(hp))
        carries = lax.fori_loop(0, (n + 1) // 2, pair, carries)
        carries = lax.cond(
            n % 2 == 0,
            lambda c: stage(buf_a, jnp.where(n == 0, diag_sb, (n - 1) * nsub), None, None, c),
            lambda c: c, carries)
        return jnp.concatenate([_softmax_out(c) for c in carries], axis=0)

    refs_ = []
    gap = None
    for h in range(hp):
        qf = qt_ref[h, 0:LANES, :].astype(F32)
        bound = jnp.sqrt(jnp.sum(qf * qf, axis=0, keepdims=True)) * kn_ref[h] * 1.01 + 1e-3
        refs_.append(bound)
        g = jnp.max(bound - ma_ref[h])
        gap = g if gap is None else jnp.maximum(gap, g)

    def weigh_add(acc, h, s, sb0, nblk, col0=0):
        p = jnp.exp2(s - refs_[h][:, col0:]).astype(BF16)
        vt = jnp.concatenate([vt_ref[sb0 * nvt + v, HEAD_DIM * h:HEAD_DIM * (h + 1), :]
                              for v in range(nblk * nvt)], axis=1)
        vt1 = jnp.concatenate([vt, jnp.ones((ONES_ROWS, nblk * sub), BF16)], axis=0)
        d = jnp.dot(vt1, p, preferred_element_type=F32)
        if col0:
            d = jnp.concatenate([jnp.zeros((d.shape[0], col0), F32), d], axis=1)
        return acc + d

    def fixed_reference(_):
        def run(kb, accs, nblk):
            ks = pl.multiple_of(kb * sub, sub)
            ss = []
            for h in range(hp):
                k = k_ref[pl.ds(ks, nblk * sub), kcols[h]]
                if mode == "fox":
                    k = jnp.concatenate([k, kaug_ref[pl.ds(ks, nblk * sub), :]], axis=1)
                ss.append(jnp.dot(k, qt_ref[h], preferred_element_type=F32))
            return tuple(weigh_add(accs[h], h, ss[h], kb, nblk) for h in range(hp))

        accs = tuple(jnp.zeros((HEAD_DIM + ONES_ROWS, bq), F32) for _ in range(hp))
        done = 0
        for width in FIXED_WIDTHS:
            trips = (diag_sb - done) // width
            accs = lax.fori_loop(0, trips,
                                 lambda j, a, done=done, width=width: run(done + j * width, a, width),
                                 accs)
            done = done + trips * width
        for c in range(nsub):
            accs = [weigh_add(accs[h], h, sa_ref[h, c, :, c * sub:], diag_sb + c, 1, c * sub)
                    for h in range(hp)]
        return jnp.concatenate([a[0:HEAD_DIM] / a[HEAD_DIM:HEAD_DIM + 1] for a in accs], axis=0)

    oT = lax.cond(gap <= SAFE_GAP, fixed_reference, online, None)
    o_ref[...] = oT.T.astype(o_ref.dtype)


def _flash(mode, q_arr, q_col0, k_arr, k_col0, vt_arr, vt_row0, extra, n_heads,
           hp=FLASH_HP, bq=FLASH_BQ):
    B, S, _ = q_arr.shape
    qw = (HEAD_DIM if mode == "fox" else LANES) * hp
    vw = HEAD_DIM * hp
    in_specs = [
        pl.BlockSpec((None, bq, qw), lambda b, g, i: (b, i, q_col0 // qw + g)),
        pl.BlockSpec((None, S, qw), lambda b, g, i: (b, 0, k_col0 // qw + g)),
        pl.BlockSpec((None, S // VT_BLK, vw, VT_BLK),
                     lambda b, g, i: (b, 0, vt_row0 // vw + g, 0)),
    ]
    args = [q_arr, k_arr, vt_arr]
    if mode == "fox":
        q_aug, k_aug = extra
        in_specs += [
            pl.BlockSpec((None, bq, LANES), lambda b, g, i: (b, i, 0)),
            pl.BlockSpec((None, S, LANES), lambda b, g, i: (b, 0, 0)),
        ]
        args += [q_aug, k_aug]
    return pl.pallas_call(
        functools.partial(_flash_kernel, mode=mode, hp=hp, bq=bq),
        grid=(B, n_heads // hp, S // bq),
        in_specs=in_specs,
        out_specs=pl.BlockSpec((None, bq, vw), lambda b, g, i: (b, i, g)),
        out_shape=jax.ShapeDtypeStruct((B, S, n_heads * HEAD_DIM), BF16),
        scratch_shapes=([pltpu.VMEM((hp, bq // FLASH_SUB, FLASH_SUB, bq), F32)] * 2
                        + [pltpu.VMEM((hp, 1, bq), F32)] * 2
                        + [pltpu.VMEM((hp, 2 * LANES if mode == "fox" else LANES, bq), BF16),
                           pltpu.VMEM((hp, 1, bq), F32)]),
        compiler_params=_cparams(("arbitrary", "arbitrary", "arbitrary")),
        name="flash_" + mode,
    )(*args)


CK_B = 2 * CHUNK
CK_NW = N_LEFT_CHUNKS * CHUNK // CK_B + 1
CK_EXT = (CK_NW + 1) * CK_B


def _chunk_kernel(q_ref, k_ref, vt_ref, ext_ref, o_ref, tab_ref, s_ref, *, hp, nq):
    i = pl.program_id(1)

    @pl.when(i == 0)
    def _():
        jj = lax.broadcasted_iota(jnp.int32, (CK_B, CK_B), 0)
        rr = lax.broadcasted_iota(jnp.int32, (CK_B, CK_B), 1)
        for h in range(hp):
            for w in range(CK_NW):
                a = (CK_NW - 1 - w) * CK_B
                g = jnp.broadcast_to(ext_ref[h:h + 1, a:a + 2 * CK_B], (CK_B, 2 * CK_B))
                t = pltpu.roll(g, CK_B, 1, stride=1, stride_axis=0)[:, :CK_B]
                if w == 0:
                    t = jnp.where((rr >= CHUNK) & (jj < CHUNK), NEG, t)
                if w == CK_NW - 1:
                    t = jnp.where((rr < CHUNK) & (jj >= CHUNK), NEG, t)
                tab_ref[h, w * CK_B:(w + 1) * CK_B, :] = t

    kcols = [slice(LANES * (h // 2), LANES * (h // 2 + 1)) for h in range(hp)]
    firsts = [i * nq + u - (CK_NW - 1) for u in range(nq)]

    def finish():
        for u in range(nq):
            kbc = [jnp.maximum(firsts[u] + w, 0) for w in range(CK_NW)]
            outs = []
            for h in range(hp):
                vt = jnp.concatenate([vt_ref[kbc[w], HEAD_DIM * h:HEAD_DIM * (h + 1), :]
                                      for w in range(CK_NW)], axis=1)
                outs.append(_softmax_out(_softmax_step([lambda u=u, h=h: s_ref[u, h]], [vt],
                                                       _softmax_init(CK_B))))
            oT = jnp.concatenate(outs, axis=0)
            o_ref[u * CK_B:(u + 1) * CK_B, :] = oT.T.astype(o_ref.dtype)

    def pair_scores(u, p, ks, nrows):
        q2 = q_ref[u * CK_B:(u + 1) * CK_B, kcols[2 * p]]
        qq = jnp.concatenate([_pair_mask_q(q2, 0), _pair_mask_q(q2, 1)], axis=0)
        return lax.dot_general(k_ref[pl.ds(ks, nrows), kcols[2 * p]], qq, _NT,
                               preferred_element_type=F32)

    @pl.when(firsts[0] >= 0)
    def _():
        for u in range(nq):
            ks = pl.multiple_of(firsts[u] * CK_B, CK_B)
            for p in range(hp // 2):
                sT = pair_scores(u, p, ks, CK_NW * CK_B)
                for j in range(2):
                    s_ref[u, 2 * p + j] = sT[:, j * CK_B:(j + 1) * CK_B] + tab_ref[2 * p + j]
        finish()

    @pl.when(firsts[0] < 0)
    def _():
        for u in range(nq):
            for p in range(hp // 2):
                for w in range(CK_NW):
                    rows = slice(w * CK_B, (w + 1) * CK_B)
                    ks = pl.multiple_of(jnp.maximum(firsts[u] + w, 0) * CK_B, CK_B)
                    sT = pair_scores(u, p, ks, CK_B)
                    for j in range(2):
                        s_ref[u, 2 * p + j, rows, :] = jnp.where(
                            firsts[u] + w >= 0,
                            sT[:, j * CK_B:(j + 1) * CK_B] + tab_ref[2 * p + j, rows, :], NEG)
        finish()


def _chunk_attn(main, q_col0, k_col0, vt_arr, vt_row0, ext, n_heads, nq=4):
    B, S, _ = main.shape
    hp = n_heads
    qw, vw = HEAD_DIM * hp, HEAD_DIM * hp
    return pl.pallas_call(
        functools.partial(_chunk_kernel, hp=hp, nq=nq),
        grid=(B, S // (nq * CK_B)),
        in_specs=[
            pl.BlockSpec((None, nq * CK_B, qw), lambda b, i: (b, i, q_col0 // qw)),
            pl.BlockSpec((None, S, qw), lambda b, i: (b, 0, k_col0 // qw)),
            pl.BlockSpec((None, S // VT_BLK, vw, VT_BLK), lambda b, i: (b, 0, vt_row0 // vw, 0)),
            pl.BlockSpec((hp, CK_EXT), lambda b, i: (0, 0)),
        ],
        out_specs=pl.BlockSpec((None, nq * CK_B, vw), lambda b, i: (b, i, 0)),
        out_shape=jax.ShapeDtypeStruct((B, S, n_heads * HEAD_DIM), BF16),
        scratch_shapes=[pltpu.VMEM((hp, CK_NW * CK_B, CK_B), F32),
                        pltpu.VMEM((nq, hp, CK_NW * CK_B, CK_B), F32)],
        compiler_params=_cparams(("arbitrary", "arbitrary")),
        name="chunk_attn",
    )(main, main, vt_arr, ext)


def _sb_kernel(q_ref, k_ref, vt_ref, o_ref, z_ref, lb_ref, sfx_ref, *, hp, bq):
    qs = pl.program_id(1) * bq
    nsub = bq // SUB
    row = lax.broadcasted_iota(jnp.int32, (SUB, bq), 0)
    col = lax.broadcasted_iota(jnp.int32, (SUB, bq), 1)
    ur = lax.broadcasted_iota(jnp.int32, (SUB, 2 * SUB), 0)
    uc = lax.broadcasted_iota(jnp.int32, (SUB, 2 * SUB), 1) & (SUB - 1)
    upper2 = jnp.where(uc > ur, 1.0, 0.0).astype(BF16)
    kcols = [slice(LANES * (h // 2), LANES * (h // 2 + 1)) for h in range(hp)]
    qms = [_pair_mask_q(q_ref[:, kcols[h]], h % 2) for h in range(hp)]

    def step(kb, carries, masked, live=None):
        ks = pl.multiple_of(kb * bq, bq)
        for h in range(hp):
            z_ref[h] = lax.dot_general(k_ref[pl.ds(ks, bq), kcols[h]], qms[h], _NT,
                                       preferred_element_type=F32)
        first_col = [c * SUB if masked else 0 for c in range(nsub)]

        def widen(x, c):
            if first_col[c] == 0:
                return x
            return jnp.concatenate([jnp.zeros((x.shape[0], first_col[c]), x.dtype), x], axis=1)

        totals = []
        for h in range(hp):
            tot = []
            for c in range(nsub):
                rows, cols = slice(c * SUB, (c + 1) * SUB), slice(first_col[c], bq)
                z = z_ref[h, rows, cols]
                l1 = jnp.log(1.0 + jnp.exp2(jnp.abs(z) * (-LOG2E)))
                log_beta = jnp.minimum(z, 0.0) - l1
                log_keep = log_beta - z
                if masked:
                    valid = (ks + c * SUB + row < qs + col)[:, cols]
                    log_keep = jnp.where(valid, log_keep, 0.0)
                lb_ref[h, rows, cols] = log_beta
                hi = log_keep.astype(BF16)
                lo = (log_keep - hi.astype(F32)).astype(BF16)
                sfx = jnp.dot(upper2, jnp.concatenate([hi, lo], axis=0),
                              preferred_element_type=F32)
                sfx_ref[h, rows, cols] = sfx
                tot.append(widen(sfx[0:1, :] + log_keep[0:1, :], c))
            totals.append(tot)
        out = []
        for h in range(hp):
            tail, acc = carries[h]
            if live is not None:
                tail = jnp.where(live, tail, NEG)
            parts = [None] * nsub
            for c in range(nsub - 1, -1, -1):
                rows, cols = slice(c * SUB, (c + 1) * SUB), slice(first_col[c], bq)
                a = jnp.exp(lb_ref[h, rows, cols] + sfx_ref[h, rows, cols] + tail[:, cols])
                if masked:
                    a = jnp.where((ks + c * SUB + row < qs + col)[:, cols], a, 0.0)
                parts[c] = a.astype(BF16)
                tail = tail + totals[h][c]
            vts = [vt_ref[kb * nsub + c, HEAD_DIM * h:HEAD_DIM * (h + 1), :] for c in range(nsub)]
            if masked:
                for c in range(nsub):
                    acc = acc + widen(jnp.dot(vts[c], parts[c], preferred_element_type=F32), c)
            else:
                acc = acc + jnp.dot(jnp.concatenate(vts, axis=1), jnp.concatenate(parts, axis=0),
                                    preferred_element_type=F32)
            out.append((tail, acc))
        return tuple(out)

    n_full = qs // bq
    carries = tuple((jnp.zeros((1, bq), F32), jnp.zeros((HEAD_DIM, bq), F32))
                    for _ in range(hp))
    carries = step(n_full, carries, True)
    carries = step(jnp.maximum(n_full - 1, 0), carries, False, live=n_full >= 1)

    def cond(state):
        kb, carries = state
        tail_max = carries[0][0]
        for h in range(1, hp):
            tail_max = jnp.maximum(tail_max, carries[h][0])
        return (kb >= 0) & (jnp.max(tail_max) > SB_ZERO_LOG)

    def body(state):
        kb, carries = state
        return kb - 1, step(kb, carries, False)

    _, carries = lax.while_loop(cond, body, (n_full - 2, carries))
    _store_heads(o_ref, [acc for (_, acc) in carries])


def _sb_attn(main, q_col0, k_col0, vt_arr, vt_row0, n_heads, bq=256):
    B, S, _ = main.shape
    hp = n_heads
    qw = HEAD_DIM * hp
    return pl.pallas_call(
        functools.partial(_sb_kernel, hp=hp, bq=bq),
        grid=(B, S // bq),
        in_specs=[
            pl.BlockSpec((None, bq, qw), lambda b, i: (b, i, q_col0 // qw)),
            pl.BlockSpec((None, S, qw), lambda b, i: (b, 0, k_col0 // qw)),
            pl.BlockSpec((None, S // VT_BLK, qw, VT_BLK), lambda b, i: (b, 0, vt_row0 // qw, 0)),
        ],
        out_specs=pl.BlockSpec((None, bq, qw), lambda b, i: (b, i, 0)),
        out_shape=jax.ShapeDtypeStruct((B, S, n_heads * HEAD_DIM), BF16),
        scratch_shapes=[pltpu.VMEM((hp, bq, bq), F32)] * 3,
        compiler_params=_cparams(("arbitrary", "arbitrary")),
        name="sb_attn",
    )(main, main, vt_arr)


def _mla_prep_kernel(aux_ref, pos_ref, invf_ref, qn_ref, kvn_ref, wuq_ref, wuqr_ref,
                     wk_ref, wvt_ref, oq_ref, ok_ref, ovt_ref, *, q_scale):
    ang = pos_ref[...] * invf_ref[...]
    cos4, sin4 = jnp.cos(ang), jnp.sin(ang)
    lane = lax.broadcasted_iota(jnp.int32, ang.shape, 1)
    rotary = (lane >= NOPE_DIM) & (lane < NOPE_DIM + ROPE_DIM)
    cos_rows, sin_rows = [], []
    for m in range(LANES // ROPE_DIM):
        shift = (NOPE_DIM - ROPE_DIM * m) % LANES
        cm = cos4 if shift == 0 else pltpu.roll(cos4, shift, 1)
        sm = sin4 if shift == 0 else pltpu.roll(sin4, shift, 1)
        cos_rows.append(jnp.where(rotary, cm, 1.0))
        sin_rows.append(jnp.where(rotary, sm, 0.0))
    cos = jnp.concatenate(cos_rows, axis=0)
    sin = jnp.concatenate(sin_rows, axis=0)
    cq = _rms_bf16(aux_ref[:, 0:Q_LORA], qn_ref[...])
    ckv = _rms_bf16(aux_ref[:, Q_LORA:Q_LORA + KV_LORA], kvn_ref[...])
    o = Q_LORA + KV_LORA
    k_rope = aux_ref[:, o:o + LANES] * cos + aux_ref[:, o + LANES:o + 2 * LANES] * sin
    cos_q, sin_q = cos * q_scale, sin * q_scale
    n_heads = oq_ref.shape[-1] // LANES
    qr = jnp.dot(cq, wuqr_ref[...], preferred_element_type=F32)
    per_group = LANES // ROPE_DIM
    for h in range(0, n_heads, 2):
        cols = slice(h * LANES, (h + 2) * LANES)
        qa = jnp.dot(cq, wuq_ref[:, cols], preferred_element_type=F32)
        kn = jnp.dot(ckv, wk_ref[:, cols], preferred_element_type=F32)
        for d in range(2):
            c1 = slice(d * LANES, (d + 1) * LANES)
            c2 = slice((h + d) * LANES, (h + d + 1) * LANES)
            grp, m = divmod(h + d, per_group)
            qb = qr[:, grp * LANES:(grp + 1) * LANES]
            shift = (NOPE_DIM - ROPE_DIM * m) % LANES
            if shift:
                qb = pltpu.roll(qb, shift, 1)
            oq_ref[:, c2] = (qa[:, c1] * cos_q + qb * sin_q).astype(BF16)
            ok_ref[:, c2] = (kn[:, c1] + k_rope).astype(BF16)
    nv = wvt_ref.shape[0]
    for r in range(0, nv, 256):
        vt = lax.dot_general(wvt_ref[r:r + 256, :], ckv, _NT,
                             preferred_element_type=F32).astype(BF16)
        _store_vt(ovt_ref, vt, r)


MLA_TM = 512


def _mla_prep(aux, pos, invf, qn, kvn, wuq, wuqr, wk, wvt, q_scale, tm=MLA_TM):
    B, S, na = aux.shape
    nq, nv = wuq.shape[1], wvt.shape[0]
    full = lambda a: pl.BlockSpec(a.shape, lambda b, i: (0,) * a.ndim)
    return pl.pallas_call(
        functools.partial(_mla_prep_kernel, q_scale=q_scale),
        grid=(B, S // tm),
        in_specs=[
            pl.BlockSpec((None, tm, na), lambda b, i: (b, i, 0)),
            pl.BlockSpec((None, None) + pos.shape[2:], lambda b, i: (b, i, 0, 0)),
            full(invf), full(qn), full(kvn), full(wuq), full(wuqr), full(wk), full(wvt),
        ],
        out_specs=[
            pl.BlockSpec((None, tm, nq), lambda b, i: (b, i, 0)),
            pl.BlockSpec((None, tm, nq), lambda b, i: (b, i, 0)),
            pl.BlockSpec((None, tm // VT_BLK, nv, VT_BLK), lambda b, i: (b, i, 0, 0)),
        ],
        out_shape=[
            jax.ShapeDtypeStruct((B, S, nq), BF16),
            jax.ShapeDtypeStruct((B, S, nq), BF16),
            jax.ShapeDtypeStruct((B, S // VT_BLK, nv, VT_BLK), BF16),
        ],
        compiler_params=_cparams(("arbitrary", "arbitrary")),
        name="mla_prep",
    )(aux, pos, invf, qn, kvn, wuq, wuqr, wk, wvt)


def _mlp_kernel(x_ref, oa_ref, ob_ref, wo_ref, g_ref, wu_ref, wd_ref, gf_ref, out_ref,
                x1_ref, h_ref, a_ref, *, final_norm, tf):
    na = oa_ref.shape[-1]
    x1 = (x_ref[...]
          + jnp.dot(oa_ref[...], wo_ref[0:na, :], preferred_element_type=F32)
          + jnp.dot(ob_ref[...], wo_ref[na:, :], preferred_element_type=F32))
    x1_ref[...] = x1
    h_ref[...] = _rms_bf16(x1, g_ref[...])
    for f in range(0, wu_ref.shape[1], tf):
        u = jnp.dot(h_ref[...], wu_ref[:, f:f + tf], preferred_element_type=F32)
        a_ref[:, f:f + tf] = jnp.square(jnp.maximum(u, 0.0)).astype(BF16)
    y = x1_ref[...] + jnp.dot(a_ref[...], wd_ref[...], preferred_element_type=F32)
    if final_norm:
        ms = jnp.mean(y * y, axis=-1, keepdims=True)
        y = y * lax.rsqrt(ms + EPS) * gf_ref[...]
    out_ref[...] = y


def _mlp(x2, oa, ob, wo, g, wu, wd, gf, final_norm, tm=1024, tf=1024):
    T, D = x2.shape
    F = wu.shape[1]
    na, nb = oa.shape[1], ob.shape[1]
    const = lambda shape: pl.BlockSpec(shape, lambda i: (0, 0), pipeline_mode=pl.Buffered(1))
    return pl.pallas_call(
        functools.partial(_mlp_kernel, final_norm=final_norm, tf=tf),
        grid=(T // tm,),
        in_specs=[
            pl.BlockSpec((tm, D), lambda i: (i, 0)),
            pl.BlockSpec((tm, na), lambda i: (i, 0)),
            pl.BlockSpec((tm, nb), lambda i: (i, 0)),
            const((na + nb, D)),
            const((1, D)),
            const((D, F)),
            const((F, D)),
            const((1, D)),
        ],
        out_specs=pl.BlockSpec((tm, D), lambda i: (i, 0)),
        out_shape=jax.ShapeDtypeStruct((T, D), F32),
        scratch_shapes=[pltpu.VMEM((tm, D), F32), pltpu.VMEM((tm, D), BF16),
                        pltpu.VMEM((tm, F), BF16)],
        compiler_params=_cparams(("arbitrary",)),
        name="mlp",
    )(x2, oa, ob, wo, g.reshape(1, D), wu, wd, gf.reshape(1, D))


def _pad_cols(w, n):
    return jnp.pad(w, ((0, 0), (0, n - w.shape[1])))


def _rot_cols(w):
    half = ROPE_DIM // 2
    return jnp.concatenate([-w[:, half:], w[:, :half]], axis=1)


def _rope_slab(w):
    z = jnp.zeros((w.shape[0], NOPE_DIM), w.dtype)
    return jnp.concatenate([z, w, jnp.zeros((w.shape[0], LANES - NOPE_DIM - ROPE_DIM), w.dtype)], axis=1)


def _even_layer(x, g_mix, w_in, b_forget, rel_bias, w_out, g_mlp, w_up, w_down, g_final,
                final_norm):
    B, S, D = x.shape
    hf, hc = b_forget.shape[0], rel_bias.shape[0]
    wf, wc = hf * HEAD_DIM, hc * HEAD_DIM
    o = np.cumsum([0, wf, wf, wf, hf, wc, wc, wc])
    qa, ka, va, fa, qb, kb, vb = [w_in[:, o[n]:o[n + 1]] for n in range(7)]
    q_scale = HEAD_DIM ** -0.5 * LOG2E
    wm = jnp.concatenate([qa * q_scale, ka, qb * q_scale, kb], axis=1).astype(BF16)
    wvt = jnp.concatenate([va, vb], axis=1).T.astype(BF16)
    wa = _pad_cols(fa, LANES).astype(BF16)
    main, vt, aux = _inproj(x, g_mix, wm, wvt, wa)

    q_aug, k_aug = _logcum(aux, _pad_cols(b_forget.reshape(1, hf), LANES), hf)
    o_a = _flash("fox", main, 0, main, wf, vt, 0, (q_aug, k_aug), hf)

    assert rel_bias.shape[1] == CHUNK + REL_CLIP
    right = CK_EXT - rel_bias.shape[1] - (CHUNK + 1)
    ext = jnp.pad(rel_bias * LOG2E, ((0, 0), (CHUNK + 1, right)), mode="edge")
    o_b = _chunk_attn(main, 2 * wf, 2 * wf + wc, vt, wf, ext, hc)

    y = _mlp(x.reshape(B * S, D), o_a.reshape(B * S, wf), o_b.reshape(B * S, wc),
             w_out.astype(BF16), g_mlp, w_up.astype(BF16), w_down.astype(BF16), g_final,
             final_norm)
    return y.reshape(B, S, D)


def _odd_layer(x, positions, g_mix, w_in, q_norm, kv_norm, w_uq, w_ukv, w_out, g_mlp, w_up,
               w_down, g_final, final_norm):
    B, S, D = x.shape
    hm = w_ukv.shape[1] // (NOPE_DIM + HEAD_DIM)
    ws = w_in.shape[1] - Q_LORA - KV_LORA - ROPE_DIM
    hs = (ws // 3) // HEAD_DIM
    wsb = hs * HEAD_DIM
    o = np.cumsum([0, wsb, wsb, wsb, Q_LORA, KV_LORA, ROPE_DIM])
    qc, kc, vc, w_cq, w_ckv, w_kr = [w_in[:, o[n]:o[n + 1]] for n in range(6)]
    wm = jnp.concatenate([qc * HEAD_DIM ** -0.5, kc], axis=1).astype(BF16)
    wa = jnp.concatenate([w_cq, w_ckv, _rope_slab(w_kr), _rope_slab(_rot_cols(w_kr))],
                         axis=1).astype(BF16)
    main, vt, aux = _inproj(x, g_mix, wm, vc.T.astype(BF16), wa)
    o_c = _sb_attn(main, 0, wsb, vt, 0, hs)

    dq = NOPE_DIM + ROPE_DIM
    wuq3 = w_uq.reshape(Q_LORA, hm, dq)
    nope, ropew = wuq3[:, :, :NOPE_DIM], wuq3[:, :, NOPE_DIM:]
    zq = jnp.zeros((Q_LORA, hm, LANES - dq), w_uq.dtype)
    wuq = jnp.concatenate([nope, ropew, zq], axis=2).reshape(Q_LORA, hm * LANES).astype(BF16)
    half = ROPE_DIM // 2
    ropr = jnp.concatenate([-ropew[:, :, half:], ropew[:, :, :half]], axis=2)
    wuqr = ropr.reshape(Q_LORA, hm * ROPE_DIM).astype(BF16)
    wkv3 = w_ukv.reshape(KV_LORA, hm, NOPE_DIM + HEAD_DIM)
    wk = jnp.concatenate([wkv3[:, :, :NOPE_DIM],
                          jnp.zeros((KV_LORA, hm, LANES - NOPE_DIM), w_ukv.dtype)], axis=2)
    wk = wk.reshape(KV_LORA, hm * LANES).astype(BF16)
    wv_t = wkv3[:, :, NOPE_DIM:].reshape(KV_LORA, hm * HEAD_DIM).T.astype(BF16)
    freqs = (ROPE_THETA ** (-jnp.arange(half, dtype=F32) / half))
    invf = jnp.tile(freqs, 2 * LANES // ROPE_DIM).reshape(1, LANES)
    groups = LANES // ROPE_DIM
    pos = positions.astype(F32).reshape(B, S // MLA_TM, groups, MLA_TM // groups)
    pos = jnp.repeat(jnp.swapaxes(pos, 2, 3), ROPE_DIM, axis=-1)
    qm, km, vtm = _mla_prep(aux, pos, invf, q_norm.reshape(1, Q_LORA),
                            kv_norm.reshape(1, KV_LORA), wuq, wuqr, wk, wv_t,
                            dq ** -0.5 * LOG2E)
    o_d = _flash("mla", qm, 0, km, 0, vtm, 0, None, hm)

    y = _mlp(x.reshape(B * S, D), o_c.reshape(B * S, wsb), o_d.reshape(B * S, hm * HEAD_DIM),
             w_out.astype(BF16), g_mlp, w_up.astype(BF16), w_down.astype(BF16), g_final,
             final_norm)
    return y.reshape(B, S, D)


def kernel(x, positions, norm_mix, norm_mlp, norm_final, w_in_ab, b_forget, rel_bias, w_out_ab,
           w_in_cd, q_norm, kv_norm, w_uq, w_ukv, w_out_cd, w_up, w_down):
    depth = norm_mix.shape[0]
    for layer in range(depth):
        last = layer == depth - 1
        if layer % 2 == 0:
            e = layer // 2
            x = _even_layer(x, norm_mix[layer], w_in_ab[e], b_forget[e], rel_bias[e], w_out_ab[e],
                            norm_mlp[layer], w_up[layer], w_down[layer], norm_final, last)
        else:
            o = layer // 2
            x = _odd_layer(x, positions, norm_mix[layer], w_in_cd[o], q_norm[o], kv_norm[o],
                           w_uq[o], w_ukv[o], w_out_cd[o], norm_mlp[layer], w_up[layer],
                           w_down[layer], norm_final, last)
    return x
```

```python
import functools
import math

import numpy as np
import jax
import jax.numpy as jnp
from jax import lax
from jax.experimental import pallas as pl
from jax.experimental.pallas import tpu as pltpu

F32 = jnp.float32
BF16 = jnp.bfloat16

EPS = 1e-6
HEAD_DIM = 64
CHUNK = 64
N_LEFT_CHUNKS = 8
REL_CLIP = 256
ROPE_DIM = 32
NOPE_DIM = 64
ROPE_THETA = 10000.0
Q_LORA = 384
KV_LORA = 256

LANES = 128
VT_BLK = LANES
SUB = LANES
FLASH_HP = 8
FLASH_BQ = 512
FLASH_SUB = 256
FIXED_WIDTHS = (4, 2, 1)
SAFE_GAP = 80.0
NEG = -1e30
LOG2E = math.log2(math.e)
SB_ZERO_LOG = -104.0
VMEM_LIMIT = 56 * 1024 * 1024

_NT = (((1,), (1,)), ((), ()))


def _cparams(sem):
    return pltpu.CompilerParams(dimension_semantics=sem, vmem_limit_bytes=VMEM_LIMIT)


def _rms_bf16(x, g):
    ms = jnp.mean(x * x, axis=-1, keepdims=True)
    return (x * lax.rsqrt(ms + EPS) * g).astype(BF16)


def _store_vt(ovt_ref, vt, row0):
    rows, tm = vt.shape
    for c in range(tm // VT_BLK):
        ovt_ref[c, row0:row0 + rows, :] = vt[:, c * VT_BLK:(c + 1) * VT_BLK]


def _inproj_kernel(x_ref, g_ref, wm_ref, wvt_ref, wa_ref, om_ref, ovt_ref, oa_ref):
    h = _rms_bf16(x_ref[...], g_ref[...])
    nm = om_ref.shape[-1]
    for c in range(0, nm, 512):
        om_ref[:, c:c + 512] = jnp.dot(
            h, wm_ref[:, c:c + 512], preferred_element_type=F32).astype(BF16)
    nv = wvt_ref.shape[0]
    for r in range(0, nv, 256):
        vt = lax.dot_general(wvt_ref[r:r + 256, :], h, _NT,
                             preferred_element_type=F32).astype(BF16)
        _store_vt(ovt_ref, vt, r)
    oa_ref[...] = jnp.dot(h, wa_ref[...], preferred_element_type=F32)


def _inproj(x, g, wm, wvt, wa, tm=1024):
    B, S, D = x.shape
    nm, nv, na = wm.shape[1], wvt.shape[0], wa.shape[1]
    return pl.pallas_call(
        _inproj_kernel,
        grid=(B, S // tm),
        in_specs=[
            pl.BlockSpec((None, tm, D), lambda b, i: (b, i, 0)),
            pl.BlockSpec((1, D), lambda b, i: (0, 0)),
            pl.BlockSpec((D, nm), lambda b, i: (0, 0)),
            pl.BlockSpec((nv, D), lambda b, i: (0, 0)),
            pl.BlockSpec((D, na), lambda b, i: (0, 0)),
        ],
        out_specs=[
            pl.BlockSpec((None, tm, nm), lambda b, i: (b, i, 0)),
            pl.BlockSpec((None, tm // VT_BLK, nv, VT_BLK), lambda b, i: (b, i, 0, 0)),
            pl.BlockSpec((None, tm, na), lambda b, i: (b, i, 0)),
        ],
        out_shape=[
            jax.ShapeDtypeStruct((B, S, nm), BF16),
            jax.ShapeDtypeStruct((B, S // VT_BLK, nv, VT_BLK), BF16),
            jax.ShapeDtypeStruct((B, S, na), F32),
        ],
        compiler_params=_cparams(("arbitrary", "arbitrary")),
        name="inproj",
    )(x, g.reshape(1, D), wm, wvt, wa)


def _split3(x):
    hi = x.astype(BF16)
    r = x - hi.astype(F32)
    mid = r.astype(BF16)
    lo = (r - mid.astype(F32)).astype(BF16)
    return hi, mid, lo


AUG_W = 8


def _logcum_kernel(fa_ref, b_ref, pqk_ref, oneq_ref, onek_ref, oq_ref, ok_ref, carry_ref):
    @pl.when(pl.program_id(1) == 0)
    def _():
        carry_ref[...] = jnp.zeros_like(carry_ref)

    z = fa_ref[...] + b_ref[...]
    lf = jnp.minimum(z, 0.0) - jnp.log(1.0 + jnp.exp(-jnp.abs(z)))
    tc = lf.shape[0]
    r = lax.broadcasted_iota(jnp.int32, (tc, tc), 0)
    c = lax.broadcasted_iota(jnp.int32, (tc, tc), 1)
    tri = jnp.where(r >= c, 1.0, 0.0).astype(BF16)
    d = jnp.dot(tri, jnp.concatenate(_split3(lf), axis=1), preferred_element_type=F32)
    w = lf.shape[1]
    cs = carry_ref[...]
    for n in range(3):
        cs = cs + d[:, n * w:(n + 1) * w]
    carry_ref[...] = cs[tc - 1:tc, :]
    placed = jnp.dot(jnp.concatenate(_split3(cs * LOG2E), axis=1),
                     pqk_ref[...], preferred_element_type=F32)
    na = oq_ref.shape[-1]
    oq_ref[...] = (oneq_ref[...] + placed[:, 0:na]).astype(BF16)
    ok_ref[...] = (onek_ref[...] + placed[:, na:2 * na]).astype(BF16)


def _logcum(fa, bias, n_heads, tc=512):
    B, S, W = fa.shape
    na = LANES
    pq = np.zeros((3, W, na), np.float32)
    pk = np.zeros((3, W, na), np.float32)
    oneq = np.zeros((1, na), np.float32)
    onek = np.zeros((1, na), np.float32)
    for h in range(n_heads):
        base = h * AUG_W
        for n in range(3):
            pq[n, h, base + n] = 1.0
            pk[n, h, base + 3 + n] = -1.0
        oneq[0, base + 3:base + 6] = 1.0
        onek[0, base:base + 3] = 1.0
    const = lambda a: pl.BlockSpec(a.shape, lambda b, i: (0,) * a.ndim)
    pqk = np.concatenate([pq, pk], axis=2).reshape(3 * W, 2 * na)
    args = [jnp.asarray(pqk, BF16), jnp.asarray(oneq), jnp.asarray(onek)]
    return pl.pallas_call(
        _logcum_kernel,
        grid=(B, S // tc),
        in_specs=[pl.BlockSpec((None, tc, W), lambda b, i: (b, i, 0)),
                  pl.BlockSpec((1, W), lambda b, i: (0, 0))] + [const(a) for a in args],
        out_specs=[pl.BlockSpec((None, tc, na), lambda b, i: (b, i, 0))] * 2,
        out_shape=[jax.ShapeDtypeStruct((B, S, na), BF16)] * 2,
        scratch_shapes=[pltpu.VMEM((1, W), F32)],
        compiler_params=_cparams(("arbitrary", "arbitrary")),
        name="logcum",
    )(fa, bias, *args)


def _pair_mask_q(q2, j):
    lane = lax.broadcasted_iota(jnp.int32, q2.shape, 1)
    keep = (lane >= HEAD_DIM * j) & (lane < HEAD_DIM * (j + 1))
    return jnp.where(keep, q2, jnp.zeros_like(q2))


ONES_ROWS = 16


def _softmax_step(tiles, vts, carry, tile_max=None):
    m, acc = carry
    if tile_max is not None:
        m_new = jnp.maximum(m, tile_max)
    else:
        m_new = m
        for tile in tiles:
            m_new = jnp.maximum(m_new, jnp.max(tile(), axis=0, keepdims=True))
    alpha = jnp.exp2(m - m_new)
    pv = None
    for tile, vt in zip(tiles, vts):
        p = jnp.exp2(tile() - m_new).astype(BF16)
        vt1 = jnp.concatenate([vt, jnp.ones((ONES_ROWS, vt.shape[1]), BF16)], axis=0)
        d = jnp.dot(vt1, p, preferred_element_type=F32)
        pv = d if pv is None else pv + d
    return m_new, alpha * acc + pv


def _softmax_init(bq):
    return (jnp.full((1, bq), NEG, F32), jnp.zeros((HEAD_DIM + ONES_ROWS, bq), F32))


def _softmax_out(carry):
    _, acc = carry
    return acc[0:HEAD_DIM] / acc[HEAD_DIM:HEAD_DIM + 1]


def _store_heads(o_ref, outs):
    oT = jnp.concatenate(outs, axis=0)
    o_ref[...] = oT.T.astype(o_ref.dtype)


def _flash_kernel(*refs, mode, hp, bq):
    if mode == "fox":
        q_ref, k_ref, vt_ref, qaug_ref, kaug_ref, o_ref = refs[:6]
    else:
        q_ref, k_ref, vt_ref, o_ref = refs[:4]
    sa_ref, sb_ref, ma_ref, mb_ref, qt_ref, kn_ref = refs[-6:]
    qs = pl.program_id(2) * bq
    sub = FLASH_SUB
    row = lax.broadcasted_iota(jnp.int32, (sub, bq), 0)
    col = lax.broadcasted_iota(jnp.int32, (sub, bq), 1)
    if mode == "fox":
        kcols = [slice(LANES * (h // 2), LANES * (h // 2 + 1)) for h in range(hp)]
        lane = lax.broadcasted_iota(jnp.int32, (bq, LANES), 1)
        qa = qaug_ref[...]
        qms = []
        for h in range(hp):
            first = AUG_W * (pl.program_id(1) * hp + h)
            own = (lane >= first) & (lane < first + AUG_W)
            qms.append(jnp.concatenate(
                [_pair_mask_q(q_ref[:, kcols[h]], h % 2),
                 jnp.where(own, qa, jnp.zeros_like(qa))], axis=1))
    else:
        kcols = [slice(LANES * h, LANES * (h + 1)) for h in range(hp)]
        qms = [q_ref[:, kcols[h]] for h in range(hp)]
    for h in range(hp):
        qt_ref[h] = qms[h].T

    @pl.when(pl.program_id(2) == 0)
    def _():
        klane = lax.broadcasted_iota(jnp.int32, (1, LANES), 1)
        for h in range(hp):
            kabs = jnp.max(jnp.abs(k_ref[:, kcols[h]].astype(F32)), axis=0, keepdims=True)
            sq = kabs * kabs
            if mode == "fox":
                sq = jnp.where((klane >= HEAD_DIM * (h % 2)) & (klane < HEAD_DIM * (h % 2 + 1)),
                               sq, 0.0)
            kn_ref[h] = jnp.broadcast_to(jnp.sqrt(jnp.sum(sq, axis=1, keepdims=True)), (1, bq))

    def scores(sb, h, masked, col0=0):
        ks = pl.multiple_of(sb * sub, sub)
        k = k_ref[pl.ds(ks, sub), kcols[h]]
        if mode == "fox":
            k = jnp.concatenate([k, kaug_ref[pl.ds(ks, sub), :]], axis=1)
        sT = jnp.dot(k, qt_ref[h, :, col0:], preferred_element_type=F32)
        if masked and mode == "fox":
            sT = jnp.where((ks + row <= qs + col)[:, col0:], sT, NEG)
        elif masked:
            sT = jnp.where((((ks + row) >> 6) <= ((qs + col) >> 6))[:, col0:], sT, NEG)
        return sT

    nsub = bq // sub
    nvt = sub // VT_BLK

    def produce(buf, sb0, masked, h):
        s_buf, m_buf = buf
        tile_max = None
        for c in range(nsub):
            col0 = c * sub if masked else 0
            sT = scores(sb0 + c, h, masked, col0)
            cm = jnp.max(sT, axis=0, keepdims=True)
            if col0:
                s_buf[h, c, :, 0:col0] = jnp.full((sub, col0), NEG, F32)
                cm = jnp.concatenate([jnp.full((1, col0), NEG, F32), cm], axis=1)
            s_buf[h, c, :, col0:] = sT
            tile_max = cm if tile_max is None else jnp.maximum(tile_max, cm)
        m_buf[h] = tile_max

    def consume(buf, sb0, carry, h):
        s_buf, m_buf = buf
        tiles = [lambda c=c: s_buf[h, c] for c in range(nsub)]
        vts = [jnp.concatenate([vt_ref[(sb0 + c) * nvt + v, HEAD_DIM * h:HEAD_DIM * (h + 1), :]
                                for v in range(nvt)], axis=1) for c in range(nsub)]
        return _softmax_step(tiles, vts, carry, tile_max=m_buf[h])

    def stage(cur, cur_sb, nxt, nxt_sb, carries):
        if nxt is not None:
            for h in range(hp):
                produce(nxt, nxt_sb, False, h)
        return tuple(consume(cur, cur_sb, carries[h], h) for h in range(hp))

    n = pl.program_id(2)
    diag_sb = qs // sub
    buf_a, buf_b = (sa_ref, ma_ref), (sb_ref, mb_ref)
    for h in range(hp):
        produce(buf_a, diag_sb, True, h)

    def pair(j, carries):
        carries = stage(buf_a, jnp.where(j == 0, diag_sb, (2 * j - 1) * nsub),
                        buf_b, 2 * j * nsub, carries)
        return stage(buf_b, 2 * j * nsub,
                     buf_a, jnp.minimum(2 * j + 1, n - 1) * nsub, carries)

    def online(_):
        carries = tuple(_softmax_init(bq) for _ in range(hp))
        carries = lax.fori_loop(0, (n + 1) // 2, pair, carries)
        carries = lax.cond(
            n % 2 == 0,
            lambda c: stage(buf_a, jnp.where(n == 0, diag_sb, (n - 1) * nsub), None, None, c),
            lambda c: c, carries)
        return jnp.concatenate([_softmax_out(c) for c in carries], axis=0)

    refs_ = []
    gap = None
    for h in range(hp):
        qf = qt_ref[h, 0:LANES, :].astype(F32)
        bound = jnp.sqrt(jnp.sum(qf * qf, axis=0, keepdims=True)) * kn_ref[h] * 1.01 + 1e-3
        refs_.append(bound)
        g = jnp.max(bound - ma_ref[h])
        gap = g if gap is None else jnp.maximum(gap, g)

    def weigh_add(acc, h, s, sb0, nblk, col0=0):
        p = jnp.exp2(s - refs_[h][:, col0:]).astype(BF16)
        vt = jnp.concatenate([vt_ref[sb0 * nvt + v, HEAD_DIM * h:HEAD_DIM * (h + 1), :]
                              for v in range(nblk * nvt)], axis=1)
        vt1 = jnp.concatenate([vt, jnp.ones((ONES_ROWS, nblk * sub), BF16)], axis=0)
        d = jnp.dot(vt1, p, preferred_element_type=F32)
        if col0:
            d = jnp.concatenate([jnp.zeros((d.shape[0], col0), F32), d], axis=1)
        return acc + d

    def fixed_reference(_):
        def run(kb, accs, nblk):
            ks = pl.multiple_of(kb * sub, sub)
            ss = []
            for h in range(hp):
                k = k_ref[pl.ds(ks, nblk * sub), kcols[h]]
                if mode == "fox":
                    k = jnp.concatenate([k, kaug_ref[pl.ds(ks, nblk * sub), :]], axis=1)
                ss.append(jnp.dot(k, qt_ref[h], preferred_element_type=F32))
            return tuple(weigh_add(accs[h], h, ss[h], kb, nblk) for h in range(hp))

        accs = tuple(jnp.zeros((HEAD_DIM + ONES_ROWS, bq), F32) for _ in range(hp))
        done = 0
        for width in FIXED_WIDTHS:
            trips = (diag_sb - done) // width
            accs = lax.fori_loop(0, trips,
                                 lambda j, a, done=done, width=width: run(done + j * width, a, width),
                                 accs)
            done = done + trips * width
        for c in range(nsub):
            accs = [weigh_add(accs[h], h, sa_ref[h, c, :, c * sub:], diag_sb + c, 1, c * sub)
                    for h in range(hp)]
        return jnp.concatenate([a[0:HEAD_DIM] / a[HEAD_DIM:HEAD_DIM + 1] for a in accs], axis=0)

    oT = lax.cond(gap <= SAFE_GAP, fixed_reference, online, None)
    o_ref[...] = oT.T.astype(o_ref.dtype)


def _flash(mode, q_arr, q_col0, k_arr, k_col0, vt_arr, vt_row0, extra, n_heads,
           hp=FLASH_HP, bq=FLASH_BQ):
    B, S, _ = q_arr.shape
    qw = (HEAD_DIM if mode == "fox" else LANES) * hp
    vw = HEAD_DIM * hp
    in_specs = [
        pl.BlockSpec((None, bq, qw), lambda b, g, i: (b, i, q_col0 // qw + g)),
        pl.BlockSpec((None, S, qw), lambda b, g, i: (b, 0, k_col0 // qw + g)),
        pl.BlockSpec((None, S // VT_BLK, vw, VT_BLK),
                     lambda b, g, i: (b, 0, vt_row0 // vw + g, 0)),
    ]
    args = [q_arr, k_arr, vt_arr]
    if mode == "fox":
        q_aug, k_aug = extra
        in_specs += [
            pl.BlockSpec((None, bq, LANES), lambda b, g, i: (b, i, 0)),
            pl.BlockSpec((None, S, LANES), lambda b, g, i: (b, 0, 0)),
        ]
        args += [q_aug, k_aug]
    return pl.pallas_call(
        functools.partial(_flash_kernel, mode=mode, hp=hp, bq=bq),
        grid=(B, n_heads // hp, S // bq),
        in_specs=in_specs,
        out_specs=pl.BlockSpec((None, bq, vw), lambda b, g, i: (b, i, g)),
        out_shape=jax.ShapeDtypeStruct((B, S, n_heads * HEAD_DIM), BF16),
        scratch_shapes=([pltpu.VMEM((hp, bq // FLASH_SUB, FLASH_SUB, bq), F32)] * 2
                        + [pltpu.VMEM((hp, 1, bq), F32)] * 2
                        + [pltpu.VMEM((hp, 2 * LANES if mode == "fox" else LANES, bq), BF16),
                           pltpu.VMEM((hp, 1, bq), F32)]),
        compiler_params=_cparams(("arbitrary", "arbitrary", "arbitrary")),
        name="flash_" + mode,
    )(*args)


CK_B = 2 * CHUNK
CK_NW = N_LEFT_CHUNKS * CHUNK // CK_B + 1
CK_EXT = (CK_NW + 1) * CK_B


def _chunk_kernel(q_ref, k_ref, vt_ref, ext_ref, o_ref, tab_ref, s_ref, *, hp, nq):
    i = pl.program_id(1)

    @pl.when(i == 0)
    def _():
        jj = lax.broadcasted_iota(jnp.int32, (CK_B, CK_B), 0)
        rr = lax.broadcasted_iota(jnp.int32, (CK_B, CK_B), 1)
        for h in range(hp):
            for w in range(CK_NW):
                a = (CK_NW - 1 - w) * CK_B
                g = jnp.broadcast_to(ext_ref[h:h + 1, a:a + 2 * CK_B], (CK_B, 2 * CK_B))
                t = pltpu.roll(g, CK_B, 1, stride=1, stride_axis=0)[:, :CK_B]
                if w == 0:
                    t = jnp.where((rr >= CHUNK) & (jj < CHUNK), NEG, t)
                if w == CK_NW - 1:
                    t = jnp.where((rr < CHUNK) & (jj >= CHUNK), NEG, t)
                tab_ref[h, w * CK_B:(w + 1) * CK_B, :] = t

    kcols = [slice(LANES * (h // 2), LANES * (h // 2 + 1)) for h in range(hp)]
    firsts = [i * nq + u - (CK_NW - 1) for u in range(nq)]

    def finish():
        for u in range(nq):
            kbc = [jnp.maximum(firsts[u] + w, 0) for w in range(CK_NW)]
            outs = []
            for h in range(hp):
                vt = jnp.concatenate([vt_ref[kbc[w], HEAD_DIM * h:HEAD_DIM * (h + 1), :]
                                      for w in range(CK_NW)], axis=1)
                outs.append(_softmax_out(_softmax_step([lambda u=u, h=h: s_ref[u, h]], [vt],
                                                       _softmax_init(CK_B))))
            oT = jnp.concatenate(outs, axis=0)
            o_ref[u * CK_B:(u + 1) * CK_B, :] = oT.T.astype(o_ref.dtype)

    def pair_scores(u, p, ks, nrows):
        q2 = q_ref[u * CK_B:(u + 1) * CK_B, kcols[2 * p]]
        qq = jnp.concatenate([_pair_mask_q(q2, 0), _pair_mask_q(q2, 1)], axis=0)
        return lax.dot_general(k_ref[pl.ds(ks, nrows), kcols[2 * p]], qq, _NT,
                               preferred_element_type=F32)

    @pl.when(firsts[0] >= 0)
    def _():
        for u in range(nq):
            ks = pl.multiple_of(firsts[u] * CK_B, CK_B)
            for p in range(hp // 2):
                sT = pair_scores(u, p, ks, CK_NW * CK_B)
                for j in range(2):
                    s_ref[u, 2 * p + j] = sT[:, j * CK_B:(j + 1) * CK_B] + tab_ref[2 * p + j]
        finish()

    @pl.when(firsts[0] < 0)
    def _():
        for u in range(nq):
            for p in range(hp // 2):
                for w in range(CK_NW):
                    rows = slice(w * CK_B, (w + 1) * CK_B)
                    ks = pl.multiple_of(jnp.maximum(firsts[u] + w, 0) * CK_B, CK_B)
                    sT = pair_scores(u, p, ks, CK_B)
                    for j in range(2):
                        s_ref[u, 2 * p + j, rows, :] = jnp.where(
                            firsts[u] + w >= 0,
                            sT[:, j * CK_B:(j + 1) * CK_B] + tab_ref[2 * p + j, rows, :], NEG)
        finish()


def _chunk_attn(main, q_col0, k_col0, vt_arr, vt_row0, ext, n_heads, nq=4):
    B, S, _ = main.shape
    hp = n_heads
    qw, vw = HEAD_DIM * hp, HEAD_DIM * hp
    return pl.pallas_call(
        functools.partial(_chunk_kernel, hp=hp, nq=nq),
        grid=(B, S // (nq * CK_B)),
        in_specs=[
            pl.BlockSpec((None, nq * CK_B, qw), lambda b, i: (b, i, q_col0 // qw)),
            pl.BlockSpec((None, S, qw), lambda b, i: (b, 0, k_col0 // qw)),
            pl.BlockSpec((None, S // VT_BLK, vw, VT_BLK), lambda b, i: (b, 0, vt_row0 // vw, 0)),
            pl.BlockSpec((hp, CK_EXT), lambda b, i: (0, 0)),
        ],
        out_specs=pl.BlockSpec((None, nq * CK_B, vw), lambda b, i: (b, i, 0)),
        out_shape=jax.ShapeDtypeStruct((B, S, n_heads * HEAD_DIM), BF16),
        scratch_shapes=[pltpu.VMEM((hp, CK_NW * CK_B, CK_B), F32),
                        pltpu.VMEM((nq, hp, CK_NW * CK_B, CK_B), F32)],
        compiler_params=_cparams(("arbitrary", "arbitrary")),
        name="chunk_attn",
    )(main, main, vt_arr, ext)


def _sb_kernel(q_ref, k_ref, vt_ref, o_ref, z_ref, lb_ref, sfx_ref, *, hp, bq):
    qs = pl.program_id(1) * bq
    nsub = bq // SUB
    row = lax.broadcasted_iota(jnp.int32, (SUB, bq), 0)
    col = lax.broadcasted_iota(jnp.int32, (SUB, bq), 1)
    ur = lax.broadcasted_iota(jnp.int32, (SUB, 2 * SUB), 0)
    uc = lax.broadcasted_iota(jnp.int32, (SUB, 2 * SUB), 1) & (SUB - 1)
    upper2 = jnp.where(uc > ur, 1.0, 0.0).astype(BF16)
    kcols = [slice(LANES * (h // 2), LANES * (h // 2 + 1)) for h in range(hp)]
    qms = [_pair_mask_q(q_ref[:, kcols[h]], h % 2) for h in range(hp)]

    def step(kb, carries, masked, live=None):
        ks = pl.multiple_of(kb * bq, bq)
        for h in range(hp):
            z_ref[h] = lax.dot_general(k_ref[pl.ds(ks, bq), kcols[h]], qms[h], _NT,
                                       preferred_element_type=F32)
        first_col = [c * SUB if masked else 0 for c in range(nsub)]

        def widen(x, c):
            if first_col[c] == 0:
                return x
            return jnp.concatenate([jnp.zeros((x.shape[0], first_col[c]), x.dtype), x], axis=1)

        totals = []
        for h in range(hp):
            tot = []
            for c in range(nsub):
                rows, cols = slice(c * SUB, (c + 1) * SUB), slice(first_col[c], bq)
                z = z_ref[h, rows, cols]
                l1 = jnp.log(1.0 + jnp.exp2(jnp.abs(z) * (-LOG2E)))
                log_beta = jnp.minimum(z, 0.0) - l1
                log_keep = log_beta - z
                if masked:
                    valid = (ks + c * SUB + row < qs + col)[:, cols]
                    log_keep = jnp.where(valid, log_keep, 0.0)
                lb_ref[h, rows, cols] = log_beta
                hi = log_keep.astype(BF16)
                lo = (log_keep - hi.astype(F32)).astype(BF16)
                sfx = jnp.dot(upper2, jnp.concatenate([hi, lo], axis=0),
                              preferred_element_type=F32)
                sfx_ref[h, rows, cols] = sfx
                tot.append(widen(sfx[0:1, :] + log_keep[0:1, :], c))
            totals.append(tot)
        out = []
        for h in range(hp):
            tail, acc = carries[h]
            if live is not None:
                tail = jnp.where(live, tail, NEG)
            parts = [None] * nsub
            for c in range(nsub - 1, -1, -1):
                rows, cols = slice(c * SUB, (c + 1) * SUB), slice(first_col[c], bq)
                a = jnp.exp(lb_ref[h, rows, cols] + sfx_ref[h, rows, cols] + tail[:, cols])
                if masked:
                    a = jnp.where((ks + c * SUB + row < qs + col)[:, cols], a, 0.0)
                parts[c] = a.astype(BF16)
                tail = tail + totals[h][c]
            vts = [vt_ref[kb * nsub + c, HEAD_DIM * h:HEAD_DIM * (h + 1), :] for c in range(nsub)]
            if masked:
                for c in range(nsub):
                    acc = acc + widen(jnp.dot(vts[c], parts[c], preferred_element_type=F32), c)
            else:
                acc = acc + jnp.dot(jnp.concatenate(vts, axis=1), jnp.concatenate(parts, axis=0),
                                    preferred_element_type=F32)
            out.append((tail, acc))
        return tuple(out)

    n_full = qs // bq
    carries = tuple((jnp.zeros((1, bq), F32), jnp.zeros((HEAD_DIM, bq), F32))
                    for _ in range(hp))
    carries = step(n_full, carries, True)
    carries = step(jnp.maximum(n_full - 1, 0), carries, False, live=n_full >= 1)

    def cond(state):
        kb, carries = state
        tail_max = carries[0][0]
        for h in range(1, hp):
            tail_max = jnp.maximum(tail_max, carries[h][0])
        return (kb >= 0) & (jnp.max(tail_max) > SB_ZERO_LOG)

    def body(state):
        kb, carries = state
        return kb - 1, step(kb, carries, False)

    _, carries = lax.while_loop(cond, body, (n_full - 2, carries))
    _store_heads(o_ref, [acc for (_, acc) in carries])


def _sb_attn(main, q_col0, k_col0, vt_arr, vt_row0, n_heads, bq=256):
    B, S, _ = main.shape
    hp = n_heads
    qw = HEAD_DIM * hp
    return pl.pallas_call(
        functools.partial(_sb_kernel, hp=hp, bq=bq),
        grid=(B, S // bq),
        in_specs=[
            pl.BlockSpec((None, bq, qw), lambda b, i: (b, i, q_col0 // qw)),
            pl.BlockSpec((None, S, qw), lambda b, i: (b, 0, k_col0 // qw)),
            pl.BlockSpec((None, S // VT_BLK, qw, VT_BLK), lambda b, i: (b, 0, vt_row0 // qw, 0)),
        ],
        out_specs=pl.BlockSpec((None, bq, qw), lambda b, i: (b, i, 0)),
        out_shape=jax.ShapeDtypeStruct((B, S, n_heads * HEAD_DIM), BF16),
        scratch_shapes=[pltpu.VMEM((hp, bq, bq), F32)] * 3,
        compiler_params=_cparams(("arbitrary", "arbitrary")),
        name="sb_attn",
    )(main, main, vt_arr)


def _mla_prep_kernel(aux_ref, pos_ref, invf_ref, qn_ref, kvn_ref, wuq_ref, wuqr_ref,
                     wk_ref, wvt_ref, oq_ref, ok_ref, ovt_ref, *, q_scale):
    ang = pos_ref[...] * invf_ref[...]
    cos4, sin4 = jnp.cos(ang), jnp.sin(ang)
    lane = lax.broadcasted_iota(jnp.int32, ang.shape, 1)
    rotary = (lane >= NOPE_DIM) & (lane < NOPE_DIM + ROPE_DIM)
    cos_rows, sin_rows = [], []
    for m in range(LANES // ROPE_DIM):
        shift = (NOPE_DIM - ROPE_DIM * m) % LANES
        cm = cos4 if shift == 0 else pltpu.roll(cos4, shift, 1)
        sm = sin4 if shift == 0 else pltpu.roll(sin4, shift, 1)
        cos_rows.append(jnp.where(rotary, cm, 1.0))
        sin_rows.append(jnp.where(rotary, sm, 0.0))
    cos = jnp.concatenate(cos_rows, axis=0)
    sin = jnp.concatenate(sin_rows, axis=0)
    cq = _rms_bf16(aux_ref[:, 0:Q_LORA], qn_ref[...])
    ckv = _rms_bf16(aux_ref[:, Q_LORA:Q_LORA + KV_LORA], kvn_ref[...])
    o = Q_LORA + KV_LORA
    k_rope = aux_ref[:, o:o + LANES] * cos + aux_ref[:, o + LANES:o + 2 * LANES] * sin
    cos_q, sin_q = cos * q_scale, sin * q_scale
    n_heads = oq_ref.shape[-1] // LANES
    qr = jnp.dot(cq, wuqr_ref[...], preferred_element_type=F32)
    per_group = LANES // ROPE_DIM
    for h in range(0, n_heads, 2):
        cols = slice(h * LANES, (h + 2) * LANES)
        qa = jnp.dot(cq, wuq_ref[:, cols], preferred_element_type=F32)
        kn = jnp.dot(ckv, wk_ref[:, cols], preferred_element_type=F32)
        for d in range(2):
            c1 = slice(d * LANES, (d + 1) * LANES)
            c2 = slice((h + d) * LANES, (h + d + 1) * LANES)
            grp, m = divmod(h + d, per_group)
            qb = qr[:, grp * LANES:(grp + 1) * LANES]
            shift = (NOPE_DIM - ROPE_DIM * m) % LANES
            if shift:
                qb = pltpu.roll(qb, shift, 1)
            oq_ref[:, c2] = (qa[:, c1] * cos_q + qb * sin_q).astype(BF16)
            ok_ref[:, c2] = (kn[:, c1] + k_rope).astype(BF16)
    nv = wvt_ref.shape[0]
    for r in range(0, nv, 256):
        vt = lax.dot_general(wvt_ref[r:r + 256, :], ckv, _NT,
                             preferred_element_type=F32).astype(BF16)
        _store_vt(ovt_ref, vt, r)


MLA_TM = 512


def _mla_prep(aux, pos, invf, qn, kvn, wuq, wuqr, wk, wvt, q_scale, tm=MLA_TM):
    B, S, na = aux.shape
    nq, nv = wuq.shape[1], wvt.shape[0]
    full = lambda a: pl.BlockSpec(a.shape, lambda b, i: (0,) * a.ndim)
    return pl.pallas_call(
        functools.partial(_mla_prep_kernel, q_scale=q_scale),
        grid=(B, S // tm),
        in_specs=[
            pl.BlockSpec((None, tm, na), lambda b, i: (b, i, 0)),
            pl.BlockSpec((None, None) + pos.shape[2:], lambda b, i: (b, i, 0, 0)),
            full(invf), full(qn), full(kvn), full(wuq), full(wuqr), full(wk), full(wvt),
        ],
        out_specs=[
            pl.BlockSpec((None, tm, nq), lambda b, i: (b, i, 0)),
            pl.BlockSpec((None, tm, nq), lambda b, i: (b, i, 0)),
            pl.BlockSpec((None, tm // VT_BLK, nv, VT_BLK), lambda b, i: (b, i, 0, 0)),
        ],
        out_shape=[
            jax.ShapeDtypeStruct((B, S, nq), BF16),
            jax.ShapeDtypeStruct((B, S, nq), BF16),
            jax.ShapeDtypeStruct((B, S // VT_BLK, nv, VT_BLK), BF16),
        ],
        compiler_params=_cparams(("arbitrary", "arbitrary")),
        name="mla_prep",
    )(aux, pos, invf, qn, kvn, wuq, wuqr, wk, wvt)


def _mlp_kernel(x_ref, oa_ref, ob_ref, wo_ref, g_ref, wu_ref, wd_ref, gf_ref, out_ref,
                x1_ref, h_ref, a_ref, *, final_norm, tf):
    na = oa_ref.shape[-1]
    x1 = (x_ref[...]
          + jnp.dot(oa_ref[...], wo_ref[0:na, :], preferred_element_type=F32)
          + jnp.dot(ob_ref[...], wo_ref[na:, :], preferred_element_type=F32))
    x1_ref[...] = x1
    h_ref[...] = _rms_bf16(x1, g_ref[...])
    for f in range(0, wu_ref.shape[1], tf):
        u = jnp.dot(h_ref[...], wu_ref[:, f:f + tf], preferred_element_type=F32)
        a_ref[:, f:f + tf] = jnp.square(jnp.maximum(u, 0.0)).astype(BF16)
    y = x1_ref[...] + jnp.dot(a_ref[...], wd_ref[...], preferred_element_type=F32)
    if final_norm:
        ms = jnp.mean(y * y, axis=-1, keepdims=True)
        y = y * lax.rsqrt(ms + EPS) * gf_ref[...]
    out_ref[...] = y


def _mlp(x2, oa, ob, wo, g, wu, wd, gf, final_norm, tm=1024, tf=1024):
    T, D = x2.shape
    F = wu.shape[1]
    na, nb = oa.shape[1], ob.shape[1]
    const = lambda shape: pl.BlockSpec(shape, lambda i: (0, 0), pipeline_mode=pl.Buffered(1))
    return pl.pallas_call(
        functools.partial(_mlp_kernel, final_norm=final_norm, tf=tf),
        grid=(T // tm,),
        in_specs=[
            pl.BlockSpec((tm, D), lambda i: (i, 0)),
            pl.BlockSpec((tm, na), lambda i: (i, 0)),
            pl.BlockSpec((tm, nb), lambda i: (i, 0)),
            const((na + nb, D)),
            const((1, D)),
            const((D, F)),
            const((F, D)),
            const((1, D)),
        ],
        out_specs=pl.BlockSpec((tm, D), lambda i: (i, 0)),
        out_shape=jax.ShapeDtypeStruct((T, D), F32),
        scratch_shapes=[pltpu.VMEM((tm, D), F32), pltpu.VMEM((tm, D), BF16),
                        pltpu.VMEM((tm, F), BF16)],
        compiler_params=_cparams(("arbitrary",)),
        name="mlp",
    )(x2, oa, ob, wo, g.reshape(1, D), wu, wd, gf.reshape(1, D))


def _pad_cols(w, n):
    return jnp.pad(w, ((0, 0), (0, n - w.shape[1])))


def _rot_cols(w):
    half = ROPE_DIM // 2
    return jnp.concatenate([-w[:, half:], w[:, :half]], axis=1)


def _rope_slab(w):
    z = jnp.zeros((w.shape[0], NOPE_DIM), w.dtype)
    return jnp.concatenate([z, w, jnp.zeros((w.shape[0], LANES - NOPE_DIM - ROPE_DIM), w.dtype)], axis=1)


def _even_layer(x, g_mix, w_in, b_forget, rel_bias, w_out, g_mlp, w_up, w_down, g_final,
                final_norm):
    B, S, D = x.shape
    hf, hc = b_forget.shape[0], rel_bias.shape[0]
    wf, wc = hf * HEAD_DIM, hc * HEAD_DIM
    o = np.cumsum([0, wf, wf, wf, hf, wc, wc, wc])
    qa, ka, va, fa, qb, kb, vb = [w_in[:, o[n]:o[n + 1]] for n in range(7)]
    q_scale = HEAD_DIM ** -0.5 * LOG2E
    wm = jnp.concatenate([qa * q_scale, ka, qb * q_scale, kb], axis=1).astype(BF16)
    wvt = jnp.concatenate([va, vb], axis=1).T.astype(BF16)
    wa = _pad_cols(fa, LANES).astype(BF16)
    main, vt, aux = _inproj(x, g_mix, wm, wvt, wa)

    q_aug, k_aug = _logcum(aux, _pad_cols(b_forget.reshape(1, hf), LANES), hf)
    o_a = _flash("fox", main, 0, main, wf, vt, 0, (q_aug, k_aug), hf)

    assert rel_bias.shape[1] == CHUNK + REL_CLIP
    right = CK_EXT - rel_bias.shape[1] - (CHUNK + 1)
    ext = jnp.pad(rel_bias * LOG2E, ((0, 0), (CHUNK + 1, right)), mode="edge")
    o_b = _chunk_attn(main, 2 * wf, 2 * wf + wc, vt, wf, ext, hc)

    y = _mlp(x.reshape(B * S, D), o_a.reshape(B * S, wf), o_b.reshape(B * S, wc),
             w_out.astype(BF16), g_mlp, w_up.astype(BF16), w_down.astype(BF16), g_final,
             final_norm)
    return y.reshape(B, S, D)


def _odd_layer(x, positions, g_mix, w_in, q_norm, kv_norm, w_uq, w_ukv, w_out, g_mlp, w_up,
               w_down, g_final, final_norm):
    B, S, D = x.shape
    hm = w_ukv.shape[1] // (NOPE_DIM + HEAD_DIM)
    ws = w_in.shape[1] - Q_LORA - KV_LORA - ROPE_DIM
    hs = (ws // 3) // HEAD_DIM
    wsb = hs * HEAD_DIM
    o = np.cumsum([0, wsb, wsb, wsb, Q_LORA, KV_LORA, ROPE_DIM])
    qc, kc, vc, w_cq, w_ckv, w_kr = [w_in[:, o[n]:o[n + 1]] for n in range(6)]
    wm = jnp.concatenate([qc * HEAD_DIM ** -0.5, kc], axis=1).astype(BF16)
    wa = jnp.concatenate([w_cq, w_ckv, _rope_slab(w_kr), _rope_slab(_rot_cols(w_kr))],
                         axis=1).astype(BF16)
    main, vt, aux = _inproj(x, g_mix, wm, vc.T.astype(BF16), wa)
    o_c = _sb_attn(main, 0, wsb, vt, 0, hs)

    dq = NOPE_DIM + ROPE_DIM
    wuq3 = w_uq.reshape(Q_LORA, hm, dq)
    nope, ropew = wuq3[:, :, :NOPE_DIM], wuq3[:, :, NOPE_DIM:]
    zq = jnp.zeros((Q_LORA, hm, LANES - dq), w_uq.dtype)
    wuq = jnp.concatenate([nope, ropew, zq], axis=2).reshape(Q_LORA, hm * LANES).astype(BF16)
    half = ROPE_DIM // 2
    ropr = jnp.concatenate([-ropew[:, :, half:], ropew[:, :, :half]], axis=2)
    wuqr = ropr.reshape(Q_LORA, hm * ROPE_DIM).astype(BF16)
    wkv3 = w_ukv.reshape(KV_LORA, hm, NOPE_DIM + HEAD_DIM)
    wk = jnp.concatenate([wkv3[:, :, :NOPE_DIM],
                          jnp.zeros((KV_LORA, hm, LANES - NOPE_DIM), w_ukv.dtype)], axis=2)
    wk = wk.reshape(KV_LORA, hm * LANES).astype(BF16)
    wv_t = wkv3[:, :, NOPE_DIM:].reshape(KV_LORA, hm * HEAD_DIM).T.astype(BF16)
    freqs = (ROPE_THETA ** (-jnp.arange(half, dtype=F32) / half))
    invf = jnp.tile(freqs, 2 * LANES // ROPE_DIM).reshape(1, LANES)
    groups = LANES // ROPE_DIM
    pos = positions.astype(F32).reshape(B, S // MLA_TM, groups, MLA_TM // groups)
    pos = jnp.repeat(jnp.swapaxes(pos, 2, 3), ROPE_DIM, axis=-1)
    qm, km, vtm = _mla_prep(aux, pos, invf, q_norm.reshape(1, Q_LORA),
                            kv_norm.reshape(1, KV_LORA), wuq, wuqr, wk, wv_t,
                            dq ** -0.5 * LOG2E)
    o_d = _flash("mla", qm, 0, km, 0, vtm, 0, None, hm)

    y = _mlp(x.reshape(B * S, D), o_c.reshape(B * S, wsb), o_d.reshape(B * S, hm * HEAD_DIM),
             w_out.astype(BF16), g_mlp, w_up.astype(BF16), w_down.astype(BF16), g_final,
             final_norm)
    return y.reshape(B, S, D)


def kernel(x, positions, norm_mix, norm_mlp, norm_final, w_in_ab, b_forget, rel_bias, w_out_ab,
           w_in_cd, q_norm, kv_norm, w_uq, w_ukv, w_out_cd, w_up, w_down):
    depth = norm_mix.shape[0]
    for layer in range(depth):
        last = layer == depth - 1
        if layer % 2 == 0:
            e = layer // 2
            x = _even_layer(x, norm_mix[layer], w_in_ab[e], b_forget[e], rel_bias[e], w_out_ab[e],
                            norm_mlp[layer], w_up[layer], w_down[layer], norm_final, last)
        else:
            o = layer // 2
            x = _odd_layer(x, positions, norm_mix[layer], w_in_cd[o], q_norm[o], kv_norm[o],
                           w_uq[o], w_ukv[o], w_out_cd[o], norm_mlp[layer], w_up[layer],
                           w_down[layer], norm_final, last)
    return x
```

```python
import functools
import math

import numpy as np
import jax
import jax.numpy as jnp
from jax import lax
from jax.experimental import pallas as pl
from jax.experimental.pallas import tpu as pltpu

F32 = jnp.float32
BF16 = jnp.bfloat16

EPS = 1e-6
HEAD_DIM = 64
CHUNK = 64
CHUNK_SHIFT = CHUNK.bit_length() - 1
N_LEFT_CHUNKS = 8
REL_CLIP = 256
ROPE_DIM = 32
NOPE_DIM = 64
ROPE_THETA = 10000.0
Q_LORA = 384
KV_LORA = 256

LANES = 128
VT_BLK = LANES
SUB = LANES
FLASH_HP = 8
FLASH_BQ = 512
FLASH_SUB = 256
FIXED_WIDTHS = (4, 2, 1)
SAFE_GAP = 80.0
NEG = -1e30
LOG2E = math.log2(math.e)
SB_ZERO_LOG = -104.0
VMEM_LIMIT = 56 * 1024 * 1024

_NT = (((1,), (1,)), ((), ()))


def _cparams(sem):
    return pltpu.CompilerParams(dimension_semantics=sem, vmem_limit_bytes=VMEM_LIMIT)


def _rms_bf16(x, g):
    ms = jnp.mean(x * x, axis=-1, keepdims=True)
    return (x * lax.rsqrt(ms + EPS) * g).astype(BF16)


def _store_vt(ovt_ref, vt, row0):
    rows, tm = vt.shape
    for c in range(tm // VT_BLK):
        ovt_ref[c, row0:row0 + rows, :] = vt[:, c * VT_BLK:(c + 1) * VT_BLK]


def _inproj_kernel(x_ref, g_ref, wm_ref, wvt_ref, wa_ref, om_ref, ovt_ref, oa_ref):
    h = _rms_bf16(x_ref[...], g_ref[...])
    nm = om_ref.shape[-1]
    for c in range(0, nm, 512):
        om_ref[:, c:c + 512] = jnp.dot(
            h, wm_ref[:, c:c + 512], preferred_element_type=F32).astype(BF16)
    nv = wvt_ref.shape[0]
    for r in range(0, nv, 256):
        vt = lax.dot_general(wvt_ref[r:r + 256, :], h, _NT,
                             preferred_element_type=F32).astype(BF16)
        _store_vt(ovt_ref, vt, r)
    oa_ref[...] = jnp.dot(h, wa_ref[...], preferred_element_type=F32)


def _inproj(x, g, wm, wvt, wa, tm=1024):
    B, S, D = x.shape
    nm, nv, na = wm.shape[1], wvt.shape[0], wa.shape[1]
    return pl.pallas_call(
        _inproj_kernel,
        grid=(B, S // tm),
        in_specs=[
            pl.BlockSpec((None, tm, D), lambda b, i: (b, i, 0)),
            pl.BlockSpec((1, D), lambda b, i: (0, 0)),
            pl.BlockSpec((D, nm), lambda b, i: (0, 0)),
            pl.BlockSpec((nv, D), lambda b, i: (0, 0)),
            pl.BlockSpec((D, na), lambda b, i: (0, 0)),
        ],
        out_specs=[
            pl.BlockSpec((None, tm, nm), lambda b, i: (b, i, 0)),
            pl.BlockSpec((None, tm // VT_BLK, nv, VT_BLK), lambda b, i: (b, i, 0, 0)),
            pl.BlockSpec((None, tm, na), lambda b, i: (b, i, 0)),
        ],
        out_shape=[
            jax.ShapeDtypeStruct((B, S, nm), BF16),
            jax.ShapeDtypeStruct((B, S // VT_BLK, nv, VT_BLK), BF16),
            jax.ShapeDtypeStruct((B, S, na), F32),
        ],
        compiler_params=_cparams(("arbitrary", "arbitrary")),
        name="inproj",
    )(x, g.reshape(1, D), wm, wvt, wa)


def _split3(x):
    hi = x.astype(BF16)
    r = x - hi.astype(F32)
    mid = r.astype(BF16)
    lo = (r - mid.astype(F32)).astype(BF16)
    return hi, mid, lo


AUG_W = 8


def _logcum_kernel(fa_ref, b_ref, pqk_ref, oneq_ref, onek_ref, oq_ref, ok_ref, carry_ref):
    @pl.when(pl.program_id(1) == 0)
    def _():
        carry_ref[...] = jnp.zeros_like(carry_ref)

    z = fa_ref[...] + b_ref[...]
    lf = jnp.minimum(z, 0.0) - jnp.log(1.0 + jnp.exp(-jnp.abs(z)))
    tc = lf.shape[0]
    r = lax.broadcasted_iota(jnp.int32, (tc, tc), 0)
    c = lax.broadcasted_iota(jnp.int32, (tc, tc), 1)
    tri = jnp.where(r >= c, 1.0, 0.0).astype(BF16)
    d = jnp.dot(tri, jnp.concatenate(_split3(lf), axis=1), preferred_element_type=F32)
    w = lf.shape[1]
    cs = carry_ref[...]
    for n in range(3):
        cs = cs + d[:, n * w:(n + 1) * w]
    carry_ref[...] = cs[tc - 1:tc, :]
    placed = jnp.dot(jnp.concatenate(_split3(cs * LOG2E), axis=1),
                     pqk_ref[...], preferred_element_type=F32)
    na = oq_ref.shape[-1]
    oq_ref[...] = (oneq_ref[...] + placed[:, 0:na]).astype(BF16)
    ok_ref[...] = (onek_ref[...] + placed[:, na:2 * na]).astype(BF16)


def _logcum(fa, bias, n_heads, tc=512):
    B, S, W = fa.shape
    na = LANES
    pq = np.zeros((3, W, na), np.float32)
    pk = np.zeros((3, W, na), np.float32)
    oneq = np.zeros((1, na), np.float32)
    onek = np.zeros((1, na), np.float32)
    for h in range(n_heads):
        base = h * AUG_W
        for n in range(3):
            pq[n, h, base + n] = 1.0
            pk[n, h, base + 3 + n] = -1.0
        oneq[0, base + 3:base + 6] = 1.0
        onek[0, base:base + 3] = 1.0
    const = lambda a: pl.BlockSpec(a.shape, lambda b, i: (0,) * a.ndim)
    pqk = np.concatenate([pq, pk], axis=2).reshape(3 * W, 2 * na)
    args = [jnp.asarray(pqk, BF16), jnp.asarray(oneq), jnp.asarray(onek)]
    return pl.pallas_call(
        _logcum_kernel,
        grid=(B, S // tc),
        in_specs=[pl.BlockSpec((None, tc, W), lambda b, i: (b, i, 0)),
                  pl.BlockSpec((1, W), lambda b, i: (0, 0))] + [const(a) for a in args],
        out_specs=[pl.BlockSpec((None, tc, na), lambda b, i: (b, i, 0))] * 2,
        out_shape=[jax.ShapeDtypeStruct((B, S, na), BF16)] * 2,
        scratch_shapes=[pltpu.VMEM((1, W), F32)],
        compiler_params=_cparams(("arbitrary", "arbitrary")),
        name="logcum",
    )(fa, bias, *args)


def _pair_mask_q(q2, j):
    lane = lax.broadcasted_iota(jnp.int32, q2.shape, 1)
    keep = (lane >= HEAD_DIM * j) & (lane < HEAD_DIM * (j + 1))
    return jnp.where(keep, q2, jnp.zeros_like(q2))


ONES_ROWS = 16


def _softmax_step(tiles, vts, carry, tile_max=None):
    m, acc = carry
    if tile_max is not None:
        m_new = jnp.maximum(m, tile_max)
    else:
        m_new = m
        for tile in tiles:
            m_new = jnp.maximum(m_new, jnp.max(tile(), axis=0, keepdims=True))
    alpha = jnp.exp2(m - m_new)
    pv = None
    for tile, vt in zip(tiles, vts):
        p = jnp.exp2(tile() - m_new).astype(BF16)
        vt1 = jnp.concatenate([vt, jnp.ones((ONES_ROWS, vt.shape[1]), BF16)], axis=0)
        d = jnp.dot(vt1, p, preferred_element_type=F32)
        pv = d if pv is None else pv + d
    return m_new, alpha * acc + pv


def _softmax_init(bq):
    return (jnp.full((1, bq), NEG, F32), jnp.zeros((HEAD_DIM + ONES_ROWS, bq), F32))


def _softmax_out(carry):
    _, acc = carry
    return acc[0:HEAD_DIM] / acc[HEAD_DIM:HEAD_DIM + 1]


def _store_heads(o_ref, outs):
    oT = jnp.concatenate(outs, axis=0)
    o_ref[...] = oT.T.astype(o_ref.dtype)


def _flash_kernel(*refs, mode, hp, bq):
    if mode == "fox":
        q_ref, k_ref, vt_ref, qaug_ref, kaug_ref, o_ref = refs[:6]
    else:
        q_ref, k_ref, vt_ref, o_ref = refs[:4]
    sa_ref, sb_ref, ma_ref, mb_ref, qt_ref, kn_ref = refs[-6:]
    qs = pl.program_id(2) * bq
    sub = FLASH_SUB
    row = lax.broadcasted_iota(jnp.int32, (sub, bq), 0)
    col = lax.broadcasted_iota(jnp.int32, (sub, bq), 1)
    if mode == "fox":
        kcols = [slice(LANES * (h // 2), LANES * (h // 2 + 1)) for h in range(hp)]
        lane = lax.broadcasted_iota(jnp.int32, (bq, LANES), 1)
        qa = qaug_ref[...]
        qms = []
        for h in range(hp):
            first = AUG_W * (pl.program_id(1) * hp + h)
            own = (lane >= first) & (lane < first + AUG_W)
            qms.append(jnp.concatenate(
                [_pair_mask_q(q_ref[:, kcols[h]], h % 2),
                 jnp.where(own, qa, jnp.zeros_like(qa))], axis=1))
    else:
        kcols = [slice(LANES * h, LANES * (h + 1)) for h in range(hp)]
        qms = [q_ref[:, kcols[h]] for h in range(hp)]
    for h in range(hp):
        qt_ref[h] = qms[h].T

    @pl.when(pl.program_id(2) == 0)
    def _():
        klane = lax.broadcasted_iota(jnp.int32, (1, LANES), 1)
        for h in range(hp):
            kabs = jnp.max(jnp.abs(k_ref[:, kcols[h]].astype(F32)), axis=0, keepdims=True)
            sq = kabs * kabs
            if mode == "fox":
                sq = jnp.where((klane >= HEAD_DIM * (h % 2)) & (klane < HEAD_DIM * (h % 2 + 1)),
                               sq, 0.0)
            kn_ref[h] = jnp.broadcast_to(jnp.sqrt(jnp.sum(sq, axis=1, keepdims=True)), (1, bq))

    def scores(sb, h, masked, col0=0):
        ks = pl.multiple_of(sb * sub, sub)
        k = k_ref[pl.ds(ks, sub), kcols[h]]
        if mode == "fox":
            k = jnp.concatenate([k, kaug_ref[pl.ds(ks, sub), :]], axis=1)
        sT = jnp.dot(k, qt_ref[h, :, col0:], preferred_element_type=F32)
        if masked and mode == "fox":
            sT = jnp.where((ks + row <= qs + col)[:, col0:], sT, NEG)
        elif masked:
            sT = jnp.where((((ks + row) >> CHUNK_SHIFT) <= ((qs + col) >> CHUNK_SHIFT))[:, col0:],
                           sT, NEG)
        return sT

    nsub = bq // sub
    nvt = sub // VT_BLK

    def produce(buf, sb0, masked, h):
        s_buf, m_buf = buf
        tile_max = None
        for c in range(nsub):
            col0 = c * sub if masked else 0
            sT = scores(sb0 + c, h, masked, col0)
            cm = jnp.max(sT, axis=0, keepdims=True)
            if col0:
                s_buf[h, c, :, 0:col0] = jnp.full((sub, col0), NEG, F32)
                cm = jnp.concatenate([jnp.full((1, col0), NEG, F32), cm], axis=1)
            s_buf[h, c, :, col0:] = sT
            tile_max = cm if tile_max is None else jnp.maximum(tile_max, cm)
        m_buf[h] = tile_max

    def consume(buf, sb0, carry, h):
        s_buf, m_buf = buf
        tiles = [lambda c=c: s_buf[h, c] for c in range(nsub)]
        vts = [jnp.concatenate([vt_ref[(sb0 + c) * nvt + v, HEAD_DIM * h:HEAD_DIM * (h + 1), :]
                                for v in range(nvt)], axis=1) for c in range(nsub)]
        return _softmax_step(tiles, vts, carry, tile_max=m_buf[h])

    def stage(cur, cur_sb, nxt, nxt_sb, carries):
        if nxt is not None:
            for h in range(hp):
                produce(nxt, nxt_sb, False, h)
        return tuple(consume(cur, cur_sb, carries[h], h) for h in range(hp))

    n = pl.program_id(2)
    diag_sb = qs // sub
    buf_a, buf_b = (sa_ref, ma_ref), (sb_ref, mb_ref)
    for h in range(hp):
        produce(buf_a, diag_sb, True, h)

    def pair(j, carries):
        carries = stage(buf_a, jnp.where(j == 0, diag_sb, (2 * j - 1) * nsub),
                        buf_b, 2 * j * nsub, carries)
        return stage(buf_b, 2 * j * nsub,
                     buf_a, jnp.minimum(2 * j + 1, n - 1) * nsub, carries)

    def online(_):
        carries = tuple(_softmax_init(bq) for _ in range(hp))
        carries = lax.fori_loop(0, (n + 1) // 2, pair, carries)
        carries = lax.cond(
            n % 2 == 0,
            lambda c: stage(buf_a, jnp.where(n == 0, diag_sb, (n - 1) * nsub), None, None, c),
            lambda c: c, carries)
        return jnp.concatenate([_softmax_out(c) for c in carries], axis=0)

    refs_ = []
    gap = None
    for h in range(hp):
        qf = qt_ref[h, 0:LANES, :].astype(F32)
        bound = jnp.sqrt(jnp.sum(qf * qf, axis=0, keepdims=True)) * kn_ref[h] * 1.01 + 1e-3
        refs_.append(bound)
        g = jnp.max(bound - ma_ref[h])
        gap = g if gap is None else jnp.maximum(gap, g)

    def weigh_add(acc, h, s, sb0, nblk, col0=0):
        p = jnp.exp2(s - refs_[h][:, col0:]).astype(BF16)
        vt = jnp.concatenate([vt_ref[sb0 * nvt + v, HEAD_DIM * h:HEAD_DIM * (h + 1), :]
                              for v in range(nblk * nvt)], axis=1)
        vt1 = jnp.concatenate([vt, jnp.ones((ONES_ROWS, nblk * sub), BF16)], axis=0)
        d = jnp.dot(vt1, p, preferred_element_type=F32)
        if col0:
            d = jnp.concatenate([jnp.zeros((d.shape[0], col0), F32), d], axis=1)
        return acc + d

    def fixed_reference(_):
        def run(kb, accs, nblk):
            ks = pl.multiple_of(kb * sub, sub)
            ss = []
            for h in range(hp):
                k = k_ref[pl.ds(ks, nblk * sub), kcols[h]]
                if mode == "fox":
                    k = jnp.concatenate([k, kaug_ref[pl.ds(ks, nblk * sub), :]], axis=1)
                ss.append(jnp.dot(k, qt_ref[h], preferred_element_type=F32))
            return tuple(weigh_add(accs[h], h, ss[h], kb, nblk) for h in range(hp))

        accs = tuple(jnp.zeros((HEAD_DIM + ONES_ROWS, bq), F32) for _ in range(hp))
        done = 0
        for width in FIXED_WIDTHS:
            trips = (diag_sb - done) // width
            accs = lax.fori_loop(0, trips,
                                 lambda j, a, done=done, width=width: run(done + j * width, a, width),
                                 accs)
            done = done + trips * width
        for c in range(nsub):
            accs = [weigh_add(accs[h], h, sa_ref[h, c, :, c * sub:], diag_sb + c, 1, c * sub)
                    for h in range(hp)]
        return jnp.concatenate([a[0:HEAD_DIM] / a[HEAD_DIM:HEAD_DIM + 1] for a in accs], axis=0)

    oT = lax.cond(gap <= SAFE_GAP, fixed_reference, online, None)
    o_ref[...] = oT.T.astype(o_ref.dtype)


def _flash(mode, q_arr, q_col0, k_arr, k_col0, vt_arr, vt_row0, extra, n_heads,
           hp=FLASH_HP, bq=FLASH_BQ):
    B, S, _ = q_arr.shape
    qw = (HEAD_DIM if mode == "fox" else LANES) * hp
    vw = HEAD_DIM * hp
    in_specs = [
        pl.BlockSpec((None, bq, qw), lambda b, g, i: (b, i, q_col0 // qw + g)),
        pl.BlockSpec((None, S, qw), lambda b, g, i: (b, 0, k_col0 // qw + g)),
        pl.BlockSpec((None, S // VT_BLK, vw, VT_BLK),
                     lambda b, g, i: (b, 0, vt_row0 // vw + g, 0)),
    ]
    args = [q_arr, k_arr, vt_arr]
    if mode == "fox":
        q_aug, k_aug = extra
        in_specs += [
            pl.BlockSpec((None, bq, LANES), lambda b, g, i: (b, i, 0)),
            pl.BlockSpec((None, S, LANES), lambda b, g, i: (b, 0, 0)),
        ]
        args += [q_aug, k_aug]
    return pl.pallas_call(
        functools.partial(_flash_kernel, mode=mode, hp=hp, bq=bq),
        grid=(B, n_heads // hp, S // bq),
        in_specs=in_specs,
        out_specs=pl.BlockSpec((None, bq, vw), lambda b, g, i: (b, i, g)),
        out_shape=jax.ShapeDtypeStruct((B, S, n_heads * HEAD_DIM), BF16),
        scratch_shapes=([pltpu.VMEM((hp, bq // FLASH_SUB, FLASH_SUB, bq), F32)] * 2
                        + [pltpu.VMEM((hp, 1, bq), F32)] * 2
                        + [pltpu.VMEM((hp, 2 * LANES if mode == "fox" else LANES, bq), BF16),
                           pltpu.VMEM((hp, 1, bq), F32)]),
        compiler_params=_cparams(("arbitrary", "arbitrary", "arbitrary")),
        name="flash_" + mode,
    )(*args)


CK_B = 2 * CHUNK
CK_NW = N_LEFT_CHUNKS * CHUNK // CK_B + 1
CK_EXT = (CK_NW + 1) * CK_B


def _chunk_kernel(q_ref, k_ref, vt_ref, ext_ref, o_ref, tab_ref, s_ref, *, hp, nq):
    i = pl.program_id(1)

    @pl.when(i == 0)
    def _():
        jj = lax.broadcasted_iota(jnp.int32, (CK_B, CK_B), 0)
        rr = lax.broadcasted_iota(jnp.int32, (CK_B, CK_B), 1)
        for h in range(hp):
            for w in range(CK_NW):
                a = (CK_NW - 1 - w) * CK_B
                g = jnp.broadcast_to(ext_ref[h:h + 1, a:a + 2 * CK_B], (CK_B, 2 * CK_B))
                t = pltpu.roll(g, CK_B, 1, stride=1, stride_axis=0)[:, :CK_B]
                if w == 0:
                    t = jnp.where((rr >= CHUNK) & (jj < CHUNK), NEG, t)
                if w == CK_NW - 1:
                    t = jnp.where((rr < CHUNK) & (jj >= CHUNK), NEG, t)
                tab_ref[h, w * CK_B:(w + 1) * CK_B, :] = t

    kcols = [slice(LANES * (h // 2), LANES * (h // 2 + 1)) for h in range(hp)]
    firsts = [i * nq + u - (CK_NW - 1) for u in range(nq)]

    def finish():
        for u in range(nq):
            kbc = [jnp.maximum(firsts[u] + w, 0) for w in range(CK_NW)]
            outs = []
            for h in range(hp):
                vt = jnp.concatenate([vt_ref[kbc[w], HEAD_DIM * h:HEAD_DIM * (h + 1), :]
                                      for w in range(CK_NW)], axis=1)
                outs.append(_softmax_out(_softmax_step([lambda u=u, h=h: s_ref[u, h]], [vt],
                                                       _softmax_init(CK_B))))
            oT = jnp.concatenate(outs, axis=0)
            o_ref[u * CK_B:(u + 1) * CK_B, :] = oT.T.astype(o_ref.dtype)

    def pair_scores(u, p, ks, nrows):
        q2 = q_ref[u * CK_B:(u + 1) * CK_B, kcols[2 * p]]
        qq = jnp.concatenate([_pair_mask_q(q2, 0), _pair_mask_q(q2, 1)], axis=0)
        return lax.dot_general(k_ref[pl.ds(ks, nrows), kcols[2 * p]], qq, _NT,
                               preferred_element_type=F32)

    @pl.when(firsts[0] >= 0)
    def _():
        for u in range(nq):
            ks = pl.multiple_of(firsts[u] * CK_B, CK_B)
            for p in range(hp // 2):
                sT = pair_scores(u, p, ks, CK_NW * CK_B)
                for j in range(2):
                    s_ref[u, 2 * p + j] = sT[:, j * CK_B:(j + 1) * CK_B] + tab_ref[2 * p + j]
        finish()

    @pl.when(firsts[0] < 0)
    def _():
        for u in range(nq):
            for p in range(hp // 2):
                for w in range(CK_NW):
                    rows = slice(w * CK_B, (w + 1) * CK_B)
                    ks = pl.multiple_of(jnp.maximum(firsts[u] + w, 0) * CK_B, CK_B)
                    sT = pair_scores(u, p, ks, CK_B)
                    for j in range(2):
                        s_ref[u, 2 * p + j, rows, :] = jnp.where(
                            firsts[u] + w >= 0,
                            sT[:, j * CK_B:(j + 1) * CK_B] + tab_ref[2 * p + j, rows, :], NEG)
        finish()


def _chunk_attn(main, q_col0, k_col0, vt_arr, vt_row0, ext, n_heads, nq=4):
    B, S, _ = main.shape
    hp = n_heads
    qw, vw = HEAD_DIM * hp, HEAD_DIM * hp
    return pl.pallas_call(
        functools.partial(_chunk_kernel, hp=hp, nq=nq),
        grid=(B, S // (nq * CK_B)),
        in_specs=[
            pl.BlockSpec((None, nq * CK_B, qw), lambda b, i: (b, i, q_col0 // qw)),
            pl.BlockSpec((None, S, qw), lambda b, i: (b, 0, k_col0 // qw)),
            pl.BlockSpec((None, S // VT_BLK, vw, VT_BLK), lambda b, i: (b, 0, vt_row0 // vw, 0)),
            pl.BlockSpec((hp, CK_EXT), lambda b, i: (0, 0)),
        ],
        out_specs=pl.BlockSpec((None, nq * CK_B, vw), lambda b, i: (b, i, 0)),
        out_shape=jax.ShapeDtypeStruct((B, S, n_heads * HEAD_DIM), BF16),
        scratch_shapes=[pltpu.VMEM((hp, CK_NW * CK_B, CK_B), F32),
                        pltpu.VMEM((nq, hp, CK_NW * CK_B, CK_B), F32)],
        compiler_params=_cparams(("arbitrary", "arbitrary")),
        name="chunk_attn",
    )(main, main, vt_arr, ext)


def _sb_kernel(q_ref, k_ref, vt_ref, o_ref, z_ref, lb_ref, sfx_ref, *, hp, bq):
    qs = pl.program_id(1) * bq
    nsub = bq // SUB
    row = lax.broadcasted_iota(jnp.int32, (SUB, bq), 0)
    col = lax.broadcasted_iota(jnp.int32, (SUB, bq), 1)
    ur = lax.broadcasted_iota(jnp.int32, (SUB, 2 * SUB), 0)
    uc = lax.broadcasted_iota(jnp.int32, (SUB, 2 * SUB), 1) & (SUB - 1)
    upper2 = jnp.where(uc > ur, 1.0, 0.0).astype(BF16)
    kcols = [slice(LANES * (h // 2), LANES * (h // 2 + 1)) for h in range(hp)]
    qms = [_pair_mask_q(q_ref[:, kcols[h]], h % 2) for h in range(hp)]

    def step(kb, carries, masked, live=None):
        ks = pl.multiple_of(kb * bq, bq)
        for h in range(hp):
            z_ref[h] = lax.dot_general(k_ref[pl.ds(ks, bq), kcols[h]], qms[h], _NT,
                                       preferred_element_type=F32)
        first_col = [c * SUB if masked else 0 for c in range(nsub)]

        def widen(x, c):
            if first_col[c] == 0:
                return x
            return jnp.concatenate([jnp.zeros((x.shape[0], first_col[c]), x.dtype), x], axis=1)

        totals = []
        for h in range(hp):
            tot = []
            for c in range(nsub):
                rows, cols = slice(c * SUB, (c + 1) * SUB), slice(first_col[c], bq)
                z = z_ref[h, rows, cols]
                l1 = jnp.log(1.0 + jnp.exp2(jnp.abs(z) * (-LOG2E)))
                log_beta = jnp.minimum(z, 0.0) - l1
                log_keep = log_beta - z
                if masked:
                    valid = (ks + c * SUB + row < qs + col)[:, cols]
                    log_keep = jnp.where(valid, log_keep, 0.0)
                lb_ref[h, rows, cols] = log_beta
                hi = log_keep.astype(BF16)
                lo = (log_keep - hi.astype(F32)).astype(BF16)
                sfx = jnp.dot(upper2, jnp.concatenate([hi, lo], axis=0),
                              preferred_element_type=F32)
                sfx_ref[h, rows, cols] = sfx
                tot.append(widen(sfx[0:1, :] + log_keep[0:1, :], c))
            totals.append(tot)
        out = []
        for h in range(hp):
            tail, acc = carries[h]
            if live is not None:
                tail = jnp.where(live, tail, NEG)
            parts = [None] * nsub
            for c in range(nsub - 1, -1, -1):
                rows, cols = slice(c * SUB, (c + 1) * SUB), slice(first_col[c], bq)
                a = jnp.exp(lb_ref[h, rows, cols] + sfx_ref[h, rows, cols] + tail[:, cols])
                if masked:
                    a = jnp.where((ks + c * SUB + row < qs + col)[:, cols], a, 0.0)
                parts[c] = a.astype(BF16)
                tail = tail + totals[h][c]
            vts = [vt_ref[kb * nsub + c, HEAD_DIM * h:HEAD_DIM * (h + 1), :] for c in range(nsub)]
            if masked:
                for c in range(nsub):
                    acc = acc + widen(jnp.dot(vts[c], parts[c], preferred_element_type=F32), c)
            else:
                acc = acc + jnp.dot(jnp.concatenate(vts, axis=1), jnp.concatenate(parts, axis=0),
                                    preferred_element_type=F32)
            out.append((tail, acc))
        return tuple(out)

    n_full = qs // bq
    carries = tuple((jnp.zeros((1, bq), F32), jnp.zeros((HEAD_DIM, bq), F32))
                    for _ in range(hp))
    carries = step(n_full, carries, True)
    carries = step(jnp.maximum(n_full - 1, 0), carries, False, live=n_full >= 1)

    def cond(state):
        kb, carries = state
        tail_max = carries[0][0]
        for h in range(1, hp):
            tail_max = jnp.maximum(tail_max, carries[h][0])
        return (kb >= 0) & (jnp.max(tail_max) > SB_ZERO_LOG)

    def body(state):
        kb, carries = state
        return kb - 1, step(kb, carries, False)

    _, carries = lax.while_loop(cond, body, (n_full - 2, carries))
    _store_heads(o_ref, [acc for (_, acc) in carries])


def _sb_attn(main, q_col0, k_col0, vt_arr, vt_row0, n_heads, bq=256):
    B, S, _ = main.shape
    hp = n_heads
    qw = HEAD_DIM * hp
    return pl.pallas_call(
        functools.partial(_sb_kernel, hp=hp, bq=bq),
        grid=(B, S // bq),
        in_specs=[
            pl.BlockSpec((None, bq, qw), lambda b, i: (b, i, q_col0 // qw)),
            pl.BlockSpec((None, S, qw), lambda b, i: (b, 0, k_col0 // qw)),
            pl.BlockSpec((None, S // VT_BLK, qw, VT_BLK), lambda b, i: (b, 0, vt_row0 // qw, 0)),
        ],
        out_specs=pl.BlockSpec((None, bq, qw), lambda b, i: (b, i, 0)),
        out_shape=jax.ShapeDtypeStruct((B, S, n_heads * HEAD_DIM), BF16),
        scratch_shapes=[pltpu.VMEM((hp, bq, bq), F32)] * 3,
        compiler_params=_cparams(("arbitrary", "arbitrary")),
        name="sb_attn",
    )(main, main, vt_arr)


def _mla_prep_kernel(aux_ref, pos_ref, invf_ref, qn_ref, kvn_ref, wuq_ref, wuqr_ref,
                     wk_ref, wvt_ref, oq_ref, ok_ref, ovt_ref, *, q_scale):
    ang = pos_ref[...] * invf_ref[...]
    cos4, sin4 = jnp.cos(ang), jnp.sin(ang)
    lane = lax.broadcasted_iota(jnp.int32, ang.shape, 1)
    rotary = (lane >= NOPE_DIM) & (lane < NOPE_DIM + ROPE_DIM)
    cos_rows, sin_rows = [], []
    for m in range(LANES // ROPE_DIM):
        shift = (NOPE_DIM - ROPE_DIM * m) % LANES
        cm = cos4 if shift == 0 else pltpu.roll(cos4, shift, 1)
        sm = sin4 if shift == 0 else pltpu.roll(sin4, shift, 1)
        cos_rows.append(jnp.where(rotary, cm, 1.0))
        sin_rows.append(jnp.where(rotary, sm, 0.0))
    cos = jnp.concatenate(cos_rows, axis=0)
    sin = jnp.concatenate(sin_rows, axis=0)
    cq = _rms_bf16(aux_ref[:, 0:Q_LORA], qn_ref[...])
    ckv = _rms_bf16(aux_ref[:, Q_LORA:Q_LORA + KV_LORA], kvn_ref[...])
    o = Q_LORA + KV_LORA
    k_rope = aux_ref[:, o:o + LANES] * cos + aux_ref[:, o + LANES:o + 2 * LANES] * sin
    cos_q, sin_q = cos * q_scale, sin * q_scale
    n_heads = oq_ref.shape[-1] // LANES
    qr = jnp.dot(cq, wuqr_ref[...], preferred_element_type=F32)
    per_group = LANES // ROPE_DIM
    for h in range(0, n_heads, 2):
        cols = slice(h * LANES, (h + 2) * LANES)
        qa = jnp.dot(cq, wuq_ref[:, cols], preferred_element_type=F32)
        kn = jnp.dot(ckv, wk_ref[:, cols], preferred_element_type=F32)
        for d in range(2):
            c1 = slice(d * LANES, (d + 1) * LANES)
            c2 = slice((h + d) * LANES, (h + d + 1) * LANES)
            grp, m = divmod(h + d, per_group)
            qb = qr[:, grp * LANES:(grp + 1) * LANES]
            shift = (NOPE_DIM - ROPE_DIM * m) % LANES
            if shift:
                qb = pltpu.roll(qb, shift, 1)
            oq_ref[:, c2] = (qa[:, c1] * cos_q + qb * sin_q).astype(BF16)
            ok_ref[:, c2] = (kn[:, c1] + k_rope).astype(BF16)
    nv = wvt_ref.shape[0]
    for r in range(0, nv, 256):
        vt = lax.dot_general(wvt_ref[r:r + 256, :], ckv, _NT,
                             preferred_element_type=F32).astype(BF16)
        _store_vt(ovt_ref, vt, r)


MLA_TM = 1024


def _mla_prep(aux, pos, invf, qn, kvn, wuq, wuqr, wk, wvt, q_scale, tm=MLA_TM):
    B, S, na = aux.shape
    nq, nv = wuq.shape[1], wvt.shape[0]
    full = lambda a: pl.BlockSpec(a.shape, lambda b, i: (0,) * a.ndim)
    return pl.pallas_call(
        functools.partial(_mla_prep_kernel, q_scale=q_scale),
        grid=(B, S // tm),
        in_specs=[
            pl.BlockSpec((None, tm, na), lambda b, i: (b, i, 0)),
            pl.BlockSpec((None, None) + pos.shape[2:], lambda b, i: (b, i, 0, 0)),
            full(invf), full(qn), full(kvn), full(wuq), full(wuqr), full(wk), full(wvt),
        ],
        out_specs=[
            pl.BlockSpec((None, tm, nq), lambda b, i: (b, i, 0)),
            pl.BlockSpec((None, tm, nq), lambda b, i: (b, i, 0)),
            pl.BlockSpec((None, tm // VT_BLK, nv, VT_BLK), lambda b, i: (b, i, 0, 0)),
        ],
        out_shape=[
            jax.ShapeDtypeStruct((B, S, nq), BF16),
            jax.ShapeDtypeStruct((B, S, nq), BF16),
            jax.ShapeDtypeStruct((B, S // VT_BLK, nv, VT_BLK), BF16),
        ],
        compiler_params=_cparams(("arbitrary", "arbitrary")),
        name="mla_prep",
    )(aux, pos, invf, qn, kvn, wuq, wuqr, wk, wvt)


def _mlp_kernel(x_ref, oa_ref, ob_ref, wo_ref, g_ref, wu_ref, wd_ref, gf_ref, out_ref,
                x1_ref, h_ref, a_ref, *, final_norm, tf):
    na = oa_ref.shape[-1]
    x1 = (x_ref[...]
          + jnp.dot(oa_ref[...], wo_ref[0:na, :], preferred_element_type=F32)
          + jnp.dot(ob_ref[...], wo_ref[na:, :], preferred_element_type=F32))
    x1_ref[...] = x1
    h_ref[...] = _rms_bf16(x1, g_ref[...])
    for f in range(0, wu_ref.shape[1], tf):
        u = jnp.dot(h_ref[...], wu_ref[:, f:f + tf], preferred_element_type=F32)
        a_ref[:, f:f + tf] = jnp.square(jnp.maximum(u, 0.0)).astype(BF16)
    y = x1_ref[...] + jnp.dot(a_ref[...], wd_ref[...], preferred_element_type=F32)
    if final_norm:
        ms = jnp.mean(y * y, axis=-1, keepdims=True)
        y = y * lax.rsqrt(ms + EPS) * gf_ref[...]
    out_ref[...] = y


def _mlp(x2, oa, ob, wo, g, wu, wd, gf, final_norm, tm=1024, tf=1024):
    T, D = x2.shape
    F = wu.shape[1]
    na, nb = oa.shape[1], ob.shape[1]
    const = lambda shape: pl.BlockSpec(shape, lambda i: (0, 0), pipeline_mode=pl.Buffered(1))
    return pl.pallas_call(
        functools.partial(_mlp_kernel, final_norm=final_norm, tf=tf),
        grid=(T // tm,),
        in_specs=[
            pl.BlockSpec((tm, D), lambda i: (i, 0)),
            pl.BlockSpec((tm, na), lambda i: (i, 0)),
            pl.BlockSpec((tm, nb), lambda i: (i, 0)),
            const((na + nb, D)),
            const((1, D)),
            const((D, F)),
            const((F, D)),
            const((1, D)),
        ],
        out_specs=pl.BlockSpec((tm, D), lambda i: (i, 0)),
        out_shape=jax.ShapeDtypeStruct((T, D), F32),
        scratch_shapes=[pltpu.VMEM((tm, D), F32), pltpu.VMEM((tm, D), BF16),
                        pltpu.VMEM((tm, F), BF16)],
        compiler_params=_cparams(("arbitrary",)),
        name="mlp",
    )(x2, oa, ob, wo, g.reshape(1, D), wu, wd, gf.reshape(1, D))


def _pad_cols(w, n):
    return jnp.pad(w, ((0, 0), (0, n - w.shape[1])))


def _rot_cols(w):
    half = ROPE_DIM // 2
    return jnp.concatenate([-w[:, half:], w[:, :half]], axis=1)


def _rope_slab(w):
    z = jnp.zeros((w.shape[0], NOPE_DIM), w.dtype)
    return jnp.concatenate([z, w, jnp.zeros((w.shape[0], LANES - NOPE_DIM - ROPE_DIM), w.dtype)], axis=1)


def _even_layer(x, g_mix, w_in, b_forget, rel_bias, w_out, g_mlp, w_up, w_down, g_final,
                final_norm):
    B, S, D = x.shape
    hf, hc = b_forget.shape[0], rel_bias.shape[0]
    wf, wc = hf * HEAD_DIM, hc * HEAD_DIM
    o = np.cumsum([0, wf, wf, wf, hf, wc, wc, wc])
    qa, ka, va, fa, qb, kb, vb = [w_in[:, o[n]:o[n + 1]] for n in range(7)]
    q_scale = HEAD_DIM ** -0.5 * LOG2E
    wm = jnp.concatenate([qa * q_scale, ka, qb * q_scale, kb], axis=1).astype(BF16)
    wvt = jnp.concatenate([va, vb], axis=1).T.astype(BF16)
    wa = _pad_cols(fa, LANES).astype(BF16)
    main, vt, aux = _inproj(x, g_mix, wm, wvt, wa)

    q_aug, k_aug = _logcum(aux, _pad_cols(b_forget.reshape(1, hf), LANES), hf)
    o_a = _flash("fox", main, 0, main, wf, vt, 0, (q_aug, k_aug), hf)

    assert rel_bias.shape[1] == CHUNK + REL_CLIP
    right = CK_EXT - rel_bias.shape[1] - (CHUNK + 1)
    ext = jnp.pad(rel_bias * LOG2E, ((0, 0), (CHUNK + 1, right)), mode="edge")
    o_b = _chunk_attn(main, 2 * wf, 2 * wf + wc, vt, wf, ext, hc)

    y = _mlp(x.reshape(B * S, D), o_a.reshape(B * S, wf), o_b.reshape(B * S, wc),
             w_out.astype(BF16), g_mlp, w_up.astype(BF16), w_down.astype(BF16), g_final,
             final_norm)
    return y.reshape(B, S, D)


def _odd_layer(x, positions, g_mix, w_in, q_norm, kv_norm, w_uq, w_ukv, w_out, g_mlp, w_up,
               w_down, g_final, final_norm):
    B, S, D = x.shape
    hm = w_ukv.shape[1] // (NOPE_DIM + HEAD_DIM)
    ws = w_in.shape[1] - Q_LORA - KV_LORA - ROPE_DIM
    hs = (ws // 3) // HEAD_DIM
    wsb = hs * HEAD_DIM
    o = np.cumsum([0, wsb, wsb, wsb, Q_LORA, KV_LORA, ROPE_DIM])
    qc, kc, vc, w_cq, w_ckv, w_kr = [w_in[:, o[n]:o[n + 1]] for n in range(6)]
    wm = jnp.concatenate([qc * HEAD_DIM ** -0.5, kc], axis=1).astype(BF16)
    wa = jnp.concatenate([w_cq, w_ckv, _rope_slab(w_kr), _rope_slab(_rot_cols(w_kr))],
                         axis=1).astype(BF16)
    main, vt, aux = _inproj(x, g_mix, wm, vc.T.astype(BF16), wa)
    o_c = _sb_attn(main, 0, wsb, vt, 0, hs)

    dq = NOPE_DIM + ROPE_DIM
    wuq3 = w_uq.reshape(Q_LORA, hm, dq)
    nope, ropew = wuq3[:, :, :NOPE_DIM], wuq3[:, :, NOPE_DIM:]
    zq = jnp.zeros((Q_LORA, hm, LANES - dq), w_uq.dtype)
    wuq = jnp.concatenate([nope, ropew, zq], axis=2).reshape(Q_LORA, hm * LANES).astype(BF16)
    half = ROPE_DIM // 2
    ropr = jnp.concatenate([-ropew[:, :, half:], ropew[:, :, :half]], axis=2)
    wuqr = ropr.reshape(Q_LORA, hm * ROPE_DIM).astype(BF16)
    wkv3 = w_ukv.reshape(KV_LORA, hm, NOPE_DIM + HEAD_DIM)
    wk = jnp.concatenate([wkv3[:, :, :NOPE_DIM],
                          jnp.zeros((KV_LORA, hm, LANES - NOPE_DIM), w_ukv.dtype)], axis=2)
    wk = wk.reshape(KV_LORA, hm * LANES).astype(BF16)
    wv_t = wkv3[:, :, NOPE_DIM:].reshape(KV_LORA, hm * HEAD_DIM).T.astype(BF16)
    freqs = (ROPE_THETA ** (-jnp.arange(half, dtype=F32) / half))
    invf = jnp.tile(freqs, 2 * LANES // ROPE_DIM).reshape(1, LANES)
    groups = LANES // ROPE_DIM
    pos = positions.astype(F32).reshape(B, S // MLA_TM, groups, MLA_TM // groups)
    pos = jnp.repeat(jnp.swapaxes(pos, 2, 3), ROPE_DIM, axis=-1)
    qm, km, vtm = _mla_prep(aux, pos, invf, q_norm.reshape(1, Q_LORA),
                            kv_norm.reshape(1, KV_LORA), wuq, wuqr, wk, wv_t,
                            dq ** -0.5 * LOG2E)
    o_d = _flash("mla", qm, 0, km, 0, vtm, 0, None, hm)

    y = _mlp(x.reshape(B * S, D), o_c.reshape(B * S, wsb), o_d.reshape(B * S, hm * HEAD_DIM),
             w_out.astype(BF16), g_mlp, w_up.astype(BF16), w_down.astype(BF16), g_final,
             final_norm)
    return y.reshape(B, S, D)


def kernel(x, positions, norm_mix, norm_mlp, norm_final, w_in_ab, b_forget, rel_bias, w_out_ab,
           w_in_cd, q_norm, kv_norm, w_uq, w_ukv, w_out_cd, w_up, w_down):
    depth = norm_mix.shape[0]
    for layer in range(depth):
        last = layer == depth - 1
        if layer % 2 == 0:
            e = layer // 2
            x = _even_layer(x, norm_mix[layer], w_in_ab[e], b_forget[e], rel_bias[e], w_out_ab[e],
                            norm_mlp[layer], w_up[layer], w_down[layer], norm_final, last)
        else:
            o = layer // 2
            x = _odd_layer(x, positions, norm_mix[layer], w_in_cd[o], q_norm[o], kv_norm[o],
                           w_uq[o], w_ukv[o], w_out_cd[o], norm_mlp[layer], w_up[layer],
                           w_down[layer], norm_final, last)
    return x
```

```python
import functools
import math

import numpy as np
import jax
import jax.numpy as jnp
from jax import lax
from jax.experimental import pallas as pl
from jax.experimental.pallas import tpu as pltpu

F32 = jnp.float32
BF16 = jnp.bfloat16

EPS = 1e-6
HEAD_DIM = 64
CHUNK = 64
CHUNK_SHIFT = CHUNK.bit_length() - 1
N_LEFT_CHUNKS = 8
REL_CLIP = 256
ROPE_DIM = 32
NOPE_DIM = 64
ROPE_THETA = 10000.0
Q_LORA = 384
KV_LORA = 256

LANES = 128
VT_BLK = LANES
SUB = LANES
FLASH_HP = 8
FLASH_BQ = 512
FLASH_SUB = 256
QHALF = 256
FIXED_WIDTHS = (4, 2, 1)
SAFE_GAP = 80.0
NEG = -1e30
LOG2E = math.log2(math.e)
SB_ZERO_LOG = -104.0
VMEM_LIMIT = 56 * 1024 * 1024

_NT = (((1,), (1,)), ((), ()))


def _cparams(sem):
    return pltpu.CompilerParams(dimension_semantics=sem, vmem_limit_bytes=VMEM_LIMIT)


def _rms_bf16(x, g):
    ms = jnp.mean(x * x, axis=-1, keepdims=True)
    return (x * lax.rsqrt(ms + EPS) * g).astype(BF16)


def _store_vt(ovt_ref, vt, row0):
    rows, tm = vt.shape
    for c in range(tm // VT_BLK):
        ovt_ref[c, row0:row0 + rows, :] = vt[:, c * VT_BLK:(c + 1) * VT_BLK]


def _inproj_kernel(x_ref, g_ref, wm_ref, wvt_ref, wa_ref, om_ref, ovt_ref, oa_ref):
    h = _rms_bf16(x_ref[...], g_ref[...])
    nm = om_ref.shape[-1]
    for c in range(0, nm, 512):
        om_ref[:, c:c + 512] = jnp.dot(
            h, wm_ref[:, c:c + 512], preferred_element_type=F32).astype(BF16)
    nv = wvt_ref.shape[0]
    for r in range(0, nv, 256):
        vt = lax.dot_general(wvt_ref[r:r + 256, :], h, _NT,
                             preferred_element_type=F32).astype(BF16)
        _store_vt(ovt_ref, vt, r)
    oa_ref[...] = jnp.dot(h, wa_ref[...], preferred_element_type=F32)


def _inproj(x, g, wm, wvt, wa, tm=1024):
    B, S, D = x.shape
    nm, nv, na = wm.shape[1], wvt.shape[0], wa.shape[1]
    return pl.pallas_call(
        _inproj_kernel,
        grid=(B, S // tm),
        in_specs=[
            pl.BlockSpec((None, tm, D), lambda b, i: (b, i, 0)),
            pl.BlockSpec((1, D), lambda b, i: (0, 0)),
            pl.BlockSpec((D, nm), lambda b, i: (0, 0)),
            pl.BlockSpec((nv, D), lambda b, i: (0, 0)),
            pl.BlockSpec((D, na), lambda b, i: (0, 0)),
        ],
        out_specs=[
            pl.BlockSpec((None, tm, nm), lambda b, i: (b, i, 0)),
            pl.BlockSpec((None, tm // VT_BLK, nv, VT_BLK), lambda b, i: (b, i, 0, 0)),
            pl.BlockSpec((None, tm, na), lambda b, i: (b, i, 0)),
        ],
        out_shape=[
            jax.ShapeDtypeStruct((B, S, nm), BF16),
            jax.ShapeDtypeStruct((B, S // VT_BLK, nv, VT_BLK), BF16),
            jax.ShapeDtypeStruct((B, S, na), F32),
        ],
        compiler_params=_cparams(("arbitrary", "arbitrary")),
        name="inproj",
    )(x, g.reshape(1, D), wm, wvt, wa)


def _split3(x):
    hi = x.astype(BF16)
    r = x - hi.astype(F32)
    mid = r.astype(BF16)
    lo = (r - mid.astype(F32)).astype(BF16)
    return hi, mid, lo


AUG_W = 8


def _logcum_kernel(fa_ref, b_ref, pqk_ref, oneq_ref, onek_ref, oq_ref, ok_ref, carry_ref):
    @pl.when(pl.program_id(1) == 0)
    def _():
        carry_ref[...] = jnp.zeros_like(carry_ref)

    z = fa_ref[...] + b_ref[...]
    lf = jnp.minimum(z, 0.0) - jnp.log(1.0 + jnp.exp(-jnp.abs(z)))
    tc = lf.shape[0]
    r = lax.broadcasted_iota(jnp.int32, (tc, tc), 0)
    c = lax.broadcasted_iota(jnp.int32, (tc, tc), 1)
    tri = jnp.where(r >= c, 1.0, 0.0).astype(BF16)
    d = jnp.dot(tri, jnp.concatenate(_split3(lf), axis=1), preferred_element_type=F32)
    w = lf.shape[1]
    cs = carry_ref[...]
    for n in range(3):
        cs = cs + d[:, n * w:(n + 1) * w]
    carry_ref[...] = cs[tc - 1:tc, :]
    placed = jnp.dot(jnp.concatenate(_split3(cs * LOG2E), axis=1),
                     pqk_ref[...], preferred_element_type=F32)
    na = oq_ref.shape[-1]
    oq_ref[...] = (oneq_ref[...] + placed[:, 0:na]).astype(BF16)
    ok_ref[...] = (onek_ref[...] + placed[:, na:2 * na]).astype(BF16)


def _logcum(fa, bias, n_heads, tc=512):
    B, S, W = fa.shape
    na = LANES
    pq = np.zeros((3, W, na), np.float32)
    pk = np.zeros((3, W, na), np.float32)
    oneq = np.zeros((1, na), np.float32)
    onek = np.zeros((1, na), np.float32)
    for h in range(n_heads):
        base = h * AUG_W
        for n in range(3):
            pq[n, h, base + n] = 1.0
            pk[n, h, base + 3 + n] = -1.0
        oneq[0, base + 3:base + 6] = 1.0
        onek[0, base:base + 3] = 1.0
    const = lambda a: pl.BlockSpec(a.shape, lambda b, i: (0,) * a.ndim)
    pqk = np.concatenate([pq, pk], axis=2).reshape(3 * W, 2 * na)
    args = [jnp.asarray(pqk, BF16), jnp.asarray(oneq), jnp.asarray(onek)]
    return pl.pallas_call(
        _logcum_kernel,
        grid=(B, S // tc),
        in_specs=[pl.BlockSpec((None, tc, W), lambda b, i: (b, i, 0)),
                  pl.BlockSpec((1, W), lambda b, i: (0, 0))] + [const(a) for a in args],
        out_specs=[pl.BlockSpec((None, tc, na), lambda b, i: (b, i, 0))] * 2,
        out_shape=[jax.ShapeDtypeStruct((B, S, na), BF16)] * 2,
        scratch_shapes=[pltpu.VMEM((1, W), F32)],
        compiler_params=_cparams(("arbitrary", "arbitrary")),
        name="logcum",
    )(fa, bias, *args)


def _pair_mask_q(q2, j):
    lane = lax.broadcasted_iota(jnp.int32, q2.shape, 1)
    keep = (lane >= HEAD_DIM * j) & (lane < HEAD_DIM * (j + 1))
    return jnp.where(keep, q2, jnp.zeros_like(q2))


ONES_ROWS = 16


def _softmax_step(tiles, vts, carry, tile_max=None):
    m, acc = carry
    if tile_max is not None:
        m_new = jnp.maximum(m, tile_max)
    else:
        m_new = m
        for tile in tiles:
            m_new = jnp.maximum(m_new, jnp.max(tile(), axis=0, keepdims=True))
    alpha = jnp.exp2(m - m_new)
    pv = None
    for tile, vt in zip(tiles, vts):
        p = jnp.exp2(tile() - m_new).astype(BF16)
        vt1 = jnp.concatenate([vt, jnp.ones((ONES_ROWS, vt.shape[1]), BF16)], axis=0)
        d = jnp.dot(vt1, p, preferred_element_type=F32)
        pv = d if pv is None else pv + d
    return m_new, alpha * acc + pv


def _softmax_init(bq):
    return (jnp.full((1, bq), NEG, F32), jnp.zeros((HEAD_DIM + ONES_ROWS, bq), F32))


def _softmax_out(carry):
    _, acc = carry
    return acc[0:HEAD_DIM] / acc[HEAD_DIM:HEAD_DIM + 1]


def _store_heads(o_ref, outs):
    oT = jnp.concatenate(outs, axis=0)
    o_ref[...] = oT.T.astype(o_ref.dtype)


def _flash_kernel(*refs, mode, hp, bq):
    if mode == "fox":
        q_ref, k_ref, vt_ref, qaug_ref, kaug_ref, o_ref = refs[:6]
    else:
        q_ref, k_ref, vt_ref, o_ref = refs[:4]
    sa_ref, sb_ref, ma_ref, mb_ref, qt_ref, kn_ref = refs[-6:]
    qs = pl.program_id(2) * bq
    sub = FLASH_SUB
    row = lax.broadcasted_iota(jnp.int32, (sub, bq), 0)
    col = lax.broadcasted_iota(jnp.int32, (sub, bq), 1)
    if mode == "fox":
        kcols = [slice(LANES * (h // 2), LANES * (h // 2 + 1)) for h in range(hp)]
        lane = lax.broadcasted_iota(jnp.int32, (bq, LANES), 1)
        qa = qaug_ref[...]
        qms = []
        for h in range(hp):
            first = AUG_W * (pl.program_id(1) * hp + h)
            own = (lane >= first) & (lane < first + AUG_W)
            qms.append(jnp.concatenate(
                [_pair_mask_q(q_ref[:, kcols[h]], h % 2),
                 jnp.where(own, qa, jnp.zeros_like(qa))], axis=1))
    else:
        kcols = [slice(LANES * h, LANES * (h + 1)) for h in range(hp)]
        qms = [q_ref[:, kcols[h]] for h in range(hp)]
    for h in range(hp):
        qt_ref[h] = qms[h].T

    @pl.when(pl.program_id(2) == 0)
    def _():
        klane = lax.broadcasted_iota(jnp.int32, (1, LANES), 1)
        for h in range(hp):
            kabs = jnp.max(jnp.abs(k_ref[:, kcols[h]].astype(F32)), axis=0, keepdims=True)
            sq = kabs * kabs
            if mode == "fox":
                sq = jnp.where((klane >= HEAD_DIM * (h % 2)) & (klane < HEAD_DIM * (h % 2 + 1)),
                               sq, 0.0)
            kn_ref[h] = jnp.broadcast_to(jnp.sqrt(jnp.sum(sq, axis=1, keepdims=True)), (1, bq))

    def scores(sb, h, masked, col0=0):
        ks = pl.multiple_of(sb * sub, sub)
        k = k_ref[pl.ds(ks, sub), kcols[h]]
        if mode == "fox":
            k = jnp.concatenate([k, kaug_ref[pl.ds(ks, sub), :]], axis=1)
        sT = jnp.dot(k, qt_ref[h, :, col0:], preferred_element_type=F32)
        if masked and mode == "fox":
            sT = jnp.where((ks + row <= qs + col)[:, col0:], sT, NEG)
        elif masked:
            sT = jnp.where((((ks + row) >> CHUNK_SHIFT) <= ((qs + col) >> CHUNK_SHIFT))[:, col0:],
                           sT, NEG)
        return sT

    nsub = bq // sub
    nvt = sub // VT_BLK

    def produce(buf, sb0, masked, h):
        s_buf, m_buf = buf
        tile_max = None
        for c in range(nsub):
            col0 = c * sub if masked else 0
            sT = scores(sb0 + c, h, masked, col0)
            cm = jnp.max(sT, axis=0, keepdims=True)
            if col0:
                s_buf[h, c, :, 0:col0] = jnp.full((sub, col0), NEG, F32)
                cm = jnp.concatenate([jnp.full((1, col0), NEG, F32), cm], axis=1)
            s_buf[h, c, :, col0:] = sT
            tile_max = cm if tile_max is None else jnp.maximum(tile_max, cm)
        m_buf[h] = tile_max

    def consume(buf, sb0, carry, h):
        s_buf, m_buf = buf
        tiles = [lambda c=c: s_buf[h, c] for c in range(nsub)]
        vts = [jnp.concatenate([vt_ref[(sb0 + c) * nvt + v, HEAD_DIM * h:HEAD_DIM * (h + 1), :]
                                for v in range(nvt)], axis=1) for c in range(nsub)]
        return _softmax_step(tiles, vts, carry, tile_max=m_buf[h])

    def stage(cur, cur_sb, nxt, nxt_sb, carries):
        if nxt is not None:
            for h in range(hp):
                produce(nxt, nxt_sb, False, h)
        return tuple(consume(cur, cur_sb, carries[h], h) for h in range(hp))

    n = pl.program_id(2)
    diag_sb = qs // sub
    buf_a, buf_b = (sa_ref, ma_ref), (sb_ref, mb_ref)
    for h in range(hp):
        produce(buf_a, diag_sb, True, h)

    def pair(j, carries):
        carries = stage(buf_a, jnp.where(j == 0, diag_sb, (2 * j - 1) * nsub),
                        buf_b, 2 * j * nsub, carries)
        return stage(buf_b, 2 * j * nsub,
                     buf_a, jnp.minimum(2 * j + 1, n - 1) * nsub, carries)

    def online(_):
        carries = tuple(_softmax_init(bq) for _ in range(hp))
        carries = lax.fori_loop(0, (n + 1) // 2, pair, carries)
        carries = lax.cond(
            n % 2 == 0,
            lambda c: stage(buf_a, jnp.where(n == 0, diag_sb, (n - 1) * nsub), None, None, c),
            lambda c: c, carries)
        return jnp.concatenate([_softmax_out(c) for c in carries], axis=0)

    refs_ = []
    gap = None
    for h in range(hp):
        qf = qt_ref[h, 0:LANES, :].astype(F32)
        bound = jnp.sqrt(jnp.sum(qf * qf, axis=0, keepdims=True)) * kn_ref[h] * 1.01 + 1e-3
        refs_.append(bound)
        g = jnp.max(bound - ma_ref[h])
        gap = g if gap is None else jnp.maximum(gap, g)

    def weigh_add(acc, h, s, sb0, nblk, col0=0):
        p = jnp.exp2(s - refs_[h][:, col0:]).astype(BF16)
        vt = jnp.concatenate([vt_ref[sb0 * nvt + v, HEAD_DIM * h:HEAD_DIM * (h + 1), :]
                              for v in range(nblk * nvt)], axis=1)
        vt1 = jnp.concatenate([vt, jnp.ones((ONES_ROWS, nblk * sub), BF16)], axis=0)
        d = jnp.dot(vt1, p, preferred_element_type=F32)
        if col0:
            d = jnp.concatenate([jnp.zeros((d.shape[0], col0), F32), d], axis=1)
        return acc + d

    def fixed_reference(_):
        def run(kb, accs, nblk):
            ks = pl.multiple_of(kb * sub, sub)
            vt1s = []
            for h in range(hp):
                vt = jnp.concatenate([vt_ref[kb * nvt + v, HEAD_DIM * h:HEAD_DIM * (h + 1), :]
                                      for v in range(nblk * nvt)], axis=1)
                vt1s.append(jnp.concatenate([vt, jnp.ones((ONES_ROWS, nblk * sub), BF16)], axis=0))
            halves = [[] for _ in range(hp)]
            for c0 in range(0, bq, QHALF):
                cols = slice(c0, c0 + QHALF)
                ss = []
                for h in range(hp):
                    k = k_ref[pl.ds(ks, nblk * sub), kcols[h]]
                    if mode == "fox":
                        k = jnp.concatenate([k, kaug_ref[pl.ds(ks, nblk * sub), :]], axis=1)
                    ss.append(jnp.dot(k, qt_ref[h, :, cols], preferred_element_type=F32))
                for h in range(hp):
                    p = jnp.exp2(ss[h] - refs_[h][:, cols]).astype(BF16)
                    halves[h].append(jnp.dot(vt1s[h], p, preferred_element_type=F32))
            return tuple(accs[h] + jnp.concatenate(halves[h], axis=1) for h in range(hp))

        accs = tuple(jnp.zeros((HEAD_DIM + ONES_ROWS, bq), F32) for _ in range(hp))
        done = 0
        for width in FIXED_WIDTHS:
            trips = (diag_sb - done) // width
            accs = lax.fori_loop(0, trips,
                                 lambda j, a, done=done, width=width: run(done + j * width, a, width),
                                 accs)
            done = done + trips * width
        for c in range(nsub):
            accs = [weigh_add(accs[h], h, sa_ref[h, c, :, c * sub:], diag_sb + c, 1, c * sub)
                    for h in range(hp)]
        return jnp.concatenate([a[0:HEAD_DIM] / a[HEAD_DIM:HEAD_DIM + 1] for a in accs], axis=0)

    oT = lax.cond(gap <= SAFE_GAP, fixed_reference, online, None)
    o_ref[...] = oT.T.astype(o_ref.dtype)


def _flash(mode, q_arr, q_col0, k_arr, k_col0, vt_arr, vt_row0, extra, n_heads,
           hp=FLASH_HP, bq=FLASH_BQ):
    B, S, _ = q_arr.shape
    qw = (HEAD_DIM if mode == "fox" else LANES) * hp
    vw = HEAD_DIM * hp
    in_specs = [
        pl.BlockSpec((None, bq, qw), lambda b, g, i: (b, i, q_col0 // qw + g)),
        pl.BlockSpec((None, S, qw), lambda b, g, i: (b, 0, k_col0 // qw + g)),
        pl.BlockSpec((None, S // VT_BLK, vw, VT_BLK),
                     lambda b, g, i: (b, 0, vt_row0 // vw + g, 0)),
    ]
    args = [q_arr, k_arr, vt_arr]
    if mode == "fox":
        q_aug, k_aug = extra
        in_specs += [
            pl.BlockSpec((None, bq, LANES), lambda b, g, i: (b, i, 0)),
            pl.BlockSpec((None, S, LANES), lambda b, g, i: (b, 0, 0)),
        ]
        args += [q_aug, k_aug]
    return pl.pallas_call(
        functools.partial(_flash_kernel, mode=mode, hp=hp, bq=bq),
        grid=(B, n_heads // hp, S // bq),
        in_specs=in_specs,
        out_specs=pl.BlockSpec((None, bq, vw), lambda b, g, i: (b, i, g)),
        out_shape=jax.ShapeDtypeStruct((B, S, n_heads * HEAD_DIM), BF16),
        scratch_shapes=([pltpu.VMEM((hp, bq // FLASH_SUB, FLASH_SUB, bq), F32)] * 2
                        + [pltpu.VMEM((hp, 1, bq), F32)] * 2
                        + [pltpu.VMEM((hp, 2 * LANES if mode == "fox" else LANES, bq), BF16),
                           pltpu.VMEM((hp, 1, bq), F32)]),
        compiler_params=_cparams(("arbitrary", "arbitrary", "arbitrary")),
        name="flash_" + mode,
    )(*args)


CK_B = 2 * CHUNK
CK_NW = N_LEFT_CHUNKS * CHUNK // CK_B + 1
CK_EXT = (CK_NW + 1) * CK_B


def _chunk_kernel(q_ref, k_ref, vt_ref, ext_ref, o_ref, tab_ref, s_ref, *, hp, nq):
    i = pl.program_id(1)

    @pl.when(i == 0)
    def _():
        jj = lax.broadcasted_iota(jnp.int32, (CK_B, CK_B), 0)
        rr = lax.broadcasted_iota(jnp.int32, (CK_B, CK_B), 1)
        for h in range(hp):
            for w in range(CK_NW):
                a = (CK_NW - 1 - w) * CK_B
                g = jnp.broadcast_to(ext_ref[h:h + 1, a:a + 2 * CK_B], (CK_B, 2 * CK_B))
                t = pltpu.roll(g, CK_B, 1, stride=1, stride_axis=0)[:, :CK_B]
                if w == 0:
                    t = jnp.where((rr >= CHUNK) & (jj < CHUNK), NEG, t)
                if w == CK_NW - 1:
                    t = jnp.where((rr < CHUNK) & (jj >= CHUNK), NEG, t)
                tab_ref[h, w * CK_B:(w + 1) * CK_B, :] = t

    kcols = [slice(LANES * (h // 2), LANES * (h // 2 + 1)) for h in range(hp)]
    firsts = [i * nq + u - (CK_NW - 1) for u in range(nq)]

    def finish():
        for u in range(nq):
            kbc = [jnp.maximum(firsts[u] + w, 0) for w in range(CK_NW)]
            outs = []
            for h in range(hp):
                vt = jnp.concatenate([vt_ref[kbc[w], HEAD_DIM * h:HEAD_DIM * (h + 1), :]
                                      for w in range(CK_NW)], axis=1)
                outs.append(_softmax_out(_softmax_step([lambda u=u, h=h: s_ref[u, h]], [vt],
                                                       _softmax_init(CK_B))))
            oT = jnp.concatenate(outs, axis=0)
            o_ref[u * CK_B:(u + 1) * CK_B, :] = oT.T.astype(o_ref.dtype)

    def pair_scores(u, p, ks, nrows):
        q2 = q_ref[u * CK_B:(u + 1) * CK_B, kcols[2 * p]]
        qq = jnp.concatenate([_pair_mask_q(q2, 0), _pair_mask_q(q2, 1)], axis=0)
        return lax.dot_general(k_ref[pl.ds(ks, nrows), kcols[2 * p]], qq, _NT,
                               preferred_element_type=F32)

    @pl.when(firsts[0] >= 0)
    def _():
        for u in range(nq):
            ks = pl.multiple_of(firsts[u] * CK_B, CK_B)
            for p in range(hp // 2):
                sT = pair_scores(u, p, ks, CK_NW * CK_B)
                for j in range(2):
                    s_ref[u, 2 * p + j] = sT[:, j * CK_B:(j + 1) * CK_B] + tab_ref[2 * p + j]
        finish()

    @pl.when(firsts[0] < 0)
    def _():
        for u in range(nq):
            for p in range(hp // 2):
                for w in range(CK_NW):
                    rows = slice(w * CK_B, (w + 1) * CK_B)
                    ks = pl.multiple_of(jnp.maximum(firsts[u] + w, 0) * CK_B, CK_B)
                    sT = pair_scores(u, p, ks, CK_B)
                    for j in range(2):
                        s_ref[u, 2 * p + j, rows, :] = jnp.where(
                            firsts[u] + w >= 0,
                            sT[:, j * CK_B:(j + 1) * CK_B] + tab_ref[2 * p + j, rows, :], NEG)
        finish()


def _chunk_attn(main, q_col0, k_col0, vt_arr, vt_row0, ext, n_heads, nq=4):
    B, S, _ = main.shape
    hp = n_heads
    qw, vw = HEAD_DIM * hp, HEAD_DIM * hp
    return pl.pallas_call(
        functools.partial(_chunk_kernel, hp=hp, nq=nq),
        grid=(B, S // (nq * CK_B)),
        in_specs=[
            pl.BlockSpec((None, nq * CK_B, qw), lambda b, i: (b, i, q_col0 // qw)),
            pl.BlockSpec((None, S, qw), lambda b, i: (b, 0, k_col0 // qw)),
            pl.BlockSpec((None, S // VT_BLK, vw, VT_BLK), lambda b, i: (b, 0, vt_row0 // vw, 0)),
            pl.BlockSpec((hp, CK_EXT), lambda b, i: (0, 0)),
        ],
        out_specs=pl.BlockSpec((None, nq * CK_B, vw), lambda b, i: (b, i, 0)),
        out_shape=jax.ShapeDtypeStruct((B, S, n_heads * HEAD_DIM), BF16),
        scratch_shapes=[pltpu.VMEM((hp, CK_NW * CK_B, CK_B), F32),
                        pltpu.VMEM((nq, hp, CK_NW * CK_B, CK_B), F32)],
        compiler_params=_cparams(("arbitrary", "arbitrary")),
        name="chunk_attn",
    )(main, main, vt_arr, ext)


def _sb_kernel(q_ref, k_ref, vt_ref, o_ref, z_ref, lb_ref, sfx_ref, *, hp, bq):
    qs = pl.program_id(1) * bq
    nsub = bq // SUB
    row = lax.broadcasted_iota(jnp.int32, (SUB, bq), 0)
    col = lax.broadcasted_iota(jnp.int32, (SUB, bq), 1)
    ur = lax.broadcasted_iota(jnp.int32, (SUB, 2 * SUB), 0)
    uc = lax.broadcasted_iota(jnp.int32, (SUB, 2 * SUB), 1) & (SUB - 1)
    upper2 = jnp.where(uc > ur, 1.0, 0.0).astype(BF16)
    kcols = [slice(LANES * (h // 2), LANES * (h // 2 + 1)) for h in range(hp)]
    qms = [_pair_mask_q(q_ref[:, kcols[h]], h % 2) for h in range(hp)]

    def step(kb, carries, masked, live=None):
        ks = pl.multiple_of(kb * bq, bq)
        for h in range(hp):
            z_ref[h] = lax.dot_general(k_ref[pl.ds(ks, bq), kcols[h]], qms[h], _NT,
                                       preferred_element_type=F32)
        first_col = [c * SUB if masked else 0 for c in range(nsub)]

        def widen(x, c):
            if first_col[c] == 0:
                return x
            return jnp.concatenate([jnp.zeros((x.shape[0], first_col[c]), x.dtype), x], axis=1)

        totals = []
        for h in range(hp):
            tot = []
            for c in range(nsub):
                rows, cols = slice(c * SUB, (c + 1) * SUB), slice(first_col[c], bq)
                z = z_ref[h, rows, cols]
                l1 = jnp.log(1.0 + jnp.exp2(jnp.abs(z) * (-LOG2E)))
                log_beta = jnp.minimum(z, 0.0) - l1
                log_keep = log_beta - z
                if masked:
                    valid = (ks + c * SUB + row < qs + col)[:, cols]
                    log_keep = jnp.where(valid, log_keep, 0.0)
                lb_ref[h, rows, cols] = log_beta
                hi = log_keep.astype(BF16)
                lo = (log_keep - hi.astype(F32)).astype(BF16)
                sfx = jnp.dot(upper2, jnp.concatenate([hi, lo], axis=0),
                              preferred_element_type=F32)
                sfx_ref[h, rows, cols] = sfx
                tot.append(widen(sfx[0:1, :] + log_keep[0:1, :], c))
            totals.append(tot)
        out = []
        for h in range(hp):
            tail, acc = carries[h]
            if live is not None:
                tail = jnp.where(live, tail, NEG)
            parts = [None] * nsub
            for c in range(nsub - 1, -1, -1):
                rows, cols = slice(c * SUB, (c + 1) * SUB), slice(first_col[c], bq)
                a = jnp.exp(lb_ref[h, rows, cols] + sfx_ref[h, rows, cols] + tail[:, cols])
                if masked:
                    a = jnp.where((ks + c * SUB + row < qs + col)[:, cols], a, 0.0)
                parts[c] = a.astype(BF16)
                tail = tail + totals[h][c]
            vts = [vt_ref[kb * nsub + c, HEAD_DIM * h:HEAD_DIM * (h + 1), :] for c in range(nsub)]
            if masked:
                for c in range(nsub):
                    acc = acc + widen(jnp.dot(vts[c], parts[c], preferred_element_type=F32), c)
            else:
                acc = acc + jnp.dot(jnp.concatenate(vts, axis=1), jnp.concatenate(parts, axis=0),
                                    preferred_element_type=F32)
            out.append((tail, acc))
        return tuple(out)

    n_full = qs // bq
    carries = tuple((jnp.zeros((1, bq), F32), jnp.zeros((HEAD_DIM, bq), F32))
                    for _ in range(hp))
    carries = step(n_full, carries, True)
    carries = step(jnp.maximum(n_full - 1, 0), carries, False, live=n_full >= 1)

    def cond(state):
        kb, carries = state
        tail_max = carries[0][0]
        for h in range(1, hp):
            tail_max = jnp.maximum(tail_max, carries[h][0])
        return (kb >= 0) & (jnp.max(tail_max) > SB_ZERO_LOG)

    def body(state):
        kb, carries = state
        return kb - 1, step(kb, carries, False)

    _, carries = lax.while_loop(cond, body, (n_full - 2, carries))
    _store_heads(o_ref, [acc for (_, acc) in carries])


def _sb_attn(main, q_col0, k_col0, vt_arr, vt_row0, n_heads, bq=256):
    B, S, _ = main.shape
    hp = n_heads
    qw = HEAD_DIM * hp
    return pl.pallas_call(
        functools.partial(_sb_kernel, hp=hp, bq=bq),
        grid=(B, S // bq),
        in_specs=[
            pl.BlockSpec((None, bq, qw), lambda b, i: (b, i, q_col0 // qw)),
            pl.BlockSpec((None, S, qw), lambda b, i: (b, 0, k_col0 // qw)),
            pl.BlockSpec((None, S // VT_BLK, qw, VT_BLK), lambda b, i: (b, 0, vt_row0 // qw, 0)),
        ],
        out_specs=pl.BlockSpec((None, bq, qw), lambda b, i: (b, i, 0)),
        out_shape=jax.ShapeDtypeStruct((B, S, n_heads * HEAD_DIM), BF16),
        scratch_shapes=[pltpu.VMEM((hp, bq, bq), F32)] * 3,
        compiler_params=_cparams(("arbitrary", "arbitrary")),
        name="sb_attn",
    )(main, main, vt_arr)


def _mla_prep_kernel(aux_ref, pos_ref, invf_ref, qn_ref, kvn_ref, wuq_ref, wuqr_ref,
                     wk_ref, wvt_ref, oq_ref, ok_ref, ovt_ref, *, q_scale):
    ang = pos_ref[...] * invf_ref[...]
    cos4, sin4 = jnp.cos(ang), jnp.sin(ang)
    lane = lax.broadcasted_iota(jnp.int32, ang.shape, 1)
    rotary = (lane >= NOPE_DIM) & (lane < NOPE_DIM + ROPE_DIM)
    cos_rows, sin_rows = [], []
    for m in range(LANES // ROPE_DIM):
        shift = (NOPE_DIM - ROPE_DIM * m) % LANES
        cm = cos4 if shift == 0 else pltpu.roll(cos4, shift, 1)
        sm = sin4 if shift == 0 else pltpu.roll(sin4, shift, 1)
        cos_rows.append(jnp.where(rotary, cm, 1.0))
        sin_rows.append(jnp.where(rotary, sm, 0.0))
    cos = jnp.concatenate(cos_rows, axis=0)
    sin = jnp.concatenate(sin_rows, axis=0)
    cq = _rms_bf16(aux_ref[:, 0:Q_LORA], qn_ref[...])
    ckv = _rms_bf16(aux_ref[:, Q_LORA:Q_LORA + KV_LORA], kvn_ref[...])
    o = Q_LORA + KV_LORA
    k_rope = aux_ref[:, o:o + LANES] * cos + aux_ref[:, o + LANES:o + 2 * LANES] * sin
    cos_q, sin_q = cos * q_scale, sin * q_scale
    n_heads = oq_ref.shape[-1] // LANES
    qr = jnp.dot(cq, wuqr_ref[...], preferred_element_type=F32)
    per_group = LANES // ROPE_DIM
    for h in range(0, n_heads, 2):
        cols = slice(h * LANES, (h + 2) * LANES)
        qa = jnp.dot(cq, wuq_ref[:, cols], preferred_element_type=F32)
        kn = jnp.dot(ckv, wk_ref[:, cols], preferred_element_type=F32)
        for d in range(2):
            c1 = slice(d * LANES, (d + 1) * LANES)
            c2 = slice((h + d) * LANES, (h + d + 1) * LANES)
            grp, m = divmod(h + d, per_group)
            qb = qr[:, grp * LANES:(grp + 1) * LANES]
            shift = (NOPE_DIM - ROPE_DIM * m) % LANES
            if shift:
                qb = pltpu.roll(qb, shift, 1)
            oq_ref[:, c2] = (qa[:, c1] * cos_q + qb * sin_q).astype(BF16)
            ok_ref[:, c2] = (kn[:, c1] + k_rope).astype(BF16)
    nv = wvt_ref.shape[0]
    for r in range(0, nv, 256):
        vt = lax.dot_general(wvt_ref[r:r + 256, :], ckv, _NT,
                             preferred_element_type=F32).astype(BF16)
        _store_vt(ovt_ref, vt, r)


MLA_TM = 1024


def _mla_prep(aux, pos, invf, qn, kvn, wuq, wuqr, wk, wvt, q_scale, tm=MLA_TM):
    B, S, na = aux.shape
    nq, nv = wuq.shape[1], wvt.shape[0]
    full = lambda a: pl.BlockSpec(a.shape, lambda b, i: (0,) * a.ndim)
    return pl.pallas_call(
        functools.partial(_mla_prep_kernel, q_scale=q_scale),
        grid=(B, S // tm),
        in_specs=[
            pl.BlockSpec((None, tm, na), lambda b, i: (b, i, 0)),
            pl.BlockSpec((None, None) + pos.shape[2:], lambda b, i: (b, i, 0, 0)),
            full(invf), full(qn), full(kvn), full(wuq), full(wuqr), full(wk), full(wvt),
        ],
        out_specs=[
            pl.BlockSpec((None, tm, nq), lambda b, i: (b, i, 0)),
            pl.BlockSpec((None, tm, nq), lambda b, i: (b, i, 0)),
            pl.BlockSpec((None, tm // VT_BLK, nv, VT_BLK), lambda b, i: (b, i, 0, 0)),
        ],
        out_shape=[
            jax.ShapeDtypeStruct((B, S, nq), BF16),
            jax.ShapeDtypeStruct((B, S, nq), BF16),
            jax.ShapeDtypeStruct((B, S // VT_BLK, nv, VT_BLK), BF16),
        ],
        compiler_params=_cparams(("arbitrary", "arbitrary")),
        name="mla_prep",
    )(aux, pos, invf, qn, kvn, wuq, wuqr, wk, wvt)


def _mlp_kernel(x_ref, oa_ref, ob_ref, wo_ref, g_ref, wu_ref, wd_ref, gf_ref, out_ref,
                x1_ref, h_ref, a_ref, *, final_norm, tf):
    na = oa_ref.shape[-1]
    x1 = (x_ref[...]
          + jnp.dot(oa_ref[...], wo_ref[0:na, :], preferred_element_type=F32)
          + jnp.dot(ob_ref[...], wo_ref[na:, :], preferred_element_type=F32))
    x1_ref[...] = x1
    h_ref[...] = _rms_bf16(x1, g_ref[...])
    for f in range(0, wu_ref.shape[1], tf):
        u = jnp.dot(h_ref[...], wu_ref[:, f:f + tf], preferred_element_type=F32)
        a_ref[:, f:f + tf] = jnp.square(jnp.maximum(u, 0.0)).astype(BF16)
    y = x1_ref[...] + jnp.dot(a_ref[...], wd_ref[...], preferred_element_type=F32)
    if final_norm:
        ms = jnp.mean(y * y, axis=-1, keepdims=True)
        y = y * lax.rsqrt(ms + EPS) * gf_ref[...]
    out_ref[...] = y


def _mlp(x2, oa, ob, wo, g, wu, wd, gf, final_norm, tm=1024, tf=1024):
    T, D = x2.shape
    F = wu.shape[1]
    na, nb = oa.shape[1], ob.shape[1]
    const = lambda shape: pl.BlockSpec(shape, lambda i: (0, 0), pipeline_mode=pl.Buffered(1))
    return pl.pallas_call(
        functools.partial(_mlp_kernel, final_norm=final_norm, tf=tf),
        grid=(T // tm,),
        in_specs=[
            pl.BlockSpec((tm, D), lambda i: (i, 0)),
            pl.BlockSpec((tm, na), lambda i: (i, 0)),
            pl.BlockSpec((tm, nb), lambda i: (i, 0)),
            const((na + nb, D)),
            const((1, D)),
            const((D, F)),
            const((F, D)),
            const((1, D)),
        ],
        out_specs=pl.BlockSpec((tm, D), lambda i: (i, 0)),
        out_shape=jax.ShapeDtypeStruct((T, D), F32),
        scratch_shapes=[pltpu.VMEM((tm, D), F32), pltpu.VMEM((tm, D), BF16),
                        pltpu.VMEM((tm, F), BF16)],
        compiler_params=_cparams(("arbitrary",)),
        name="mlp",
    )(x2, oa, ob, wo, g.reshape(1, D), wu, wd, gf.reshape(1, D))


def _pad_cols(w, n):
    return jnp.pad(w, ((0, 0), (0, n - w.shape[1])))


def _rot_cols(w):
    half = ROPE_DIM // 2
    return jnp.concatenate([-w[:, half:], w[:, :half]], axis=1)


def _rope_slab(w):
    z = jnp.zeros((w.shape[0], NOPE_DIM), w.dtype)
    return jnp.concatenate([z, w, jnp.zeros((w.shape[0], LANES - NOPE_DIM - ROPE_DIM), w.dtype)], axis=1)


def _even_layer(x, g_mix, w_in, b_forget, rel_bias, w_out, g_mlp, w_up, w_down, g_final,
                final_norm):
    B, S, D = x.shape
    hf, hc = b_forget.shape[0], rel_bias.shape[0]
    wf, wc = hf * HEAD_DIM, hc * HEAD_DIM
    o = np.cumsum([0, wf, wf, wf, hf, wc, wc, wc])
    qa, ka, va, fa, qb, kb, vb = [w_in[:, o[n]:o[n + 1]] for n in range(7)]
    q_scale = HEAD_DIM ** -0.5 * LOG2E
    wm = jnp.concatenate([qa * q_scale, ka, qb * q_scale, kb], axis=1).astype(BF16)
    wvt = jnp.concatenate([va, vb], axis=1).T.astype(BF16)
    wa = _pad_cols(fa, LANES).astype(BF16)
    main, vt, aux = _inproj(x, g_mix, wm, wvt, wa)

    q_aug, k_aug = _logcum(aux, _pad_cols(b_forget.reshape(1, hf), LANES), hf)
    o_a = _flash("fox", main, 0, main, wf, vt, 0, (q_aug, k_aug), hf)

    assert rel_bias.shape[1] == CHUNK + REL_CLIP
    right = CK_EXT - rel_bias.shape[1] - (CHUNK + 1)
    ext = jnp.pad(rel_bias * LOG2E, ((0, 0), (CHUNK + 1, right)), mode="edge")
    o_b = _chunk_attn(main, 2 * wf, 2 * wf + wc, vt, wf, ext, hc)

    y = _mlp(x.reshape(B * S, D), o_a.reshape(B * S, wf), o_b.reshape(B * S, wc),
             w_out.astype(BF16), g_mlp, w_up.astype(BF16), w_down.astype(BF16), g_final,
             final_norm)
    return y.reshape(B, S, D)


def _odd_layer(x, positions, g_mix, w_in, q_norm, kv_norm, w_uq, w_ukv, w_out, g_mlp, w_up,
               w_down, g_final, final_norm):
    B, S, D = x.shape
    hm = w_ukv.shape[1] // (NOPE_DIM + HEAD_DIM)
    ws = w_in.shape[1] - Q_LORA - KV_LORA - ROPE_DIM
    hs = (ws // 3) // HEAD_DIM
    wsb = hs * HEAD_DIM
    o = np.cumsum([0, wsb, wsb, wsb, Q_LORA, KV_LORA, ROPE_DIM])
    qc, kc, vc, w_cq, w_ckv, w_kr = [w_in[:, o[n]:o[n + 1]] for n in range(6)]
    wm = jnp.concatenate([qc * HEAD_DIM ** -0.5, kc], axis=1).astype(BF16)
    wa = jnp.concatenate([w_cq, w_ckv, _rope_slab(w_kr), _rope_slab(_rot_cols(w_kr))],
                         axis=1).astype(BF16)
    main, vt, aux = _inproj(x, g_mix, wm, vc.T.astype(BF16), wa)
    o_c = _sb_attn(main, 0, wsb, vt, 0, hs)

    dq = NOPE_DIM + ROPE_DIM
    wuq3 = w_uq.reshape(Q_LORA, hm, dq)
    nope, ropew = wuq3[:, :, :NOPE_DIM], wuq3[:, :, NOPE_DIM:]
    zq = jnp.zeros((Q_LORA, hm, LANES - dq), w_uq.dtype)
    wuq = jnp.concatenate([nope, ropew, zq], axis=2).reshape(Q_LORA, hm * LANES).astype(BF16)
    half = ROPE_DIM // 2
    ropr = jnp.concatenate([-ropew[:, :, half:], ropew[:, :, :half]], axis=2)
    wuqr = ropr.reshape(Q_LORA, hm * ROPE_DIM).astype(BF16)
    wkv3 = w_ukv.reshape(KV_LORA, hm, NOPE_DIM + HEAD_DIM)
    wk = jnp.concatenate([wkv3[:, :, :NOPE_DIM],
                          jnp.zeros((KV_LORA, hm, LANES - NOPE_DIM), w_ukv.dtype)], axis=2)
    wk = wk.reshape(KV_LORA, hm * LANES).astype(BF16)
    wv_t = wkv3[:, :, NOPE_DIM:].reshape(KV_LORA, hm * HEAD_DIM).T.astype(BF16)
    freqs = (ROPE_THETA ** (-jnp.arange(half, dtype=F32) / half))
    invf = jnp.tile(freqs, 2 * LANES // ROPE_DIM).reshape(1, LANES)
    groups = LANES // ROPE_DIM
    pos = positions.astype(F32).reshape(B, S // MLA_TM, groups, MLA_TM // groups)
    pos = jnp.repeat(jnp.swapaxes(pos, 2, 3), ROPE_DIM, axis=-1)
    qm, km, vtm = _mla_prep(aux, pos, invf, q_norm.reshape(1, Q_LORA),
                            kv_norm.reshape(1, KV_LORA), wuq, wuqr, wk, wv_t,
                            dq ** -0.5 * LOG2E)
    o_d = _flash("mla", qm, 0, km, 0, vtm, 0, None, hm)

    y = _mlp(x.reshape(B * S, D), o_c.reshape(B * S, wsb), o_d.reshape(B * S, hm * HEAD_DIM),
             w_out.astype(BF16), g_mlp, w_up.astype(BF16), w_down.astype(BF16), g_final,
             final_norm)
    return y.reshape(B, S, D)


def kernel(x, positions, norm_mix, norm_mlp, norm_final, w_in_ab, b_forget, rel_bias, w_out_ab,
           w_in_cd, q_norm, kv_norm, w_uq, w_ukv, w_out_cd, w_up, w_down):
    depth = norm_mix.shape[0]
    for layer in range(depth):
        last = layer == depth - 1
        if layer % 2 == 0:
            e = layer // 2
            x = _even_layer(x, norm_mix[layer], w_in_ab[e], b_forget[e], rel_bias[e], w_out_ab[e],
                            norm_mlp[layer], w_up[layer], w_down[layer], norm_final, last)
        else:
            o = layer // 2
            x = _odd_layer(x, positions, norm_mix[layer], w_in_cd[o], q_norm[o], kv_norm[o],
                           w_uq[o], w_ukv[o], w_out_cd[o], norm_mlp[layer], w_up[layer],
                           w_down[layer], norm_final, last)
    return x
```

```python
import functools
import math

import numpy as np
import jax
import jax.numpy as jnp
from jax import lax
from jax.experimental import pallas as pl
from jax.experimental.pallas import tpu as pltpu

F32 = jnp.float32
BF16 = jnp.bfloat16

EPS = 1e-6
HEAD_DIM = 64
CHUNK = 64
CHUNK_SHIFT = CHUNK.bit_length() - 1
N_LEFT_CHUNKS = 8
REL_CLIP = 256
ROPE_DIM = 32
NOPE_DIM = 64
ROPE_THETA = 10000.0
Q_LORA = 384
KV_LORA = 256

LANES = 128
VT_BLK = LANES
SUB = LANES
FLASH_HP = 8
FLASH_BQ = 512
FLASH_SUB = 256
FIXED_WIDTHS = (4, 2, 1)
SAFE_GAP = 80.0
NEG = -1e30
LOG2E = math.log2(math.e)
SB_ZERO_LOG = -104.0
VMEM_LIMIT = 56 * 1024 * 1024

_NT = (((1,), (1,)), ((), ()))


def _cparams(sem):
    return pltpu.CompilerParams(dimension_semantics=sem, vmem_limit_bytes=VMEM_LIMIT)


def _rms_bf16(x, g):
    ms = jnp.mean(x * x, axis=-1, keepdims=True)
    return (x * lax.rsqrt(ms + EPS) * g).astype(BF16)


def _store_vt(ovt_ref, vt, row0):
    rows, tm = vt.shape
    for c in range(tm // VT_BLK):
        ovt_ref[c, row0:row0 + rows, :] = vt[:, c * VT_BLK:(c + 1) * VT_BLK]


def _inproj_kernel(x_ref, g_ref, wm_ref, wvt_ref, wa_ref, om_ref, ovt_ref, oa_ref):
    h = _rms_bf16(x_ref[...], g_ref[...])
    nm = om_ref.shape[-1]
    for c in range(0, nm, 512):
        om_ref[:, c:c + 512] = jnp.dot(
            h, wm_ref[:, c:c + 512], preferred_element_type=F32).astype(BF16)
    nv = wvt_ref.shape[0]
    for r in range(0, nv, 256):
        vt = lax.dot_general(wvt_ref[r:r + 256, :], h, _NT,
                             preferred_element_type=F32).astype(BF16)
        _store_vt(ovt_ref, vt, r)
    oa_ref[...] = jnp.dot(h, wa_ref[...], preferred_element_type=F32)


def _inproj(x, g, wm, wvt, wa, tm=1024):
    B, S, D = x.shape
    nm, nv, na = wm.shape[1], wvt.shape[0], wa.shape[1]
    return pl.pallas_call(
        _inproj_kernel,
        grid=(B, S // tm),
        in_specs=[
            pl.BlockSpec((None, tm, D), lambda b, i: (b, i, 0)),
            pl.BlockSpec((1, D), lambda b, i: (0, 0)),
            pl.BlockSpec((D, nm), lambda b, i: (0, 0)),
            pl.BlockSpec((nv, D), lambda b, i: (0, 0)),
            pl.BlockSpec((D, na), lambda b, i: (0, 0)),
        ],
        out_specs=[
            pl.BlockSpec((None, tm, nm), lambda b, i: (b, i, 0)),
            pl.BlockSpec((None, tm // VT_BLK, nv, VT_BLK), lambda b, i: (b, i, 0, 0)),
            pl.BlockSpec((None, tm, na), lambda b, i: (b, i, 0)),
        ],
        out_shape=[
            jax.ShapeDtypeStruct((B, S, nm), BF16),
            jax.ShapeDtypeStruct((B, S // VT_BLK, nv, VT_BLK), BF16),
            jax.ShapeDtypeStruct((B, S, na), F32),
        ],
        compiler_params=_cparams(("arbitrary", "arbitrary")),
        name="inproj",
    )(x, g.reshape(1, D), wm, wvt, wa)


def _split3(x):
    hi = x.astype(BF16)
    r = x - hi.astype(F32)
    mid = r.astype(BF16)
    lo = (r - mid.astype(F32)).astype(BF16)
    return hi, mid, lo


AUG_W = 8


def _logcum_kernel(fa_ref, b_ref, pqk_ref, oneq_ref, onek_ref, oq_ref, ok_ref, carry_ref):
    @pl.when(pl.program_id(1) == 0)
    def _():
        carry_ref[...] = jnp.zeros_like(carry_ref)

    z = fa_ref[...] + b_ref[...]
    lf = jnp.minimum(z, 0.0) - jnp.log(1.0 + jnp.exp(-jnp.abs(z)))
    tc = lf.shape[0]
    r = lax.broadcasted_iota(jnp.int32, (tc, tc), 0)
    c = lax.broadcasted_iota(jnp.int32, (tc, tc), 1)
    tri = jnp.where(r >= c, 1.0, 0.0).astype(BF16)
    d = jnp.dot(tri, jnp.concatenate(_split3(lf), axis=1), preferred_element_type=F32)
    w = lf.shape[1]
    cs = carry_ref[...]
    for n in range(3):
        cs = cs + d[:, n * w:(n + 1) * w]
    carry_ref[...] = cs[tc - 1:tc, :]
    placed = jnp.dot(jnp.concatenate(_split3(cs * LOG2E), axis=1),
                     pqk_ref[...], preferred_element_type=F32)
    na = oq_ref.shape[-1]
    oq_ref[...] = (oneq_ref[...] + placed[:, 0:na]).astype(BF16)
    ok_ref[...] = (onek_ref[...] + placed[:, na:2 * na]).astype(BF16)


def _logcum(fa, bias, n_heads, tc=512):
    B, S, W = fa.shape
    na = LANES
    pq = np.zeros((3, W, na), np.float32)
    pk = np.zeros((3, W, na), np.float32)
    oneq = np.zeros((1, na), np.float32)
    onek = np.zeros((1, na), np.float32)
    for h in range(n_heads):
        base = h * AUG_W
        for n in range(3):
            pq[n, h, base + n] = 1.0
            pk[n, h, base + 3 + n] = -1.0
        oneq[0, base + 3:base + 6] = 1.0
        onek[0, base:base + 3] = 1.0
    const = lambda a: pl.BlockSpec(a.shape, lambda b, i: (0,) * a.ndim)
    pqk = np.concatenate([pq, pk], axis=2).reshape(3 * W, 2 * na)
    args = [jnp.asarray(pqk, BF16), jnp.asarray(oneq), jnp.asarray(onek)]
    return pl.pallas_call(
        _logcum_kernel,
        grid=(B, S // tc),
        in_specs=[pl.BlockSpec((None, tc, W), lambda b, i: (b, i, 0)),
                  pl.BlockSpec((1, W), lambda b, i: (0, 0))] + [const(a) for a in args],
        out_specs=[pl.BlockSpec((None, tc, na), lambda b, i: (b, i, 0))] * 2,
        out_shape=[jax.ShapeDtypeStruct((B, S, na), BF16)] * 2,
        scratch_shapes=[pltpu.VMEM((1, W), F32)],
        compiler_params=_cparams(("arbitrary", "arbitrary")),
        name="logcum",
    )(fa, bias, *args)


def _pair_mask_q(q2, j):
    lane = lax.broadcasted_iota(jnp.int32, q2.shape, 1)
    keep = (lane >= HEAD_DIM * j) & (lane < HEAD_DIM * (j + 1))
    return jnp.where(keep, q2, jnp.zeros_like(q2))


ONES_ROWS = 16


def _softmax_step(tiles, vts, carry, tile_max=None):
    m, acc = carry
    if tile_max is not None:
        m_new = jnp.maximum(m, tile_max)
    else:
        m_new = m
        for tile in tiles:
            m_new = jnp.maximum(m_new, jnp.max(tile(), axis=0, keepdims=True))
    alpha = jnp.exp2(m - m_new)
    pv = None
    for tile, vt in zip(tiles, vts):
        p = jnp.exp2(tile() - m_new).astype(BF16)
        vt1 = jnp.concatenate([vt, jnp.ones((ONES_ROWS, vt.shape[1]), BF16)], axis=0)
        d = jnp.dot(vt1, p, preferred_element_type=F32)
        pv = d if pv is None else pv + d
    return m_new, alpha * acc + pv


def _softmax_init(bq):
    return (jnp.full((1, bq), NEG, F32), jnp.zeros((HEAD_DIM + ONES_ROWS, bq), F32))


def _softmax_out(carry):
    _, acc = carry
    return acc[0:HEAD_DIM] / acc[HEAD_DIM:HEAD_DIM + 1]


def _store_heads(o_ref, outs):
    oT = jnp.concatenate(outs, axis=0)
    o_ref[...] = oT.T.astype(o_ref.dtype)


def _flash_kernel(*refs, mode, hp, bq):
    if mode == "fox":
        q_ref, k_ref, vt_ref, qaug_ref, kaug_ref, o_ref = refs[:6]
    else:
        q_ref, k_ref, vt_ref, o_ref = refs[:4]
    sa_ref, sb_ref, ma_ref, mb_ref, qt_ref, kn_ref = refs[-6:]
    qs = pl.program_id(2) * bq
    sub = FLASH_SUB
    row = lax.broadcasted_iota(jnp.int32, (sub, bq), 0)
    col = lax.broadcasted_iota(jnp.int32, (sub, bq), 1)
    if mode == "fox":
        kcols = [slice(LANES * (h // 2), LANES * (h // 2 + 1)) for h in range(hp)]
        lane = lax.broadcasted_iota(jnp.int32, (bq, LANES), 1)
        qa = qaug_ref[...]
        qms = []
        for h in range(hp):
            first = AUG_W * (pl.program_id(1) * hp + h)
            own = (lane >= first) & (lane < first + AUG_W)
            qms.append(jnp.concatenate(
                [_pair_mask_q(q_ref[:, kcols[h]], h % 2),
                 jnp.where(own, qa, jnp.zeros_like(qa))], axis=1))
    else:
        kcols = [slice(LANES * h, LANES * (h + 1)) for h in range(hp)]
        qms = [q_ref[:, kcols[h]] for h in range(hp)]
    for h in range(hp):
        qt_ref[h] = qms[h].T

    @pl.when(pl.program_id(2) == 0)
    def _():
        klane = lax.broadcasted_iota(jnp.int32, (1, LANES), 1)
        for h in range(hp):
            kabs = jnp.max(jnp.abs(k_ref[:, kcols[h]].astype(F32)), axis=0, keepdims=True)
            sq = kabs * kabs
            if mode == "fox":
                sq = jnp.where((klane >= HEAD_DIM * (h % 2)) & (klane < HEAD_DIM * (h % 2 + 1)),
                               sq, 0.0)
            kn_ref[h] = jnp.broadcast_to(jnp.sqrt(jnp.sum(sq, axis=1, keepdims=True)), (1, bq))

    def scores(sb, h, masked, col0=0):
        ks = pl.multiple_of(sb * sub, sub)
        k = k_ref[pl.ds(ks, sub), kcols[h]]
        if mode == "fox":
            k = jnp.concatenate([k, kaug_ref[pl.ds(ks, sub), :]], axis=1)
        sT = jnp.dot(k, qt_ref[h, :, col0:], preferred_element_type=F32)
        if masked and mode == "fox":
            sT = jnp.where((ks + row <= qs + col)[:, col0:], sT, NEG)
        elif masked:
            sT = jnp.where((((ks + row) >> CHUNK_SHIFT) <= ((qs + col) >> CHUNK_SHIFT))[:, col0:],
                           sT, NEG)
        return sT

    nsub = bq // sub
    nvt = sub // VT_BLK

    def produce(buf, sb0, masked, h):
        s_buf, m_buf = buf
        tile_max = None
        for c in range(nsub):
            col0 = c * sub if masked else 0
            sT = scores(sb0 + c, h, masked, col0)
            cm = jnp.max(sT, axis=0, keepdims=True)
            if col0:
                s_buf[h, c, :, 0:col0] = jnp.full((sub, col0), NEG, F32)
                cm = jnp.concatenate([jnp.full((1, col0), NEG, F32), cm], axis=1)
            s_buf[h, c, :, col0:] = sT
            tile_max = cm if tile_max is None else jnp.maximum(tile_max, cm)
        m_buf[h] = tile_max

    def consume(buf, sb0, carry, h):
        s_buf, m_buf = buf
        tiles = [lambda c=c: s_buf[h, c] for c in range(nsub)]
        vts = [jnp.concatenate([vt_ref[(sb0 + c) * nvt + v, HEAD_DIM * h:HEAD_DIM * (h + 1), :]
                                for v in range(nvt)], axis=1) for c in range(nsub)]
        return _softmax_step(tiles, vts, carry, tile_max=m_buf[h])

    def stage(cur, cur_sb, nxt, nxt_sb, carries):
        if nxt is not None:
            for h in range(hp):
                produce(nxt, nxt_sb, False, h)
        return tuple(consume(cur, cur_sb, carries[h], h) for h in range(hp))

    n = pl.program_id(2)
    diag_sb = qs // sub
    buf_a, buf_b = (sa_ref, ma_ref), (sb_ref, mb_ref)
    for h in range(hp):
        produce(buf_a, diag_sb, True, h)

    def pair(j, carries):
        carries = stage(buf_a, jnp.where(j == 0, diag_sb, (2 * j - 1) * nsub),
                        buf_b, 2 * j * nsub, carries)
        return stage(buf_b, 2 * j * nsub,
                     buf_a, jnp.minimum(2 * j + 1, n - 1) * nsub, carries)

    def online(_):
        carries = tuple(_softmax_init(bq) for _ in range(hp))
        carries = lax.fori_loop(0, (n + 1) // 2, pair, carries)
        carries = lax.cond(
            n % 2 == 0,
            lambda c: stage(buf_a, jnp.where(n == 0, diag_sb, (n - 1) * nsub), None, None, c),
            lambda c: c, carries)
        return jnp.concatenate([_softmax_out(c) for c in carries], axis=0)

    refs_ = []
    gap = None
    for h in range(hp):
        qf = qt_ref[h, 0:LANES, :].astype(F32)
        bound = jnp.sqrt(jnp.sum(qf * qf, axis=0, keepdims=True)) * kn_ref[h] * 1.01 + 1e-3
        refs_.append(bound)
        g = jnp.max(bound - ma_ref[h])
        gap = g if gap is None else jnp.maximum(gap, g)

    def weigh_add(acc, h, s, sb0, nblk, col0=0):
        p = jnp.exp2(s - refs_[h][:, col0:]).astype(BF16)
        vt = jnp.concatenate([vt_ref[sb0 * nvt + v, HEAD_DIM * h:HEAD_DIM * (h + 1), :]
                              for v in range(nblk * nvt)], axis=1)
        vt1 = jnp.concatenate([vt, jnp.ones((ONES_ROWS, nblk * sub), BF16)], axis=0)
        d = jnp.dot(vt1, p, preferred_element_type=F32)
        if col0:
            d = jnp.concatenate([jnp.zeros((d.shape[0], col0), F32), d], axis=1)
        return acc + d

    def fixed_reference(_):
        def run(kb, accs, nblk):
            ks = pl.multiple_of(kb * sub, sub)
            ss = []
            for h in range(hp):
                k = k_ref[pl.ds(ks, nblk * sub), kcols[h]]
                if mode == "fox":
                    k = jnp.concatenate([k, kaug_ref[pl.ds(ks, nblk * sub), :]], axis=1)
                ss.append(jnp.dot(k, qt_ref[h], preferred_element_type=F32))
            return tuple(weigh_add(accs[h], h, ss[h], kb, nblk) for h in range(hp))

        accs = tuple(jnp.zeros((HEAD_DIM + ONES_ROWS, bq), F32) for _ in range(hp))
        done = 0
        for width in FIXED_WIDTHS:
            trips = (diag_sb - done) // width
            accs = lax.fori_loop(0, trips,
                                 lambda j, a, done=done, width=width: run(done + j * width, a, width),
                                 accs)
            done = done + trips * width
        for c in range(nsub):
            accs = [weigh_add(accs[h], h, sa_ref[h, c, :, c * sub:], diag_sb + c, 1, c * sub)
                    for h in range(hp)]
        return jnp.concatenate([a[0:HEAD_DIM] / a[HEAD_DIM:HEAD_DIM + 1] for a in accs], axis=0)

    oT = lax.cond(gap <= SAFE_GAP, fixed_reference, online, None)
    o_ref[...] = oT.T.astype(o_ref.dtype)


def _flash(mode, q_arr, q_col0, k_arr, k_col0, vt_arr, vt_row0, extra, n_heads,
           hp=FLASH_HP, bq=FLASH_BQ):
    B, S, _ = q_arr.shape
    qw = (HEAD_DIM if mode == "fox" else LANES) * hp
    vw = HEAD_DIM * hp
    in_specs = [
        pl.BlockSpec((None, bq, qw), lambda b, g, i: (b, i, q_col0 // qw + g)),
        pl.BlockSpec((None, S, qw), lambda b, g, i: (b, 0, k_col0 // qw + g)),
        pl.BlockSpec((None, S // VT_BLK, vw, VT_BLK),
                     lambda b, g, i: (b, 0, vt_row0 // vw + g, 0)),
    ]
    args = [q_arr, k_arr, vt_arr]
    if mode == "fox":
        q_aug, k_aug = extra
        in_specs += [
            pl.BlockSpec((None, bq, LANES), lambda b, g, i: (b, i, 0)),
            pl.BlockSpec((None, S, LANES), lambda b, g, i: (b, 0, 0)),
        ]
        args += [q_aug, k_aug]
    return pl.pallas_call(
        functools.partial(_flash_kernel, mode=mode, hp=hp, bq=bq),
        grid=(B, n_heads // hp, S // bq),
        in_specs=in_specs,
        out_specs=pl.BlockSpec((None, bq, vw), lambda b, g, i: (b, i, g)),
        out_shape=jax.ShapeDtypeStruct((B, S, n_heads * HEAD_DIM), BF16),
        scratch_shapes=([pltpu.VMEM((hp, bq // FLASH_SUB, FLASH_SUB, bq), F32)] * 2
                        + [pltpu.VMEM((hp, 1, bq), F32)] * 2
                        + [pltpu.VMEM((hp, 2 * LANES if mode == "fox" else LANES, bq), BF16),
                           pltpu.VMEM((hp, 1, bq), F32)]),
        compiler_params=_cparams(("arbitrary", "arbitrary", "arbitrary")),
        name="flash_" + mode,
    )(*args)


CK_B = 2 * CHUNK
CK_NW = N_LEFT_CHUNKS * CHUNK // CK_B + 1
CK_EXT = (CK_NW + 1) * CK_B


def _chunk_kernel(q_ref, k_ref, vt_ref, ext_ref, o_ref, tab_ref, s_ref, *, hp, nq):
    i = pl.program_id(1)

    @pl.when(i == 0)
    def _():
        jj = lax.broadcasted_iota(jnp.int32, (CK_B, CK_B), 0)
        rr = lax.broadcasted_iota(jnp.int32, (CK_B, CK_B), 1)
        for h in range(hp):
            for w in range(CK_NW):
                a = (CK_NW - 1 - w) * CK_B
                g = jnp.broadcast_to(ext_ref[h:h + 1, a:a + 2 * CK_B], (CK_B, 2 * CK_B))
                t = pltpu.roll(g, CK_B, 1, stride=1, stride_axis=0)[:, :CK_B]
                if w == 0:
                    t = jnp.where((rr >= CHUNK) & (jj < CHUNK), NEG, t)
                if w == CK_NW - 1:
                    t = jnp.where((rr < CHUNK) & (jj >= CHUNK), NEG, t)
                tab_ref[h, w * CK_B:(w + 1) * CK_B, :] = t

    kcols = [slice(LANES * (h // 2), LANES * (h // 2 + 1)) for h in range(hp)]
    firsts = [i * nq + u - (CK_NW - 1) for u in range(nq)]

    def finish():
        for u in range(nq):
            kbc = [jnp.maximum(firsts[u] + w, 0) for w in range(CK_NW)]
            outs = []
            for h in range(hp):
                vt = jnp.concatenate([vt_ref[kbc[w], HEAD_DIM * h:HEAD_DIM * (h + 1), :]
                                      for w in range(CK_NW)], axis=1)
                outs.append(_softmax_out(_softmax_step([lambda u=u, h=h: s_ref[u, h]], [vt],
                                                       _softmax_init(CK_B))))
            oT = jnp.concatenate(outs, axis=0)
            o_ref[u * CK_B:(u + 1) * CK_B, :] = oT.T.astype(o_ref.dtype)

    def pair_scores(u, p, ks, nrows):
        q2 = q_ref[u * CK_B:(u + 1) * CK_B, kcols[2 * p]]
        qq = jnp.concatenate([_pair_mask_q(q2, 0), _pair_mask_q(q2, 1)], axis=0)
        return lax.dot_general(k_ref[pl.ds(ks, nrows), kcols[2 * p]], qq, _NT,
                               preferred_element_type=F32)

    @pl.when(firsts[0] >= 0)
    def _():
        for u in range(nq):
            ks = pl.multiple_of(firsts[u] * CK_B, CK_B)
            for p in range(hp // 2):
                sT = pair_scores(u, p, ks, CK_NW * CK_B)
                for j in range(2):
                    s_ref[u, 2 * p + j] = sT[:, j * CK_B:(j + 1) * CK_B] + tab_ref[2 * p + j]
        finish()

    @pl.when(firsts[0] < 0)
    def _():
        for u in range(nq):
            for p in range(hp // 2):
                for w in range(CK_NW):
                    rows = slice(w * CK_B, (w + 1) * CK_B)
                    ks = pl.multiple_of(jnp.maximum(firsts[u] + w, 0) * CK_B, CK_B)
                    sT = pair_scores(u, p, ks, CK_B)
                    for j in range(2):
                        s_ref[u, 2 * p + j, rows, :] = jnp.where(
                            firsts[u] + w >= 0,
                            sT[:, j * CK_B:(j + 1) * CK_B] + tab_ref[2 * p + j, rows, :], NEG)
        finish()


def _chunk_attn(main, q_col0, k_col0, vt_arr, vt_row0, ext, n_heads, nq=4):
    B, S, _ = main.shape
    hp = n_heads
    qw, vw = HEAD_DIM * hp, HEAD_DIM * hp
    return pl.pallas_call(
        functools.partial(_chunk_kernel, hp=hp, nq=nq),
        grid=(B, S // (nq * CK_B)),
        in_specs=[
            pl.BlockSpec((None, nq * CK_B, qw), lambda b, i: (b, i, q_col0 // qw)),
            pl.BlockSpec((None, S, qw), lambda b, i: (b, 0, k_col0 // qw)),
            pl.BlockSpec((None, S // VT_BLK, vw, VT_BLK), lambda b, i: (b, 0, vt_row0 // vw, 0)),
            pl.BlockSpec((hp, CK_EXT), lambda b, i: (0, 0)),
        ],
        out_specs=pl.BlockSpec((None, nq * CK_B, vw), lambda b, i: (b, i, 0)),
        out_shape=jax.ShapeDtypeStruct((B, S, n_heads * HEAD_DIM), BF16),
        scratch_shapes=[pltpu.VMEM((hp, CK_NW * CK_B, CK_B), F32),
                        pltpu.VMEM((nq, hp, CK_NW * CK_B, CK_B), F32)],
        compiler_params=_cparams(("arbitrary", "arbitrary")),
        name="chunk_attn",
    )(main, main, vt_arr, ext)


def _sb_kernel(q_ref, k_ref, vt_ref, o_ref, z_ref, lb_ref, sfx_ref, *, hp, bq):
    qs = pl.program_id(1) * bq
    nsub = bq // SUB
    row = lax.broadcasted_iota(jnp.int32, (SUB, bq), 0)
    col = lax.broadcasted_iota(jnp.int32, (SUB, bq), 1)
    ur = lax.broadcasted_iota(jnp.int32, (SUB, 2 * SUB), 0)
    uc = lax.broadcasted_iota(jnp.int32, (SUB, 2 * SUB), 1) & (SUB - 1)
    upper2 = jnp.where(uc > ur, 1.0, 0.0).astype(BF16)
    kcols = [slice(LANES * (h // 2), LANES * (h // 2 + 1)) for h in range(hp)]
    qms = [_pair_mask_q(q_ref[:, kcols[h]], h % 2) for h in range(hp)]

    def step(kb, carries, masked, live=None):
        ks = pl.multiple_of(kb * bq, bq)
        for h in range(hp):
            z_ref[h] = lax.dot_general(k_ref[pl.ds(ks, bq), kcols[h]], qms[h], _NT,
                                       preferred_element_type=F32)
        first_col = [c * SUB if masked else 0 for c in range(nsub)]

        def widen(x, c):
            if first_col[c] == 0:
                return x
            return jnp.concatenate([jnp.zeros((x.shape[0], first_col[c]), x.dtype), x], axis=1)

        totals = []
        for h in range(hp):
            tot = []
            for c in range(nsub):
                rows, cols = slice(c * SUB, (c + 1) * SUB), slice(first_col[c], bq)
                z = z_ref[h, rows, cols]
                l1 = jnp.log(1.0 + jnp.exp2(jnp.abs(z) * (-LOG2E)))
                log_beta = jnp.minimum(z, 0.0) - l1
                log_keep = log_beta - z
                if masked:
                    valid = (ks + c * SUB + row < qs + col)[:, cols]
                    log_keep = jnp.where(valid, log_keep, 0.0)
                lb_ref[h, rows, cols] = log_beta
                hi = log_keep.astype(BF16)
                lo = (log_keep - hi.astype(F32)).astype(BF16)
                sfx = jnp.dot(upper2, jnp.concatenate([hi, lo], axis=0),
                              preferred_element_type=F32)
                sfx_ref[h, rows, cols] = sfx
                tot.append(widen(sfx[0:1, :] + log_keep[0:1, :], c))
            totals.append(tot)
        out = []
        for h in range(hp):
            tail, acc = carries[h]
            if live is not None:
                tail = jnp.where(live, tail, NEG)
            parts = [None] * nsub
            for c in range(nsub - 1, -1, -1):
                rows, cols = slice(c * SUB, (c + 1) * SUB), slice(first_col[c], bq)
                a = jnp.exp(lb_ref[h, rows, cols] + sfx_ref[h, rows, cols] + tail[:, cols])
                if masked:
                    a = jnp.where((ks + c * SUB + row < qs + col)[:, cols], a, 0.0)
                parts[c] = a.astype(BF16)
                tail = tail + totals[h][c]
            vts = [vt_ref[kb * nsub + c, HEAD_DIM * h:HEAD_DIM * (h + 1), :] for c in range(nsub)]
            if masked:
                for c in range(nsub):
                    acc = acc + widen(jnp.dot(vts[c], parts[c], preferred_element_type=F32), c)
            else:
                acc = acc + jnp.dot(jnp.concatenate(vts, axis=1), jnp.concatenate(parts, axis=0),
                                    preferred_element_type=F32)
            out.append((tail, acc))
        return tuple(out)

    n_full = qs // bq
    carries = tuple((jnp.zeros((1, bq), F32), jnp.zeros((HEAD_DIM, bq), F32))
                    for _ in range(hp))
    carries = step(n_full, carries, True)
    carries = step(jnp.maximum(n_full - 1, 0), carries, False, live=n_full >= 1)

    def cond(state):
        kb, carries = state
        tail_max = carries[0][0]
        for h in range(1, hp):
            tail_max = jnp.maximum(tail_max, carries[h][0])
        return (kb >= 0) & (jnp.max(tail_max) > SB_ZERO_LOG)

    def body(state):
        kb, carries = state
        return kb - 1, step(kb, carries, False)

    _, carries = lax.while_loop(cond, body, (n_full - 2, carries))
    _store_heads(o_ref, [acc for (_, acc) in carries])


def _sb_attn(main, q_col0, k_col0, vt_arr, vt_row0, n_heads, bq=256):
    B, S, _ = main.shape
    hp = n_heads
    qw = HEAD_DIM * hp
    return pl.pallas_call(
        functools.partial(_sb_kernel, hp=hp, bq=bq),
        grid=(B, S // bq),
        in_specs=[
            pl.BlockSpec((None, bq, qw), lambda b, i: (b, i, q_col0 // qw)),
            pl.BlockSpec((None, S, qw), lambda b, i: (b, 0, k_col0 // qw)),
            pl.BlockSpec((None, S // VT_BLK, qw, VT_BLK), lambda b, i: (b, 0, vt_row0 // qw, 0)),
        ],
        out_specs=pl.BlockSpec((None, bq, qw), lambda b, i: (b, i, 0)),
        out_shape=jax.ShapeDtypeStruct((B, S, n_heads * HEAD_DIM), BF16),
        scratch_shapes=[pltpu.VMEM((hp, bq, bq), F32)] * 3,
        compiler_params=_cparams(("arbitrary", "arbitrary")),
        name="sb_attn",
    )(main, main, vt_arr)


def _mla_prep_kernel(aux_ref, pos_ref, invf_ref, qn_ref, kvn_ref, wuq_ref, wuqr_ref,
                     wk_ref, wvt_ref, oq_ref, ok_ref, ovt_ref, *, q_scale):
    ang = pos_ref[...] * invf_ref[...]
    cos4, sin4 = jnp.cos(ang), jnp.sin(ang)
    lane = lax.broadcasted_iota(jnp.int32, ang.shape, 1)
    rotary = (lane >= NOPE_DIM) & (lane < NOPE_DIM + ROPE_DIM)
    cos_rows, sin_rows = [], []
    for m in range(LANES // ROPE_DIM):
        shift = (NOPE_DIM - ROPE_DIM * m) % LANES
        cm = cos4 if shift == 0 else pltpu.roll(cos4, shift, 1)
        sm = sin4 if shift == 0 else pltpu.roll(sin4, shift, 1)
        cos_rows.append(jnp.where(rotary, cm, 1.0))
        sin_rows.append(jnp.where(rotary, sm, 0.0))
    cos = jnp.concatenate(cos_rows, axis=0)
    sin = jnp.concatenate(sin_rows, axis=0)
    cq = _rms_bf16(aux_ref[:, 0:Q_LORA], qn_ref[...])
    ckv = _rms_bf16(aux_ref[:, Q_LORA:Q_LORA + KV_LORA], kvn_ref[...])
    o = Q_LORA + KV_LORA
    slab = aux_ref[:, o:o + LANES]
    klane = lax.broadcasted_iota(jnp.int32, slab.shape, 1)
    k_rope = jnp.where((klane >= NOPE_DIM) & (klane < NOPE_DIM + ROPE_DIM),
                       slab * cos + pltpu.roll(slab, LANES - ROPE_DIM, 1) * sin, 0.0)
    cos_q, sin_q = cos * q_scale, sin * q_scale
    n_heads = oq_ref.shape[-1] // LANES
    qr = jnp.dot(cq, wuqr_ref[...], preferred_element_type=F32)
    per_group = LANES // ROPE_DIM
    for h in range(0, n_heads, 2):
        cols = slice(h * LANES, (h + 2) * LANES)
        qa = jnp.dot(cq, wuq_ref[:, cols], preferred_element_type=F32)
        kn = jnp.dot(ckv, wk_ref[:, cols], preferred_element_type=F32)
        for d in range(2):
            c1 = slice(d * LANES, (d + 1) * LANES)
            c2 = slice((h + d) * LANES, (h + d + 1) * LANES)
            grp, m = divmod(h + d, per_group)
            qb = qr[:, grp * LANES:(grp + 1) * LANES]
            shift = (NOPE_DIM - ROPE_DIM * m) % LANES
            if shift:
                qb = pltpu.roll(qb, shift, 1)
            oq_ref[:, c2] = (qa[:, c1] * cos_q + qb * sin_q).astype(BF16)
            ok_ref[:, c2] = (kn[:, c1] + k_rope).astype(BF16)
    nv = wvt_ref.shape[0]
    for r in range(0, nv, 256):
        vt = lax.dot_general(wvt_ref[r:r + 256, :], ckv, _NT,
                             preferred_element_type=F32).astype(BF16)
        _store_vt(ovt_ref, vt, r)


MLA_TM = 1024


def _mla_prep(aux, pos, invf, qn, kvn, wuq, wuqr, wk, wvt, q_scale, tm=MLA_TM):
    B, S, na = aux.shape
    nq, nv = wuq.shape[1], wvt.shape[0]
    full = lambda a: pl.BlockSpec(a.shape, lambda b, i: (0,) * a.ndim)
    return pl.pallas_call(
        functools.partial(_mla_prep_kernel, q_scale=q_scale),
        grid=(B, S // tm),
        in_specs=[
            pl.BlockSpec((None, tm, na), lambda b, i: (b, i, 0)),
            pl.BlockSpec((None, None) + pos.shape[2:], lambda b, i: (b, i, 0, 0)),
            full(invf), full(qn), full(kvn), full(wuq), full(wuqr), full(wk), full(wvt),
        ],
        out_specs=[
            pl.BlockSpec((None, tm, nq), lambda b, i: (b, i, 0)),
            pl.BlockSpec((None, tm, nq), lambda b, i: (b, i, 0)),
            pl.BlockSpec((None, tm // VT_BLK, nv, VT_BLK), lambda b, i: (b, i, 0, 0)),
        ],
        out_shape=[
            jax.ShapeDtypeStruct((B, S, nq), BF16),
            jax.ShapeDtypeStruct((B, S, nq), BF16),
            jax.ShapeDtypeStruct((B, S // VT_BLK, nv, VT_BLK), BF16),
        ],
        compiler_params=_cparams(("arbitrary", "arbitrary")),
        name="mla_prep",
    )(aux, pos, invf, qn, kvn, wuq, wuqr, wk, wvt)


def _mlp_kernel(x_ref, oa_ref, ob_ref, wo_ref, g_ref, wu_ref, wd_ref, gf_ref, out_ref,
                x1_ref, h_ref, a_ref, *, final_norm, tf):
    na = oa_ref.shape[-1]
    x1 = (x_ref[...]
          + jnp.dot(oa_ref[...], wo_ref[0:na, :], preferred_element_type=F32)
          + jnp.dot(ob_ref[...], wo_ref[na:, :], preferred_element_type=F32))
    x1_ref[...] = x1
    h_ref[...] = _rms_bf16(x1, g_ref[...])
    for f in range(0, wu_ref.shape[1], tf):
        u = jnp.dot(h_ref[...], wu_ref[:, f:f + tf], preferred_element_type=F32)
        a_ref[:, f:f + tf] = jnp.square(jnp.maximum(u, 0.0)).astype(BF16)
    y = x1_ref[...] + jnp.dot(a_ref[...], wd_ref[...], preferred_element_type=F32)
    if final_norm:
        ms = jnp.mean(y * y, axis=-1, keepdims=True)
        y = y * lax.rsqrt(ms + EPS) * gf_ref[...]
    out_ref[...] = y


def _mlp(x2, oa, ob, wo, g, wu, wd, gf, final_norm, tm=1024, tf=1024):
    T, D = x2.shape
    F = wu.shape[1]
    na, nb = oa.shape[1], ob.shape[1]
    const = lambda shape: pl.BlockSpec(shape, lambda i: (0, 0), pipeline_mode=pl.Buffered(1))
    return pl.pallas_call(
        functools.partial(_mlp_kernel, final_norm=final_norm, tf=tf),
        grid=(T // tm,),
        in_specs=[
            pl.BlockSpec((tm, D), lambda i: (i, 0)),
            pl.BlockSpec((tm, na), lambda i: (i, 0)),
            pl.BlockSpec((tm, nb), lambda i: (i, 0)),
            const((na + nb, D)),
            const((1, D)),
            const((D, F)),
            const((F, D)),
            const((1, D)),
        ],
        out_specs=pl.BlockSpec((tm, D), lambda i: (i, 0)),
        out_shape=jax.ShapeDtypeStruct((T, D), F32),
        scratch_shapes=[pltpu.VMEM((tm, D), F32), pltpu.VMEM((tm, D), BF16),
                        pltpu.VMEM((tm, F), BF16)],
        compiler_params=_cparams(("arbitrary",)),
        name="mlp",
    )(x2, oa, ob, wo, g.reshape(1, D), wu, wd, gf.reshape(1, D))


def _pad_cols(w, n):
    return jnp.pad(w, ((0, 0), (0, n - w.shape[1])))


def _rot_cols(w):
    half = ROPE_DIM // 2
    return jnp.concatenate([-w[:, half:], w[:, :half]], axis=1)


def _even_layer(x, g_mix, w_in, b_forget, rel_bias, w_out, g_mlp, w_up, w_down, g_final,
                final_norm):
    B, S, D = x.shape
    hf, hc = b_forget.shape[0], rel_bias.shape[0]
    wf, wc = hf * HEAD_DIM, hc * HEAD_DIM
    o = np.cumsum([0, wf, wf, wf, hf, wc, wc, wc])
    qa, ka, va, fa, qb, kb, vb = [w_in[:, o[n]:o[n + 1]] for n in range(7)]
    q_scale = HEAD_DIM ** -0.5 * LOG2E
    wm = jnp.concatenate([qa * q_scale, ka, qb * q_scale, kb], axis=1).astype(BF16)
    wvt = jnp.concatenate([va, vb], axis=1).T.astype(BF16)
    wa = _pad_cols(fa, LANES).astype(BF16)
    main, vt, aux = _inproj(x, g_mix, wm, wvt, wa)

    q_aug, k_aug = _logcum(aux, _pad_cols(b_forget.reshape(1, hf), LANES), hf)
    o_a = _flash("fox", main, 0, main, wf, vt, 0, (q_aug, k_aug), hf)

    assert rel_bias.shape[1] == CHUNK + REL_CLIP
    right = CK_EXT - rel_bias.shape[1] - (CHUNK + 1)
    ext = jnp.pad(rel_bias * LOG2E, ((0, 0), (CHUNK + 1, right)), mode="edge")
    o_b = _chunk_attn(main, 2 * wf, 2 * wf + wc, vt, wf, ext, hc)

    y = _mlp(x.reshape(B * S, D), o_a.reshape(B * S, wf), o_b.reshape(B * S, wc),
             w_out.astype(BF16), g_mlp, w_up.astype(BF16), w_down.astype(BF16), g_final,
             final_norm)
    return y.reshape(B, S, D)


def _odd_layer(x, positions, g_mix, w_in, q_norm, kv_norm, w_uq, w_ukv, w_out, g_mlp, w_up,
               w_down, g_final, final_norm):
    B, S, D = x.shape
    hm = w_ukv.shape[1] // (NOPE_DIM + HEAD_DIM)
    ws = w_in.shape[1] - Q_LORA - KV_LORA - ROPE_DIM
    hs = (ws // 3) // HEAD_DIM
    wsb = hs * HEAD_DIM
    o = np.cumsum([0, wsb, wsb, wsb, Q_LORA, KV_LORA, ROPE_DIM])
    qc, kc, vc, w_cq, w_ckv, w_kr = [w_in[:, o[n]:o[n + 1]] for n in range(6)]
    wm = jnp.concatenate([qc * HEAD_DIM ** -0.5, kc], axis=1).astype(BF16)
    wa = jnp.concatenate([w_cq, w_ckv, jnp.zeros((D, NOPE_DIM), w_kr.dtype), w_kr,
                          _rot_cols(w_kr)], axis=1).astype(BF16)
    main, vt, aux = _inproj(x, g_mix, wm, vc.T.astype(BF16), wa)
    o_c = _sb_attn(main, 0, wsb, vt, 0, hs)

    dq = NOPE_DIM + ROPE_DIM
    wuq3 = w_uq.reshape(Q_LORA, hm, dq)
    nope, ropew = wuq3[:, :, :NOPE_DIM], wuq3[:, :, NOPE_DIM:]
    zq = jnp.zeros((Q_LORA, hm, LANES - dq), w_uq.dtype)
    wuq = jnp.concatenate([nope, ropew, zq], axis=2).reshape(Q_LORA, hm * LANES).astype(BF16)
    half = ROPE_DIM // 2
    ropr = jnp.concatenate([-ropew[:, :, half:], ropew[:, :, :half]], axis=2)
    wuqr = ropr.reshape(Q_LORA, hm * ROPE_DIM).astype(BF16)
    wkv3 = w_ukv.reshape(KV_LORA, hm, NOPE_DIM + HEAD_DIM)
    wk = jnp.concatenate([wkv3[:, :, :NOPE_DIM],
                          jnp.zeros((KV_LORA, hm, LANES - NOPE_DIM), w_ukv.dtype)], axis=2)
    wk = wk.reshape(KV_LORA, hm * LANES).astype(BF16)
    wv_t = wkv3[:, :, NOPE_DIM:].reshape(KV_LORA, hm * HEAD_DIM).T.astype(BF16)
    freqs = (ROPE_THETA ** (-jnp.arange(half, dtype=F32) / half))
    invf = jnp.tile(freqs, 2 * LANES // ROPE_DIM).reshape(1, LANES)
    groups = LANES // ROPE_DIM
    pos = positions.astype(F32).reshape(B, S // MLA_TM, groups, MLA_TM // groups)
    pos = jnp.repeat(jnp.swapaxes(pos, 2, 3), ROPE_DIM, axis=-1)
    qm, km, vtm = _mla_prep(aux, pos, invf, q_norm.reshape(1, Q_LORA),
                            kv_norm.reshape(1, KV_LORA), wuq, wuqr, wk, wv_t,
                            dq ** -0.5 * LOG2E)
    o_d = _flash("mla", qm, 0, km, 0, vtm, 0, None, hm)

    y = _mlp(x.reshape(B * S, D), o_c.reshape(B * S, wsb), o_d.reshape(B * S, hm * HEAD_DIM),
             w_out.astype(BF16), g_mlp, w_up.astype(BF16), w_down.astype(BF16), g_final,
             final_norm)
    return y.reshape(B, S, D)


def kernel(x, positions, norm_mix, norm_mlp, norm_final, w_in_ab, b_forget, rel_bias, w_out_ab,
           w_in_cd, q_norm, kv_norm, w_uq, w_ukv, w_out_cd, w_up, w_down):
    depth = norm_mix.shape[0]
    for layer in range(depth):
        last = layer == depth - 1
        if layer % 2 == 0:
            e = layer // 2
            x = _even_layer(x, norm_mix[layer], w_in_ab[e], b_forget[e], rel_bias[e], w_out_ab[e],
                            norm_mlp[layer], w_up[layer], w_down[layer], norm_final, last)
        else:
            o = layer // 2
            x = _odd_layer(x, positions, norm_mix[layer], w_in_cd[o], q_norm[o], kv_norm[o],
                           w_uq[o], w_ukv[o], w_out_cd[o], norm_mlp[layer], w_up[layer],
                           w_down[layer], norm_final, last)
    return x
```

```python
import functools
import math

import numpy as np
import jax
import jax.numpy as jnp
from jax import lax
from jax.experimental import pallas as pl
from jax.experimental.pallas import tpu as pltpu

F32 = jnp.float32
BF16 = jnp.bfloat16

EPS = 1e-6
HEAD_DIM = 64
CHUNK = 64
CHUNK_SHIFT = CHUNK.bit_length() - 1
N_LEFT_CHUNKS = 8
REL_CLIP = 256
ROPE_DIM = 32
NOPE_DIM = 64
ROPE_THETA = 10000.0
Q_LORA = 384
KV_LORA = 256

LANES = 128
VT_BLK = LANES
SUB = LANES
FLASH_HP = 8
FLASH_BQ = 512
FLASH_SUB = 256
FIXED_WIDTHS = (4, 2, 1)
SAFE_GAP = 80.0
NEG = -1e30
LOG2E = math.log2(math.e)
SB_ZERO_LOG = -104.0
VMEM_LIMIT = 56 * 1024 * 1024

_NT = (((1,), (1,)), ((), ()))


def _cparams(sem):
    return pltpu.CompilerParams(dimension_semantics=sem, vmem_limit_bytes=VMEM_LIMIT)


def _rms_bf16(x, g):
    ms = jnp.mean(x * x, axis=-1, keepdims=True)
    return (x * lax.rsqrt(ms + EPS) * g).astype(BF16)


def _store_vt(ovt_ref, vt, row0):
    rows, tm = vt.shape
    for c in range(tm // VT_BLK):
        ovt_ref[c, row0:row0 + rows, :] = vt[:, c * VT_BLK:(c + 1) * VT_BLK]


def _inproj_kernel(x_ref, g_ref, wm_ref, wvt_ref, wa_ref, om_ref, ovt_ref, oa_ref):
    h = _rms_bf16(x_ref[...], g_ref[...])
    nm = om_ref.shape[-1]
    for c in range(0, nm, 512):
        om_ref[:, c:c + 512] = jnp.dot(
            h, wm_ref[:, c:c + 512], preferred_element_type=F32).astype(BF16)
    nv = wvt_ref.shape[0]
    for r in range(0, nv, 256):
        vt = lax.dot_general(wvt_ref[r:r + 256, :], h, _NT,
                             preferred_element_type=F32).astype(BF16)
        _store_vt(ovt_ref, vt, r)
    oa_ref[...] = jnp.dot(h, wa_ref[...], preferred_element_type=F32)


def _inproj(x, g, wm, wvt, wa, tm=1024):
    B, S, D = x.shape
    nm, nv, na = wm.shape[1], wvt.shape[0], wa.shape[1]
    return pl.pallas_call(
        _inproj_kernel,
        grid=(B, S // tm),
        in_specs=[
            pl.BlockSpec((None, tm, D), lambda b, i: (b, i, 0)),
            pl.BlockSpec((1, D), lambda b, i: (0, 0)),
            pl.BlockSpec((D, nm), lambda b, i: (0, 0)),
            pl.BlockSpec((nv, D), lambda b, i: (0, 0)),
            pl.BlockSpec((D, na), lambda b, i: (0, 0)),
        ],
        out_specs=[
            pl.BlockSpec((None, tm, nm), lambda b, i: (b, i, 0)),
            pl.BlockSpec((None, tm // VT_BLK, nv, VT_BLK), lambda b, i: (b, i, 0, 0)),
            pl.BlockSpec((None, tm, na), lambda b, i: (b, i, 0)),
        ],
        out_shape=[
            jax.ShapeDtypeStruct((B, S, nm), BF16),
            jax.ShapeDtypeStruct((B, S // VT_BLK, nv, VT_BLK), BF16),
            jax.ShapeDtypeStruct((B, S, na), F32),
        ],
        compiler_params=_cparams(("arbitrary", "arbitrary")),
        name="inproj",
    )(x, g.reshape(1, D), wm, wvt, wa)


def _split3(x):
    hi = x.astype(BF16)
    r = x - hi.astype(F32)
    mid = r.astype(BF16)
    lo = (r - mid.astype(F32)).astype(BF16)
    return hi, mid, lo


AUG_W = 8


def _logcum_kernel(fa_ref, b_ref, pqk_ref, oneq_ref, onek_ref, oq_ref, ok_ref, carry_ref):
    @pl.when(pl.program_id(1) == 0)
    def _():
        carry_ref[...] = jnp.zeros_like(carry_ref)

    z = fa_ref[...] + b_ref[...]
    lf = jnp.minimum(z, 0.0) - jnp.log(1.0 + jnp.exp(-jnp.abs(z)))
    tc = lf.shape[0]
    r = lax.broadcasted_iota(jnp.int32, (tc, tc), 0)
    c = lax.broadcasted_iota(jnp.int32, (tc, tc), 1)
    tri = jnp.where(r >= c, 1.0, 0.0).astype(BF16)
    d = jnp.dot(tri, jnp.concatenate(_split3(lf), axis=1), preferred_element_type=F32)
    w = lf.shape[1]
    cs = carry_ref[...]
    for n in range(3):
        cs = cs + d[:, n * w:(n + 1) * w]
    carry_ref[...] = cs[tc - 1:tc, :]
    placed = jnp.dot(jnp.concatenate(_split3(cs * LOG2E), axis=1),
                     pqk_ref[...], preferred_element_type=F32)
    na = oq_ref.shape[-1]
    oq_ref[...] = (oneq_ref[...] + placed[:, 0:na]).astype(BF16)
    ok_ref[...] = (onek_ref[...] + placed[:, na:2 * na]).astype(BF16)


def _logcum(fa, bias, n_heads, tc=512):
    B, S, W = fa.shape
    na = LANES
    pq = np.zeros((3, W, na), np.float32)
    pk = np.zeros((3, W, na), np.float32)
    oneq = np.zeros((1, na), np.float32)
    onek = np.zeros((1, na), np.float32)
    for h in range(n_heads):
        base = h * AUG_W
        for n in range(3):
            pq[n, h, base + n] = 1.0
            pk[n, h, base + 3 + n] = -1.0
        oneq[0, base + 3:base + 6] = 1.0
        onek[0, base:base + 3] = 1.0
    const = lambda a: pl.BlockSpec(a.shape, lambda b, i: (0,) * a.ndim)
    pqk = np.concatenate([pq, pk], axis=2).reshape(3 * W, 2 * na)
    args = [jnp.asarray(pqk, BF16), jnp.asarray(oneq), jnp.asarray(onek)]
    return pl.pallas_call(
        _logcum_kernel,
        grid=(B, S // tc),
        in_specs=[pl.BlockSpec((None, tc, W), lambda b, i: (b, i, 0)),
                  pl.BlockSpec((1, W), lambda b, i: (0, 0))] + [const(a) for a in args],
        out_specs=[pl.BlockSpec((None, tc, na), lambda b, i: (b, i, 0))] * 2,
        out_shape=[jax.ShapeDtypeStruct((B, S, na), BF16)] * 2,
        scratch_shapes=[pltpu.VMEM((1, W), F32)],
        compiler_params=_cparams(("arbitrary", "arbitrary")),
        name="logcum",
    )(fa, bias, *args)


def _pair_mask_q(q2, j):
    lane = lax.broadcasted_iota(jnp.int32, q2.shape, 1)
    keep = (lane >= HEAD_DIM * j) & (lane < HEAD_DIM * (j + 1))
    return jnp.where(keep, q2, jnp.zeros_like(q2))


ONES_ROWS = 16


def _softmax_step(tiles, vts, carry, tile_max=None):
    m, acc = carry
    if tile_max is not None:
        m_new = jnp.maximum(m, tile_max)
    else:
        m_new = m
        for tile in tiles:
            m_new = jnp.maximum(m_new, jnp.max(tile(), axis=0, keepdims=True))
    alpha = jnp.exp2(m - m_new)
    pv = None
    for tile, vt in zip(tiles, vts):
        p = jnp.exp2(tile() - m_new).astype(BF16)
        vt1 = jnp.concatenate([vt, jnp.ones((ONES_ROWS, vt.shape[1]), BF16)], axis=0)
        d = jnp.dot(vt1, p, preferred_element_type=F32)
        pv = d if pv is None else pv + d
    return m_new, alpha * acc + pv


def _softmax_init(bq):
    return (jnp.full((1, bq), NEG, F32), jnp.zeros((HEAD_DIM + ONES_ROWS, bq), F32))


def _softmax_out(carry):
    _, acc = carry
    return acc[0:HEAD_DIM] / acc[HEAD_DIM:HEAD_DIM + 1]


def _store_heads(o_ref, outs):
    oT = jnp.concatenate(outs, axis=0)
    o_ref[...] = oT.T.astype(o_ref.dtype)


def _flash_kernel(*refs, mode, hp, bq):
    if mode == "fox":
        q_ref, k_ref, vt_ref, qaug_ref, kaug_ref, o_ref = refs[:6]
    else:
        q_ref, k_ref, vt_ref, o_ref = refs[:4]
    sa_ref, sb_ref, ma_ref, mb_ref, qt_ref, kn_ref = refs[-6:]
    qs = pl.program_id(2) * bq
    sub = FLASH_SUB
    row = lax.broadcasted_iota(jnp.int32, (sub, bq), 0)
    col = lax.broadcasted_iota(jnp.int32, (sub, bq), 1)
    if mode == "fox":
        kcols = [slice(LANES * (h // 2), LANES * (h // 2 + 1)) for h in range(hp)]
        lane = lax.broadcasted_iota(jnp.int32, (bq, LANES), 1)
        qa = qaug_ref[...]
        qms = []
        for h in range(hp):
            first = AUG_W * (pl.program_id(1) * hp + h)
            own = (lane >= first) & (lane < first + AUG_W)
            qms.append(jnp.concatenate(
                [_pair_mask_q(q_ref[:, kcols[h]], h % 2),
                 jnp.where(own, qa, jnp.zeros_like(qa))], axis=1))
    else:
        kcols = [slice(LANES * h, LANES * (h + 1)) for h in range(hp)]
        qms = [q_ref[:, kcols[h]] for h in range(hp)]
    for h in range(hp):
        qt_ref[h] = qms[h].T

    @pl.when(pl.program_id(2) == 0)
    def _():
        klane = lax.broadcasted_iota(jnp.int32, (1, LANES), 1)
        for h in range(hp):
            kabs = jnp.max(jnp.abs(k_ref[:, kcols[h]].astype(F32)), axis=0, keepdims=True)
            sq = kabs * kabs
            if mode == "fox":
                sq = jnp.where((klane >= HEAD_DIM * (h % 2)) & (klane < HEAD_DIM * (h % 2 + 1)),
                               sq, 0.0)
            kn_ref[h] = jnp.broadcast_to(jnp.sqrt(jnp.sum(sq, axis=1, keepdims=True)), (1, bq))

    def scores(sb, h, masked, col0=0):
        ks = pl.multiple_of(sb * sub, sub)
        k = k_ref[pl.ds(ks, sub), kcols[h]]
        if mode == "fox":
            k = jnp.concatenate([k, kaug_ref[pl.ds(ks, sub), :]], axis=1)
        sT = jnp.dot(k, qt_ref[h, :, col0:], preferred_element_type=F32)
        if masked and mode == "fox":
            sT = jnp.where((ks + row <= qs + col)[:, col0:], sT, NEG)
        elif masked:
            sT = jnp.where((((ks + row) >> CHUNK_SHIFT) <= ((qs + col) >> CHUNK_SHIFT))[:, col0:],
                           sT, NEG)
        return sT

    nsub = bq // sub
    nvt = sub // VT_BLK

    def produce(buf, sb0, masked, h):
        s_buf, m_buf = buf
        tile_max = None
        for c in range(nsub):
            col0 = c * sub if masked else 0
            sT = scores(sb0 + c, h, masked, col0)
            cm = jnp.max(sT, axis=0, keepdims=True)
            if col0:
                s_buf[h, c, :, 0:col0] = jnp.full((sub, col0), NEG, F32)
                cm = jnp.concatenate([jnp.full((1, col0), NEG, F32), cm], axis=1)
            s_buf[h, c, :, col0:] = sT
            tile_max = cm if tile_max is None else jnp.maximum(tile_max, cm)
        m_buf[h] = tile_max

    def consume(buf, sb0, carry, h):
        s_buf, m_buf = buf
        tiles = [lambda c=c: s_buf[h, c] for c in range(nsub)]
        vts = [jnp.concatenate([vt_ref[(sb0 + c) * nvt + v, HEAD_DIM * h:HEAD_DIM * (h + 1), :]
                                for v in range(nvt)], axis=1) for c in range(nsub)]
        return _softmax_step(tiles, vts, carry, tile_max=m_buf[h])

    def stage(cur, cur_sb, nxt, nxt_sb, carries):
        if nxt is not None:
            for h in range(hp):
                produce(nxt, nxt_sb, False, h)
        return tuple(consume(cur, cur_sb, carries[h], h) for h in range(hp))

    n = pl.program_id(2)
    diag_sb = qs // sub
    buf_a, buf_b = (sa_ref, ma_ref), (sb_ref, mb_ref)
    for h in range(hp):
        produce(buf_a, diag_sb, True, h)

    def pair(j, carries):
        carries = stage(buf_a, jnp.where(j == 0, diag_sb, (2 * j - 1) * nsub),
                        buf_b, 2 * j * nsub, carries)
        return stage(buf_b, 2 * j * nsub,
                     buf_a, jnp.minimum(2 * j + 1, n - 1) * nsub, carries)

    def online(_):
        carries = tuple(_softmax_init(bq) for _ in range(hp))
        carries = lax.fori_loop(0, (n + 1) // 2, pair, carries)
        carries = lax.cond(
            n % 2 == 0,
            lambda c: stage(buf_a, jnp.where(n == 0, diag_sb, (n - 1) * nsub), None, None, c),
            lambda c: c, carries)
        return jnp.concatenate([_softmax_out(c) for c in carries], axis=0)

    refs_ = []
    gap = None
    for h in range(hp):
        qf = qt_ref[h, 0:LANES, :].astype(F32)
        bound = jnp.sqrt(jnp.sum(qf * qf, axis=0, keepdims=True)) * kn_ref[h] * 1.01 + 1e-3
        refs_.append(bound)
        g = jnp.max(bound - ma_ref[h])
        gap = g if gap is None else jnp.maximum(gap, g)

    def weigh_add(acc, h, s, sb0, nblk, col0=0):
        p = jnp.exp2(s - refs_[h][:, col0:]).astype(BF16)
        vt = jnp.concatenate([vt_ref[sb0 * nvt + v, HEAD_DIM * h:HEAD_DIM * (h + 1), :]
                              for v in range(nblk * nvt)], axis=1)
        vt1 = jnp.concatenate([vt, jnp.ones((ONES_ROWS, nblk * sub), BF16)], axis=0)
        d = jnp.dot(vt1, p, preferred_element_type=F32)
        if col0:
            d = jnp.concatenate([jnp.zeros((d.shape[0], col0), F32), d], axis=1)
        return acc + d

    def fixed_reference(_):
        def run(kb, accs, nblk):
            ks = pl.multiple_of(kb * sub, sub)
            ss = []
            for h in range(hp):
                k = k_ref[pl.ds(ks, nblk * sub), kcols[h]]
                if mode == "fox":
                    k = jnp.concatenate([k, kaug_ref[pl.ds(ks, nblk * sub), :]], axis=1)
                ss.append(jnp.dot(k, qt_ref[h], preferred_element_type=F32))
            return tuple(weigh_add(accs[h], h, ss[h], kb, nblk) for h in range(hp))

        accs = tuple(jnp.zeros((HEAD_DIM + ONES_ROWS, bq), F32) for _ in range(hp))
        done = 0
        for width in FIXED_WIDTHS:
            trips = (diag_sb - done) // width
            accs = lax.fori_loop(0, trips,
                                 lambda j, a, done=done, width=width: run(done + j * width, a, width),
                                 accs)
            done = done + trips * width
        for c in range(nsub):
            accs = [weigh_add(accs[h], h, sa_ref[h, c, :, c * sub:], diag_sb + c, 1, c * sub)
                    for h in range(hp)]
        return jnp.concatenate([a[0:HEAD_DIM] / a[HEAD_DIM:HEAD_DIM + 1] for a in accs], axis=0)

    oT = lax.cond(gap <= SAFE_GAP, fixed_reference, online, None)
    o_ref[...] = oT.T.astype(o_ref.dtype)


def _flash(mode, q_arr, q_col0, k_arr, k_col0, vt_arr, vt_row0, extra, n_heads,
           hp=FLASH_HP, bq=FLASH_BQ):
    B, S, _ = q_arr.shape
    qw = (HEAD_DIM if mode == "fox" else LANES) * hp
    vw = HEAD_DIM * hp
    in_specs = [
        pl.BlockSpec((None, bq, qw), lambda b, g, i: (b, i, q_col0 // qw + g)),
        pl.BlockSpec((None, S, qw), lambda b, g, i: (b, 0, k_col0 // qw + g)),
        pl.BlockSpec((None, S // VT_BLK, vw, VT_BLK),
                     lambda b, g, i: (b, 0, vt_row0 // vw + g, 0)),
    ]
    args = [q_arr, k_arr, vt_arr]
    if mode == "fox":
        q_aug, k_aug = extra
        in_specs += [
            pl.BlockSpec((None, bq, LANES), lambda b, g, i: (b, i, 0)),
            pl.BlockSpec((None, S, LANES), lambda b, g, i: (b, 0, 0)),
        ]
        args += [q_aug, k_aug]
    return pl.pallas_call(
        functools.partial(_flash_kernel, mode=mode, hp=hp, bq=bq),
        grid=(B, n_heads // hp, S // bq),
        in_specs=in_specs,
        out_specs=pl.BlockSpec((None, bq, vw), lambda b, g, i: (b, i, g)),
        out_shape=jax.ShapeDtypeStruct((B, S, n_heads * HEAD_DIM), BF16),
        scratch_shapes=([pltpu.VMEM((hp, bq // FLASH_SUB, FLASH_SUB, bq), F32)] * 2
                        + [pltpu.VMEM((hp, 1, bq), F32)] * 2
                        + [pltpu.VMEM((hp, 2 * LANES if mode == "fox" else LANES, bq), BF16),
                           pltpu.VMEM((hp, 1, bq), F32)]),
        compiler_params=_cparams(("arbitrary", "arbitrary", "arbitrary")),
        name="flash_" + mode,
    )(*args)


CK_B = 2 * CHUNK
CK_NW = N_LEFT_CHUNKS * CHUNK // CK_B + 1
CK_EXT = (CK_NW + 1) * CK_B


def _chunk_kernel(q_ref, k_ref, vt_ref, ext_ref, o_ref, tab_ref, s_ref, *, hp, nq):
    i = pl.program_id(1)

    @pl.when(i == 0)
    def _():
        jj = lax.broadcasted_iota(jnp.int32, (CK_B, CK_B), 0)
        rr = lax.broadcasted_iota(jnp.int32, (CK_B, CK_B), 1)
        for h in range(hp):
            for w in range(CK_NW):
                a = (CK_NW - 1 - w) * CK_B
                g = jnp.broadcast_to(ext_ref[h:h + 1, a:a + 2 * CK_B], (CK_B, 2 * CK_B))
                t = pltpu.roll(g, CK_B, 1, stride=1, stride_axis=0)[:, :CK_B]
                if w == 0:
                    t = jnp.where((rr >= CHUNK) & (jj < CHUNK), NEG, t)
                if w == CK_NW - 1:
                    t = jnp.where((rr < CHUNK) & (jj >= CHUNK), NEG, t)
                tab_ref[h, w * CK_B:(w + 1) * CK_B, :] = t

    kcols = [slice(LANES * (h // 2), LANES * (h // 2 + 1)) for h in range(hp)]
    firsts = [i * nq + u - (CK_NW - 1) for u in range(nq)]

    def finish():
        for u in range(nq):
            kbc = [jnp.maximum(firsts[u] + w, 0) for w in range(CK_NW)]
            outs = []
            for h in range(hp):
                vt = jnp.concatenate([vt_ref[kbc[w], HEAD_DIM * h:HEAD_DIM * (h + 1), :]
                                      for w in range(CK_NW)], axis=1)
                outs.append(_softmax_out(_softmax_step([lambda u=u, h=h: s_ref[u, h]], [vt],
                                                       _softmax_init(CK_B))))
            oT = jnp.concatenate(outs, axis=0)
            o_ref[u * CK_B:(u + 1) * CK_B, :] = oT.T.astype(o_ref.dtype)

    def pair_scores(u, p, ks, nrows):
        q2 = q_ref[u * CK_B:(u + 1) * CK_B, kcols[2 * p]]
        qq = jnp.concatenate([_pair_mask_q(q2, 0), _pair_mask_q(q2, 1)], axis=0)
        return lax.dot_general(k_ref[pl.ds(ks, nrows), kcols[2 * p]], qq, _NT,
                               preferred_element_type=F32)

    @pl.when(firsts[0] >= 0)
    def _():
        for u in range(nq):
            ks = pl.multiple_of(firsts[u] * CK_B, CK_B)
            for p in range(hp // 2):
                sT = pair_scores(u, p, ks, CK_NW * CK_B)
                for j in range(2):
                    s_ref[u, 2 * p + j] = sT[:, j * CK_B:(j + 1) * CK_B] + tab_ref[2 * p + j]
        finish()

    @pl.when(firsts[0] < 0)
    def _():
        for u in range(nq):
            for p in range(hp // 2):
                for w in range(CK_NW):
                    rows = slice(w * CK_B, (w + 1) * CK_B)
                    ks = pl.multiple_of(jnp.maximum(firsts[u] + w, 0) * CK_B, CK_B)
                    sT = pair_scores(u, p, ks, CK_B)
                    for j in range(2):
                        s_ref[u, 2 * p + j, rows, :] = jnp.where(
                            firsts[u] + w >= 0,
                            sT[:, j * CK_B:(j + 1) * CK_B] + tab_ref[2 * p + j, rows, :], NEG)
        finish()


def _chunk_attn(main, q_col0, k_col0, vt_arr, vt_row0, ext, n_heads, nq=4):
    B, S, _ = main.shape
    hp = n_heads
    qw, vw = HEAD_DIM * hp, HEAD_DIM * hp
    return pl.pallas_call(
        functools.partial(_chunk_kernel, hp=hp, nq=nq),
        grid=(B, S // (nq * CK_B)),
        in_specs=[
            pl.BlockSpec((None, nq * CK_B, qw), lambda b, i: (b, i, q_col0 // qw)),
            pl.BlockSpec((None, S, qw), lambda b, i: (b, 0, k_col0 // qw)),
            pl.BlockSpec((None, S // VT_BLK, vw, VT_BLK), lambda b, i: (b, 0, vt_row0 // vw, 0)),
            pl.BlockSpec((hp, CK_EXT), lambda b, i: (0, 0)),
        ],
        out_specs=pl.BlockSpec((None, nq * CK_B, vw), lambda b, i: (b, i, 0)),
        out_shape=jax.ShapeDtypeStruct((B, S, n_heads * HEAD_DIM), BF16),
        scratch_shapes=[pltpu.VMEM((hp, CK_NW * CK_B, CK_B), F32),
                        pltpu.VMEM((nq, hp, CK_NW * CK_B, CK_B), F32)],
        compiler_params=_cparams(("arbitrary", "arbitrary")),
        name="chunk_attn",
    )(main, main, vt_arr, ext)


def _sb_kernel(q_ref, k_ref, vt_ref, o_ref, z_ref, lb_ref, sfx_ref, *, hp, bq):
    qs = pl.program_id(1) * bq
    nsub = bq // SUB
    row = lax.broadcasted_iota(jnp.int32, (SUB, bq), 0)
    col = lax.broadcasted_iota(jnp.int32, (SUB, bq), 1)
    ur = lax.broadcasted_iota(jnp.int32, (SUB, 2 * SUB), 0)
    uc = lax.broadcasted_iota(jnp.int32, (SUB, 2 * SUB), 1) & (SUB - 1)
    upper2 = jnp.where(uc > ur, 1.0, 0.0).astype(BF16)
    kcols = [slice(LANES * (h // 2), LANES * (h // 2 + 1)) for h in range(hp)]
    qms = [_pair_mask_q(q_ref[:, kcols[h]], h % 2) for h in range(hp)]

    def step(kb, carries, masked, live=None):
        ks = pl.multiple_of(kb * bq, bq)
        for h in range(hp):
            z_ref[h] = lax.dot_general(k_ref[pl.ds(ks, bq), kcols[h]], qms[h], _NT,
                                       preferred_element_type=F32)
        first_col = [c * SUB if masked else 0 for c in range(nsub)]

        def widen(x, c):
            if first_col[c] == 0:
                return x
            return jnp.concatenate([jnp.zeros((x.shape[0], first_col[c]), x.dtype), x], axis=1)

        totals = []
        for h in range(hp):
            tot = []
            for c in range(nsub):
                rows, cols = slice(c * SUB, (c + 1) * SUB), slice(first_col[c], bq)
                z = z_ref[h, rows, cols]
                l1 = jnp.log(1.0 + jnp.exp2(jnp.abs(z) * (-LOG2E)))
                log_beta = jnp.minimum(z, 0.0) - l1
                log_keep = log_beta - z
                if masked:
                    valid = (ks + c * SUB + row < qs + col)[:, cols]
                    log_keep = jnp.where(valid, log_keep, 0.0)
                lb_ref[h, rows, cols] = log_beta
                hi = log_keep.astype(BF16)
                lo = (log_keep - hi.astype(F32)).astype(BF16)
                sfx = jnp.dot(upper2, jnp.concatenate([hi, lo], axis=0),
                              preferred_element_type=F32)
                sfx_ref[h, rows, cols] = sfx
                tot.append(widen(sfx[0:1, :] + log_keep[0:1, :], c))
            totals.append(tot)
        out = []
        for h in range(hp):
            tail, acc = carries[h]
            if live is not None:
                tail = jnp.where(live, tail, NEG)
            parts = [None] * nsub
            for c in range(nsub - 1, -1, -1):
                rows, cols = slice(c * SUB, (c + 1) * SUB), slice(first_col[c], bq)
                a = jnp.exp(lb_ref[h, rows, cols] + sfx_ref[h, rows, cols] + tail[:, cols])
                if masked:
                    a = jnp.where((ks + c * SUB + row < qs + col)[:, cols], a, 0.0)
                parts[c] = a.astype(BF16)
                tail = tail + totals[h][c]
            vts = [vt_ref[kb * nsub + c, HEAD_DIM * h:HEAD_DIM * (h + 1), :] for c in range(nsub)]
            if masked:
                for c in range(nsub):
                    acc = acc + widen(jnp.dot(vts[c], parts[c], preferred_element_type=F32), c)
            else:
                acc = acc + jnp.dot(jnp.concatenate(vts, axis=1), jnp.concatenate(parts, axis=0),
                                    preferred_element_type=F32)
            out.append((tail, acc))
        return tuple(out)

    n_full = qs // bq
    carries = tuple((jnp.zeros((1, bq), F32), jnp.zeros((HEAD_DIM, bq), F32))
                    for _ in range(hp))
    carries = step(n_full, carries, True)
    carries = step(jnp.maximum(n_full - 1, 0), carries, False, live=n_full >= 1)

    def cond(state):
        kb, carries = state
        tail_max = carries[0][0]
        for h in range(1, hp):
            tail_max = jnp.maximum(tail_max, carries[h][0])
        return (kb >= 0) & (jnp.max(tail_max) > SB_ZERO_LOG)

    def body(state):
        kb, carries = state
        return kb - 1, step(kb, carries, False)

    _, carries = lax.while_loop(cond, body, (n_full - 2, carries))
    _store_heads(o_ref, [acc for (_, acc) in carries])


def _sb_attn(main, q_col0, k_col0, vt_arr, vt_row0, n_heads, bq=256):
    B, S, _ = main.shape
    hp = n_heads
    qw = HEAD_DIM * hp
    return pl.pallas_call(
        functools.partial(_sb_kernel, hp=hp, bq=bq),
        grid=(B, S // bq),
        in_specs=[
            pl.BlockSpec((None, bq, qw), lambda b, i: (b, i, q_col0 // qw)),
            pl.BlockSpec((None, S, qw), lambda b, i: (b, 0, k_col0 // qw)),
            pl.BlockSpec((None, S // VT_BLK, qw, VT_BLK), lambda b, i: (b, 0, vt_row0 // qw, 0)),
        ],
        out_specs=pl.BlockSpec((None, bq, qw), lambda b, i: (b, i, 0)),
        out_shape=jax.ShapeDtypeStruct((B, S, n_heads * HEAD_DIM), BF16),
        scratch_shapes=[pltpu.VMEM((hp, bq, bq), F32)] * 3,
        compiler_params=_cparams(("arbitrary", "arbitrary")),
        name="sb_attn",
    )(main, main, vt_arr)


def _mla_prep_kernel(aux_ref, pos_ref, invf_ref, qn_ref, kvn_ref, wuq_ref, wuqr_ref,
                     wk_ref, wvt_ref, oq_ref, ok_ref, ovt_ref, *, q_scale):
    ang = pos_ref[...] * invf_ref[...]
    cos4, sin4 = jnp.cos(ang), jnp.sin(ang)
    lane = lax.broadcasted_iota(jnp.int32, ang.shape, 1)
    rotary = (lane >= NOPE_DIM) & (lane < NOPE_DIM + ROPE_DIM)
    cos_rows, sin_rows = [], []
    for m in range(LANES // ROPE_DIM):
        shift = (NOPE_DIM - ROPE_DIM * m) % LANES
        cm = cos4 if shift == 0 else pltpu.roll(cos4, shift, 1)
        sm = sin4 if shift == 0 else pltpu.roll(sin4, shift, 1)
        cos_rows.append(jnp.where(rotary, cm, 1.0))
        sin_rows.append(jnp.where(rotary, sm, 0.0))
    cos = jnp.concatenate(cos_rows, axis=0)
    sin = jnp.concatenate(sin_rows, axis=0)
    cq = _rms_bf16(aux_ref[:, 0:Q_LORA], qn_ref[...])
    ckv = _rms_bf16(aux_ref[:, Q_LORA:Q_LORA + KV_LORA], kvn_ref[...])
    o = Q_LORA + KV_LORA
    slab = aux_ref[:, o:o + LANES]
    klane = lax.broadcasted_iota(jnp.int32, slab.shape, 1)
    k_rope = jnp.where((klane >= NOPE_DIM) & (klane < NOPE_DIM + ROPE_DIM),
                       slab * cos + pltpu.roll(slab, LANES - ROPE_DIM, 1) * sin, 0.0)
    cos_q, sin_q = cos * q_scale, sin * q_scale
    n_heads = oq_ref.shape[-1] // LANES
    qr = jnp.dot(cq, wuqr_ref[...], preferred_element_type=F32)
    per_group = LANES // ROPE_DIM
    for h in range(0, n_heads, 2):
        cols = slice(h * LANES, (h + 2) * LANES)
        qa = jnp.dot(cq, wuq_ref[:, cols], preferred_element_type=F32)
        kn = jnp.dot(ckv, wk_ref[:, cols], preferred_element_type=F32)
        for d in range(2):
            c1 = slice(d * LANES, (d + 1) * LANES)
            c2 = slice((h + d) * LANES, (h + d + 1) * LANES)
            grp, m = divmod(h + d, per_group)
            qb = qr[:, grp * LANES:(grp + 1) * LANES]
            shift = (NOPE_DIM - ROPE_DIM * m) % LANES
            if shift:
                qb = pltpu.roll(qb, shift, 1)
            oq_ref[:, c2] = (qa[:, c1] * cos_q + qb * sin_q).astype(BF16)
            ok_ref[:, c2] = (kn[:, c1] + k_rope).astype(BF16)
    nv = wvt_ref.shape[0]
    for r in range(0, nv, 256):
        vt = lax.dot_general(wvt_ref[r:r + 256, :], ckv, _NT,
                             preferred_element_type=F32).astype(BF16)
        _store_vt(ovt_ref, vt, r)


MLA_TM = 2048


def _mla_prep(aux, pos, invf, qn, kvn, wuq, wuqr, wk, wvt, q_scale, tm=MLA_TM):
    B, S, na = aux.shape
    nq, nv = wuq.shape[1], wvt.shape[0]
    full = lambda a: pl.BlockSpec(a.shape, lambda b, i: (0,) * a.ndim)
    return pl.pallas_call(
        functools.partial(_mla_prep_kernel, q_scale=q_scale),
        grid=(B, S // tm),
        in_specs=[
            pl.BlockSpec((None, tm, na), lambda b, i: (b, i, 0)),
            pl.BlockSpec((None, None) + pos.shape[2:], lambda b, i: (b, i, 0, 0)),
            full(invf), full(qn), full(kvn), full(wuq), full(wuqr), full(wk), full(wvt),
        ],
        out_specs=[
            pl.BlockSpec((None, tm, nq), lambda b, i: (b, i, 0)),
            pl.BlockSpec((None, tm, nq), lambda b, i: (b, i, 0)),
            pl.BlockSpec((None, tm // VT_BLK, nv, VT_BLK), lambda b, i: (b, i, 0, 0)),
        ],
        out_shape=[
            jax.ShapeDtypeStruct((B, S, nq), BF16),
            jax.ShapeDtypeStruct((B, S, nq), BF16),
            jax.ShapeDtypeStruct((B, S // VT_BLK, nv, VT_BLK), BF16),
        ],
        compiler_params=_cparams(("arbitrary", "arbitrary")),
        name="mla_prep",
    )(aux, pos, invf, qn, kvn, wuq, wuqr, wk, wvt)


def _mlp_kernel(x_ref, oa_ref, ob_ref, wo_ref, g_ref, wu_ref, wd_ref, gf_ref, out_ref,
                x1_ref, h_ref, a_ref, *, final_norm, tf):
    na = oa_ref.shape[-1]
    x1 = (x_ref[...]
          + jnp.dot(oa_ref[...], wo_ref[0:na, :], preferred_element_type=F32)
          + jnp.dot(ob_ref[...], wo_ref[na:, :], preferred_element_type=F32))
    x1_ref[...] = x1
    h_ref[...] = _rms_bf16(x1, g_ref[...])
    for f in range(0, wu_ref.shape[1], tf):
        u = jnp.dot(h_ref[...], wu_ref[:, f:f + tf], preferred_element_type=F32)
        a_ref[:, f:f + tf] = jnp.square(jnp.maximum(u, 0.0)).astype(BF16)
    y = x1_ref[...] + jnp.dot(a_ref[...], wd_ref[...], preferred_element_type=F32)
    if final_norm:
        ms = jnp.mean(y * y, axis=-1, keepdims=True)
        y = y * lax.rsqrt(ms + EPS) * gf_ref[...]
    out_ref[...] = y


def _mlp(x2, oa, ob, wo, g, wu, wd, gf, final_norm, tm=1024, tf=1024):
    T, D = x2.shape
    F = wu.shape[1]
    na, nb = oa.shape[1], ob.shape[1]
    const = lambda shape: pl.BlockSpec(shape, lambda i: (0, 0), pipeline_mode=pl.Buffered(1))
    return pl.pallas_call(
        functools.partial(_mlp_kernel, final_norm=final_norm, tf=tf),
        grid=(T // tm,),
        in_specs=[
            pl.BlockSpec((tm, D), lambda i: (i, 0)),
            pl.BlockSpec((tm, na), lambda i: (i, 0)),
            pl.BlockSpec((tm, nb), lambda i: (i, 0)),
            const((na + nb, D)),
            const((1, D)),
            const((D, F)),
            const((F, D)),
            const((1, D)),
        ],
        out_specs=pl.BlockSpec((tm, D), lambda i: (i, 0)),
        out_shape=jax.ShapeDtypeStruct((T, D), F32),
        scratch_shapes=[pltpu.VMEM((tm, D), F32), pltpu.VMEM((tm, D), BF16),
                        pltpu.VMEM((tm, F), BF16)],
        compiler_params=_cparams(("arbitrary",)),
        name="mlp",
    )(x2, oa, ob, wo, g.reshape(1, D), wu, wd, gf.reshape(1, D))


def _pad_cols(w, n):
    return jnp.pad(w, ((0, 0), (0, n - w.shape[1])))


def _rot_cols(w):
    half = ROPE_DIM // 2
    return jnp.concatenate([-w[:, half:], w[:, :half]], axis=1)


def _even_layer(x, g_mix, w_in, b_forget, rel_bias, w_out, g_mlp, w_up, w_down, g_final,
                final_norm):
    B, S, D = x.shape
    hf, hc = b_forget.shape[0], rel_bias.shape[0]
    wf, wc = hf * HEAD_DIM, hc * HEAD_DIM
    o = np.cumsum([0, wf, wf, wf, hf, wc, wc, wc])
    qa, ka, va, fa, qb, kb, vb = [w_in[:, o[n]:o[n + 1]] for n in range(7)]
    q_scale = HEAD_DIM ** -0.5 * LOG2E
    wm = jnp.concatenate([qa * q_scale, ka, qb * q_scale, kb], axis=1).astype(BF16)
    wvt = jnp.concatenate([va, vb], axis=1).T.astype(BF16)
    wa = _pad_cols(fa, LANES).astype(BF16)
    main, vt, aux = _inproj(x, g_mix, wm, wvt, wa)

    q_aug, k_aug = _logcum(aux, _pad_cols(b_forget.reshape(1, hf), LANES), hf)
    o_a = _flash("fox", main, 0, main, wf, vt, 0, (q_aug, k_aug), hf)

    assert rel_bias.shape[1] == CHUNK + REL_CLIP
    right = CK_EXT - rel_bias.shape[1] - (CHUNK + 1)
    ext = jnp.pad(rel_bias * LOG2E, ((0, 0), (CHUNK + 1, right)), mode="edge")
    o_b = _chunk_attn(main, 2 * wf, 2 * wf + wc, vt, wf, ext, hc)

    y = _mlp(x.reshape(B * S, D), o_a.reshape(B * S, wf), o_b.reshape(B * S, wc),
             w_out.astype(BF16), g_mlp, w_up.astype(BF16), w_down.astype(BF16), g_final,
             final_norm)
    return y.reshape(B, S, D)


def _odd_layer(x, positions, g_mix, w_in, q_norm, kv_norm, w_uq, w_ukv, w_out, g_mlp, w_up,
               w_down, g_final, final_norm):
    B, S, D = x.shape
    hm = w_ukv.shape[1] // (NOPE_DIM + HEAD_DIM)
    ws = w_in.shape[1] - Q_LORA - KV_LORA - ROPE_DIM
    hs = (ws // 3) // HEAD_DIM
    wsb = hs * HEAD_DIM
    o = np.cumsum([0, wsb, wsb, wsb, Q_LORA, KV_LORA, ROPE_DIM])
    qc, kc, vc, w_cq, w_ckv, w_kr = [w_in[:, o[n]:o[n + 1]] for n in range(6)]
    wm = jnp.concatenate([qc * HEAD_DIM ** -0.5, kc], axis=1).astype(BF16)
    wa = jnp.concatenate([w_cq, w_ckv, jnp.zeros((D, NOPE_DIM), w_kr.dtype), w_kr,
                          _rot_cols(w_kr)], axis=1).astype(BF16)
    main, vt, aux = _inproj(x, g_mix, wm, vc.T.astype(BF16), wa)
    o_c = _sb_attn(main, 0, wsb, vt, 0, hs)

    dq = NOPE_DIM + ROPE_DIM
    wuq3 = w_uq.reshape(Q_LORA, hm, dq)
    nope, ropew = wuq3[:, :, :NOPE_DIM], wuq3[:, :, NOPE_DIM:]
    zq = jnp.zeros((Q_LORA, hm, LANES - dq), w_uq.dtype)
    wuq = jnp.concatenate([nope, ropew, zq], axis=2).reshape(Q_LORA, hm * LANES).astype(BF16)
    half = ROPE_DIM // 2
    ropr = jnp.concatenate([-ropew[:, :, half:], ropew[:, :, :half]], axis=2)
    wuqr = ropr.reshape(Q_LORA, hm * ROPE_DIM).astype(BF16)
    wkv3 = w_ukv.reshape(KV_LORA, hm, NOPE_DIM + HEAD_DIM)
    wk = jnp.concatenate([wkv3[:, :, :NOPE_DIM],
                          jnp.zeros((KV_LORA, hm, LANES - NOPE_DIM), w_ukv.dtype)], axis=2)
    wk = wk.reshape(KV_LORA, hm * LANES).astype(BF16)
    wv_t = wkv3[:, :, NOPE_DIM:].reshape(KV_LORA, hm * HEAD_DIM).T.astype(BF16)
    freqs = (ROPE_THETA ** (-jnp.arange(half, dtype=F32) / half))
    invf = jnp.tile(freqs, 2 * LANES // ROPE_DIM).reshape(1, LANES)
    groups = LANES // ROPE_DIM
    pos = positions.astype(F32).reshape(B, S // MLA_TM, groups, MLA_TM // groups)
    pos = jnp.repeat(jnp.swapaxes(pos, 2, 3), ROPE_DIM, axis=-1)
    qm, km, vtm = _mla_prep(aux, pos, invf, q_norm.reshape(1, Q_LORA),
                            kv_norm.reshape(1, KV_LORA), wuq, wuqr, wk, wv_t,
                            dq ** -0.5 * LOG2E)
    o_d = _flash("mla", qm, 0, km, 0, vtm, 0, None, hm)

    y = _mlp(x.reshape(B * S, D), o_c.reshape(B * S, wsb), o_d.reshape(B * S, hm * HEAD_DIM),
             w_out.astype(BF16), g_mlp, w_up.astype(BF16), w_down.astype(BF16), g_final,
             final_norm)
    return y.reshape(B, S, D)


def kernel(x, positions, norm_mix, norm_mlp, norm_final, w_in_ab, b_forget, rel_bias, w_out_ab,
           w_in_cd, q_norm, kv_norm, w_uq, w_ukv, w_out_cd, w_up, w_down):
    depth = norm_mix.shape[0]
    for layer in range(depth):
        last = layer == depth - 1
        if layer % 2 == 0:
            e = layer // 2
            x = _even_layer(x, norm_mix[layer], w_in_ab[e], b_forget[e], rel_bias[e], w_out_ab[e],
                            norm_mlp[layer], w_up[layer], w_down[layer], norm_final, last)
        else:
            o = layer // 2
            x = _odd_layer(x, positions, norm_mix[layer], w_in_cd[o], q_norm[o], kv_norm[o],
                           w_uq[o], w_ukv[o], w_out_cd[o], norm_mlp[layer], w_up[layer],
                           w_down[layer], norm_final, last)
    return x
```
